```python
import jax, jax.numpy as jnp
from jax import lax
import numpy as np

D_MODEL = 2048
BATCH = 8
SEQ = 4096
DEPTH = 1

CHUNK = 64
N_MEM = 256
EPS = 1e-6
D_MIX = D_MODEL
D_POOL = D_MIX // 2
POOL_WINDOWS = (2, 4, 8, 16)
N_POOL_GROUPS = len(POOL_WINDOWS)
POOL_GROUP_DIM = D_POOL // N_POOL_GROUPS
D_SGU = D_MIX - D_POOL
SGU_BLOCK = 128
N_SGU_HEADS = 8
SGU_HEAD_DIM = D_SGU // N_SGU_HEADS
D_IN = D_POOL + 2 * D_SGU
N_XATTN_HEADS = 4
XATTN_HEAD_DIM = D_MODEL // N_XATTN_HEADS
D_FF = ((8 * D_MODEL // 3 + 255) // 256) * 256

kernel_name = "hybrid_pool_sgu_memxattn_block"


def rmsnorm(x, g):
    x32 = x.astype(jnp.float32)
    y = x32 * lax.rsqrt(jnp.mean(x32 * x32, axis=-1, keepdims=True) + EPS)
    return (y * g.astype(jnp.float32)).astype(x.dtype)


def multiscale_pool(a, pool_w, pool_scale):
    B, S, _ = a.shape
    a32 = a.astype(jnp.float32)
    csum = jnp.cumsum(a32, axis=1)
    pos = jnp.arange(1, S + 1, dtype=jnp.float32)[None, :, None]
    outs = []
    for g, w in enumerate(POOL_WINDOWS):
        sl = slice(g * POOL_GROUP_DIM, (g + 1) * POOL_GROUP_DIM)
        c = csum[..., sl]
        prev = jnp.pad(c, ((0, 0), (w, 0), (0, 0)))[:, :S]
        mean = (c - prev) / jnp.minimum(pos, float(w))
        outs.append(mean - a32[..., sl])
    p = jnp.stack(outs, axis=2).astype(a.dtype)
    y = jnp.einsum('bsgc,gcd->bsgd', p, pool_w)
    return y.reshape(B, S, D_POOL) * pool_scale


def spatial_gating(uv, sgu_norm_g, w_spatial, b_spatial):
    B, S, _ = uv.shape
    u, v = uv[..., :D_SGU], uv[..., D_SGU:]
    v = rmsnorm(v, sgu_norm_g)
    v = v.reshape(B, S // SGU_BLOCK, SGU_BLOCK, N_SGU_HEADS, SGU_HEAD_DIM)
    t = jnp.arange(SGU_BLOCK)
    mask = (t[None, :] // CHUNK) <= (t[:, None] // CHUNK)
    ws = jnp.where(mask[None], w_spatial, 0.0)
    mixed = jnp.einsum('hts,bnshc->bnthc', ws, v)
    mixed = mixed + b_spatial.T[None, None, :, :, None]
    return u * mixed.reshape(B, S, D_SGU)


def memory_cross_attention(h, m, w_q, w_k, w_v, w_o):
    B, S, _ = h.shape
    M = m.shape[1]
    q = (h @ w_q).reshape(B, S, N_XATTN_HEADS, XATTN_HEAD_DIM)
    k = (m @ w_k).reshape(B, M, N_XATTN_HEADS, XATTN_HEAD_DIM)
    v = (m @ w_v).reshape(B, M, N_XATTN_HEADS, XATTN_HEAD_DIM)
    s = jnp.einsum('bshd,bmhd->bhsm', q, k).astype(jnp.float32) * (XATTN_HEAD_DIM ** -0.5)
    p = jax.nn.softmax(s, axis=-1).astype(v.dtype)
    o = jnp.einsum('bhsm,bmhd->bshd', p, v).reshape(B, S, D_MODEL)
    return o @ w_o


def swiglu(h, w_gate, w_up, w_down):
    return (jax.nn.silu(h @ w_gate) * (h @ w_up)) @ w_down


def _fwd_setup_inputs(seed: int = 0) -> dict:
    key = jax.random.key(seed)
    ks = jax.random.split(key, 24)
    L = DEPTH
    f32 = jnp.float32

    def nrm(k, shape, scale):
        return jax.random.normal(k, shape, f32) * scale

    def gain(k, shape):
        return 1.0 + 0.02 * jax.random.normal(k, shape, f32)

    return {
        "x": jax.random.normal(ks[0], (BATCH, SEQ, D_MODEL), f32),
        "mem": jax.random.normal(ks[1], (BATCH, N_MEM, D_MODEL), f32),
        "norm_mix_g": gain(ks[2], (L, D_MODEL)),
        "w_in": nrm(ks[3], (L, D_MODEL, D_IN), D_MODEL ** -0.5),
        "pool_w": nrm(ks[4], (L, N_POOL_GROUPS, POOL_GROUP_DIM, POOL_GROUP_DIM), POOL_GROUP_DIM ** -0.5),
        "pool_scale": 1.0 + 0.1 * jax.random.normal(ks[5], (L, D_POOL), f32),
        "sgu_norm_g": gain(ks[6], (L, D_SGU)),
        "w_spatial": nrm(ks[7], (L, N_SGU_HEADS, SGU_BLOCK, SGU_BLOCK), SGU_BLOCK ** -0.5),
        "b_spatial": 1.0 + 0.1 * jax.random.normal(ks[8], (L, N_SGU_HEADS, SGU_BLOCK), f32),
        "w_out": nrm(ks[9], (L, D_MIX, D_MODEL), D_MIX ** -0.5),
        "norm_xattn_g": gain(ks[10], (L, D_MODEL)),
        "norm_mem_g": gain(ks[11], (L, D_MODEL)),
        "w_q": nrm(ks[12], (L, D_MODEL, D_MODEL), D_MODEL ** -0.5),
        "w_k": nrm(ks[13], (L, D_MODEL, D_MODEL), D_MODEL ** -0.5),
        "w_v": nrm(ks[14], (L, D_MODEL, D_MODEL), D_MODEL ** -0.5),
        "w_o": nrm(ks[15], (L, D_MODEL, D_MODEL), D_MODEL ** -0.5),
        "norm_ffn_g": gain(ks[16], (L, D_MODEL)),
        "w_gate": nrm(ks[17], (L, D_MODEL, D_FF), D_MODEL ** -0.5),
        "w_up": nrm(ks[18], (L, D_MODEL, D_FF), D_MODEL ** -0.5),
        "w_down": nrm(ks[19], (L, D_FF, D_MODEL), D_FF ** -0.5),
        "final_norm_g": gain(ks[20], (D_MODEL,)),
    }


def _fwd_reference(x, mem, norm_mix_g, w_in, pool_w, pool_scale, sgu_norm_g, w_spatial,
              b_spatial, w_out, norm_xattn_g, norm_mem_g, w_q, w_k, w_v, w_o,
              norm_ffn_g, w_gate, w_up, w_down, final_norm_g):
    for l in range(DEPTH):
        h = rmsnorm(x, norm_mix_g[l])
        proj = h @ w_in[l]
        y_pool = multiscale_pool(proj[..., :D_POOL], pool_w[l], pool_scale[l])
        y_sgu = spatial_gating(proj[..., D_POOL:], sgu_norm_g[l], w_spatial[l], b_spatial[l])
        x = x + jnp.concatenate([y_pool, y_sgu], axis=-1) @ w_out[l]
        h = rmsnorm(x, norm_xattn_g[l])
        m = rmsnorm(mem, norm_mem_g[l])
        x = x + memory_cross_attention(h, m, w_q[l], w_k[l], w_v[l], w_o[l])
        h = rmsnorm(x, norm_ffn_g[l])
        x = x + swiglu(h, w_gate[l], w_up[l], w_down[l])
    return rmsnorm(x, final_norm_g)


import jax as _jax
import jax.numpy as _jnp

TWIN_FORMAT = 'train_step'
FWD_PARAMS = ['x', 'mem', 'norm_mix_g', 'w_in', 'pool_w', 'pool_scale', 'sgu_norm_g', 'w_spatial', 'b_spatial', 'w_out', 'norm_xattn_g', 'norm_mem_g', 'w_q', 'w_k', 'w_v', 'w_o', 'norm_ffn_g', 'w_gate', 'w_up', 'w_down', 'final_norm_g']
TWIN_WEIGHTS = ['norm_mix_g', 'w_in', 'pool_w', 'pool_scale', 'sgu_norm_g', 'w_spatial', 'b_spatial', 'w_out', 'norm_xattn_g', 'norm_mem_g', 'w_q', 'w_k', 'w_v', 'w_o', 'norm_ffn_g', 'w_gate', 'w_up', 'w_down', 'final_norm_g']
TWIN_DIFF_INPUT = 'x'
TWIN_INPUTS = ['x', 'mem', 'norm_mix_g', 'w_in', 'pool_w', 'pool_scale', 'sgu_norm_g', 'w_spatial', 'b_spatial', 'w_out', 'norm_xattn_g', 'norm_mem_g', 'w_q', 'w_k', 'w_v', 'w_o', 'norm_ffn_g', 'w_gate', 'w_up', 'w_down', 'final_norm_g', 'loss_target', 'm_norm_mix_g', 'm_w_in', 'm_pool_w', 'm_pool_scale', 'm_sgu_norm_g', 'm_w_spatial', 'm_b_spatial', 'm_w_out', 'm_norm_xattn_g', 'm_norm_mem_g', 'm_w_q', 'm_w_k', 'm_w_v', 'm_w_o', 'm_norm_ffn_g', 'm_w_gate', 'm_w_up', 'm_w_down', 'm_final_norm_g', 'v_norm_mix_g', 'v_w_in', 'v_pool_w', 'v_pool_scale', 'v_sgu_norm_g', 'v_w_spatial', 'v_b_spatial', 'v_w_out', 'v_norm_xattn_g', 'v_norm_mem_g', 'v_w_q', 'v_w_k', 'v_w_v', 'v_w_o', 'v_norm_ffn_g', 'v_w_gate', 'v_w_up', 'v_w_down', 'v_final_norm_g']
TWIN_OUTPUTS = ['loss', 'grad_x', 'grad_norm_mix_g', 'grad_w_in', 'grad_pool_w', 'grad_pool_scale', 'grad_sgu_norm_g', 'grad_w_spatial', 'grad_b_spatial', 'grad_w_out', 'grad_norm_xattn_g', 'grad_norm_mem_g', 'grad_w_q', 'grad_w_k', 'grad_w_v', 'grad_w_o', 'grad_norm_ffn_g', 'grad_w_gate', 'grad_w_up', 'grad_w_down', 'grad_final_norm_g', 'delta_norm_mix_g', 'delta_w_in', 'delta_pool_w', 'delta_pool_scale', 'delta_sgu_norm_g', 'delta_w_spatial', 'delta_b_spatial', 'delta_w_out', 'delta_norm_xattn_g', 'delta_norm_mem_g', 'delta_w_q', 'delta_w_k', 'delta_w_v', 'delta_w_o', 'delta_norm_ffn_g', 'delta_w_gate', 'delta_w_up', 'delta_w_down', 'delta_final_norm_g', 'new_m_norm_mix_g', 'new_m_w_in', 'new_m_pool_w', 'new_m_pool_scale', 'new_m_sgu_norm_g', 'new_m_w_spatial', 'new_m_b_spatial', 'new_m_w_out', 'new_m_norm_xattn_g', 'new_m_norm_mem_g', 'new_m_w_q', 'new_m_w_k', 'new_m_w_v', 'new_m_w_o', 'new_m_norm_ffn_g', 'new_m_w_gate', 'new_m_w_up', 'new_m_w_down', 'new_m_final_norm_g', 'new_v_norm_mix_g', 'new_v_w_in', 'new_v_pool_w', 'new_v_pool_scale', 'new_v_sgu_norm_g', 'new_v_w_spatial', 'new_v_b_spatial', 'new_v_w_out', 'new_v_norm_xattn_g', 'new_v_norm_mem_g', 'new_v_w_q', 'new_v_w_k', 'new_v_w_v', 'new_v_w_o', 'new_v_norm_ffn_g', 'new_v_w_gate', 'new_v_w_up', 'new_v_w_down', 'new_v_final_norm_g']
TWIN_LEAF_KINDS = {'loss': 'loss', 'grad_x': 'grad_x', 'grad_norm_mix_g': 'grad_w', 'grad_w_in': 'grad_w', 'grad_pool_w': 'grad_w', 'grad_pool_scale': 'grad_w', 'grad_sgu_norm_g': 'grad_w', 'grad_w_spatial': 'grad_w', 'grad_b_spatial': 'grad_w', 'grad_w_out': 'grad_w', 'grad_norm_xattn_g': 'grad_w', 'grad_norm_mem_g': 'grad_w', 'grad_w_q': 'grad_w', 'grad_w_k': 'grad_w', 'grad_w_v': 'grad_w', 'grad_w_o': 'grad_w', 'grad_norm_ffn_g': 'grad_w', 'grad_w_gate': 'grad_w', 'grad_w_up': 'grad_w', 'grad_w_down': 'grad_w', 'grad_final_norm_g': 'grad_w', 'delta_norm_mix_g': 'delta_w', 'delta_w_in': 'delta_w', 'delta_pool_w': 'delta_w', 'delta_pool_scale': 'delta_w', 'delta_sgu_norm_g': 'delta_w', 'delta_w_spatial': 'delta_w', 'delta_b_spatial': 'delta_w', 'delta_w_out': 'delta_w', 'delta_norm_xattn_g': 'delta_w', 'delta_norm_mem_g': 'delta_w', 'delta_w_q': 'delta_w', 'delta_w_k': 'delta_w', 'delta_w_v': 'delta_w', 'delta_w_o': 'delta_w', 'delta_norm_ffn_g': 'delta_w', 'delta_w_gate': 'delta_w', 'delta_w_up': 'delta_w', 'delta_w_down': 'delta_w', 'delta_final_norm_g': 'delta_w', 'new_m_norm_mix_g': 'new_m', 'new_m_w_in': 'new_m', 'new_m_pool_w': 'new_m', 'new_m_pool_scale': 'new_m', 'new_m_sgu_norm_g': 'new_m', 'new_m_w_spatial': 'new_m', 'new_m_b_spatial': 'new_m', 'new_m_w_out': 'new_m', 'new_m_norm_xattn_g': 'new_m', 'new_m_norm_mem_g': 'new_m', 'new_m_w_q': 'new_m', 'new_m_w_k': 'new_m', 'new_m_w_v': 'new_m', 'new_m_w_o': 'new_m', 'new_m_norm_ffn_g': 'new_m', 'new_m_w_gate': 'new_m', 'new_m_w_up': 'new_m', 'new_m_w_down': 'new_m', 'new_m_final_norm_g': 'new_m', 'new_v_norm_mix_g': 'new_v', 'new_v_w_in': 'new_v', 'new_v_pool_w': 'new_v', 'new_v_pool_scale': 'new_v', 'new_v_sgu_norm_g': 'new_v', 'new_v_w_spatial': 'new_v', 'new_v_b_spatial': 'new_v', 'new_v_w_out': 'new_v', 'new_v_norm_xattn_g': 'new_v', 'new_v_norm_mem_g': 'new_v', 'new_v_w_q': 'new_v', 'new_v_w_k': 'new_v', 'new_v_w_v': 'new_v', 'new_v_w_o': 'new_v', 'new_v_norm_ffn_g': 'new_v', 'new_v_w_gate': 'new_v', 'new_v_w_up': 'new_v', 'new_v_w_down': 'new_v', 'new_v_final_norm_g': 'new_v'}


def _forward(args):
    return _fwd_reference(*[args[k] for k in FWD_PARAMS])


def _output_shape():
    def fwd():
        inp = _fwd_setup_inputs(0)
        return _fwd_reference(*[inp[k] for k in FWD_PARAMS])
    out = _jax.eval_shape(fwd)
    return out.shape, out.dtype

N_MICROBATCH = 1
ADAM_LR = 0.001
ADAM_B1 = 0.9
ADAM_B2 = 0.999
ADAM_EPS = 1e-08
ADAM_WD = 0.01
ADAM_STEP = 10
PER_EXAMPLE_BATCH_AXIS = {'x': 0, 'mem': 0, 'loss_target': 0}
SHARED_INPUTS = []
_WEIGHT_DTYPES = {'norm_mix_g': _jnp.float32, 'w_in': _jnp.float32, 'pool_w': _jnp.float32, 'pool_scale': _jnp.float32, 'sgu_norm_g': _jnp.float32, 'w_spatial': _jnp.float32, 'b_spatial': _jnp.float32, 'w_out': _jnp.float32, 'norm_xattn_g': _jnp.float32, 'norm_mem_g': _jnp.float32, 'w_q': _jnp.float32, 'w_k': _jnp.float32, 'w_v': _jnp.float32, 'w_o': _jnp.float32, 'norm_ffn_g': _jnp.float32, 'w_gate': _jnp.float32, 'w_up': _jnp.float32, 'w_down': _jnp.float32, 'final_norm_g': _jnp.float32}
MOMENT_SCALE = {'norm_mix_g': 8.146338e-02, 'w_in': 6.585873e-02, 'pool_w': 5.569563e-02, 'pool_scale': 5.549467e-02, 'sgu_norm_g': 5.303249e-02, 'w_spatial': 5.386017e-02, 'b_spatial': 6.208083e-02, 'w_out': 7.112465e-02, 'norm_xattn_g': 6.304012e-03, 'norm_mem_g': 9.596139e-03, 'w_q': 6.486517e-03, 'w_k': 6.494033e-03, 'w_v': 6.559713e-03, 'w_o': 6.558209e-03, 'norm_ffn_g': 4.631332e-02, 'w_gate': 2.036582e-02, 'w_up': 1.967479e-02, 'w_down': 3.265713e-02, 'final_norm_g': 1.600653e+01}


def _to_microbatches(a, axis):
    t = _jnp.moveaxis(a, axis, 0)
    t = t.reshape((N_MICROBATCH, t.shape[0] // N_MICROBATCH) + t.shape[1:])
    return _jnp.moveaxis(t, 1, axis + 1)


def setup_inputs(seed: int = 0) -> dict:
    inp = _fwd_setup_inputs(seed)
    key = _jax.random.fold_in(_jax.random.key(seed), 7919)
    shape, _ = _output_shape()
    out = dict(inp)
    out["loss_target"] = _jax.random.normal(_jax.random.fold_in(key, 0), shape, _jnp.float32)
    for i, name in enumerate(TWIN_WEIGHTS):
        w = inp[name].astype(_jnp.float32)
        if MOMENT_SCALE is None:
            s = _jnp.sqrt(_jnp.mean(_jnp.square(w)) + 1e-30)
        else:
            s = MOMENT_SCALE[name]
        km, kv = _jax.random.split(_jax.random.fold_in(key, i + 1))
        out[name] = w
        out["m_" + name] = s * _jax.random.normal(km, w.shape, _jnp.float32)
        out["v_" + name] = (s * s) * _jax.random.uniform(kv, w.shape, _jnp.float32, 0.5, 1.5)
    if N_MICROBATCH > 1:
        for name, axis in PER_EXAMPLE_BATCH_AXIS.items():
            out[name] = _to_microbatches(out[name], axis)
    return {'x': out['x'], 'mem': out['mem'], 'norm_mix_g': out['norm_mix_g'], 'w_in': out['w_in'], 'pool_w': out['pool_w'], 'pool_scale': out['pool_scale'], 'sgu_norm_g': out['sgu_norm_g'], 'w_spatial': out['w_spatial'], 'b_spatial': out['b_spatial'], 'w_out': out['w_out'], 'norm_xattn_g': out['norm_xattn_g'], 'norm_mem_g': out['norm_mem_g'], 'w_q': out['w_q'], 'w_k': out['w_k'], 'w_v': out['w_v'], 'w_o': out['w_o'], 'norm_ffn_g': out['norm_ffn_g'], 'w_gate': out['w_gate'], 'w_up': out['w_up'], 'w_down': out['w_down'], 'final_norm_g': out['final_norm_g'], 'loss_target': out['loss_target'], 'm_norm_mix_g': out['m_norm_mix_g'], 'm_w_in': out['m_w_in'], 'm_pool_w': out['m_pool_w'], 'm_pool_scale': out['m_pool_scale'], 'm_sgu_norm_g': out['m_sgu_norm_g'], 'm_w_spatial': out['m_w_spatial'], 'm_b_spatial': out['m_b_spatial'], 'm_w_out': out['m_w_out'], 'm_norm_xattn_g': out['m_norm_xattn_g'], 'm_norm_mem_g': out['m_norm_mem_g'], 'm_w_q': out['m_w_q'], 'm_w_k': out['m_w_k'], 'm_w_v': out['m_w_v'], 'm_w_o': out['m_w_o'], 'm_norm_ffn_g': out['m_norm_ffn_g'], 'm_w_gate': out['m_w_gate'], 'm_w_up': out['m_w_up'], 'm_w_down': out['m_w_down'], 'm_final_norm_g': out['m_final_norm_g'], 'v_norm_mix_g': out['v_norm_mix_g'], 'v_w_in': out['v_w_in'], 'v_pool_w': out['v_pool_w'], 'v_pool_scale': out['v_pool_scale'], 'v_sgu_norm_g': out['v_sgu_norm_g'], 'v_w_spatial': out['v_w_spatial'], 'v_b_spatial': out['v_b_spatial'], 'v_w_out': out['v_w_out'], 'v_norm_xattn_g': out['v_norm_xattn_g'], 'v_norm_mem_g': out['v_norm_mem_g'], 'v_w_q': out['v_w_q'], 'v_w_k': out['v_w_k'], 'v_w_v': out['v_w_v'], 'v_w_o': out['v_w_o'], 'v_norm_ffn_g': out['v_norm_ffn_g'], 'v_w_gate': out['v_w_gate'], 'v_w_up': out['v_w_up'], 'v_w_down': out['v_w_down'], 'v_final_norm_g': out['v_final_norm_g']}


def _loss(weights, diff, rest, loss_target):
    with _jax.named_scope("forward"):
        args = {**rest, TWIN_DIFF_INPUT: diff, **{k: w.astype(_WEIGHT_DTYPES[k]) for k, w in weights.items()}}
        y = _forward(args)
    with _jax.named_scope("loss_head"):
        err = _jnp.square(y.astype(_jnp.float32) - loss_target)
        return 0.5 * _jnp.sum(_jnp.mean(err, axis=-1)) if err.ndim else 0.5 * err


def _adamw(w, g, m, v):
    m = ADAM_B1 * m + (1.0 - ADAM_B1) * g
    v = ADAM_B2 * v + (1.0 - ADAM_B2) * _jnp.square(g)
    m_hat = m / (1.0 - ADAM_B1 ** ADAM_STEP)
    v_hat = v / (1.0 - ADAM_B2 ** ADAM_STEP)
    delta = -ADAM_LR * (m_hat / (_jnp.sqrt(v_hat) + ADAM_EPS) + ADAM_WD * w)
    return delta, m, v


def reference(x, mem, norm_mix_g, w_in, pool_w, pool_scale, sgu_norm_g, w_spatial, b_spatial, w_out, norm_xattn_g, norm_mem_g, w_q, w_k, w_v, w_o, norm_ffn_g, w_gate, w_up, w_down, final_norm_g, loss_target, m_norm_mix_g, m_w_in, m_pool_w, m_pool_scale, m_sgu_norm_g, m_w_spatial, m_b_spatial, m_w_out, m_norm_xattn_g, m_norm_mem_g, m_w_q, m_w_k, m_w_v, m_w_o, m_norm_ffn_g, m_w_gate, m_w_up, m_w_down, m_final_norm_g, v_norm_mix_g, v_w_in, v_pool_w, v_pool_scale, v_sgu_norm_g, v_w_spatial, v_b_spatial, v_w_out, v_norm_xattn_g, v_norm_mem_g, v_w_q, v_w_k, v_w_v, v_w_o, v_norm_ffn_g, v_w_gate, v_w_up, v_w_down, v_final_norm_g):
    given = dict(x=x, mem=mem, norm_mix_g=norm_mix_g, w_in=w_in, pool_w=pool_w, pool_scale=pool_scale, sgu_norm_g=sgu_norm_g, w_spatial=w_spatial, b_spatial=b_spatial, w_out=w_out, norm_xattn_g=norm_xattn_g, norm_mem_g=norm_mem_g, w_q=w_q, w_k=w_k, w_v=w_v, w_o=w_o, norm_ffn_g=norm_ffn_g, w_gate=w_gate, w_up=w_up, w_down=w_down, final_norm_g=final_norm_g, loss_target=loss_target, m_norm_mix_g=m_norm_mix_g, m_w_in=m_w_in, m_pool_w=m_pool_w, m_pool_scale=m_pool_scale, m_sgu_norm_g=m_sgu_norm_g, m_w_spatial=m_w_spatial, m_b_spatial=m_b_spatial, m_w_out=m_w_out, m_norm_xattn_g=m_norm_xattn_g, m_norm_mem_g=m_norm_mem_g, m_w_q=m_w_q, m_w_k=m_w_k, m_w_v=m_w_v, m_w_o=m_w_o, m_norm_ffn_g=m_norm_ffn_g, m_w_gate=m_w_gate, m_w_up=m_w_up, m_w_down=m_w_down, m_final_norm_g=m_final_norm_g, v_norm_mix_g=v_norm_mix_g, v_w_in=v_w_in, v_pool_w=v_pool_w, v_pool_scale=v_pool_scale, v_sgu_norm_g=v_sgu_norm_g, v_w_spatial=v_w_spatial, v_b_spatial=v_b_spatial, v_w_out=v_w_out, v_norm_xattn_g=v_norm_xattn_g, v_norm_mem_g=v_norm_mem_g, v_w_q=v_w_q, v_w_k=v_w_k, v_w_v=v_w_v, v_w_o=v_w_o, v_norm_ffn_g=v_norm_ffn_g, v_w_gate=v_w_gate, v_w_up=v_w_up, v_w_down=v_w_down, v_final_norm_g=v_final_norm_g)
    weights = {n: given[n] for n in TWIN_WEIGHTS}
    shared = {n: given[n] for n in SHARED_INPUTS}
    per_example = {n: given[n] for n in ['x', 'mem']}
    grad_fn = _jax.value_and_grad(_loss, argnums=(0, 1))

    def one_microbatch(ex, loss_target):
        ex = dict(ex)
        diff = ex.pop(TWIN_DIFF_INPUT)
        return grad_fn(weights, diff, {**shared, **ex}, loss_target)

    if N_MICROBATCH == 1:
        loss, (grad_w, grad_x) = one_microbatch(per_example, given["loss_target"])
    else:
        def body(carry, xs):
            loss_sum, grad_sum = carry
            l_k, (gw_k, gx_k) = one_microbatch(xs[0], xs[1])
            with _jax.named_scope("update"):
                return (loss_sum + l_k, _jax.tree.map(_jnp.add, grad_sum, gw_k)), gx_k

        init = (_jnp.zeros((), _jnp.float32), _jax.tree.map(_jnp.zeros_like, weights))
        (loss, grad_w), grad_x = _jax.lax.scan(body, init, (per_example, given["loss_target"]))
    with _jax.named_scope("update"):
        delta_w, new_m, new_v = {}, {}, {}
        for n in TWIN_WEIGHTS:
            delta_w[n], new_m[n], new_v[n] = _adamw(weights[n], grad_w[n], given["m_" + n], given["v_" + n])
    return (loss, grad_x, *[grad_w[n] for n in TWIN_WEIGHTS], *[delta_w[n] for n in TWIN_WEIGHTS],
            *[new_m[n] for n in TWIN_WEIGHTS], *[new_v[n] for n in TWIN_WEIGHTS])
```

```python
import jax
import jax.numpy as jnp
from jax import lax
from jax.experimental import pallas as pl
from jax.experimental.pallas import tpu as pltpu

MXU_DTYPE = jnp.bfloat16
WIRE_DTYPE = jnp.bfloat16

EPS = 1e-6
CHUNK = 64
POOL_WINDOWS = (2, 4, 8, 16)
POOL_HALO = 16
N_XATTN_HEADS = 4
ADAM_LR = 0.001
ADAM_B1 = 0.9
ADAM_B2 = 0.999
ADAM_EPS = 1e-08
ADAM_WD = 0.01
ADAM_STEP = 10

N_CHIPS = 4
N_DEV = 8
LANES = 128
VMEM_LIMIT_BYTES = 56 * 2 ** 20

MESH = pl.DeviceIdType.MESH
ANY = pl.BlockSpec(memory_space=pl.ANY)
VMEM_WHOLE = pl.BlockSpec(memory_space=pltpu.VMEM)

NN = (((1,), (0,)), ((), ()))
NT = (((1,), (1,)), ((), ()))
TN = (((0,), (0,)), ((), ()))


def _tile(n, target, mult):
    best = None
    for t in range(mult, min(n, target) + 1, mult):
        if n % t == 0:
            best = t
    return n if best is None else best


def _params(n_grid_axes):
    return pltpu.CompilerParams(dimension_semantics=("arbitrary",) * n_grid_axes, vmem_limit_bytes=VMEM_LIMIT_BYTES)


def _rsqrt_mean_sq(x):
    return lax.rsqrt(jnp.mean(x * x, axis=-1, keepdims=True) + EPS)


def _sum_all(x):
    return jnp.sum(jnp.sum(x, axis=1, keepdims=True), axis=0, keepdims=True)


def _sigmoid(x):
    return 1.0 / (1.0 + jnp.exp(-x))


def _fused_matmul(name, grid, operands, pairs, acc_shapes, extras, outs, epilogue):
    nk = grid[2]
    n_op, n_ex, n_out, n_acc = len(operands), len(extras), len(outs), len(acc_shapes)

    def body(*refs):
        op_refs = refs[:n_op]
        ex_refs = refs[n_op:n_op + n_ex]
        out_refs = refs[n_op + n_ex:n_op + n_ex + n_out]
        acc_refs = refs[n_op + n_ex + n_out:]
        i, j, k = pl.program_id(0), pl.program_id(1), pl.program_id(2)
        loaded = {}

        def operand(n):
            if n not in loaded:
                loaded[n] = op_refs[n][...].astype(MXU_DTYPE)
            return loaded[n]

        parts = [None] * n_acc
        for ia, ib, iacc, dims in pairs:
            d = lax.dot_general(operand(ia), operand(ib), dims, preferred_element_type=jnp.float32)
            parts[iacc] = d if parts[iacc] is None else parts[iacc] + d

        if nk == 1:
            epilogue(parts, ex_refs, out_refs, i, j)
            return

        @pl.when(k == 0)
        def _():
            for a in range(n_acc):
                acc_refs[a][...] = parts[a]

        @pl.when(k > 0)
        def _():
            for a in range(n_acc):
                acc_refs[a][...] += parts[a]

        @pl.when(k == nk - 1)
        def _():
            epilogue([acc_refs[a][...] for a in range(n_acc)], ex_refs, out_refs, i, j)

    arrays = [o[0] for o in operands] + [e[0] for e in extras]
    in_specs = [pl.BlockSpec(o[1], o[2]) for o in operands] + [pl.BlockSpec(e[1], e[2]) for e in extras]
    return pl.pallas_call(
        body,
        name=name,
        grid=grid,
        in_specs=in_specs,
        out_specs=[pl.BlockSpec(o[2], o[3]) for o in outs],
        out_shape=[jax.ShapeDtypeStruct(o[0], o[1]) for o in outs],
        scratch_shapes=[pltpu.VMEM(s, jnp.float32) for s in acc_shapes] if nk > 1 else [],
        compiler_params=_params(3),
    )(*arrays)


def _store_epilogue(accs, ex_refs, out_refs, i, j):
    for a, o in zip(accs, out_refs):
        o[...] = a.astype(o.dtype)


def _matmul_nn(name, a, b, out_dtype, tm=1024, tn=512):
    M, K = a.shape
    N = b.shape[1]
    tm, tn = _tile(M, tm, 16), _tile(N, tn, LANES)
    return _fused_matmul(
        name, (M // tm, N // tn, 1),
        [(a, (tm, K), lambda i, j, k: (i, 0)), (b, (K, tn), lambda i, j, k: (0, j))],
        [(0, 1, 0, NN)], [(tm, tn)], [],
        [((M, N), out_dtype, (tm, tn), lambda i, j, k: (i, j))], _store_epilogue)[0]


def _matmul_nn_cols(name, a, b3, out_dtype, tm=1024):
    M, K = a.shape
    J, _, Nc = b3.shape
    tm = _tile(M, tm, 16)
    return _fused_matmul(
        name, (M // tm, J, 1),
        [(a, (tm, K), lambda i, j, k: (i, 0)), (b3, (None, K, Nc), lambda i, j, k: (j, 0, 0))],
        [(0, 1, 0, NN)], [(tm, Nc)], [],
        [((M, J * Nc), out_dtype, (tm, Nc), lambda i, j, k: (i, j))], _store_epilogue)[0]


def _matmul_nt(name, a_list, b_list, out_dtype, tm=1024, tn=512):
    M, K = a_list[0].shape
    N = b_list[0].shape[0]
    tm, tn = _tile(M, tm, 16), _tile(N, tn, LANES)
    n = len(a_list)
    operands = [(a, (tm, K), lambda i, j, k: (i, 0)) for a in a_list]
    operands += [(b, (tn, K), lambda i, j, k: (j, 0)) for b in b_list]
    return _fused_matmul(
        name, (M // tm, N // tn, 1), operands,
        [(p, n + p, 0, NT) for p in range(n)], [(tm, tn)], [],
        [((M, N), out_dtype, (tm, tn), lambda i, j, k: (i, j))], _store_epilogue)[0]


def _matmul_nt_cols(name, a_list, b3_list, out_dtype, tm=1024, tn=512):
    M = a_list[0].shape[0]
    J, N, Kc = b3_list[0].shape
    tm, tn = _tile(M, tm, 16), _tile(N, tn, LANES)
    n = len(a_list)
    operands = [(a, (tm, Kc), lambda i, j, k: (i, k)) for a in a_list]
    operands += [(b, (None, tn, Kc), lambda i, j, k: (k, j, 0)) for b in b3_list]
    return _fused_matmul(
        name, (M // tm, N // tn, J), operands,
        [(p, n + p, 0, NT) for p in range(n)], [(tm, tn)], [],
        [((M, N), out_dtype, (tm, tn), lambda i, j, k: (i, j))], _store_epilogue)[0]


def _matmul_tn(name, a, b_list, out_dtype, tka=1024, tn=1024, tkm=1024):
    M, Ka = a.shape
    N = b_list[0].shape[1]
    tka, tn, tkm = _tile(Ka, tka, LANES), _tile(N, tn, LANES), _tile(M, tkm, 16)
    n = len(b_list)
    operands = [(a, (tkm, tka), lambda i, j, k: (k, i))]
    operands += [(b, (tkm, tn), lambda i, j, k: (k, j)) for b in b_list]
    return _fused_matmul(
        name, (Ka // tka, N // tn, M // tkm), operands,
        [(0, 1 + p, p, TN) for p in range(n)], [(tka, tn)] * n, [],
        [((Ka, N), out_dtype, (tka, tn), lambda i, j, k: (i, j))] * n, _store_epilogue)


def _matmul_tn_cols(name, a, b_list, n_shards, out_dtype, tka=512, tkm=1024):
    M, Ka = a.shape
    Nc = b_list[0].shape[1] // n_shards
    tka, tkm = _tile(Ka, tka, LANES), _tile(M, tkm, 16)
    n = len(b_list)
    operands = [(a, (tkm, tka), lambda i, j, k: (k, i))]
    operands += [(b, (tkm, Nc), lambda i, j, k: (k, j)) for b in b_list]
    return _fused_matmul(
        name, (Ka // tka, n_shards, M // tkm), operands,
        [(0, 1 + p, p, TN) for p in range(n)], [(tka, Nc)] * n, [],
        [((n_shards, Ka, Nc), out_dtype, (None, tka, Nc), lambda i, j, k: (j, i, 0))] * n, _store_epilogue)


def _rmsnorm(name, x, g_row, out_dtype, tm=512):
    S, D = x.shape
    tm = _tile(S, tm, 16)

    def body(x_ref, g_ref, o_ref):
        xv = x_ref[...]
        o_ref[...] = (xv * _rsqrt_mean_sq(xv) * g_ref[...]).astype(o_ref.dtype)

    return pl.pallas_call(
        body, name=name, grid=(S // tm,),
        in_specs=[pl.BlockSpec((tm, D), lambda i: (i, 0)), pl.BlockSpec((1, D), lambda i: (0, 0))],
        out_specs=pl.BlockSpec((tm, D), lambda i: (i, 0)),
        out_shape=jax.ShapeDtypeStruct((S, D), out_dtype),
        compiler_params=_params(1),
    )(x, g_row)


def _rmsnorm_bwd_values(dh, xin, g):
    r = _rsqrt_mean_sq(xin)
    xhat = xin * r
    dhg = dh * g
    dx = r * (dhg - xhat * jnp.mean(dhg * xhat, axis=-1, keepdims=True))
    return dx, jnp.sum(dh * xhat, axis=0, keepdims=True)


def _rmsnorm_bwd(name, dh, xin, g_row, dres, low_dtype, tm=256):
    S, D = xin.shape
    tm = _tile(S, tm, 16)
    has_dx = dres is not None
    has_low = has_dx and low_dtype is not None

    def body(*refs):
        dh_ref, x_ref, g_ref = refs[:3]
        rest = refs[3:]
        dx, dg = _rmsnorm_bwd_values(dh_ref[...], x_ref[...], g_ref[...])
        if has_dx:
            dres_ref, rest = rest[0], rest[1:]
            dx = dx + dres_ref[...]
            rest[0][...] = dx
            if has_low:
                rest[1][...] = dx.astype(low_dtype)
        dg_ref = rest[-1]

        @pl.when(pl.program_id(0) == 0)
        def _():
            dg_ref[...] = jnp.zeros_like(dg_ref)

        dg_ref[...] += dg

    row = pl.BlockSpec((tm, D), lambda i: (i, 0))
    vec = pl.BlockSpec((1, D), lambda i: (0, 0))
    ins, in_specs = [dh, xin, g_row], [row, row, vec]
    out_shape, out_specs = [], []
    if has_dx:
        ins.append(dres)
        in_specs.append(row)
        out_shape.append(jax.ShapeDtypeStruct((S, D), jnp.float32))
        out_specs.append(row)
        if has_low:
            out_shape.append(jax.ShapeDtypeStruct((S, D), low_dtype))
            out_specs.append(row)
    out_shape.append(jax.ShapeDtypeStruct((1, D), jnp.float32))
    out_specs.append(vec)
    return pl.pallas_call(
        body, name=name, grid=(S // tm,), in_specs=in_specs, out_specs=out_specs, out_shape=out_shape,
        compiler_params=_params(1),
    )(*ins)


def _chunk_mask(blk):
    t = lax.broadcasted_iota(jnp.int32, (blk, blk), 0)
    s = lax.broadcasted_iota(jnp.int32, (blk, blk), 1)
    return (s // CHUNK) <= (t // CHUNK)


def _pool_inputs(a, halo, row0, tm, g, pg):
    cols = slice(g * pg, (g + 1) * pg)
    w = POOL_WINDOWS[g]
    s = jnp.concatenate([halo[:, cols], a[:, cols]], axis=0)
    span = 1
    while span < w:
        s = s + pltpu.roll(s, span, axis=0)
        span *= 2
    t = lax.broadcasted_iota(jnp.int32, (tm, 1), 0) + row0
    cnt = jnp.minimum(t + 1, w).astype(jnp.float32)
    return s[POOL_HALO:] / cnt - a[:, cols]


def _mixer_specs(S, tm, DP, DS):
    per_halo = tm // POOL_HALO
    n_halo = S // POOL_HALO
    a_spec = pl.BlockSpec((tm, DP), lambda i: (i, 0))
    prev_spec = pl.BlockSpec((POOL_HALO, DP), lambda i: (jnp.maximum(i * per_halo - 1, 0), 0))
    next_spec = pl.BlockSpec((POOL_HALO, DP), lambda i: (jnp.minimum((i + 1) * per_halo, n_halo - 1), 0))
    u_spec = pl.BlockSpec((tm, DS), lambda i: (i, 1))
    v_spec = pl.BlockSpec((tm, DS), lambda i: (i, 2))
    return a_spec, prev_spec, next_spec, u_spec, v_spec


def _whole(arr):
    nd = arr.ndim
    return pl.BlockSpec(arr.shape, lambda i: (0,) * nd)


def _mixer_fwd(proj, pw, ps_row, gs_row, ws, b_full, tm=512):
    S = proj.shape[0]
    G, PG, _ = pw.shape
    DP = G * PG
    H, BLK, _ = ws.shape
    DS = gs_row.shape[1]
    HD = DS // H
    assert DP == DS and proj.shape[1] == DP + 2 * DS
    tm = _tile(S, tm, BLK)

    def body(a_ref, prev_ref, u_ref, v_ref, pw_ref, ps_ref, gs_ref, ws_ref, b_ref, z_ref):
        i = pl.program_id(0)
        a = a_ref[...]
        halo = jnp.where(i > 0, prev_ref[...], 0.0)
        for g in range(G):
            cols = slice(g * PG, (g + 1) * PG)
            p = _pool_inputs(a, halo, i * tm, tm, g, PG)
            y = jnp.dot(p.astype(MXU_DTYPE), pw_ref[g], preferred_element_type=jnp.float32) * ps_ref[:, cols]
            z_ref[:, cols] = y.astype(z_ref.dtype)
        v = v_ref[...]
        vn = (v * _rsqrt_mean_sq(v) * gs_ref[...]).astype(MXU_DTYPE)
        u = u_ref[...]
        mask = _chunk_mask(BLK)
        for h in range(H):
            wm = jnp.where(mask, ws_ref[h], 0.0).astype(MXU_DTYPE)
            bias = b_ref[h]
            cols = slice(h * HD, (h + 1) * HD)
            for n in range(tm // BLK):
                rows = slice(n * BLK, (n + 1) * BLK)
                mixed = jnp.dot(wm, vn[rows, cols], preferred_element_type=jnp.float32) + bias
                z_ref[rows, DP + h * HD:DP + (h + 1) * HD] = (u[rows, cols] * mixed).astype(z_ref.dtype)

    a_spec, prev_spec, _, u_spec, v_spec = _mixer_specs(S, tm, DP, DS)
    return pl.pallas_call(
        body, name="mixer_fwd", grid=(S // tm,),
        in_specs=[a_spec, prev_spec, u_spec, v_spec, _whole(pw), _whole(ps_row), _whole(gs_row), _whole(ws), _whole(b_full)],
        out_specs=pl.BlockSpec((tm, DP + DS), lambda i: (i, 0)),
        out_shape=jax.ShapeDtypeStruct((S, DP + DS), MXU_DTYPE),
        compiler_params=_params(1),
    )(proj, proj, proj, proj, pw, ps_row, gs_row, ws, b_full)


def _mixer_bwd(proj, dz, pw, ps_row, gs_row, ws, b_full, tm=256):
    S = proj.shape[0]
    G, PG, _ = pw.shape
    DP = G * PG
    H, BLK, _ = ws.shape
    DS = gs_row.shape[1]
    HD = DS // H
    tm = _tile(S, tm, BLK)
    nb = S // tm
    ext = tm + POOL_HALO

    def body(a_ref, prev_ref, u_ref, v_ref, dzp_ref, dzp_next_ref, dzs_ref, pw_ref, ps_ref, gs_ref, ws_ref, b_ref,
             dproj_ref, dpw_ref, dps_ref, dgs_ref, dws_ref, db_ref, dvn_ref):
        i = pl.program_id(0)

        @pl.when(i == 0)
        def _():
            for r in (dpw_ref, dps_ref, dgs_ref, dws_ref, db_ref):
                r[...] = jnp.zeros_like(r)

        a = a_ref[...]
        halo = jnp.where(i > 0, prev_ref[...], 0.0)
        dyp = dzp_ref[...]
        dyp_next = jnp.where(i < nb - 1, dzp_next_ref[...], 0.0)
        dq_ext = jnp.concatenate([dyp, dyp_next], axis=0) * ps_ref[...]
        t_ext = lax.broadcasted_iota(jnp.int32, (ext, 1), 0) + i * tm
        for g in range(G):
            cols = slice(g * PG, (g + 1) * PG)
            w = POOL_WINDOWS[g]
            pb = _pool_inputs(a, halo, i * tm, tm, g, PG).astype(MXU_DTYPE)
            y_pre = jnp.dot(pb, pw_ref[g], preferred_element_type=jnp.float32)
            dps_ref[:, cols] += jnp.sum(dyp[:, cols] * y_pre, axis=0, keepdims=True)
            dqb = dq_ext[:, cols].astype(MXU_DTYPE)
            dpw_ref[g] += lax.dot_general(pb, dqb[:tm], TN, preferred_element_type=jnp.float32)
            dp = lax.dot_general(dqb, pw_ref[g], NT, preferred_element_type=jnp.float32)
            f = dp / jnp.minimum(t_ext + 1, w).astype(jnp.float32)
            span = 1
            while span < w:
                f = f + pltpu.roll(f, ext - span, axis=0)
                span *= 2
            dproj_ref[:, cols] = (f[:tm] - dp[:tm]).astype(dproj_ref.dtype)

        v = v_ref[...]
        rv = _rsqrt_mean_sq(v)
        vhat = v * rv
        gs = gs_ref[...]
        vnb = (vhat * gs).astype(MXU_DTYPE)
        u = u_ref[...]
        dys = dzs_ref[...]
        mask = _chunk_mask(BLK)
        for h in range(H):
            wm = jnp.where(mask, ws_ref[h], 0.0).astype(MXU_DTYPE)
            bias = b_ref[h]
            cols = slice(h * HD, (h + 1) * HD)
            dm_sum = jnp.zeros((BLK, HD), jnp.float32)
            dws_sum = jnp.zeros((BLK, BLK), jnp.float32)
            for n in range(tm // BLK):
                rows = slice(n * BLK, (n + 1) * BLK)
                vblk = vnb[rows, cols]
                mixed = jnp.dot(wm, vblk, preferred_element_type=jnp.float32) + bias
                dy = dys[rows, cols]
                dproj_ref[rows, DP + h * HD:DP + (h + 1) * HD] = (dy * mixed).astype(dproj_ref.dtype)
                dm = dy * u[rows, cols]
                dm_sum = dm_sum + dm
                dmb = dm.astype(MXU_DTYPE)
                dws_sum = dws_sum + lax.dot_general(dmb, vblk, NT, preferred_element_type=jnp.float32)
                dvn_ref[rows, cols] = lax.dot_general(wm, dmb, TN, preferred_element_type=jnp.float32)
            dws_ref[h] += jnp.where(mask, dws_sum, 0.0)
            db_ref[h:h + 1, :] += jnp.sum(dm_sum.T, axis=0, keepdims=True)
        dvn = dvn_ref[...]
        dvg = dvn * gs
        dv = rv * (dvg - vhat * jnp.mean(dvg * vhat, axis=-1, keepdims=True))
        dgs_ref[...] += jnp.sum(dvn * vhat, axis=0, keepdims=True)
        dproj_ref[:, DP + DS:] = dv.astype(dproj_ref.dtype)

    a_spec, prev_spec, next_spec, u_spec, v_spec = _mixer_specs(S, tm, DP, DS)
    dzp_spec = pl.BlockSpec((tm, DP), lambda i: (i, 0))
    dzs_spec = pl.BlockSpec((tm, DS), lambda i: (i, 1))
    small = [pw, ps_row, gs_row, ws, b_full]
    out_shape = [
        jax.ShapeDtypeStruct((S, DP + 2 * DS), MXU_DTYPE),
        jax.ShapeDtypeStruct(pw.shape, jnp.float32),
        jax.ShapeDtypeStruct(ps_row.shape, jnp.float32),
        jax.ShapeDtypeStruct(gs_row.shape, jnp.float32),
        jax.ShapeDtypeStruct(ws.shape, jnp.float32),
        jax.ShapeDtypeStruct((H, BLK), jnp.float32),
    ]
    out_specs = [pl.BlockSpec((tm, DP + 2 * DS), lambda i: (i, 0))] + [_whole(s) for s in out_shape[1:]]
    return pl.pallas_call(
        body, name="mixer_bwd", grid=(nb,),
        in_specs=[a_spec, prev_spec, u_spec, v_spec, dzp_spec, next_spec, dzs_spec] + [_whole(s) for s in small],
        out_specs=out_specs, out_shape=out_shape,
        scratch_shapes=[pltpu.VMEM((tm, DS), jnp.float32)],
        compiler_params=_params(1),
    )(proj, proj, proj, proj, dz, dz, dz, *small)


def _softmax_rows(q, k, scale):
    s = lax.dot_general(q, k, NT, preferred_element_type=jnp.float32) * scale
    e = jnp.exp(s - jnp.max(s, axis=-1, keepdims=True))
    return e / jnp.sum(e, axis=-1, keepdims=True)


def _attn_fwd(q, k, v, tm=512):
    S, D = q.shape
    M = k.shape[0]
    HD = D // N_XATTN_HEADS
    scale = HD ** -0.5
    tm = _tile(S, tm, 16)

    def body(q_ref, k_ref, v_ref, o_ref):
        for h in range(N_XATTN_HEADS):
            cols = slice(h * HD, (h + 1) * HD)
            p = _softmax_rows(q_ref[:, cols], k_ref[:, cols], scale)
            o = jnp.dot(p.astype(MXU_DTYPE), v_ref[:, cols], preferred_element_type=jnp.float32)
            o_ref[:, cols] = o.astype(o_ref.dtype)

    row = pl.BlockSpec((tm, D), lambda i: (i, 0))
    mem = pl.BlockSpec((M, D), lambda i: (0, 0))
    return pl.pallas_call(
        body, name="attn_fwd", grid=(S // tm,), in_specs=[row, mem, mem], out_specs=row,
        out_shape=jax.ShapeDtypeStruct((S, D), MXU_DTYPE), compiler_params=_params(1),
    )(q, k, v)


def _attn_bwd(q, k, v, do, tm=512):
    S, D = q.shape
    M = k.shape[0]
    HD = D // N_XATTN_HEADS
    scale = HD ** -0.5
    tm = _tile(S, tm, 16)

    def body(q_ref, k_ref, v_ref, do_ref, dq_ref, dk_ref, dv_ref):
        @pl.when(pl.program_id(0) == 0)
        def _():
            dk_ref[...] = jnp.zeros_like(dk_ref)
            dv_ref[...] = jnp.zeros_like(dv_ref)

        for h in range(N_XATTN_HEADS):
            cols = slice(h * HD, (h + 1) * HD)
            qh, kh, vh, doh = q_ref[:, cols], k_ref[:, cols], v_ref[:, cols], do_ref[:, cols]
            p = _softmax_rows(qh, kh, scale)
            dp = lax.dot_general(doh, vh, NT, preferred_element_type=jnp.float32)
            dv_ref[:, cols] += lax.dot_general(p.astype(MXU_DTYPE), doh, TN, preferred_element_type=jnp.float32)
            ds = (p * (dp - jnp.sum(dp * p, axis=-1, keepdims=True)) * scale).astype(MXU_DTYPE)
            dq_ref[:, cols] = jnp.dot(ds, kh, preferred_element_type=jnp.float32).astype(dq_ref.dtype)
            dk_ref[:, cols] += lax.dot_general(ds, qh, TN, preferred_element_type=jnp.float32)

    row = pl.BlockSpec((tm, D), lambda i: (i, 0))
    mem = pl.BlockSpec((M, D), lambda i: (0, 0))
    return pl.pallas_call(
        body, name="attn_bwd", grid=(S // tm,), in_specs=[row, mem, mem, row], out_specs=[row, mem, mem],
        out_shape=[jax.ShapeDtypeStruct((S, D), MXU_DTYPE), jax.ShapeDtypeStruct((M, D), jnp.float32),
                   jax.ShapeDtypeStruct((M, D), jnp.float32)],
        compiler_params=_params(1),
    )(q, k, v, do)


def _proj_residual_norm(name, a, w2d, resid, g_row, tm=256):
    M, K = a.shape
    N = w2d.shape[1]
    tm = _tile(M, tm, 16)

    def epilogue(accs, ex_refs, out_refs, i, j):
        x_new = ex_refs[0][...] + accs[0]
        out_refs[0][...] = x_new
        out_refs[1][...] = (x_new * _rsqrt_mean_sq(x_new) * ex_refs[1][...]).astype(MXU_DTYPE)

    row = lambda i, j, k: (i, 0)
    return _fused_matmul(
        name, (M // tm, 1, 1),
        [(a, (tm, K), row), (w2d, (K, N), lambda i, j, k: (0, 0))],
        [(0, 1, 0, NN)], [(tm, N)],
        [(resid, (tm, N), row), (g_row, (1, N), lambda i, j, k: (0, 0))],
        [((M, N), jnp.float32, (tm, N), row), ((M, N), MXU_DTYPE, (tm, N), row)], epilogue)


def _ffn_gate_up(h, wg3, wu3, tm=512):
    M, K = h.shape
    J, _, Nc = wg3.shape
    tm = _tile(M, tm, 16)

    def epilogue(accs, ex_refs, out_refs, i, j):
        gate, up = accs
        out_refs[0][...] = gate
        out_refs[1][...] = up
        out_refs[2][...] = (gate * _sigmoid(gate) * up).astype(MXU_DTYPE)

    col = lambda i, j, k: (i, j)
    wspec = lambda i, j, k: (j, 0, 0)
    return _fused_matmul(
        "ffn_gate_up", (M // tm, J, 1),
        [(h, (tm, K), lambda i, j, k: (i, 0)), (wg3, (None, K, Nc), wspec), (wu3, (None, K, Nc), wspec)],
        [(0, 1, 0, NN), (0, 2, 1, NN)], [(tm, Nc)] * 2, [],
        [((M, J * Nc), jnp.float32, (tm, Nc), col), ((M, J * Nc), jnp.float32, (tm, Nc), col),
         ((M, J * Nc), MXU_DTYPE, (tm, Nc), col)], epilogue)


def _ffn_down_loss(act, wd2d, x2, target, g_row, n_k, tm=256):
    M, F = act.shape
    D = wd2d.shape[1]
    tm = _tile(M, tm, 16)
    tk = F // n_k

    def epilogue(accs, ex_refs, out_refs, i, j):
        x3 = ex_refs[0][...] + accs[0]
        g = ex_refs[2][...]
        r = _rsqrt_mean_sq(x3)
        xhat = x3 * r
        diff = xhat * g - ex_refs[1][...]
        dy = diff / D
        dyg = dy * g
        dx = r * (dyg - xhat * jnp.mean(dyg * xhat, axis=-1, keepdims=True))
        out_refs[0][...] = dx
        out_refs[1][...] = dx.astype(MXU_DTYPE)

        @pl.when(i == 0)
        def _():
            out_refs[2][...] = jnp.zeros_like(out_refs[2])
            out_refs[3][...] = jnp.zeros_like(out_refs[3])

        out_refs[2][...] += 0.5 * _sum_all(jnp.mean(diff * diff, axis=-1, keepdims=True))
        out_refs[3][...] += jnp.sum(dy * xhat, axis=0, keepdims=True)

    row = lambda i, j, k: (i, 0)
    const = lambda i, j, k: (0, 0)
    return _fused_matmul(
        "ffn_down_loss", (M // tm, 1, n_k),
        [(act, (tm, tk), lambda i, j, k: (i, k)), (wd2d, (tk, D), lambda i, j, k: (k, 0))],
        [(0, 1, 0, NN)], [(tm, D)],
        [(x2, (tm, D), row), (target, (tm, D), row), (g_row, (1, D), const)],
        [((M, D), jnp.float32, (tm, D), row), ((M, D), MXU_DTYPE, (tm, D), row),
         ((1, 1), jnp.float32, (1, 1), const), ((1, D), jnp.float32, (1, D), const)], epilogue)


def _ffn_down_bwd(dx3_low, wd2d, gate, up, n_shards, tm=512):
    M, D = dx3_low.shape
    F = wd2d.shape[0]
    tn = F // n_shards
    tm = _tile(M, tm, 16)

    def epilogue(accs, ex_refs, out_refs, i, j):
        dact = accs[0]
        gate_v, up_v = ex_refs[0][...], ex_refs[1][...]
        sg = _sigmoid(gate_v)
        out_refs[0][...] = (dact * up_v * (sg * (1.0 + gate_v * (1.0 - sg)))).astype(MXU_DTYPE)
        out_refs[1][...] = (dact * (gate_v * sg)).astype(MXU_DTYPE)

    col = lambda i, j, k: (i, j)
    return _fused_matmul(
        "ffn_down_bwd", (M // tm, n_shards, 1),
        [(dx3_low, (tm, D), lambda i, j, k: (i, 0)), (wd2d, (tn, D), lambda i, j, k: (j, 0))],
        [(0, 1, 0, NT)], [(tm, tn)],
        [(gate, (tm, tn), col), (up, (tm, tn), col)],
        [((M, F), MXU_DTYPE, (tm, tn), col)] * 2, epilogue)


def _row_tile(R, C, target_bytes=2 ** 21):
    return _tile(R, max(16, target_bytes // (4 * C)), 16)


def _cast(name, x2d, dtype):
    R, C = x2d.shape
    tr = _row_tile(R, C)

    def body(x_ref, o_ref):
        o_ref[...] = x_ref[...].astype(o_ref.dtype)

    blk = pl.BlockSpec((tr, C), lambda i: (i, 0))
    return pl.pallas_call(body, name=name, grid=(R // tr,), in_specs=[blk], out_specs=blk,
                          out_shape=jax.ShapeDtypeStruct((R, C), dtype), compiler_params=_params(1))(x2d)


def _add2(name, a2d, b2d, dtype):
    R, C = a2d.shape
    tr = _row_tile(R, C)

    def body(a_ref, b_ref, o_ref):
        o_ref[...] = (a_ref[...].astype(jnp.float32) + b_ref[...].astype(jnp.float32)).astype(o_ref.dtype)

    blk = pl.BlockSpec((tr, C), lambda i: (i, 0))
    return pl.pallas_call(body, name=name, grid=(R // tr,), in_specs=[blk, blk], out_specs=blk,
                          out_shape=jax.ShapeDtypeStruct((R, C), dtype), compiler_params=_params(1))(a2d, b2d)


def _sum_leading(name, x3d, dtype):
    J, R, C = x3d.shape
    tr = _row_tile(R, C * J)

    def body(x_ref, o_ref):
        acc = x_ref[0].astype(jnp.float32)
        for j in range(1, J):
            acc = acc + x_ref[j].astype(jnp.float32)
        o_ref[...] = acc.astype(o_ref.dtype)

    return pl.pallas_call(
        body, name=name, grid=(R // tr,), in_specs=[pl.BlockSpec((J, tr, C), lambda i: (0, i, 0))],
        out_specs=pl.BlockSpec((tr, C), lambda i: (i, 0)), out_shape=jax.ShapeDtypeStruct((R, C), dtype),
        compiler_params=_params(1))(x3d)


def _adamw_values(w, g, m, v):
    m = ADAM_B1 * m + (1.0 - ADAM_B1) * g
    v = ADAM_B2 * v + (1.0 - ADAM_B2) * (g * g)
    m_hat = m / (1.0 - ADAM_B1 ** ADAM_STEP)
    v_hat = v / (1.0 - ADAM_B2 ** ADAM_STEP)
    delta = -ADAM_LR * (m_hat / (jnp.sqrt(v_hat) + ADAM_EPS) + ADAM_WD * w)
    return delta, m, v


def _adamw(name, w, g, m, v):
    R, C = w.shape
    tr = _row_tile(R, C, 2 ** 20)

    def body(w_ref, g_ref, m_ref, v_ref, d_ref, mo_ref, vo_ref):
        d_ref[...], mo_ref[...], vo_ref[...] = _adamw_values(w_ref[...], g_ref[...], m_ref[...], v_ref[...])

    blk = pl.BlockSpec((tr, C), lambda i: (i, 0))
    return pl.pallas_call(
        body, name=name, grid=(R // tr,), in_specs=[blk] * 4, out_specs=[blk] * 3,
        out_shape=[jax.ShapeDtypeStruct((R, C), jnp.float32)] * 3, compiler_params=_params(1))(w, g, m, v)


def _adamw_small(g_packed, ws, ms, vs, row_counts):
    n = len(ws)

    def body(*refs):
        g_ref = refs[0]
        w_refs, m_refs, v_refs = refs[1:1 + n], refs[1 + n:1 + 2 * n], refs[1 + 2 * n:1 + 3 * n]
        outs = refs[1 + 3 * n:]
        off = 0
        for t in range(n):
            g = g_ref[off:off + row_counts[t], :]
            off += row_counts[t]
            delta, m_new, v_new = _adamw_values(w_refs[t][...], g, m_refs[t][...], v_refs[t][...])
            outs[4 * t][...] = g
            outs[4 * t + 1][...] = delta
            outs[4 * t + 2][...] = m_new
            outs[4 * t + 3][...] = v_new

    out_shape = []
    for t in range(n):
        out_shape += [jax.ShapeDtypeStruct((row_counts[t], LANES), jnp.float32)] * 4
    return pl.pallas_call(
        body, name="adamw_small", in_specs=[VMEM_WHOLE] * (1 + 3 * n), out_specs=[VMEM_WHOLE] * (4 * n),
        out_shape=out_shape, compiler_params=pltpu.CompilerParams(vmem_limit_bytes=VMEM_LIMIT_BYTES),
    )(g_packed, *ws, *ms, *vs)


def _position():
    x, y, c = lax.axis_index("x"), lax.axis_index("y"), lax.axis_index("c")
    other_chips = [(1 - x, y), (x, 1 - y), (1 - x, 1 - y)]
    return x, y, c, other_chips


def _half(rows, which):
    return pl.ds(which * (rows // 2), rows // 2)


def _gather_weights(shards):
    T = len(shards)

    def body(*refs):
        ins, outs = refs[:T], refs[T:2 * T]
        send_sems, recv_sems, local_sems = refs[2 * T:]
        x, y, c, other_chips = _position()
        mine = 2 * x + y
        sibling = (x, y, 1 - c)

        def block(t, chip, which):
            return outs[t].at[chip, _half(ins[t].shape[0], which)]

        def copy(t, k, dst, to, src=None):
            return pltpu.make_async_remote_copy(
                src_ref=dst if src is None else src, dst_ref=dst, send_sem=send_sems.at[t, k],
                recv_sem=recv_sems.at[t, k], device_id=to, device_id_type=MESH)

        local = [pltpu.make_async_copy(ins[t], outs[t].at[mine], local_sems.at[t]) for t in range(T)]
        for cp in local:
            cp.start()
        started = []
        for t in range(T):
            for k, (cx, cy) in enumerate(other_chips):
                cp = copy(t, k, block(t, mine, c), (cx, cy, c), src=ins[t].at[_half(ins[t].shape[0], c)])
                cp.start()
                started.append(cp)
        for k, (cx, cy) in enumerate(other_chips):
            for t in range(T):
                landed = block(t, 2 * cx + cy, c)
                copy(t, k, landed, (cx, cy, c)).wait_recv()
                cp = copy(t, 3 + k, landed, sibling)
                cp.start()
                started.append(cp)
        for k, (cx, cy) in enumerate(other_chips):
            for t in range(T):
                copy(t, 3 + k, block(t, 2 * cx + cy, 1 - c), sibling).wait_recv()
        for cp in started:
            cp.wait_send()
        for cp in local:
            cp.wait()

    return pl.pallas_call(
        body, name="gather_weights", in_specs=[ANY] * T, out_specs=[ANY] * T,
        out_shape=[jax.ShapeDtypeStruct((N_CHIPS,) + s.shape, s.dtype) for s in shards],
        scratch_shapes=[pltpu.SemaphoreType.DMA((T, 6)), pltpu.SemaphoreType.DMA((T, 6)), pltpu.SemaphoreType.DMA((T,))],
    )(*shards)


def _swap_halves(grads):
    T = len(grads)

    def body(*refs):
        ins, kept, got = refs[:T], refs[T:2 * T], refs[2 * T:3 * T]
        send_sems, recv_sems, local_sems = refs[3 * T:]
        x, y, c, _ = _position()
        local, remote = [], []
        for t in range(T):
            rows = ins[t].shape[1]
            local.append(pltpu.make_async_copy(ins[t].at[:, _half(rows, c)], kept[t], local_sems.at[t]))
            remote.append(pltpu.make_async_remote_copy(
                src_ref=ins[t].at[:, _half(rows, 1 - c)], dst_ref=got[t], send_sem=send_sems.at[t],
                recv_sem=recv_sems.at[t], device_id=(x, y, 1 - c), device_id_type=MESH))
        for cp in local + remote:
            cp.start()
        for cp in remote:
            cp.wait()
        for cp in local:
            cp.wait()

    half_shapes = [jax.ShapeDtypeStruct((g.shape[0], g.shape[1] // 2, g.shape[2]), g.dtype) for g in grads]
    res = pl.pallas_call(
        body, name="swap_grad_halves", in_specs=[ANY] * T, out_specs=[ANY] * (2 * T), out_shape=half_shapes * 2,
        scratch_shapes=[pltpu.SemaphoreType.DMA((T,)), pltpu.SemaphoreType.DMA((T,)), pltpu.SemaphoreType.DMA((T,))],
    )(*grads)
    return res[:T], res[T:]


def _scatter_to_owners(chip_sums):
    T = len(chip_sums)

    def body(*refs):
        ins, outs = refs[:T], refs[T:2 * T]
        send_sems, recv_sems, local_sems = refs[2 * T:]
        x, y, c, other_chips = _position()
        mine = 2 * x + y
        local = [pltpu.make_async_copy(ins[t].at[mine], outs[t].at[mine], local_sems.at[t]) for t in range(T)]
        remote = []
        for t in range(T):
            for k, (cx, cy) in enumerate(other_chips):
                remote.append(pltpu.make_async_remote_copy(
                    src_ref=ins[t].at[2 * cx + cy], dst_ref=outs[t].at[mine], send_sem=send_sems.at[t, k],
                    recv_sem=recv_sems.at[t, k], device_id=(cx, cy, c), device_id_type=MESH))
        for cp in local + remote:
            cp.start()
        for t in range(T):
            for k, (cx, cy) in enumerate(other_chips):
                pltpu.make_async_remote_copy(
                    src_ref=ins[t].at[mine], dst_ref=outs[t].at[2 * cx + cy], send_sem=send_sems.at[t, k],
                    recv_sem=recv_sems.at[t, k], device_id=(cx, cy, c), device_id_type=MESH).wait_recv()
        for cp in remote:
            cp.wait_send()
        for cp in local:
            cp.wait()

    return pl.pallas_call(
        body, name="scatter_grads_to_owners", in_specs=[ANY] * T, out_specs=[ANY] * T,
        out_shape=[jax.ShapeDtypeStruct(s.shape, s.dtype) for s in chip_sums],
        scratch_shapes=[pltpu.SemaphoreType.DMA((T, 3)), pltpu.SemaphoreType.DMA((T, 3)), pltpu.SemaphoreType.DMA((T,))],
    )(*chip_sums)


def _join_halves(halves):
    T = len(halves)

    def body(*refs):
        ins, outs = refs[:T], refs[T:2 * T]
        send_sems, recv_sems, local_sems = refs[2 * T:]
        x, y, c, _ = _position()
        local, remote = [], []
        for t in range(T):
            rows = 2 * ins[t].shape[0]
            local.append(pltpu.make_async_copy(ins[t], outs[t].at[_half(rows, c)], local_sems.at[t]))
            remote.append(pltpu.make_async_remote_copy(
                src_ref=ins[t], dst_ref=outs[t].at[_half(rows, c)], send_sem=send_sems.at[t],
                recv_sem=recv_sems.at[t], device_id=(x, y, 1 - c), device_id_type=MESH))
        for cp in local + remote:
            cp.start()
        for t in range(T):
            rows = 2 * ins[t].shape[0]
            pltpu.make_async_remote_copy(
                src_ref=ins[t], dst_ref=outs[t].at[_half(rows, 1 - c)], send_sem=send_sems.at[t],
                recv_sem=recv_sems.at[t], device_id=(x, y, 1 - c), device_id_type=MESH).wait_recv()
        for cp in remote:
            cp.wait_send()
        for cp in local:
            cp.wait()

    return pl.pallas_call(
        body, name="join_grad_halves", in_specs=[ANY] * T, out_specs=[ANY] * T,
        out_shape=[jax.ShapeDtypeStruct((2 * h.shape[0], h.shape[1]), h.dtype) for h in halves],
        scratch_shapes=[pltpu.SemaphoreType.DMA((T,)), pltpu.SemaphoreType.DMA((T,)), pltpu.SemaphoreType.DMA((T,))],
    )(*halves)


def _allreduce_small(part):
    rows = part.shape[0]

    def body(x_ref, out_ref, gath, send_sems, recv_sems):
        x, y, c, other_chips = _position()
        sibling = (x, y, 1 - c)

        def slot(px, py, pc):
            return gath.at[4 * px + 2 * py + pc]

        def copy(k, owner, to, src=None):
            return pltpu.make_async_remote_copy(
                src_ref=slot(*owner) if src is None else src, dst_ref=slot(*owner), send_sem=send_sems.at[k],
                recv_sem=recv_sems.at[k], device_id=to, device_id_type=MESH)

        gath[4 * x + 2 * y + c] = x_ref[...]
        first = [copy(0, (x, y, c), sibling, src=x_ref)]
        first += [copy(1 + k, (x, y, c), (cx, cy, c), src=x_ref) for k, (cx, cy) in enumerate(other_chips)]
        for cp in first:
            cp.start()
        passed = [copy(4 + k, (cx, cy, c), sibling) for k, (cx, cy) in enumerate(other_chips)]
        for k, (cx, cy) in enumerate(other_chips):
            copy(1 + k, (cx, cy, c), (x, y, c)).wait_recv()
            passed[k].start()
        copy(0, (x, y, 1 - c), (x, y, c)).wait_recv()
        for k, (cx, cy) in enumerate(other_chips):
            copy(4 + k, (cx, cy, 1 - c), (x, y, c)).wait_recv()
        for cp in first + passed:
            cp.wait_send()
        acc = gath[0]
        for d in range(1, N_DEV):
            acc = acc + gath[d]
        out_ref[...] = acc

    return pl.pallas_call(
        body, name="allreduce_small_grads", in_specs=[VMEM_WHOLE], out_specs=VMEM_WHOLE,
        out_shape=jax.ShapeDtypeStruct((rows, LANES), jnp.float32),
        scratch_shapes=[pltpu.VMEM((N_DEV, rows, LANES), jnp.float32), pltpu.SemaphoreType.DMA((7,)),
                        pltpu.SemaphoreType.DMA((7,))],
        compiler_params=pltpu.CompilerParams(vmem_limit_bytes=VMEM_LIMIT_BYTES),
    )(part)


BIG = ("w_in", "pool_w", "w_out", "w_q", "w_k", "w_v", "w_o", "w_gate", "w_up", "w_down")
SMALL = ("norm_mix_g", "pool_scale", "sgu_norm_g", "w_spatial", "b_spatial", "norm_xattn_g", "norm_mem_g",
         "norm_ffn_g", "final_norm_g")
WEIGHTS = ("norm_mix_g", "w_in", "pool_w", "pool_scale", "sgu_norm_g", "w_spatial", "b_spatial", "w_out",
           "norm_xattn_g", "norm_mem_g", "w_q", "w_k", "w_v", "w_o", "norm_ffn_g", "w_gate", "w_up", "w_down",
           "final_norm_g")


def _as2d(a):
    return a.reshape(-1, a.shape[-1])


def _as_lanes(a):
    return a.reshape(-1, LANES)


def kernel(x, mem, norm_mix_g, w_in, pool_w, pool_scale, sgu_norm_g, w_spatial, b_spatial, w_out, norm_xattn_g, norm_mem_g, w_q, w_k, w_v, w_o, norm_ffn_g, w_gate, w_up, w_down, final_norm_g, loss_target, m_norm_mix_g, m_w_in, m_pool_w, m_pool_scale, m_sgu_norm_g, m_w_spatial, m_b_spatial, m_w_out, m_norm_xattn_g, m_norm_mem_g, m_w_q, m_w_k, m_w_v, m_w_o, m_norm_ffn_g, m_w_gate, m_w_up, m_w_down, m_final_norm_g, v_norm_mix_g, v_w_in, v_pool_w, v_pool_scale, v_sgu_norm_g, v_w_spatial, v_b_spatial, v_w_out, v_norm_xattn_g, v_norm_mem_g, v_w_q, v_w_k, v_w_v, v_w_o, v_norm_ffn_g, v_w_gate, v_w_up, v_w_down, v_final_norm_g):
    args = dict(locals())
    weights = {n: args[n] for n in WEIGHTS}
    moments_m = {n: args["m_" + n] for n in WEIGHTS}
    moments_v = {n: args["v_" + n] for n in WEIGHTS}
    low = MXU_DTYPE

    xs = x[0]
    mems = mem[0]
    target = loss_target[0]
    D = xs.shape[1]
    G, PG = pool_w.shape[1], pool_w.shape[3]

    shards = [_cast("cast_" + n, _as2d(weights[n]), WIRE_DTYPE) for n in BIG]
    gathered = dict(zip(BIG, _gather_weights(shards)))
    wg_in = gathered["w_in"]
    pw = gathered["pool_w"].reshape(N_CHIPS, G, PG // N_CHIPS, PG).transpose(1, 0, 2, 3).reshape(G, PG, PG)
    full = {n: gathered[n].reshape(-1, gathered[n].shape[-1]) for n in ("w_out", "w_q", "w_k", "w_v", "w_o", "w_down")}
    wg_gate, wg_up = gathered["w_gate"], gathered["w_up"]

    g_mix, g_xattn, g_mem, g_ffn = norm_mix_g, norm_xattn_g, norm_mem_g, norm_ffn_g
    g_final = final_norm_g.reshape(1, D)
    ws = w_spatial[0]
    b_full = jnp.broadcast_to(b_spatial[0][:, :, None], ws.shape[:2] + (sgu_norm_g.shape[1] // ws.shape[0],))

    h1 = _rmsnorm("norm_mix", xs, g_mix, low)
    proj = _matmul_nn_cols("proj_in", h1, wg_in, jnp.float32)
    z = _mixer_fwd(proj, pw, pool_scale, sgu_norm_g, ws, b_full)
    x1, h2 = _proj_residual_norm("mix_out", z, full["w_out"], xs, g_xattn)
    q = _matmul_nn("xattn_q", h2, full["w_q"], low)
    mn = _rmsnorm("norm_mem", mems, g_mem, low)
    k = _matmul_nn("xattn_k", mn, full["w_k"], low)
    vv = _matmul_nn("xattn_v", mn, full["w_v"], low)
    o = _attn_fwd(q, k, vv)
    x2, h3 = _proj_residual_norm("xattn_out", o, full["w_o"], x1, g_ffn)
    gate, up, act = _ffn_gate_up(h3, wg_gate, wg_up)
    dx3, dx3_low, loss_part, d_final_g = _ffn_down_loss(act, full["w_down"], x2, target, g_final, N_CHIPS)

    dgate, dup = _ffn_down_bwd(dx3_low, full["w_down"], gate, up, N_CHIPS)
    (d_w_down,) = _matmul_tn("grad_w_down", act, [dx3_low], WIRE_DTYPE)
    d_w_gate, d_w_up = _matmul_tn_cols("grad_w_gate_up", h3, [dgate, dup], N_CHIPS, WIRE_DTYPE)
    dh3 = _matmul_nt_cols("ffn_dh", [dgate, dup], [wg_gate, wg_up], jnp.float32)
    dx2, dx2_low, d_ffn_g = _rmsnorm_bwd("norm_ffn_bwd", dh3, x2, g_ffn, dx3, low)

    do = _matmul_nt("xattn_do", [dx2_low], [full["w_o"]], low)
    (d_w_o,) = _matmul_tn("grad_w_o", o, [dx2_low], WIRE_DTYPE)
    dq, dk, dvv = _attn_bwd(q, k, vv, do)
    (d_w_q,) = _matmul_tn("grad_w_q", h2, [dq], WIRE_DTYPE)
    dh2 = _matmul_nt("xattn_dh", [dq], [full["w_q"]], jnp.float32)
    d_w_k, d_w_v = _matmul_tn("grad_w_kv", mn, [dk, dvv], WIRE_DTYPE)
    dmn = _matmul_nt("xattn_dmem", [dk, dvv], [full["w_k"], full["w_v"]], jnp.float32)
    (d_mem_g,) = _rmsnorm_bwd("norm_mem_bwd", dmn, mems, g_mem, None, None)
    dx1, dx1_low, d_xattn_g = _rmsnorm_bwd("norm_xattn_bwd", dh2, x1, g_xattn, dx2, low)

    dz = _matmul_nt("mix_dz", [dx1_low], [full["w_out"]], jnp.float32)
    (d_w_out,) = _matmul_tn("grad_w_out", z, [dx1_low], WIRE_DTYPE)
    dproj, d_pw, d_ps, d_gs, d_ws, d_b = _mixer_bwd(proj, dz, pw, pool_scale, sgu_norm_g, ws, b_full)
    (d_w_in,) = _matmul_tn_cols("grad_w_in", h1, [dproj], N_CHIPS, WIRE_DTYPE)
    dh1 = _matmul_nt_cols("mix_dh", [dproj], [wg_in], jnp.float32)
    grad_x, d_mix_g = _rmsnorm_bwd("norm_mix_bwd", dh1, xs, g_mix, dx1, None)

    d_pool_w = d_pw.reshape(G, N_CHIPS, PG // N_CHIPS, PG).transpose(1, 0, 2, 3).reshape(N_CHIPS, PG, PG).astype(WIRE_DTYPE)
    partial = {
        "w_in": d_w_in, "pool_w": d_pool_w, "w_gate": d_w_gate, "w_up": d_w_up,
        "w_out": d_w_out, "w_q": d_w_q, "w_k": d_w_k, "w_v": d_w_v, "w_o": d_w_o, "w_down": d_w_down,
    }
    partial = [partial[n].reshape((N_CHIPS, -1, partial[n].shape[-1])) for n in BIG]
    kept, received = _swap_halves(partial)
    chip_sums = [
        _add2("chip_sum_" + n, a.reshape(-1, a.shape[-1]), b.reshape(-1, b.shape[-1]), WIRE_DTYPE).reshape(a.shape)
        for n, a, b in zip(BIG, kept, received)]
    contributions = _scatter_to_owners(chip_sums)
    reduced_halves = [_sum_leading("owner_sum_" + n, cb, jnp.float32) for n, cb in zip(BIG, contributions)]
    grads = dict(zip(BIG, _join_halves(reduced_halves)))

    small_parts = {
        "norm_mix_g": d_mix_g, "pool_scale": d_ps, "sgu_norm_g": d_gs, "w_spatial": d_ws, "b_spatial": d_b,
        "norm_xattn_g": d_xattn_g, "norm_mem_g": d_mem_g, "norm_ffn_g": d_ffn_g, "final_norm_g": d_final_g,
    }
    row_counts = [small_parts[n].size // LANES for n in SMALL]
    packed = jnp.concatenate([_as_lanes(small_parts[n]) for n in SMALL], axis=0)
    small_sum = _allreduce_small(packed)

    out_grad, out_delta, out_m, out_v = {}, {}, {}, {}
    for n in BIG:
        shape = weights[n].shape
        g2d = grads[n]
        delta, m_new, v_new = _adamw("adamw_" + n, _as2d(weights[n]), g2d, _as2d(moments_m[n]), _as2d(moments_v[n]))
        out_grad[n], out_delta[n] = g2d.reshape(shape), delta.reshape(shape)
        out_m[n], out_v[n] = m_new.reshape(shape), v_new.reshape(shape)
    small_out = _adamw_small(
        small_sum, [_as_lanes(weights[n]) for n in SMALL], [_as_lanes(moments_m[n]) for n in SMALL],
        [_as_lanes(moments_v[n]) for n in SMALL], row_counts)
    for t, n in enumerate(SMALL):
        shape = weights[n].shape
        out_grad[n], out_delta[n], out_m[n], out_v[n] = (a.reshape(shape) for a in small_out[4 * t:4 * t + 4])

    loss = lax.psum(loss_part[0, 0], ("x", "y", "c"))
    return (loss, grad_x[None], *[out_grad[n] for n in WEIGHTS], *[out_delta[n] for n in WEIGHTS],
            *[out_m[n] for n in WEIGHTS], *[out_v[n] for n in WEIGHTS])
```

```python
import math

import jax
import jax.numpy as jnp
from jax import lax
from jax.experimental import pallas as pl
from jax.experimental.pallas import tpu as pltpu

MXU_DTYPE = jnp.bfloat16
WIRE_DTYPE = jnp.bfloat16

EPS = 1e-6
CHUNK = 64
POOL_WINDOWS = (2, 4, 8, 16)
POOL_HALO = 16
N_XATTN_HEADS = 4
ADAM_LR = 0.001
ADAM_B1 = 0.9
ADAM_B2 = 0.999
ADAM_EPS = 1e-08
ADAM_WD = 0.01
ADAM_STEP = 10

N_CHIPS = 4
N_DEV = 8
LANES = 128
VMEM_LIMIT_BYTES = 56 * 2 ** 20
COPY_CHUNK_BYTES = 512 * 2 ** 10
COPY_ROW_ALIGN = 16

MESH = pl.DeviceIdType.MESH
ANY = pl.BlockSpec(memory_space=pl.ANY)
VMEM_WHOLE = pl.BlockSpec(memory_space=pltpu.VMEM)

NN = (((1,), (0,)), ((), ()))
NT = (((1,), (1,)), ((), ()))
TN = (((0,), (0,)), ((), ()))


def _tile(n, target, mult):
    best = None
    for t in range(mult, min(n, target) + 1, mult):
        if n % t == 0:
            best = t
    return n if best is None else best


def _params(n_grid_axes):
    return pltpu.CompilerParams(dimension_semantics=("arbitrary",) * n_grid_axes, vmem_limit_bytes=VMEM_LIMIT_BYTES)


def _rsqrt_mean_sq(x):
    return lax.rsqrt(jnp.mean(x * x, axis=-1, keepdims=True) + EPS)


def _sum_all(x):
    return jnp.sum(jnp.sum(x, axis=1, keepdims=True), axis=0, keepdims=True)


def _sigmoid(x):
    return 1.0 / (1.0 + jnp.exp(-x))


def _fused_matmul(name, grid, operands, pairs, acc_shapes, extras, outs, epilogue):
    nk = grid[2]
    n_op, n_ex, n_out, n_acc = len(operands), len(extras), len(outs), len(acc_shapes)

    def body(*refs):
        op_refs = refs[:n_op]
        ex_refs = refs[n_op:n_op + n_ex]
        out_refs = refs[n_op + n_ex:n_op + n_ex + n_out]
        acc_refs = refs[n_op + n_ex + n_out:]
        i, j, k = pl.program_id(0), pl.program_id(1), pl.program_id(2)
        loaded = {}

        def operand(n):
            if n not in loaded:
                loaded[n] = op_refs[n][...].astype(MXU_DTYPE)
            return loaded[n]

        parts = [None] * n_acc
        for ia, ib, iacc, dims in pairs:
            d = lax.dot_general(operand(ia), operand(ib), dims, preferred_element_type=jnp.float32)
            parts[iacc] = d if parts[iacc] is None else parts[iacc] + d

        if nk == 1:
            epilogue(parts, ex_refs, out_refs, i, j)
            return

        @pl.when(k == 0)
        def _():
            for a in range(n_acc):
                acc_refs[a][...] = parts[a]

        @pl.when(k > 0)
        def _():
            for a in range(n_acc):
                acc_refs[a][...] += parts[a]

        @pl.when(k == nk - 1)
        def _():
            epilogue([acc_refs[a][...] for a in range(n_acc)], ex_refs, out_refs, i, j)

    arrays = [o[0] for o in operands] + [e[0] for e in extras]
    in_specs = [pl.BlockSpec(o[1], o[2]) for o in operands] + [pl.BlockSpec(e[1], e[2]) for e in extras]
    return pl.pallas_call(
        body,
        name=name,
        grid=grid,
        in_specs=in_specs,
        out_specs=[pl.BlockSpec(o[2], o[3]) for o in outs],
        out_shape=[jax.ShapeDtypeStruct(o[0], o[1]) for o in outs],
        scratch_shapes=[pltpu.VMEM(s, jnp.float32) for s in acc_shapes] if nk > 1 else [],
        compiler_params=_params(3),
    )(*arrays)


def _store_epilogue(accs, ex_refs, out_refs, i, j):
    for a, o in zip(accs, out_refs):
        o[...] = a.astype(o.dtype)


def _matmul_nn(name, a, b, out_dtype, tm=1024, tn=512):
    M, K = a.shape
    N = b.shape[1]
    tm, tn = _tile(M, tm, 16), _tile(N, tn, LANES)
    return _fused_matmul(
        name, (M // tm, N // tn, 1),
        [(a, (tm, K), lambda i, j, k: (i, 0)), (b, (K, tn), lambda i, j, k: (0, j))],
        [(0, 1, 0, NN)], [(tm, tn)], [],
        [((M, N), out_dtype, (tm, tn), lambda i, j, k: (i, j))], _store_epilogue)[0]


def _matmul_nn_cols(name, a, b3, out_dtype, tm=1024):
    M, K = a.shape
    J, _, Nc = b3.shape
    tm = _tile(M, tm, 16)
    return _fused_matmul(
        name, (M // tm, J, 1),
        [(a, (tm, K), lambda i, j, k: (i, 0)), (b3, (None, K, Nc), lambda i, j, k: (j, 0, 0))],
        [(0, 1, 0, NN)], [(tm, Nc)], [],
        [((M, J * Nc), out_dtype, (tm, Nc), lambda i, j, k: (i, j))], _store_epilogue)[0]


def _matmul_nt(name, a_list, b_list, out_dtype, tm=1024, tn=512):
    M, K = a_list[0].shape
    N = b_list[0].shape[0]
    tm, tn = _tile(M, tm, 16), _tile(N, tn, LANES)
    n = len(a_list)
    operands = [(a, (tm, K), lambda i, j, k: (i, 0)) for a in a_list]
    operands += [(b, (tn, K), lambda i, j, k: (j, 0)) for b in b_list]
    return _fused_matmul(
        name, (M // tm, N // tn, 1), operands,
        [(p, n + p, 0, NT) for p in range(n)], [(tm, tn)], [],
        [((M, N), out_dtype, (tm, tn), lambda i, j, k: (i, j))], _store_epilogue)[0]


def _matmul_nt_cols(name, a_list, b3_list, out_dtype, tm=1024, tn=512):
    M = a_list[0].shape[0]
    J, N, Kc = b3_list[0].shape
    tm, tn = _tile(M, tm, 16), _tile(N, tn, LANES)
    n = len(a_list)
    operands = [(a, (tm, Kc), lambda i, j, k: (i, k)) for a in a_list]
    operands += [(b, (None, tn, Kc), lambda i, j, k: (k, j, 0)) for b in b3_list]
    return _fused_matmul(
        name, (M // tm, N // tn, J), operands,
        [(p, n + p, 0, NT) for p in range(n)], [(tm, tn)], [],
        [((M, N), out_dtype, (tm, tn), lambda i, j, k: (i, j))], _store_epilogue)[0]


def _matmul_tn(name, a, b_list, out_dtype, tka=1024, tn=1024, tkm=1024):
    M, Ka = a.shape
    N = b_list[0].shape[1]
    tka, tn, tkm = _tile(Ka, tka, LANES), _tile(N, tn, LANES), _tile(M, tkm, 16)
    n = len(b_list)
    operands = [(a, (tkm, tka), lambda i, j, k: (k, i))]
    operands += [(b, (tkm, tn), lambda i, j, k: (k, j)) for b in b_list]
    return _fused_matmul(
        name, (Ka // tka, N // tn, M // tkm), operands,
        [(0, 1 + p, p, TN) for p in range(n)], [(tka, tn)] * n, [],
        [((Ka, N), out_dtype, (tka, tn), lambda i, j, k: (i, j))] * n, _store_epilogue)


def _matmul_tn_cols(name, a, b_list, n_shards, out_dtype, tka=512, tkm=1024):
    M, Ka = a.shape
    Nc = b_list[0].shape[1] // n_shards
    tka, tkm = _tile(Ka, tka, LANES), _tile(M, tkm, 16)
    n = len(b_list)
    operands = [(a, (tkm, tka), lambda i, j, k: (k, i))]
    operands += [(b, (tkm, Nc), lambda i, j, k: (k, j)) for b in b_list]
    return _fused_matmul(
        name, (Ka // tka, n_shards, M // tkm), operands,
        [(0, 1 + p, p, TN) for p in range(n)], [(tka, Nc)] * n, [],
        [((n_shards, Ka, Nc), out_dtype, (None, tka, Nc), lambda i, j, k: (j, i, 0))] * n, _store_epilogue)


def _rmsnorm(name, x, g_row, out_dtype, tm=512):
    S, D = x.shape
    tm = _tile(S, tm, 16)

    def body(x_ref, g_ref, o_ref):
        xv = x_ref[...]
        o_ref[...] = (xv * _rsqrt_mean_sq(xv) * g_ref[...]).astype(o_ref.dtype)

    return pl.pallas_call(
        body, name=name, grid=(S // tm,),
        in_specs=[pl.BlockSpec((tm, D), lambda i: (i, 0)), pl.BlockSpec((1, D), lambda i: (0, 0))],
        out_specs=pl.BlockSpec((tm, D), lambda i: (i, 0)),
        out_shape=jax.ShapeDtypeStruct((S, D), out_dtype),
        compiler_params=_params(1),
    )(x, g_row)


def _rmsnorm_bwd_values(dh, xin, g):
    r = _rsqrt_mean_sq(xin)
    xhat = xin * r
    dhg = dh * g
    dx = r * (dhg - xhat * jnp.mean(dhg * xhat, axis=-1, keepdims=True))
    return dx, jnp.sum(dh * xhat, axis=0, keepdims=True)


def _rmsnorm_bwd(name, dh, xin, g_row, dres, low_dtype, tm=256):
    S, D = xin.shape
    tm = _tile(S, tm, 16)
    has_dx = dres is not None
    has_low = has_dx and low_dtype is not None

    def body(*refs):
        dh_ref, x_ref, g_ref = refs[:3]
        rest = refs[3:]
        dx, dg = _rmsnorm_bwd_values(dh_ref[...], x_ref[...], g_ref[...])
        if has_dx:
            dres_ref, rest = rest[0], rest[1:]
            dx = dx + dres_ref[...]
            rest[0][...] = dx
            if has_low:
                rest[1][...] = dx.astype(low_dtype)
        dg_ref = rest[-1]

        @pl.when(pl.program_id(0) == 0)
        def _():
            dg_ref[...] = jnp.zeros_like(dg_ref)

        dg_ref[...] += dg

    row = pl.BlockSpec((tm, D), lambda i: (i, 0))
    vec = pl.BlockSpec((1, D), lambda i: (0, 0))
    ins, in_specs = [dh, xin, g_row], [row, row, vec]
    out_shape, out_specs = [], []
    if has_dx:
        ins.append(dres)
        in_specs.append(row)
        out_shape.append(jax.ShapeDtypeStruct((S, D), jnp.float32))
        out_specs.append(row)
        if has_low:
            out_shape.append(jax.ShapeDtypeStruct((S, D), low_dtype))
            out_specs.append(row)
    out_shape.append(jax.ShapeDtypeStruct((1, D), jnp.float32))
    out_specs.append(vec)
    return pl.pallas_call(
        body, name=name, grid=(S // tm,), in_specs=in_specs, out_specs=out_specs, out_shape=out_shape,
        compiler_params=_params(1),
    )(*ins)


def _chunk_mask(blk):
    t = lax.broadcasted_iota(jnp.int32, (blk, blk), 0)
    s = lax.broadcasted_iota(jnp.int32, (blk, blk), 1)
    return (s // CHUNK) <= (t // CHUNK)


def _pool_inputs(a, halo, row0, tm, g, pg):
    cols = slice(g * pg, (g + 1) * pg)
    w = POOL_WINDOWS[g]
    s = jnp.concatenate([halo[:, cols], a[:, cols]], axis=0)
    span = 1
    while span < w:
        s = s + pltpu.roll(s, span, axis=0)
        span *= 2
    t = lax.broadcasted_iota(jnp.int32, (tm, 1), 0) + row0
    cnt = jnp.minimum(t + 1, w).astype(jnp.float32)
    return s[POOL_HALO:] / cnt - a[:, cols]


def _mixer_specs(S, tm, DP, DS):
    per_halo = tm // POOL_HALO
    n_halo = S // POOL_HALO
    a_spec = pl.BlockSpec((tm, DP), lambda i: (i, 0))
    prev_spec = pl.BlockSpec((POOL_HALO, DP), lambda i: (jnp.maximum(i * per_halo - 1, 0), 0))
    next_spec = pl.BlockSpec((POOL_HALO, DP), lambda i: (jnp.minimum((i + 1) * per_halo, n_halo - 1), 0))
    u_spec = pl.BlockSpec((tm, DS), lambda i: (i, 1))
    v_spec = pl.BlockSpec((tm, DS), lambda i: (i, 2))
    return a_spec, prev_spec, next_spec, u_spec, v_spec


def _whole(arr):
    nd = arr.ndim
    return pl.BlockSpec(arr.shape, lambda i: (0,) * nd)


def _mixer_fwd(proj, pw, ps_row, gs_row, ws, b_full, tm=512):
    S = proj.shape[0]
    G, PG, _ = pw.shape
    DP = G * PG
    H, BLK, _ = ws.shape
    DS = gs_row.shape[1]
    HD = DS // H
    assert DP == DS and proj.shape[1] == DP + 2 * DS
    tm = _tile(S, tm, BLK)

    def body(a_ref, prev_ref, u_ref, v_ref, pw_ref, ps_ref, gs_ref, ws_ref, b_ref, z_ref):
        i = pl.program_id(0)
        a = a_ref[...]
        halo = jnp.where(i > 0, prev_ref[...], 0.0)
        for g in range(G):
            cols = slice(g * PG, (g + 1) * PG)
            p = _pool_inputs(a, halo, i * tm, tm, g, PG)
            y = jnp.dot(p.astype(MXU_DTYPE), pw_ref[g], preferred_element_type=jnp.float32) * ps_ref[:, cols]
            z_ref[:, cols] = y.astype(z_ref.dtype)
        v = v_ref[...]
        vn = (v * _rsqrt_mean_sq(v) * gs_ref[...]).astype(MXU_DTYPE)
        u = u_ref[...]
        mask = _chunk_mask(BLK)
        for h in range(H):
            wm = jnp.where(mask, ws_ref[h], 0.0).astype(MXU_DTYPE)
            bias = b_ref[h]
            cols = slice(h * HD, (h + 1) * HD)
            for n in range(tm // BLK):
                rows = slice(n * BLK, (n + 1) * BLK)
                mixed = jnp.dot(wm, vn[rows, cols], preferred_element_type=jnp.float32) + bias
                z_ref[rows, DP + h * HD:DP + (h + 1) * HD] = (u[rows, cols] * mixed).astype(z_ref.dtype)

    a_spec, prev_spec, _, u_spec, v_spec = _mixer_specs(S, tm, DP, DS)
    return pl.pallas_call(
        body, name="mixer_fwd", grid=(S // tm,),
        in_specs=[a_spec, prev_spec, u_spec, v_spec, _whole(pw), _whole(ps_row), _whole(gs_row), _whole(ws), _whole(b_full)],
        out_specs=pl.BlockSpec((tm, DP + DS), lambda i: (i, 0)),
        out_shape=jax.ShapeDtypeStruct((S, DP + DS), MXU_DTYPE),
        compiler_params=_params(1),
    )(proj, proj, proj, proj, pw, ps_row, gs_row, ws, b_full)


def _mixer_bwd(proj, dz, pw, ps_row, gs_row, ws, b_full, tm=256):
    S = proj.shape[0]
    G, PG, _ = pw.shape
    DP = G * PG
    H, BLK, _ = ws.shape
    DS = gs_row.shape[1]
    HD = DS // H
    tm = _tile(S, tm, BLK)
    nb = S // tm
    ext = tm + POOL_HALO

    def body(a_ref, prev_ref, u_ref, v_ref, dzp_ref, dzp_next_ref, dzs_ref, pw_ref, ps_ref, gs_ref, ws_ref, b_ref,
             dproj_ref, dpw_ref, dps_ref, dgs_ref, dws_ref, db_ref, dvn_ref):
        i = pl.program_id(0)

        @pl.when(i == 0)
        def _():
            for r in (dpw_ref, dps_ref, dgs_ref, dws_ref, db_ref):
                r[...] = jnp.zeros_like(r)

        a = a_ref[...]
        halo = jnp.where(i > 0, prev_ref[...], 0.0)
        dyp = dzp_ref[...]
        dyp_next = jnp.where(i < nb - 1, dzp_next_ref[...], 0.0)
        dq_ext = jnp.concatenate([dyp, dyp_next], axis=0) * ps_ref[...]
        t_ext = lax.broadcasted_iota(jnp.int32, (ext, 1), 0) + i * tm
        for g in range(G):
            cols = slice(g * PG, (g + 1) * PG)
            w = POOL_WINDOWS[g]
            pb = _pool_inputs(a, halo, i * tm, tm, g, PG).astype(MXU_DTYPE)
            y_pre = jnp.dot(pb, pw_ref[g], preferred_element_type=jnp.float32)
            dps_ref[:, cols] += jnp.sum(dyp[:, cols] * y_pre, axis=0, keepdims=True)
            dqb = dq_ext[:, cols].astype(MXU_DTYPE)
            dpw_ref[g] += lax.dot_general(pb, dqb[:tm], TN, preferred_element_type=jnp.float32)
            dp = lax.dot_general(dqb, pw_ref[g], NT, preferred_element_type=jnp.float32)
            f = dp / jnp.minimum(t_ext + 1, w).astype(jnp.float32)
            span = 1
            while span < w:
                f = f + pltpu.roll(f, ext - span, axis=0)
                span *= 2
            dproj_ref[:, cols] = (f[:tm] - dp[:tm]).astype(dproj_ref.dtype)

        v = v_ref[...]
        rv = _rsqrt_mean_sq(v)
        vhat = v * rv
        gs = gs_ref[...]
        vnb = (vhat * gs).astype(MXU_DTYPE)
        u = u_ref[...]
        dys = dzs_ref[...]
        mask = _chunk_mask(BLK)
        for h in range(H):
            wm = jnp.where(mask, ws_ref[h], 0.0).astype(MXU_DTYPE)
            bias = b_ref[h]
            cols = slice(h * HD, (h + 1) * HD)
            dm_sum = jnp.zeros((BLK, HD), jnp.float32)
            dws_sum = jnp.zeros((BLK, BLK), jnp.float32)
            for n in range(tm // BLK):
                rows = slice(n * BLK, (n + 1) * BLK)
                vblk = vnb[rows, cols]
                mixed = jnp.dot(wm, vblk, preferred_element_type=jnp.float32) + bias
                dy = dys[rows, cols]
                dproj_ref[rows, DP + h * HD:DP + (h + 1) * HD] = (dy * mixed).astype(dproj_ref.dtype)
                dm = dy * u[rows, cols]
                dm_sum = dm_sum + dm
                dmb = dm.astype(MXU_DTYPE)
                dws_sum = dws_sum + lax.dot_general(dmb, vblk, NT, preferred_element_type=jnp.float32)
                dvn_ref[rows, cols] = lax.dot_general(wm, dmb, TN, preferred_element_type=jnp.float32)
            dws_ref[h] += jnp.where(mask, dws_sum, 0.0)
            db_ref[h:h + 1, :] += jnp.sum(dm_sum.T, axis=0, keepdims=True)
        dvn = dvn_ref[...]
        dvg = dvn * gs
        dv = rv * (dvg - vhat * jnp.mean(dvg * vhat, axis=-1, keepdims=True))
        dgs_ref[...] += jnp.sum(dvn * vhat, axis=0, keepdims=True)
        dproj_ref[:, DP + DS:] = dv.astype(dproj_ref.dtype)

    a_spec, prev_spec, next_spec, u_spec, v_spec = _mixer_specs(S, tm, DP, DS)
    dzp_spec = pl.BlockSpec((tm, DP), lambda i: (i, 0))
    dzs_spec = pl.BlockSpec((tm, DS), lambda i: (i, 1))
    small = [pw, ps_row, gs_row, ws, b_full]
    out_shape = [
        jax.ShapeDtypeStruct((S, DP + 2 * DS), MXU_DTYPE),
        jax.ShapeDtypeStruct(pw.shape, jnp.float32),
        jax.ShapeDtypeStruct(ps_row.shape, jnp.float32),
        jax.ShapeDtypeStruct(gs_row.shape, jnp.float32),
        jax.ShapeDtypeStruct(ws.shape, jnp.float32),
        jax.ShapeDtypeStruct((H, BLK), jnp.float32),
    ]
    out_specs = [pl.BlockSpec((tm, DP + 2 * DS), lambda i: (i, 0))] + [_whole(s) for s in out_shape[1:]]
    return pl.pallas_call(
        body, name="mixer_bwd", grid=(nb,),
        in_specs=[a_spec, prev_spec, u_spec, v_spec, dzp_spec, next_spec, dzs_spec] + [_whole(s) for s in small],
        out_specs=out_specs, out_shape=out_shape,
        scratch_shapes=[pltpu.VMEM((tm, DS), jnp.float32)],
        compiler_params=_params(1),
    )(proj, proj, proj, proj, dz, dz, dz, *small)


def _softmax_rows(q, k, scale):
    s = lax.dot_general(q, k, NT, preferred_element_type=jnp.float32) * scale
    e = jnp.exp(s - jnp.max(s, axis=-1, keepdims=True))
    return e / jnp.sum(e, axis=-1, keepdims=True)


def _attn_fwd(q, k, v, tm=512):
    S, D = q.shape
    M = k.shape[0]
    HD = D // N_XATTN_HEADS
    scale = HD ** -0.5
    tm = _tile(S, tm, 16)

    def body(q_ref, k_ref, v_ref, o_ref):
        for h in range(N_XATTN_HEADS):
            cols = slice(h * HD, (h + 1) * HD)
            p = _softmax_rows(q_ref[:, cols], k_ref[:, cols], scale)
            o = jnp.dot(p.astype(MXU_DTYPE), v_ref[:, cols], preferred_element_type=jnp.float32)
            o_ref[:, cols] = o.astype(o_ref.dtype)

    row = pl.BlockSpec((tm, D), lambda i: (i, 0))
    mem = pl.BlockSpec((M, D), lambda i: (0, 0))
    return pl.pallas_call(
        body, name="attn_fwd", grid=(S // tm,), in_specs=[row, mem, mem], out_specs=row,
        out_shape=jax.ShapeDtypeStruct((S, D), MXU_DTYPE), compiler_params=_params(1),
    )(q, k, v)


def _attn_bwd(q, k, v, do, tm=512):
    S, D = q.shape
    M = k.shape[0]
    HD = D // N_XATTN_HEADS
    scale = HD ** -0.5
    tm = _tile(S, tm, 16)

    def body(q_ref, k_ref, v_ref, do_ref, dq_ref, dk_ref, dv_ref):
        @pl.when(pl.program_id(0) == 0)
        def _():
            dk_ref[...] = jnp.zeros_like(dk_ref)
            dv_ref[...] = jnp.zeros_like(dv_ref)

        for h in range(N_XATTN_HEADS):
            cols = slice(h * HD, (h + 1) * HD)
            qh, kh, vh, doh = q_ref[:, cols], k_ref[:, cols], v_ref[:, cols], do_ref[:, cols]
            p = _softmax_rows(qh, kh, scale)
            dp = lax.dot_general(doh, vh, NT, preferred_element_type=jnp.float32)
            dv_ref[:, cols] += lax.dot_general(p.astype(MXU_DTYPE), doh, TN, preferred_element_type=jnp.float32)
            ds = (p * (dp - jnp.sum(dp * p, axis=-1, keepdims=True)) * scale).astype(MXU_DTYPE)
            dq_ref[:, cols] = jnp.dot(ds, kh, preferred_element_type=jnp.float32).astype(dq_ref.dtype)
            dk_ref[:, cols] += lax.dot_general(ds, qh, TN, preferred_element_type=jnp.float32)

    row = pl.BlockSpec((tm, D), lambda i: (i, 0))
    mem = pl.BlockSpec((M, D), lambda i: (0, 0))
    return pl.pallas_call(
        body, name="attn_bwd", grid=(S // tm,), in_specs=[row, mem, mem, row], out_specs=[row, mem, mem],
        out_shape=[jax.ShapeDtypeStruct((S, D), MXU_DTYPE), jax.ShapeDtypeStruct((M, D), jnp.float32),
                   jax.ShapeDtypeStruct((M, D), jnp.float32)],
        compiler_params=_params(1),
    )(q, k, v, do)


def _proj_residual_norm(name, a, w2d, resid, g_row, tm=256):
    M, K = a.shape
    N = w2d.shape[1]
    tm = _tile(M, tm, 16)

    def epilogue(accs, ex_refs, out_refs, i, j):
        x_new = ex_refs[0][...] + accs[0]
        out_refs[0][...] = x_new
        out_refs[1][...] = (x_new * _rsqrt_mean_sq(x_new) * ex_refs[1][...]).astype(MXU_DTYPE)

    row = lambda i, j, k: (i, 0)
    return _fused_matmul(
        name, (M // tm, 1, 1),
        [(a, (tm, K), row), (w2d, (K, N), lambda i, j, k: (0, 0))],
        [(0, 1, 0, NN)], [(tm, N)],
        [(resid, (tm, N), row), (g_row, (1, N), lambda i, j, k: (0, 0))],
        [((M, N), jnp.float32, (tm, N), row), ((M, N), MXU_DTYPE, (tm, N), row)], epilogue)


def _ffn_gate_up(h, wg3, wu3, tm=512):
    M, K = h.shape
    J, _, Nc = wg3.shape
    tm = _tile(M, tm, 16)

    def epilogue(accs, ex_refs, out_refs, i, j):
        gate, up = accs
        out_refs[0][...] = gate
        out_refs[1][...] = up
        out_refs[2][...] = (gate * _sigmoid(gate) * up).astype(MXU_DTYPE)

    col = lambda i, j, k: (i, j)
    wspec = lambda i, j, k: (j, 0, 0)
    return _fused_matmul(
        "ffn_gate_up", (M // tm, J, 1),
        [(h, (tm, K), lambda i, j, k: (i, 0)), (wg3, (None, K, Nc), wspec), (wu3, (None, K, Nc), wspec)],
        [(0, 1, 0, NN), (0, 2, 1, NN)], [(tm, Nc)] * 2, [],
        [((M, J * Nc), jnp.float32, (tm, Nc), col), ((M, J * Nc), jnp.float32, (tm, Nc), col),
         ((M, J * Nc), MXU_DTYPE, (tm, Nc), col)], epilogue)


def _ffn_down_loss(act, wd2d, x2, target, g_row, n_k, tm=256):
    M, F = act.shape
    D = wd2d.shape[1]
    tm = _tile(M, tm, 16)
    tk = F // n_k

    def epilogue(accs, ex_refs, out_refs, i, j):
        x3 = ex_refs[0][...] + accs[0]
        g = ex_refs[2][...]
        r = _rsqrt_mean_sq(x3)
        xhat = x3 * r
        diff = xhat * g - ex_refs[1][...]
        dy = diff / D
        dyg = dy * g
        dx = r * (dyg - xhat * jnp.mean(dyg * xhat, axis=-1, keepdims=True))
        out_refs[0][...] = dx
        out_refs[1][...] = dx.astype(MXU_DTYPE)

        @pl.when(i == 0)
        def _():
            out_refs[2][...] = jnp.zeros_like(out_refs[2])
            out_refs[3][...] = jnp.zeros_like(out_refs[3])

        out_refs[2][...] += 0.5 * _sum_all(jnp.mean(diff * diff, axis=-1, keepdims=True))
        out_refs[3][...] += jnp.sum(dy * xhat, axis=0, keepdims=True)

    row = lambda i, j, k: (i, 0)
    const = lambda i, j, k: (0, 0)
    return _fused_matmul(
        "ffn_down_loss", (M // tm, 1, n_k),
        [(act, (tm, tk), lambda i, j, k: (i, k)), (wd2d, (tk, D), lambda i, j, k: (k, 0))],
        [(0, 1, 0, NN)], [(tm, D)],
        [(x2, (tm, D), row), (target, (tm, D), row), (g_row, (1, D), const)],
        [((M, D), jnp.float32, (tm, D), row), ((M, D), MXU_DTYPE, (tm, D), row),
         ((1, 1), jnp.float32, (1, 1), const), ((1, D), jnp.float32, (1, D), const)], epilogue)


def _ffn_down_bwd(dx3_low, wd2d, gate, up, n_shards, tm=512):
    M, D = dx3_low.shape
    F = wd2d.shape[0]
    tn = F // n_shards
    tm = _tile(M, tm, 16)

    def epilogue(accs, ex_refs, out_refs, i, j):
        dact = accs[0]
        gate_v, up_v = ex_refs[0][...], ex_refs[1][...]
        sg = _sigmoid(gate_v)
        out_refs[0][...] = (dact * up_v * (sg * (1.0 + gate_v * (1.0 - sg)))).astype(MXU_DTYPE)
        out_refs[1][...] = (dact * (gate_v * sg)).astype(MXU_DTYPE)

    col = lambda i, j, k: (i, j)
    return _fused_matmul(
        "ffn_down_bwd", (M // tm, n_shards, 1),
        [(dx3_low, (tm, D), lambda i, j, k: (i, 0)), (wd2d, (tn, D), lambda i, j, k: (j, 0))],
        [(0, 1, 0, NT)], [(tm, tn)],
        [(gate, (tm, tn), col), (up, (tm, tn), col)],
        [((M, F), MXU_DTYPE, (tm, tn), col)] * 2, epilogue)


def _row_tile(R, C, target_bytes=2 ** 21):
    return _tile(R, max(16, target_bytes // (4 * C)), 16)


def _cast(name, x2d, dtype):
    R, C = x2d.shape
    tr = _row_tile(R, C)

    def body(x_ref, o_ref):
        o_ref[...] = x_ref[...].astype(o_ref.dtype)

    blk = pl.BlockSpec((tr, C), lambda i: (i, 0))
    return pl.pallas_call(body, name=name, grid=(R // tr,), in_specs=[blk], out_specs=blk,
                          out_shape=jax.ShapeDtypeStruct((R, C), dtype), compiler_params=_params(1))(x2d)


def _add2(name, a2d, b2d, dtype):
    R, C = a2d.shape
    tr = _row_tile(R, C)

    def body(a_ref, b_ref, o_ref):
        o_ref[...] = (a_ref[...].astype(jnp.float32) + b_ref[...].astype(jnp.float32)).astype(o_ref.dtype)

    blk = pl.BlockSpec((tr, C), lambda i: (i, 0))
    return pl.pallas_call(body, name=name, grid=(R // tr,), in_specs=[blk, blk], out_specs=blk,
                          out_shape=jax.ShapeDtypeStruct((R, C), dtype), compiler_params=_params(1))(a2d, b2d)


def _sum_leading(name, x3d, dtype):
    J, R, C = x3d.shape
    tr = _row_tile(R, C * J)

    def body(x_ref, o_ref):
        acc = x_ref[0].astype(jnp.float32)
        for j in range(1, J):
            acc = acc + x_ref[j].astype(jnp.float32)
        o_ref[...] = acc.astype(o_ref.dtype)

    return pl.pallas_call(
        body, name=name, grid=(R // tr,), in_specs=[pl.BlockSpec((J, tr, C), lambda i: (0, i, 0))],
        out_specs=pl.BlockSpec((tr, C), lambda i: (i, 0)), out_shape=jax.ShapeDtypeStruct((R, C), dtype),
        compiler_params=_params(1))(x3d)


def _adamw_values(w, g, m, v):
    m = ADAM_B1 * m + (1.0 - ADAM_B1) * g
    v = ADAM_B2 * v + (1.0 - ADAM_B2) * (g * g)
    m_hat = m / (1.0 - ADAM_B1 ** ADAM_STEP)
    v_hat = v / (1.0 - ADAM_B2 ** ADAM_STEP)
    delta = -ADAM_LR * (m_hat / (jnp.sqrt(v_hat) + ADAM_EPS) + ADAM_WD * w)
    return delta, m, v


def _adamw(name, w, g, m, v):
    R, C = w.shape
    tr = _row_tile(R, C, 2 ** 20)

    def body(w_ref, g_ref, m_ref, v_ref, d_ref, mo_ref, vo_ref):
        d_ref[...], mo_ref[...], vo_ref[...] = _adamw_values(w_ref[...], g_ref[...], m_ref[...], v_ref[...])

    blk = pl.BlockSpec((tr, C), lambda i: (i, 0))
    return pl.pallas_call(
        body, name=name, grid=(R // tr,), in_specs=[blk] * 4, out_specs=[blk] * 3,
        out_shape=[jax.ShapeDtypeStruct((R, C), jnp.float32)] * 3, compiler_params=_params(1))(w, g, m, v)


def _adamw_small(g_packed, ws, ms, vs, row_counts):
    n = len(ws)

    def body(*refs):
        g_ref = refs[0]
        w_refs, m_refs, v_refs = refs[1:1 + n], refs[1 + n:1 + 2 * n], refs[1 + 2 * n:1 + 3 * n]
        outs = refs[1 + 3 * n:]
        off = 0
        for t in range(n):
            g = g_ref[off:off + row_counts[t], :]
            off += row_counts[t]
            delta, m_new, v_new = _adamw_values(w_refs[t][...], g, m_refs[t][...], v_refs[t][...])
            outs[4 * t][...] = g
            outs[4 * t + 1][...] = delta
            outs[4 * t + 2][...] = m_new
            outs[4 * t + 3][...] = v_new

    out_shape = []
    for t in range(n):
        out_shape += [jax.ShapeDtypeStruct((row_counts[t], LANES), jnp.float32)] * 4
    return pl.pallas_call(
        body, name="adamw_small", in_specs=[VMEM_WHOLE] * (1 + 3 * n), out_specs=[VMEM_WHOLE] * (4 * n),
        out_shape=out_shape, compiler_params=pltpu.CompilerParams(vmem_limit_bytes=VMEM_LIMIT_BYTES),
    )(g_packed, *ws, *ms, *vs)


def _position():
    x, y, c = lax.axis_index("x"), lax.axis_index("y"), lax.axis_index("c")
    other_chips = [(1 - x, y), (x, 1 - y), (1 - x, 1 - y)]
    return x, y, c, other_chips


def _half(rows, which):
    return pl.ds(which * (rows // 2), rows // 2)


def _chunk_index(shape, dtype):
    rows = shape[-2]
    total = math.prod(shape) * jnp.dtype(dtype).itemsize
    n = max(1, min(rows // COPY_ROW_ALIGN, total // COPY_CHUNK_BYTES))
    step = -(-rows // n)
    step = -(-step // COPY_ROW_ALIGN) * COPY_ROW_ALIGN
    lead = (slice(None),) * (len(shape) - 2)
    return [lead + (pl.ds(s, min(step, rows - s)),) for s in range(0, rows, step)]


def _start_in_chunks(make, src, dst):
    for idx in _chunk_index(src.shape, src.dtype):
        make(src.at[idx], dst.at[idx]).start()
    return make(src, dst)


def _gather_weights(shards):
    T = len(shards)

    def body(*refs):
        ins, outs = refs[:T], refs[T:2 * T]
        send_sems, recv_sems, local_sems = refs[2 * T:]
        x, y, c, other_chips = _position()
        mine = 2 * x + y
        sibling = (x, y, 1 - c)

        def block(t, chip, which):
            return outs[t].at[chip, _half(ins[t].shape[0], which)]

        def copier(t, k, to):
            return lambda src, dst: pltpu.make_async_remote_copy(
                src_ref=src, dst_ref=dst, send_sem=send_sems.at[t, k], recv_sem=recv_sems.at[t, k],
                device_id=to, device_id_type=MESH)

        def local_copier(t):
            return lambda src, dst: pltpu.make_async_copy(src, dst, local_sems.at[t])

        local = [_start_in_chunks(local_copier(t), ins[t], outs[t].at[mine]) for t in range(T)]
        started = []
        for t in range(T):
            for k, (cx, cy) in enumerate(other_chips):
                started.append(_start_in_chunks(
                    copier(t, k, (cx, cy, c)), ins[t].at[_half(ins[t].shape[0], c)], block(t, mine, c)))
        for k, (cx, cy) in enumerate(other_chips):
            for t in range(T):
                landed = block(t, 2 * cx + cy, c)
                copier(t, k, (cx, cy, c))(landed, landed).wait_recv()
                started.append(_start_in_chunks(copier(t, 3 + k, sibling), landed, landed))
        for k, (cx, cy) in enumerate(other_chips):
            for t in range(T):
                arriving = block(t, 2 * cx + cy, 1 - c)
                copier(t, 3 + k, sibling)(arriving, arriving).wait_recv()
        for cp in started:
            cp.wait_send()
        for cp in local:
            cp.wait()

    return pl.pallas_call(
        body, name="gather_weights", in_specs=[ANY] * T, out_specs=[ANY] * T,
        out_shape=[jax.ShapeDtypeStruct((N_CHIPS,) + s.shape, s.dtype) for s in shards],
        scratch_shapes=[pltpu.SemaphoreType.DMA((T, 6)), pltpu.SemaphoreType.DMA((T, 6)), pltpu.SemaphoreType.DMA((T,))],
    )(*shards)


def _swap_halves(grads):
    T = len(grads)

    def body(*refs):
        ins, kept, got = refs[:T], refs[T:2 * T], refs[2 * T:3 * T]
        send_sems, recv_sems, local_sems = refs[3 * T:]
        x, y, c, _ = _position()
        local, remote = [], []
        for t in range(T):
            rows = ins[t].shape[1]
            local.append(_start_in_chunks(
                lambda src, dst, t=t: pltpu.make_async_copy(src, dst, local_sems.at[t]),
                ins[t].at[:, _half(rows, c)], kept[t]))
            remote.append(_start_in_chunks(
                lambda src, dst, t=t: pltpu.make_async_remote_copy(
                    src_ref=src, dst_ref=dst, send_sem=send_sems.at[t], recv_sem=recv_sems.at[t],
                    device_id=(x, y, 1 - c), device_id_type=MESH),
                ins[t].at[:, _half(rows, 1 - c)], got[t]))
        for cp in remote:
            cp.wait()
        for cp in local:
            cp.wait()

    half_shapes = [jax.ShapeDtypeStruct((g.shape[0], g.shape[1] // 2, g.shape[2]), g.dtype) for g in grads]
    res = pl.pallas_call(
        body, name="swap_grad_halves", in_specs=[ANY] * T, out_specs=[ANY] * (2 * T), out_shape=half_shapes * 2,
        scratch_shapes=[pltpu.SemaphoreType.DMA((T,)), pltpu.SemaphoreType.DMA((T,)), pltpu.SemaphoreType.DMA((T,))],
    )(*grads)
    return res[:T], res[T:]


def _scatter_to_owners(chip_sums):
    T = len(chip_sums)

    def body(*refs):
        ins, outs = refs[:T], refs[T:2 * T]
        send_sems, recv_sems, local_sems = refs[2 * T:]
        x, y, c, other_chips = _position()
        mine = 2 * x + y

        def copier(t, k, to):
            return lambda src, dst: pltpu.make_async_remote_copy(
                src_ref=src, dst_ref=dst, send_sem=send_sems.at[t, k], recv_sem=recv_sems.at[t, k],
                device_id=to, device_id_type=MESH)

        local = [_start_in_chunks(lambda src, dst, t=t: pltpu.make_async_copy(src, dst, local_sems.at[t]),
                                  ins[t].at[mine], outs[t].at[mine]) for t in range(T)]
        remote = []
        for t in range(T):
            for k, (cx, cy) in enumerate(other_chips):
                remote.append(_start_in_chunks(copier(t, k, (cx, cy, c)), ins[t].at[2 * cx + cy], outs[t].at[mine]))
        for t in range(T):
            for k, (cx, cy) in enumerate(other_chips):
                arriving = outs[t].at[2 * cx + cy]
                copier(t, k, (cx, cy, c))(arriving, arriving).wait_recv()
        for cp in remote:
            cp.wait_send()
        for cp in local:
            cp.wait()

    return pl.pallas_call(
        body, name="scatter_grads_to_owners", in_specs=[ANY] * T, out_specs=[ANY] * T,
        out_shape=[jax.ShapeDtypeStruct(s.shape, s.dtype) for s in chip_sums],
        scratch_shapes=[pltpu.SemaphoreType.DMA((T, 3)), pltpu.SemaphoreType.DMA((T, 3)), pltpu.SemaphoreType.DMA((T,))],
    )(*chip_sums)


def _join_halves(halves):
    T = len(halves)

    def body(*refs):
        ins, outs = refs[:T], refs[T:2 * T]
        send_sems, recv_sems, local_sems = refs[2 * T:]
        x, y, c, _ = _position()

        def copier(t):
            return lambda src, dst: pltpu.make_async_remote_copy(
                src_ref=src, dst_ref=dst, send_sem=send_sems.at[t], recv_sem=recv_sems.at[t],
                device_id=(x, y, 1 - c), device_id_type=MESH)

        local, remote = [], []
        for t in range(T):
            rows = 2 * ins[t].shape[0]
            local.append(_start_in_chunks(
                lambda src, dst, t=t: pltpu.make_async_copy(src, dst, local_sems.at[t]), ins[t], outs[t].at[_half(rows, c)]))
            remote.append(_start_in_chunks(copier(t), ins[t], outs[t].at[_half(rows, c)]))
        for t in range(T):
            arriving = outs[t].at[_half(2 * ins[t].shape[0], 1 - c)]
            copier(t)(arriving, arriving).wait_recv()
        for cp in remote:
            cp.wait_send()
        for cp in local:
            cp.wait()

    return pl.pallas_call(
        body, name="join_grad_halves", in_specs=[ANY] * T, out_specs=[ANY] * T,
        out_shape=[jax.ShapeDtypeStruct((2 * h.shape[0], h.shape[1]), h.dtype) for h in halves],
        scratch_shapes=[pltpu.SemaphoreType.DMA((T,)), pltpu.SemaphoreType.DMA((T,)), pltpu.SemaphoreType.DMA((T,))],
    )(*halves)


def _allreduce_small(part):
    rows = part.shape[0]

    def body(x_ref, out_ref, gath, send_sems, recv_sems):
        x, y, c, other_chips = _position()
        sibling = (x, y, 1 - c)

        def slot(px, py, pc):
            return gath.at[4 * px + 2 * py + pc]

        def copy(k, owner, to, src=None):
            return pltpu.make_async_remote_copy(
                src_ref=slot(*owner) if src is None else src, dst_ref=slot(*owner), send_sem=send_sems.at[k],
                recv_sem=recv_sems.at[k], device_id=to, device_id_type=MESH)

        gath[4 * x + 2 * y + c] = x_ref[...]
        first = [copy(0, (x, y, c), sibling, src=x_ref)]
        first += [copy(1 + k, (x, y, c), (cx, cy, c), src=x_ref) for k, (cx, cy) in enumerate(other_chips)]
        for cp in first:
            cp.start()
        passed = [copy(4 + k, (cx, cy, c), sibling) for k, (cx, cy) in enumerate(other_chips)]
        for k, (cx, cy) in enumerate(other_chips):
            copy(1 + k, (cx, cy, c), (x, y, c)).wait_recv()
            passed[k].start()
        copy(0, (x, y, 1 - c), (x, y, c)).wait_recv()
        for k, (cx, cy) in enumerate(other_chips):
            copy(4 + k, (cx, cy, 1 - c), (x, y, c)).wait_recv()
        for cp in first + passed:
            cp.wait_send()
        acc = gath[0]
        for d in range(1, N_DEV):
            acc = acc + gath[d]
        out_ref[...] = acc

    return pl.pallas_call(
        body, name="allreduce_small_grads", in_specs=[VMEM_WHOLE], out_specs=VMEM_WHOLE,
        out_shape=jax.ShapeDtypeStruct((rows, LANES), jnp.float32),
        scratch_shapes=[pltpu.VMEM((N_DEV, rows, LANES), jnp.float32), pltpu.SemaphoreType.DMA((7,)),
                        pltpu.SemaphoreType.DMA((7,))],
        compiler_params=pltpu.CompilerParams(vmem_limit_bytes=VMEM_LIMIT_BYTES),
    )(part)


BIG = ("w_in", "pool_w", "w_out", "w_q", "w_k", "w_v", "w_o", "w_gate", "w_up", "w_down")
SMALL = ("norm_mix_g", "pool_scale", "sgu_norm_g", "w_spatial", "b_spatial", "norm_xattn_g", "norm_mem_g",
         "norm_ffn_g", "final_norm_g")
WEIGHTS = ("norm_mix_g", "w_in", "pool_w", "pool_scale", "sgu_norm_g", "w_spatial", "b_spatial", "w_out",
           "norm_xattn_g", "norm_mem_g", "w_q", "w_k", "w_v", "w_o", "norm_ffn_g", "w_gate", "w_up", "w_down",
           "final_norm_g")


def _as2d(a):
    return a.reshape(-1, a.shape[-1])


def _as_lanes(a):
    return a.reshape(-1, LANES)


def kernel(x, mem, norm_mix_g, w_in, pool_w, pool_scale, sgu_norm_g, w_spatial, b_spatial, w_out, norm_xattn_g, norm_mem_g, w_q, w_k, w_v, w_o, norm_ffn_g, w_gate, w_up, w_down, final_norm_g, loss_target, m_norm_mix_g, m_w_in, m_pool_w, m_pool_scale, m_sgu_norm_g, m_w_spatial, m_b_spatial, m_w_out, m_norm_xattn_g, m_norm_mem_g, m_w_q, m_w_k, m_w_v, m_w_o, m_norm_ffn_g, m_w_gate, m_w_up, m_w_down, m_final_norm_g, v_norm_mix_g, v_w_in, v_pool_w, v_pool_scale, v_sgu_norm_g, v_w_spatial, v_b_spatial, v_w_out, v_norm_xattn_g, v_norm_mem_g, v_w_q, v_w_k, v_w_v, v_w_o, v_norm_ffn_g, v_w_gate, v_w_up, v_w_down, v_final_norm_g):
    args = dict(locals())
    weights = {n: args[n] for n in WEIGHTS}
    moments_m = {n: args["m_" + n] for n in WEIGHTS}
    moments_v = {n: args["v_" + n] for n in WEIGHTS}
    low = MXU_DTYPE

    xs = x[0]
    mems = mem[0]
    target = loss_target[0]
    D = xs.shape[1]
    G, PG = pool_w.shape[1], pool_w.shape[3]

    shards = [_cast("cast_" + n, _as2d(weights[n]), WIRE_DTYPE) for n in BIG]
    gathered = dict(zip(BIG, _gather_weights(shards)))
    wg_in = gathered["w_in"]
    pw = gathered["pool_w"].reshape(N_CHIPS, G, PG // N_CHIPS, PG).transpose(1, 0, 2, 3).reshape(G, PG, PG)
    full = {n: gathered[n].reshape(-1, gathered[n].shape[-1]) for n in ("w_out", "w_q", "w_k", "w_v", "w_o", "w_down")}
    wg_gate, wg_up = gathered["w_gate"], gathered["w_up"]

    g_mix, g_xattn, g_mem, g_ffn = norm_mix_g, norm_xattn_g, norm_mem_g, norm_ffn_g
    g_final = final_norm_g.reshape(1, D)
    ws = w_spatial[0]
    b_full = jnp.broadcast_to(b_spatial[0][:, :, None], ws.shape[:2] + (sgu_norm_g.shape[1] // ws.shape[0],))

    h1 = _rmsnorm("norm_mix", xs, g_mix, low)
    proj = _matmul_nn_cols("proj_in", h1, wg_in, jnp.float32)
    z = _mixer_fwd(proj, pw, pool_scale, sgu_norm_g, ws, b_full)
    x1, h2 = _proj_residual_norm("mix_out", z, full["w_out"], xs, g_xattn)
    q = _matmul_nn("xattn_q", h2, full["w_q"], low)
    mn = _rmsnorm("norm_mem", mems, g_mem, low)
    k = _matmul_nn("xattn_k", mn, full["w_k"], low)
    vv = _matmul_nn("xattn_v", mn, full["w_v"], low)
    o = _attn_fwd(q, k, vv)
    x2, h3 = _proj_residual_norm("xattn_out", o, full["w_o"], x1, g_ffn)
    gate, up, act = _ffn_gate_up(h3, wg_gate, wg_up)
    dx3, dx3_low, loss_part, d_final_g = _ffn_down_loss(act, full["w_down"], x2, target, g_final, N_CHIPS)

    dgate, dup = _ffn_down_bwd(dx3_low, full["w_down"], gate, up, N_CHIPS)
    (d_w_down,) = _matmul_tn("grad_w_down", act, [dx3_low], WIRE_DTYPE)
    d_w_gate, d_w_up = _matmul_tn_cols("grad_w_gate_up", h3, [dgate, dup], N_CHIPS, WIRE_DTYPE)
    dh3 = _matmul_nt_cols("ffn_dh", [dgate, dup], [wg_gate, wg_up], jnp.float32)
    dx2, dx2_low, d_ffn_g = _rmsnorm_bwd("norm_ffn_bwd", dh3, x2, g_ffn, dx3, low)

    do = _matmul_nt("xattn_do", [dx2_low], [full["w_o"]], low)
    (d_w_o,) = _matmul_tn("grad_w_o", o, [dx2_low], WIRE_DTYPE)
    dq, dk, dvv = _attn_bwd(q, k, vv, do)
    (d_w_q,) = _matmul_tn("grad_w_q", h2, [dq], WIRE_DTYPE)
    dh2 = _matmul_nt("xattn_dh", [dq], [full["w_q"]], jnp.float32)
    d_w_k, d_w_v = _matmul_tn("grad_w_kv", mn, [dk, dvv], WIRE_DTYPE)
    dmn = _matmul_nt("xattn_dmem", [dk, dvv], [full["w_k"], full["w_v"]], jnp.float32)
    (d_mem_g,) = _rmsnorm_bwd("norm_mem_bwd", dmn, mems, g_mem, None, None)
    dx1, dx1_low, d_xattn_g = _rmsnorm_bwd("norm_xattn_bwd", dh2, x1, g_xattn, dx2, low)

    dz = _matmul_nt("mix_dz", [dx1_low], [full["w_out"]], jnp.float32)
    (d_w_out,) = _matmul_tn("grad_w_out", z, [dx1_low], WIRE_DTYPE)
    dproj, d_pw, d_ps, d_gs, d_ws, d_b = _mixer_bwd(proj, dz, pw, pool_scale, sgu_norm_g, ws, b_full)
    (d_w_in,) = _matmul_tn_cols("grad_w_in", h1, [dproj], N_CHIPS, WIRE_DTYPE)
    dh1 = _matmul_nt_cols("mix_dh", [dproj], [wg_in], jnp.float32)
    grad_x, d_mix_g = _rmsnorm_bwd("norm_mix_bwd", dh1, xs, g_mix, dx1, None)

    d_pool_w = d_pw.reshape(G, N_CHIPS, PG // N_CHIPS, PG).transpose(1, 0, 2, 3).reshape(N_CHIPS, PG, PG).astype(WIRE_DTYPE)
    partial = {
        "w_in": d_w_in, "pool_w": d_pool_w, "w_gate": d_w_gate, "w_up": d_w_up,
        "w_out": d_w_out, "w_q": d_w_q, "w_k": d_w_k, "w_v": d_w_v, "w_o": d_w_o, "w_down": d_w_down,
    }
    partial = [partial[n].reshape((N_CHIPS, -1, partial[n].shape[-1])) for n in BIG]
    kept, received = _swap_halves(partial)
    chip_sums = [
        _add2("chip_sum_" + n, a.reshape(-1, a.shape[-1]), b.reshape(-1, b.shape[-1]), WIRE_DTYPE).reshape(a.shape)
        for n, a, b in zip(BIG, kept, received)]
    contributions = _scatter_to_owners(chip_sums)
    reduced_halves = [_sum_leading("owner_sum_" + n, cb, jnp.float32) for n, cb in zip(BIG, contributions)]
    grads = dict(zip(BIG, _join_halves(reduced_halves)))

    small_parts = {
        "norm_mix_g": d_mix_g, "pool_scale": d_ps, "sgu_norm_g": d_gs, "w_spatial": d_ws, "b_spatial": d_b,
        "norm_xattn_g": d_xattn_g, "norm_mem_g": d_mem_g, "norm_ffn_g": d_ffn_g, "final_norm_g": d_final_g,
    }
    row_counts = [small_parts[n].size // LANES for n in SMALL]
    packed = jnp.concatenate([_as_lanes(small_parts[n]) for n in SMALL], axis=0)
    small_sum = _allreduce_small(packed)

    out_grad, out_delta, out_m, out_v = {}, {}, {}, {}
    for n in BIG:
        shape = weights[n].shape
        g2d = grads[n]
        delta, m_new, v_new = _adamw("adamw_" + n, _as2d(weights[n]), g2d, _as2d(moments_m[n]), _as2d(moments_v[n]))
        out_grad[n], out_delta[n] = g2d.reshape(shape), delta.reshape(shape)
        out_m[n], out_v[n] = m_new.reshape(shape), v_new.reshape(shape)
    small_out = _adamw_small(
        small_sum, [_as_lanes(weights[n]) for n in SMALL], [_as_lanes(moments_m[n]) for n in SMALL],
        [_as_lanes(moments_v[n]) for n in SMALL], row_counts)
    for t, n in enumerate(SMALL):
        shape = weights[n].shape
        out_grad[n], out_delta[n], out_m[n], out_v[n] = (a.reshape(shape) for a in small_out[4 * t:4 * t + 4])

    loss = lax.psum(loss_part[0, 0], ("x", "y", "c"))
    return (loss, grad_x[None], *[out_grad[n] for n in WEIGHTS], *[out_delta[n] for n in WEIGHTS],
            *[out_m[n] for n in WEIGHTS], *[out_v[n] for n in WEIGHTS])
```

```python
import math

import jax
import jax.numpy as jnp
from jax import lax
from jax.experimental import pallas as pl
from jax.experimental.pallas import tpu as pltpu

MXU_DTYPE = jnp.bfloat16
WIRE_DTYPE = jnp.bfloat16

EPS = 1e-6
CHUNK = 64
POOL_WINDOWS = (2, 4, 8, 16)
POOL_HALO = 16
N_XATTN_HEADS = 4
ADAM_LR = 0.001
ADAM_B1 = 0.9
ADAM_B2 = 0.999
ADAM_EPS = 1e-08
ADAM_WD = 0.01
ADAM_STEP = 10

N_CHIPS = 4
N_DEV = 8
LANES = 128
VMEM_LIMIT_BYTES = 56 * 2 ** 20
COPY_CHUNK_BYTES = 512 * 2 ** 10
COPY_ROW_ALIGN = 16

MESH = pl.DeviceIdType.MESH
ANY = pl.BlockSpec(memory_space=pl.ANY)
VMEM_WHOLE = pl.BlockSpec(memory_space=pltpu.VMEM)

NN = (((1,), (0,)), ((), ()))
NT = (((1,), (1,)), ((), ()))
TN = (((0,), (0,)), ((), ()))


def _tile(n, target, mult):
    best = None
    for t in range(mult, min(n, target) + 1, mult):
        if n % t == 0:
            best = t
    return n if best is None else best


def _params(n_grid_axes):
    return pltpu.CompilerParams(dimension_semantics=("arbitrary",) * n_grid_axes, vmem_limit_bytes=VMEM_LIMIT_BYTES)


def _rsqrt_mean_sq(x):
    return lax.rsqrt(jnp.mean(x * x, axis=-1, keepdims=True) + EPS)


def _sum_all(x):
    return jnp.sum(jnp.sum(x, axis=1, keepdims=True), axis=0, keepdims=True)


def _sigmoid(x):
    return 1.0 / (1.0 + jnp.exp(-x))


def _fused_matmul(name, grid, operands, pairs, acc_shapes, extras, outs, epilogue):
    nk = grid[2]
    n_op, n_ex, n_out, n_acc = len(operands), len(extras), len(outs), len(acc_shapes)

    def body(*refs):
        op_refs = refs[:n_op]
        ex_refs = refs[n_op:n_op + n_ex]
        out_refs = refs[n_op + n_ex:n_op + n_ex + n_out]
        acc_refs = refs[n_op + n_ex + n_out:]
        i, j, k = pl.program_id(0), pl.program_id(1), pl.program_id(2)
        loaded = {}

        def operand(n):
            if n not in loaded:
                loaded[n] = op_refs[n][...].astype(MXU_DTYPE)
            return loaded[n]

        parts = [None] * n_acc
        for ia, ib, iacc, dims in pairs:
            d = lax.dot_general(operand(ia), operand(ib), dims, preferred_element_type=jnp.float32)
            parts[iacc] = d if parts[iacc] is None else parts[iacc] + d

        if nk == 1:
            epilogue(parts, ex_refs, out_refs, i, j)
            return

        @pl.when(k == 0)
        def _():
            for a in range(n_acc):
                acc_refs[a][...] = parts[a]

        @pl.when(k > 0)
        def _():
            for a in range(n_acc):
                acc_refs[a][...] += parts[a]

        @pl.when(k == nk - 1)
        def _():
            epilogue([acc_refs[a][...] for a in range(n_acc)], ex_refs, out_refs, i, j)

    arrays = [o[0] for o in operands] + [e[0] for e in extras]
    in_specs = [pl.BlockSpec(o[1], o[2]) for o in operands] + [pl.BlockSpec(e[1], e[2]) for e in extras]
    return pl.pallas_call(
        body,
        name=name,
        grid=grid,
        in_specs=in_specs,
        out_specs=[pl.BlockSpec(o[2], o[3]) for o in outs],
        out_shape=[jax.ShapeDtypeStruct(o[0], o[1]) for o in outs],
        scratch_shapes=[pltpu.VMEM(s, jnp.float32) for s in acc_shapes] if nk > 1 else [],
        compiler_params=_params(3),
    )(*arrays)


def _store_epilogue(accs, ex_refs, out_refs, i, j):
    for a, o in zip(accs, out_refs):
        o[...] = a.astype(o.dtype)


def _matmul_nn(name, a, b, out_dtype, tm=1024, tn=512):
    M, K = a.shape
    N = b.shape[1]
    tm, tn = _tile(M, tm, 16), _tile(N, tn, LANES)
    return _fused_matmul(
        name, (M // tm, N // tn, 1),
        [(a, (tm, K), lambda i, j, k: (i, 0)), (b, (K, tn), lambda i, j, k: (0, j))],
        [(0, 1, 0, NN)], [(tm, tn)], [],
        [((M, N), out_dtype, (tm, tn), lambda i, j, k: (i, j))], _store_epilogue)[0]


def _matmul_nn_cols(name, a, b3, out_dtype, tm=1024):
    M, K = a.shape
    J, _, Nc = b3.shape
    tm = _tile(M, tm, 16)
    return _fused_matmul(
        name, (M // tm, J, 1),
        [(a, (tm, K), lambda i, j, k: (i, 0)), (b3, (None, K, Nc), lambda i, j, k: (j, 0, 0))],
        [(0, 1, 0, NN)], [(tm, Nc)], [],
        [((M, J * Nc), out_dtype, (tm, Nc), lambda i, j, k: (i, j))], _store_epilogue)[0]


def _matmul_nt(name, a_list, b_list, out_dtype, tm=1024, tn=512):
    M, K = a_list[0].shape
    N = b_list[0].shape[0]
    tm, tn = _tile(M, tm, 16), _tile(N, tn, LANES)
    n = len(a_list)
    operands = [(a, (tm, K), lambda i, j, k: (i, 0)) for a in a_list]
    operands += [(b, (tn, K), lambda i, j, k: (j, 0)) for b in b_list]
    return _fused_matmul(
        name, (M // tm, N // tn, 1), operands,
        [(p, n + p, 0, NT) for p in range(n)], [(tm, tn)], [],
        [((M, N), out_dtype, (tm, tn), lambda i, j, k: (i, j))], _store_epilogue)[0]


def _matmul_nt_cols(name, a_list, b3_list, out_dtype, tm=1024, tn=512):
    M = a_list[0].shape[0]
    J, N, Kc = b3_list[0].shape
    tm, tn = _tile(M, tm, 16), _tile(N, tn, LANES)
    n = len(a_list)
    operands = [(a, (tm, Kc), lambda i, j, k: (i, k)) for a in a_list]
    operands += [(b, (None, tn, Kc), lambda i, j, k: (k, j, 0)) for b in b3_list]
    return _fused_matmul(
        name, (M // tm, N // tn, J), operands,
        [(p, n + p, 0, NT) for p in range(n)], [(tm, tn)], [],
        [((M, N), out_dtype, (tm, tn), lambda i, j, k: (i, j))], _store_epilogue)[0]


def _matmul_tn(name, a, b_list, out_dtype, tka=1024, tn=1024, tkm=1024):
    M, Ka = a.shape
    N = b_list[0].shape[1]
    tka, tn, tkm = _tile(Ka, tka, LANES), _tile(N, tn, LANES), _tile(M, tkm, 16)
    n = len(b_list)
    operands = [(a, (tkm, tka), lambda i, j, k: (k, i))]
    operands += [(b, (tkm, tn), lambda i, j, k: (k, j)) for b in b_list]
    return _fused_matmul(
        name, (Ka // tka, N // tn, M // tkm), operands,
        [(0, 1 + p, p, TN) for p in range(n)], [(tka, tn)] * n, [],
        [((Ka, N), out_dtype, (tka, tn), lambda i, j, k: (i, j))] * n, _store_epilogue)


def _matmul_tn_cols(name, a, b_list, n_shards, out_dtype, tka=512, tkm=1024):
    M, Ka = a.shape
    Nc = b_list[0].shape[1] // n_shards
    tka, tkm = _tile(Ka, tka, LANES), _tile(M, tkm, 16)
    n = len(b_list)
    operands = [(a, (tkm, tka), lambda i, j, k: (k, i))]
    operands += [(b, (tkm, Nc), lambda i, j, k: (k, j)) for b in b_list]
    return _fused_matmul(
        name, (Ka // tka, n_shards, M // tkm), operands,
        [(0, 1 + p, p, TN) for p in range(n)], [(tka, Nc)] * n, [],
        [((n_shards, Ka, Nc), out_dtype, (None, tka, Nc), lambda i, j, k: (j, i, 0))] * n, _store_epilogue)


def _rmsnorm(name, x, g_row, out_dtype, tm=512):
    S, D = x.shape
    tm = _tile(S, tm, 16)

    def body(x_ref, g_ref, o_ref):
        xv = x_ref[...]
        o_ref[...] = (xv * _rsqrt_mean_sq(xv) * g_ref[...]).astype(o_ref.dtype)

    return pl.pallas_call(
        body, name=name, grid=(S // tm,),
        in_specs=[pl.BlockSpec((tm, D), lambda i: (i, 0)), pl.BlockSpec((1, D), lambda i: (0, 0))],
        out_specs=pl.BlockSpec((tm, D), lambda i: (i, 0)),
        out_shape=jax.ShapeDtypeStruct((S, D), out_dtype),
        compiler_params=_params(1),
    )(x, g_row)


def _rmsnorm_bwd_values(dh, xin, g):
    r = _rsqrt_mean_sq(xin)
    xhat = xin * r
    dhg = dh * g
    dx = r * (dhg - xhat * jnp.mean(dhg * xhat, axis=-1, keepdims=True))
    return dx, jnp.sum(dh * xhat, axis=0, keepdims=True)


def _rmsnorm_bwd(name, dh, xin, g_row, dres, low_dtype, tm=256):
    S, D = xin.shape
    tm = _tile(S, tm, 16)
    has_dx = dres is not None
    has_low = has_dx and low_dtype is not None

    def body(*refs):
        dh_ref, x_ref, g_ref = refs[:3]
        rest = refs[3:]
        dx, dg = _rmsnorm_bwd_values(dh_ref[...], x_ref[...], g_ref[...])
        if has_dx:
            dres_ref, rest = rest[0], rest[1:]
            dx = dx + dres_ref[...]
            rest[0][...] = dx
            if has_low:
                rest[1][...] = dx.astype(low_dtype)
        dg_ref = rest[-1]

        @pl.when(pl.program_id(0) == 0)
        def _():
            dg_ref[...] = jnp.zeros_like(dg_ref)

        dg_ref[...] += dg

    row = pl.BlockSpec((tm, D), lambda i: (i, 0))
    vec = pl.BlockSpec((1, D), lambda i: (0, 0))
    ins, in_specs = [dh, xin, g_row], [row, row, vec]
    out_shape, out_specs = [], []
    if has_dx:
        ins.append(dres)
        in_specs.append(row)
        out_shape.append(jax.ShapeDtypeStruct((S, D), jnp.float32))
        out_specs.append(row)
        if has_low:
            out_shape.append(jax.ShapeDtypeStruct((S, D), low_dtype))
            out_specs.append(row)
    out_shape.append(jax.ShapeDtypeStruct((1, D), jnp.float32))
    out_specs.append(vec)
    return pl.pallas_call(
        body, name=name, grid=(S // tm,), in_specs=in_specs, out_specs=out_specs, out_shape=out_shape,
        compiler_params=_params(1),
    )(*ins)


def _chunk_mask(blk):
    t = lax.broadcasted_iota(jnp.int32, (blk, blk), 0)
    s = lax.broadcasted_iota(jnp.int32, (blk, blk), 1)
    return (s // CHUNK) <= (t // CHUNK)


def _pool_inputs(a, halo, row0, tm, g, pg):
    cols = slice(g * pg, (g + 1) * pg)
    w = POOL_WINDOWS[g]
    s = jnp.concatenate([halo[:, cols], a[:, cols]], axis=0)
    span = 1
    while span < w:
        s = s + pltpu.roll(s, span, axis=0)
        span *= 2
    t = lax.broadcasted_iota(jnp.int32, (tm, 1), 0) + row0
    cnt = jnp.minimum(t + 1, w).astype(jnp.float32)
    return s[POOL_HALO:] / cnt - a[:, cols]


def _mixer_specs(S, tm, DP, DS):
    per_halo = tm // POOL_HALO
    n_halo = S // POOL_HALO
    a_spec = pl.BlockSpec((tm, DP), lambda i: (i, 0))
    prev_spec = pl.BlockSpec((POOL_HALO, DP), lambda i: (jnp.maximum(i * per_halo - 1, 0), 0))
    next_spec = pl.BlockSpec((POOL_HALO, DP), lambda i: (jnp.minimum((i + 1) * per_halo, n_halo - 1), 0))
    u_spec = pl.BlockSpec((tm, DS), lambda i: (i, 1))
    v_spec = pl.BlockSpec((tm, DS), lambda i: (i, 2))
    return a_spec, prev_spec, next_spec, u_spec, v_spec


def _whole(arr):
    nd = arr.ndim
    return pl.BlockSpec(arr.shape, lambda i: (0,) * nd)


def _mixer_fwd(proj, pw, ps_row, gs_row, ws, b_full, tm=512):
    S = proj.shape[0]
    G, PG, _ = pw.shape
    DP = G * PG
    H, BLK, _ = ws.shape
    DS = gs_row.shape[1]
    HD = DS // H
    assert DP == DS and proj.shape[1] == DP + 2 * DS
    tm = _tile(S, tm, BLK)

    def body(a_ref, prev_ref, u_ref, v_ref, pw_ref, ps_ref, gs_ref, ws_ref, b_ref, z_ref):
        i = pl.program_id(0)
        a = a_ref[...]
        halo = jnp.where(i > 0, prev_ref[...], 0.0)
        for g in range(G):
            cols = slice(g * PG, (g + 1) * PG)
            p = _pool_inputs(a, halo, i * tm, tm, g, PG)
            y = jnp.dot(p.astype(MXU_DTYPE), pw_ref[g], preferred_element_type=jnp.float32) * ps_ref[:, cols]
            z_ref[:, cols] = y.astype(z_ref.dtype)
        v = v_ref[...]
        vn = (v * _rsqrt_mean_sq(v) * gs_ref[...]).astype(MXU_DTYPE)
        u = u_ref[...]
        mask = _chunk_mask(BLK)
        for h in range(H):
            wm = jnp.where(mask, ws_ref[h], 0.0).astype(MXU_DTYPE)
            bias = b_ref[h]
            cols = slice(h * HD, (h + 1) * HD)
            for n in range(tm // BLK):
                rows = slice(n * BLK, (n + 1) * BLK)
                mixed = jnp.dot(wm, vn[rows, cols], preferred_element_type=jnp.float32) + bias
                z_ref[rows, DP + h * HD:DP + (h + 1) * HD] = (u[rows, cols] * mixed).astype(z_ref.dtype)

    a_spec, prev_spec, _, u_spec, v_spec = _mixer_specs(S, tm, DP, DS)
    return pl.pallas_call(
        body, name="mixer_fwd", grid=(S // tm,),
        in_specs=[a_spec, prev_spec, u_spec, v_spec, _whole(pw), _whole(ps_row), _whole(gs_row), _whole(ws), _whole(b_full)],
        out_specs=pl.BlockSpec((tm, DP + DS), lambda i: (i, 0)),
        out_shape=jax.ShapeDtypeStruct((S, DP + DS), MXU_DTYPE),
        compiler_params=_params(1),
    )(proj, proj, proj, proj, pw, ps_row, gs_row, ws, b_full)


def _mixer_bwd(proj, dz, pw, ps_row, gs_row, ws, b_full, tm=256):
    S = proj.shape[0]
    G, PG, _ = pw.shape
    DP = G * PG
    H, BLK, _ = ws.shape
    DS = gs_row.shape[1]
    HD = DS // H
    tm = _tile(S, tm, BLK)
    nb = S // tm
    ext = tm + POOL_HALO

    def body(a_ref, prev_ref, u_ref, v_ref, dzp_ref, dzp_next_ref, dzs_ref, pw_ref, ps_ref, gs_ref, ws_ref, b_ref,
             dproj_ref, dpw_ref, dps_ref, dgs_ref, dws_ref, db_ref, dvn_ref):
        i = pl.program_id(0)

        @pl.when(i == 0)
        def _():
            for r in (dpw_ref, dps_ref, dgs_ref, dws_ref, db_ref):
                r[...] = jnp.zeros_like(r)

        a = a_ref[...]
        halo = jnp.where(i > 0, prev_ref[...], 0.0)
        dyp = dzp_ref[...]
        dyp_next = jnp.where(i < nb - 1, dzp_next_ref[...], 0.0)
        dq_ext = jnp.concatenate([dyp, dyp_next], axis=0) * ps_ref[...]
        t_ext = lax.broadcasted_iota(jnp.int32, (ext, 1), 0) + i * tm
        for g in range(G):
            cols = slice(g * PG, (g + 1) * PG)
            w = POOL_WINDOWS[g]
            pb = _pool_inputs(a, halo, i * tm, tm, g, PG).astype(MXU_DTYPE)
            y_pre = jnp.dot(pb, pw_ref[g], preferred_element_type=jnp.float32)
            dps_ref[:, cols] += jnp.sum(dyp[:, cols] * y_pre, axis=0, keepdims=True)
            dqb = dq_ext[:, cols].astype(MXU_DTYPE)
            dpw_ref[g] += lax.dot_general(pb, dqb[:tm], TN, preferred_element_type=jnp.float32)
            dp = lax.dot_general(dqb, pw_ref[g], NT, preferred_element_type=jnp.float32)
            f = dp / jnp.minimum(t_ext + 1, w).astype(jnp.float32)
            span = 1
            while span < w:
                f = f + pltpu.roll(f, ext - span, axis=0)
                span *= 2
            dproj_ref[:, cols] = (f[:tm] - dp[:tm]).astype(dproj_ref.dtype)

        v = v_ref[...]
        rv = _rsqrt_mean_sq(v)
        vhat = v * rv
        gs = gs_ref[...]
        vnb = (vhat * gs).astype(MXU_DTYPE)
        u = u_ref[...]
        dys = dzs_ref[...]
        mask = _chunk_mask(BLK)
        for h in range(H):
            wm = jnp.where(mask, ws_ref[h], 0.0).astype(MXU_DTYPE)
            bias = b_ref[h]
            cols = slice(h * HD, (h + 1) * HD)
            dm_sum = jnp.zeros((BLK, HD), jnp.float32)
            dws_sum = jnp.zeros((BLK, BLK), jnp.float32)
            for n in range(tm // BLK):
                rows = slice(n * BLK, (n + 1) * BLK)
                vblk = vnb[rows, cols]
                mixed = jnp.dot(wm, vblk, preferred_element_type=jnp.float32) + bias
                dy = dys[rows, cols]
                dproj_ref[rows, DP + h * HD:DP + (h + 1) * HD] = (dy * mixed).astype(dproj_ref.dtype)
                dm = dy * u[rows, cols]
                dm_sum = dm_sum + dm
                dmb = dm.astype(MXU_DTYPE)
                dws_sum = dws_sum + lax.dot_general(dmb, vblk, NT, preferred_element_type=jnp.float32)
                dvn_ref[rows, cols] = lax.dot_general(wm, dmb, TN, preferred_element_type=jnp.float32)
            dws_ref[h] += jnp.where(mask, dws_sum, 0.0)
            db_ref[h:h + 1, :] += jnp.sum(dm_sum.T, axis=0, keepdims=True)
        dvn = dvn_ref[...]
        dvg = dvn * gs
        dv = rv * (dvg - vhat * jnp.mean(dvg * vhat, axis=-1, keepdims=True))
        dgs_ref[...] += jnp.sum(dvn * vhat, axis=0, keepdims=True)
        dproj_ref[:, DP + DS:] = dv.astype(dproj_ref.dtype)

    a_spec, prev_spec, next_spec, u_spec, v_spec = _mixer_specs(S, tm, DP, DS)
    dzp_spec = pl.BlockSpec((tm, DP), lambda i: (i, 0))
    dzs_spec = pl.BlockSpec((tm, DS), lambda i: (i, 1))
    small = [pw, ps_row, gs_row, ws, b_full]
    out_shape = [
        jax.ShapeDtypeStruct((S, DP + 2 * DS), MXU_DTYPE),
        jax.ShapeDtypeStruct(pw.shape, jnp.float32),
        jax.ShapeDtypeStruct(ps_row.shape, jnp.float32),
        jax.ShapeDtypeStruct(gs_row.shape, jnp.float32),
        jax.ShapeDtypeStruct(ws.shape, jnp.float32),
        jax.ShapeDtypeStruct((H, BLK), jnp.float32),
    ]
    out_specs = [pl.BlockSpec((tm, DP + 2 * DS), lambda i: (i, 0))] + [_whole(s) for s in out_shape[1:]]
    return pl.pallas_call(
        body, name="mixer_bwd", grid=(nb,),
        in_specs=[a_spec, prev_spec, u_spec, v_spec, dzp_spec, next_spec, dzs_spec] + [_whole(s) for s in small],
        out_specs=out_specs, out_shape=out_shape,
        scratch_shapes=[pltpu.VMEM((tm, DS), jnp.float32)],
        compiler_params=_params(1),
    )(proj, proj, proj, proj, dz, dz, dz, *small)


def _softmax_rows(q, k, scale):
    s = lax.dot_general(q, k, NT, preferred_element_type=jnp.float32) * scale
    e = jnp.exp(s - jnp.max(s, axis=-1, keepdims=True))
    return e / jnp.sum(e, axis=-1, keepdims=True)


def _attn_fwd(q, k, v, tm=512):
    S, D = q.shape
    M = k.shape[0]
    HD = D // N_XATTN_HEADS
    scale = HD ** -0.5
    tm = _tile(S, tm, 16)

    def body(q_ref, k_ref, v_ref, o_ref):
        for h in range(N_XATTN_HEADS):
            cols = slice(h * HD, (h + 1) * HD)
            p = _softmax_rows(q_ref[:, cols], k_ref[:, cols], scale)
            o = jnp.dot(p.astype(MXU_DTYPE), v_ref[:, cols], preferred_element_type=jnp.float32)
            o_ref[:, cols] = o.astype(o_ref.dtype)

    row = pl.BlockSpec((tm, D), lambda i: (i, 0))
    mem = pl.BlockSpec((M, D), lambda i: (0, 0))
    return pl.pallas_call(
        body, name="attn_fwd", grid=(S // tm,), in_specs=[row, mem, mem], out_specs=row,
        out_shape=jax.ShapeDtypeStruct((S, D), MXU_DTYPE), compiler_params=_params(1),
    )(q, k, v)


def _attn_bwd(q, k, v, do, tm=512):
    S, D = q.shape
    M = k.shape[0]
    HD = D // N_XATTN_HEADS
    scale = HD ** -0.5
    tm = _tile(S, tm, 16)

    def body(q_ref, k_ref, v_ref, do_ref, dq_ref, dk_ref, dv_ref):
        @pl.when(pl.program_id(0) == 0)
        def _():
            dk_ref[...] = jnp.zeros_like(dk_ref)
            dv_ref[...] = jnp.zeros_like(dv_ref)

        for h in range(N_XATTN_HEADS):
            cols = slice(h * HD, (h + 1) * HD)
            qh, kh, vh, doh = q_ref[:, cols], k_ref[:, cols], v_ref[:, cols], do_ref[:, cols]
            p = _softmax_rows(qh, kh, scale)
            dp = lax.dot_general(doh, vh, NT, preferred_element_type=jnp.float32)
            dv_ref[:, cols] += lax.dot_general(p.astype(MXU_DTYPE), doh, TN, preferred_element_type=jnp.float32)
            ds = (p * (dp - jnp.sum(dp * p, axis=-1, keepdims=True)) * scale).astype(MXU_DTYPE)
            dq_ref[:, cols] = jnp.dot(ds, kh, preferred_element_type=jnp.float32).astype(dq_ref.dtype)
            dk_ref[:, cols] += lax.dot_general(ds, qh, TN, preferred_element_type=jnp.float32)

    row = pl.BlockSpec((tm, D), lambda i: (i, 0))
    mem = pl.BlockSpec((M, D), lambda i: (0, 0))
    return pl.pallas_call(
        body, name="attn_bwd", grid=(S // tm,), in_specs=[row, mem, mem, row], out_specs=[row, mem, mem],
        out_shape=[jax.ShapeDtypeStruct((S, D), MXU_DTYPE), jax.ShapeDtypeStruct((M, D), jnp.float32),
                   jax.ShapeDtypeStruct((M, D), jnp.float32)],
        compiler_params=_params(1),
    )(q, k, v, do)


def _proj_residual_norm(name, a, w2d, resid, g_row, tm=256):
    M, K = a.shape
    N = w2d.shape[1]
    tm = _tile(M, tm, 16)

    def epilogue(accs, ex_refs, out_refs, i, j):
        x_new = ex_refs[0][...] + accs[0]
        out_refs[0][...] = x_new
        out_refs[1][...] = (x_new * _rsqrt_mean_sq(x_new) * ex_refs[1][...]).astype(MXU_DTYPE)

    row = lambda i, j, k: (i, 0)
    return _fused_matmul(
        name, (M // tm, 1, 1),
        [(a, (tm, K), row), (w2d, (K, N), lambda i, j, k: (0, 0))],
        [(0, 1, 0, NN)], [(tm, N)],
        [(resid, (tm, N), row), (g_row, (1, N), lambda i, j, k: (0, 0))],
        [((M, N), jnp.float32, (tm, N), row), ((M, N), MXU_DTYPE, (tm, N), row)], epilogue)


def _ffn_gate_up(h, wg3, wu3, tm=512):
    M, K = h.shape
    J, _, Nc = wg3.shape
    tm = _tile(M, tm, 16)

    def epilogue(accs, ex_refs, out_refs, i, j):
        gate, up = accs
        out_refs[0][...] = gate
        out_refs[1][...] = up
        out_refs[2][...] = (gate * _sigmoid(gate) * up).astype(MXU_DTYPE)

    col = lambda i, j, k: (i, j)
    wspec = lambda i, j, k: (j, 0, 0)
    return _fused_matmul(
        "ffn_gate_up", (M // tm, J, 1),
        [(h, (tm, K), lambda i, j, k: (i, 0)), (wg3, (None, K, Nc), wspec), (wu3, (None, K, Nc), wspec)],
        [(0, 1, 0, NN), (0, 2, 1, NN)], [(tm, Nc)] * 2, [],
        [((M, J * Nc), jnp.float32, (tm, Nc), col), ((M, J * Nc), jnp.float32, (tm, Nc), col),
         ((M, J * Nc), MXU_DTYPE, (tm, Nc), col)], epilogue)


def _ffn_down_loss(act, wd2d, x2, target, g_row, n_k, tm=256):
    M, F = act.shape
    D = wd2d.shape[1]
    tm = _tile(M, tm, 16)
    tk = F // n_k

    def epilogue(accs, ex_refs, out_refs, i, j):
        x3 = ex_refs[0][...] + accs[0]
        g = ex_refs[2][...]
        r = _rsqrt_mean_sq(x3)
        xhat = x3 * r
        diff = xhat * g - ex_refs[1][...]
        dy = diff / D
        dyg = dy * g
        dx = r * (dyg - xhat * jnp.mean(dyg * xhat, axis=-1, keepdims=True))
        out_refs[0][...] = dx
        out_refs[1][...] = dx.astype(MXU_DTYPE)

        @pl.when(i == 0)
        def _():
            out_refs[2][...] = jnp.zeros_like(out_refs[2])
            out_refs[3][...] = jnp.zeros_like(out_refs[3])

        out_refs[2][...] += 0.5 * _sum_all(jnp.mean(diff * diff, axis=-1, keepdims=True))
        out_refs[3][...] += jnp.sum(dy * xhat, axis=0, keepdims=True)

    row = lambda i, j, k: (i, 0)
    const = lambda i, j, k: (0, 0)
    return _fused_matmul(
        "ffn_down_loss", (M // tm, 1, n_k),
        [(act, (tm, tk), lambda i, j, k: (i, k)), (wd2d, (tk, D), lambda i, j, k: (k, 0))],
        [(0, 1, 0, NN)], [(tm, D)],
        [(x2, (tm, D), row), (target, (tm, D), row), (g_row, (1, D), const)],
        [((M, D), jnp.float32, (tm, D), row), ((M, D), MXU_DTYPE, (tm, D), row),
         ((1, 1), jnp.float32, (1, 1), const), ((1, D), jnp.float32, (1, D), const)], epilogue)


def _ffn_down_bwd(dx3_low, wd2d, gate, up, n_shards, tm=512):
    M, D = dx3_low.shape
    F = wd2d.shape[0]
    tn = F // n_shards
    tm = _tile(M, tm, 16)

    def epilogue(accs, ex_refs, out_refs, i, j):
        dact = accs[0]
        gate_v, up_v = ex_refs[0][...], ex_refs[1][...]
        sg = _sigmoid(gate_v)
        out_refs[0][...] = (dact * up_v * (sg * (1.0 + gate_v * (1.0 - sg)))).astype(MXU_DTYPE)
        out_refs[1][...] = (dact * (gate_v * sg)).astype(MXU_DTYPE)

    col = lambda i, j, k: (i, j)
    return _fused_matmul(
        "ffn_down_bwd", (M // tm, n_shards, 1),
        [(dx3_low, (tm, D), lambda i, j, k: (i, 0)), (wd2d, (tn, D), lambda i, j, k: (j, 0))],
        [(0, 1, 0, NT)], [(tm, tn)],
        [(gate, (tm, tn), col), (up, (tm, tn), col)],
        [((M, F), MXU_DTYPE, (tm, tn), col)] * 2, epilogue)


def _row_tile(R, C, target_bytes=2 ** 21):
    return _tile(R, max(16, target_bytes // (4 * C)), 16)


def _prefetch_call(body, name, grid, pos, in_specs, out_specs, out_shape, ins):
    grid_spec = pltpu.PrefetchScalarGridSpec(
        num_scalar_prefetch=1, grid=grid, in_specs=in_specs, out_specs=out_specs)
    return pl.pallas_call(body, name=name, grid_spec=grid_spec, out_shape=out_shape,
                          compiler_params=_params(len(grid)))(pos, *ins)


def _cast_into_slab(name, x2d, dtype, pos):
    R, C = x2d.shape
    tr = _row_tile(R, C)

    def body(pos_ref, x_ref, o_ref):
        o_ref[...] = x_ref[...].astype(o_ref.dtype)

    return _prefetch_call(
        body, name, (R // tr,), pos, [pl.BlockSpec((tr, C), lambda i, pos: (i, 0))],
        pl.BlockSpec((None, tr, C), lambda i, pos: (pos[0], i, 0)),
        jax.ShapeDtypeStruct((N_CHIPS, R, C), dtype), [x2d])


def _chip_sum(name, partial, got, dtype, pos):
    J, R, C = partial.shape
    Rh = R // 2
    tr = _row_tile(Rh, C)

    def body(pos_ref, a_ref, b_ref, o_ref):
        o_ref[...] = (a_ref[...].astype(jnp.float32) + b_ref[...].astype(jnp.float32)).astype(o_ref.dtype)

    blk = pl.BlockSpec((None, tr, C), lambda j, i, pos: (j, i, 0))
    return _prefetch_call(
        body, name, (J, Rh // tr), pos,
        [pl.BlockSpec((None, None, tr, C), lambda j, i, pos: (j, pos[1], i, 0)), blk], blk,
        jax.ShapeDtypeStruct((J, Rh, C), dtype), [partial.reshape(J, 2, Rh, C), got])


def _owner_sum(name, chip_sums, others, pos):
    J, Rh, C = chip_sums.shape
    n = others.shape[0]
    tr = _row_tile(Rh, C * (n + 1))

    def body(pos_ref, own_ref, oth_ref, o_ref):
        acc = own_ref[...].astype(jnp.float32)
        for k in range(n):
            acc = acc + oth_ref[k].astype(jnp.float32)
        o_ref[...] = acc

    return _prefetch_call(
        body, name, (Rh // tr,), pos,
        [pl.BlockSpec((None, tr, C), lambda i, pos: (pos[0], i, 0)), pl.BlockSpec((n, tr, C), lambda i, pos: (0, i, 0))],
        pl.BlockSpec((None, tr, C), lambda i, pos: (pos[1], i, 0)),
        jax.ShapeDtypeStruct((2, Rh, C), jnp.float32), [chip_sums, others])


def _adamw_values(w, g, m, v):
    m = ADAM_B1 * m + (1.0 - ADAM_B1) * g
    v = ADAM_B2 * v + (1.0 - ADAM_B2) * (g * g)
    m_hat = m / (1.0 - ADAM_B1 ** ADAM_STEP)
    v_hat = v / (1.0 - ADAM_B2 ** ADAM_STEP)
    delta = -ADAM_LR * (m_hat / (jnp.sqrt(v_hat) + ADAM_EPS) + ADAM_WD * w)
    return delta, m, v


def _adamw(name, w, g, m, v):
    R, C = w.shape
    tr = _row_tile(R, C, 2 ** 20)

    def body(w_ref, g_ref, m_ref, v_ref, d_ref, mo_ref, vo_ref):
        d_ref[...], mo_ref[...], vo_ref[...] = _adamw_values(w_ref[...], g_ref[...], m_ref[...], v_ref[...])

    blk = pl.BlockSpec((tr, C), lambda i: (i, 0))
    return pl.pallas_call(
        body, name=name, grid=(R // tr,), in_specs=[blk] * 4, out_specs=[blk] * 3,
        out_shape=[jax.ShapeDtypeStruct((R, C), jnp.float32)] * 3, compiler_params=_params(1))(w, g, m, v)


def _adamw_small(g_packed, ws, ms, vs, row_counts):
    n = len(ws)

    def body(*refs):
        g_ref = refs[0]
        w_refs, m_refs, v_refs = refs[1:1 + n], refs[1 + n:1 + 2 * n], refs[1 + 2 * n:1 + 3 * n]
        outs = refs[1 + 3 * n:]
        off = 0
        for t in range(n):
            g = g_ref[off:off + row_counts[t], :]
            off += row_counts[t]
            delta, m_new, v_new = _adamw_values(w_refs[t][...], g, m_refs[t][...], v_refs[t][...])
            outs[4 * t][...] = g
            outs[4 * t + 1][...] = delta
            outs[4 * t + 2][...] = m_new
            outs[4 * t + 3][...] = v_new

    out_shape = []
    for t in range(n):
        out_shape += [jax.ShapeDtypeStruct((row_counts[t], LANES), jnp.float32)] * 4
    return pl.pallas_call(
        body, name="adamw_small", in_specs=[VMEM_WHOLE] * (1 + 3 * n), out_specs=[VMEM_WHOLE] * (4 * n),
        out_shape=out_shape, compiler_params=pltpu.CompilerParams(vmem_limit_bytes=VMEM_LIMIT_BYTES),
    )(g_packed, *ws, *ms, *vs)


def _position():
    x, y, c = lax.axis_index("x"), lax.axis_index("y"), lax.axis_index("c")
    other_chips = [(1 - x, y), (x, 1 - y), (1 - x, 1 - y)]
    return x, y, c, other_chips


def _half(rows, which):
    return pl.ds(which * (rows // 2), rows // 2)


def _chunk_index(shape, dtype):
    rows = shape[-2]
    total = math.prod(shape) * jnp.dtype(dtype).itemsize
    n = max(1, min(rows // COPY_ROW_ALIGN, total // COPY_CHUNK_BYTES))
    step = -(-rows // n)
    step = -(-step // COPY_ROW_ALIGN) * COPY_ROW_ALIGN
    lead = (slice(None),) * (len(shape) - 2)
    return [lead + (pl.ds(s, min(step, rows - s)),) for s in range(0, rows, step)]


def _start_in_chunks(make, src, dst):
    for idx in _chunk_index(src.shape, src.dtype):
        make(src.at[idx], dst.at[idx]).start()
    return make(src, dst)


def _remote_copier(send_sem, recv_sem, to):
    return lambda src, dst: pltpu.make_async_remote_copy(
        src_ref=src, dst_ref=dst, send_sem=send_sem, recv_sem=recv_sem, device_id=to, device_id_type=MESH)


def _gather_weights(slabs):
    T = len(slabs)

    def body(*refs):
        outs = refs[T:2 * T]
        send_sems, recv_sems = refs[2 * T:]
        x, y, c, other_chips = _position()
        mine = 2 * x + y
        sibling = (x, y, 1 - c)

        def block(t, chip, which):
            return outs[t].at[chip, _half(outs[t].shape[1], which)]

        def copier(t, k, to):
            return _remote_copier(send_sems.at[t, k], recv_sems.at[t, k], to)

        started = []
        for t in range(T):
            for k, (cx, cy) in enumerate(other_chips):
                started.append(_start_in_chunks(copier(t, k, (cx, cy, c)), block(t, mine, c), block(t, mine, c)))
        for k, (cx, cy) in enumerate(other_chips):
            for t in range(T):
                landed = block(t, 2 * cx + cy, c)
                copier(t, k, (cx, cy, c))(landed, landed).wait_recv()
                started.append(_start_in_chunks(copier(t, 3 + k, sibling), landed, landed))
        for k, (cx, cy) in enumerate(other_chips):
            for t in range(T):
                arriving = block(t, 2 * cx + cy, 1 - c)
                copier(t, 3 + k, sibling)(arriving, arriving).wait_recv()
        for cp in started:
            cp.wait_send()

    return pl.pallas_call(
        body, name="gather_weights", in_specs=[ANY] * T, out_specs=[ANY] * T,
        out_shape=[jax.ShapeDtypeStruct(s.shape, s.dtype) for s in slabs],
        input_output_aliases={t: t for t in range(T)},
        scratch_shapes=[pltpu.SemaphoreType.DMA((T, 6)), pltpu.SemaphoreType.DMA((T, 6))],
    )(*slabs)


def _swap_halves(grads):
    T = len(grads)

    def body(*refs):
        ins, got = refs[:T], refs[T:2 * T]
        send_sems, recv_sems = refs[2 * T:]
        x, y, c, _ = _position()
        remote = []
        for t in range(T):
            rows = ins[t].shape[1]
            remote.append(_start_in_chunks(
                _remote_copier(send_sems.at[t], recv_sems.at[t], (x, y, 1 - c)), ins[t].at[:, _half(rows, 1 - c)], got[t]))
        for cp in remote:
            cp.wait()

    return pl.pallas_call(
        body, name="swap_grad_halves", in_specs=[ANY] * T, out_specs=[ANY] * T,
        out_shape=[jax.ShapeDtypeStruct((g.shape[0], g.shape[1] // 2, g.shape[2]), g.dtype) for g in grads],
        scratch_shapes=[pltpu.SemaphoreType.DMA((T,)), pltpu.SemaphoreType.DMA((T,))],
    )(*grads)


def _scatter_to_owners(chip_sums):
    T = len(chip_sums)

    def body(*refs):
        ins, outs = refs[:T], refs[T:2 * T]
        send_sems, recv_sems = refs[2 * T:]
        x, y, c, other_chips = _position()

        def copier(t, k, to):
            return _remote_copier(send_sems.at[t, k], recv_sems.at[t, k], to)

        remote = []
        for t in range(T):
            for k, (cx, cy) in enumerate(other_chips):
                remote.append(_start_in_chunks(copier(t, k, (cx, cy, c)), ins[t].at[2 * cx + cy], outs[t].at[k]))
        for t in range(T):
            for k, (cx, cy) in enumerate(other_chips):
                copier(t, k, (cx, cy, c))(outs[t].at[k], outs[t].at[k]).wait_recv()
        for cp in remote:
            cp.wait_send()

    return pl.pallas_call(
        body, name="scatter_grads_to_owners", in_specs=[ANY] * T, out_specs=[ANY] * T,
        out_shape=[jax.ShapeDtypeStruct((3,) + s.shape[1:], s.dtype) for s in chip_sums],
        scratch_shapes=[pltpu.SemaphoreType.DMA((T, 3)), pltpu.SemaphoreType.DMA((T, 3))],
    )(*chip_sums)


def _join_halves(halves):
    T = len(halves)

    def body(*refs):
        outs = refs[T:2 * T]
        send_sems, recv_sems = refs[2 * T:]
        x, y, c, _ = _position()

        def copier(t):
            return _remote_copier(send_sems.at[t], recv_sems.at[t], (x, y, 1 - c))

        remote = [_start_in_chunks(copier(t), outs[t].at[c], outs[t].at[c]) for t in range(T)]
        for t in range(T):
            copier(t)(outs[t].at[1 - c], outs[t].at[1 - c]).wait_recv()
        for cp in remote:
            cp.wait_send()

    return pl.pallas_call(
        body, name="join_grad_halves", in_specs=[ANY] * T, out_specs=[ANY] * T,
        out_shape=[jax.ShapeDtypeStruct(h.shape, h.dtype) for h in halves],
        input_output_aliases={t: t for t in range(T)},
        scratch_shapes=[pltpu.SemaphoreType.DMA((T,)), pltpu.SemaphoreType.DMA((T,))],
    )(*halves)


def _allreduce_small(part):
    rows = part.shape[0]

    def body(x_ref, out_ref, gath, send_sems, recv_sems):
        x, y, c, other_chips = _position()
        sibling = (x, y, 1 - c)

        def slot(px, py, pc):
            return gath.at[4 * px + 2 * py + pc]

        def copy(k, owner, to, src=None):
            return pltpu.make_async_remote_copy(
                src_ref=slot(*owner) if src is None else src, dst_ref=slot(*owner), send_sem=send_sems.at[k],
                recv_sem=recv_sems.at[k], device_id=to, device_id_type=MESH)

        gath[4 * x + 2 * y + c] = x_ref[...]
        first = [copy(0, (x, y, c), sibling, src=x_ref)]
        first += [copy(1 + k, (x, y, c), (cx, cy, c), src=x_ref) for k, (cx, cy) in enumerate(other_chips)]
        for cp in first:
            cp.start()
        passed = [copy(4 + k, (cx, cy, c), sibling) for k, (cx, cy) in enumerate(other_chips)]
        for k, (cx, cy) in enumerate(other_chips):
            copy(1 + k, (cx, cy, c), (x, y, c)).wait_recv()
            passed[k].start()
        copy(0, (x, y, 1 - c), (x, y, c)).wait_recv()
        for k, (cx, cy) in enumerate(other_chips):
            copy(4 + k, (cx, cy, 1 - c), (x, y, c)).wait_recv()
        for cp in first + passed:
            cp.wait_send()
        acc = gath[0]
        for d in range(1, N_DEV):
            acc = acc + gath[d]
        out_ref[...] = acc

    return pl.pallas_call(
        body, name="allreduce_small_grads", in_specs=[VMEM_WHOLE], out_specs=VMEM_WHOLE,
        out_shape=jax.ShapeDtypeStruct((rows, LANES), jnp.float32),
        scratch_shapes=[pltpu.VMEM((N_DEV, rows, LANES), jnp.float32), pltpu.SemaphoreType.DMA((7,)),
                        pltpu.SemaphoreType.DMA((7,))],
        compiler_params=pltpu.CompilerParams(vmem_limit_bytes=VMEM_LIMIT_BYTES),
    )(part)


BIG = ("w_in", "pool_w", "w_out", "w_q", "w_k", "w_v", "w_o", "w_gate", "w_up", "w_down")
SMALL = ("norm_mix_g", "pool_scale", "sgu_norm_g", "w_spatial", "b_spatial", "norm_xattn_g", "norm_mem_g",
         "norm_ffn_g", "final_norm_g")
WEIGHTS = ("norm_mix_g", "w_in", "pool_w", "pool_scale", "sgu_norm_g", "w_spatial", "b_spatial", "w_out",
           "norm_xattn_g", "norm_mem_g", "w_q", "w_k", "w_v", "w_o", "norm_ffn_g", "w_gate", "w_up", "w_down",
           "final_norm_g")


def _as2d(a):
    return a.reshape(-1, a.shape[-1])


def _as_lanes(a):
    return a.reshape(-1, LANES)


def kernel(x, mem, norm_mix_g, w_in, pool_w, pool_scale, sgu_norm_g, w_spatial, b_spatial, w_out, norm_xattn_g, norm_mem_g, w_q, w_k, w_v, w_o, norm_ffn_g, w_gate, w_up, w_down, final_norm_g, loss_target, m_norm_mix_g, m_w_in, m_pool_w, m_pool_scale, m_sgu_norm_g, m_w_spatial, m_b_spatial, m_w_out, m_norm_xattn_g, m_norm_mem_g, m_w_q, m_w_k, m_w_v, m_w_o, m_norm_ffn_g, m_w_gate, m_w_up, m_w_down, m_final_norm_g, v_norm_mix_g, v_w_in, v_pool_w, v_pool_scale, v_sgu_norm_g, v_w_spatial, v_b_spatial, v_w_out, v_norm_xattn_g, v_norm_mem_g, v_w_q, v_w_k, v_w_v, v_w_o, v_norm_ffn_g, v_w_gate, v_w_up, v_w_down, v_final_norm_g):
    args = dict(locals())
    weights = {n: args[n] for n in WEIGHTS}
    moments_m = {n: args["m_" + n] for n in WEIGHTS}
    moments_v = {n: args["v_" + n] for n in WEIGHTS}
    low = MXU_DTYPE

    xs = x[0]
    mems = mem[0]
    target = loss_target[0]
    D = xs.shape[1]
    G, PG = pool_w.shape[1], pool_w.shape[3]

    pos = jnp.stack([2 * lax.axis_index("x") + lax.axis_index("y"), lax.axis_index("c")]).astype(jnp.int32)
    slabs = [_cast_into_slab("cast_" + n, _as2d(weights[n]), WIRE_DTYPE, pos) for n in BIG]
    gathered = dict(zip(BIG, _gather_weights(slabs)))
    wg_in = gathered["w_in"]
    pw = gathered["pool_w"].reshape(N_CHIPS, G, PG // N_CHIPS, PG).transpose(1, 0, 2, 3).reshape(G, PG, PG)
    full = {n: gathered[n].reshape(-1, gathered[n].shape[-1]) for n in ("w_out", "w_q", "w_k", "w_v", "w_o", "w_down")}
    wg_gate, wg_up = gathered["w_gate"], gathered["w_up"]

    g_mix, g_xattn, g_mem, g_ffn = norm_mix_g, norm_xattn_g, norm_mem_g, norm_ffn_g
    g_final = final_norm_g.reshape(1, D)
    ws = w_spatial[0]
    b_full = jnp.broadcast_to(b_spatial[0][:, :, None], ws.shape[:2] + (sgu_norm_g.shape[1] // ws.shape[0],))

    h1 = _rmsnorm("norm_mix", xs, g_mix, low)
    proj = _matmul_nn_cols("proj_in", h1, wg_in, jnp.float32)
    z = _mixer_fwd(proj, pw, pool_scale, sgu_norm_g, ws, b_full)
    x1, h2 = _proj_residual_norm("mix_out", z, full["w_out"], xs, g_xattn)
    q = _matmul_nn("xattn_q", h2, full["w_q"], low)
    mn = _rmsnorm("norm_mem", mems, g_mem, low)
    k = _matmul_nn("xattn_k", mn, full["w_k"], low)
    vv = _matmul_nn("xattn_v", mn, full["w_v"], low)
    o = _attn_fwd(q, k, vv)
    x2, h3 = _proj_residual_norm("xattn_out", o, full["w_o"], x1, g_ffn)
    gate, up, act = _ffn_gate_up(h3, wg_gate, wg_up)
    dx3, dx3_low, loss_part, d_final_g = _ffn_down_loss(act, full["w_down"], x2, target, g_final, N_CHIPS)

    dgate, dup = _ffn_down_bwd(dx3_low, full["w_down"], gate, up, N_CHIPS)
    (d_w_down,) = _matmul_tn("grad_w_down", act, [dx3_low], WIRE_DTYPE)
    d_w_gate, d_w_up = _matmul_tn_cols("grad_w_gate_up", h3, [dgate, dup], N_CHIPS, WIRE_DTYPE)
    dh3 = _matmul_nt_cols("ffn_dh", [dgate, dup], [wg_gate, wg_up], jnp.float32)
    dx2, dx2_low, d_ffn_g = _rmsnorm_bwd("norm_ffn_bwd", dh3, x2, g_ffn, dx3, low)

    do = _matmul_nt("xattn_do", [dx2_low], [full["w_o"]], low)
    (d_w_o,) = _matmul_tn("grad_w_o", o, [dx2_low], WIRE_DTYPE)
    dq, dk, dvv = _attn_bwd(q, k, vv, do)
    (d_w_q,) = _matmul_tn("grad_w_q", h2, [dq], WIRE_DTYPE)
    dh2 = _matmul_nt("xattn_dh", [dq], [full["w_q"]], jnp.float32)
    d_w_k, d_w_v = _matmul_tn("grad_w_kv", mn, [dk, dvv], WIRE_DTYPE)
    dmn = _matmul_nt("xattn_dmem", [dk, dvv], [full["w_k"], full["w_v"]], jnp.float32)
    (d_mem_g,) = _rmsnorm_bwd("norm_mem_bwd", dmn, mems, g_mem, None, None)
    dx1, dx1_low, d_xattn_g = _rmsnorm_bwd("norm_xattn_bwd", dh2, x1, g_xattn, dx2, low)

    dz = _matmul_nt("mix_dz", [dx1_low], [full["w_out"]], jnp.float32)
    (d_w_out,) = _matmul_tn("grad_w_out", z, [dx1_low], WIRE_DTYPE)
    dproj, d_pw, d_ps, d_gs, d_ws, d_b = _mixer_bwd(proj, dz, pw, pool_scale, sgu_norm_g, ws, b_full)
    (d_w_in,) = _matmul_tn_cols("grad_w_in", h1, [dproj], N_CHIPS, WIRE_DTYPE)
    dh1 = _matmul_nt_cols("mix_dh", [dproj], [wg_in], jnp.float32)
    grad_x, d_mix_g = _rmsnorm_bwd("norm_mix_bwd", dh1, xs, g_mix, dx1, None)

    d_pool_w = d_pw.reshape(G, N_CHIPS, PG // N_CHIPS, PG).transpose(1, 0, 2, 3).reshape(N_CHIPS, PG, PG).astype(WIRE_DTYPE)
    partial = {
        "w_in": d_w_in, "pool_w": d_pool_w, "w_gate": d_w_gate, "w_up": d_w_up,
        "w_out": d_w_out, "w_q": d_w_q, "w_k": d_w_k, "w_v": d_w_v, "w_o": d_w_o, "w_down": d_w_down,
    }
    partial = [partial[n].reshape((N_CHIPS, -1, partial[n].shape[-1])) for n in BIG]
    received = _swap_halves(partial)
    chip_sums = [_chip_sum("chip_sum_" + n, p, r, WIRE_DTYPE, pos) for n, p, r in zip(BIG, partial, received)]
    others = _scatter_to_owners(chip_sums)
    reduced = [_owner_sum("owner_sum_" + n, cs, ot, pos) for n, cs, ot in zip(BIG, chip_sums, others)]
    grads = {n: g.reshape(-1, g.shape[-1]) for n, g in zip(BIG, _join_halves(reduced))}

    small_parts = {
        "norm_mix_g": d_mix_g, "pool_scale": d_ps, "sgu_norm_g": d_gs, "w_spatial": d_ws, "b_spatial": d_b,
        "norm_xattn_g": d_xattn_g, "norm_mem_g": d_mem_g, "norm_ffn_g": d_ffn_g, "final_norm_g": d_final_g,
    }
    row_counts = [small_parts[n].size // LANES for n in SMALL]
    packed = jnp.concatenate([_as_lanes(small_parts[n]) for n in SMALL], axis=0)
    small_sum = _allreduce_small(packed)

    out_grad, out_delta, out_m, out_v = {}, {}, {}, {}
    for n in BIG:
        shape = weights[n].shape
        g2d = grads[n]
        delta, m_new, v_new = _adamw("adamw_" + n, _as2d(weights[n]), g2d, _as2d(moments_m[n]), _as2d(moments_v[n]))
        out_grad[n], out_delta[n] = g2d.reshape(shape), delta.reshape(shape)
        out_m[n], out_v[n] = m_new.reshape(shape), v_new.reshape(shape)
    small_out = _adamw_small(
        small_sum, [_as_lanes(weights[n]) for n in SMALL], [_as_lanes(moments_m[n]) for n in SMALL],
        [_as_lanes(moments_v[n]) for n in SMALL], row_counts)
    for t, n in enumerate(SMALL):
        shape = weights[n].shape
        out_grad[n], out_delta[n], out_m[n], out_v[n] = (a.reshape(shape) for a in small_out[4 * t:4 * t + 4])

    loss = lax.psum(loss_part[0, 0], ("x", "y", "c"))
    return (loss, grad_x[None], *[out_grad[n] for n in WEIGHTS], *[out_delta[n] for n in WEIGHTS],
            *[out_m[n] for n in WEIGHTS], *[out_v[n] for n in WEIGHTS])
```

```python
import math

import jax
import jax.numpy as jnp
from jax import lax
from jax.experimental import pallas as pl
from jax.experimental.pallas import tpu as pltpu

MXU_DTYPE = jnp.bfloat16
WIRE_DTYPE = jnp.bfloat16

EPS = 1e-6
CHUNK = 64
POOL_WINDOWS = (2, 4, 8, 16)
POOL_HALO = 16
N_XATTN_HEADS = 4
ADAM_LR = 0.001
ADAM_B1 = 0.9
ADAM_B2 = 0.999
ADAM_EPS = 1e-08
ADAM_WD = 0.01
ADAM_STEP = 10

N_CHIPS = 4
N_DEV = 8
LANES = 128
VMEM_LIMIT_BYTES = 56 * 2 ** 20
COPY_CHUNK_BYTES = 512 * 2 ** 10
COPY_ROW_ALIGN = 16

MESH = pl.DeviceIdType.MESH
ANY = pl.BlockSpec(memory_space=pl.ANY)
VMEM_WHOLE = pl.BlockSpec(memory_space=pltpu.VMEM)
HBM_SPEC = pl.BlockSpec(memory_space=pltpu.HBM)
SEM_SPEC = pl.BlockSpec(memory_space=pltpu.SEMAPHORE)
SIDE_EFFECT = pltpu.SideEffectType.DATAFLOW_SIDE_EFFECTING

NN = (((1,), (0,)), ((), ()))
NT = (((1,), (1,)), ((), ()))
TN = (((0,), (0,)), ((), ()))


def _tile(n, target, mult):
    best = None
    for t in range(mult, min(n, target) + 1, mult):
        if n % t == 0:
            best = t
    return n if best is None else best


def _params(n_grid_axes):
    return pltpu.CompilerParams(dimension_semantics=("arbitrary",) * n_grid_axes, vmem_limit_bytes=VMEM_LIMIT_BYTES)


def _rsqrt_mean_sq(x):
    return lax.rsqrt(jnp.mean(x * x, axis=-1, keepdims=True) + EPS)


def _sum_all(x):
    return jnp.sum(jnp.sum(x, axis=1, keepdims=True), axis=0, keepdims=True)


def _sigmoid(x):
    return 1.0 / (1.0 + jnp.exp(-x))


def _fused_matmul(name, grid, operands, pairs, acc_shapes, extras, outs, epilogue):
    nk = grid[2]
    n_op, n_ex, n_out, n_acc = len(operands), len(extras), len(outs), len(acc_shapes)

    def body(*refs):
        op_refs = refs[:n_op]
        ex_refs = refs[n_op:n_op + n_ex]
        out_refs = refs[n_op + n_ex:n_op + n_ex + n_out]
        acc_refs = refs[n_op + n_ex + n_out:]
        i, j, k = pl.program_id(0), pl.program_id(1), pl.program_id(2)
        loaded = {}

        def operand(n):
            if n not in loaded:
                loaded[n] = op_refs[n][...].astype(MXU_DTYPE)
            return loaded[n]

        parts = [None] * n_acc
        for ia, ib, iacc, dims in pairs:
            d = lax.dot_general(operand(ia), operand(ib), dims, preferred_element_type=jnp.float32)
            parts[iacc] = d if parts[iacc] is None else parts[iacc] + d

        if nk == 1:
            epilogue(parts, ex_refs, out_refs, i, j)
            return

        @pl.when(k == 0)
        def _():
            for a in range(n_acc):
                acc_refs[a][...] = parts[a]

        @pl.when(k > 0)
        def _():
            for a in range(n_acc):
                acc_refs[a][...] += parts[a]

        @pl.when(k == nk - 1)
        def _():
            epilogue([acc_refs[a][...] for a in range(n_acc)], ex_refs, out_refs, i, j)

    arrays = [o[0] for o in operands] + [e[0] for e in extras]
    in_specs = [pl.BlockSpec(o[1], o[2]) for o in operands]
    in_specs += [ANY if e[1] is None else pl.BlockSpec(e[1], e[2]) for e in extras]
    return pl.pallas_call(
        body,
        name=name,
        grid=grid,
        in_specs=in_specs,
        out_specs=[pl.BlockSpec(o[2], o[3]) for o in outs],
        out_shape=[jax.ShapeDtypeStruct(o[0], o[1]) for o in outs],
        scratch_shapes=[pltpu.VMEM(s, jnp.float32) for s in acc_shapes] if nk > 1 else [],
        compiler_params=_params(3),
    )(*arrays)


def _store_epilogue(accs, ex_refs, out_refs, i, j):
    for a, o in zip(accs, out_refs):
        o[...] = a.astype(o.dtype)


def _matmul_nn(name, a, b, out_dtype, tm=1024, tn=512):
    M, K = a.shape
    N = b.shape[1]
    tm, tn = _tile(M, tm, 16), _tile(N, tn, LANES)
    return _fused_matmul(
        name, (M // tm, N // tn, 1),
        [(a, (tm, K), lambda i, j, k: (i, 0)), (b, (K, tn), lambda i, j, k: (0, j))],
        [(0, 1, 0, NN)], [(tm, tn)], [],
        [((M, N), out_dtype, (tm, tn), lambda i, j, k: (i, j))], _store_epilogue)[0]


def _matmul_nn_cols(name, a, b3, out_dtype, tm=1024):
    M, K = a.shape
    J, _, Nc = b3.shape
    tm = _tile(M, tm, 16)
    return _fused_matmul(
        name, (M // tm, J, 1),
        [(a, (tm, K), lambda i, j, k: (i, 0)), (b3, (None, K, Nc), lambda i, j, k: (j, 0, 0))],
        [(0, 1, 0, NN)], [(tm, Nc)], [],
        [((M, J * Nc), out_dtype, (tm, Nc), lambda i, j, k: (i, j))], _store_epilogue)[0]


def _order_after(after):
    return [] if after is None else [(after, None, None)]


def _matmul_nt(name, a_list, b_list, out_dtype, tm=1024, tn=512, after=None):
    M, K = a_list[0].shape
    N = b_list[0].shape[0]
    tm, tn = _tile(M, tm, 16), _tile(N, tn, LANES)
    n = len(a_list)
    operands = [(a, (tm, K), lambda i, j, k: (i, 0)) for a in a_list]
    operands += [(b, (tn, K), lambda i, j, k: (j, 0)) for b in b_list]
    return _fused_matmul(
        name, (M // tm, N // tn, 1), operands,
        [(p, n + p, 0, NT) for p in range(n)], [(tm, tn)], _order_after(after),
        [((M, N), out_dtype, (tm, tn), lambda i, j, k: (i, j))], _store_epilogue)[0]


def _matmul_nt_cols(name, a_list, b3_list, out_dtype, tm=1024, tn=512, after=None):
    M = a_list[0].shape[0]
    J, N, Kc = b3_list[0].shape
    tm, tn = _tile(M, tm, 16), _tile(N, tn, LANES)
    n = len(a_list)
    operands = [(a, (tm, Kc), lambda i, j, k: (i, k)) for a in a_list]
    operands += [(b, (None, tn, Kc), lambda i, j, k: (k, j, 0)) for b in b3_list]
    return _fused_matmul(
        name, (M // tm, N // tn, J), operands,
        [(p, n + p, 0, NT) for p in range(n)], [(tm, tn)], _order_after(after),
        [((M, N), out_dtype, (tm, tn), lambda i, j, k: (i, j))], _store_epilogue)[0]


def _matmul_tn(name, a, b_list, out_dtype, tka=1024, tn=1024, tkm=1024):
    M, Ka = a.shape
    N = b_list[0].shape[1]
    tka, tn, tkm = _tile(Ka, tka, LANES), _tile(N, tn, LANES), _tile(M, tkm, 16)
    n = len(b_list)
    operands = [(a, (tkm, tka), lambda i, j, k: (k, i))]
    operands += [(b, (tkm, tn), lambda i, j, k: (k, j)) for b in b_list]
    return _fused_matmul(
        name, (Ka // tka, N // tn, M // tkm), operands,
        [(0, 1 + p, p, TN) for p in range(n)], [(tka, tn)] * n, [],
        [((Ka, N), out_dtype, (tka, tn), lambda i, j, k: (i, j))] * n, _store_epilogue)


def _matmul_tn_cols(name, a, b_list, n_shards, out_dtype, tka=512, tkm=1024):
    M, Ka = a.shape
    Nc = b_list[0].shape[1] // n_shards
    tka, tkm = _tile(Ka, tka, LANES), _tile(M, tkm, 16)
    n = len(b_list)
    operands = [(a, (tkm, tka), lambda i, j, k: (k, i))]
    operands += [(b, (tkm, Nc), lambda i, j, k: (k, j)) for b in b_list]
    return _fused_matmul(
        name, (Ka // tka, n_shards, M // tkm), operands,
        [(0, 1 + p, p, TN) for p in range(n)], [(tka, Nc)] * n, [],
        [((n_shards, Ka, Nc), out_dtype, (None, tka, Nc), lambda i, j, k: (j, i, 0))] * n, _store_epilogue)


def _rmsnorm(name, x, g_row, out_dtype, tm=512):
    S, D = x.shape
    tm = _tile(S, tm, 16)

    def body(x_ref, g_ref, o_ref):
        xv = x_ref[...]
        o_ref[...] = (xv * _rsqrt_mean_sq(xv) * g_ref[...]).astype(o_ref.dtype)

    return pl.pallas_call(
        body, name=name, grid=(S // tm,),
        in_specs=[pl.BlockSpec((tm, D), lambda i: (i, 0)), pl.BlockSpec((1, D), lambda i: (0, 0))],
        out_specs=pl.BlockSpec((tm, D), lambda i: (i, 0)),
        out_shape=jax.ShapeDtypeStruct((S, D), out_dtype),
        compiler_params=_params(1),
    )(x, g_row)


def _rmsnorm_bwd_values(dh, xin, g):
    r = _rsqrt_mean_sq(xin)
    xhat = xin * r
    dhg = dh * g
    dx = r * (dhg - xhat * jnp.mean(dhg * xhat, axis=-1, keepdims=True))
    return dx, jnp.sum(dh * xhat, axis=0, keepdims=True)


def _rmsnorm_bwd(name, dh, xin, g_row, dres, low_dtype, tm=256):
    S, D = xin.shape
    tm = _tile(S, tm, 16)
    has_dx = dres is not None
    has_low = has_dx and low_dtype is not None

    def body(*refs):
        dh_ref, x_ref, g_ref = refs[:3]
        rest = refs[3:]
        dx, dg = _rmsnorm_bwd_values(dh_ref[...], x_ref[...], g_ref[...])
        if has_dx:
            dres_ref, rest = rest[0], rest[1:]
            dx = dx + dres_ref[...]
            rest[0][...] = dx
            if has_low:
                rest[1][...] = dx.astype(low_dtype)
        dg_ref = rest[-1]

        @pl.when(pl.program_id(0) == 0)
        def _():
            dg_ref[...] = jnp.zeros_like(dg_ref)

        dg_ref[...] += dg

    row = pl.BlockSpec((tm, D), lambda i: (i, 0))
    vec = pl.BlockSpec((1, D), lambda i: (0, 0))
    ins, in_specs = [dh, xin, g_row], [row, row, vec]
    out_shape, out_specs = [], []
    if has_dx:
        ins.append(dres)
        in_specs.append(row)
        out_shape.append(jax.ShapeDtypeStruct((S, D), jnp.float32))
        out_specs.append(row)
        if has_low:
            out_shape.append(jax.ShapeDtypeStruct((S, D), low_dtype))
            out_specs.append(row)
    out_shape.append(jax.ShapeDtypeStruct((1, D), jnp.float32))
    out_specs.append(vec)
    return pl.pallas_call(
        body, name=name, grid=(S // tm,), in_specs=in_specs, out_specs=out_specs, out_shape=out_shape,
        compiler_params=_params(1),
    )(*ins)


def _chunk_mask(blk):
    t = lax.broadcasted_iota(jnp.int32, (blk, blk), 0)
    s = lax.broadcasted_iota(jnp.int32, (blk, blk), 1)
    return (s // CHUNK) <= (t // CHUNK)


def _pool_inputs(a, halo, row0, tm, g, pg):
    cols = slice(g * pg, (g + 1) * pg)
    w = POOL_WINDOWS[g]
    s = jnp.concatenate([halo[:, cols], a[:, cols]], axis=0)
    span = 1
    while span < w:
        s = s + pltpu.roll(s, span, axis=0)
        span *= 2
    t = lax.broadcasted_iota(jnp.int32, (tm, 1), 0) + row0
    cnt = jnp.minimum(t + 1, w).astype(jnp.float32)
    return s[POOL_HALO:] / cnt - a[:, cols]


def _mixer_specs(S, tm, DP, DS):
    per_halo = tm // POOL_HALO
    n_halo = S // POOL_HALO
    a_spec = pl.BlockSpec((tm, DP), lambda i: (i, 0))
    prev_spec = pl.BlockSpec((POOL_HALO, DP), lambda i: (jnp.maximum(i * per_halo - 1, 0), 0))
    next_spec = pl.BlockSpec((POOL_HALO, DP), lambda i: (jnp.minimum((i + 1) * per_halo, n_halo - 1), 0))
    u_spec = pl.BlockSpec((tm, DS), lambda i: (i, 1))
    v_spec = pl.BlockSpec((tm, DS), lambda i: (i, 2))
    return a_spec, prev_spec, next_spec, u_spec, v_spec


def _whole(arr):
    nd = arr.ndim
    return pl.BlockSpec(arr.shape, lambda i: (0,) * nd)


def _mixer_fwd(proj, pw, ps_row, gs_row, ws, b_full, tm=512):
    S = proj.shape[0]
    G, PG, _ = pw.shape
    DP = G * PG
    H, BLK, _ = ws.shape
    DS = gs_row.shape[1]
    HD = DS // H
    assert DP == DS and proj.shape[1] == DP + 2 * DS
    tm = _tile(S, tm, BLK)

    def body(a_ref, prev_ref, u_ref, v_ref, pw_ref, ps_ref, gs_ref, ws_ref, b_ref, z_ref):
        i = pl.program_id(0)
        a = a_ref[...]
        halo = jnp.where(i > 0, prev_ref[...], 0.0)
        for g in range(G):
            cols = slice(g * PG, (g + 1) * PG)
            p = _pool_inputs(a, halo, i * tm, tm, g, PG)
            y = jnp.dot(p.astype(MXU_DTYPE), pw_ref[g], preferred_element_type=jnp.float32) * ps_ref[:, cols]
            z_ref[:, cols] = y.astype(z_ref.dtype)
        v = v_ref[...]
        vn = (v * _rsqrt_mean_sq(v) * gs_ref[...]).astype(MXU_DTYPE)
        u = u_ref[...]
        mask = _chunk_mask(BLK)
        for h in range(H):
            wm = jnp.where(mask, ws_ref[h], 0.0).astype(MXU_DTYPE)
            bias = b_ref[h]
            cols = slice(h * HD, (h + 1) * HD)
            for n in range(tm // BLK):
                rows = slice(n * BLK, (n + 1) * BLK)
                mixed = jnp.dot(wm, vn[rows, cols], preferred_element_type=jnp.float32) + bias
                z_ref[rows, DP + h * HD:DP + (h + 1) * HD] = (u[rows, cols] * mixed).astype(z_ref.dtype)

    a_spec, prev_spec, _, u_spec, v_spec = _mixer_specs(S, tm, DP, DS)
    return pl.pallas_call(
        body, name="mixer_fwd", grid=(S // tm,),
        in_specs=[a_spec, prev_spec, u_spec, v_spec, _whole(pw), _whole(ps_row), _whole(gs_row), _whole(ws), _whole(b_full)],
        out_specs=pl.BlockSpec((tm, DP + DS), lambda i: (i, 0)),
        out_shape=jax.ShapeDtypeStruct((S, DP + DS), MXU_DTYPE),
        compiler_params=_params(1),
    )(proj, proj, proj, proj, pw, ps_row, gs_row, ws, b_full)


def _mixer_bwd(proj, dz, pw, ps_row, gs_row, ws, b_full, tm=256):
    S = proj.shape[0]
    G, PG, _ = pw.shape
    DP = G * PG
    H, BLK, _ = ws.shape
    DS = gs_row.shape[1]
    HD = DS // H
    tm = _tile(S, tm, BLK)
    nb = S // tm
    ext = tm + POOL_HALO

    def body(a_ref, prev_ref, u_ref, v_ref, dzp_ref, dzp_next_ref, dzs_ref, pw_ref, ps_ref, gs_ref, ws_ref, b_ref,
             dproj_ref, dpw_ref, dps_ref, dgs_ref, dws_ref, db_ref, dvn_ref):
        i = pl.program_id(0)

        @pl.when(i == 0)
        def _():
            for r in (dpw_ref, dps_ref, dgs_ref, dws_ref, db_ref):
                r[...] = jnp.zeros_like(r)

        a = a_ref[...]
        halo = jnp.where(i > 0, prev_ref[...], 0.0)
        dyp = dzp_ref[...]
        dyp_next = jnp.where(i < nb - 1, dzp_next_ref[...], 0.0)
        dq_ext = jnp.concatenate([dyp, dyp_next], axis=0) * ps_ref[...]
        t_ext = lax.broadcasted_iota(jnp.int32, (ext, 1), 0) + i * tm
        for g in range(G):
            cols = slice(g * PG, (g + 1) * PG)
            w = POOL_WINDOWS[g]
            pb = _pool_inputs(a, halo, i * tm, tm, g, PG).astype(MXU_DTYPE)
            y_pre = jnp.dot(pb, pw_ref[g], preferred_element_type=jnp.float32)
            dps_ref[:, cols] += jnp.sum(dyp[:, cols] * y_pre, axis=0, keepdims=True)
            dqb = dq_ext[:, cols].astype(MXU_DTYPE)
            dpw_ref[g] += lax.dot_general(pb, dqb[:tm], TN, preferred_element_type=jnp.float32)
            dp = lax.dot_general(dqb, pw_ref[g], NT, preferred_element_type=jnp.float32)
            f = dp / jnp.minimum(t_ext + 1, w).astype(jnp.float32)
            span = 1
            while span < w:
                f = f + pltpu.roll(f, ext - span, axis=0)
                span *= 2
            dproj_ref[:, cols] = (f[:tm] - dp[:tm]).astype(dproj_ref.dtype)

        v = v_ref[...]
        rv = _rsqrt_mean_sq(v)
        vhat = v * rv
        gs = gs_ref[...]
        vnb = (vhat * gs).astype(MXU_DTYPE)
        u = u_ref[...]
        dys = dzs_ref[...]
        mask = _chunk_mask(BLK)
        for h in range(H):
            wm = jnp.where(mask, ws_ref[h], 0.0).astype(MXU_DTYPE)
            bias = b_ref[h]
            cols = slice(h * HD, (h + 1) * HD)
            dm_sum = jnp.zeros((BLK, HD), jnp.float32)
            dws_sum = jnp.zeros((BLK, BLK), jnp.float32)
            for n in range(tm // BLK):
                rows = slice(n * BLK, (n + 1) * BLK)
                vblk = vnb[rows, cols]
                mixed = jnp.dot(wm, vblk, preferred_element_type=jnp.float32) + bias
                dy = dys[rows, cols]
                dproj_ref[rows, DP + h * HD:DP + (h + 1) * HD] = (dy * mixed).astype(dproj_ref.dtype)
                dm = dy * u[rows, cols]
                dm_sum = dm_sum + dm
                dmb = dm.astype(MXU_DTYPE)
                dws_sum = dws_sum + lax.dot_general(dmb, vblk, NT, preferred_element_type=jnp.float32)
                dvn_ref[rows, cols] = lax.dot_general(wm, dmb, TN, preferred_element_type=jnp.float32)
            dws_ref[h] += jnp.where(mask, dws_sum, 0.0)
            db_ref[h:h + 1, :] += jnp.sum(dm_sum.T, axis=0, keepdims=True)
        dvn = dvn_ref[...]
        dvg = dvn * gs
        dv = rv * (dvg - vhat * jnp.mean(dvg * vhat, axis=-1, keepdims=True))
        dgs_ref[...] += jnp.sum(dvn * vhat, axis=0, keepdims=True)
        dproj_ref[:, DP + DS:] = dv.astype(dproj_ref.dtype)

    a_spec, prev_spec, next_spec, u_spec, v_spec = _mixer_specs(S, tm, DP, DS)
    dzp_spec = pl.BlockSpec((tm, DP), lambda i: (i, 0))
    dzs_spec = pl.BlockSpec((tm, DS), lambda i: (i, 1))
    small = [pw, ps_row, gs_row, ws, b_full]
    out_shape = [
        jax.ShapeDtypeStruct((S, DP + 2 * DS), MXU_DTYPE),
        jax.ShapeDtypeStruct(pw.shape, jnp.float32),
        jax.ShapeDtypeStruct(ps_row.shape, jnp.float32),
        jax.ShapeDtypeStruct(gs_row.shape, jnp.float32),
        jax.ShapeDtypeStruct(ws.shape, jnp.float32),
        jax.ShapeDtypeStruct((H, BLK), jnp.float32),
    ]
    out_specs = [pl.BlockSpec((tm, DP + 2 * DS), lambda i: (i, 0))] + [_whole(s) for s in out_shape[1:]]
    return pl.pallas_call(
        body, name="mixer_bwd", grid=(nb,),
        in_specs=[a_spec, prev_spec, u_spec, v_spec, dzp_spec, next_spec, dzs_spec] + [_whole(s) for s in small],
        out_specs=out_specs, out_shape=out_shape,
        scratch_shapes=[pltpu.VMEM((tm, DS), jnp.float32)],
        compiler_params=_params(1),
    )(proj, proj, proj, proj, dz, dz, dz, *small)


def _softmax_rows(q, k, scale):
    s = lax.dot_general(q, k, NT, preferred_element_type=jnp.float32) * scale
    e = jnp.exp(s - jnp.max(s, axis=-1, keepdims=True))
    return e / jnp.sum(e, axis=-1, keepdims=True)


def _attn_fwd(q, k, v, tm=512):
    S, D = q.shape
    M = k.shape[0]
    HD = D // N_XATTN_HEADS
    scale = HD ** -0.5
    tm = _tile(S, tm, 16)

    def body(q_ref, k_ref, v_ref, o_ref):
        for h in range(N_XATTN_HEADS):
            cols = slice(h * HD, (h + 1) * HD)
            p = _softmax_rows(q_ref[:, cols], k_ref[:, cols], scale)
            o = jnp.dot(p.astype(MXU_DTYPE), v_ref[:, cols], preferred_element_type=jnp.float32)
            o_ref[:, cols] = o.astype(o_ref.dtype)

    row = pl.BlockSpec((tm, D), lambda i: (i, 0))
    mem = pl.BlockSpec((M, D), lambda i: (0, 0))
    return pl.pallas_call(
        body, name="attn_fwd", grid=(S // tm,), in_specs=[row, mem, mem], out_specs=row,
        out_shape=jax.ShapeDtypeStruct((S, D), MXU_DTYPE), compiler_params=_params(1),
    )(q, k, v)


def _attn_bwd(q, k, v, do, tm=512):
    S, D = q.shape
    M = k.shape[0]
    HD = D // N_XATTN_HEADS
    scale = HD ** -0.5
    tm = _tile(S, tm, 16)

    def body(q_ref, k_ref, v_ref, do_ref, dq_ref, dk_ref, dv_ref):
        @pl.when(pl.program_id(0) == 0)
        def _():
            dk_ref[...] = jnp.zeros_like(dk_ref)
            dv_ref[...] = jnp.zeros_like(dv_ref)

        for h in range(N_XATTN_HEADS):
            cols = slice(h * HD, (h + 1) * HD)
            qh, kh, vh, doh = q_ref[:, cols], k_ref[:, cols], v_ref[:, cols], do_ref[:, cols]
            p = _softmax_rows(qh, kh, scale)
            dp = lax.dot_general(doh, vh, NT, preferred_element_type=jnp.float32)
            dv_ref[:, cols] += lax.dot_general(p.astype(MXU_DTYPE), doh, TN, preferred_element_type=jnp.float32)
            ds = (p * (dp - jnp.sum(dp * p, axis=-1, keepdims=True)) * scale).astype(MXU_DTYPE)
            dq_ref[:, cols] = jnp.dot(ds, kh, preferred_element_type=jnp.float32).astype(dq_ref.dtype)
            dk_ref[:, cols] += lax.dot_general(ds, qh, TN, preferred_element_type=jnp.float32)

    row = pl.BlockSpec((tm, D), lambda i: (i, 0))
    mem = pl.BlockSpec((M, D), lambda i: (0, 0))
    return pl.pallas_call(
        body, name="attn_bwd", grid=(S // tm,), in_specs=[row, mem, mem, row], out_specs=[row, mem, mem],
        out_shape=[jax.ShapeDtypeStruct((S, D), MXU_DTYPE), jax.ShapeDtypeStruct((M, D), jnp.float32),
                   jax.ShapeDtypeStruct((M, D), jnp.float32)],
        compiler_params=_params(1),
    )(q, k, v, do)


def _proj_residual_norm(name, a, w2d, resid, g_row, tm=256):
    M, K = a.shape
    N = w2d.shape[1]
    tm = _tile(M, tm, 16)

    def epilogue(accs, ex_refs, out_refs, i, j):
        x_new = ex_refs[0][...] + accs[0]
        out_refs[0][...] = x_new
        out_refs[1][...] = (x_new * _rsqrt_mean_sq(x_new) * ex_refs[1][...]).astype(MXU_DTYPE)

    row = lambda i, j, k: (i, 0)
    return _fused_matmul(
        name, (M // tm, 1, 1),
        [(a, (tm, K), row), (w2d, (K, N), lambda i, j, k: (0, 0))],
        [(0, 1, 0, NN)], [(tm, N)],
        [(resid, (tm, N), row), (g_row, (1, N), lambda i, j, k: (0, 0))],
        [((M, N), jnp.float32, (tm, N), row), ((M, N), MXU_DTYPE, (tm, N), row)], epilogue)


def _ffn_gate_up(h, wg3, wu3, tm=512):
    M, K = h.shape
    J, _, Nc = wg3.shape
    tm = _tile(M, tm, 16)

    def epilogue(accs, ex_refs, out_refs, i, j):
        gate, up = accs
        out_refs[0][...] = gate
        out_refs[1][...] = up
        out_refs[2][...] = (gate * _sigmoid(gate) * up).astype(MXU_DTYPE)

    col = lambda i, j, k: (i, j)
    wspec = lambda i, j, k: (j, 0, 0)
    return _fused_matmul(
        "ffn_gate_up", (M // tm, J, 1),
        [(h, (tm, K), lambda i, j, k: (i, 0)), (wg3, (None, K, Nc), wspec), (wu3, (None, K, Nc), wspec)],
        [(0, 1, 0, NN), (0, 2, 1, NN)], [(tm, Nc)] * 2, [],
        [((M, J * Nc), jnp.float32, (tm, Nc), col), ((M, J * Nc), jnp.float32, (tm, Nc), col),
         ((M, J * Nc), MXU_DTYPE, (tm, Nc), col)], epilogue)


def _ffn_down_loss(act, wd2d, x2, target, g_row, n_k, tm=256):
    M, F = act.shape
    D = wd2d.shape[1]
    tm = _tile(M, tm, 16)
    tk = F // n_k

    def epilogue(accs, ex_refs, out_refs, i, j):
        x3 = ex_refs[0][...] + accs[0]
        g = ex_refs[2][...]
        r = _rsqrt_mean_sq(x3)
        xhat = x3 * r
        diff = xhat * g - ex_refs[1][...]
        dy = diff / D
        dyg = dy * g
        dx = r * (dyg - xhat * jnp.mean(dyg * xhat, axis=-1, keepdims=True))
        out_refs[0][...] = dx
        out_refs[1][...] = dx.astype(MXU_DTYPE)

        @pl.when(i == 0)
        def _():
            out_refs[2][...] = jnp.zeros_like(out_refs[2])
            out_refs[3][...] = jnp.zeros_like(out_refs[3])

        out_refs[2][...] += 0.5 * _sum_all(jnp.mean(diff * diff, axis=-1, keepdims=True))
        out_refs[3][...] += jnp.sum(dy * xhat, axis=0, keepdims=True)

    row = lambda i, j, k: (i, 0)
    const = lambda i, j, k: (0, 0)
    return _fused_matmul(
        "ffn_down_loss", (M // tm, 1, n_k),
        [(act, (tm, tk), lambda i, j, k: (i, k)), (wd2d, (tk, D), lambda i, j, k: (k, 0))],
        [(0, 1, 0, NN)], [(tm, D)],
        [(x2, (tm, D), row), (target, (tm, D), row), (g_row, (1, D), const)],
        [((M, D), jnp.float32, (tm, D), row), ((M, D), MXU_DTYPE, (tm, D), row),
         ((1, 1), jnp.float32, (1, 1), const), ((1, D), jnp.float32, (1, D), const)], epilogue)


def _ffn_down_bwd(dx3_low, wd2d, gate, up, n_shards, tm=512):
    M, D = dx3_low.shape
    F = wd2d.shape[0]
    tn = F // n_shards
    tm = _tile(M, tm, 16)

    def epilogue(accs, ex_refs, out_refs, i, j):
        dact = accs[0]
        gate_v, up_v = ex_refs[0][...], ex_refs[1][...]
        sg = _sigmoid(gate_v)
        out_refs[0][...] = (dact * up_v * (sg * (1.0 + gate_v * (1.0 - sg)))).astype(MXU_DTYPE)
        out_refs[1][...] = (dact * (gate_v * sg)).astype(MXU_DTYPE)

    col = lambda i, j, k: (i, j)
    return _fused_matmul(
        "ffn_down_bwd", (M // tm, n_shards, 1),
        [(dx3_low, (tm, D), lambda i, j, k: (i, 0)), (wd2d, (tn, D), lambda i, j, k: (j, 0))],
        [(0, 1, 0, NT)], [(tm, tn)],
        [(gate, (tm, tn), col), (up, (tm, tn), col)],
        [((M, F), MXU_DTYPE, (tm, tn), col)] * 2, epilogue)


def _row_tile(R, C, target_bytes=2 ** 21):
    return _tile(R, max(16, target_bytes // (4 * C)), 16)


def _prefetch_call(body, name, grid, pos, in_specs, out_specs, out_shape, ins):
    grid_spec = pltpu.PrefetchScalarGridSpec(
        num_scalar_prefetch=1, grid=grid, in_specs=in_specs, out_specs=out_specs)
    return pl.pallas_call(body, name=name, grid_spec=grid_spec, out_shape=out_shape,
                          compiler_params=_params(len(grid)))(pos, *ins)


def _cast_into_slab(name, x2d, dtype, pos):
    R, C = x2d.shape
    tr = _row_tile(R, C)

    def body(pos_ref, x_ref, o_ref):
        o_ref[...] = x_ref[...].astype(o_ref.dtype)

    return _prefetch_call(
        body, name, (R // tr,), pos, [pl.BlockSpec((tr, C), lambda i, pos: (i, 0))],
        pl.BlockSpec((None, tr, C), lambda i, pos: (pos[0], i, 0)),
        jax.ShapeDtypeStruct((N_CHIPS, R, C), dtype), [x2d])


def _chip_sum(name, partial, got, dtype, pos):
    J, R, C = partial.shape
    Rh = R // 2
    tr = _row_tile(Rh, C)

    def body(pos_ref, a_ref, b_ref, o_ref):
        o_ref[...] = (a_ref[...].astype(jnp.float32) + b_ref[...].astype(jnp.float32)).astype(o_ref.dtype)

    blk = pl.BlockSpec((None, tr, C), lambda j, i, pos: (j, i, 0))
    return _prefetch_call(
        body, name, (J, Rh // tr), pos,
        [pl.BlockSpec((None, None, tr, C), lambda j, i, pos: (j, pos[1], i, 0)), blk], blk,
        jax.ShapeDtypeStruct((J, Rh, C), dtype), [partial.reshape(J, 2, Rh, C), got])


def _owner_sum(name, chip_sums, others, pos):
    J, Rh, C = chip_sums.shape
    n = others.shape[0]
    tr = _row_tile(Rh, C * (n + 1))

    def body(pos_ref, own_ref, oth_ref, o_ref):
        acc = own_ref[...].astype(jnp.float32)
        for k in range(n):
            acc = acc + oth_ref[k].astype(jnp.float32)
        o_ref[...] = acc

    return _prefetch_call(
        body, name, (Rh // tr,), pos,
        [pl.BlockSpec((None, tr, C), lambda i, pos: (pos[0], i, 0)), pl.BlockSpec((n, tr, C), lambda i, pos: (0, i, 0))],
        pl.BlockSpec((None, tr, C), lambda i, pos: (pos[1], i, 0)),
        jax.ShapeDtypeStruct((2, Rh, C), jnp.float32), [chip_sums, others])


def _adamw_values(w, g, m, v):
    m = ADAM_B1 * m + (1.0 - ADAM_B1) * g
    v = ADAM_B2 * v + (1.0 - ADAM_B2) * (g * g)
    m_hat = m / (1.0 - ADAM_B1 ** ADAM_STEP)
    v_hat = v / (1.0 - ADAM_B2 ** ADAM_STEP)
    delta = -ADAM_LR * (m_hat / (jnp.sqrt(v_hat) + ADAM_EPS) + ADAM_WD * w)
    return delta, m, v


def _adamw(name, w, g, m, v):
    R, C = w.shape
    tr = _row_tile(R, C, 2 ** 20)

    def body(w_ref, g_ref, m_ref, v_ref, d_ref, mo_ref, vo_ref):
        d_ref[...], mo_ref[...], vo_ref[...] = _adamw_values(w_ref[...], g_ref[...], m_ref[...], v_ref[...])

    blk = pl.BlockSpec((tr, C), lambda i: (i, 0))
    return pl.pallas_call(
        body, name=name, grid=(R // tr,), in_specs=[blk] * 4, out_specs=[blk] * 3,
        out_shape=[jax.ShapeDtypeStruct((R, C), jnp.float32)] * 3, compiler_params=_params(1))(w, g, m, v)


def _adamw_small(g_packed, ws, ms, vs, row_counts):
    n = len(ws)

    def body(*refs):
        g_ref = refs[0]
        w_refs, m_refs, v_refs = refs[1:1 + n], refs[1 + n:1 + 2 * n], refs[1 + 2 * n:1 + 3 * n]
        outs = refs[1 + 3 * n:]
        off = 0
        for t in range(n):
            g = g_ref[off:off + row_counts[t], :]
            off += row_counts[t]
            delta, m_new, v_new = _adamw_values(w_refs[t][...], g, m_refs[t][...], v_refs[t][...])
            outs[4 * t][...] = g
            outs[4 * t + 1][...] = delta
            outs[4 * t + 2][...] = m_new
            outs[4 * t + 3][...] = v_new

    out_shape = []
    for t in range(n):
        out_shape += [jax.ShapeDtypeStruct((row_counts[t], LANES), jnp.float32)] * 4
    return pl.pallas_call(
        body, name="adamw_small", in_specs=[VMEM_WHOLE] * (1 + 3 * n), out_specs=[VMEM_WHOLE] * (4 * n),
        out_shape=out_shape, compiler_params=pltpu.CompilerParams(vmem_limit_bytes=VMEM_LIMIT_BYTES),
    )(g_packed, *ws, *ms, *vs)


def _position():
    x, y, c = lax.axis_index("x"), lax.axis_index("y"), lax.axis_index("c")
    other_chips = [(1 - x, y), (x, 1 - y), (1 - x, 1 - y)]
    return x, y, c, other_chips


def _half(rows, which):
    return pl.ds(which * (rows // 2), rows // 2)


def _chunk_index(shape, dtype):
    rows = shape[-2]
    total = math.prod(shape) * jnp.dtype(dtype).itemsize
    n = max(1, min(rows // COPY_ROW_ALIGN, total // COPY_CHUNK_BYTES))
    step = -(-rows // n)
    step = -(-step // COPY_ROW_ALIGN) * COPY_ROW_ALIGN
    lead = (slice(None),) * (len(shape) - 2)
    return [lead + (pl.ds(s, min(step, rows - s)),) for s in range(0, rows, step)]


def _start_in_chunks(make, src, dst, waited_here=True):
    for idx in _chunk_index(src.shape, src.dtype):
        make(src.at[idx], dst.at[idx]).start()
    return make(src, dst) if waited_here else None


def _remote_copier(send_sem, recv_sem, to):
    return lambda src, dst: pltpu.make_async_remote_copy(
        src_ref=src, dst_ref=dst, send_sem=send_sem, recv_sem=recv_sem, device_id=to, device_id_type=MESH)


def _gather_weights(slabs):
    T = len(slabs)

    def body(*refs):
        outs = refs[T:2 * T]
        send_sems, recv_sems = refs[2 * T:]
        x, y, c, other_chips = _position()
        mine = 2 * x + y
        sibling = (x, y, 1 - c)

        def block(t, chip, which):
            return outs[t].at[chip, _half(outs[t].shape[1], which)]

        def copier(t, k, to):
            return _remote_copier(send_sems.at[t, k], recv_sems.at[t, k], to)

        started = []
        for t in range(T):
            for k, (cx, cy) in enumerate(other_chips):
                started.append(_start_in_chunks(copier(t, k, (cx, cy, c)), block(t, mine, c), block(t, mine, c)))
        for k, (cx, cy) in enumerate(other_chips):
            for t in range(T):
                landed = block(t, 2 * cx + cy, c)
                copier(t, k, (cx, cy, c))(landed, landed).wait_recv()
                started.append(_start_in_chunks(copier(t, 3 + k, sibling), landed, landed))
        for k, (cx, cy) in enumerate(other_chips):
            for t in range(T):
                arriving = block(t, 2 * cx + cy, 1 - c)
                copier(t, 3 + k, sibling)(arriving, arriving).wait_recv()
        for cp in started:
            cp.wait_send()

    return pl.pallas_call(
        body, name="gather_weights", in_specs=[ANY] * T, out_specs=[ANY] * T,
        out_shape=[jax.ShapeDtypeStruct(s.shape, s.dtype) for s in slabs],
        input_output_aliases={t: t for t in range(T)},
        scratch_shapes=[pltpu.SemaphoreType.DMA((T, 6)), pltpu.SemaphoreType.DMA((T, 6))],
    )(*slabs)


def _swap_halves(name, grads):
    T = len(grads)

    def body(*refs):
        ins, got = refs[:T], refs[T:2 * T]
        send_sems, recv_sems = refs[2 * T:]
        x, y, c, _ = _position()
        remote = []
        for t in range(T):
            rows = ins[t].shape[1]
            remote.append(_start_in_chunks(
                _remote_copier(send_sems.at[t], recv_sems.at[t], (x, y, 1 - c)), ins[t].at[:, _half(rows, 1 - c)], got[t]))
        for cp in remote:
            cp.wait()

    return pl.pallas_call(
        body, name=name, in_specs=[ANY] * T, out_specs=[ANY] * T,
        out_shape=[jax.ShapeDtypeStruct((g.shape[0], g.shape[1] // 2, g.shape[2]), g.dtype) for g in grads],
        scratch_shapes=[pltpu.SemaphoreType.DMA((T,)), pltpu.SemaphoreType.DMA((T,))],
    )(*grads)


def _scatter_to_owners(chip_sums):
    T = len(chip_sums)

    def body(*refs):
        ins, outs = refs[:T], refs[T:2 * T]
        send_sems, recv_sems = refs[2 * T:]
        x, y, c, other_chips = _position()

        def copier(t, k, to):
            return _remote_copier(send_sems.at[t, k], recv_sems.at[t, k], to)

        remote = []
        for t in range(T):
            for k, (cx, cy) in enumerate(other_chips):
                remote.append(_start_in_chunks(copier(t, k, (cx, cy, c)), ins[t].at[2 * cx + cy], outs[t].at[k]))
        for t in range(T):
            for k, (cx, cy) in enumerate(other_chips):
                copier(t, k, (cx, cy, c))(outs[t].at[k], outs[t].at[k]).wait_recv()
        for cp in remote:
            cp.wait_send()

    return pl.pallas_call(
        body, name="scatter_grads_to_owners", in_specs=[ANY] * T, out_specs=[ANY] * T,
        out_shape=[jax.ShapeDtypeStruct((3,) + s.shape[1:], s.dtype) for s in chip_sums],
        scratch_shapes=[pltpu.SemaphoreType.DMA((T, 3)), pltpu.SemaphoreType.DMA((T, 3))],
    )(*chip_sums)


def _scatter_start(name, chip_sums):
    T = len(chip_sums)
    lands = [lax.empty((3,) + s.shape[1:], s.dtype) for s in chip_sums]

    def body(*refs):
        srcs, dsts = refs[:T], refs[T:2 * T]
        send_sems, recv_sems = refs[2 * T:2 * T + 2]
        token = refs[-1]
        x, y, c, other_chips = _position()
        for t in range(T):
            for k, (cx, cy) in enumerate(other_chips):
                _start_in_chunks(_remote_copier(send_sems.at[3 * t + k], recv_sems.at[3 * t + k], (cx, cy, c)),
                                 srcs[t].at[2 * cx + cy], dsts[t].at[k], waited_here=False)
        token[...] = jnp.zeros_like(token)

    hbm = [pltpu.HBM(a.shape, a.dtype) for a in list(chip_sums) + lands]
    res = pl.pallas_call(
        body, name=name, in_specs=[HBM_SPEC] * (2 * T),
        out_specs=[SEM_SPEC, SEM_SPEC] + [HBM_SPEC] * (2 * T) + [VMEM_WHOLE],
        out_shape=[pltpu.SemaphoreType.DMA((3 * T,)), pltpu.SemaphoreType.DMA((3 * T,))] + hbm
        + [jax.ShapeDtypeStruct((8, LANES), jnp.float32)],
        input_output_aliases={i: 2 + i for i in range(2 * T)},
        compiler_params=pltpu.CompilerParams(has_side_effects=SIDE_EFFECT),
    )(*[pltpu.with_memory_space_constraint(a, pltpu.HBM) for a in list(chip_sums) + lands])
    return res[0], res[1], res[2:2 + T], res[2 + T:2 + 2 * T], res[-1]


def _scatter_wait(name, send_sems, recv_sems, chip_sums, lands, after):
    T = len(chip_sums)

    def body(*refs):
        srcs, dsts = refs[:T], refs[T:2 * T]
        send, recv = refs[2 * T:2 * T + 2]
        x, y, c, other_chips = _position()
        for t in range(T):
            for k, (cx, cy) in enumerate(other_chips):
                cp = _remote_copier(send.at[3 * t + k], recv.at[3 * t + k], (cx, cy, c))(srcs[t].at[2 * cx + cy], dsts[t].at[k])
                cp.wait_send()
                cp.wait_recv()

    arrays = list(chip_sums) + list(lands)
    res = pl.pallas_call(
        body, name=name, in_specs=[HBM_SPEC] * (2 * T) + [SEM_SPEC, SEM_SPEC, ANY],
        out_specs=[HBM_SPEC] * (2 * T), out_shape=[pltpu.HBM(a.shape, a.dtype) for a in arrays],
        input_output_aliases={i: i for i in range(2 * T)},
        compiler_params=pltpu.CompilerParams(has_side_effects=SIDE_EFFECT),
    )(*arrays, send_sems, recv_sems, after)
    return res[:T], res[T:]


def _chip_sums(tag, names, partial, pos):
    parts = [p.reshape((N_CHIPS, -1, p.shape[-1])) for p in partial]
    received = _swap_halves("swap_grad_halves_" + tag, parts)
    return [_chip_sum("chip_sum_" + n, p, r, WIRE_DTYPE, pos) for n, p, r in zip(names, parts, received)]


def _reduce_begin(tag, names, partial, pos):
    return _scatter_start("scatter_start_" + tag, _chip_sums(tag, names, partial, pos))


def _reduce_end(tag, names, flight, after, pos):
    send_sems, recv_sems, chip_sums, lands, _ = flight
    chip_sums, others = _scatter_wait("scatter_wait_" + tag, send_sems, recv_sems, chip_sums, lands, after)
    return {n: _owner_sum("owner_sum_" + n, cs, ot, pos) for n, cs, ot in zip(names, chip_sums, others)}


def _join_halves(halves):
    T = len(halves)

    def body(*refs):
        outs = refs[T:2 * T]
        send_sems, recv_sems = refs[2 * T:]
        x, y, c, _ = _position()

        def copier(t):
            return _remote_copier(send_sems.at[t], recv_sems.at[t], (x, y, 1 - c))

        remote = [_start_in_chunks(copier(t), outs[t].at[c], outs[t].at[c]) for t in range(T)]
        for t in range(T):
            copier(t)(outs[t].at[1 - c], outs[t].at[1 - c]).wait_recv()
        for cp in remote:
            cp.wait_send()

    return pl.pallas_call(
        body, name="join_grad_halves", in_specs=[ANY] * T, out_specs=[ANY] * T,
        out_shape=[jax.ShapeDtypeStruct(h.shape, h.dtype) for h in halves],
        input_output_aliases={t: t for t in range(T)},
        scratch_shapes=[pltpu.SemaphoreType.DMA((T,)), pltpu.SemaphoreType.DMA((T,))],
    )(*halves)


def _allreduce_small(part):
    rows = part.shape[0]

    def body(x_ref, out_ref, gath, send_sems, recv_sems):
        x, y, c, other_chips = _position()
        sibling = (x, y, 1 - c)

        def slot(px, py, pc):
            return gath.at[4 * px + 2 * py + pc]

        def copy(k, owner, to, src=None):
            return pltpu.make_async_remote_copy(
                src_ref=slot(*owner) if src is None else src, dst_ref=slot(*owner), send_sem=send_sems.at[k],
                recv_sem=recv_sems.at[k], device_id=to, device_id_type=MESH)

        gath[4 * x + 2 * y + c] = x_ref[...]
        first = [copy(0, (x, y, c), sibling, src=x_ref)]
        first += [copy(1 + k, (x, y, c), (cx, cy, c), src=x_ref) for k, (cx, cy) in enumerate(other_chips)]
        for cp in first:
            cp.start()
        passed = [copy(4 + k, (cx, cy, c), sibling) for k, (cx, cy) in enumerate(other_chips)]
        for k, (cx, cy) in enumerate(other_chips):
            copy(1 + k, (cx, cy, c), (x, y, c)).wait_recv()
            passed[k].start()
        copy(0, (x, y, 1 - c), (x, y, c)).wait_recv()
        for k, (cx, cy) in enumerate(other_chips):
            copy(4 + k, (cx, cy, 1 - c), (x, y, c)).wait_recv()
        for cp in first + passed:
            cp.wait_send()
        acc = gath[0]
        for d in range(1, N_DEV):
            acc = acc + gath[d]
        out_ref[...] = acc

    return pl.pallas_call(
        body, name="allreduce_small_grads", in_specs=[VMEM_WHOLE], out_specs=VMEM_WHOLE,
        out_shape=jax.ShapeDtypeStruct((rows, LANES), jnp.float32),
        scratch_shapes=[pltpu.VMEM((N_DEV, rows, LANES), jnp.float32), pltpu.SemaphoreType.DMA((7,)),
                        pltpu.SemaphoreType.DMA((7,))],
        compiler_params=pltpu.CompilerParams(vmem_limit_bytes=VMEM_LIMIT_BYTES),
    )(part)


BIG = ("w_in", "pool_w", "w_out", "w_q", "w_k", "w_v", "w_o", "w_gate", "w_up", "w_down")
SMALL = ("norm_mix_g", "pool_scale", "sgu_norm_g", "w_spatial", "b_spatial", "norm_xattn_g", "norm_mem_g",
         "norm_ffn_g", "final_norm_g")
WEIGHTS = ("norm_mix_g", "w_in", "pool_w", "pool_scale", "sgu_norm_g", "w_spatial", "b_spatial", "w_out",
           "norm_xattn_g", "norm_mem_g", "w_q", "w_k", "w_v", "w_o", "norm_ffn_g", "w_gate", "w_up", "w_down",
           "final_norm_g")


def _as2d(a):
    return a.reshape(-1, a.shape[-1])


def _as_lanes(a):
    return a.reshape(-1, LANES)


def kernel(x, mem, norm_mix_g, w_in, pool_w, pool_scale, sgu_norm_g, w_spatial, b_spatial, w_out, norm_xattn_g, norm_mem_g, w_q, w_k, w_v, w_o, norm_ffn_g, w_gate, w_up, w_down, final_norm_g, loss_target, m_norm_mix_g, m_w_in, m_pool_w, m_pool_scale, m_sgu_norm_g, m_w_spatial, m_b_spatial, m_w_out, m_norm_xattn_g, m_norm_mem_g, m_w_q, m_w_k, m_w_v, m_w_o, m_norm_ffn_g, m_w_gate, m_w_up, m_w_down, m_final_norm_g, v_norm_mix_g, v_w_in, v_pool_w, v_pool_scale, v_sgu_norm_g, v_w_spatial, v_b_spatial, v_w_out, v_norm_xattn_g, v_norm_mem_g, v_w_q, v_w_k, v_w_v, v_w_o, v_norm_ffn_g, v_w_gate, v_w_up, v_w_down, v_final_norm_g):
    args = dict(locals())
    weights = {n: args[n] for n in WEIGHTS}
    moments_m = {n: args["m_" + n] for n in WEIGHTS}
    moments_v = {n: args["v_" + n] for n in WEIGHTS}
    low = MXU_DTYPE

    xs = x[0]
    mems = mem[0]
    target = loss_target[0]
    D = xs.shape[1]
    G, PG = pool_w.shape[1], pool_w.shape[3]

    pos = jnp.stack([2 * lax.axis_index("x") + lax.axis_index("y"), lax.axis_index("c")]).astype(jnp.int32)
    slabs = [_cast_into_slab("cast_" + n, _as2d(weights[n]), WIRE_DTYPE, pos) for n in BIG]
    gathered = dict(zip(BIG, _gather_weights(slabs)))
    wg_in = gathered["w_in"]
    pw = gathered["pool_w"].reshape(N_CHIPS, G, PG // N_CHIPS, PG).transpose(1, 0, 2, 3).reshape(G, PG, PG)
    full = {n: gathered[n].reshape(-1, gathered[n].shape[-1]) for n in ("w_out", "w_q", "w_k", "w_v", "w_o", "w_down")}
    wg_gate, wg_up = gathered["w_gate"], gathered["w_up"]

    g_mix, g_xattn, g_mem, g_ffn = norm_mix_g, norm_xattn_g, norm_mem_g, norm_ffn_g
    g_final = final_norm_g.reshape(1, D)
    ws = w_spatial[0]
    b_full = jnp.broadcast_to(b_spatial[0][:, :, None], ws.shape[:2] + (sgu_norm_g.shape[1] // ws.shape[0],))

    h1 = _rmsnorm("norm_mix", xs, g_mix, low)
    proj = _matmul_nn_cols("proj_in", h1, wg_in, jnp.float32)
    z = _mixer_fwd(proj, pw, pool_scale, sgu_norm_g, ws, b_full)
    x1, h2 = _proj_residual_norm("mix_out", z, full["w_out"], xs, g_xattn)
    q = _matmul_nn("xattn_q", h2, full["w_q"], low)
    mn = _rmsnorm("norm_mem", mems, g_mem, low)
    k = _matmul_nn("xattn_k", mn, full["w_k"], low)
    vv = _matmul_nn("xattn_v", mn, full["w_v"], low)
    o = _attn_fwd(q, k, vv)
    x2, h3 = _proj_residual_norm("xattn_out", o, full["w_o"], x1, g_ffn)
    gate, up, act = _ffn_gate_up(h3, wg_gate, wg_up)
    dx3, dx3_low, loss_part, d_final_g = _ffn_down_loss(act, full["w_down"], x2, target, g_final, N_CHIPS)

    dgate, dup = _ffn_down_bwd(dx3_low, full["w_down"], gate, up, N_CHIPS)
    (d_w_down,) = _matmul_tn("grad_w_down", act, [dx3_low], WIRE_DTYPE)
    d_w_gate, d_w_up = _matmul_tn_cols("grad_w_gate_up", h3, [dgate, dup], N_CHIPS, WIRE_DTYPE)
    ffn_names = ("w_gate", "w_up", "w_down")
    ffn_flight = _reduce_begin("ffn", ffn_names, [d_w_gate, d_w_up, d_w_down], pos)
    dh3 = _matmul_nt_cols("ffn_dh", [dgate, dup], [wg_gate, wg_up], jnp.float32, after=ffn_flight[-1])
    dx2, dx2_low, d_ffn_g = _rmsnorm_bwd("norm_ffn_bwd", dh3, x2, g_ffn, dx3, low)

    do = _matmul_nt("xattn_do", [dx2_low], [full["w_o"]], low)
    (d_w_o,) = _matmul_tn("grad_w_o", o, [dx2_low], WIRE_DTYPE)
    dq, dk, dvv = _attn_bwd(q, k, vv, do)
    (d_w_q,) = _matmul_tn("grad_w_q", h2, [dq], WIRE_DTYPE)
    dh2 = _matmul_nt("xattn_dh", [dq], [full["w_q"]], jnp.float32)
    d_w_k, d_w_v = _matmul_tn("grad_w_kv", mn, [dk, dvv], WIRE_DTYPE)
    dmn = _matmul_nt("xattn_dmem", [dk, dvv], [full["w_k"], full["w_v"]], jnp.float32)
    (d_mem_g,) = _rmsnorm_bwd("norm_mem_bwd", dmn, mems, g_mem, None, None)
    dx1, dx1_low, d_xattn_g = _rmsnorm_bwd("norm_xattn_bwd", dh2, x1, g_xattn, dx2, low)

    attn_names = ("w_q", "w_k", "w_v", "w_o")
    attn_flight = _reduce_begin("attn", attn_names, [d_w_q, d_w_k, d_w_v, d_w_o], pos)
    dz = _matmul_nt("mix_dz", [dx1_low], [full["w_out"]], jnp.float32, after=attn_flight[-1])
    (d_w_out,) = _matmul_tn("grad_w_out", z, [dx1_low], WIRE_DTYPE)
    dproj, d_pw, d_ps, d_gs, d_ws, d_b = _mixer_bwd(proj, dz, pw, pool_scale, sgu_norm_g, ws, b_full)
    (d_w_in,) = _matmul_tn_cols("grad_w_in", h1, [dproj], N_CHIPS, WIRE_DTYPE)
    dh1 = _matmul_nt_cols("mix_dh", [dproj], [wg_in], jnp.float32)
    grad_x, d_mix_g = _rmsnorm_bwd("norm_mix_bwd", dh1, xs, g_mix, dx1, None)

    d_pool_w = d_pw.reshape(G, N_CHIPS, PG // N_CHIPS, PG).transpose(1, 0, 2, 3).reshape(N_CHIPS, PG, PG).astype(WIRE_DTYPE)
    mix_names = ("w_in", "pool_w", "w_out")
    mix_sums = _chip_sums("mix", mix_names, [d_w_in, d_pool_w, d_w_out], pos)
    mix_others = _scatter_to_owners(mix_sums)
    reduced = dict(zip(mix_names, [_owner_sum("owner_sum_" + n, cs, ot, pos)
                                   for n, cs, ot in zip(mix_names, mix_sums, mix_others)]))
    reduced.update(_reduce_end("ffn", ffn_names, ffn_flight, grad_x, pos))
    reduced.update(_reduce_end("attn", attn_names, attn_flight, grad_x, pos))
    grads = {n: g.reshape(-1, g.shape[-1]) for n, g in zip(BIG, _join_halves([reduced[n] for n in BIG]))}

    small_parts = {
        "norm_mix_g": d_mix_g, "pool_scale": d_ps, "sgu_norm_g": d_gs, "w_spatial": d_ws, "b_spatial": d_b,
        "norm_xattn_g": d_xattn_g, "norm_mem_g": d_mem_g, "norm_ffn_g": d_ffn_g, "final_norm_g": d_final_g,
    }
    row_counts = [small_parts[n].size // LANES for n in SMALL]
    packed = jnp.concatenate([_as_lanes(small_parts[n]) for n in SMALL], axis=0)
    small_sum = _allreduce_small(packed)

    out_grad, out_delta, out_m, out_v = {}, {}, {}, {}
    for n in BIG:
        shape = weights[n].shape
        g2d = grads[n]
        delta, m_new, v_new = _adamw("adamw_" + n, _as2d(weights[n]), g2d, _as2d(moments_m[n]), _as2d(moments_v[n]))
        out_grad[n], out_delta[n] = g2d.reshape(shape), delta.reshape(shape)
        out_m[n], out_v[n] = m_new.reshape(shape), v_new.reshape(shape)
    small_out = _adamw_small(
        small_sum, [_as_lanes(weights[n]) for n in SMALL], [_as_lanes(moments_m[n]) for n in SMALL],
        [_as_lanes(moments_v[n]) for n in SMALL], row_counts)
    for t, n in enumerate(SMALL):
        shape = weights[n].shape
        out_grad[n], out_delta[n], out_m[n], out_v[n] = (a.reshape(shape) for a in small_out[4 * t:4 * t + 4])

    loss = lax.psum(loss_part[0, 0], ("x", "y", "c"))
    return (loss, grad_x[None], *[out_grad[n] for n in WEIGHTS], *[out_delta[n] for n in WEIGHTS],
            *[out_m[n] for n in WEIGHTS], *[out_v[n] for n in WEIGHTS])
```

```python
import math

import jax
import jax.numpy as jnp
from jax import lax
from jax.experimental import pallas as pl
from jax.experimental.pallas import tpu as pltpu

MXU_DTYPE = jnp.bfloat16
WIRE_DTYPE = jnp.bfloat16

EPS = 1e-6
CHUNK = 64
POOL_WINDOWS = (2, 4, 8, 16)
POOL_HALO = 16
N_XATTN_HEADS = 4
ADAM_LR = 0.001
ADAM_B1 = 0.9
ADAM_B2 = 0.999
ADAM_EPS = 1e-08
ADAM_WD = 0.01
ADAM_STEP = 10

N_CHIPS = 4
N_DEV = 8
LANES = 128
VMEM_LIMIT_BYTES = 56 * 2 ** 20
COPY_CHUNK_BYTES = 512 * 2 ** 10
COPY_ROW_ALIGN = 16

MESH = pl.DeviceIdType.MESH
ANY = pl.BlockSpec(memory_space=pl.ANY)
VMEM_WHOLE = pl.BlockSpec(memory_space=pltpu.VMEM)
HBM_SPEC = pl.BlockSpec(memory_space=pltpu.HBM)
SEM_SPEC = pl.BlockSpec(memory_space=pltpu.SEMAPHORE)
SIDE_EFFECT = pltpu.SideEffectType.DATAFLOW_SIDE_EFFECTING

NN = (((1,), (0,)), ((), ()))
NT = (((1,), (1,)), ((), ()))
TN = (((0,), (0,)), ((), ()))


def _tile(n, target, mult):
    best = None
    for t in range(mult, min(n, target) + 1, mult):
        if n % t == 0:
            best = t
    return n if best is None else best


def _params(n_grid_axes):
    return pltpu.CompilerParams(dimension_semantics=("arbitrary",) * n_grid_axes, vmem_limit_bytes=VMEM_LIMIT_BYTES)


def _rsqrt_mean_sq(x):
    return lax.rsqrt(jnp.mean(x * x, axis=-1, keepdims=True) + EPS)


def _sum_all(x):
    return jnp.sum(jnp.sum(x, axis=1, keepdims=True), axis=0, keepdims=True)


def _sigmoid(x):
    return 1.0 / (1.0 + jnp.exp(-x))


def _fused_matmul(name, grid, operands, pairs, acc_shapes, extras, outs, epilogue):
    nk = grid[2]
    n_op, n_ex, n_out, n_acc = len(operands), len(extras), len(outs), len(acc_shapes)

    def body(*refs):
        op_refs = refs[:n_op]
        ex_refs = refs[n_op:n_op + n_ex]
        out_refs = refs[n_op + n_ex:n_op + n_ex + n_out]
        acc_refs = refs[n_op + n_ex + n_out:]
        i, j, k = pl.program_id(0), pl.program_id(1), pl.program_id(2)
        loaded = {}

        def operand(n):
            if n not in loaded:
                loaded[n] = op_refs[n][...].astype(MXU_DTYPE)
            return loaded[n]

        parts = [None] * n_acc
        for ia, ib, iacc, dims in pairs:
            d = lax.dot_general(operand(ia), operand(ib), dims, preferred_element_type=jnp.float32)
            parts[iacc] = d if parts[iacc] is None else parts[iacc] + d

        if nk == 1:
            epilogue(parts, ex_refs, out_refs, i, j)
            return

        @pl.when(k == 0)
        def _():
            for a in range(n_acc):
                acc_refs[a][...] = parts[a]

        @pl.when(k > 0)
        def _():
            for a in range(n_acc):
                acc_refs[a][...] += parts[a]

        @pl.when(k == nk - 1)
        def _():
            epilogue([acc_refs[a][...] for a in range(n_acc)], ex_refs, out_refs, i, j)

    arrays = [o[0] for o in operands] + [e[0] for e in extras]
    in_specs = [pl.BlockSpec(o[1], o[2]) for o in operands]
    in_specs += [ANY if e[1] is None else pl.BlockSpec(e[1], e[2]) for e in extras]
    return pl.pallas_call(
        body,
        name=name,
        grid=grid,
        in_specs=in_specs,
        out_specs=[pl.BlockSpec(o[2], o[3]) for o in outs],
        out_shape=[jax.ShapeDtypeStruct(o[0], o[1]) for o in outs],
        scratch_shapes=[pltpu.VMEM(s, jnp.float32) for s in acc_shapes] if nk > 1 else [],
        compiler_params=_params(3),
    )(*arrays)


def _store_epilogue(accs, ex_refs, out_refs, i, j):
    for a, o in zip(accs, out_refs):
        o[...] = a.astype(o.dtype)


def _matmul_nn(name, a, b, out_dtype, tm=1024, tn=512):
    M, K = a.shape
    N = b.shape[1]
    tm, tn = _tile(M, tm, 16), _tile(N, tn, LANES)
    return _fused_matmul(
        name, (M // tm, N // tn, 1),
        [(a, (tm, K), lambda i, j, k: (i, 0)), (b, (K, tn), lambda i, j, k: (0, j))],
        [(0, 1, 0, NN)], [(tm, tn)], [],
        [((M, N), out_dtype, (tm, tn), lambda i, j, k: (i, j))], _store_epilogue)[0]


def _matmul_nn_cols(name, a, b3, out_dtype, tm=1024):
    M, K = a.shape
    J, _, Nc = b3.shape
    tm = _tile(M, tm, 16)
    return _fused_matmul(
        name, (M // tm, J, 1),
        [(a, (tm, K), lambda i, j, k: (i, 0)), (b3, (None, K, Nc), lambda i, j, k: (j, 0, 0))],
        [(0, 1, 0, NN)], [(tm, Nc)], [],
        [((M, J * Nc), out_dtype, (tm, Nc), lambda i, j, k: (i, j))], _store_epilogue)[0]


def _order_after(after):
    return [] if after is None else [(after, None, None)]


def _matmul_nt(name, a_list, b_list, out_dtype, tm=1024, tn=512, after=None):
    M, K = a_list[0].shape
    N = b_list[0].shape[0]
    tm, tn = _tile(M, tm, 16), _tile(N, tn, LANES)
    n = len(a_list)
    operands = [(a, (tm, K), lambda i, j, k: (i, 0)) for a in a_list]
    operands += [(b, (tn, K), lambda i, j, k: (j, 0)) for b in b_list]
    return _fused_matmul(
        name, (M // tm, N // tn, 1), operands,
        [(p, n + p, 0, NT) for p in range(n)], [(tm, tn)], _order_after(after),
        [((M, N), out_dtype, (tm, tn), lambda i, j, k: (i, j))], _store_epilogue)[0]


def _matmul_nt_cols(name, a_list, b3_list, out_dtype, tm=1024, tn=512, after=None):
    M = a_list[0].shape[0]
    J, N, Kc = b3_list[0].shape
    tm, tn = _tile(M, tm, 16), _tile(N, tn, LANES)
    n = len(a_list)
    operands = [(a, (tm, Kc), lambda i, j, k: (i, k)) for a in a_list]
    operands += [(b, (None, tn, Kc), lambda i, j, k: (k, j, 0)) for b in b3_list]
    return _fused_matmul(
        name, (M // tm, N // tn, J), operands,
        [(p, n + p, 0, NT) for p in range(n)], [(tm, tn)], _order_after(after),
        [((M, N), out_dtype, (tm, tn), lambda i, j, k: (i, j))], _store_epilogue)[0]


def _matmul_tn(name, a, b_list, out_dtype, tka=1024, tn=1024, tkm=1024):
    M, Ka = a.shape
    N = b_list[0].shape[1]
    tka, tn, tkm = _tile(Ka, tka, LANES), _tile(N, tn, LANES), _tile(M, tkm, 16)
    n = len(b_list)
    operands = [(a, (tkm, tka), lambda i, j, k: (k, i))]
    operands += [(b, (tkm, tn), lambda i, j, k: (k, j)) for b in b_list]
    return _fused_matmul(
        name, (Ka // tka, N // tn, M // tkm), operands,
        [(0, 1 + p, p, TN) for p in range(n)], [(tka, tn)] * n, [],
        [((Ka, N), out_dtype, (tka, tn), lambda i, j, k: (i, j))] * n, _store_epilogue)


def _matmul_tn_cols(name, a, b_list, n_shards, out_dtype, tka=512, tkm=1024):
    M, Ka = a.shape
    Nc = b_list[0].shape[1] // n_shards
    tka, tkm = _tile(Ka, tka, LANES), _tile(M, tkm, 16)
    n = len(b_list)
    operands = [(a, (tkm, tka), lambda i, j, k: (k, i))]
    operands += [(b, (tkm, Nc), lambda i, j, k: (k, j)) for b in b_list]
    return _fused_matmul(
        name, (Ka // tka, n_shards, M // tkm), operands,
        [(0, 1 + p, p, TN) for p in range(n)], [(tka, Nc)] * n, [],
        [((n_shards, Ka, Nc), out_dtype, (None, tka, Nc), lambda i, j, k: (j, i, 0))] * n, _store_epilogue)


def _rmsnorm(name, x, g_row, out_dtype, tm=512, after=None):
    S, D = x.shape
    tm = _tile(S, tm, 16)

    def body(x_ref, g_ref, *rest):
        o_ref = rest[-1]
        xv = x_ref[...]
        o_ref[...] = (xv * _rsqrt_mean_sq(xv) * g_ref[...]).astype(o_ref.dtype)

    ordered = [] if after is None else [after]
    return pl.pallas_call(
        body, name=name, grid=(S // tm,),
        in_specs=[pl.BlockSpec((tm, D), lambda i: (i, 0)), pl.BlockSpec((1, D), lambda i: (0, 0))] + [ANY] * len(ordered),
        out_specs=pl.BlockSpec((tm, D), lambda i: (i, 0)),
        out_shape=jax.ShapeDtypeStruct((S, D), out_dtype),
        compiler_params=_params(1),
    )(x, g_row, *ordered)


def _rmsnorm_bwd_values(dh, xin, g):
    r = _rsqrt_mean_sq(xin)
    xhat = xin * r
    dhg = dh * g
    dx = r * (dhg - xhat * jnp.mean(dhg * xhat, axis=-1, keepdims=True))
    return dx, jnp.sum(dh * xhat, axis=0, keepdims=True)


def _rmsnorm_bwd(name, dh, xin, g_row, dres, low_dtype, tm=256):
    S, D = xin.shape
    tm = _tile(S, tm, 16)
    has_dx = dres is not None
    has_low = has_dx and low_dtype is not None

    def body(*refs):
        dh_ref, x_ref, g_ref = refs[:3]
        rest = refs[3:]
        dx, dg = _rmsnorm_bwd_values(dh_ref[...], x_ref[...], g_ref[...])
        if has_dx:
            dres_ref, rest = rest[0], rest[1:]
            dx = dx + dres_ref[...]
            rest[0][...] = dx
            if has_low:
                rest[1][...] = dx.astype(low_dtype)
        dg_ref = rest[-1]

        @pl.when(pl.program_id(0) == 0)
        def _():
            dg_ref[...] = jnp.zeros_like(dg_ref)

        dg_ref[...] += dg

    row = pl.BlockSpec((tm, D), lambda i: (i, 0))
    vec = pl.BlockSpec((1, D), lambda i: (0, 0))
    ins, in_specs = [dh, xin, g_row], [row, row, vec]
    out_shape, out_specs = [], []
    if has_dx:
        ins.append(dres)
        in_specs.append(row)
        out_shape.append(jax.ShapeDtypeStruct((S, D), jnp.float32))
        out_specs.append(row)
        if has_low:
            out_shape.append(jax.ShapeDtypeStruct((S, D), low_dtype))
            out_specs.append(row)
    out_shape.append(jax.ShapeDtypeStruct((1, D), jnp.float32))
    out_specs.append(vec)
    return pl.pallas_call(
        body, name=name, grid=(S // tm,), in_specs=in_specs, out_specs=out_specs, out_shape=out_shape,
        compiler_params=_params(1),
    )(*ins)


def _chunk_mask(blk):
    t = lax.broadcasted_iota(jnp.int32, (blk, blk), 0)
    s = lax.broadcasted_iota(jnp.int32, (blk, blk), 1)
    return (s // CHUNK) <= (t // CHUNK)


def _pool_inputs(a, halo, row0, tm, g, pg):
    cols = slice(g * pg, (g + 1) * pg)
    w = POOL_WINDOWS[g]
    s = jnp.concatenate([halo[:, cols], a[:, cols]], axis=0)
    span = 1
    while span < w:
        s = s + pltpu.roll(s, span, axis=0)
        span *= 2
    t = lax.broadcasted_iota(jnp.int32, (tm, 1), 0) + row0
    cnt = jnp.minimum(t + 1, w).astype(jnp.float32)
    return s[POOL_HALO:] / cnt - a[:, cols]


def _mixer_specs(S, tm, DP, DS):
    per_halo = tm // POOL_HALO
    n_halo = S // POOL_HALO
    a_spec = pl.BlockSpec((tm, DP), lambda i: (i, 0))
    prev_spec = pl.BlockSpec((POOL_HALO, DP), lambda i: (jnp.maximum(i * per_halo - 1, 0), 0))
    next_spec = pl.BlockSpec((POOL_HALO, DP), lambda i: (jnp.minimum((i + 1) * per_halo, n_halo - 1), 0))
    u_spec = pl.BlockSpec((tm, DS), lambda i: (i, 1))
    v_spec = pl.BlockSpec((tm, DS), lambda i: (i, 2))
    return a_spec, prev_spec, next_spec, u_spec, v_spec


def _whole(arr):
    nd = arr.ndim
    return pl.BlockSpec(arr.shape, lambda i: (0,) * nd)


def _mixer_fwd(proj, pw, ps_row, gs_row, ws, b_full, tm=512):
    S = proj.shape[0]
    G, PG, _ = pw.shape
    DP = G * PG
    H, BLK, _ = ws.shape
    DS = gs_row.shape[1]
    HD = DS // H
    assert DP == DS and proj.shape[1] == DP + 2 * DS
    tm = _tile(S, tm, BLK)

    def body(a_ref, prev_ref, u_ref, v_ref, pw_ref, ps_ref, gs_ref, ws_ref, b_ref, z_ref):
        i = pl.program_id(0)
        a = a_ref[...]
        halo = jnp.where(i > 0, prev_ref[...], 0.0)
        for g in range(G):
            cols = slice(g * PG, (g + 1) * PG)
            p = _pool_inputs(a, halo, i * tm, tm, g, PG)
            y = jnp.dot(p.astype(MXU_DTYPE), pw_ref[g], preferred_element_type=jnp.float32) * ps_ref[:, cols]
            z_ref[:, cols] = y.astype(z_ref.dtype)
        v = v_ref[...]
        vn = (v * _rsqrt_mean_sq(v) * gs_ref[...]).astype(MXU_DTYPE)
        u = u_ref[...]
        mask = _chunk_mask(BLK)
        for h in range(H):
            wm = jnp.where(mask, ws_ref[h], 0.0).astype(MXU_DTYPE)
            bias = b_ref[h]
            cols = slice(h * HD, (h + 1) * HD)
            for n in range(tm // BLK):
                rows = slice(n * BLK, (n + 1) * BLK)
                mixed = jnp.dot(wm, vn[rows, cols], preferred_element_type=jnp.float32) + bias
                z_ref[rows, DP + h * HD:DP + (h + 1) * HD] = (u[rows, cols] * mixed).astype(z_ref.dtype)

    a_spec, prev_spec, _, u_spec, v_spec = _mixer_specs(S, tm, DP, DS)
    return pl.pallas_call(
        body, name="mixer_fwd", grid=(S // tm,),
        in_specs=[a_spec, prev_spec, u_spec, v_spec, _whole(pw), _whole(ps_row), _whole(gs_row), _whole(ws), _whole(b_full)],
        out_specs=pl.BlockSpec((tm, DP + DS), lambda i: (i, 0)),
        out_shape=jax.ShapeDtypeStruct((S, DP + DS), MXU_DTYPE),
        compiler_params=_params(1),
    )(proj, proj, proj, proj, pw, ps_row, gs_row, ws, b_full)


def _mixer_bwd(proj, dz, pw, ps_row, gs_row, ws, b_full, tm=256):
    S = proj.shape[0]
    G, PG, _ = pw.shape
    DP = G * PG
    H, BLK, _ = ws.shape
    DS = gs_row.shape[1]
    HD = DS // H
    tm = _tile(S, tm, BLK)
    nb = S // tm
    ext = tm + POOL_HALO

    def body(a_ref, prev_ref, u_ref, v_ref, dzp_ref, dzp_next_ref, dzs_ref, pw_ref, ps_ref, gs_ref, ws_ref, b_ref,
             dproj_ref, dpw_ref, dps_ref, dgs_ref, dws_ref, db_ref, dvn_ref):
        i = pl.program_id(0)

        @pl.when(i == 0)
        def _():
            for r in (dpw_ref, dps_ref, dgs_ref, dws_ref, db_ref):
                r[...] = jnp.zeros_like(r)

        a = a_ref[...]
        halo = jnp.where(i > 0, prev_ref[...], 0.0)
        dyp = dzp_ref[...]
        dyp_next = jnp.where(i < nb - 1, dzp_next_ref[...], 0.0)
        dq_ext = jnp.concatenate([dyp, dyp_next], axis=0) * ps_ref[...]
        t_ext = lax.broadcasted_iota(jnp.int32, (ext, 1), 0) + i * tm
        for g in range(G):
            cols = slice(g * PG, (g + 1) * PG)
            w = POOL_WINDOWS[g]
            pb = _pool_inputs(a, halo, i * tm, tm, g, PG).astype(MXU_DTYPE)
            y_pre = jnp.dot(pb, pw_ref[g], preferred_element_type=jnp.float32)
            dps_ref[:, cols] += jnp.sum(dyp[:, cols] * y_pre, axis=0, keepdims=True)
            dqb = dq_ext[:, cols].astype(MXU_DTYPE)
            dpw_ref[g] += lax.dot_general(pb, dqb[:tm], TN, preferred_element_type=jnp.float32)
            dp = lax.dot_general(dqb, pw_ref[g], NT, preferred_element_type=jnp.float32)
            f = dp / jnp.minimum(t_ext + 1, w).astype(jnp.float32)
            span = 1
            while span < w:
                f = f + pltpu.roll(f, ext - span, axis=0)
                span *= 2
            dproj_ref[:, cols] = (f[:tm] - dp[:tm]).astype(dproj_ref.dtype)

        v = v_ref[...]
        rv = _rsqrt_mean_sq(v)
        vhat = v * rv
        gs = gs_ref[...]
        vnb = (vhat * gs).astype(MXU_DTYPE)
        u = u_ref[...]
        dys = dzs_ref[...]
        mask = _chunk_mask(BLK)
        for h in range(H):
            wm = jnp.where(mask, ws_ref[h], 0.0).astype(MXU_DTYPE)
            bias = b_ref[h]
            cols = slice(h * HD, (h + 1) * HD)
            dm_sum = jnp.zeros((BLK, HD), jnp.float32)
            dws_sum = jnp.zeros((BLK, BLK), jnp.float32)
            for n in range(tm // BLK):
                rows = slice(n * BLK, (n + 1) * BLK)
                vblk = vnb[rows, cols]
                mixed = jnp.dot(wm, vblk, preferred_element_type=jnp.float32) + bias
                dy = dys[rows, cols]
                dproj_ref[rows, DP + h * HD:DP + (h + 1) * HD] = (dy * mixed).astype(dproj_ref.dtype)
                dm = dy * u[rows, cols]
                dm_sum = dm_sum + dm
                dmb = dm.astype(MXU_DTYPE)
                dws_sum = dws_sum + lax.dot_general(dmb, vblk, NT, preferred_element_type=jnp.float32)
                dvn_ref[rows, cols] = lax.dot_general(wm, dmb, TN, preferred_element_type=jnp.float32)
            dws_ref[h] += jnp.where(mask, dws_sum, 0.0)
            db_ref[h:h + 1, :] += jnp.sum(dm_sum.T, axis=0, keepdims=True)
        dvn = dvn_ref[...]
        dvg = dvn * gs
        dv = rv * (dvg - vhat * jnp.mean(dvg * vhat, axis=-1, keepdims=True))
        dgs_ref[...] += jnp.sum(dvn * vhat, axis=0, keepdims=True)
        dproj_ref[:, DP + DS:] = dv.astype(dproj_ref.dtype)

    a_spec, prev_spec, next_spec, u_spec, v_spec = _mixer_specs(S, tm, DP, DS)
    dzp_spec = pl.BlockSpec((tm, DP), lambda i: (i, 0))
    dzs_spec = pl.BlockSpec((tm, DS), lambda i: (i, 1))
    small = [pw, ps_row, gs_row, ws, b_full]
    out_shape = [
        jax.ShapeDtypeStruct((S, DP + 2 * DS), MXU_DTYPE),
        jax.ShapeDtypeStruct(pw.shape, jnp.float32),
        jax.ShapeDtypeStruct(ps_row.shape, jnp.float32),
        jax.ShapeDtypeStruct(gs_row.shape, jnp.float32),
        jax.ShapeDtypeStruct(ws.shape, jnp.float32),
        jax.ShapeDtypeStruct((H, BLK), jnp.float32),
    ]
    out_specs = [pl.BlockSpec((tm, DP + 2 * DS), lambda i: (i, 0))] + [_whole(s) for s in out_shape[1:]]
    return pl.pallas_call(
        body, name="mixer_bwd", grid=(nb,),
        in_specs=[a_spec, prev_spec, u_spec, v_spec, dzp_spec, next_spec, dzs_spec] + [_whole(s) for s in small],
        out_specs=out_specs, out_shape=out_shape,
        scratch_shapes=[pltpu.VMEM((tm, DS), jnp.float32)],
        compiler_params=_params(1),
    )(proj, proj, proj, proj, dz, dz, dz, *small)


def _softmax_rows(q, k, scale):
    s = lax.dot_general(q, k, NT, preferred_element_type=jnp.float32) * scale
    e = jnp.exp(s - jnp.max(s, axis=-1, keepdims=True))
    return e / jnp.sum(e, axis=-1, keepdims=True)


def _attn_fwd(q, k, v, tm=512):
    S, D = q.shape
    M = k.shape[0]
    HD = D // N_XATTN_HEADS
    scale = HD ** -0.5
    tm = _tile(S, tm, 16)

    def body(q_ref, k_ref, v_ref, o_ref):
        for h in range(N_XATTN_HEADS):
            cols = slice(h * HD, (h + 1) * HD)
            p = _softmax_rows(q_ref[:, cols], k_ref[:, cols], scale)
            o = jnp.dot(p.astype(MXU_DTYPE), v_ref[:, cols], preferred_element_type=jnp.float32)
            o_ref[:, cols] = o.astype(o_ref.dtype)

    row = pl.BlockSpec((tm, D), lambda i: (i, 0))
    mem = pl.BlockSpec((M, D), lambda i: (0, 0))
    return pl.pallas_call(
        body, name="attn_fwd", grid=(S // tm,), in_specs=[row, mem, mem], out_specs=row,
        out_shape=jax.ShapeDtypeStruct((S, D), MXU_DTYPE), compiler_params=_params(1),
    )(q, k, v)


def _attn_bwd(q, k, v, do, tm=512):
    S, D = q.shape
    M = k.shape[0]
    HD = D // N_XATTN_HEADS
    scale = HD ** -0.5
    tm = _tile(S, tm, 16)

    def body(q_ref, k_ref, v_ref, do_ref, dq_ref, dk_ref, dv_ref):
        @pl.when(pl.program_id(0) == 0)
        def _():
            dk_ref[...] = jnp.zeros_like(dk_ref)
            dv_ref[...] = jnp.zeros_like(dv_ref)

        for h in range(N_XATTN_HEADS):
            cols = slice(h * HD, (h + 1) * HD)
            qh, kh, vh, doh = q_ref[:, cols], k_ref[:, cols], v_ref[:, cols], do_ref[:, cols]
            p = _softmax_rows(qh, kh, scale)
            dp = lax.dot_general(doh, vh, NT, preferred_element_type=jnp.float32)
            dv_ref[:, cols] += lax.dot_general(p.astype(MXU_DTYPE), doh, TN, preferred_element_type=jnp.float32)
            ds = (p * (dp - jnp.sum(dp * p, axis=-1, keepdims=True)) * scale).astype(MXU_DTYPE)
            dq_ref[:, cols] = jnp.dot(ds, kh, preferred_element_type=jnp.float32).astype(dq_ref.dtype)
            dk_ref[:, cols] += lax.dot_general(ds, qh, TN, preferred_element_type=jnp.float32)

    row = pl.BlockSpec((tm, D), lambda i: (i, 0))
    mem = pl.BlockSpec((M, D), lambda i: (0, 0))
    return pl.pallas_call(
        body, name="attn_bwd", grid=(S // tm,), in_specs=[row, mem, mem, row], out_specs=[row, mem, mem],
        out_shape=[jax.ShapeDtypeStruct((S, D), MXU_DTYPE), jax.ShapeDtypeStruct((M, D), jnp.float32),
                   jax.ShapeDtypeStruct((M, D), jnp.float32)],
        compiler_params=_params(1),
    )(q, k, v, do)


def _proj_residual_norm(name, a, w2d, resid, g_row, tm=256):
    M, K = a.shape
    N = w2d.shape[1]
    tm = _tile(M, tm, 16)

    def epilogue(accs, ex_refs, out_refs, i, j):
        x_new = ex_refs[0][...] + accs[0]
        out_refs[0][...] = x_new
        out_refs[1][...] = (x_new * _rsqrt_mean_sq(x_new) * ex_refs[1][...]).astype(MXU_DTYPE)

    row = lambda i, j, k: (i, 0)
    return _fused_matmul(
        name, (M // tm, 1, 1),
        [(a, (tm, K), row), (w2d, (K, N), lambda i, j, k: (0, 0))],
        [(0, 1, 0, NN)], [(tm, N)],
        [(resid, (tm, N), row), (g_row, (1, N), lambda i, j, k: (0, 0))],
        [((M, N), jnp.float32, (tm, N), row), ((M, N), MXU_DTYPE, (tm, N), row)], epilogue)


def _ffn_gate_up(h, wg3, wu3, tm=512):
    M, K = h.shape
    J, _, Nc = wg3.shape
    tm = _tile(M, tm, 16)

    def epilogue(accs, ex_refs, out_refs, i, j):
        gate, up = accs
        out_refs[0][...] = gate
        out_refs[1][...] = up
        out_refs[2][...] = (gate * _sigmoid(gate) * up).astype(MXU_DTYPE)

    col = lambda i, j, k: (i, j)
    wspec = lambda i, j, k: (j, 0, 0)
    return _fused_matmul(
        "ffn_gate_up", (M // tm, J, 1),
        [(h, (tm, K), lambda i, j, k: (i, 0)), (wg3, (None, K, Nc), wspec), (wu3, (None, K, Nc), wspec)],
        [(0, 1, 0, NN), (0, 2, 1, NN)], [(tm, Nc)] * 2, [],
        [((M, J * Nc), jnp.float32, (tm, Nc), col), ((M, J * Nc), jnp.float32, (tm, Nc), col),
         ((M, J * Nc), MXU_DTYPE, (tm, Nc), col)], epilogue)


def _ffn_down_loss(act, wd2d, x2, target, g_row, n_k, tm=256):
    M, F = act.shape
    D = wd2d.shape[1]
    tm = _tile(M, tm, 16)
    tk = F // n_k

    def epilogue(accs, ex_refs, out_refs, i, j):
        x3 = ex_refs[0][...] + accs[0]
        g = ex_refs[2][...]
        r = _rsqrt_mean_sq(x3)
        xhat = x3 * r
        diff = xhat * g - ex_refs[1][...]
        dy = diff / D
        dyg = dy * g
        dx = r * (dyg - xhat * jnp.mean(dyg * xhat, axis=-1, keepdims=True))
        out_refs[0][...] = dx
        out_refs[1][...] = dx.astype(MXU_DTYPE)

        @pl.when(i == 0)
        def _():
            out_refs[2][...] = jnp.zeros_like(out_refs[2])
            out_refs[3][...] = jnp.zeros_like(out_refs[3])

        out_refs[2][...] += 0.5 * _sum_all(jnp.mean(diff * diff, axis=-1, keepdims=True))
        out_refs[3][...] += jnp.sum(dy * xhat, axis=0, keepdims=True)

    row = lambda i, j, k: (i, 0)
    const = lambda i, j, k: (0, 0)
    return _fused_matmul(
        "ffn_down_loss", (M // tm, 1, n_k),
        [(act, (tm, tk), lambda i, j, k: (i, k)), (wd2d, (tk, D), lambda i, j, k: (k, 0))],
        [(0, 1, 0, NN)], [(tm, D)],
        [(x2, (tm, D), row), (target, (tm, D), row), (g_row, (1, D), const)],
        [((M, D), jnp.float32, (tm, D), row), ((M, D), MXU_DTYPE, (tm, D), row),
         ((1, 1), jnp.float32, (1, 1), const), ((1, D), jnp.float32, (1, D), const)], epilogue)


def _ffn_down_bwd(dx3_low, wd2d, gate, up, n_shards, tm=512):
    M, D = dx3_low.shape
    F = wd2d.shape[0]
    tn = F // n_shards
    tm = _tile(M, tm, 16)

    def epilogue(accs, ex_refs, out_refs, i, j):
        dact = accs[0]
        gate_v, up_v = ex_refs[0][...], ex_refs[1][...]
        sg = _sigmoid(gate_v)
        out_refs[0][...] = (dact * up_v * (sg * (1.0 + gate_v * (1.0 - sg)))).astype(MXU_DTYPE)
        out_refs[1][...] = (dact * (gate_v * sg)).astype(MXU_DTYPE)

    col = lambda i, j, k: (i, j)
    return _fused_matmul(
        "ffn_down_bwd", (M // tm, n_shards, 1),
        [(dx3_low, (tm, D), lambda i, j, k: (i, 0)), (wd2d, (tn, D), lambda i, j, k: (j, 0))],
        [(0, 1, 0, NT)], [(tm, tn)],
        [(gate, (tm, tn), col), (up, (tm, tn), col)],
        [((M, F), MXU_DTYPE, (tm, tn), col)] * 2, epilogue)


def _row_tile(R, C, target_bytes=2 ** 21):
    return _tile(R, max(16, target_bytes // (4 * C)), 16)


def _prefetch_call(body, name, grid, pos, in_specs, out_specs, out_shape, ins):
    grid_spec = pltpu.PrefetchScalarGridSpec(
        num_scalar_prefetch=1, grid=grid, in_specs=in_specs, out_specs=out_specs)
    return pl.pallas_call(body, name=name, grid_spec=grid_spec, out_shape=out_shape,
                          compiler_params=_params(len(grid)))(pos, *ins)


def _cast_into_slab(name, x2d, dtype, pos):
    R, C = x2d.shape
    tr = _row_tile(R, C)

    def body(pos_ref, x_ref, o_ref):
        o_ref[...] = x_ref[...].astype(o_ref.dtype)

    return _prefetch_call(
        body, name, (R // tr,), pos, [pl.BlockSpec((tr, C), lambda i, pos: (i, 0))],
        pl.BlockSpec((None, tr, C), lambda i, pos: (pos[0], i, 0)),
        jax.ShapeDtypeStruct((N_CHIPS, R, C), dtype), [x2d])


def _chip_sum(name, partial, got, dtype, pos):
    J, R, C = partial.shape
    Rh = R // 2
    tr = _row_tile(Rh, C)

    def body(pos_ref, a_ref, b_ref, o_ref):
        o_ref[...] = (a_ref[...].astype(jnp.float32) + b_ref[...].astype(jnp.float32)).astype(o_ref.dtype)

    blk = pl.BlockSpec((None, tr, C), lambda j, i, pos: (j, i, 0))
    return _prefetch_call(
        body, name, (J, Rh // tr), pos,
        [pl.BlockSpec((None, None, tr, C), lambda j, i, pos: (j, pos[1], i, 0)), blk], blk,
        jax.ShapeDtypeStruct((J, Rh, C), dtype), [partial.reshape(J, 2, Rh, C), got])


def _owner_sum(name, chip_sums, others, pos):
    J, Rh, C = chip_sums.shape
    n = others.shape[0]
    tr = _row_tile(Rh, C * (n + 1))

    def body(pos_ref, own_ref, oth_ref, o_ref):
        acc = own_ref[...].astype(jnp.float32)
        for k in range(n):
            acc = acc + oth_ref[k].astype(jnp.float32)
        o_ref[...] = acc

    return _prefetch_call(
        body, name, (Rh // tr,), pos,
        [pl.BlockSpec((None, tr, C), lambda i, pos: (pos[0], i, 0)), pl.BlockSpec((n, tr, C), lambda i, pos: (0, i, 0))],
        pl.BlockSpec((None, tr, C), lambda i, pos: (pos[1], i, 0)),
        jax.ShapeDtypeStruct((2, Rh, C), jnp.float32), [chip_sums, others])


def _adamw_values(w, g, m, v):
    m = ADAM_B1 * m + (1.0 - ADAM_B1) * g
    v = ADAM_B2 * v + (1.0 - ADAM_B2) * (g * g)
    m_hat = m / (1.0 - ADAM_B1 ** ADAM_STEP)
    v_hat = v / (1.0 - ADAM_B2 ** ADAM_STEP)
    delta = -ADAM_LR * (m_hat / (jnp.sqrt(v_hat) + ADAM_EPS) + ADAM_WD * w)
    return delta, m, v


def _adamw(name, w, g, m, v):
    R, C = w.shape
    tr = _row_tile(R, C, 2 ** 20)

    def body(w_ref, g_ref, m_ref, v_ref, d_ref, mo_ref, vo_ref):
        d_ref[...], mo_ref[...], vo_ref[...] = _adamw_values(w_ref[...], g_ref[...], m_ref[...], v_ref[...])

    blk = pl.BlockSpec((tr, C), lambda i: (i, 0))
    return pl.pallas_call(
        body, name=name, grid=(R // tr,), in_specs=[blk] * 4, out_specs=[blk] * 3,
        out_shape=[jax.ShapeDtypeStruct((R, C), jnp.float32)] * 3, compiler_params=_params(1))(w, g, m, v)


def _adamw_small(g_packed, ws, ms, vs, row_counts):
    n = len(ws)

    def body(*refs):
        g_ref = refs[0]
        w_refs, m_refs, v_refs = refs[1:1 + n], refs[1 + n:1 + 2 * n], refs[1 + 2 * n:1 + 3 * n]
        outs = refs[1 + 3 * n:]
        off = 0
        for t in range(n):
            g = g_ref[off:off + row_counts[t], :]
            off += row_counts[t]
            delta, m_new, v_new = _adamw_values(w_refs[t][...], g, m_refs[t][...], v_refs[t][...])
            outs[4 * t][...] = g
            outs[4 * t + 1][...] = delta
            outs[4 * t + 2][...] = m_new
            outs[4 * t + 3][...] = v_new

    out_shape = []
    for t in range(n):
        out_shape += [jax.ShapeDtypeStruct((row_counts[t], LANES), jnp.float32)] * 4
    return pl.pallas_call(
        body, name="adamw_small", in_specs=[VMEM_WHOLE] * (1 + 3 * n), out_specs=[VMEM_WHOLE] * (4 * n),
        out_shape=out_shape, compiler_params=pltpu.CompilerParams(vmem_limit_bytes=VMEM_LIMIT_BYTES),
    )(g_packed, *ws, *ms, *vs)


def _position():
    x, y, c = lax.axis_index("x"), lax.axis_index("y"), lax.axis_index("c")
    other_chips = [(1 - x, y), (x, 1 - y), (1 - x, 1 - y)]
    return x, y, c, other_chips


def _half(rows, which):
    return pl.ds(which * (rows // 2), rows // 2)


def _chunk_index(shape, dtype):
    rows = shape[-2]
    total = math.prod(shape) * jnp.dtype(dtype).itemsize
    n = max(1, min(rows // COPY_ROW_ALIGN, total // COPY_CHUNK_BYTES))
    step = -(-rows // n)
    step = -(-step // COPY_ROW_ALIGN) * COPY_ROW_ALIGN
    lead = (slice(None),) * (len(shape) - 2)
    return [lead + (pl.ds(s, min(step, rows - s)),) for s in range(0, rows, step)]


def _start_in_chunks(make, src, dst, waited_here=True):
    for idx in _chunk_index(src.shape, src.dtype):
        make(src.at[idx], dst.at[idx]).start()
    return make(src, dst) if waited_here else None


def _remote_copier(send_sem, recv_sem, to):
    return lambda src, dst: pltpu.make_async_remote_copy(
        src_ref=src, dst_ref=dst, send_sem=send_sem, recv_sem=recv_sem, device_id=to, device_id_type=MESH)


def _gather_weights(slabs):
    T = len(slabs)

    def body(*refs):
        outs = refs[T:2 * T]
        send_sems, recv_sems = refs[2 * T:]
        x, y, c, other_chips = _position()
        mine = 2 * x + y
        sibling = (x, y, 1 - c)

        def block(t, chip, which):
            return outs[t].at[chip, _half(outs[t].shape[1], which)]

        def copier(t, k, to):
            return _remote_copier(send_sems.at[t, k], recv_sems.at[t, k], to)

        started = []
        for t in range(T):
            for k, (cx, cy) in enumerate(other_chips):
                started.append(_start_in_chunks(copier(t, k, (cx, cy, c)), block(t, mine, c), block(t, mine, c)))
        for k, (cx, cy) in enumerate(other_chips):
            for t in range(T):
                landed = block(t, 2 * cx + cy, c)
                copier(t, k, (cx, cy, c))(landed, landed).wait_recv()
                started.append(_start_in_chunks(copier(t, 3 + k, sibling), landed, landed))
        for k, (cx, cy) in enumerate(other_chips):
            for t in range(T):
                arriving = block(t, 2 * cx + cy, 1 - c)
                copier(t, 3 + k, sibling)(arriving, arriving).wait_recv()
        for cp in started:
            cp.wait_send()

    return pl.pallas_call(
        body, name="gather_weights", in_specs=[ANY] * T, out_specs=[ANY] * T,
        out_shape=[jax.ShapeDtypeStruct(s.shape, s.dtype) for s in slabs],
        input_output_aliases={t: t for t in range(T)},
        scratch_shapes=[pltpu.SemaphoreType.DMA((T, 6)), pltpu.SemaphoreType.DMA((T, 6))],
    )(*slabs)


def _slab_half(ref, chip, which):
    return ref.at[chip, _half(ref.shape[1], which)]


def _gather_start(groups):
    flat = [s for g in groups for s in g]
    T, G = len(flat), len(groups)

    def body(*refs):
        ins = refs[:T]
        sems = refs[T:T + 2 * G]
        token = refs[-1]
        x, y, c, other_chips = _position()
        mine = 2 * x + y
        t0 = 0
        for gi, group in enumerate(groups):
            for t in range(len(group)):
                own = _slab_half(ins[t0 + t], mine, c)
                for k, (cx, cy) in enumerate(other_chips):
                    _start_in_chunks(_remote_copier(sems[2 * gi].at[3 * t + k], sems[2 * gi + 1].at[3 * t + k], (cx, cy, c)),
                                     own, own, waited_here=False)
            t0 += len(group)
        token[...] = jnp.zeros_like(token)

    sem_shapes = []
    for group in groups:
        sem_shapes += [pltpu.SemaphoreType.DMA((3 * len(group),))] * 2
    res = pl.pallas_call(
        body, name="gather_start", in_specs=[HBM_SPEC] * T,
        out_specs=[SEM_SPEC] * (2 * G) + [HBM_SPEC] * T + [VMEM_WHOLE],
        out_shape=sem_shapes + [pltpu.HBM(s.shape, s.dtype) for s in flat] + [jax.ShapeDtypeStruct((8, LANES), jnp.float32)],
        input_output_aliases={t: 2 * G + t for t in range(T)},
        compiler_params=pltpu.CompilerParams(has_side_effects=SIDE_EFFECT),
    )(*[pltpu.with_memory_space_constraint(s, pltpu.HBM) for s in flat])
    flights, t0 = [], 2 * G
    for gi, group in enumerate(groups):
        flights.append((res[2 * gi], res[2 * gi + 1], res[t0:t0 + len(group)]))
        t0 += len(group)
    return flights, res[-1]


def _gather_wait(name, flight, after):
    send_sems, recv_sems, slabs = flight
    T = len(slabs)

    def body(*refs):
        ins = refs[:T]
        send, recv = refs[T:T + 2]
        x, y, c, other_chips = _position()
        mine = 2 * x + y
        for t in range(T):
            own = _slab_half(ins[t], mine, c)
            for k, (cx, cy) in enumerate(other_chips):
                landed = _slab_half(ins[t], 2 * cx + cy, c)
                cp = _remote_copier(send.at[3 * t + k], recv.at[3 * t + k], (cx, cy, c))(own, landed)
                cp.wait_send()
                cp.wait_recv()

    return pl.pallas_call(
        body, name=name, in_specs=[HBM_SPEC] * T + [SEM_SPEC, SEM_SPEC, ANY], out_specs=[HBM_SPEC] * T,
        out_shape=[pltpu.HBM(s.shape, s.dtype) for s in slabs], input_output_aliases={t: t for t in range(T)},
        compiler_params=pltpu.CompilerParams(has_side_effects=SIDE_EFFECT),
    )(*slabs, send_sems, recv_sems, after)


def _share_with_sibling(name, slabs):
    T = len(slabs)

    def body(*refs):
        outs = refs[T:2 * T]
        send_sems, recv_sems = refs[2 * T:]
        x, y, c, other_chips = _position()
        sibling = (x, y, 1 - c)
        started = []
        for t in range(T):
            for k, (cx, cy) in enumerate(other_chips):
                landed = _slab_half(outs[t], 2 * cx + cy, c)
                started.append(_start_in_chunks(_remote_copier(send_sems.at[t, k], recv_sems.at[t, k], sibling), landed, landed))
        for t in range(T):
            for k, (cx, cy) in enumerate(other_chips):
                arriving = _slab_half(outs[t], 2 * cx + cy, 1 - c)
                _remote_copier(send_sems.at[t, k], recv_sems.at[t, k], sibling)(arriving, arriving).wait_recv()
        for cp in started:
            cp.wait_send()

    return pl.pallas_call(
        body, name=name, in_specs=[ANY] * T, out_specs=[ANY] * T,
        out_shape=[jax.ShapeDtypeStruct(s.shape, s.dtype) for s in slabs],
        input_output_aliases={t: t for t in range(T)},
        scratch_shapes=[pltpu.SemaphoreType.DMA((T, 3)), pltpu.SemaphoreType.DMA((T, 3))],
    )(*slabs)


def _swap_halves(name, grads):
    T = len(grads)

    def body(*refs):
        ins, got = refs[:T], refs[T:2 * T]
        send_sems, recv_sems = refs[2 * T:]
        x, y, c, _ = _position()
        remote = []
        for t in range(T):
            rows = ins[t].shape[1]
            remote.append(_start_in_chunks(
                _remote_copier(send_sems.at[t], recv_sems.at[t], (x, y, 1 - c)), ins[t].at[:, _half(rows, 1 - c)], got[t]))
        for cp in remote:
            cp.wait()

    return pl.pallas_call(
        body, name=name, in_specs=[ANY] * T, out_specs=[ANY] * T,
        out_shape=[jax.ShapeDtypeStruct((g.shape[0], g.shape[1] // 2, g.shape[2]), g.dtype) for g in grads],
        scratch_shapes=[pltpu.SemaphoreType.DMA((T,)), pltpu.SemaphoreType.DMA((T,))],
    )(*grads)


def _scatter_to_owners(chip_sums):
    T = len(chip_sums)

    def body(*refs):
        ins, outs = refs[:T], refs[T:2 * T]
        send_sems, recv_sems = refs[2 * T:]
        x, y, c, other_chips = _position()

        def copier(t, k, to):
            return _remote_copier(send_sems.at[t, k], recv_sems.at[t, k], to)

        remote = []
        for t in range(T):
            for k, (cx, cy) in enumerate(other_chips):
                remote.append(_start_in_chunks(copier(t, k, (cx, cy, c)), ins[t].at[2 * cx + cy], outs[t].at[k]))
        for t in range(T):
            for k, (cx, cy) in enumerate(other_chips):
                copier(t, k, (cx, cy, c))(outs[t].at[k], outs[t].at[k]).wait_recv()
        for cp in remote:
            cp.wait_send()

    return pl.pallas_call(
        body, name="scatter_grads_to_owners", in_specs=[ANY] * T, out_specs=[ANY] * T,
        out_shape=[jax.ShapeDtypeStruct((3,) + s.shape[1:], s.dtype) for s in chip_sums],
        scratch_shapes=[pltpu.SemaphoreType.DMA((T, 3)), pltpu.SemaphoreType.DMA((T, 3))],
    )(*chip_sums)


def _scatter_start(name, chip_sums):
    T = len(chip_sums)
    lands = [lax.empty((3,) + s.shape[1:], s.dtype) for s in chip_sums]

    def body(*refs):
        srcs, dsts = refs[:T], refs[T:2 * T]
        send_sems, recv_sems = refs[2 * T:2 * T + 2]
        token = refs[-1]
        x, y, c, other_chips = _position()
        for t in range(T):
            for k, (cx, cy) in enumerate(other_chips):
                _start_in_chunks(_remote_copier(send_sems.at[3 * t + k], recv_sems.at[3 * t + k], (cx, cy, c)),
                                 srcs[t].at[2 * cx + cy], dsts[t].at[k], waited_here=False)
        token[...] = jnp.zeros_like(token)

    hbm = [pltpu.HBM(a.shape, a.dtype) for a in list(chip_sums) + lands]
    res = pl.pallas_call(
        body, name=name, in_specs=[HBM_SPEC] * (2 * T),
        out_specs=[SEM_SPEC, SEM_SPEC] + [HBM_SPEC] * (2 * T) + [VMEM_WHOLE],
        out_shape=[pltpu.SemaphoreType.DMA((3 * T,)), pltpu.SemaphoreType.DMA((3 * T,))] + hbm
        + [jax.ShapeDtypeStruct((8, LANES), jnp.float32)],
        input_output_aliases={i: 2 + i for i in range(2 * T)},
        compiler_params=pltpu.CompilerParams(has_side_effects=SIDE_EFFECT),
    )(*[pltpu.with_memory_space_constraint(a, pltpu.HBM) for a in list(chip_sums) + lands])
    return res[0], res[1], res[2:2 + T], res[2 + T:2 + 2 * T], res[-1]


def _scatter_wait(name, send_sems, recv_sems, chip_sums, lands, after):
    T = len(chip_sums)

    def body(*refs):
        srcs, dsts = refs[:T], refs[T:2 * T]
        send, recv = refs[2 * T:2 * T + 2]
        x, y, c, other_chips = _position()
        for t in range(T):
            for k, (cx, cy) in enumerate(other_chips):
                cp = _remote_copier(send.at[3 * t + k], recv.at[3 * t + k], (cx, cy, c))(srcs[t].at[2 * cx + cy], dsts[t].at[k])
                cp.wait_send()
                cp.wait_recv()

    arrays = list(chip_sums) + list(lands)
    res = pl.pallas_call(
        body, name=name, in_specs=[HBM_SPEC] * (2 * T) + [SEM_SPEC, SEM_SPEC, ANY],
        out_specs=[HBM_SPEC] * (2 * T), out_shape=[pltpu.HBM(a.shape, a.dtype) for a in arrays],
        input_output_aliases={i: i for i in range(2 * T)},
        compiler_params=pltpu.CompilerParams(has_side_effects=SIDE_EFFECT),
    )(*arrays, send_sems, recv_sems, after)
    return res[:T], res[T:]


def _chip_sums(tag, names, partial, pos):
    parts = [p.reshape((N_CHIPS, -1, p.shape[-1])) for p in partial]
    received = _swap_halves("swap_grad_halves_" + tag, parts)
    return [_chip_sum("chip_sum_" + n, p, r, WIRE_DTYPE, pos) for n, p, r in zip(names, parts, received)]


def _reduce_begin(tag, names, partial, pos):
    return _scatter_start("scatter_start_" + tag, _chip_sums(tag, names, partial, pos))


def _reduce_end(tag, names, flight, after, pos):
    send_sems, recv_sems, chip_sums, lands, _ = flight
    chip_sums, others = _scatter_wait("scatter_wait_" + tag, send_sems, recv_sems, chip_sums, lands, after)
    return {n: _owner_sum("owner_sum_" + n, cs, ot, pos) for n, cs, ot in zip(names, chip_sums, others)}


def _join_halves(halves):
    T = len(halves)

    def body(*refs):
        outs = refs[T:2 * T]
        send_sems, recv_sems = refs[2 * T:]
        x, y, c, _ = _position()

        def copier(t):
            return _remote_copier(send_sems.at[t], recv_sems.at[t], (x, y, 1 - c))

        remote = [_start_in_chunks(copier(t), outs[t].at[c], outs[t].at[c]) for t in range(T)]
        for t in range(T):
            copier(t)(outs[t].at[1 - c], outs[t].at[1 - c]).wait_recv()
        for cp in remote:
            cp.wait_send()

    return pl.pallas_call(
        body, name="join_grad_halves", in_specs=[ANY] * T, out_specs=[ANY] * T,
        out_shape=[jax.ShapeDtypeStruct(h.shape, h.dtype) for h in halves],
        input_output_aliases={t: t for t in range(T)},
        scratch_shapes=[pltpu.SemaphoreType.DMA((T,)), pltpu.SemaphoreType.DMA((T,))],
    )(*halves)


def _allreduce_small(part):
    rows = part.shape[0]

    def body(x_ref, out_ref, gath, send_sems, recv_sems):
        x, y, c, other_chips = _position()
        sibling = (x, y, 1 - c)

        def slot(px, py, pc):
            return gath.at[4 * px + 2 * py + pc]

        def copy(k, owner, to, src=None):
            return pltpu.make_async_remote_copy(
                src_ref=slot(*owner) if src is None else src, dst_ref=slot(*owner), send_sem=send_sems.at[k],
                recv_sem=recv_sems.at[k], device_id=to, device_id_type=MESH)

        gath[4 * x + 2 * y + c] = x_ref[...]
        first = [copy(0, (x, y, c), sibling, src=x_ref)]
        first += [copy(1 + k, (x, y, c), (cx, cy, c), src=x_ref) for k, (cx, cy) in enumerate(other_chips)]
        for cp in first:
            cp.start()
        passed = [copy(4 + k, (cx, cy, c), sibling) for k, (cx, cy) in enumerate(other_chips)]
        for k, (cx, cy) in enumerate(other_chips):
            copy(1 + k, (cx, cy, c), (x, y, c)).wait_recv()
            passed[k].start()
        copy(0, (x, y, 1 - c), (x, y, c)).wait_recv()
        for k, (cx, cy) in enumerate(other_chips):
            copy(4 + k, (cx, cy, 1 - c), (x, y, c)).wait_recv()
        for cp in first + passed:
            cp.wait_send()
        acc = gath[0]
        for d in range(1, N_DEV):
            acc = acc + gath[d]
        out_ref[...] = acc

    return pl.pallas_call(
        body, name="allreduce_small_grads", in_specs=[VMEM_WHOLE], out_specs=VMEM_WHOLE,
        out_shape=jax.ShapeDtypeStruct((rows, LANES), jnp.float32),
        scratch_shapes=[pltpu.VMEM((N_DEV, rows, LANES), jnp.float32), pltpu.SemaphoreType.DMA((7,)),
                        pltpu.SemaphoreType.DMA((7,))],
        compiler_params=pltpu.CompilerParams(vmem_limit_bytes=VMEM_LIMIT_BYTES),
    )(part)


BIG = ("w_in", "pool_w", "w_out", "w_q", "w_k", "w_v", "w_o", "w_gate", "w_up", "w_down")
SMALL = ("norm_mix_g", "pool_scale", "sgu_norm_g", "w_spatial", "b_spatial", "norm_xattn_g", "norm_mem_g",
         "norm_ffn_g", "final_norm_g")
GATHER_ORDER = (("w_in", "pool_w"), ("w_out",), ("w_q", "w_k", "w_v", "w_o"), ("w_gate", "w_up"), ("w_down",))
WEIGHTS = ("norm_mix_g", "w_in", "pool_w", "pool_scale", "sgu_norm_g", "w_spatial", "b_spatial", "w_out",
           "norm_xattn_g", "norm_mem_g", "w_q", "w_k", "w_v", "w_o", "norm_ffn_g", "w_gate", "w_up", "w_down",
           "final_norm_g")


def _as2d(a):
    return a.reshape(-1, a.shape[-1])


def _as_lanes(a):
    return a.reshape(-1, LANES)


def kernel(x, mem, norm_mix_g, w_in, pool_w, pool_scale, sgu_norm_g, w_spatial, b_spatial, w_out, norm_xattn_g, norm_mem_g, w_q, w_k, w_v, w_o, norm_ffn_g, w_gate, w_up, w_down, final_norm_g, loss_target, m_norm_mix_g, m_w_in, m_pool_w, m_pool_scale, m_sgu_norm_g, m_w_spatial, m_b_spatial, m_w_out, m_norm_xattn_g, m_norm_mem_g, m_w_q, m_w_k, m_w_v, m_w_o, m_norm_ffn_g, m_w_gate, m_w_up, m_w_down, m_final_norm_g, v_norm_mix_g, v_w_in, v_pool_w, v_pool_scale, v_sgu_norm_g, v_w_spatial, v_b_spatial, v_w_out, v_norm_xattn_g, v_norm_mem_g, v_w_q, v_w_k, v_w_v, v_w_o, v_norm_ffn_g, v_w_gate, v_w_up, v_w_down, v_final_norm_g):
    args = dict(locals())
    weights = {n: args[n] for n in WEIGHTS}
    moments_m = {n: args["m_" + n] for n in WEIGHTS}
    moments_v = {n: args["v_" + n] for n in WEIGHTS}
    low = MXU_DTYPE

    xs = x[0]
    mems = mem[0]
    target = loss_target[0]
    D = xs.shape[1]
    G, PG = pool_w.shape[1], pool_w.shape[3]

    pos = jnp.stack([2 * lax.axis_index("x") + lax.axis_index("y"), lax.axis_index("c")]).astype(jnp.int32)
    slabs = {n: _cast_into_slab("cast_" + n, _as2d(weights[n]), WIRE_DTYPE, pos) for n in BIG}
    flights, started = _gather_start([[slabs[n] for n in group] for group in GATHER_ORDER])

    def arrive(gi, after):
        landed = _gather_wait("gather_wait_%d" % gi, flights[gi], after)
        return dict(zip(GATHER_ORDER[gi], _share_with_sibling("gather_share_%d" % gi, landed)))

    def rows(a):
        return a.reshape(-1, a.shape[-1])

    g_mix, g_xattn, g_mem, g_ffn = norm_mix_g, norm_xattn_g, norm_mem_g, norm_ffn_g
    g_final = final_norm_g.reshape(1, D)
    ws = w_spatial[0]
    b_full = jnp.broadcast_to(b_spatial[0][:, :, None], ws.shape[:2] + (sgu_norm_g.shape[1] // ws.shape[0],))

    h1 = _rmsnorm("norm_mix", xs, g_mix, low, after=started)
    mn = _rmsnorm("norm_mem", mems, g_mem, low)
    got = arrive(0, mn)
    wg_in = got["w_in"]
    pw = got["pool_w"].reshape(N_CHIPS, G, PG // N_CHIPS, PG).transpose(1, 0, 2, 3).reshape(G, PG, PG)
    proj = _matmul_nn_cols("proj_in", h1, wg_in, jnp.float32)
    z = _mixer_fwd(proj, pw, pool_scale, sgu_norm_g, ws, b_full)
    full = {"w_out": rows(arrive(1, z)["w_out"])}
    x1, h2 = _proj_residual_norm("mix_out", z, full["w_out"], xs, g_xattn)
    full.update({n: rows(a) for n, a in arrive(2, h2).items()})
    q = _matmul_nn("xattn_q", h2, full["w_q"], low)
    k = _matmul_nn("xattn_k", mn, full["w_k"], low)
    vv = _matmul_nn("xattn_v", mn, full["w_v"], low)
    o = _attn_fwd(q, k, vv)
    x2, h3 = _proj_residual_norm("xattn_out", o, full["w_o"], x1, g_ffn)
    got = arrive(3, h3)
    wg_gate, wg_up = got["w_gate"], got["w_up"]
    gate, up, act = _ffn_gate_up(h3, wg_gate, wg_up)
    full["w_down"] = rows(arrive(4, act)["w_down"])
    dx3, dx3_low, loss_part, d_final_g = _ffn_down_loss(act, full["w_down"], x2, target, g_final, N_CHIPS)

    dgate, dup = _ffn_down_bwd(dx3_low, full["w_down"], gate, up, N_CHIPS)
    (d_w_down,) = _matmul_tn("grad_w_down", act, [dx3_low], WIRE_DTYPE)
    d_w_gate, d_w_up = _matmul_tn_cols("grad_w_gate_up", h3, [dgate, dup], N_CHIPS, WIRE_DTYPE)
    ffn_names = ("w_gate", "w_up", "w_down")
    ffn_flight = _reduce_begin("ffn", ffn_names, [d_w_gate, d_w_up, d_w_down], pos)
    dh3 = _matmul_nt_cols("ffn_dh", [dgate, dup], [wg_gate, wg_up], jnp.float32, after=ffn_flight[-1])
    dx2, dx2_low, d_ffn_g = _rmsnorm_bwd("norm_ffn_bwd", dh3, x2, g_ffn, dx3, low)

    do = _matmul_nt("xattn_do", [dx2_low], [full["w_o"]], low)
    (d_w_o,) = _matmul_tn("grad_w_o", o, [dx2_low], WIRE_DTYPE)
    dq, dk, dvv = _attn_bwd(q, k, vv, do)
    (d_w_q,) = _matmul_tn("grad_w_q", h2, [dq], WIRE_DTYPE)
    dh2 = _matmul_nt("xattn_dh", [dq], [full["w_q"]], jnp.float32)
    d_w_k, d_w_v = _matmul_tn("grad_w_kv", mn, [dk, dvv], WIRE_DTYPE)
    dmn = _matmul_nt("xattn_dmem", [dk, dvv], [full["w_k"], full["w_v"]], jnp.float32)
    (d_mem_g,) = _rmsnorm_bwd("norm_mem_bwd", dmn, mems, g_mem, None, None)
    dx1, dx1_low, d_xattn_g = _rmsnorm_bwd("norm_xattn_bwd", dh2, x1, g_xattn, dx2, low)

    attn_names = ("w_q", "w_k", "w_v", "w_o")
    attn_flight = _reduce_begin("attn", attn_names, [d_w_q, d_w_k, d_w_v, d_w_o], pos)
    dz = _matmul_nt("mix_dz", [dx1_low], [full["w_out"]], jnp.float32, after=attn_flight[-1])
    (d_w_out,) = _matmul_tn("grad_w_out", z, [dx1_low], WIRE_DTYPE)
    dproj, d_pw, d_ps, d_gs, d_ws, d_b = _mixer_bwd(proj, dz, pw, pool_scale, sgu_norm_g, ws, b_full)
    (d_w_in,) = _matmul_tn_cols("grad_w_in", h1, [dproj], N_CHIPS, WIRE_DTYPE)
    dh1 = _matmul_nt_cols("mix_dh", [dproj], [wg_in], jnp.float32)
    grad_x, d_mix_g = _rmsnorm_bwd("norm_mix_bwd", dh1, xs, g_mix, dx1, None)

    d_pool_w = d_pw.reshape(G, N_CHIPS, PG // N_CHIPS, PG).transpose(1, 0, 2, 3).reshape(N_CHIPS, PG, PG).astype(WIRE_DTYPE)
    mix_names = ("w_in", "pool_w", "w_out")
    mix_sums = _chip_sums("mix", mix_names, [d_w_in, d_pool_w, d_w_out], pos)
    mix_others = _scatter_to_owners(mix_sums)
    reduced = dict(zip(mix_names, [_owner_sum("owner_sum_" + n, cs, ot, pos)
                                   for n, cs, ot in zip(mix_names, mix_sums, mix_others)]))
    reduced.update(_reduce_end("ffn", ffn_names, ffn_flight, grad_x, pos))
    reduced.update(_reduce_end("attn", attn_names, attn_flight, grad_x, pos))
    grads = {n: g.reshape(-1, g.shape[-1]) for n, g in zip(BIG, _join_halves([reduced[n] for n in BIG]))}

    small_parts = {
        "norm_mix_g": d_mix_g, "pool_scale": d_ps, "sgu_norm_g": d_gs, "w_spatial": d_ws, "b_spatial": d_b,
        "norm_xattn_g": d_xattn_g, "norm_mem_g": d_mem_g, "norm_ffn_g": d_ffn_g, "final_norm_g": d_final_g,
    }
    row_counts = [small_parts[n].size // LANES for n in SMALL]
    packed = jnp.concatenate([_as_lanes(small_parts[n]) for n in SMALL], axis=0)
    small_sum = _allreduce_small(packed)

    out_grad, out_delta, out_m, out_v = {}, {}, {}, {}
    for n in BIG:
        shape = weights[n].shape
        g2d = grads[n]
        delta, m_new, v_new = _adamw("adamw_" + n, _as2d(weights[n]), g2d, _as2d(moments_m[n]), _as2d(moments_v[n]))
        out_grad[n], out_delta[n] = g2d.reshape(shape), delta.reshape(shape)
        out_m[n], out_v[n] = m_new.reshape(shape), v_new.reshape(shape)
    small_out = _adamw_small(
        small_sum, [_as_lanes(weights[n]) for n in SMALL], [_as_lanes(moments_m[n]) for n in SMALL],
        [_as_lanes(moments_v[n]) for n in SMALL], row_counts)
    for t, n in enumerate(SMALL):
        shape = weights[n].shape
        out_grad[n], out_delta[n], out_m[n], out_v[n] = (a.reshape(shape) for a in small_out[4 * t:4 * t + 4])

    loss = lax.psum(loss_part[0, 0], ("x", "y", "c"))
    return (loss, grad_x[None], *[out_grad[n] for n in WEIGHTS], *[out_delta[n] for n in WEIGHTS],
            *[out_m[n] for n in WEIGHTS], *[out_v[n] for n in WEIGHTS])
```

```python
import math

import jax
import jax.numpy as jnp
from jax import lax
from jax.experimental import pallas as pl
from jax.experimental.pallas import tpu as pltpu

MXU_DTYPE = jnp.bfloat16
WIRE_DTYPE = jnp.bfloat16

EPS = 1e-6
CHUNK = 64
POOL_WINDOWS = (2, 4, 8, 16)
POOL_HALO = 16
N_XATTN_HEADS = 4
ADAM_LR = 0.001
ADAM_B1 = 0.9
ADAM_B2 = 0.999
ADAM_EPS = 1e-08
ADAM_WD = 0.01
ADAM_STEP = 10

N_CHIPS = 4
N_DEV = 8
LANES = 128
VMEM_LIMIT_BYTES = 56 * 2 ** 20
COPY_CHUNK_BYTES = 512 * 2 ** 10
COPY_ROW_ALIGN = 16

MESH = pl.DeviceIdType.MESH
ANY = pl.BlockSpec(memory_space=pl.ANY)
VMEM_WHOLE = pl.BlockSpec(memory_space=pltpu.VMEM)
HBM_SPEC = pl.BlockSpec(memory_space=pltpu.HBM)
SEM_SPEC = pl.BlockSpec(memory_space=pltpu.SEMAPHORE)
SIDE_EFFECT = pltpu.SideEffectType.DATAFLOW_SIDE_EFFECTING

NN = (((1,), (0,)), ((), ()))
NT = (((1,), (1,)), ((), ()))
TN = (((0,), (0,)), ((), ()))


def _tile(n, target, mult):
    best = None
    for t in range(mult, min(n, target) + 1, mult):
        if n % t == 0:
            best = t
    return n if best is None else best


def _params(n_grid_axes):
    return pltpu.CompilerParams(dimension_semantics=("arbitrary",) * n_grid_axes, vmem_limit_bytes=VMEM_LIMIT_BYTES)


def _rsqrt_mean_sq(x):
    return lax.rsqrt(jnp.mean(x * x, axis=-1, keepdims=True) + EPS)


def _sum_all(x):
    return jnp.sum(jnp.sum(x, axis=1, keepdims=True), axis=0, keepdims=True)


def _sigmoid(x):
    return 0.5 * jnp.tanh(0.5 * x) + 0.5


def _store_epilogue(accs, ex_refs, out_refs, i, j):
    for a, o in zip(accs, out_refs):
        o[...] = a.astype(o.dtype)


def _fused_matmul(name, grid, operands, pairs, acc_shapes, extras, outs, epilogue):
    nk = grid[2]
    n_op, n_ex, n_out, n_acc = len(operands), len(extras), len(outs), len(acc_shapes)
    in_place = nk > 1 and epilogue is _store_epilogue and all(o[1] == jnp.float32 for o in outs)

    def body(*refs):
        op_refs = refs[:n_op]
        ex_refs = refs[n_op:n_op + n_ex]
        out_refs = refs[n_op + n_ex:n_op + n_ex + n_out]
        acc_refs = out_refs if in_place else refs[n_op + n_ex + n_out:]
        i, j, k = pl.program_id(0), pl.program_id(1), pl.program_id(2)
        loaded = {}

        def operand(n):
            if n not in loaded:
                loaded[n] = op_refs[n][...].astype(MXU_DTYPE)
            return loaded[n]

        def product(ia, ib, dims):
            return lax.dot_general(operand(ia), operand(ib), dims, preferred_element_type=jnp.float32)

        def products():
            parts = [None] * n_acc
            for ia, ib, iacc, dims in pairs:
                d = product(ia, ib, dims)
                parts[iacc] = d if parts[iacc] is None else parts[iacc] + d
            return parts

        if nk == 1:
            epilogue(products(), ex_refs, out_refs, i, j)
            return

        @pl.when(k == 0)
        def _():
            for a in range(n_acc):
                acc_refs[a][...] = jnp.zeros_like(acc_refs[a])

        for a, part in enumerate(products()):
            acc_refs[a][...] += part

        if not in_place:
            @pl.when(k == nk - 1)
            def _():
                epilogue([acc_refs[a][...] for a in range(n_acc)], ex_refs, out_refs, i, j)

    arrays = [o[0] for o in operands] + [e[0] for e in extras]
    in_specs = [pl.BlockSpec(o[1], o[2]) for o in operands]
    in_specs += [ANY if e[1] is None else pl.BlockSpec(e[1], e[2]) for e in extras]
    return pl.pallas_call(
        body,
        name=name,
        grid=grid,
        in_specs=in_specs,
        out_specs=[pl.BlockSpec(o[2], o[3]) for o in outs],
        out_shape=[jax.ShapeDtypeStruct(o[0], o[1]) for o in outs],
        scratch_shapes=[pltpu.VMEM(s, jnp.float32) for s in acc_shapes] if nk > 1 and not in_place else [],
        compiler_params=_params(3),
    )(*arrays)


def _matmul_nn(name, a, b, out_dtype, tm=1024, tn=512):
    M, K = a.shape
    N = b.shape[1]
    tm, tn = _tile(M, tm, 16), _tile(N, tn, LANES)
    return _fused_matmul(
        name, (M // tm, N // tn, 1),
        [(a, (tm, K), lambda i, j, k: (i, 0)), (b, (K, tn), lambda i, j, k: (0, j))],
        [(0, 1, 0, NN)], [(tm, tn)], [],
        [((M, N), out_dtype, (tm, tn), lambda i, j, k: (i, j))], _store_epilogue)[0]


def _matmul_nn_cols(name, a, b3, out_dtype, tm=1024):
    M, K = a.shape
    J, _, Nc = b3.shape
    tm = _tile(M, tm, 16)
    return _fused_matmul(
        name, (M // tm, J, 1),
        [(a, (tm, K), lambda i, j, k: (i, 0)), (b3, (None, K, Nc), lambda i, j, k: (j, 0, 0))],
        [(0, 1, 0, NN)], [(tm, Nc)], [],
        [((M, J * Nc), out_dtype, (tm, Nc), lambda i, j, k: (i, j))], _store_epilogue)[0]


def _order_after(after):
    return [] if after is None else [(after, None, None)]


def _matmul_nt(name, a_list, b_list, out_dtype, tm=1024, tn=512, after=None):
    M, K = a_list[0].shape
    N = b_list[0].shape[0]
    tm, tn = _tile(M, tm, 16), _tile(N, tn, LANES)
    n = len(a_list)
    operands = [(a, (tm, K), lambda i, j, k: (i, 0)) for a in a_list]
    operands += [(b, (tn, K), lambda i, j, k: (j, 0)) for b in b_list]
    return _fused_matmul(
        name, (M // tm, N // tn, 1), operands,
        [(p, n + p, 0, NT) for p in range(n)], [(tm, tn)], _order_after(after),
        [((M, N), out_dtype, (tm, tn), lambda i, j, k: (i, j))], _store_epilogue)[0]


def _matmul_nt_cols(name, a_list, b3_list, out_dtype, tm=1024, tn=512, after=None):
    M = a_list[0].shape[0]
    J, N, Kc = b3_list[0].shape
    tm, tn = _tile(M, tm, 16), _tile(N, tn, LANES)
    n = len(a_list)
    operands = [(a, (tm, Kc), lambda i, j, k: (i, k)) for a in a_list]
    operands += [(b, (None, tn, Kc), lambda i, j, k: (k, j, 0)) for b in b3_list]
    return _fused_matmul(
        name, (M // tm, N // tn, J), operands,
        [(p, n + p, 0, NT) for p in range(n)], [(tm, tn)], _order_after(after),
        [((M, N), out_dtype, (tm, tn), lambda i, j, k: (i, j))], _store_epilogue)[0]


def _matmul_tn(name, a, b_list, out_dtype, tka=512, tn=1024, tkm=4096):
    M, Ka = a.shape
    N = b_list[0].shape[1]
    tka, tn, tkm = _tile(Ka, tka, LANES), _tile(N, tn, LANES), _tile(M, tkm, 16)
    n = len(b_list)
    operands = [(a, (tkm, tka), lambda i, j, k: (k, i))]
    operands += [(b, (tkm, tn), lambda i, j, k: (k, j)) for b in b_list]
    return _fused_matmul(
        name, (Ka // tka, N // tn, M // tkm), operands,
        [(0, 1 + p, p, TN) for p in range(n)], [(tka, tn)] * n, [],
        [((Ka, N), out_dtype, (tka, tn), lambda i, j, k: (i, j))] * n, _store_epilogue)


def _matmul_tn_cols(name, a, b_list, n_shards, out_dtype, tka=512, tkm=4096):
    M, Ka = a.shape
    Nc = b_list[0].shape[1] // n_shards
    tka, tkm = _tile(Ka, tka, LANES), _tile(M, tkm, 16)
    n = len(b_list)
    operands = [(a, (tkm, tka), lambda i, j, k: (k, i))]
    operands += [(b, (tkm, Nc), lambda i, j, k: (k, j)) for b in b_list]
    return _fused_matmul(
        name, (Ka // tka, n_shards, M // tkm), operands,
        [(0, 1 + p, p, TN) for p in range(n)], [(tka, Nc)] * n, [],
        [((n_shards, Ka, Nc), out_dtype, (None, tka, Nc), lambda i, j, k: (j, i, 0))] * n, _store_epilogue)


def _rmsnorm(name, x, g_row, out_dtype, tm=512, after=None):
    S, D = x.shape
    tm = _tile(S, tm, 16)

    def body(x_ref, g_ref, *rest):
        o_ref = rest[-1]
        xv = x_ref[...]
        o_ref[...] = (xv * _rsqrt_mean_sq(xv) * g_ref[...]).astype(o_ref.dtype)

    ordered = [] if after is None else [after]
    return pl.pallas_call(
        body, name=name, grid=(S // tm,),
        in_specs=[pl.BlockSpec((tm, D), lambda i: (i, 0)), pl.BlockSpec((1, D), lambda i: (0, 0))] + [ANY] * len(ordered),
        out_specs=pl.BlockSpec((tm, D), lambda i: (i, 0)),
        out_shape=jax.ShapeDtypeStruct((S, D), out_dtype),
        compiler_params=_params(1),
    )(x, g_row, *ordered)


def _rmsnorm_bwd_values(dh, xin, g):
    r = _rsqrt_mean_sq(xin)
    xhat = xin * r
    dhg = dh * g
    dx = r * (dhg - xhat * jnp.mean(dhg * xhat, axis=-1, keepdims=True))
    return dx, jnp.sum(dh * xhat, axis=0, keepdims=True)


def _rmsnorm_bwd(name, dh, xin, g_row, dres, low_dtype, tm=256):
    S, D = xin.shape
    tm = _tile(S, tm, 16)
    has_dx = dres is not None
    has_low = has_dx and low_dtype is not None

    def body(*refs):
        dh_ref, x_ref, g_ref = refs[:3]
        rest = refs[3:]
        dx, dg = _rmsnorm_bwd_values(dh_ref[...], x_ref[...], g_ref[...])
        if has_dx:
            dres_ref, rest = rest[0], rest[1:]
            dx = dx + dres_ref[...]
            rest[0][...] = dx
            if has_low:
                rest[1][...] = dx.astype(low_dtype)
        dg_ref = rest[-1]

        @pl.when(pl.program_id(0) == 0)
        def _():
            dg_ref[...] = jnp.zeros_like(dg_ref)

        dg_ref[...] += dg

    row = pl.BlockSpec((tm, D), lambda i: (i, 0))
    vec = pl.BlockSpec((1, D), lambda i: (0, 0))
    ins, in_specs = [dh, xin, g_row], [row, row, vec]
    out_shape, out_specs = [], []
    if has_dx:
        ins.append(dres)
        in_specs.append(row)
        out_shape.append(jax.ShapeDtypeStruct((S, D), jnp.float32))
        out_specs.append(row)
        if has_low:
            out_shape.append(jax.ShapeDtypeStruct((S, D), low_dtype))
            out_specs.append(row)
    out_shape.append(jax.ShapeDtypeStruct((1, D), jnp.float32))
    out_specs.append(vec)
    return pl.pallas_call(
        body, name=name, grid=(S // tm,), in_specs=in_specs, out_specs=out_specs, out_shape=out_shape,
        compiler_params=_params(1),
    )(*ins)


def _chunk_mask(blk):
    t = lax.broadcasted_iota(jnp.int32, (blk, blk), 0)
    s = lax.broadcasted_iota(jnp.int32, (blk, blk), 1)
    return (s // CHUNK) <= (t // CHUNK)


def _pool_inputs(a, halo, row0, tm, g, pg):
    cols = slice(g * pg, (g + 1) * pg)
    w = POOL_WINDOWS[g]
    s = jnp.concatenate([halo[:, cols], a[:, cols]], axis=0)
    span = 1
    while span < w:
        s = s + pltpu.roll(s, span, axis=0)
        span *= 2
    t = lax.broadcasted_iota(jnp.int32, (tm, 1), 0) + row0
    cnt = jnp.minimum(t + 1, w).astype(jnp.float32)
    return s[POOL_HALO:] / cnt - a[:, cols]


def _mixer_specs(S, tm, DP, DS):
    per_halo = tm // POOL_HALO
    n_halo = S // POOL_HALO
    a_spec = pl.BlockSpec((tm, DP), lambda i: (i, 0))
    prev_spec = pl.BlockSpec((POOL_HALO, DP), lambda i: (jnp.maximum(i * per_halo - 1, 0), 0))
    next_spec = pl.BlockSpec((POOL_HALO, DP), lambda i: (jnp.minimum((i + 1) * per_halo, n_halo - 1), 0))
    u_spec = pl.BlockSpec((tm, DS), lambda i: (i, 1))
    v_spec = pl.BlockSpec((tm, DS), lambda i: (i, 2))
    return a_spec, prev_spec, next_spec, u_spec, v_spec


def _whole(arr):
    nd = arr.ndim
    return pl.BlockSpec(arr.shape, lambda i: (0,) * nd)


def _mixer_fwd(proj, pw, ps_row, gs_row, ws, b_full, tm=512):
    S = proj.shape[0]
    G, PG, _ = pw.shape
    DP = G * PG
    H, BLK, _ = ws.shape
    DS = gs_row.shape[1]
    HD = DS // H
    assert DP == DS and proj.shape[1] == DP + 2 * DS
    tm = _tile(S, tm, BLK)

    def body(a_ref, prev_ref, u_ref, v_ref, pw_ref, ps_ref, gs_ref, ws_ref, b_ref, z_ref):
        i = pl.program_id(0)
        a = a_ref[...]
        halo = jnp.where(i > 0, prev_ref[...], 0.0)
        for g in range(G):
            cols = slice(g * PG, (g + 1) * PG)
            p = _pool_inputs(a, halo, i * tm, tm, g, PG)
            y = jnp.dot(p.astype(MXU_DTYPE), pw_ref[g], preferred_element_type=jnp.float32) * ps_ref[:, cols]
            z_ref[:, cols] = y.astype(z_ref.dtype)
        v = v_ref[...]
        vn = (v * _rsqrt_mean_sq(v) * gs_ref[...]).astype(MXU_DTYPE)
        u = u_ref[...]
        mask = _chunk_mask(BLK)
        for h in range(H):
            wm = jnp.where(mask, ws_ref[h], 0.0).astype(MXU_DTYPE)
            bias = b_ref[h]
            cols = slice(h * HD, (h + 1) * HD)
            for n in range(tm // BLK):
                rows = slice(n * BLK, (n + 1) * BLK)
                mixed = jnp.dot(wm, vn[rows, cols], preferred_element_type=jnp.float32) + bias
                z_ref[rows, DP + h * HD:DP + (h + 1) * HD] = (u[rows, cols] * mixed).astype(z_ref.dtype)

    a_spec, prev_spec, _, u_spec, v_spec = _mixer_specs(S, tm, DP, DS)
    return pl.pallas_call(
        body, name="mixer_fwd", grid=(S // tm,),
        in_specs=[a_spec, prev_spec, u_spec, v_spec, _whole(pw), _whole(ps_row), _whole(gs_row), _whole(ws), _whole(b_full)],
        out_specs=pl.BlockSpec((tm, DP + DS), lambda i: (i, 0)),
        out_shape=jax.ShapeDtypeStruct((S, DP + DS), MXU_DTYPE),
        compiler_params=_params(1),
    )(proj, proj, proj, proj, pw, ps_row, gs_row, ws, b_full)


def _mixer_bwd(proj, dz, pw, ps_row, gs_row, ws, b_full, tm=256):
    S = proj.shape[0]
    G, PG, _ = pw.shape
    DP = G * PG
    H, BLK, _ = ws.shape
    DS = gs_row.shape[1]
    HD = DS // H
    tm = _tile(S, tm, BLK)
    nb = S // tm
    ext = tm + POOL_HALO

    def body(a_ref, prev_ref, u_ref, v_ref, dzp_ref, dzp_next_ref, dzs_ref, pw_ref, ps_ref, gs_ref, ws_ref, b_ref,
             dproj_ref, dpw_ref, dps_ref, dgs_ref, dws_ref, db_ref, dvn_ref):
        i = pl.program_id(0)

        @pl.when(i == 0)
        def _():
            for r in (dpw_ref, dps_ref, dgs_ref, dws_ref, db_ref):
                r[...] = jnp.zeros_like(r)

        a = a_ref[...]
        halo = jnp.where(i > 0, prev_ref[...], 0.0)
        dyp = dzp_ref[...]
        dyp_next = jnp.where(i < nb - 1, dzp_next_ref[...], 0.0)
        dq_ext = jnp.concatenate([dyp, dyp_next], axis=0) * ps_ref[...]
        t_ext = lax.broadcasted_iota(jnp.int32, (ext, 1), 0) + i * tm
        for g in range(G):
            cols = slice(g * PG, (g + 1) * PG)
            w = POOL_WINDOWS[g]
            pb = _pool_inputs(a, halo, i * tm, tm, g, PG).astype(MXU_DTYPE)
            y_pre = jnp.dot(pb, pw_ref[g], preferred_element_type=jnp.float32)
            dps_ref[:, cols] += jnp.sum(dyp[:, cols] * y_pre, axis=0, keepdims=True)
            dqb = dq_ext[:, cols].astype(MXU_DTYPE)
            dpw_ref[g] += lax.dot_general(pb, dqb[:tm], TN, preferred_element_type=jnp.float32)
            dp = lax.dot_general(dqb, pw_ref[g], NT, preferred_element_type=jnp.float32)
            f = dp / jnp.minimum(t_ext + 1, w).astype(jnp.float32)
            span = 1
            while span < w:
                f = f + pltpu.roll(f, ext - span, axis=0)
                span *= 2
            dproj_ref[:, cols] = (f[:tm] - dp[:tm]).astype(dproj_ref.dtype)

        v = v_ref[...]
        rv = _rsqrt_mean_sq(v)
        vhat = v * rv
        gs = gs_ref[...]
        vnb = (vhat * gs).astype(MXU_DTYPE)
        u = u_ref[...]
        dys = dzs_ref[...]
        mask = _chunk_mask(BLK)
        for h in range(H):
            wm = jnp.where(mask, ws_ref[h], 0.0).astype(MXU_DTYPE)
            bias = b_ref[h]
            cols = slice(h * HD, (h + 1) * HD)
            dm_sum = jnp.zeros((BLK, HD), jnp.float32)
            dws_sum = jnp.zeros((BLK, BLK), jnp.float32)
            for n in range(tm // BLK):
                rows = slice(n * BLK, (n + 1) * BLK)
                vblk = vnb[rows, cols]
                mixed = jnp.dot(wm, vblk, preferred_element_type=jnp.float32) + bias
                dy = dys[rows, cols]
                dproj_ref[rows, DP + h * HD:DP + (h + 1) * HD] = (dy * mixed).astype(dproj_ref.dtype)
                dm = dy * u[rows, cols]
                dm_sum = dm_sum + dm
                dmb = dm.astype(MXU_DTYPE)
                dws_sum = dws_sum + lax.dot_general(dmb, vblk, NT, preferred_element_type=jnp.float32)
                dvn_ref[rows, cols] = lax.dot_general(wm, dmb, TN, preferred_element_type=jnp.float32)
            dws_ref[h] += jnp.where(mask, dws_sum, 0.0)
            db_ref[h:h + 1, :] += jnp.sum(dm_sum.T, axis=0, keepdims=True)
        dvn = dvn_ref[...]
        dvg = dvn * gs
        dv = rv * (dvg - vhat * jnp.mean(dvg * vhat, axis=-1, keepdims=True))
        dgs_ref[...] += jnp.sum(dvn * vhat, axis=0, keepdims=True)
        dproj_ref[:, DP + DS:] = dv.astype(dproj_ref.dtype)

    a_spec, prev_spec, next_spec, u_spec, v_spec = _mixer_specs(S, tm, DP, DS)
    dzp_spec = pl.BlockSpec((tm, DP), lambda i: (i, 0))
    dzs_spec = pl.BlockSpec((tm, DS), lambda i: (i, 1))
    small = [pw, ps_row, gs_row, ws, b_full]
    out_shape = [
        jax.ShapeDtypeStruct((S, DP + 2 * DS), MXU_DTYPE),
        jax.ShapeDtypeStruct(pw.shape, jnp.float32),
        jax.ShapeDtypeStruct(ps_row.shape, jnp.float32),
        jax.ShapeDtypeStruct(gs_row.shape, jnp.float32),
        jax.ShapeDtypeStruct(ws.shape, jnp.float32),
        jax.ShapeDtypeStruct((H, BLK), jnp.float32),
    ]
    out_specs = [pl.BlockSpec((tm, DP + 2 * DS), lambda i: (i, 0))] + [_whole(s) for s in out_shape[1:]]
    return pl.pallas_call(
        body, name="mixer_bwd", grid=(nb,),
        in_specs=[a_spec, prev_spec, u_spec, v_spec, dzp_spec, next_spec, dzs_spec] + [_whole(s) for s in small],
        out_specs=out_specs, out_shape=out_shape,
        scratch_shapes=[pltpu.VMEM((tm, DS), jnp.float32)],
        compiler_params=_params(1),
    )(proj, proj, proj, proj, dz, dz, dz, *small)


def _softmax_rows(q, k, scale):
    s = lax.dot_general(q, k, NT, preferred_element_type=jnp.float32) * scale
    e = jnp.exp(s - jnp.max(s, axis=-1, keepdims=True))
    return e / jnp.sum(e, axis=-1, keepdims=True)


def _attn_fwd(q, k, v, tm=512):
    S, D = q.shape
    M = k.shape[0]
    HD = D // N_XATTN_HEADS
    scale = HD ** -0.5
    tm = _tile(S, tm, 16)

    def body(q_ref, k_ref, v_ref, o_ref):
        for h in range(N_XATTN_HEADS):
            cols = slice(h * HD, (h + 1) * HD)
            p = _softmax_rows(q_ref[:, cols], k_ref[:, cols], scale)
            o = jnp.dot(p.astype(MXU_DTYPE), v_ref[:, cols], preferred_element_type=jnp.float32)
            o_ref[:, cols] = o.astype(o_ref.dtype)

    row = pl.BlockSpec((tm, D), lambda i: (i, 0))
    mem = pl.BlockSpec((M, D), lambda i: (0, 0))
    return pl.pallas_call(
        body, name="attn_fwd", grid=(S // tm,), in_specs=[row, mem, mem], out_specs=row,
        out_shape=jax.ShapeDtypeStruct((S, D), MXU_DTYPE), compiler_params=_params(1),
    )(q, k, v)


def _attn_bwd(q, k, v, do, tm=512):
    S, D = q.shape
    M = k.shape[0]
    HD = D // N_XATTN_HEADS
    scale = HD ** -0.5
    tm = _tile(S, tm, 16)

    def body(q_ref, k_ref, v_ref, do_ref, dq_ref, dk_ref, dv_ref):
        @pl.when(pl.program_id(0) == 0)
        def _():
            dk_ref[...] = jnp.zeros_like(dk_ref)
            dv_ref[...] = jnp.zeros_like(dv_ref)

        for h in range(N_XATTN_HEADS):
            cols = slice(h * HD, (h + 1) * HD)
            qh, kh, vh, doh = q_ref[:, cols], k_ref[:, cols], v_ref[:, cols], do_ref[:, cols]
            p = _softmax_rows(qh, kh, scale)
            dp = lax.dot_general(doh, vh, NT, preferred_element_type=jnp.float32)
            dv_ref[:, cols] += lax.dot_general(p.astype(MXU_DTYPE), doh, TN, preferred_element_type=jnp.float32)
            ds = (p * (dp - jnp.sum(dp * p, axis=-1, keepdims=True)) * scale).astype(MXU_DTYPE)
            dq_ref[:, cols] = jnp.dot(ds, kh, preferred_element_type=jnp.float32).astype(dq_ref.dtype)
            dk_ref[:, cols] += lax.dot_general(ds, qh, TN, preferred_element_type=jnp.float32)

    row = pl.BlockSpec((tm, D), lambda i: (i, 0))
    mem = pl.BlockSpec((M, D), lambda i: (0, 0))
    return pl.pallas_call(
        body, name="attn_bwd", grid=(S // tm,), in_specs=[row, mem, mem, row], out_specs=[row, mem, mem],
        out_shape=[jax.ShapeDtypeStruct((S, D), MXU_DTYPE), jax.ShapeDtypeStruct((M, D), jnp.float32),
                   jax.ShapeDtypeStruct((M, D), jnp.float32)],
        compiler_params=_params(1),
    )(q, k, v, do)


def _proj_residual_norm(name, a, w2d, resid, g_row, tm=256):
    M, K = a.shape
    N = w2d.shape[1]
    tm = _tile(M, tm, 16)

    def epilogue(accs, ex_refs, out_refs, i, j):
        x_new = ex_refs[0][...] + accs[0]
        out_refs[0][...] = x_new
        out_refs[1][...] = (x_new * _rsqrt_mean_sq(x_new) * ex_refs[1][...]).astype(MXU_DTYPE)

    row = lambda i, j, k: (i, 0)
    return _fused_matmul(
        name, (M // tm, 1, 1),
        [(a, (tm, K), row), (w2d, (K, N), lambda i, j, k: (0, 0))],
        [(0, 1, 0, NN)], [(tm, N)],
        [(resid, (tm, N), row), (g_row, (1, N), lambda i, j, k: (0, 0))],
        [((M, N), jnp.float32, (tm, N), row), ((M, N), MXU_DTYPE, (tm, N), row)], epilogue)


def _ffn_gate_up(h, wg3, wu3, tm=512):
    M, K = h.shape
    J, _, Nc = wg3.shape
    tm = _tile(M, tm, 16)

    def epilogue(accs, ex_refs, out_refs, i, j):
        gate, up = accs
        out_refs[0][...] = gate
        out_refs[1][...] = up
        out_refs[2][...] = (gate * _sigmoid(gate) * up).astype(MXU_DTYPE)

    col = lambda i, j, k: (i, j)
    wspec = lambda i, j, k: (j, 0, 0)
    return _fused_matmul(
        "ffn_gate_up", (M // tm, J, 1),
        [(h, (tm, K), lambda i, j, k: (i, 0)), (wg3, (None, K, Nc), wspec), (wu3, (None, K, Nc), wspec)],
        [(0, 1, 0, NN), (0, 2, 1, NN)], [(tm, Nc)] * 2, [],
        [((M, J * Nc), jnp.float32, (tm, Nc), col), ((M, J * Nc), jnp.float32, (tm, Nc), col),
         ((M, J * Nc), MXU_DTYPE, (tm, Nc), col)], epilogue)


def _ffn_down(act, wd2d, x2, tm=512, tn=512):
    M, F = act.shape
    D = wd2d.shape[1]
    tm, tn = _tile(M, tm, 16), _tile(D, tn, LANES)

    def epilogue(accs, ex_refs, out_refs, i, j):
        out_refs[0][...] = ex_refs[0][...] + accs[0]

    tile = lambda i, j, k: (i, j)
    return _fused_matmul(
        "ffn_down", (M // tm, D // tn, 1),
        [(act, (tm, F), lambda i, j, k: (i, 0)), (wd2d, (F, tn), lambda i, j, k: (0, j))],
        [(0, 1, 0, NN)], [(tm, tn)], [(x2, (tm, tn), tile)],
        [((M, D), jnp.float32, (tm, tn), tile)], epilogue)[0]


def _final_norm_loss(x3, target, g_row, tm=256):
    S, D = x3.shape
    tm = _tile(S, tm, 16)

    def body(x_ref, t_ref, g_ref, dx_ref, dxl_ref, loss_ref, dg_ref):
        x3v = x_ref[...]
        g = g_ref[...]
        r = _rsqrt_mean_sq(x3v)
        xhat = x3v * r
        diff = xhat * g - t_ref[...]
        dy = diff / D
        dyg = dy * g
        dx = r * (dyg - xhat * jnp.mean(dyg * xhat, axis=-1, keepdims=True))
        dx_ref[...] = dx
        dxl_ref[...] = dx.astype(MXU_DTYPE)

        @pl.when(pl.program_id(0) == 0)
        def _():
            loss_ref[...] = jnp.zeros_like(loss_ref)
            dg_ref[...] = jnp.zeros_like(dg_ref)

        loss_ref[...] += 0.5 * _sum_all(jnp.mean(diff * diff, axis=-1, keepdims=True))
        dg_ref[...] += jnp.sum(dy * xhat, axis=0, keepdims=True)

    row = pl.BlockSpec((tm, D), lambda i: (i, 0))
    vec = pl.BlockSpec((1, D), lambda i: (0, 0))
    return pl.pallas_call(
        body, name="final_norm_loss", grid=(S // tm,), in_specs=[row, row, vec],
        out_specs=[row, row, pl.BlockSpec((1, 1), lambda i: (0, 0)), vec],
        out_shape=[jax.ShapeDtypeStruct((S, D), jnp.float32), jax.ShapeDtypeStruct((S, D), MXU_DTYPE),
                   jax.ShapeDtypeStruct((1, 1), jnp.float32), jax.ShapeDtypeStruct((1, D), jnp.float32)],
        compiler_params=_params(1),
    )(x3, target, g_row)


def _ffn_down_bwd(dx3_low, wd2d, gate, up, n_shards, tm=512):
    M, D = dx3_low.shape
    F = wd2d.shape[0]
    tn = F // n_shards
    tm = _tile(M, tm, 16)

    def epilogue(accs, ex_refs, out_refs, i, j):
        dact = accs[0]
        gate_v, up_v = ex_refs[0][...], ex_refs[1][...]
        sg = _sigmoid(gate_v)
        out_refs[0][...] = (dact * up_v * (sg * (1.0 + gate_v * (1.0 - sg)))).astype(MXU_DTYPE)
        out_refs[1][...] = (dact * (gate_v * sg)).astype(MXU_DTYPE)

    col = lambda i, j, k: (i, j)
    return _fused_matmul(
        "ffn_down_bwd", (M // tm, n_shards, 1),
        [(dx3_low, (tm, D), lambda i, j, k: (i, 0)), (wd2d, (tn, D), lambda i, j, k: (j, 0))],
        [(0, 1, 0, NT)], [(tm, tn)],
        [(gate, (tm, tn), col), (up, (tm, tn), col)],
        [((M, F), MXU_DTYPE, (tm, tn), col)] * 2, epilogue)


def _row_tile(R, C, target_bytes=2 ** 21):
    return _tile(R, max(16, target_bytes // (4 * C)), 16)


def _prefetch_call(body, name, grid, pos, in_specs, out_specs, out_shape, ins):
    grid_spec = pltpu.PrefetchScalarGridSpec(
        num_scalar_prefetch=1, grid=grid, in_specs=in_specs, out_specs=out_specs)
    return pl.pallas_call(body, name=name, grid_spec=grid_spec, out_shape=out_shape,
                          compiler_params=_params(len(grid)))(pos, *ins)


def _cast_into_slab(name, x2d, dtype, pos):
    R, C = x2d.shape
    tr = _row_tile(R, C)

    def body(pos_ref, x_ref, o_ref):
        o_ref[...] = x_ref[...].astype(o_ref.dtype)

    return _prefetch_call(
        body, name, (R // tr,), pos, [pl.BlockSpec((tr, C), lambda i, pos: (i, 0))],
        pl.BlockSpec((None, tr, C), lambda i, pos: (pos[0], i, 0)),
        jax.ShapeDtypeStruct((N_CHIPS, R, C), dtype), [x2d])


def _chip_sum(name, partial, got, dtype, pos):
    J, R, C = partial.shape
    Rh = R // 2
    tr = _row_tile(Rh, C)

    def body(pos_ref, a_ref, b_ref, o_ref):
        o_ref[...] = (a_ref[...].astype(jnp.float32) + b_ref[...].astype(jnp.float32)).astype(o_ref.dtype)

    blk = pl.BlockSpec((None, tr, C), lambda j, i, pos: (j, i, 0))
    return _prefetch_call(
        body, name, (J, Rh // tr), pos,
        [pl.BlockSpec((None, None, tr, C), lambda j, i, pos: (j, pos[1], i, 0)), blk], blk,
        jax.ShapeDtypeStruct((J, Rh, C), dtype), [partial.reshape(J, 2, Rh, C), got])


def _owner_sum(name, chip_sums, others, pos):
    J, Rh, C = chip_sums.shape
    n = others.shape[0]
    tr = _row_tile(Rh, C * (n + 1))

    def body(pos_ref, own_ref, oth_ref, o_ref):
        acc = own_ref[...].astype(jnp.float32)
        for k in range(n):
            acc = acc + oth_ref[k].astype(jnp.float32)
        o_ref[...] = acc

    return _prefetch_call(
        body, name, (Rh // tr,), pos,
        [pl.BlockSpec((None, tr, C), lambda i, pos: (pos[0], i, 0)), pl.BlockSpec((n, tr, C), lambda i, pos: (0, i, 0))],
        pl.BlockSpec((None, tr, C), lambda i, pos: (pos[1], i, 0)),
        jax.ShapeDtypeStruct((2, Rh, C), jnp.float32), [chip_sums, others])


def _adamw_values(w, g, m, v):
    m = ADAM_B1 * m + (1.0 - ADAM_B1) * g
    v = ADAM_B2 * v + (1.0 - ADAM_B2) * (g * g)
    m_hat = m / (1.0 - ADAM_B1 ** ADAM_STEP)
    v_hat = v / (1.0 - ADAM_B2 ** ADAM_STEP)
    delta = -ADAM_LR * (m_hat / (jnp.sqrt(v_hat) + ADAM_EPS) + ADAM_WD * w)
    return delta, m, v


def _adamw(name, w, g, m, v):
    R, C = w.shape
    tr = _row_tile(R, C, 2 ** 20)

    def body(w_ref, g_ref, m_ref, v_ref, d_ref, mo_ref, vo_ref):
        d_ref[...], mo_ref[...], vo_ref[...] = _adamw_values(w_ref[...], g_ref[...], m_ref[...], v_ref[...])

    blk = pl.BlockSpec((tr, C), lambda i: (i, 0))
    return pl.pallas_call(
        body, name=name, grid=(R // tr,), in_specs=[blk] * 4, out_specs=[blk] * 3,
        out_shape=[jax.ShapeDtypeStruct((R, C), jnp.float32)] * 3, compiler_params=_params(1))(w, g, m, v)


def _adamw_small(g_packed, ws, ms, vs, row_counts):
    n = len(ws)

    def body(*refs):
        g_ref = refs[0]
        w_refs, m_refs, v_refs = refs[1:1 + n], refs[1 + n:1 + 2 * n], refs[1 + 2 * n:1 + 3 * n]
        outs = refs[1 + 3 * n:]
        off = 0
        for t in range(n):
            g = g_ref[off:off + row_counts[t], :]
            off += row_counts[t]
            delta, m_new, v_new = _adamw_values(w_refs[t][...], g, m_refs[t][...], v_refs[t][...])
            outs[4 * t][...] = g
            outs[4 * t + 1][...] = delta
            outs[4 * t + 2][...] = m_new
            outs[4 * t + 3][...] = v_new

    out_shape = []
    for t in range(n):
        out_shape += [jax.ShapeDtypeStruct((row_counts[t], LANES), jnp.float32)] * 4
    return pl.pallas_call(
        body, name="adamw_small", in_specs=[VMEM_WHOLE] * (1 + 3 * n), out_specs=[VMEM_WHOLE] * (4 * n),
        out_shape=out_shape, compiler_params=pltpu.CompilerParams(vmem_limit_bytes=VMEM_LIMIT_BYTES),
    )(g_packed, *ws, *ms, *vs)


def _position():
    x, y, c = lax.axis_index("x"), lax.axis_index("y"), lax.axis_index("c")
    other_chips = [(1 - x, y), (x, 1 - y), (1 - x, 1 - y)]
    return x, y, c, other_chips


def _half(rows, which):
    return pl.ds(which * (rows // 2), rows // 2)


def _chunk_index(shape, dtype):
    rows = shape[-2]
    total = math.prod(shape) * jnp.dtype(dtype).itemsize
    n = max(1, min(rows // COPY_ROW_ALIGN, total // COPY_CHUNK_BYTES))
    step = -(-rows // n)
    step = -(-step // COPY_ROW_ALIGN) * COPY_ROW_ALIGN
    lead = (slice(None),) * (len(shape) - 2)
    return [lead + (pl.ds(s, min(step, rows - s)),) for s in range(0, rows, step)]


def _start_in_chunks(make, src, dst, waited_here=True):
    for idx in _chunk_index(src.shape, src.dtype):
        make(src.at[idx], dst.at[idx]).start()
    return make(src, dst) if waited_here else None


def _remote_copier(send_sem, recv_sem, to):
    return lambda src, dst: pltpu.make_async_remote_copy(
        src_ref=src, dst_ref=dst, send_sem=send_sem, recv_sem=recv_sem, device_id=to, device_id_type=MESH)


def _slab_half(ref, chip, which):
    return ref.at[chip, _half(ref.shape[1], which)]


def _gather_start(groups):
    flat = [s for g in groups for s in g]
    T, G = len(flat), len(groups)

    def body(*refs):
        ins = refs[:T]
        sems = refs[T:T + 2 * G]
        token = refs[-1]
        x, y, c, other_chips = _position()
        mine = 2 * x + y
        t0 = 0
        for gi, group in enumerate(groups):
            for t in range(len(group)):
                own = _slab_half(ins[t0 + t], mine, c)
                for k, (cx, cy) in enumerate(other_chips):
                    _start_in_chunks(_remote_copier(sems[2 * gi].at[3 * t + k], sems[2 * gi + 1].at[3 * t + k], (cx, cy, c)),
                                     own, own, waited_here=False)
            t0 += len(group)
        token[...] = jnp.zeros_like(token)

    sem_shapes = []
    for group in groups:
        sem_shapes += [pltpu.SemaphoreType.DMA((3 * len(group),))] * 2
    res = pl.pallas_call(
        body, name="gather_start", in_specs=[HBM_SPEC] * T,
        out_specs=[SEM_SPEC] * (2 * G) + [HBM_SPEC] * T + [VMEM_WHOLE],
        out_shape=sem_shapes + [pltpu.HBM(s.shape, s.dtype) for s in flat] + [jax.ShapeDtypeStruct((8, LANES), jnp.float32)],
        input_output_aliases={t: 2 * G + t for t in range(T)},
        compiler_params=pltpu.CompilerParams(has_side_effects=SIDE_EFFECT),
    )(*[pltpu.with_memory_space_constraint(s, pltpu.HBM) for s in flat])
    flights, t0 = [], 2 * G
    for gi, group in enumerate(groups):
        flights.append((res[2 * gi], res[2 * gi + 1], res[t0:t0 + len(group)]))
        t0 += len(group)
    return flights, res[-1]


def _gather_wait(name, flight, after):
    send_sems, recv_sems, slabs = flight
    T = len(slabs)

    def body(*refs):
        ins = refs[:T]
        send, recv = refs[T:T + 2]
        x, y, c, other_chips = _position()
        mine = 2 * x + y
        for t in range(T):
            own = _slab_half(ins[t], mine, c)
            for k, (cx, cy) in enumerate(other_chips):
                landed = _slab_half(ins[t], 2 * cx + cy, c)
                cp = _remote_copier(send.at[3 * t + k], recv.at[3 * t + k], (cx, cy, c))(own, landed)
                cp.wait_send()
                cp.wait_recv()

    return pl.pallas_call(
        body, name=name, in_specs=[HBM_SPEC] * T + [SEM_SPEC, SEM_SPEC, ANY], out_specs=[HBM_SPEC] * T,
        out_shape=[pltpu.HBM(s.shape, s.dtype) for s in slabs], input_output_aliases={t: t for t in range(T)},
        compiler_params=pltpu.CompilerParams(has_side_effects=SIDE_EFFECT),
    )(*slabs, send_sems, recv_sems, after)


def _share_with_sibling(name, slabs):
    T = len(slabs)

    def body(*refs):
        outs = refs[T:2 * T]
        send_sems, recv_sems = refs[2 * T:]
        x, y, c, other_chips = _position()
        sibling = (x, y, 1 - c)
        started = []
        for t in range(T):
            for k, (cx, cy) in enumerate(other_chips):
                landed = _slab_half(outs[t], 2 * cx + cy, c)
                started.append(_start_in_chunks(_remote_copier(send_sems.at[t, k], recv_sems.at[t, k], sibling), landed, landed))
        for t in range(T):
            for k, (cx, cy) in enumerate(other_chips):
                arriving = _slab_half(outs[t], 2 * cx + cy, 1 - c)
                _remote_copier(send_sems.at[t, k], recv_sems.at[t, k], sibling)(arriving, arriving).wait_recv()
        for cp in started:
            cp.wait_send()

    return pl.pallas_call(
        body, name=name, in_specs=[ANY] * T, out_specs=[ANY] * T,
        out_shape=[jax.ShapeDtypeStruct(s.shape, s.dtype) for s in slabs],
        input_output_aliases={t: t for t in range(T)},
        scratch_shapes=[pltpu.SemaphoreType.DMA((T, 3)), pltpu.SemaphoreType.DMA((T, 3))],
    )(*slabs)


def _swap_halves(name, grads):
    T = len(grads)

    def body(*refs):
        ins, got = refs[:T], refs[T:2 * T]
        send_sems, recv_sems = refs[2 * T:]
        x, y, c, _ = _position()
        remote = []
        for t in range(T):
            rows = ins[t].shape[1]
            remote.append(_start_in_chunks(
                _remote_copier(send_sems.at[t], recv_sems.at[t], (x, y, 1 - c)), ins[t].at[:, _half(rows, 1 - c)], got[t]))
        for cp in remote:
            cp.wait()

    return pl.pallas_call(
        body, name=name, in_specs=[ANY] * T, out_specs=[ANY] * T,
        out_shape=[jax.ShapeDtypeStruct((g.shape[0], g.shape[1] // 2, g.shape[2]), g.dtype) for g in grads],
        scratch_shapes=[pltpu.SemaphoreType.DMA((T,)), pltpu.SemaphoreType.DMA((T,))],
    )(*grads)


def _scatter_start(name, chip_sums):
    T = len(chip_sums)
    lands = [lax.empty((3,) + s.shape[1:], s.dtype) for s in chip_sums]

    def body(*refs):
        srcs, dsts = refs[:T], refs[T:2 * T]
        send_sems, recv_sems = refs[2 * T:2 * T + 2]
        token = refs[-1]
        x, y, c, other_chips = _position()
        for t in range(T):
            for k, (cx, cy) in enumerate(other_chips):
                _start_in_chunks(_remote_copier(send_sems.at[3 * t + k], recv_sems.at[3 * t + k], (cx, cy, c)),
                                 srcs[t].at[2 * cx + cy], dsts[t].at[k], waited_here=False)
        token[...] = jnp.zeros_like(token)

    hbm = [pltpu.HBM(a.shape, a.dtype) for a in list(chip_sums) + lands]
    res = pl.pallas_call(
        body, name=name, in_specs=[HBM_SPEC] * (2 * T),
        out_specs=[SEM_SPEC, SEM_SPEC] + [HBM_SPEC] * (2 * T) + [VMEM_WHOLE],
        out_shape=[pltpu.SemaphoreType.DMA((3 * T,)), pltpu.SemaphoreType.DMA((3 * T,))] + hbm
        + [jax.ShapeDtypeStruct((8, LANES), jnp.float32)],
        input_output_aliases={i: 2 + i for i in range(2 * T)},
        compiler_params=pltpu.CompilerParams(has_side_effects=SIDE_EFFECT),
    )(*[pltpu.with_memory_space_constraint(a, pltpu.HBM) for a in list(chip_sums) + lands])
    return res[0], res[1], res[2:2 + T], res[2 + T:2 + 2 * T], res[-1]


def _scatter_wait(name, send_sems, recv_sems, chip_sums, lands, after):
    T = len(chip_sums)

    def body(*refs):
        srcs, dsts = refs[:T], refs[T:2 * T]
        send, recv = refs[2 * T:2 * T + 2]
        x, y, c, other_chips = _position()
        for t in range(T):
            for k, (cx, cy) in enumerate(other_chips):
                cp = _remote_copier(send.at[3 * t + k], recv.at[3 * t + k], (cx, cy, c))(srcs[t].at[2 * cx + cy], dsts[t].at[k])
                cp.wait_send()
                cp.wait_recv()

    arrays = list(chip_sums) + list(lands)
    res = pl.pallas_call(
        body, name=name, in_specs=[HBM_SPEC] * (2 * T) + [SEM_SPEC, SEM_SPEC, ANY],
        out_specs=[HBM_SPEC] * (2 * T), out_shape=[pltpu.HBM(a.shape, a.dtype) for a in arrays],
        input_output_aliases={i: i for i in range(2 * T)},
        compiler_params=pltpu.CompilerParams(has_side_effects=SIDE_EFFECT),
    )(*arrays, send_sems, recv_sems, after)
    return res[:T], res[T:]


def _chip_sums(tag, names, partial, pos):
    parts = [p.reshape((N_CHIPS, -1, p.shape[-1])) for p in partial]
    received = _swap_halves("swap_grad_halves_" + tag, parts)
    return [_chip_sum("chip_sum_" + n, p, r, WIRE_DTYPE, pos) for n, p, r in zip(names, parts, received)]


def _reduce_begin(tag, names, partial, pos):
    return _scatter_start("scatter_start_" + tag, _chip_sums(tag, names, partial, pos))


def _reduce_end(tag, names, flight, after, pos):
    send_sems, recv_sems, chip_sums, lands, _ = flight
    chip_sums, others = _scatter_wait("scatter_wait_" + tag, send_sems, recv_sems, chip_sums, lands, after)
    return {n: _owner_sum("owner_sum_" + n, cs, ot, pos) for n, cs, ot in zip(names, chip_sums, others)}


def _join_halves(halves):
    T = len(halves)

    def body(*refs):
        outs = refs[T:2 * T]
        send_sems, recv_sems = refs[2 * T:]
        x, y, c, _ = _position()

        def copier(t):
            return _remote_copier(send_sems.at[t], recv_sems.at[t], (x, y, 1 - c))

        remote = [_start_in_chunks(copier(t), outs[t].at[c], outs[t].at[c]) for t in range(T)]
        for t in range(T):
            copier(t)(outs[t].at[1 - c], outs[t].at[1 - c]).wait_recv()
        for cp in remote:
            cp.wait_send()

    return pl.pallas_call(
        body, name="join_grad_halves", in_specs=[ANY] * T, out_specs=[ANY] * T,
        out_shape=[jax.ShapeDtypeStruct(h.shape, h.dtype) for h in halves],
        input_output_aliases={t: t for t in range(T)},
        scratch_shapes=[pltpu.SemaphoreType.DMA((T,)), pltpu.SemaphoreType.DMA((T,))],
    )(*halves)


def _allreduce_small(part):
    rows = part.shape[0]

    def body(x_ref, out_ref, gath, send_sems, recv_sems):
        x, y, c, other_chips = _position()
        sibling = (x, y, 1 - c)

        def slot(px, py, pc):
            return gath.at[4 * px + 2 * py + pc]

        def copy(k, owner, to, src=None):
            return pltpu.make_async_remote_copy(
                src_ref=slot(*owner) if src is None else src, dst_ref=slot(*owner), send_sem=send_sems.at[k],
                recv_sem=recv_sems.at[k], device_id=to, device_id_type=MESH)

        gath[4 * x + 2 * y + c] = x_ref[...]
        first = [copy(0, (x, y, c), sibling, src=x_ref)]
        first += [copy(1 + k, (x, y, c), (cx, cy, c), src=x_ref) for k, (cx, cy) in enumerate(other_chips)]
        for cp in first:
            cp.start()
        passed = [copy(4 + k, (cx, cy, c), sibling) for k, (cx, cy) in enumerate(other_chips)]
        for k, (cx, cy) in enumerate(other_chips):
            copy(1 + k, (cx, cy, c), (x, y, c)).wait_recv()
            passed[k].start()
        copy(0, (x, y, 1 - c), (x, y, c)).wait_recv()
        for k, (cx, cy) in enumerate(other_chips):
            copy(4 + k, (cx, cy, 1 - c), (x, y, c)).wait_recv()
        for cp in first + passed:
            cp.wait_send()
        acc = gath[0]
        for d in range(1, N_DEV):
            acc = acc + gath[d]
        out_ref[...] = acc

    return pl.pallas_call(
        body, name="allreduce_small_grads", in_specs=[VMEM_WHOLE], out_specs=VMEM_WHOLE,
        out_shape=jax.ShapeDtypeStruct((rows, LANES), jnp.float32),
        scratch_shapes=[pltpu.VMEM((N_DEV, rows, LANES), jnp.float32), pltpu.SemaphoreType.DMA((7,)),
                        pltpu.SemaphoreType.DMA((7,))],
        compiler_params=pltpu.CompilerParams(vmem_limit_bytes=VMEM_LIMIT_BYTES),
    )(part)


BIG = ("w_in", "pool_w", "w_out", "w_q", "w_k", "w_v", "w_o", "w_gate", "w_up", "w_down")
SMALL = ("norm_mix_g", "pool_scale", "sgu_norm_g", "w_spatial", "b_spatial", "norm_xattn_g", "norm_mem_g",
         "norm_ffn_g", "final_norm_g")
GATHER_ORDER = (("w_in", "pool_w"), ("w_out",), ("w_q", "w_k", "w_v", "w_o"), ("w_gate", "w_up"), ("w_down",))
WEIGHTS = ("norm_mix_g", "w_in", "pool_w", "pool_scale", "sgu_norm_g", "w_spatial", "b_spatial", "w_out",
           "norm_xattn_g", "norm_mem_g", "w_q", "w_k", "w_v", "w_o", "norm_ffn_g", "w_gate", "w_up", "w_down",
           "final_norm_g")


def _as2d(a):
    return a.reshape(-1, a.shape[-1])


def _as_lanes(a):
    return a.reshape(-1, LANES)


def kernel(x, mem, norm_mix_g, w_in, pool_w, pool_scale, sgu_norm_g, w_spatial, b_spatial, w_out, norm_xattn_g, norm_mem_g, w_q, w_k, w_v, w_o, norm_ffn_g, w_gate, w_up, w_down, final_norm_g, loss_target, m_norm_mix_g, m_w_in, m_pool_w, m_pool_scale, m_sgu_norm_g, m_w_spatial, m_b_spatial, m_w_out, m_norm_xattn_g, m_norm_mem_g, m_w_q, m_w_k, m_w_v, m_w_o, m_norm_ffn_g, m_w_gate, m_w_up, m_w_down, m_final_norm_g, v_norm_mix_g, v_w_in, v_pool_w, v_pool_scale, v_sgu_norm_g, v_w_spatial, v_b_spatial, v_w_out, v_norm_xattn_g, v_norm_mem_g, v_w_q, v_w_k, v_w_v, v_w_o, v_norm_ffn_g, v_w_gate, v_w_up, v_w_down, v_final_norm_g):
    args = dict(locals())
    weights = {n: args[n] for n in WEIGHTS}
    moments_m = {n: args["m_" + n] for n in WEIGHTS}
    moments_v = {n: args["v_" + n] for n in WEIGHTS}
    low = MXU_DTYPE

    xs = x[0]
    mems = mem[0]
    target = loss_target[0]
    D = xs.shape[1]
    G, PG = pool_w.shape[1], pool_w.shape[3]

    pos = jnp.stack([2 * lax.axis_index("x") + lax.axis_index("y"), lax.axis_index("c")]).astype(jnp.int32)
    slabs = {n: _cast_into_slab("cast_" + n, _as2d(weights[n]), WIRE_DTYPE, pos) for n in BIG}
    flights, started = _gather_start([[slabs[n] for n in group] for group in GATHER_ORDER])

    def arrive(gi, after):
        landed = _gather_wait("gather_wait_%d" % gi, flights[gi], after)
        return dict(zip(GATHER_ORDER[gi], _share_with_sibling("gather_share_%d" % gi, landed)))

    def rows(a):
        return a.reshape(-1, a.shape[-1])

    g_mix, g_xattn, g_mem, g_ffn = norm_mix_g, norm_xattn_g, norm_mem_g, norm_ffn_g
    g_final = final_norm_g.reshape(1, D)
    ws = w_spatial[0]
    b_full = jnp.broadcast_to(b_spatial[0][:, :, None], ws.shape[:2] + (sgu_norm_g.shape[1] // ws.shape[0],))

    h1 = _rmsnorm("norm_mix", xs, g_mix, low, after=started)
    mn = _rmsnorm("norm_mem", mems, g_mem, low)
    got = arrive(0, mn)
    wg_in = got["w_in"]
    pw = got["pool_w"].reshape(N_CHIPS, G, PG // N_CHIPS, PG).transpose(1, 0, 2, 3).reshape(G, PG, PG)
    proj = _matmul_nn_cols("proj_in", h1, wg_in, jnp.float32)
    z = _mixer_fwd(proj, pw, pool_scale, sgu_norm_g, ws, b_full)
    full = {"w_out": rows(arrive(1, z)["w_out"])}
    x1, h2 = _proj_residual_norm("mix_out", z, full["w_out"], xs, g_xattn)
    full.update({n: rows(a) for n, a in arrive(2, h2).items()})
    q = _matmul_nn("xattn_q", h2, full["w_q"], low)
    k = _matmul_nn("xattn_k", mn, full["w_k"], low)
    vv = _matmul_nn("xattn_v", mn, full["w_v"], low)
    o = _attn_fwd(q, k, vv)
    x2, h3 = _proj_residual_norm("xattn_out", o, full["w_o"], x1, g_ffn)
    got = arrive(3, h3)
    wg_gate, wg_up = got["w_gate"], got["w_up"]
    gate, up, act = _ffn_gate_up(h3, wg_gate, wg_up)
    full["w_down"] = rows(arrive(4, act)["w_down"])
    x3 = _ffn_down(act, full["w_down"], x2)
    dx3, dx3_low, loss_part, d_final_g = _final_norm_loss(x3, target, g_final)

    dgate, dup = _ffn_down_bwd(dx3_low, full["w_down"], gate, up, N_CHIPS)
    (d_w_down,) = _matmul_tn("grad_w_down", act, [dx3_low], WIRE_DTYPE)
    (d_w_gate,) = _matmul_tn_cols("grad_w_gate", h3, [dgate], N_CHIPS, WIRE_DTYPE)
    (d_w_up,) = _matmul_tn_cols("grad_w_up", h3, [dup], N_CHIPS, WIRE_DTYPE)
    ffn_names = ("w_gate", "w_up", "w_down")
    ffn_flight = _reduce_begin("ffn", ffn_names, [d_w_gate, d_w_up, d_w_down], pos)
    dh3 = _matmul_nt_cols("ffn_dh", [dgate, dup], [wg_gate, wg_up], jnp.float32, after=ffn_flight[-1])
    dx2, dx2_low, d_ffn_g = _rmsnorm_bwd("norm_ffn_bwd", dh3, x2, g_ffn, dx3, low)

    do = _matmul_nt("xattn_do", [dx2_low], [full["w_o"]], low)
    (d_w_o,) = _matmul_tn("grad_w_o", o, [dx2_low], WIRE_DTYPE)
    dq, dk, dvv = _attn_bwd(q, k, vv, do)
    (d_w_q,) = _matmul_tn("grad_w_q", h2, [dq], WIRE_DTYPE)
    dh2 = _matmul_nt("xattn_dh", [dq], [full["w_q"]], jnp.float32)
    d_w_k, d_w_v = _matmul_tn("grad_w_kv", mn, [dk, dvv], WIRE_DTYPE)
    dmn = _matmul_nt("xattn_dmem", [dk, dvv], [full["w_k"], full["w_v"]], jnp.float32)
    (d_mem_g,) = _rmsnorm_bwd("norm_mem_bwd", dmn, mems, g_mem, None, None)
    dx1, dx1_low, d_xattn_g = _rmsnorm_bwd("norm_xattn_bwd", dh2, x1, g_xattn, dx2, low)

    (d_w_out,) = _matmul_tn("grad_w_out", z, [dx1_low], WIRE_DTYPE)
    attn_names = ("w_q", "w_k", "w_v", "w_o", "w_out")
    attn_flight = _reduce_begin("attn", attn_names, [d_w_q, d_w_k, d_w_v, d_w_o, d_w_out], pos)
    dz = _matmul_nt("mix_dz", [dx1_low], [full["w_out"]], jnp.float32, after=attn_flight[-1])
    dproj, d_pw, d_ps, d_gs, d_ws, d_b = _mixer_bwd(proj, dz, pw, pool_scale, sgu_norm_g, ws, b_full)
    (d_w_in,) = _matmul_tn_cols("grad_w_in", h1, [dproj], N_CHIPS, WIRE_DTYPE, tka=1024)
    d_pool_w = d_pw.reshape(G, N_CHIPS, PG // N_CHIPS, PG).transpose(1, 0, 2, 3).reshape(N_CHIPS, PG, PG).astype(WIRE_DTYPE)
    mix_names = ("w_in", "pool_w")
    mix_flight = _reduce_begin("mix", mix_names, [d_w_in, d_pool_w], pos)
    dh1 = _matmul_nt_cols("mix_dh", [dproj], [wg_in], jnp.float32, tn=1024, after=mix_flight[-1])
    grad_x, d_mix_g = _rmsnorm_bwd("norm_mix_bwd", dh1, xs, g_mix, dx1, None)

    reduced = _reduce_end("ffn", ffn_names, ffn_flight, grad_x, pos)
    reduced.update(_reduce_end("attn", attn_names, attn_flight, grad_x, pos))
    reduced.update(_reduce_end("mix", mix_names, mix_flight, grad_x, pos))
    grads = {n: g.reshape(-1, g.shape[-1]) for n, g in zip(BIG, _join_halves([reduced[n] for n in BIG]))}

    small_parts = {
        "norm_mix_g": d_mix_g, "pool_scale": d_ps, "sgu_norm_g": d_gs, "w_spatial": d_ws, "b_spatial": d_b,
        "norm_xattn_g": d_xattn_g, "norm_mem_g": d_mem_g, "norm_ffn_g": d_ffn_g, "final_norm_g": d_final_g,
    }
    row_counts = [small_parts[n].size // LANES for n in SMALL]
    packed = jnp.concatenate([_as_lanes(small_parts[n]) for n in SMALL], axis=0)
    small_sum = _allreduce_small(packed)

    out_grad, out_delta, out_m, out_v = {}, {}, {}, {}
    for n in BIG:
        shape = weights[n].shape
        g2d = grads[n]
        delta, m_new, v_new = _adamw("adamw_" + n, _as2d(weights[n]), g2d, _as2d(moments_m[n]), _as2d(moments_v[n]))
        out_grad[n], out_delta[n] = g2d.reshape(shape), delta.reshape(shape)
        out_m[n], out_v[n] = m_new.reshape(shape), v_new.reshape(shape)
    small_out = _adamw_small(
        small_sum, [_as_lanes(weights[n]) for n in SMALL], [_as_lanes(moments_m[n]) for n in SMALL],
        [_as_lanes(moments_v[n]) for n in SMALL], row_counts)
    for t, n in enumerate(SMALL):
        shape = weights[n].shape
        out_grad[n], out_delta[n], out_m[n], out_v[n] = (a.reshape(shape) for a in small_out[4 * t:4 * t + 4])

    loss = lax.psum(loss_part[0, 0], ("x", "y", "c"))
    return (loss, grad_x[None], *[out_grad[n] for n in WEIGHTS], *[out_delta[n] for n in WEIGHTS],
            *[out_m[n] for n in WEIGHTS], *[out_v[n] for n in WEIGHTS])
```

```python
import math

import jax
import jax.numpy as jnp
from jax import lax
from jax.experimental import pallas as pl
from jax.experimental.pallas import tpu as pltpu

MXU_DTYPE = jnp.bfloat16
WIRE_DTYPE = jnp.bfloat16

EPS = 1e-6
CHUNK = 64
POOL_WINDOWS = (2, 4, 8, 16)
POOL_HALO = 16
N_XATTN_HEADS = 4
ADAM_LR = 0.001
ADAM_B1 = 0.9
ADAM_B2 = 0.999
ADAM_EPS = 1e-08
ADAM_WD = 0.01
ADAM_STEP = 10

N_CHIPS = 4
N_DEV = 8
LANES = 128
VMEM_LIMIT_BYTES = 56 * 2 ** 20
COPY_CHUNK_BYTES = 512 * 2 ** 10
COPY_ROW_ALIGN = 16

MESH = pl.DeviceIdType.MESH
ANY = pl.BlockSpec(memory_space=pl.ANY)
VMEM_WHOLE = pl.BlockSpec(memory_space=pltpu.VMEM)
HBM_SPEC = pl.BlockSpec(memory_space=pltpu.HBM)
SEM_SPEC = pl.BlockSpec(memory_space=pltpu.SEMAPHORE)
SIDE_EFFECT = pltpu.SideEffectType.DATAFLOW_SIDE_EFFECTING

NN = (((1,), (0,)), ((), ()))
NT = (((1,), (1,)), ((), ()))
TN = (((0,), (0,)), ((), ()))


def _tile(n, target, mult):
    best = None
    for t in range(mult, min(n, target) + 1, mult):
        if n % t == 0:
            best = t
    return n if best is None else best


def _params(n_grid_axes):
    return pltpu.CompilerParams(dimension_semantics=("arbitrary",) * n_grid_axes, vmem_limit_bytes=VMEM_LIMIT_BYTES)


def _rsqrt_mean_sq(x):
    return lax.rsqrt(jnp.mean(x * x, axis=-1, keepdims=True) + EPS)


def _sum_all(x):
    return jnp.sum(jnp.sum(x, axis=1, keepdims=True), axis=0, keepdims=True)


def _sigmoid(x):
    return 0.5 * jnp.tanh(0.5 * x) + 0.5


def _store_epilogue(accs, ex_refs, out_refs, i, j):
    for a, o in zip(accs, out_refs):
        o[...] = a.astype(o.dtype)


def _fused_matmul(name, grid, operands, pairs, acc_shapes, extras, outs, epilogue):
    nk = grid[2]
    n_op, n_ex, n_out, n_acc = len(operands), len(extras), len(outs), len(acc_shapes)
    in_place = nk > 1 and epilogue is _store_epilogue and all(o[1] == jnp.float32 for o in outs)

    def body(*refs):
        op_refs = refs[:n_op]
        ex_refs = refs[n_op:n_op + n_ex]
        out_refs = refs[n_op + n_ex:n_op + n_ex + n_out]
        acc_refs = out_refs if in_place else refs[n_op + n_ex + n_out:]
        i, j, k = pl.program_id(0), pl.program_id(1), pl.program_id(2)
        loaded = {}

        def operand(n):
            if n not in loaded:
                loaded[n] = op_refs[n][...].astype(MXU_DTYPE)
            return loaded[n]

        def product(ia, ib, dims):
            return lax.dot_general(operand(ia), operand(ib), dims, preferred_element_type=jnp.float32)

        def products():
            parts = [None] * n_acc
            for ia, ib, iacc, dims in pairs:
                d = product(ia, ib, dims)
                parts[iacc] = d if parts[iacc] is None else parts[iacc] + d
            return parts

        if nk == 1:
            epilogue(products(), ex_refs, out_refs, i, j)
            return

        @pl.when(k == 0)
        def _():
            for a in range(n_acc):
                acc_refs[a][...] = jnp.zeros_like(acc_refs[a])

        for a, part in enumerate(products()):
            acc_refs[a][...] += part

        if not in_place:
            @pl.when(k == nk - 1)
            def _():
                epilogue([acc_refs[a][...] for a in range(n_acc)], ex_refs, out_refs, i, j)

    arrays = [o[0] for o in operands] + [e[0] for e in extras]
    in_specs = [pl.BlockSpec(o[1], o[2]) for o in operands]
    in_specs += [ANY if e[1] is None else pl.BlockSpec(e[1], e[2]) for e in extras]
    return pl.pallas_call(
        body,
        name=name,
        grid=grid,
        in_specs=in_specs,
        out_specs=[pl.BlockSpec(o[2], o[3]) for o in outs],
        out_shape=[jax.ShapeDtypeStruct(o[0], o[1]) for o in outs],
        scratch_shapes=[pltpu.VMEM(s, jnp.float32) for s in acc_shapes] if nk > 1 and not in_place else [],
        compiler_params=_params(3),
    )(*arrays)


def _matmul_nn(name, a, b, out_dtype, tm=1024, tn=512):
    M, K = a.shape
    N = b.shape[1]
    tm, tn = _tile(M, tm, 16), _tile(N, tn, LANES)
    return _fused_matmul(
        name, (M // tm, N // tn, 1),
        [(a, (tm, K), lambda i, j, k: (i, 0)), (b, (K, tn), lambda i, j, k: (0, j))],
        [(0, 1, 0, NN)], [(tm, tn)], [],
        [((M, N), out_dtype, (tm, tn), lambda i, j, k: (i, j))], _store_epilogue)[0]


def _matmul_nn_cols(name, a, b3, out_dtype, tm=1024):
    M, K = a.shape
    J, _, Nc = b3.shape
    tm = _tile(M, tm, 16)
    return _fused_matmul(
        name, (M // tm, J, 1),
        [(a, (tm, K), lambda i, j, k: (i, 0)), (b3, (None, K, Nc), lambda i, j, k: (j, 0, 0))],
        [(0, 1, 0, NN)], [(tm, Nc)], [],
        [((M, J * Nc), out_dtype, (tm, Nc), lambda i, j, k: (i, j))], _store_epilogue)[0]


def _order_after(after):
    return [] if after is None else [(after, None, None)]


def _matmul_nt(name, a_list, b_list, out_dtype, tm=1024, tn=512, after=None):
    M, K = a_list[0].shape
    N = b_list[0].shape[0]
    tm, tn = _tile(M, tm, 16), _tile(N, tn, LANES)
    n = len(a_list)
    operands = [(a, (tm, K), lambda i, j, k: (i, 0)) for a in a_list]
    operands += [(b, (tn, K), lambda i, j, k: (j, 0)) for b in b_list]
    return _fused_matmul(
        name, (M // tm, N // tn, 1), operands,
        [(p, n + p, 0, NT) for p in range(n)], [(tm, tn)], _order_after(after),
        [((M, N), out_dtype, (tm, tn), lambda i, j, k: (i, j))], _store_epilogue)[0]


def _matmul_nt_cols(name, a_list, b3_list, out_dtype, tm=1024, tn=512, after=None):
    M = a_list[0].shape[0]
    J, N, Kc = b3_list[0].shape
    tm, tn = _tile(M, tm, 16), _tile(N, tn, LANES)
    n = len(a_list)
    operands = [(a, (tm, Kc), lambda i, j, k: (i, k)) for a in a_list]
    operands += [(b, (None, tn, Kc), lambda i, j, k: (k, j, 0)) for b in b3_list]
    return _fused_matmul(
        name, (M // tm, N // tn, J), operands,
        [(p, n + p, 0, NT) for p in range(n)], [(tm, tn)], _order_after(after),
        [((M, N), out_dtype, (tm, tn), lambda i, j, k: (i, j))], _store_epilogue)[0]


def _matmul_tn(name, a, b_list, out_dtype, tka=512, tn=1024, tkm=4096):
    M, Ka = a.shape
    N = b_list[0].shape[1]
    tka, tn, tkm = _tile(Ka, tka, LANES), _tile(N, tn, LANES), _tile(M, tkm, 16)
    n = len(b_list)
    operands = [(a, (tkm, tka), lambda i, j, k: (k, i))]
    operands += [(b, (tkm, tn), lambda i, j, k: (k, j)) for b in b_list]
    return _fused_matmul(
        name, (Ka // tka, N // tn, M // tkm), operands,
        [(0, 1 + p, p, TN) for p in range(n)], [(tka, tn)] * n, [],
        [((Ka, N), out_dtype, (tka, tn), lambda i, j, k: (i, j))] * n, _store_epilogue)


def _matmul_tn_cols(name, a, b_list, n_shards, out_dtype, tka=512, tkm=4096, after=None):
    M, Ka = a.shape
    Nc = b_list[0].shape[1] // n_shards
    tka, tkm = _tile(Ka, tka, LANES), _tile(M, tkm, 16)
    n = len(b_list)
    operands = [(a, (tkm, tka), lambda i, j, k: (k, i))]
    operands += [(b, (tkm, Nc), lambda i, j, k: (k, j)) for b in b_list]
    return _fused_matmul(
        name, (Ka // tka, n_shards, M // tkm), operands,
        [(0, 1 + p, p, TN) for p in range(n)], [(tka, Nc)] * n, _order_after(after),
        [((n_shards, Ka, Nc), out_dtype, (None, tka, Nc), lambda i, j, k: (j, i, 0))] * n, _store_epilogue)


def _rmsnorm(name, x, g_row, out_dtype, tm=512, after=None):
    S, D = x.shape
    tm = _tile(S, tm, 16)

    def body(x_ref, g_ref, *rest):
        o_ref = rest[-1]
        xv = x_ref[...]
        o_ref[...] = (xv * _rsqrt_mean_sq(xv) * g_ref[...]).astype(o_ref.dtype)

    ordered = [] if after is None else [after]
    return pl.pallas_call(
        body, name=name, grid=(S // tm,),
        in_specs=[pl.BlockSpec((tm, D), lambda i: (i, 0)), pl.BlockSpec((1, D), lambda i: (0, 0))] + [ANY] * len(ordered),
        out_specs=pl.BlockSpec((tm, D), lambda i: (i, 0)),
        out_shape=jax.ShapeDtypeStruct((S, D), out_dtype),
        compiler_params=_params(1),
    )(x, g_row, *ordered)


def _rmsnorm_bwd_values(dh, xin, g):
    r = _rsqrt_mean_sq(xin)
    xhat = xin * r
    dhg = dh * g
    dx = r * (dhg - xhat * jnp.mean(dhg * xhat, axis=-1, keepdims=True))
    return dx, jnp.sum(dh * xhat, axis=0, keepdims=True)


def _rmsnorm_bwd(name, dh, xin, g_row, dres, low_dtype, tm=256):
    S, D = xin.shape
    tm = _tile(S, tm, 16)
    has_dx = dres is not None
    has_low = has_dx and low_dtype is not None

    def body(*refs):
        dh_ref, x_ref, g_ref = refs[:3]
        rest = refs[3:]
        dx, dg = _rmsnorm_bwd_values(dh_ref[...], x_ref[...], g_ref[...])
        if has_dx:
            dres_ref, rest = rest[0], rest[1:]
            dx = dx + dres_ref[...]
            rest[0][...] = dx
            if has_low:
                rest[1][...] = dx.astype(low_dtype)
        dg_ref = rest[-1]

        @pl.when(pl.program_id(0) == 0)
        def _():
            dg_ref[...] = jnp.zeros_like(dg_ref)

        dg_ref[...] += dg

    row = pl.BlockSpec((tm, D), lambda i: (i, 0))
    vec = pl.BlockSpec((1, D), lambda i: (0, 0))
    ins, in_specs = [dh, xin, g_row], [row, row, vec]
    out_shape, out_specs = [], []
    if has_dx:
        ins.append(dres)
        in_specs.append(row)
        out_shape.append(jax.ShapeDtypeStruct((S, D), jnp.float32))
        out_specs.append(row)
        if has_low:
            out_shape.append(jax.ShapeDtypeStruct((S, D), low_dtype))
            out_specs.append(row)
    out_shape.append(jax.ShapeDtypeStruct((1, D), jnp.float32))
    out_specs.append(vec)
    return pl.pallas_call(
        body, name=name, grid=(S // tm,), in_specs=in_specs, out_specs=out_specs, out_shape=out_shape,
        compiler_params=_params(1),
    )(*ins)


def _chunk_mask(blk):
    t = lax.broadcasted_iota(jnp.int32, (blk, blk), 0)
    s = lax.broadcasted_iota(jnp.int32, (blk, blk), 1)
    return (s // CHUNK) <= (t // CHUNK)


def _pool_inputs(a, halo, row0, tm, g, pg):
    cols = slice(g * pg, (g + 1) * pg)
    w = POOL_WINDOWS[g]
    s = jnp.concatenate([halo[:, cols], a[:, cols]], axis=0)
    span = 1
    while span < w:
        s = s + pltpu.roll(s, span, axis=0)
        span *= 2
    t = lax.broadcasted_iota(jnp.int32, (tm, 1), 0) + row0
    cnt = jnp.minimum(t + 1, w).astype(jnp.float32)
    return s[POOL_HALO:] / cnt - a[:, cols]


def _mixer_specs(S, tm, DP, DS):
    per_halo = tm // POOL_HALO
    n_halo = S // POOL_HALO
    a_spec = pl.BlockSpec((tm, DP), lambda i: (i, 0))
    prev_spec = pl.BlockSpec((POOL_HALO, DP), lambda i: (jnp.maximum(i * per_halo - 1, 0), 0))
    next_spec = pl.BlockSpec((POOL_HALO, DP), lambda i: (jnp.minimum((i + 1) * per_halo, n_halo - 1), 0))
    u_spec = pl.BlockSpec((tm, DS), lambda i: (i, 1))
    v_spec = pl.BlockSpec((tm, DS), lambda i: (i, 2))
    return a_spec, prev_spec, next_spec, u_spec, v_spec


def _whole(arr):
    nd = arr.ndim
    return pl.BlockSpec(arr.shape, lambda i: (0,) * nd)


def _mixer_fwd(proj, pw, ps_row, gs_row, ws, b_full, tm=512):
    S = proj.shape[0]
    G, PG, _ = pw.shape
    DP = G * PG
    H, BLK, _ = ws.shape
    DS = gs_row.shape[1]
    HD = DS // H
    assert DP == DS and proj.shape[1] == DP + 2 * DS
    tm = _tile(S, tm, BLK)

    def body(a_ref, prev_ref, u_ref, v_ref, pw_ref, ps_ref, gs_ref, ws_ref, b_ref, z_ref):
        i = pl.program_id(0)
        a = a_ref[...]
        halo = jnp.where(i > 0, prev_ref[...], 0.0)
        for g in range(G):
            cols = slice(g * PG, (g + 1) * PG)
            p = _pool_inputs(a, halo, i * tm, tm, g, PG)
            y = jnp.dot(p.astype(MXU_DTYPE), pw_ref[g], preferred_element_type=jnp.float32) * ps_ref[:, cols]
            z_ref[:, cols] = y.astype(z_ref.dtype)
        v = v_ref[...]
        vn = (v * _rsqrt_mean_sq(v) * gs_ref[...]).astype(MXU_DTYPE)
        u = u_ref[...]
        mask = _chunk_mask(BLK)
        for h in range(H):
            wm = jnp.where(mask, ws_ref[h], 0.0).astype(MXU_DTYPE)
            bias = b_ref[h]
            cols = slice(h * HD, (h + 1) * HD)
            for n in range(tm // BLK):
                rows = slice(n * BLK, (n + 1) * BLK)
                mixed = jnp.dot(wm, vn[rows, cols], preferred_element_type=jnp.float32) + bias
                z_ref[rows, DP + h * HD:DP + (h + 1) * HD] = (u[rows, cols] * mixed).astype(z_ref.dtype)

    a_spec, prev_spec, _, u_spec, v_spec = _mixer_specs(S, tm, DP, DS)
    return pl.pallas_call(
        body, name="mixer_fwd", grid=(S // tm,),
        in_specs=[a_spec, prev_spec, u_spec, v_spec, _whole(pw), _whole(ps_row), _whole(gs_row), _whole(ws), _whole(b_full)],
        out_specs=pl.BlockSpec((tm, DP + DS), lambda i: (i, 0)),
        out_shape=jax.ShapeDtypeStruct((S, DP + DS), MXU_DTYPE),
        compiler_params=_params(1),
    )(proj, proj, proj, proj, pw, ps_row, gs_row, ws, b_full)


def _mixer_bwd(proj, dz, pw, ps_row, gs_row, ws, b_full, tm=256):
    S = proj.shape[0]
    G, PG, _ = pw.shape
    DP = G * PG
    H, BLK, _ = ws.shape
    DS = gs_row.shape[1]
    HD = DS // H
    tm = _tile(S, tm, BLK)
    nb = S // tm
    ext = tm + POOL_HALO

    def body(a_ref, prev_ref, u_ref, v_ref, dzp_ref, dzp_next_ref, dzs_ref, pw_ref, ps_ref, gs_ref, ws_ref, b_ref,
             dproj_ref, dpw_ref, dps_ref, dgs_ref, dws_ref, db_ref, dvn_ref):
        i = pl.program_id(0)

        @pl.when(i == 0)
        def _():
            for r in (dpw_ref, dps_ref, dgs_ref, dws_ref, db_ref):
                r[...] = jnp.zeros_like(r)

        a = a_ref[...]
        halo = jnp.where(i > 0, prev_ref[...], 0.0)
        dyp = dzp_ref[...]
        dyp_next = jnp.where(i < nb - 1, dzp_next_ref[...], 0.0)
        dq_ext = jnp.concatenate([dyp, dyp_next], axis=0) * ps_ref[...]
        t_ext = lax.broadcasted_iota(jnp.int32, (ext, 1), 0) + i * tm
        for g in range(G):
            cols = slice(g * PG, (g + 1) * PG)
            w = POOL_WINDOWS[g]
            pb = _pool_inputs(a, halo, i * tm, tm, g, PG).astype(MXU_DTYPE)
            y_pre = jnp.dot(pb, pw_ref[g], preferred_element_type=jnp.float32)
            dps_ref[:, cols] += jnp.sum(dyp[:, cols] * y_pre, axis=0, keepdims=True)
            dqb = dq_ext[:, cols].astype(MXU_DTYPE)
            dpw_ref[g] += lax.dot_general(pb, dqb[:tm], TN, preferred_element_type=jnp.float32)
            dp = lax.dot_general(dqb, pw_ref[g], NT, preferred_element_type=jnp.float32)
            f = dp / jnp.minimum(t_ext + 1, w).astype(jnp.float32)
            span = 1
            while span < w:
                f = f + pltpu.roll(f, ext - span, axis=0)
                span *= 2
            dproj_ref[:, cols] = (f[:tm] - dp[:tm]).astype(dproj_ref.dtype)

        v = v_ref[...]
        rv = _rsqrt_mean_sq(v)
        vhat = v * rv
        gs = gs_ref[...]
        vnb = (vhat * gs).astype(MXU_DTYPE)
        u = u_ref[...]
        dys = dzs_ref[...]
        mask = _chunk_mask(BLK)
        for h in range(H):
            wm = jnp.where(mask, ws_ref[h], 0.0).astype(MXU_DTYPE)
            bias = b_ref[h]
            cols = slice(h * HD, (h + 1) * HD)
            dm_sum = jnp.zeros((BLK, HD), jnp.float32)
            dws_sum = jnp.zeros((BLK, BLK), jnp.float32)
            for n in range(tm // BLK):
                rows = slice(n * BLK, (n + 1) * BLK)
                vblk = vnb[rows, cols]
                mixed = jnp.dot(wm, vblk, preferred_element_type=jnp.float32) + bias
                dy = dys[rows, cols]
                dproj_ref[rows, DP + h * HD:DP + (h + 1) * HD] = (dy * mixed).astype(dproj_ref.dtype)
                dm = dy * u[rows, cols]
                dm_sum = dm_sum + dm
                dmb = dm.astype(MXU_DTYPE)
                dws_sum = dws_sum + lax.dot_general(dmb, vblk, NT, preferred_element_type=jnp.float32)
                dvn_ref[rows, cols] = lax.dot_general(wm, dmb, TN, preferred_element_type=jnp.float32)
            dws_ref[h] += jnp.where(mask, dws_sum, 0.0)
            db_ref[h:h + 1, :] += jnp.sum(dm_sum.T, axis=0, keepdims=True)
        dvn = dvn_ref[...]
        dvg = dvn * gs
        dv = rv * (dvg - vhat * jnp.mean(dvg * vhat, axis=-1, keepdims=True))
        dgs_ref[...] += jnp.sum(dvn * vhat, axis=0, keepdims=True)
        dproj_ref[:, DP + DS:] = dv.astype(dproj_ref.dtype)

    a_spec, prev_spec, next_spec, u_spec, v_spec = _mixer_specs(S, tm, DP, DS)
    dzp_spec = pl.BlockSpec((tm, DP), lambda i: (i, 0))
    dzs_spec = pl.BlockSpec((tm, DS), lambda i: (i, 1))
    small = [pw, ps_row, gs_row, ws, b_full]
    out_shape = [
        jax.ShapeDtypeStruct((S, DP + 2 * DS), MXU_DTYPE),
        jax.ShapeDtypeStruct(pw.shape, jnp.float32),
        jax.ShapeDtypeStruct(ps_row.shape, jnp.float32),
        jax.ShapeDtypeStruct(gs_row.shape, jnp.float32),
        jax.ShapeDtypeStruct(ws.shape, jnp.float32),
        jax.ShapeDtypeStruct((H, BLK), jnp.float32),
    ]
    out_specs = [pl.BlockSpec((tm, DP + 2 * DS), lambda i: (i, 0))] + [_whole(s) for s in out_shape[1:]]
    return pl.pallas_call(
        body, name="mixer_bwd", grid=(nb,),
        in_specs=[a_spec, prev_spec, u_spec, v_spec, dzp_spec, next_spec, dzs_spec] + [_whole(s) for s in small],
        out_specs=out_specs, out_shape=out_shape,
        scratch_shapes=[pltpu.VMEM((tm, DS), jnp.float32)],
        compiler_params=_params(1),
    )(proj, proj, proj, proj, dz, dz, dz, *small)


def _softmax_rows(q, k, scale):
    s = lax.dot_general(q, k, NT, preferred_element_type=jnp.float32) * scale
    e = jnp.exp(s - jnp.max(s, axis=-1, keepdims=True))
    return e / jnp.sum(e, axis=-1, keepdims=True)


def _attn_fwd(q, k, v, tm=512):
    S, D = q.shape
    M = k.shape[0]
    HD = D // N_XATTN_HEADS
    scale = HD ** -0.5
    tm = _tile(S, tm, 16)

    def body(q_ref, k_ref, v_ref, o_ref):
        for h in range(N_XATTN_HEADS):
            cols = slice(h * HD, (h + 1) * HD)
            p = _softmax_rows(q_ref[:, cols], k_ref[:, cols], scale)
            o = jnp.dot(p.astype(MXU_DTYPE), v_ref[:, cols], preferred_element_type=jnp.float32)
            o_ref[:, cols] = o.astype(o_ref.dtype)

    row = pl.BlockSpec((tm, D), lambda i: (i, 0))
    mem = pl.BlockSpec((M, D), lambda i: (0, 0))
    return pl.pallas_call(
        body, name="attn_fwd", grid=(S // tm,), in_specs=[row, mem, mem], out_specs=row,
        out_shape=jax.ShapeDtypeStruct((S, D), MXU_DTYPE), compiler_params=_params(1),
    )(q, k, v)


def _attn_bwd(q, k, v, do, tm=512):
    S, D = q.shape
    M = k.shape[0]
    HD = D // N_XATTN_HEADS
    scale = HD ** -0.5
    tm = _tile(S, tm, 16)

    def body(q_ref, k_ref, v_ref, do_ref, dq_ref, dk_ref, dv_ref):
        @pl.when(pl.program_id(0) == 0)
        def _():
            dk_ref[...] = jnp.zeros_like(dk_ref)
            dv_ref[...] = jnp.zeros_like(dv_ref)

        for h in range(N_XATTN_HEADS):
            cols = slice(h * HD, (h + 1) * HD)
            qh, kh, vh, doh = q_ref[:, cols], k_ref[:, cols], v_ref[:, cols], do_ref[:, cols]
            p = _softmax_rows(qh, kh, scale)
            dp = lax.dot_general(doh, vh, NT, preferred_element_type=jnp.float32)
            dv_ref[:, cols] += lax.dot_general(p.astype(MXU_DTYPE), doh, TN, preferred_element_type=jnp.float32)
            ds = (p * (dp - jnp.sum(dp * p, axis=-1, keepdims=True)) * scale).astype(MXU_DTYPE)
            dq_ref[:, cols] = jnp.dot(ds, kh, preferred_element_type=jnp.float32).astype(dq_ref.dtype)
            dk_ref[:, cols] += lax.dot_general(ds, qh, TN, preferred_element_type=jnp.float32)

    row = pl.BlockSpec((tm, D), lambda i: (i, 0))
    mem = pl.BlockSpec((M, D), lambda i: (0, 0))
    return pl.pallas_call(
        body, name="attn_bwd", grid=(S // tm,), in_specs=[row, mem, mem, row], out_specs=[row, mem, mem],
        out_shape=[jax.ShapeDtypeStruct((S, D), MXU_DTYPE), jax.ShapeDtypeStruct((M, D), jnp.float32),
                   jax.ShapeDtypeStruct((M, D), jnp.float32)],
        compiler_params=_params(1),
    )(q, k, v, do)


def _proj_residual_norm(name, a, w2d, resid, g_row, tm=256):
    M, K = a.shape
    N = w2d.shape[1]
    tm = _tile(M, tm, 16)

    def epilogue(accs, ex_refs, out_refs, i, j):
        x_new = ex_refs[0][...] + accs[0]
        out_refs[0][...] = x_new
        out_refs[1][...] = (x_new * _rsqrt_mean_sq(x_new) * ex_refs[1][...]).astype(MXU_DTYPE)

    row = lambda i, j, k: (i, 0)
    return _fused_matmul(
        name, (M // tm, 1, 1),
        [(a, (tm, K), row), (w2d, (K, N), lambda i, j, k: (0, 0))],
        [(0, 1, 0, NN)], [(tm, N)],
        [(resid, (tm, N), row), (g_row, (1, N), lambda i, j, k: (0, 0))],
        [((M, N), jnp.float32, (tm, N), row), ((M, N), MXU_DTYPE, (tm, N), row)], epilogue)


def _ffn_gate_up(h, wg3, wu3, tm=512):
    M, K = h.shape
    J, _, Nc = wg3.shape
    tm = _tile(M, tm, 16)

    def epilogue(accs, ex_refs, out_refs, i, j):
        gate, up = accs
        out_refs[0][...] = gate
        out_refs[1][...] = up
        out_refs[2][...] = (gate * _sigmoid(gate) * up).astype(MXU_DTYPE)

    col = lambda i, j, k: (i, j)
    wspec = lambda i, j, k: (j, 0, 0)
    return _fused_matmul(
        "ffn_gate_up", (M // tm, J, 1),
        [(h, (tm, K), lambda i, j, k: (i, 0)), (wg3, (None, K, Nc), wspec), (wu3, (None, K, Nc), wspec)],
        [(0, 1, 0, NN), (0, 2, 1, NN)], [(tm, Nc)] * 2, [],
        [((M, J * Nc), jnp.float32, (tm, Nc), col), ((M, J * Nc), jnp.float32, (tm, Nc), col),
         ((M, J * Nc), MXU_DTYPE, (tm, Nc), col)], epilogue)


def _ffn_down(act, wd2d, x2, tm=512, tn=512):
    M, F = act.shape
    D = wd2d.shape[1]
    tm, tn = _tile(M, tm, 16), _tile(D, tn, LANES)

    def epilogue(accs, ex_refs, out_refs, i, j):
        out_refs[0][...] = ex_refs[0][...] + accs[0]

    tile = lambda i, j, k: (i, j)
    return _fused_matmul(
        "ffn_down", (M // tm, D // tn, 1),
        [(act, (tm, F), lambda i, j, k: (i, 0)), (wd2d, (F, tn), lambda i, j, k: (0, j))],
        [(0, 1, 0, NN)], [(tm, tn)], [(x2, (tm, tn), tile)],
        [((M, D), jnp.float32, (tm, tn), tile)], epilogue)[0]


def _final_norm_loss(x3, target, g_row, tm=256):
    S, D = x3.shape
    tm = _tile(S, tm, 16)

    def body(x_ref, t_ref, g_ref, dx_ref, dxl_ref, loss_ref, dg_ref):
        x3v = x_ref[...]
        g = g_ref[...]
        r = _rsqrt_mean_sq(x3v)
        xhat = x3v * r
        diff = xhat * g - t_ref[...]
        dy = diff / D
        dyg = dy * g
        dx = r * (dyg - xhat * jnp.mean(dyg * xhat, axis=-1, keepdims=True))
        dx_ref[...] = dx
        dxl_ref[...] = dx.astype(MXU_DTYPE)

        @pl.when(pl.program_id(0) == 0)
        def _():
            loss_ref[...] = jnp.zeros_like(loss_ref)
            dg_ref[...] = jnp.zeros_like(dg_ref)

        loss_ref[...] += 0.5 * _sum_all(jnp.mean(diff * diff, axis=-1, keepdims=True))
        dg_ref[...] += jnp.sum(dy * xhat, axis=0, keepdims=True)

    row = pl.BlockSpec((tm, D), lambda i: (i, 0))
    vec = pl.BlockSpec((1, D), lambda i: (0, 0))
    return pl.pallas_call(
        body, name="final_norm_loss", grid=(S // tm,), in_specs=[row, row, vec],
        out_specs=[row, row, pl.BlockSpec((1, 1), lambda i: (0, 0)), vec],
        out_shape=[jax.ShapeDtypeStruct((S, D), jnp.float32), jax.ShapeDtypeStruct((S, D), MXU_DTYPE),
                   jax.ShapeDtypeStruct((1, 1), jnp.float32), jax.ShapeDtypeStruct((1, D), jnp.float32)],
        compiler_params=_params(1),
    )(x3, target, g_row)


def _ffn_down_bwd(dx3_low, wd2d, gate, up, n_shards, tm=512):
    M, D = dx3_low.shape
    F = wd2d.shape[0]
    tn = F // n_shards
    tm = _tile(M, tm, 16)

    def epilogue(accs, ex_refs, out_refs, i, j):
        dact = accs[0]
        gate_v, up_v = ex_refs[0][...], ex_refs[1][...]
        sg = _sigmoid(gate_v)
        out_refs[0][...] = (dact * up_v * (sg * (1.0 + gate_v * (1.0 - sg)))).astype(MXU_DTYPE)
        out_refs[1][...] = (dact * (gate_v * sg)).astype(MXU_DTYPE)

    col = lambda i, j, k: (i, j)
    return _fused_matmul(
        "ffn_down_bwd", (M // tm, n_shards, 1),
        [(dx3_low, (tm, D), lambda i, j, k: (i, 0)), (wd2d, (tn, D), lambda i, j, k: (j, 0))],
        [(0, 1, 0, NT)], [(tm, tn)],
        [(gate, (tm, tn), col), (up, (tm, tn), col)],
        [((M, F), MXU_DTYPE, (tm, tn), col)] * 2, epilogue)


def _row_tile(R, C, target_bytes=2 ** 21):
    return _tile(R, max(16, target_bytes // (4 * C)), 16)


def _prefetch_call(body, name, grid, pos, in_specs, out_specs, out_shape, ins, after=None):
    ordered = [] if after is None else [after]
    grid_spec = pltpu.PrefetchScalarGridSpec(
        num_scalar_prefetch=1, grid=grid, in_specs=list(in_specs) + [ANY] * len(ordered), out_specs=out_specs)
    return pl.pallas_call(body, name=name, grid_spec=grid_spec, out_shape=out_shape,
                          compiler_params=_params(len(grid)))(pos, *ins, *ordered)


def _cast_into_slab(name, x2d, dtype, pos, after=None):
    R, C = x2d.shape
    tr = _row_tile(R, C)

    def body(pos_ref, x_ref, *rest):
        o_ref = rest[-1]
        o_ref[...] = x_ref[...].astype(o_ref.dtype)

    return _prefetch_call(
        body, name, (R // tr,), pos, [pl.BlockSpec((tr, C), lambda i, pos: (i, 0))],
        pl.BlockSpec((None, tr, C), lambda i, pos: (pos[0], i, 0)),
        jax.ShapeDtypeStruct((N_CHIPS, R, C), dtype), [x2d], after)


def _chip_sum(name, partial, got, dtype, pos):
    J, R, C = partial.shape
    Rh = R // 2
    tr = _row_tile(Rh, C)

    def body(pos_ref, a_ref, b_ref, o_ref):
        o_ref[...] = (a_ref[...].astype(jnp.float32) + b_ref[...].astype(jnp.float32)).astype(o_ref.dtype)

    blk = pl.BlockSpec((None, tr, C), lambda j, i, pos: (j, i, 0))
    return _prefetch_call(
        body, name, (J, Rh // tr), pos,
        [pl.BlockSpec((None, None, tr, C), lambda j, i, pos: (j, pos[1], i, 0)), blk], blk,
        jax.ShapeDtypeStruct((J, Rh, C), dtype), [partial.reshape(J, 2, Rh, C), got])


def _owner_sum(name, chip_sums, others, pos):
    J, Rh, C = chip_sums.shape
    n = others.shape[0]
    tr = _row_tile(Rh, C * (n + 1))

    def body(pos_ref, own_ref, oth_ref, o_ref):
        acc = own_ref[...].astype(jnp.float32)
        for k in range(n):
            acc = acc + oth_ref[k].astype(jnp.float32)
        o_ref[...] = acc

    return _prefetch_call(
        body, name, (Rh // tr,), pos,
        [pl.BlockSpec((None, tr, C), lambda i, pos: (pos[0], i, 0)), pl.BlockSpec((n, tr, C), lambda i, pos: (0, i, 0))],
        pl.BlockSpec((None, tr, C), lambda i, pos: (pos[1], i, 0)),
        jax.ShapeDtypeStruct((2, Rh, C), jnp.float32), [chip_sums, others])


def _adamw_values(w, g, m, v):
    m = ADAM_B1 * m + (1.0 - ADAM_B1) * g
    v = ADAM_B2 * v + (1.0 - ADAM_B2) * (g * g)
    m_hat = m / (1.0 - ADAM_B1 ** ADAM_STEP)
    v_hat = v / (1.0 - ADAM_B2 ** ADAM_STEP)
    delta = -ADAM_LR * (m_hat / (jnp.sqrt(v_hat) + ADAM_EPS) + ADAM_WD * w)
    return delta, m, v


def _adamw(name, w, g, m, v):
    R, C = w.shape
    tr = _row_tile(R, C, 2 ** 20)

    def body(w_ref, g_ref, m_ref, v_ref, go_ref, d_ref, mo_ref, vo_ref):
        g = g_ref[...]
        go_ref[...] = g
        d_ref[...], mo_ref[...], vo_ref[...] = _adamw_values(w_ref[...], g, m_ref[...], v_ref[...])

    blk = pl.BlockSpec((tr, C), lambda i: (i, 0))
    return pl.pallas_call(
        body, name=name, grid=(R // tr,), in_specs=[blk] * 4, out_specs=[blk] * 4,
        out_shape=[jax.ShapeDtypeStruct((R, C), jnp.float32)] * 4, compiler_params=_params(1))(w, g, m, v)


def _adamw_small(g_packed, ws, ms, vs, row_counts):
    n = len(ws)

    def body(*refs):
        g_ref = refs[0]
        w_refs, m_refs, v_refs = refs[1:1 + n], refs[1 + n:1 + 2 * n], refs[1 + 2 * n:1 + 3 * n]
        outs = refs[1 + 3 * n:]
        off = 0
        for t in range(n):
            g = g_ref[off:off + row_counts[t], :]
            off += row_counts[t]
            delta, m_new, v_new = _adamw_values(w_refs[t][...], g, m_refs[t][...], v_refs[t][...])
            outs[4 * t][...] = g
            outs[4 * t + 1][...] = delta
            outs[4 * t + 2][...] = m_new
            outs[4 * t + 3][...] = v_new

    out_shape = []
    for t in range(n):
        out_shape += [jax.ShapeDtypeStruct((row_counts[t], LANES), jnp.float32)] * 4
    return pl.pallas_call(
        body, name="adamw_small", in_specs=[VMEM_WHOLE] * (1 + 3 * n), out_specs=[VMEM_WHOLE] * (4 * n),
        out_shape=out_shape, compiler_params=pltpu.CompilerParams(vmem_limit_bytes=VMEM_LIMIT_BYTES),
    )(g_packed, *ws, *ms, *vs)


def _position():
    x, y, c = lax.axis_index("x"), lax.axis_index("y"), lax.axis_index("c")
    other_chips = [(1 - x, y), (x, 1 - y), (1 - x, 1 - y)]
    return x, y, c, other_chips


def _half(rows, which):
    return pl.ds(which * (rows // 2), rows // 2)


def _chunk_index(shape, dtype):
    rows = shape[-2]
    total = math.prod(shape) * jnp.dtype(dtype).itemsize
    n = max(1, min(rows // COPY_ROW_ALIGN, total // COPY_CHUNK_BYTES))
    step = -(-rows // n)
    step = -(-step // COPY_ROW_ALIGN) * COPY_ROW_ALIGN
    lead = (slice(None),) * (len(shape) - 2)
    return [lead + (pl.ds(s, min(step, rows - s)),) for s in range(0, rows, step)]


def _start_in_chunks(make, src, dst, waited_here=True):
    for idx in _chunk_index(src.shape, src.dtype):
        make(src.at[idx], dst.at[idx]).start()
    return make(src, dst) if waited_here else None


def _remote_copier(send_sem, recv_sem, to):
    return lambda src, dst: pltpu.make_async_remote_copy(
        src_ref=src, dst_ref=dst, send_sem=send_sem, recv_sem=recv_sem, device_id=to, device_id_type=MESH)


def _slab_half(ref, chip, which):
    return ref.at[chip, _half(ref.shape[1], which)]


def _gather_start(name, groups, after=None):
    flat = [s for g in groups for s in g]
    T, G = len(flat), len(groups)
    ordered = [] if after is None else [after]

    def body(*refs):
        ins = refs[:T]
        sems = refs[T + len(ordered):T + len(ordered) + 2 * G]
        token = refs[-1]
        x, y, c, other_chips = _position()
        mine = 2 * x + y
        t0 = 0
        for gi, group in enumerate(groups):
            for t in range(len(group)):
                own = _slab_half(ins[t0 + t], mine, c)
                for k, (cx, cy) in enumerate(other_chips):
                    _start_in_chunks(_remote_copier(sems[2 * gi].at[3 * t + k], sems[2 * gi + 1].at[3 * t + k], (cx, cy, c)),
                                     own, own, waited_here=False)
            t0 += len(group)
        token[...] = jnp.zeros_like(token)

    sem_shapes = []
    for group in groups:
        sem_shapes += [pltpu.SemaphoreType.DMA((3 * len(group),))] * 2
    res = pl.pallas_call(
        body, name=name, in_specs=[HBM_SPEC] * T + [ANY] * len(ordered),
        out_specs=[SEM_SPEC] * (2 * G) + [HBM_SPEC] * T + [VMEM_WHOLE],
        out_shape=sem_shapes + [pltpu.HBM(s.shape, s.dtype) for s in flat] + [jax.ShapeDtypeStruct((8, LANES), jnp.float32)],
        input_output_aliases={t: 2 * G + t for t in range(T)},
        compiler_params=pltpu.CompilerParams(has_side_effects=SIDE_EFFECT),
    )(*[pltpu.with_memory_space_constraint(s, pltpu.HBM) for s in flat], *ordered)
    flights, t0 = [], 2 * G
    for gi, group in enumerate(groups):
        flights.append((res[2 * gi], res[2 * gi + 1], res[t0:t0 + len(group)]))
        t0 += len(group)
    return flights, res[-1]


def _gather_wait(name, flight, after):
    send_sems, recv_sems, slabs = flight
    T = len(slabs)

    def body(*refs):
        ins = refs[:T]
        send, recv = refs[T:T + 2]
        x, y, c, other_chips = _position()
        mine = 2 * x + y
        for t in range(T):
            own = _slab_half(ins[t], mine, c)
            for k, (cx, cy) in enumerate(other_chips):
                landed = _slab_half(ins[t], 2 * cx + cy, c)
                cp = _remote_copier(send.at[3 * t + k], recv.at[3 * t + k], (cx, cy, c))(own, landed)
                cp.wait_send()
                cp.wait_recv()

    return pl.pallas_call(
        body, name=name, in_specs=[HBM_SPEC] * T + [SEM_SPEC, SEM_SPEC, ANY], out_specs=[HBM_SPEC] * T,
        out_shape=[pltpu.HBM(s.shape, s.dtype) for s in slabs], input_output_aliases={t: t for t in range(T)},
        compiler_params=pltpu.CompilerParams(has_side_effects=SIDE_EFFECT),
    )(*slabs, send_sems, recv_sems, after)


def _share_with_sibling(name, slabs):
    T = len(slabs)

    def body(*refs):
        outs = refs[T:2 * T]
        send_sems, recv_sems = refs[2 * T:]
        x, y, c, other_chips = _position()
        sibling = (x, y, 1 - c)
        started = []
        for t in range(T):
            for k, (cx, cy) in enumerate(other_chips):
                landed = _slab_half(outs[t], 2 * cx + cy, c)
                started.append(_start_in_chunks(_remote_copier(send_sems.at[t, k], recv_sems.at[t, k], sibling), landed, landed))
        for t in range(T):
            for k, (cx, cy) in enumerate(other_chips):
                arriving = _slab_half(outs[t], 2 * cx + cy, 1 - c)
                _remote_copier(send_sems.at[t, k], recv_sems.at[t, k], sibling)(arriving, arriving).wait_recv()
        for cp in started:
            cp.wait_send()

    return pl.pallas_call(
        body, name=name, in_specs=[ANY] * T, out_specs=[ANY] * T,
        out_shape=[jax.ShapeDtypeStruct(s.shape, s.dtype) for s in slabs],
        input_output_aliases={t: t for t in range(T)},
        scratch_shapes=[pltpu.SemaphoreType.DMA((T, 3)), pltpu.SemaphoreType.DMA((T, 3))],
    )(*slabs)


def _swap_halves(name, grads):
    T = len(grads)

    def body(*refs):
        ins, got = refs[:T], refs[T:2 * T]
        send_sems, recv_sems = refs[2 * T:]
        x, y, c, _ = _position()
        remote = []
        for t in range(T):
            rows = ins[t].shape[1]
            remote.append(_start_in_chunks(
                _remote_copier(send_sems.at[t], recv_sems.at[t], (x, y, 1 - c)), ins[t].at[:, _half(rows, 1 - c)], got[t]))
        for cp in remote:
            cp.wait()

    return pl.pallas_call(
        body, name=name, in_specs=[ANY] * T, out_specs=[ANY] * T,
        out_shape=[jax.ShapeDtypeStruct((g.shape[0], g.shape[1] // 2, g.shape[2]), g.dtype) for g in grads],
        scratch_shapes=[pltpu.SemaphoreType.DMA((T,)), pltpu.SemaphoreType.DMA((T,))],
    )(*grads)


def _swap_start(name, grads):
    T = len(grads)
    lands = [lax.empty((g.shape[0], g.shape[1] // 2, g.shape[2]), g.dtype) for g in grads]

    def body(*refs):
        ins, got = refs[:T], refs[T:2 * T]
        send_sems, recv_sems = refs[2 * T:2 * T + 2]
        token = refs[-1]
        x, y, c, _ = _position()
        for t in range(T):
            _start_in_chunks(_remote_copier(send_sems.at[t], recv_sems.at[t], (x, y, 1 - c)),
                             ins[t].at[:, _half(ins[t].shape[1], 1 - c)], got[t], waited_here=False)
        token[...] = jnp.zeros_like(token)

    arrays = list(grads) + lands
    res = pl.pallas_call(
        body, name=name, in_specs=[HBM_SPEC] * (2 * T),
        out_specs=[SEM_SPEC, SEM_SPEC] + [HBM_SPEC] * (2 * T) + [VMEM_WHOLE],
        out_shape=[pltpu.SemaphoreType.DMA((T,)), pltpu.SemaphoreType.DMA((T,))]
        + [pltpu.HBM(a.shape, a.dtype) for a in arrays] + [jax.ShapeDtypeStruct((8, LANES), jnp.float32)],
        input_output_aliases={i: 2 + i for i in range(2 * T)},
        compiler_params=pltpu.CompilerParams(has_side_effects=SIDE_EFFECT),
    )(*[pltpu.with_memory_space_constraint(a, pltpu.HBM) for a in arrays])
    return res[0], res[1], res[2:2 + T], res[2 + T:2 + 2 * T], res[-1]


def _swap_wait(name, flight, after):
    send_sems, recv_sems, grads, lands, _ = flight
    T = len(grads)

    def body(*refs):
        ins, got = refs[:T], refs[T:2 * T]
        send, recv = refs[2 * T:2 * T + 2]
        x, y, c, _ = _position()
        for t in range(T):
            cp = _remote_copier(send.at[t], recv.at[t], (x, y, 1 - c))(ins[t].at[:, _half(ins[t].shape[1], 1 - c)], got[t])
            cp.wait_send()
            cp.wait_recv()

    arrays = list(grads) + list(lands)
    res = pl.pallas_call(
        body, name=name, in_specs=[HBM_SPEC] * (2 * T) + [SEM_SPEC, SEM_SPEC, ANY], out_specs=[HBM_SPEC] * (2 * T),
        out_shape=[pltpu.HBM(a.shape, a.dtype) for a in arrays], input_output_aliases={i: i for i in range(2 * T)},
        compiler_params=pltpu.CompilerParams(has_side_effects=SIDE_EFFECT),
    )(*arrays, send_sems, recv_sems, after)
    return res[:T], res[T:]


def _scatter_start(name, chip_sums):
    T = len(chip_sums)
    lands = [lax.empty((3,) + s.shape[1:], s.dtype) for s in chip_sums]

    def body(*refs):
        srcs, dsts = refs[:T], refs[T:2 * T]
        send_sems, recv_sems = refs[2 * T:2 * T + 2]
        token = refs[-1]
        x, y, c, other_chips = _position()
        for t in range(T):
            for k, (cx, cy) in enumerate(other_chips):
                _start_in_chunks(_remote_copier(send_sems.at[3 * t + k], recv_sems.at[3 * t + k], (cx, cy, c)),
                                 srcs[t].at[2 * cx + cy], dsts[t].at[k], waited_here=False)
        token[...] = jnp.zeros_like(token)

    hbm = [pltpu.HBM(a.shape, a.dtype) for a in list(chip_sums) + lands]
    res = pl.pallas_call(
        body, name=name, in_specs=[HBM_SPEC] * (2 * T),
        out_specs=[SEM_SPEC, SEM_SPEC] + [HBM_SPEC] * (2 * T) + [VMEM_WHOLE],
        out_shape=[pltpu.SemaphoreType.DMA((3 * T,)), pltpu.SemaphoreType.DMA((3 * T,))] + hbm
        + [jax.ShapeDtypeStruct((8, LANES), jnp.float32)],
        input_output_aliases={i: 2 + i for i in range(2 * T)},
        compiler_params=pltpu.CompilerParams(has_side_effects=SIDE_EFFECT),
    )(*[pltpu.with_memory_space_constraint(a, pltpu.HBM) for a in list(chip_sums) + lands])
    return res[0], res[1], res[2:2 + T], res[2 + T:2 + 2 * T], res[-1]


def _scatter_wait(name, send_sems, recv_sems, chip_sums, lands, after):
    T = len(chip_sums)

    def body(*refs):
        srcs, dsts = refs[:T], refs[T:2 * T]
        send, recv = refs[2 * T:2 * T + 2]
        x, y, c, other_chips = _position()
        for t in range(T):
            for k, (cx, cy) in enumerate(other_chips):
                cp = _remote_copier(send.at[3 * t + k], recv.at[3 * t + k], (cx, cy, c))(srcs[t].at[2 * cx + cy], dsts[t].at[k])
                cp.wait_send()
                cp.wait_recv()

    arrays = list(chip_sums) + list(lands)
    res = pl.pallas_call(
        body, name=name, in_specs=[HBM_SPEC] * (2 * T) + [SEM_SPEC, SEM_SPEC, ANY],
        out_specs=[HBM_SPEC] * (2 * T), out_shape=[pltpu.HBM(a.shape, a.dtype) for a in arrays],
        input_output_aliases={i: i for i in range(2 * T)},
        compiler_params=pltpu.CompilerParams(has_side_effects=SIDE_EFFECT),
    )(*arrays, send_sems, recv_sems, after)
    return res[:T], res[T:]


def _as_shards(partial):
    return [p.reshape((N_CHIPS, -1, p.shape[-1])) for p in partial]


def _reduce_begin(tag, names, partial, pos):
    parts = _as_shards(partial)
    received = _swap_halves("swap_grad_halves_" + tag, parts)
    return _reduce_scatter_start(tag, names, parts, received, pos)


def _reduce_scatter_start(tag, names, parts, received, pos):
    chip_sums = [_chip_sum("chip_sum_" + n, p, r, WIRE_DTYPE, pos) for n, p, r in zip(names, parts, received)]
    return _scatter_start("scatter_start_" + tag, chip_sums)


def _reduce_end(tag, names, flight, after, pos):
    send_sems, recv_sems, chip_sums, lands, _ = flight
    chip_sums, others = _scatter_wait("scatter_wait_" + tag, send_sems, recv_sems, chip_sums, lands, after)
    return {n: _owner_sum("owner_sum_" + n, cs, ot, pos) for n, cs, ot in zip(names, chip_sums, others)}


def _join_halves(halves):
    T = len(halves)

    def body(*refs):
        outs = refs[T:2 * T]
        send_sems, recv_sems = refs[2 * T:]
        x, y, c, _ = _position()

        def copier(t):
            return _remote_copier(send_sems.at[t], recv_sems.at[t], (x, y, 1 - c))

        remote = [_start_in_chunks(copier(t), outs[t].at[c], outs[t].at[c]) for t in range(T)]
        for t in range(T):
            copier(t)(outs[t].at[1 - c], outs[t].at[1 - c]).wait_recv()
        for cp in remote:
            cp.wait_send()

    return pl.pallas_call(
        body, name="join_grad_halves", in_specs=[ANY] * T, out_specs=[ANY] * T,
        out_shape=[jax.ShapeDtypeStruct(h.shape, h.dtype) for h in halves],
        input_output_aliases={t: t for t in range(T)},
        scratch_shapes=[pltpu.SemaphoreType.DMA((T,)), pltpu.SemaphoreType.DMA((T,))],
    )(*halves)


def _allreduce_small(part):
    rows = part.shape[0]

    def body(x_ref, out_ref, gath, send_sems, recv_sems):
        x, y, c, other_chips = _position()
        sibling = (x, y, 1 - c)

        def slot(px, py, pc):
            return gath.at[4 * px + 2 * py + pc]

        def copy(k, owner, to, src=None):
            return pltpu.make_async_remote_copy(
                src_ref=slot(*owner) if src is None else src, dst_ref=slot(*owner), send_sem=send_sems.at[k],
                recv_sem=recv_sems.at[k], device_id=to, device_id_type=MESH)

        gath[4 * x + 2 * y + c] = x_ref[...]
        first = [copy(0, (x, y, c), sibling, src=x_ref)]
        first += [copy(1 + k, (x, y, c), (cx, cy, c), src=x_ref) for k, (cx, cy) in enumerate(other_chips)]
        for cp in first:
            cp.start()
        passed = [copy(4 + k, (cx, cy, c), sibling) for k, (cx, cy) in enumerate(other_chips)]
        for k, (cx, cy) in enumerate(other_chips):
            copy(1 + k, (cx, cy, c), (x, y, c)).wait_recv()
            passed[k].start()
        copy(0, (x, y, 1 - c), (x, y, c)).wait_recv()
        for k, (cx, cy) in enumerate(other_chips):
            copy(4 + k, (cx, cy, 1 - c), (x, y, c)).wait_recv()
        for cp in first + passed:
            cp.wait_send()
        acc = gath[0]
        for d in range(1, N_DEV):
            acc = acc + gath[d]
        out_ref[...] = acc

    return pl.pallas_call(
        body, name="allreduce_small_grads", in_specs=[VMEM_WHOLE], out_specs=VMEM_WHOLE,
        out_shape=jax.ShapeDtypeStruct((rows, LANES), jnp.float32),
        scratch_shapes=[pltpu.VMEM((N_DEV, rows, LANES), jnp.float32), pltpu.SemaphoreType.DMA((7,)),
                        pltpu.SemaphoreType.DMA((7,))],
        compiler_params=pltpu.CompilerParams(vmem_limit_bytes=VMEM_LIMIT_BYTES),
    )(part)


BIG = ("w_in", "pool_w", "w_out", "w_q", "w_k", "w_v", "w_o", "w_gate", "w_up", "w_down")
SMALL = ("norm_mix_g", "pool_scale", "sgu_norm_g", "w_spatial", "b_spatial", "norm_xattn_g", "norm_mem_g",
         "norm_ffn_g", "final_norm_g")
GATHER_ORDER = (("w_in", "pool_w"), ("w_out",), ("w_q", "w_k", "w_v", "w_o"), ("w_gate", "w_up"), ("w_down",))
WEIGHTS = ("norm_mix_g", "w_in", "pool_w", "pool_scale", "sgu_norm_g", "w_spatial", "b_spatial", "w_out",
           "norm_xattn_g", "norm_mem_g", "w_q", "w_k", "w_v", "w_o", "norm_ffn_g", "w_gate", "w_up", "w_down",
           "final_norm_g")


def _as2d(a):
    return a.reshape(-1, a.shape[-1])


def _as_lanes(a):
    return a.reshape(-1, LANES)


def kernel(x, mem, norm_mix_g, w_in, pool_w, pool_scale, sgu_norm_g, w_spatial, b_spatial, w_out, norm_xattn_g, norm_mem_g, w_q, w_k, w_v, w_o, norm_ffn_g, w_gate, w_up, w_down, final_norm_g, loss_target, m_norm_mix_g, m_w_in, m_pool_w, m_pool_scale, m_sgu_norm_g, m_w_spatial, m_b_spatial, m_w_out, m_norm_xattn_g, m_norm_mem_g, m_w_q, m_w_k, m_w_v, m_w_o, m_norm_ffn_g, m_w_gate, m_w_up, m_w_down, m_final_norm_g, v_norm_mix_g, v_w_in, v_pool_w, v_pool_scale, v_sgu_norm_g, v_w_spatial, v_b_spatial, v_w_out, v_norm_xattn_g, v_norm_mem_g, v_w_q, v_w_k, v_w_v, v_w_o, v_norm_ffn_g, v_w_gate, v_w_up, v_w_down, v_final_norm_g):
    args = dict(locals())
    weights = {n: args[n] for n in WEIGHTS}
    moments_m = {n: args["m_" + n] for n in WEIGHTS}
    moments_v = {n: args["v_" + n] for n in WEIGHTS}
    low = MXU_DTYPE

    xs = x[0]
    mems = mem[0]
    target = loss_target[0]
    D = xs.shape[1]
    G, PG = pool_w.shape[1], pool_w.shape[3]

    pos = jnp.stack([2 * lax.axis_index("x") + lax.axis_index("y"), lax.axis_index("c")]).astype(jnp.int32)
    first = {n: _cast_into_slab("cast_" + n, _as2d(weights[n]), WIRE_DTYPE, pos) for n in GATHER_ORDER[0]}
    flights, started = _gather_start("gather_start_first", [[first[n] for n in GATHER_ORDER[0]]])
    rest = {n: _cast_into_slab("cast_" + n, _as2d(weights[n]), WIRE_DTYPE, pos, after=started)
            for group in GATHER_ORDER[1:] for n in group}
    more, started = _gather_start("gather_start_rest", [[rest[n] for n in group] for group in GATHER_ORDER[1:]], after=started)
    flights = flights + more

    def arrive(gi, after):
        landed = _gather_wait("gather_wait_%d" % gi, flights[gi], after)
        return dict(zip(GATHER_ORDER[gi], _share_with_sibling("gather_share_%d" % gi, landed)))

    def rows(a):
        return a.reshape(-1, a.shape[-1])

    g_mix, g_xattn, g_mem, g_ffn = norm_mix_g, norm_xattn_g, norm_mem_g, norm_ffn_g
    g_final = final_norm_g.reshape(1, D)
    ws = w_spatial[0]
    b_full = jnp.broadcast_to(b_spatial[0][:, :, None], ws.shape[:2] + (sgu_norm_g.shape[1] // ws.shape[0],))

    h1 = _rmsnorm("norm_mix", xs, g_mix, low, after=started)
    mn = _rmsnorm("norm_mem", mems, g_mem, low)
    got = arrive(0, mn)
    wg_in = got["w_in"]
    pw = got["pool_w"].reshape(N_CHIPS, G, PG // N_CHIPS, PG).transpose(1, 0, 2, 3).reshape(G, PG, PG)
    proj = _matmul_nn_cols("proj_in", h1, wg_in, jnp.float32)
    z = _mixer_fwd(proj, pw, pool_scale, sgu_norm_g, ws, b_full)
    full = {"w_out": rows(arrive(1, z)["w_out"])}
    x1, h2 = _proj_residual_norm("mix_out", z, full["w_out"], xs, g_xattn)
    full.update({n: rows(a) for n, a in arrive(2, h2).items()})
    q = _matmul_nn("xattn_q", h2, full["w_q"], low)
    k = _matmul_nn("xattn_k", mn, full["w_k"], low)
    vv = _matmul_nn("xattn_v", mn, full["w_v"], low)
    o = _attn_fwd(q, k, vv)
    x2, h3 = _proj_residual_norm("xattn_out", o, full["w_o"], x1, g_ffn)
    got = arrive(3, h3)
    wg_gate, wg_up = got["w_gate"], got["w_up"]
    gate, up, act = _ffn_gate_up(h3, wg_gate, wg_up)
    full["w_down"] = rows(arrive(4, act)["w_down"])
    x3 = _ffn_down(act, full["w_down"], x2)
    dx3, dx3_low, loss_part, d_final_g = _final_norm_loss(x3, target, g_final)

    dgate, dup = _ffn_down_bwd(dx3_low, full["w_down"], gate, up, N_CHIPS)
    (d_w_down,) = _matmul_tn("grad_w_down", act, [dx3_low], WIRE_DTYPE)
    (d_w_gate,) = _matmul_tn_cols("grad_w_gate", h3, [dgate], N_CHIPS, WIRE_DTYPE)
    (d_w_up,) = _matmul_tn_cols("grad_w_up", h3, [dup], N_CHIPS, WIRE_DTYPE)
    ffn_names = ("w_gate", "w_up", "w_down")
    ffn_swap = _swap_start("swap_start_ffn", _as_shards([d_w_gate, d_w_up, d_w_down]))
    dh3 = _matmul_nt_cols("ffn_dh", [dgate, dup], [wg_gate, wg_up], jnp.float32, after=ffn_swap[-1])
    ffn_flight = _reduce_scatter_start("ffn", ffn_names, *_swap_wait("swap_wait_ffn", ffn_swap, dh3), pos)
    dx2, dx2_low, d_ffn_g = _rmsnorm_bwd("norm_ffn_bwd", dh3, x2, g_ffn, dx3, low)

    do = _matmul_nt("xattn_do", [dx2_low], [full["w_o"]], low, after=ffn_flight[-1])
    (d_w_o,) = _matmul_tn("grad_w_o", o, [dx2_low], WIRE_DTYPE)
    dq, dk, dvv = _attn_bwd(q, k, vv, do)
    (d_w_q,) = _matmul_tn("grad_w_q", h2, [dq], WIRE_DTYPE)
    dh2 = _matmul_nt("xattn_dh", [dq], [full["w_q"]], jnp.float32)
    d_w_k, d_w_v = _matmul_tn("grad_w_kv", mn, [dk, dvv], WIRE_DTYPE)
    dmn = _matmul_nt("xattn_dmem", [dk, dvv], [full["w_k"], full["w_v"]], jnp.float32)
    (d_mem_g,) = _rmsnorm_bwd("norm_mem_bwd", dmn, mems, g_mem, None, None)
    dx1, dx1_low, d_xattn_g = _rmsnorm_bwd("norm_xattn_bwd", dh2, x1, g_xattn, dx2, low)

    (d_w_out,) = _matmul_tn("grad_w_out", z, [dx1_low], WIRE_DTYPE)
    attn_names = ("w_q", "w_k", "w_v", "w_o", "w_out")
    attn_swap = _swap_start("swap_start_attn", _as_shards([d_w_q, d_w_k, d_w_v, d_w_o, d_w_out]))
    dz = _matmul_nt("mix_dz", [dx1_low], [full["w_out"]], jnp.float32, after=attn_swap[-1])
    attn_flight = _reduce_scatter_start("attn", attn_names, *_swap_wait("swap_wait_attn", attn_swap, dz), pos)
    dproj, d_pw, d_ps, d_gs, d_ws, d_b = _mixer_bwd(proj, dz, pw, pool_scale, sgu_norm_g, ws, b_full)
    (d_w_in,) = _matmul_tn_cols("grad_w_in", h1, [dproj], N_CHIPS, WIRE_DTYPE, tka=1024, after=attn_flight[-1])
    d_pool_w = d_pw.reshape(G, N_CHIPS, PG // N_CHIPS, PG).transpose(1, 0, 2, 3).reshape(N_CHIPS, PG, PG).astype(WIRE_DTYPE)
    mix_names = ("w_in", "pool_w")
    mix_flight = _reduce_begin("mix", mix_names, [d_w_in, d_pool_w], pos)
    dh1 = _matmul_nt_cols("mix_dh", [dproj], [wg_in], jnp.float32, tn=1024, after=mix_flight[-1])
    grad_x, d_mix_g = _rmsnorm_bwd("norm_mix_bwd", dh1, xs, g_mix, dx1, None)

    reduced = _reduce_end("ffn", ffn_names, ffn_flight, grad_x, pos)
    reduced.update(_reduce_end("attn", attn_names, attn_flight, grad_x, pos))
    reduced.update(_reduce_end("mix", mix_names, mix_flight, grad_x, pos))
    grads = {n: g.reshape(-1, g.shape[-1]) for n, g in zip(BIG, _join_halves([reduced[n] for n in BIG]))}

    small_parts = {
        "norm_mix_g": d_mix_g, "pool_scale": d_ps, "sgu_norm_g": d_gs, "w_spatial": d_ws, "b_spatial": d_b,
        "norm_xattn_g": d_xattn_g, "norm_mem_g": d_mem_g, "norm_ffn_g": d_ffn_g, "final_norm_g": d_final_g,
    }
    row_counts = [small_parts[n].size // LANES for n in SMALL]
    packed = jnp.concatenate([_as_lanes(small_parts[n]) for n in SMALL], axis=0)
    small_sum = _allreduce_small(packed)

    out_grad, out_delta, out_m, out_v = {}, {}, {}, {}
    for n in BIG:
        shape = weights[n].shape
        g2d, delta, m_new, v_new = _adamw("adamw_" + n, _as2d(weights[n]), grads[n], _as2d(moments_m[n]), _as2d(moments_v[n]))
        out_grad[n], out_delta[n] = g2d.reshape(shape), delta.reshape(shape)
        out_m[n], out_v[n] = m_new.reshape(shape), v_new.reshape(shape)
    small_out = _adamw_small(
        small_sum, [_as_lanes(weights[n]) for n in SMALL], [_as_lanes(moments_m[n]) for n in SMALL],
        [_as_lanes(moments_v[n]) for n in SMALL], row_counts)
    for t, n in enumerate(SMALL):
        shape = weights[n].shape
        out_grad[n], out_delta[n], out_m[n], out_v[n] = (a.reshape(shape) for a in small_out[4 * t:4 * t + 4])

    loss = lax.psum(loss_part[0, 0], ("x", "y", "c"))
    return (loss, grad_x[None], *[out_grad[n] for n in WEIGHTS], *[out_delta[n] for n in WEIGHTS],
            *[out_m[n] for n in WEIGHTS], *[out_v[n] for n in WEIGHTS])
```

```python
import math

import jax
import jax.numpy as jnp
from jax import lax
from jax.experimental import pallas as pl
from jax.experimental.pallas import tpu as pltpu

MXU_DTYPE = jnp.bfloat16
WIRE_DTYPE = jnp.bfloat16

EPS = 1e-6
CHUNK = 64
POOL_WINDOWS = (2, 4, 8, 16)
POOL_HALO = 16
N_XATTN_HEADS = 4
ADAM_LR = 0.001
ADAM_B1 = 0.9
ADAM_B2 = 0.999
ADAM_EPS = 1e-08
ADAM_WD = 0.01
ADAM_STEP = 10

N_CHIPS = 4
N_DEV = 8
LANES = 128
VMEM_LIMIT_BYTES = 56 * 2 ** 20
COPY_CHUNK_BYTES = 512 * 2 ** 10
COPY_ROW_ALIGN = 16

MESH = pl.DeviceIdType.MESH
ANY = pl.BlockSpec(memory_space=pl.ANY)
VMEM_WHOLE = pl.BlockSpec(memory_space=pltpu.VMEM)
HBM_SPEC = pl.BlockSpec(memory_space=pltpu.HBM)
SEM_SPEC = pl.BlockSpec(memory_space=pltpu.SEMAPHORE)
SIDE_EFFECT = pltpu.SideEffectType.DATAFLOW_SIDE_EFFECTING

NN = (((1,), (0,)), ((), ()))
NT = (((1,), (1,)), ((), ()))
TN = (((0,), (0,)), ((), ()))


def _tile(n, target, mult):
    best = None
    for t in range(mult, min(n, target) + 1, mult):
        if n % t == 0:
            best = t
    return n if best is None else best


def _params(n_grid_axes):
    return pltpu.CompilerParams(dimension_semantics=("arbitrary",) * n_grid_axes, vmem_limit_bytes=VMEM_LIMIT_BYTES)


def _rsqrt_mean_sq(x):
    return lax.rsqrt(jnp.mean(x * x, axis=-1, keepdims=True) + EPS)


def _sum_all(x):
    return jnp.sum(jnp.sum(x, axis=1, keepdims=True), axis=0, keepdims=True)


def _sigmoid(x):
    return 0.5 * jnp.tanh(0.5 * x) + 0.5


def _store_epilogue(accs, ex_refs, out_refs, i, j):
    for a, o in zip(accs, out_refs):
        o[...] = a.astype(o.dtype)


def _fused_matmul(name, grid, operands, pairs, acc_shapes, extras, outs, epilogue):
    nk = grid[2]
    n_op, n_ex, n_out, n_acc = len(operands), len(extras), len(outs), len(acc_shapes)
    in_place = nk > 1 and epilogue is _store_epilogue and all(o[1] == jnp.float32 for o in outs)

    def body(*refs):
        op_refs = refs[:n_op]
        ex_refs = refs[n_op:n_op + n_ex]
        out_refs = refs[n_op + n_ex:n_op + n_ex + n_out]
        acc_refs = out_refs if in_place else refs[n_op + n_ex + n_out:]
        i, j, k = pl.program_id(0), pl.program_id(1), pl.program_id(2)
        loaded = {}

        def operand(n):
            if n not in loaded:
                loaded[n] = op_refs[n][...].astype(MXU_DTYPE)
            return loaded[n]

        def product(ia, ib, dims):
            return lax.dot_general(operand(ia), operand(ib), dims, preferred_element_type=jnp.float32)

        def products():
            parts = [None] * n_acc
            for ia, ib, iacc, dims in pairs:
                d = product(ia, ib, dims)
                parts[iacc] = d if parts[iacc] is None else parts[iacc] + d
            return parts

        if nk == 1:
            epilogue(products(), ex_refs, out_refs, i, j)
            return

        @pl.when(k == 0)
        def _():
            for a in range(n_acc):
                acc_refs[a][...] = jnp.zeros_like(acc_refs[a])

        for a, part in enumerate(products()):
            acc_refs[a][...] += part

        if not in_place:
            @pl.when(k == nk - 1)
            def _():
                epilogue([acc_refs[a][...] for a in range(n_acc)], ex_refs, out_refs, i, j)

    arrays = [o[0] for o in operands] + [e[0] for e in extras]
    in_specs = [pl.BlockSpec(o[1], o[2]) for o in operands]
    in_specs += [ANY if e[1] is None else pl.BlockSpec(e[1], e[2]) for e in extras]
    return pl.pallas_call(
        body,
        name=name,
        grid=grid,
        in_specs=in_specs,
        out_specs=[pl.BlockSpec(o[2], o[3]) for o in outs],
        out_shape=[jax.ShapeDtypeStruct(o[0], o[1]) for o in outs],
        scratch_shapes=[pltpu.VMEM(s, jnp.float32) for s in acc_shapes] if nk > 1 and not in_place else [],
        compiler_params=_params(3),
    )(*arrays)


def _matmul_nn(name, a, b, out_dtype, tm=1024, tn=512):
    M, K = a.shape
    N = b.shape[1]
    tm, tn = _tile(M, tm, 16), _tile(N, tn, LANES)
    return _fused_matmul(
        name, (M // tm, N // tn, 1),
        [(a, (tm, K), lambda i, j, k: (i, 0)), (b, (K, tn), lambda i, j, k: (0, j))],
        [(0, 1, 0, NN)], [(tm, tn)], [],
        [((M, N), out_dtype, (tm, tn), lambda i, j, k: (i, j))], _store_epilogue)[0]


def _matmul_nn_cols(name, a, b3, out_dtype, tm=1024):
    M, K = a.shape
    J, _, Nc = b3.shape
    tm = _tile(M, tm, 16)
    return _fused_matmul(
        name, (M // tm, J, 1),
        [(a, (tm, K), lambda i, j, k: (i, 0)), (b3, (None, K, Nc), lambda i, j, k: (j, 0, 0))],
        [(0, 1, 0, NN)], [(tm, Nc)], [],
        [((M, J * Nc), out_dtype, (tm, Nc), lambda i, j, k: (i, j))], _store_epilogue)[0]


def _order_after(after):
    return [] if after is None else [(after, None, None)]


def _matmul_nt(name, a_list, b_list, out_dtype, tm=1024, tn=512, after=None):
    M, K = a_list[0].shape
    N = b_list[0].shape[0]
    tm, tn = _tile(M, tm, 16), _tile(N, tn, LANES)
    n = len(a_list)
    operands = [(a, (tm, K), lambda i, j, k: (i, 0)) for a in a_list]
    operands += [(b, (tn, K), lambda i, j, k: (j, 0)) for b in b_list]
    return _fused_matmul(
        name, (M // tm, N // tn, 1), operands,
        [(p, n + p, 0, NT) for p in range(n)], [(tm, tn)], _order_after(after),
        [((M, N), out_dtype, (tm, tn), lambda i, j, k: (i, j))], _store_epilogue)[0]


def _matmul_nt_cols(name, a_list, b3_list, out_dtype, tm=1024, tn=512, after=None):
    M = a_list[0].shape[0]
    J, N, Kc = b3_list[0].shape
    tm, tn = _tile(M, tm, 16), _tile(N, tn, LANES)
    n = len(a_list)
    operands = [(a, (tm, Kc), lambda i, j, k: (i, k)) for a in a_list]
    operands += [(b, (None, tn, Kc), lambda i, j, k: (k, j, 0)) for b in b3_list]
    return _fused_matmul(
        name, (M // tm, N // tn, J), operands,
        [(p, n + p, 0, NT) for p in range(n)], [(tm, tn)], _order_after(after),
        [((M, N), out_dtype, (tm, tn), lambda i, j, k: (i, j))], _store_epilogue)[0]


def _matmul_tn(name, a, b_list, out_dtype, tka=512, tn=1024, tkm=4096):
    M, Ka = a.shape
    N = b_list[0].shape[1]
    tka, tn, tkm = _tile(Ka, tka, LANES), _tile(N, tn, LANES), _tile(M, tkm, 16)
    n = len(b_list)
    operands = [(a, (tkm, tka), lambda i, j, k: (k, i))]
    operands += [(b, (tkm, tn), lambda i, j, k: (k, j)) for b in b_list]
    return _fused_matmul(
        name, (Ka // tka, N // tn, M // tkm), operands,
        [(0, 1 + p, p, TN) for p in range(n)], [(tka, tn)] * n, [],
        [((Ka, N), out_dtype, (tka, tn), lambda i, j, k: (i, j))] * n, _store_epilogue)


def _matmul_tn_cols(name, a, b_list, n_shards, out_dtype, tka=512, tkm=4096, after=None):
    M, Ka = a.shape
    Nc = b_list[0].shape[1] // n_shards
    tka, tkm = _tile(Ka, tka, LANES), _tile(M, tkm, 16)
    n = len(b_list)
    operands = [(a, (tkm, tka), lambda i, j, k: (k, i))]
    operands += [(b, (tkm, Nc), lambda i, j, k: (k, j)) for b in b_list]
    return _fused_matmul(
        name, (Ka // tka, n_shards, M // tkm), operands,
        [(0, 1 + p, p, TN) for p in range(n)], [(tka, Nc)] * n, _order_after(after),
        [((n_shards, Ka, Nc), out_dtype, (None, tka, Nc), lambda i, j, k: (j, i, 0))] * n, _store_epilogue)


def _rmsnorm(name, x, g_row, out_dtype, tm=512, after=None):
    S, D = x.shape
    tm = _tile(S, tm, 16)

    def body(x_ref, g_ref, *rest):
        o_ref = rest[-1]
        xv = x_ref[...]
        o_ref[...] = (xv * _rsqrt_mean_sq(xv) * g_ref[...]).astype(o_ref.dtype)

    ordered = [] if after is None else [after]
    return pl.pallas_call(
        body, name=name, grid=(S // tm,),
        in_specs=[pl.BlockSpec((tm, D), lambda i: (i, 0)), pl.BlockSpec((1, D), lambda i: (0, 0))] + [ANY] * len(ordered),
        out_specs=pl.BlockSpec((tm, D), lambda i: (i, 0)),
        out_shape=jax.ShapeDtypeStruct((S, D), out_dtype),
        compiler_params=_params(1),
    )(x, g_row, *ordered)


def _rmsnorm_bwd_values(dh, xin, g):
    r = _rsqrt_mean_sq(xin)
    xhat = xin * r
    dhg = dh * g
    dx = r * (dhg - xhat * jnp.mean(dhg * xhat, axis=-1, keepdims=True))
    return dx, jnp.sum(dh * xhat, axis=0, keepdims=True)


def _rmsnorm_bwd(name, dh, xin, g_row, dres, low_dtype, tm=256):
    S, D = xin.shape
    tm = _tile(S, tm, 16)
    has_dx = dres is not None
    has_low = has_dx and low_dtype is not None

    def body(*refs):
        dh_ref, x_ref, g_ref = refs[:3]
        rest = refs[3:]
        dx, dg = _rmsnorm_bwd_values(dh_ref[...], x_ref[...], g_ref[...])
        if has_dx:
            dres_ref, rest = rest[0], rest[1:]
            dx = dx + dres_ref[...]
            rest[0][...] = dx
            if has_low:
                rest[1][...] = dx.astype(low_dtype)
        dg_ref = rest[-1]

        @pl.when(pl.program_id(0) == 0)
        def _():
            dg_ref[...] = jnp.zeros_like(dg_ref)

        dg_ref[...] += dg

    row = pl.BlockSpec((tm, D), lambda i: (i, 0))
    vec = pl.BlockSpec((1, D), lambda i: (0, 0))
    ins, in_specs = [dh, xin, g_row], [row, row, vec]
    out_shape, out_specs = [], []
    if has_dx:
        ins.append(dres)
        in_specs.append(row)
        out_shape.append(jax.ShapeDtypeStruct((S, D), jnp.float32))
        out_specs.append(row)
        if has_low:
            out_shape.append(jax.ShapeDtypeStruct((S, D), low_dtype))
            out_specs.append(row)
    out_shape.append(jax.ShapeDtypeStruct((1, D), jnp.float32))
    out_specs.append(vec)
    return pl.pallas_call(
        body, name=name, grid=(S // tm,), in_specs=in_specs, out_specs=out_specs, out_shape=out_shape,
        compiler_params=_params(1),
    )(*ins)


def _chunk_mask(blk):
    t = lax.broadcasted_iota(jnp.int32, (blk, blk), 0)
    s = lax.broadcasted_iota(jnp.int32, (blk, blk), 1)
    return (s // CHUNK) <= (t // CHUNK)


def _pool_inputs(a, halo, row0, tm, g, pg):
    cols = slice(g * pg, (g + 1) * pg)
    w = POOL_WINDOWS[g]
    s = jnp.concatenate([halo[:, cols], a[:, cols]], axis=0)
    span = 1
    while span < w:
        s = s + pltpu.roll(s, span, axis=0)
        span *= 2
    t = lax.broadcasted_iota(jnp.int32, (tm, 1), 0) + row0
    cnt = jnp.minimum(t + 1, w).astype(jnp.float32)
    return s[POOL_HALO:] / cnt - a[:, cols]


def _mixer_specs(S, tm, DP, DS):
    per_halo = tm // POOL_HALO
    n_halo = S // POOL_HALO
    a_spec = pl.BlockSpec((tm, DP), lambda i: (i, 0))
    prev_spec = pl.BlockSpec((POOL_HALO, DP), lambda i: (jnp.maximum(i * per_halo - 1, 0), 0))
    next_spec = pl.BlockSpec((POOL_HALO, DP), lambda i: (jnp.minimum((i + 1) * per_halo, n_halo - 1), 0))
    u_spec = pl.BlockSpec((tm, DS), lambda i: (i, 1))
    v_spec = pl.BlockSpec((tm, DS), lambda i: (i, 2))
    return a_spec, prev_spec, next_spec, u_spec, v_spec


def _whole(arr):
    nd = arr.ndim
    return pl.BlockSpec(arr.shape, lambda i: (0,) * nd)


def _mixer_fwd(proj, pw, ps_row, gs_row, ws, b_full, tm=512):
    S = proj.shape[0]
    G, PG, _ = pw.shape
    DP = G * PG
    H, BLK, _ = ws.shape
    DS = gs_row.shape[1]
    HD = DS // H
    assert DP == DS and proj.shape[1] == DP + 2 * DS
    tm = _tile(S, tm, BLK)

    def body(a_ref, prev_ref, u_ref, v_ref, pw_ref, ps_ref, gs_ref, ws_ref, b_ref, z_ref):
        i = pl.program_id(0)
        a = a_ref[...]
        halo = jnp.where(i > 0, prev_ref[...], 0.0)
        for g in range(G):
            cols = slice(g * PG, (g + 1) * PG)
            p = _pool_inputs(a, halo, i * tm, tm, g, PG)
            y = jnp.dot(p.astype(MXU_DTYPE), pw_ref[g], preferred_element_type=jnp.float32) * ps_ref[:, cols]
            z_ref[:, cols] = y.astype(z_ref.dtype)
        v = v_ref[...]
        vn = (v * _rsqrt_mean_sq(v) * gs_ref[...]).astype(MXU_DTYPE)
        u = u_ref[...]
        mask = _chunk_mask(BLK)
        for h in range(H):
            wm = jnp.where(mask, ws_ref[h], 0.0).astype(MXU_DTYPE)
            bias = b_ref[h]
            cols = slice(h * HD, (h + 1) * HD)
            for n in range(tm // BLK):
                rows = slice(n * BLK, (n + 1) * BLK)
                mixed = jnp.dot(wm, vn[rows, cols], preferred_element_type=jnp.float32) + bias
                z_ref[rows, DP + h * HD:DP + (h + 1) * HD] = (u[rows, cols] * mixed).astype(z_ref.dtype)

    a_spec, prev_spec, _, u_spec, v_spec = _mixer_specs(S, tm, DP, DS)
    return pl.pallas_call(
        body, name="mixer_fwd", grid=(S // tm,),
        in_specs=[a_spec, prev_spec, u_spec, v_spec, _whole(pw), _whole(ps_row), _whole(gs_row), _whole(ws), _whole(b_full)],
        out_specs=pl.BlockSpec((tm, DP + DS), lambda i: (i, 0)),
        out_shape=jax.ShapeDtypeStruct((S, DP + DS), MXU_DTYPE),
        compiler_params=_params(1),
    )(proj, proj, proj, proj, pw, ps_row, gs_row, ws, b_full)


def _mixer_bwd(proj, dz, pw, ps_row, gs_row, ws, b_full, tm=256):
    S = proj.shape[0]
    G, PG, _ = pw.shape
    DP = G * PG
    H, BLK, _ = ws.shape
    DS = gs_row.shape[1]
    HD = DS // H
    tm = _tile(S, tm, BLK)
    nb = S // tm
    ext = tm + POOL_HALO

    def body(a_ref, prev_ref, u_ref, v_ref, dzp_ref, dzp_next_ref, dzs_ref, pw_ref, ps_ref, gs_ref, ws_ref, b_ref,
             dproj_ref, dpw_ref, dps_ref, dgs_ref, dws_ref, db_ref, dvn_ref):
        i = pl.program_id(0)

        @pl.when(i == 0)
        def _():
            for r in (dpw_ref, dps_ref, dgs_ref, dws_ref, db_ref):
                r[...] = jnp.zeros_like(r)

        a = a_ref[...]
        halo = jnp.where(i > 0, prev_ref[...], 0.0)
        dyp = dzp_ref[...]
        dyp_next = jnp.where(i < nb - 1, dzp_next_ref[...], 0.0)
        dq_ext = jnp.concatenate([dyp, dyp_next], axis=0) * ps_ref[...]
        t_ext = lax.broadcasted_iota(jnp.int32, (ext, 1), 0) + i * tm
        for g in range(G):
            cols = slice(g * PG, (g + 1) * PG)
            w = POOL_WINDOWS[g]
            pb = _pool_inputs(a, halo, i * tm, tm, g, PG).astype(MXU_DTYPE)
            y_pre = jnp.dot(pb, pw_ref[g], preferred_element_type=jnp.float32)
            dps_ref[:, cols] += jnp.sum(dyp[:, cols] * y_pre, axis=0, keepdims=True)
            dqb = dq_ext[:, cols].astype(MXU_DTYPE)
            dpw_ref[g] += lax.dot_general(pb, dqb[:tm], TN, preferred_element_type=jnp.float32)
            dp = lax.dot_general(dqb, pw_ref[g], NT, preferred_element_type=jnp.float32)
            f = dp / jnp.minimum(t_ext + 1, w).astype(jnp.float32)
            span = 1
            while span < w:
                f = f + pltpu.roll(f, ext - span, axis=0)
                span *= 2
            dproj_ref[:, cols] = (f[:tm] - dp[:tm]).astype(dproj_ref.dtype)

        v = v_ref[...]
        rv = _rsqrt_mean_sq(v)
        vhat = v * rv
        gs = gs_ref[...]
        vnb = (vhat * gs).astype(MXU_DTYPE)
        u = u_ref[...]
        dys = dzs_ref[...]
        mask = _chunk_mask(BLK)
        for h in range(H):
            wm = jnp.where(mask, ws_ref[h], 0.0).astype(MXU_DTYPE)
            bias = b_ref[h]
            cols = slice(h * HD, (h + 1) * HD)
            dm_sum = jnp.zeros((BLK, HD), jnp.float32)
            dws_sum = jnp.zeros((BLK, BLK), jnp.float32)
            for n in range(tm // BLK):
                rows = slice(n * BLK, (n + 1) * BLK)
                vblk = vnb[rows, cols]
                mixed = jnp.dot(wm, vblk, preferred_element_type=jnp.float32) + bias
                dy = dys[rows, cols]
                dproj_ref[rows, DP + h * HD:DP + (h + 1) * HD] = (dy * mixed).astype(dproj_ref.dtype)
                dm = dy * u[rows, cols]
                dm_sum = dm_sum + dm
                dmb = dm.astype(MXU_DTYPE)
                dws_sum = dws_sum + lax.dot_general(dmb, vblk, NT, preferred_element_type=jnp.float32)
                dvn_ref[rows, cols] = lax.dot_general(wm, dmb, TN, preferred_element_type=jnp.float32)
            dws_ref[h] += jnp.where(mask, dws_sum, 0.0)
            db_ref[h:h + 1, :] += jnp.sum(dm_sum.T, axis=0, keepdims=True)
        dvn = dvn_ref[...]
        dvg = dvn * gs
        dv = rv * (dvg - vhat * jnp.mean(dvg * vhat, axis=-1, keepdims=True))
        dgs_ref[...] += jnp.sum(dvn * vhat, axis=0, keepdims=True)
        dproj_ref[:, DP + DS:] = dv.astype(dproj_ref.dtype)

    a_spec, prev_spec, next_spec, u_spec, v_spec = _mixer_specs(S, tm, DP, DS)
    dzp_spec = pl.BlockSpec((tm, DP), lambda i: (i, 0))
    dzs_spec = pl.BlockSpec((tm, DS), lambda i: (i, 1))
    small = [pw, ps_row, gs_row, ws, b_full]
    out_shape = [
        jax.ShapeDtypeStruct((S, DP + 2 * DS), MXU_DTYPE),
        jax.ShapeDtypeStruct(pw.shape, jnp.float32),
        jax.ShapeDtypeStruct(ps_row.shape, jnp.float32),
        jax.ShapeDtypeStruct(gs_row.shape, jnp.float32),
        jax.ShapeDtypeStruct(ws.shape, jnp.float32),
        jax.ShapeDtypeStruct((H, BLK), jnp.float32),
    ]
    out_specs = [pl.BlockSpec((tm, DP + 2 * DS), lambda i: (i, 0))] + [_whole(s) for s in out_shape[1:]]
    return pl.pallas_call(
        body, name="mixer_bwd", grid=(nb,),
        in_specs=[a_spec, prev_spec, u_spec, v_spec, dzp_spec, next_spec, dzs_spec] + [_whole(s) for s in small],
        out_specs=out_specs, out_shape=out_shape,
        scratch_shapes=[pltpu.VMEM((tm, DS), jnp.float32)],
        compiler_params=_params(1),
    )(proj, proj, proj, proj, dz, dz, dz, *small)


def _softmax_rows(q, k, scale):
    s = lax.dot_general(q, k, NT, preferred_element_type=jnp.float32) * scale
    e = jnp.exp(s - jnp.max(s, axis=-1, keepdims=True))
    return e / jnp.sum(e, axis=-1, keepdims=True)


def _attn_fwd(q, k, v, tm=512):
    S, D = q.shape
    M = k.shape[0]
    HD = D // N_XATTN_HEADS
    scale = HD ** -0.5
    tm = _tile(S, tm, 16)

    def body(q_ref, k_ref, v_ref, o_ref):
        for h in range(N_XATTN_HEADS):
            cols = slice(h * HD, (h + 1) * HD)
            p = _softmax_rows(q_ref[:, cols], k_ref[:, cols], scale)
            o = jnp.dot(p.astype(MXU_DTYPE), v_ref[:, cols], preferred_element_type=jnp.float32)
            o_ref[:, cols] = o.astype(o_ref.dtype)

    row = pl.BlockSpec((tm, D), lambda i: (i, 0))
    mem = pl.BlockSpec((M, D), lambda i: (0, 0))
    return pl.pallas_call(
        body, name="attn_fwd", grid=(S // tm,), in_specs=[row, mem, mem], out_specs=row,
        out_shape=jax.ShapeDtypeStruct((S, D), MXU_DTYPE), compiler_params=_params(1),
    )(q, k, v)


def _attn_bwd(q, k, v, do, tm=512):
    S, D = q.shape
    M = k.shape[0]
    HD = D // N_XATTN_HEADS
    scale = HD ** -0.5
    tm = _tile(S, tm, 16)

    def body(q_ref, k_ref, v_ref, do_ref, dq_ref, dk_ref, dv_ref):
        @pl.when(pl.program_id(0) == 0)
        def _():
            dk_ref[...] = jnp.zeros_like(dk_ref)
            dv_ref[...] = jnp.zeros_like(dv_ref)

        for h in range(N_XATTN_HEADS):
            cols = slice(h * HD, (h + 1) * HD)
            qh, kh, vh, doh = q_ref[:, cols], k_ref[:, cols], v_ref[:, cols], do_ref[:, cols]
            p = _softmax_rows(qh, kh, scale)
            dp = lax.dot_general(doh, vh, NT, preferred_element_type=jnp.float32)
            dv_ref[:, cols] += lax.dot_general(p.astype(MXU_DTYPE), doh, TN, preferred_element_type=jnp.float32)
            ds = (p * (dp - jnp.sum(dp * p, axis=-1, keepdims=True)) * scale).astype(MXU_DTYPE)
            dq_ref[:, cols] = jnp.dot(ds, kh, preferred_element_type=jnp.float32).astype(dq_ref.dtype)
            dk_ref[:, cols] += lax.dot_general(ds, qh, TN, preferred_element_type=jnp.float32)

    row = pl.BlockSpec((tm, D), lambda i: (i, 0))
    mem = pl.BlockSpec((M, D), lambda i: (0, 0))
    return pl.pallas_call(
        body, name="attn_bwd", grid=(S // tm,), in_specs=[row, mem, mem, row], out_specs=[row, mem, mem],
        out_shape=[jax.ShapeDtypeStruct((S, D), MXU_DTYPE), jax.ShapeDtypeStruct((M, D), jnp.float32),
                   jax.ShapeDtypeStruct((M, D), jnp.float32)],
        compiler_params=_params(1),
    )(q, k, v, do)


def _proj_residual_norm(name, a, w2d, resid, g_row, tm=256):
    M, K = a.shape
    N = w2d.shape[1]
    tm = _tile(M, tm, 16)

    def epilogue(accs, ex_refs, out_refs, i, j):
        x_new = ex_refs[0][...] + accs[0]
        out_refs[0][...] = x_new
        out_refs[1][...] = (x_new * _rsqrt_mean_sq(x_new) * ex_refs[1][...]).astype(MXU_DTYPE)

    row = lambda i, j, k: (i, 0)
    return _fused_matmul(
        name, (M // tm, 1, 1),
        [(a, (tm, K), row), (w2d, (K, N), lambda i, j, k: (0, 0))],
        [(0, 1, 0, NN)], [(tm, N)],
        [(resid, (tm, N), row), (g_row, (1, N), lambda i, j, k: (0, 0))],
        [((M, N), jnp.float32, (tm, N), row), ((M, N), MXU_DTYPE, (tm, N), row)], epilogue)


def _ffn_gate_up(h, wg3, wu3, tm=512):
    M, K = h.shape
    J, _, Nc = wg3.shape
    tm = _tile(M, tm, 16)

    def epilogue(accs, ex_refs, out_refs, i, j):
        gate, up = accs
        sg = _sigmoid(gate)
        silu = gate * sg
        out_refs[0][...] = (silu * up).astype(MXU_DTYPE)
        out_refs[1][...] = (up * (sg * (1.0 + gate * (1.0 - sg)))).astype(MXU_DTYPE)
        out_refs[2][...] = silu.astype(MXU_DTYPE)

    col = lambda i, j, k: (i, j)
    wspec = lambda i, j, k: (j, 0, 0)
    return _fused_matmul(
        "ffn_gate_up", (M // tm, J, 1),
        [(h, (tm, K), lambda i, j, k: (i, 0)), (wg3, (None, K, Nc), wspec), (wu3, (None, K, Nc), wspec)],
        [(0, 1, 0, NN), (0, 2, 1, NN)], [(tm, Nc)] * 2, [],
        [((M, J * Nc), MXU_DTYPE, (tm, Nc), col)] * 3, epilogue)


def _ffn_down(act, wd2d, x2, tm=512, tn=512):
    M, F = act.shape
    D = wd2d.shape[1]
    tm, tn = _tile(M, tm, 16), _tile(D, tn, LANES)

    def epilogue(accs, ex_refs, out_refs, i, j):
        out_refs[0][...] = ex_refs[0][...] + accs[0]

    tile = lambda i, j, k: (i, j)
    return _fused_matmul(
        "ffn_down", (M // tm, D // tn, 1),
        [(act, (tm, F), lambda i, j, k: (i, 0)), (wd2d, (F, tn), lambda i, j, k: (0, j))],
        [(0, 1, 0, NN)], [(tm, tn)], [(x2, (tm, tn), tile)],
        [((M, D), jnp.float32, (tm, tn), tile)], epilogue)[0]


def _final_norm_loss(x3, target, g_row, tm=256):
    S, D = x3.shape
    tm = _tile(S, tm, 16)

    def body(x_ref, t_ref, g_ref, dx_ref, dxl_ref, loss_ref, dg_ref):
        x3v = x_ref[...]
        g = g_ref[...]
        r = _rsqrt_mean_sq(x3v)
        xhat = x3v * r
        diff = xhat * g - t_ref[...]
        dy = diff / D
        dyg = dy * g
        dx = r * (dyg - xhat * jnp.mean(dyg * xhat, axis=-1, keepdims=True))
        dx_ref[...] = dx
        dxl_ref[...] = dx.astype(MXU_DTYPE)

        @pl.when(pl.program_id(0) == 0)
        def _():
            loss_ref[...] = jnp.zeros_like(loss_ref)
            dg_ref[...] = jnp.zeros_like(dg_ref)

        loss_ref[...] += 0.5 * _sum_all(jnp.mean(diff * diff, axis=-1, keepdims=True))
        dg_ref[...] += jnp.sum(dy * xhat, axis=0, keepdims=True)

    row = pl.BlockSpec((tm, D), lambda i: (i, 0))
    vec = pl.BlockSpec((1, D), lambda i: (0, 0))
    return pl.pallas_call(
        body, name="final_norm_loss", grid=(S // tm,), in_specs=[row, row, vec],
        out_specs=[row, row, pl.BlockSpec((1, 1), lambda i: (0, 0)), vec],
        out_shape=[jax.ShapeDtypeStruct((S, D), jnp.float32), jax.ShapeDtypeStruct((S, D), MXU_DTYPE),
                   jax.ShapeDtypeStruct((1, 1), jnp.float32), jax.ShapeDtypeStruct((1, D), jnp.float32)],
        compiler_params=_params(1),
    )(x3, target, g_row)


def _ffn_down_bwd(dx3_low, wd2d, act_by_gate, act_by_up, n_shards, tm=512):
    M, D = dx3_low.shape
    F = wd2d.shape[0]
    tn = F // n_shards
    tm = _tile(M, tm, 16)

    def epilogue(accs, ex_refs, out_refs, i, j):
        dact = accs[0]
        out_refs[0][...] = (dact * ex_refs[0][...].astype(jnp.float32)).astype(MXU_DTYPE)
        out_refs[1][...] = (dact * ex_refs[1][...].astype(jnp.float32)).astype(MXU_DTYPE)

    col = lambda i, j, k: (i, j)
    return _fused_matmul(
        "ffn_down_bwd", (M // tm, n_shards, 1),
        [(dx3_low, (tm, D), lambda i, j, k: (i, 0)), (wd2d, (tn, D), lambda i, j, k: (j, 0))],
        [(0, 1, 0, NT)], [(tm, tn)],
        [(act_by_gate, (tm, tn), col), (act_by_up, (tm, tn), col)],
        [((M, F), MXU_DTYPE, (tm, tn), col)] * 2, epilogue)


def _matmul_nt_norm_bwd(name, a, b, xin, g_row, dres, low_dtype, tm=256, after=None):
    M = a.shape[0]
    sharded = b.ndim == 3
    N, Kc = b.shape[-2:]
    J = b.shape[0] if sharded else 1
    tm = _tile(M, tm, 16)
    has_low = low_dtype is not None

    def epilogue(accs, ex_refs, out_refs, i, j):
        dx, dg = _rmsnorm_bwd_values(accs[0], ex_refs[0][...], ex_refs[1][...])
        dx = dx + ex_refs[2][...]
        out_refs[0][...] = dx
        if has_low:
            out_refs[1][...] = dx.astype(low_dtype)
        dg_ref = out_refs[-1]

        @pl.when(i == 0)
        def _():
            dg_ref[...] = jnp.zeros_like(dg_ref)

        dg_ref[...] += dg

    row = lambda i, j, k: (i, 0)
    const = lambda i, j, k: (0, 0)
    b_operand = (b, (None, N, Kc), lambda i, j, k: (k, 0, 0)) if sharded else (b, (N, Kc), const)
    outs = [((M, N), jnp.float32, (tm, N), row)]
    if has_low:
        outs.append(((M, N), low_dtype, (tm, N), row))
    outs.append(((1, N), jnp.float32, (1, N), const))
    return _fused_matmul(
        name, (M // tm, 1, J), [(a, (tm, Kc), lambda i, j, k: (i, k)), b_operand],
        [(0, 1, 0, NT)], [(tm, N)],
        [(xin, (tm, N), row), (g_row, (1, N), const), (dres, (tm, N), row)] + _order_after(after),
        outs, epilogue)


def _row_tile(R, C, target_bytes=2 ** 21):
    return _tile(R, max(16, target_bytes // (4 * C)), 16)


def _prefetch_call(body, name, grid, pos, in_specs, out_specs, out_shape, ins, after=None):
    ordered = [] if after is None else [after]
    grid_spec = pltpu.PrefetchScalarGridSpec(
        num_scalar_prefetch=1, grid=grid, in_specs=list(in_specs) + [ANY] * len(ordered), out_specs=out_specs)
    return pl.pallas_call(body, name=name, grid_spec=grid_spec, out_shape=out_shape,
                          compiler_params=_params(len(grid)))(pos, *ins, *ordered)


def _cast_into_slab(name, x2d, dtype, pos, after=None):
    R, C = x2d.shape
    tr = _row_tile(R, C)

    def body(pos_ref, x_ref, *rest):
        o_ref = rest[-1]
        o_ref[...] = x_ref[...].astype(o_ref.dtype)

    return _prefetch_call(
        body, name, (R // tr,), pos, [pl.BlockSpec((tr, C), lambda i, pos: (i, 0))],
        pl.BlockSpec((None, tr, C), lambda i, pos: (pos[0], i, 0)),
        jax.ShapeDtypeStruct((N_CHIPS, R, C), dtype), [x2d], after)


def _chip_sum(name, partial, got, dtype, pos):
    J, R, C = partial.shape
    Rh = R // 2
    tr = _row_tile(Rh, C)

    def body(pos_ref, a_ref, b_ref, o_ref):
        o_ref[...] = (a_ref[...].astype(jnp.float32) + b_ref[...].astype(jnp.float32)).astype(o_ref.dtype)

    blk = pl.BlockSpec((None, tr, C), lambda j, i, pos: (j, i, 0))
    return _prefetch_call(
        body, name, (J, Rh // tr), pos,
        [pl.BlockSpec((None, None, tr, C), lambda j, i, pos: (j, pos[1], i, 0)), blk], blk,
        jax.ShapeDtypeStruct((J, Rh, C), dtype), [partial.reshape(J, 2, Rh, C), got])


def _owner_sum(name, chip_sums, others, pos):
    J, Rh, C = chip_sums.shape
    n = others.shape[0]
    tr = _row_tile(Rh, C * (n + 1))

    def body(pos_ref, own_ref, oth_ref, o_ref):
        acc = own_ref[...].astype(jnp.float32)
        for k in range(n):
            acc = acc + oth_ref[k].astype(jnp.float32)
        o_ref[...] = acc

    return _prefetch_call(
        body, name, (Rh // tr,), pos,
        [pl.BlockSpec((None, tr, C), lambda i, pos: (pos[0], i, 0)), pl.BlockSpec((n, tr, C), lambda i, pos: (0, i, 0))],
        pl.BlockSpec((None, tr, C), lambda i, pos: (pos[1], i, 0)),
        jax.ShapeDtypeStruct((2, Rh, C), jnp.float32), [chip_sums, others])


def _adamw_values(w, g, m, v):
    m = ADAM_B1 * m + (1.0 - ADAM_B1) * g
    v = ADAM_B2 * v + (1.0 - ADAM_B2) * (g * g)
    m_hat = m / (1.0 - ADAM_B1 ** ADAM_STEP)
    v_hat = v / (1.0 - ADAM_B2 ** ADAM_STEP)
    delta = -ADAM_LR * (m_hat / (jnp.sqrt(v_hat) + ADAM_EPS) + ADAM_WD * w)
    return delta, m, v


def _adamw(name, w, g, m, v):
    R, C = w.shape
    tr = _row_tile(R, C, 2 ** 20)

    def body(w_ref, g_ref, m_ref, v_ref, go_ref, d_ref, mo_ref, vo_ref):
        g = g_ref[...]
        go_ref[...] = g
        d_ref[...], mo_ref[...], vo_ref[...] = _adamw_values(w_ref[...], g, m_ref[...], v_ref[...])

    blk = pl.BlockSpec((tr, C), lambda i: (i, 0))
    return pl.pallas_call(
        body, name=name, grid=(R // tr,), in_specs=[blk] * 4, out_specs=[blk] * 4,
        out_shape=[jax.ShapeDtypeStruct((R, C), jnp.float32)] * 4, compiler_params=_params(1))(w, g, m, v)


def _adamw_small(g_packed, ws, ms, vs, row_counts):
    n = len(ws)

    def body(*refs):
        g_ref = refs[0]
        w_refs, m_refs, v_refs = refs[1:1 + n], refs[1 + n:1 + 2 * n], refs[1 + 2 * n:1 + 3 * n]
        outs = refs[1 + 3 * n:]
        off = 0
        for t in range(n):
            g = g_ref[off:off + row_counts[t], :]
            off += row_counts[t]
            delta, m_new, v_new = _adamw_values(w_refs[t][...], g, m_refs[t][...], v_refs[t][...])
            outs[4 * t][...] = g
            outs[4 * t + 1][...] = delta
            outs[4 * t + 2][...] = m_new
            outs[4 * t + 3][...] = v_new

    out_shape = []
    for t in range(n):
        out_shape += [jax.ShapeDtypeStruct((row_counts[t], LANES), jnp.float32)] * 4
    return pl.pallas_call(
        body, name="adamw_small", in_specs=[VMEM_WHOLE] * (1 + 3 * n), out_specs=[VMEM_WHOLE] * (4 * n),
        out_shape=out_shape, compiler_params=pltpu.CompilerParams(vmem_limit_bytes=VMEM_LIMIT_BYTES),
    )(g_packed, *ws, *ms, *vs)


def _position():
    x, y, c = lax.axis_index("x"), lax.axis_index("y"), lax.axis_index("c")
    other_chips = [(1 - x, y), (x, 1 - y), (1 - x, 1 - y)]
    return x, y, c, other_chips


def _half(rows, which):
    return pl.ds(which * (rows // 2), rows // 2)


def _chunk_index(shape, dtype):
    rows = shape[-2]
    total = math.prod(shape) * jnp.dtype(dtype).itemsize
    n = max(1, min(rows // COPY_ROW_ALIGN, total // COPY_CHUNK_BYTES))
    step = -(-rows // n)
    step = -(-step // COPY_ROW_ALIGN) * COPY_ROW_ALIGN
    lead = (slice(None),) * (len(shape) - 2)
    return [lead + (pl.ds(s, min(step, rows - s)),) for s in range(0, rows, step)]


def _start_in_chunks(make, src, dst, waited_here=True):
    for idx in _chunk_index(src.shape, src.dtype):
        make(src.at[idx], dst.at[idx]).start()
    return make(src, dst) if waited_here else None


def _remote_copier(send_sem, recv_sem, to):
    return lambda src, dst: pltpu.make_async_remote_copy(
        src_ref=src, dst_ref=dst, send_sem=send_sem, recv_sem=recv_sem, device_id=to, device_id_type=MESH)


def _slab_half(ref, chip, which):
    return ref.at[chip, _half(ref.shape[1], which)]


def _gather_start(name, groups, after=None):
    flat = [s for g in groups for s in g]
    T, G = len(flat), len(groups)
    ordered = [] if after is None else [after]

    def body(*refs):
        ins = refs[:T]
        sems = refs[T + len(ordered):T + len(ordered) + 2 * G]
        token = refs[-1]
        x, y, c, other_chips = _position()
        mine = 2 * x + y
        t0 = 0
        for gi, group in enumerate(groups):
            for t in range(len(group)):
                own = _slab_half(ins[t0 + t], mine, c)
                for k, (cx, cy) in enumerate(other_chips):
                    _start_in_chunks(_remote_copier(sems[2 * gi].at[3 * t + k], sems[2 * gi + 1].at[3 * t + k], (cx, cy, c)),
                                     own, own, waited_here=False)
            t0 += len(group)
        token[...] = jnp.zeros_like(token)

    sem_shapes = []
    for group in groups:
        sem_shapes += [pltpu.SemaphoreType.DMA((3 * len(group),))] * 2
    res = pl.pallas_call(
        body, name=name, in_specs=[HBM_SPEC] * T + [ANY] * len(ordered),
        out_specs=[SEM_SPEC] * (2 * G) + [HBM_SPEC] * T + [VMEM_WHOLE],
        out_shape=sem_shapes + [pltpu.HBM(s.shape, s.dtype) for s in flat] + [jax.ShapeDtypeStruct((8, LANES), jnp.float32)],
        input_output_aliases={t: 2 * G + t for t in range(T)},
        compiler_params=pltpu.CompilerParams(has_side_effects=SIDE_EFFECT),
    )(*[pltpu.with_memory_space_constraint(s, pltpu.HBM) for s in flat], *ordered)
    flights, t0 = [], 2 * G
    for gi, group in enumerate(groups):
        flights.append((res[2 * gi], res[2 * gi + 1], res[t0:t0 + len(group)]))
        t0 += len(group)
    return flights, res[-1]


def _gather_wait(name, flight, after):
    send_sems, recv_sems, slabs = flight
    T = len(slabs)

    def body(*refs):
        ins = refs[:T]
        send, recv = refs[T:T + 2]
        x, y, c, other_chips = _position()
        mine = 2 * x + y
        for t in range(T):
            own = _slab_half(ins[t], mine, c)
            for k, (cx, cy) in enumerate(other_chips):
                landed = _slab_half(ins[t], 2 * cx + cy, c)
                cp = _remote_copier(send.at[3 * t + k], recv.at[3 * t + k], (cx, cy, c))(own, landed)
                cp.wait_send()
                cp.wait_recv()

    return pl.pallas_call(
        body, name=name, in_specs=[HBM_SPEC] * T + [SEM_SPEC, SEM_SPEC, ANY], out_specs=[HBM_SPEC] * T,
        out_shape=[pltpu.HBM(s.shape, s.dtype) for s in slabs], input_output_aliases={t: t for t in range(T)},
        compiler_params=pltpu.CompilerParams(has_side_effects=SIDE_EFFECT),
    )(*slabs, send_sems, recv_sems, after)


def _share_with_sibling(name, slabs):
    T = len(slabs)

    def body(*refs):
        outs = refs[T:2 * T]
        send_sems, recv_sems = refs[2 * T:]
        x, y, c, other_chips = _position()
        sibling = (x, y, 1 - c)
        started = []
        for t in range(T):
            for k, (cx, cy) in enumerate(other_chips):
                landed = _slab_half(outs[t], 2 * cx + cy, c)
                started.append(_start_in_chunks(_remote_copier(send_sems.at[t, k], recv_sems.at[t, k], sibling), landed, landed))
        for t in range(T):
            for k, (cx, cy) in enumerate(other_chips):
                arriving = _slab_half(outs[t], 2 * cx + cy, 1 - c)
                _remote_copier(send_sems.at[t, k], recv_sems.at[t, k], sibling)(arriving, arriving).wait_recv()
        for cp in started:
            cp.wait_send()

    return pl.pallas_call(
        body, name=name, in_specs=[ANY] * T, out_specs=[ANY] * T,
        out_shape=[jax.ShapeDtypeStruct(s.shape, s.dtype) for s in slabs],
        input_output_aliases={t: t for t in range(T)},
        scratch_shapes=[pltpu.SemaphoreType.DMA((T, 3)), pltpu.SemaphoreType.DMA((T, 3))],
    )(*slabs)


def _swap_halves(name, grads):
    T = len(grads)

    def body(*refs):
        ins, got = refs[:T], refs[T:2 * T]
        send_sems, recv_sems = refs[2 * T:]
        x, y, c, _ = _position()
        remote = []
        for t in range(T):
            rows = ins[t].shape[1]
            remote.append(_start_in_chunks(
                _remote_copier(send_sems.at[t], recv_sems.at[t], (x, y, 1 - c)), ins[t].at[:, _half(rows, 1 - c)], got[t]))
        for cp in remote:
            cp.wait()

    return pl.pallas_call(
        body, name=name, in_specs=[ANY] * T, out_specs=[ANY] * T,
        out_shape=[jax.ShapeDtypeStruct((g.shape[0], g.shape[1] // 2, g.shape[2]), g.dtype) for g in grads],
        scratch_shapes=[pltpu.SemaphoreType.DMA((T,)), pltpu.SemaphoreType.DMA((T,))],
    )(*grads)


def _swap_start(name, grads):
    T = len(grads)
    lands = [lax.empty((g.shape[0], g.shape[1] // 2, g.shape[2]), g.dtype) for g in grads]

    def body(*refs):
        ins, got = refs[:T], refs[T:2 * T]
        send_sems, recv_sems = refs[2 * T:2 * T + 2]
        token = refs[-1]
        x, y, c, _ = _position()
        for t in range(T):
            _start_in_chunks(_remote_copier(send_sems.at[t], recv_sems.at[t], (x, y, 1 - c)),
                             ins[t].at[:, _half(ins[t].shape[1], 1 - c)], got[t], waited_here=False)
        token[...] = jnp.zeros_like(token)

    arrays = list(grads) + lands
    res = pl.pallas_call(
        body, name=name, in_specs=[HBM_SPEC] * (2 * T),
        out_specs=[SEM_SPEC, SEM_SPEC] + [HBM_SPEC] * (2 * T) + [VMEM_WHOLE],
        out_shape=[pltpu.SemaphoreType.DMA((T,)), pltpu.SemaphoreType.DMA((T,))]
        + [pltpu.HBM(a.shape, a.dtype) for a in arrays] + [jax.ShapeDtypeStruct((8, LANES), jnp.float32)],
        input_output_aliases={i: 2 + i for i in range(2 * T)},
        compiler_params=pltpu.CompilerParams(has_side_effects=SIDE_EFFECT),
    )(*[pltpu.with_memory_space_constraint(a, pltpu.HBM) for a in arrays])
    return res[0], res[1], res[2:2 + T], res[2 + T:2 + 2 * T], res[-1]


def _swap_wait(name, flight, after):
    send_sems, recv_sems, grads, lands, _ = flight
    T = len(grads)

    def body(*refs):
        ins, got = refs[:T], refs[T:2 * T]
        send, recv = refs[2 * T:2 * T + 2]
        x, y, c, _ = _position()
        for t in range(T):
            cp = _remote_copier(send.at[t], recv.at[t], (x, y, 1 - c))(ins[t].at[:, _half(ins[t].shape[1], 1 - c)], got[t])
            cp.wait_send()
            cp.wait_recv()

    arrays = list(grads) + list(lands)
    res = pl.pallas_call(
        body, name=name, in_specs=[HBM_SPEC] * (2 * T) + [SEM_SPEC, SEM_SPEC, ANY], out_specs=[HBM_SPEC] * (2 * T),
        out_shape=[pltpu.HBM(a.shape, a.dtype) for a in arrays], input_output_aliases={i: i for i in range(2 * T)},
        compiler_params=pltpu.CompilerParams(has_side_effects=SIDE_EFFECT),
    )(*arrays, send_sems, recv_sems, after)
    return res[:T], res[T:]


def _scatter_start(name, chip_sums):
    T = len(chip_sums)
    lands = [lax.empty((3,) + s.shape[1:], s.dtype) for s in chip_sums]

    def body(*refs):
        srcs, dsts = refs[:T], refs[T:2 * T]
        send_sems, recv_sems = refs[2 * T:2 * T + 2]
        token = refs[-1]
        x, y, c, other_chips = _position()
        for t in range(T):
            for k, (cx, cy) in enumerate(other_chips):
                _start_in_chunks(_remote_copier(send_sems.at[3 * t + k], recv_sems.at[3 * t + k], (cx, cy, c)),
                                 srcs[t].at[2 * cx + cy], dsts[t].at[k], waited_here=False)
        token[...] = jnp.zeros_like(token)

    hbm = [pltpu.HBM(a.shape, a.dtype) for a in list(chip_sums) + lands]
    res = pl.pallas_call(
        body, name=name, in_specs=[HBM_SPEC] * (2 * T),
        out_specs=[SEM_SPEC, SEM_SPEC] + [HBM_SPEC] * (2 * T) + [VMEM_WHOLE],
        out_shape=[pltpu.SemaphoreType.DMA((3 * T,)), pltpu.SemaphoreType.DMA((3 * T,))] + hbm
        + [jax.ShapeDtypeStruct((8, LANES), jnp.float32)],
        input_output_aliases={i: 2 + i for i in range(2 * T)},
        compiler_params=pltpu.CompilerParams(has_side_effects=SIDE_EFFECT),
    )(*[pltpu.with_memory_space_constraint(a, pltpu.HBM) for a in list(chip_sums) + lands])
    return res[0], res[1], res[2:2 + T], res[2 + T:2 + 2 * T], res[-1]


def _scatter_wait(name, send_sems, recv_sems, chip_sums, lands, after):
    T = len(chip_sums)

    def body(*refs):
        srcs, dsts = refs[:T], refs[T:2 * T]
        send, recv = refs[2 * T:2 * T + 2]
        x, y, c, other_chips = _position()
        for t in range(T):
            for k, (cx, cy) in enumerate(other_chips):
                cp = _remote_copier(send.at[3 * t + k], recv.at[3 * t + k], (cx, cy, c))(srcs[t].at[2 * cx + cy], dsts[t].at[k])
                cp.wait_send()
                cp.wait_recv()

    arrays = list(chip_sums) + list(lands)
    res = pl.pallas_call(
        body, name=name, in_specs=[HBM_SPEC] * (2 * T) + [SEM_SPEC, SEM_SPEC, ANY],
        out_specs=[HBM_SPEC] * (2 * T), out_shape=[pltpu.HBM(a.shape, a.dtype) for a in arrays],
        input_output_aliases={i: i for i in range(2 * T)},
        compiler_params=pltpu.CompilerParams(has_side_effects=SIDE_EFFECT),
    )(*arrays, send_sems, recv_sems, after)
    return res[:T], res[T:]


def _as_shards(partial):
    return [p.reshape((N_CHIPS, -1, p.shape[-1])) for p in partial]


def _reduce_begin(tag, names, partial, pos):
    parts = _as_shards(partial)
    received = _swap_halves("swap_grad_halves_" + tag, parts)
    return _reduce_scatter_start(tag, names, parts, received, pos)


def _reduce_scatter_start(tag, names, parts, received, pos):
    chip_sums = [_chip_sum("chip_sum_" + n, p, r, WIRE_DTYPE, pos) for n, p, r in zip(names, parts, received)]
    return _scatter_start("scatter_start_" + tag, chip_sums)


def _reduce_end(tag, names, flight, after, pos):
    send_sems, recv_sems, chip_sums, lands, _ = flight
    chip_sums, others = _scatter_wait("scatter_wait_" + tag, send_sems, recv_sems, chip_sums, lands, after)
    return {n: _owner_sum("owner_sum_" + n, cs, ot, pos) for n, cs, ot in zip(names, chip_sums, others)}


def _join_halves(halves):
    T = len(halves)

    def body(*refs):
        outs = refs[T:2 * T]
        send_sems, recv_sems = refs[2 * T:]
        x, y, c, _ = _position()

        def copier(t):
            return _remote_copier(send_sems.at[t], recv_sems.at[t], (x, y, 1 - c))

        remote = [_start_in_chunks(copier(t), outs[t].at[c], outs[t].at[c]) for t in range(T)]
        for t in range(T):
            copier(t)(outs[t].at[1 - c], outs[t].at[1 - c]).wait_recv()
        for cp in remote:
            cp.wait_send()

    return pl.pallas_call(
        body, name="join_grad_halves", in_specs=[ANY] * T, out_specs=[ANY] * T,
        out_shape=[jax.ShapeDtypeStruct(h.shape, h.dtype) for h in halves],
        input_output_aliases={t: t for t in range(T)},
        scratch_shapes=[pltpu.SemaphoreType.DMA((T,)), pltpu.SemaphoreType.DMA((T,))],
    )(*halves)


def _allreduce_small(part):
    rows = part.shape[0]

    def body(x_ref, out_ref, gath, send_sems, recv_sems):
        x, y, c, other_chips = _position()
        sibling = (x, y, 1 - c)

        def slot(px, py, pc):
            return gath.at[4 * px + 2 * py + pc]

        def copy(k, owner, to, src=None):
            return pltpu.make_async_remote_copy(
                src_ref=slot(*owner) if src is None else src, dst_ref=slot(*owner), send_sem=send_sems.at[k],
                recv_sem=recv_sems.at[k], device_id=to, device_id_type=MESH)

        gath[4 * x + 2 * y + c] = x_ref[...]
        first = [copy(0, (x, y, c), sibling, src=x_ref)]
        first += [copy(1 + k, (x, y, c), (cx, cy, c), src=x_ref) for k, (cx, cy) in enumerate(other_chips)]
        for cp in first:
            cp.start()
        passed = [copy(4 + k, (cx, cy, c), sibling) for k, (cx, cy) in enumerate(other_chips)]
        for k, (cx, cy) in enumerate(other_chips):
            copy(1 + k, (cx, cy, c), (x, y, c)).wait_recv()
            passed[k].start()
        copy(0, (x, y, 1 - c), (x, y, c)).wait_recv()
        for k, (cx, cy) in enumerate(other_chips):
            copy(4 + k, (cx, cy, 1 - c), (x, y, c)).wait_recv()
        for cp in first + passed:
            cp.wait_send()
        acc = gath[0]
        for d in range(1, N_DEV):
            acc = acc + gath[d]
        out_ref[...] = acc

    return pl.pallas_call(
        body, name="allreduce_small_grads", in_specs=[VMEM_WHOLE], out_specs=VMEM_WHOLE,
        out_shape=jax.ShapeDtypeStruct((rows, LANES), jnp.float32),
        scratch_shapes=[pltpu.VMEM((N_DEV, rows, LANES), jnp.float32), pltpu.SemaphoreType.DMA((7,)),
                        pltpu.SemaphoreType.DMA((7,))],
        compiler_params=pltpu.CompilerParams(vmem_limit_bytes=VMEM_LIMIT_BYTES),
    )(part)


BIG = ("w_in", "pool_w", "w_out", "w_q", "w_k", "w_v", "w_o", "w_gate", "w_up", "w_down")
SMALL = ("norm_mix_g", "pool_scale", "sgu_norm_g", "w_spatial", "b_spatial", "norm_xattn_g", "norm_mem_g",
         "norm_ffn_g", "final_norm_g")
GATHER_ORDER = (("w_in", "pool_w"), ("w_out",), ("w_q", "w_k", "w_v", "w_o"), ("w_gate", "w_up"), ("w_down",))
WEIGHTS = ("norm_mix_g", "w_in", "pool_w", "pool_scale", "sgu_norm_g", "w_spatial", "b_spatial", "w_out",
           "norm_xattn_g", "norm_mem_g", "w_q", "w_k", "w_v", "w_o", "norm_ffn_g", "w_gate", "w_up", "w_down",
           "final_norm_g")


def _as2d(a):
    return a.reshape(-1, a.shape[-1])


def _as_lanes(a):
    return a.reshape(-1, LANES)


def kernel(x, mem, norm_mix_g, w_in, pool_w, pool_scale, sgu_norm_g, w_spatial, b_spatial, w_out, norm_xattn_g, norm_mem_g, w_q, w_k, w_v, w_o, norm_ffn_g, w_gate, w_up, w_down, final_norm_g, loss_target, m_norm_mix_g, m_w_in, m_pool_w, m_pool_scale, m_sgu_norm_g, m_w_spatial, m_b_spatial, m_w_out, m_norm_xattn_g, m_norm_mem_g, m_w_q, m_w_k, m_w_v, m_w_o, m_norm_ffn_g, m_w_gate, m_w_up, m_w_down, m_final_norm_g, v_norm_mix_g, v_w_in, v_pool_w, v_pool_scale, v_sgu_norm_g, v_w_spatial, v_b_spatial, v_w_out, v_norm_xattn_g, v_norm_mem_g, v_w_q, v_w_k, v_w_v, v_w_o, v_norm_ffn_g, v_w_gate, v_w_up, v_w_down, v_final_norm_g):
    args = dict(locals())
    weights = {n: args[n] for n in WEIGHTS}
    moments_m = {n: args["m_" + n] for n in WEIGHTS}
    moments_v = {n: args["v_" + n] for n in WEIGHTS}
    low = MXU_DTYPE

    xs = x[0]
    mems = mem[0]
    target = loss_target[0]
    D = xs.shape[1]
    G, PG = pool_w.shape[1], pool_w.shape[3]

    pos = jnp.stack([2 * lax.axis_index("x") + lax.axis_index("y"), lax.axis_index("c")]).astype(jnp.int32)
    first = {n: _cast_into_slab("cast_" + n, _as2d(weights[n]), WIRE_DTYPE, pos) for n in GATHER_ORDER[0]}
    flights, started = _gather_start("gather_start_first", [[first[n] for n in GATHER_ORDER[0]]])
    rest = {n: _cast_into_slab("cast_" + n, _as2d(weights[n]), WIRE_DTYPE, pos, after=started)
            for group in GATHER_ORDER[1:] for n in group}
    more, started = _gather_start("gather_start_rest", [[rest[n] for n in group] for group in GATHER_ORDER[1:]], after=started)
    flights = flights + more

    def arrive(gi, after):
        landed = _gather_wait("gather_wait_%d" % gi, flights[gi], after)
        return dict(zip(GATHER_ORDER[gi], _share_with_sibling("gather_share_%d" % gi, landed)))

    def rows(a):
        return a.reshape(-1, a.shape[-1])

    g_mix, g_xattn, g_mem, g_ffn = norm_mix_g, norm_xattn_g, norm_mem_g, norm_ffn_g
    g_final = final_norm_g.reshape(1, D)
    ws = w_spatial[0]
    b_full = jnp.broadcast_to(b_spatial[0][:, :, None], ws.shape[:2] + (sgu_norm_g.shape[1] // ws.shape[0],))

    h1 = _rmsnorm("norm_mix", xs, g_mix, low, after=started)
    mn = _rmsnorm("norm_mem", mems, g_mem, low)
    got = arrive(0, mn)
    wg_in = got["w_in"]
    pw = got["pool_w"].reshape(N_CHIPS, G, PG // N_CHIPS, PG).transpose(1, 0, 2, 3).reshape(G, PG, PG)
    proj = _matmul_nn_cols("proj_in", h1, wg_in, jnp.float32)
    z = _mixer_fwd(proj, pw, pool_scale, sgu_norm_g, ws, b_full)
    full = {"w_out": rows(arrive(1, z)["w_out"])}
    x1, h2 = _proj_residual_norm("mix_out", z, full["w_out"], xs, g_xattn)
    full.update({n: rows(a) for n, a in arrive(2, h2).items()})
    q = _matmul_nn("xattn_q", h2, full["w_q"], low)
    k = _matmul_nn("xattn_k", mn, full["w_k"], low)
    vv = _matmul_nn("xattn_v", mn, full["w_v"], low)
    o = _attn_fwd(q, k, vv)
    x2, h3 = _proj_residual_norm("xattn_out", o, full["w_o"], x1, g_ffn)
    got = arrive(3, h3)
    wg_gate, wg_up = got["w_gate"], got["w_up"]
    act, act_by_gate, act_by_up = _ffn_gate_up(h3, wg_gate, wg_up)
    full["w_down"] = rows(arrive(4, act)["w_down"])
    x3 = _ffn_down(act, full["w_down"], x2)
    dx3, dx3_low, loss_part, d_final_g = _final_norm_loss(x3, target, g_final)

    dgate, dup = _ffn_down_bwd(dx3_low, full["w_down"], act_by_gate, act_by_up, N_CHIPS)
    (d_w_down,) = _matmul_tn("grad_w_down", act, [dx3_low], WIRE_DTYPE)
    (d_w_gate,) = _matmul_tn_cols("grad_w_gate", h3, [dgate], N_CHIPS, WIRE_DTYPE)
    (d_w_up,) = _matmul_tn_cols("grad_w_up", h3, [dup], N_CHIPS, WIRE_DTYPE)
    ffn_names = ("w_gate", "w_up", "w_down")
    ffn_swap = _swap_start("swap_start_ffn", _as_shards([d_w_gate, d_w_up, d_w_down]))
    dh3 = _matmul_nt_cols("ffn_dh", [dgate, dup], [wg_gate, wg_up], jnp.float32, after=ffn_swap[-1])
    ffn_flight = _reduce_scatter_start("ffn", ffn_names, *_swap_wait("swap_wait_ffn", ffn_swap, dh3), pos)
    dx2, dx2_low, d_ffn_g = _rmsnorm_bwd("norm_ffn_bwd", dh3, x2, g_ffn, dx3, low)

    do = _matmul_nt("xattn_do", [dx2_low], [full["w_o"]], low, after=ffn_flight[-1])
    (d_w_o,) = _matmul_tn("grad_w_o", o, [dx2_low], WIRE_DTYPE)
    dq, dk, dvv = _attn_bwd(q, k, vv, do)
    (d_w_q,) = _matmul_tn("grad_w_q", h2, [dq], WIRE_DTYPE)
    dx1, dx1_low, d_xattn_g = _matmul_nt_norm_bwd("xattn_dh_norm_bwd", dq, full["w_q"], x1, g_xattn, dx2, low)
    d_w_k, d_w_v = _matmul_tn("grad_w_kv", mn, [dk, dvv], WIRE_DTYPE)
    dmn = _matmul_nt("xattn_dmem", [dk, dvv], [full["w_k"], full["w_v"]], jnp.float32)
    (d_mem_g,) = _rmsnorm_bwd("norm_mem_bwd", dmn, mems, g_mem, None, None)

    (d_w_out,) = _matmul_tn("grad_w_out", z, [dx1_low], WIRE_DTYPE)
    attn_names = ("w_q", "w_k", "w_v", "w_o", "w_out")
    attn_swap = _swap_start("swap_start_attn", _as_shards([d_w_q, d_w_k, d_w_v, d_w_o, d_w_out]))
    dz = _matmul_nt("mix_dz", [dx1_low], [full["w_out"]], jnp.float32, after=attn_swap[-1])
    attn_flight = _reduce_scatter_start("attn", attn_names, *_swap_wait("swap_wait_attn", attn_swap, dz), pos)
    dproj, d_pw, d_ps, d_gs, d_ws, d_b = _mixer_bwd(proj, dz, pw, pool_scale, sgu_norm_g, ws, b_full)
    (d_w_in,) = _matmul_tn_cols("grad_w_in", h1, [dproj], N_CHIPS, WIRE_DTYPE, tka=1024, after=attn_flight[-1])
    d_pool_w = d_pw.reshape(G, N_CHIPS, PG // N_CHIPS, PG).transpose(1, 0, 2, 3).reshape(N_CHIPS, PG, PG).astype(WIRE_DTYPE)
    mix_names = ("w_in", "pool_w")
    mix_flight = _reduce_begin("mix", mix_names, [d_w_in, d_pool_w], pos)
    grad_x, d_mix_g = _matmul_nt_norm_bwd("mix_dh_norm_bwd", dproj, wg_in, xs, g_mix, dx1, None, after=mix_flight[-1])

    reduced = _reduce_end("ffn", ffn_names, ffn_flight, grad_x, pos)
    reduced.update(_reduce_end("attn", attn_names, attn_flight, grad_x, pos))
    reduced.update(_reduce_end("mix", mix_names, mix_flight, grad_x, pos))
    grads = {n: g.reshape(-1, g.shape[-1]) for n, g in zip(BIG, _join_halves([reduced[n] for n in BIG]))}

    small_parts = {
        "norm_mix_g": d_mix_g, "pool_scale": d_ps, "sgu_norm_g": d_gs, "w_spatial": d_ws, "b_spatial": d_b,
        "norm_xattn_g": d_xattn_g, "norm_mem_g": d_mem_g, "norm_ffn_g": d_ffn_g, "final_norm_g": d_final_g,
    }
    row_counts = [small_parts[n].size // LANES for n in SMALL]
    packed = jnp.concatenate([_as_lanes(small_parts[n]) for n in SMALL], axis=0)
    small_sum = _allreduce_small(packed)

    out_grad, out_delta, out_m, out_v = {}, {}, {}, {}
    for n in BIG:
        shape = weights[n].shape
        g2d, delta, m_new, v_new = _adamw("adamw_" + n, _as2d(weights[n]), grads[n], _as2d(moments_m[n]), _as2d(moments_v[n]))
        out_grad[n], out_delta[n] = g2d.reshape(shape), delta.reshape(shape)
        out_m[n], out_v[n] = m_new.reshape(shape), v_new.reshape(shape)
    small_out = _adamw_small(
        small_sum, [_as_lanes(weights[n]) for n in SMALL], [_as_lanes(moments_m[n]) for n in SMALL],
        [_as_lanes(moments_v[n]) for n in SMALL], row_counts)
    for t, n in enumerate(SMALL):
        shape = weights[n].shape
        out_grad[n], out_delta[n], out_m[n], out_v[n] = (a.reshape(shape) for a in small_out[4 * t:4 * t + 4])

    loss = lax.psum(loss_part[0, 0], ("x", "y", "c"))
    return (loss, grad_x[None], *[out_grad[n] for n in WEIGHTS], *[out_delta[n] for n in WEIGHTS],
            *[out_m[n] for n in WEIGHTS], *[out_v[n] for n in WEIGHTS])
```

```python
import math

import jax
import jax.numpy as jnp
from jax import lax
from jax.experimental import pallas as pl
from jax.experimental.pallas import tpu as pltpu

MXU_DTYPE = jnp.bfloat16
WIRE_DTYPE = jnp.bfloat16

EPS = 1e-6
CHUNK = 64
POOL_WINDOWS = (2, 4, 8, 16)
POOL_HALO = 16
N_XATTN_HEADS = 4
ADAM_LR = 0.001
ADAM_B1 = 0.9
ADAM_B2 = 0.999
ADAM_EPS = 1e-08
ADAM_WD = 0.01
ADAM_STEP = 10

N_CHIPS = 4
N_DEV = 8
LANES = 128
VMEM_LIMIT_BYTES = 56 * 2 ** 20
COPY_CHUNK_BYTES = 512 * 2 ** 10
COPY_ROW_ALIGN = 16

MESH = pl.DeviceIdType.MESH
ANY = pl.BlockSpec(memory_space=pl.ANY)
VMEM_WHOLE = pl.BlockSpec(memory_space=pltpu.VMEM)
HBM_SPEC = pl.BlockSpec(memory_space=pltpu.HBM)
SEM_SPEC = pl.BlockSpec(memory_space=pltpu.SEMAPHORE)
SIDE_EFFECT = pltpu.SideEffectType.DATAFLOW_SIDE_EFFECTING

NN = (((1,), (0,)), ((), ()))
NT = (((1,), (1,)), ((), ()))
TN = (((0,), (0,)), ((), ()))


def _tile(n, target, mult):
    best = None
    for t in range(mult, min(n, target) + 1, mult):
        if n % t == 0:
            best = t
    return n if best is None else best


def _params(n_grid_axes):
    return pltpu.CompilerParams(dimension_semantics=("arbitrary",) * n_grid_axes, vmem_limit_bytes=VMEM_LIMIT_BYTES)


def _rsqrt_mean_sq(x):
    return lax.rsqrt(jnp.mean(x * x, axis=-1, keepdims=True) + EPS)


def _sum_all(x):
    return jnp.sum(jnp.sum(x, axis=1, keepdims=True), axis=0, keepdims=True)


def _sigmoid(x):
    return 0.5 * jnp.tanh(0.5 * x) + 0.5


def _store_epilogue(accs, ex_refs, out_refs, i, j):
    for a, o in zip(accs, out_refs):
        o[...] = a.astype(o.dtype)


def _fused_matmul(name, grid, operands, pairs, acc_shapes, extras, outs, epilogue):
    nk = grid[2]
    n_op, n_ex, n_out, n_acc = len(operands), len(extras), len(outs), len(acc_shapes)
    in_place = nk > 1 and epilogue is _store_epilogue and all(o[1] == jnp.float32 for o in outs)

    def body(*refs):
        op_refs = refs[:n_op]
        ex_refs = refs[n_op:n_op + n_ex]
        out_refs = refs[n_op + n_ex:n_op + n_ex + n_out]
        acc_refs = out_refs if in_place else refs[n_op + n_ex + n_out:]
        i, j, k = pl.program_id(0), pl.program_id(1), pl.program_id(2)
        loaded = {}

        def operand(n):
            if n not in loaded:
                loaded[n] = op_refs[n][...].astype(MXU_DTYPE)
            return loaded[n]

        def product(ia, ib, dims):
            return lax.dot_general(operand(ia), operand(ib), dims, preferred_element_type=jnp.float32)

        def products():
            parts = [None] * n_acc
            for ia, ib, iacc, dims in pairs:
                d = product(ia, ib, dims)
                parts[iacc] = d if parts[iacc] is None else parts[iacc] + d
            return parts

        if nk == 1:
            epilogue(products(), ex_refs, out_refs, i, j)
            return

        @pl.when(k == 0)
        def _():
            for a in range(n_acc):
                acc_refs[a][...] = jnp.zeros_like(acc_refs[a])

        for a, part in enumerate(products()):
            acc_refs[a][...] += part

        if not in_place:
            @pl.when(k == nk - 1)
            def _():
                epilogue([acc_refs[a][...] for a in range(n_acc)], ex_refs, out_refs, i, j)

    arrays = [o[0] for o in operands] + [e[0] for e in extras]
    in_specs = [pl.BlockSpec(o[1], o[2]) for o in operands]
    in_specs += [ANY if e[1] is None else pl.BlockSpec(e[1], e[2]) for e in extras]
    return pl.pallas_call(
        body,
        name=name,
        grid=grid,
        in_specs=in_specs,
        out_specs=[pl.BlockSpec(o[2], o[3]) for o in outs],
        out_shape=[jax.ShapeDtypeStruct(o[0], o[1]) for o in outs],
        scratch_shapes=[pltpu.VMEM(s, jnp.float32) for s in acc_shapes] if nk > 1 and not in_place else [],
        compiler_params=_params(3),
    )(*arrays)


def _matmul_nn(name, a, b, out_dtype, tm=1024, tn=512):
    M, K = a.shape
    N = b.shape[1]
    tm, tn = _tile(M, tm, 16), _tile(N, tn, LANES)
    return _fused_matmul(
        name, (M // tm, N // tn, 1),
        [(a, (tm, K), lambda i, j, k: (i, 0)), (b, (K, tn), lambda i, j, k: (0, j))],
        [(0, 1, 0, NN)], [(tm, tn)], [],
        [((M, N), out_dtype, (tm, tn), lambda i, j, k: (i, j))], _store_epilogue)[0]


def _matmul_nn_cols(name, a, b3, out_dtype, tm=1024):
    M, K = a.shape
    J, _, Nc = b3.shape
    tm = _tile(M, tm, 16)
    return _fused_matmul(
        name, (M // tm, J, 1),
        [(a, (tm, K), lambda i, j, k: (i, 0)), (b3, (None, K, Nc), lambda i, j, k: (j, 0, 0))],
        [(0, 1, 0, NN)], [(tm, Nc)], [],
        [((M, J * Nc), out_dtype, (tm, Nc), lambda i, j, k: (i, j))], _store_epilogue)[0]


def _order_after(after):
    return [] if after is None else [(after, None, None)]


def _matmul_nt(name, a_list, b_list, out_dtype, tm=1024, tn=512, after=None):
    M, K = a_list[0].shape
    N = b_list[0].shape[0]
    tm, tn = _tile(M, tm, 16), _tile(N, tn, LANES)
    n = len(a_list)
    operands = [(a, (tm, K), lambda i, j, k: (i, 0)) for a in a_list]
    operands += [(b, (tn, K), lambda i, j, k: (j, 0)) for b in b_list]
    return _fused_matmul(
        name, (M // tm, N // tn, 1), operands,
        [(p, n + p, 0, NT) for p in range(n)], [(tm, tn)], _order_after(after),
        [((M, N), out_dtype, (tm, tn), lambda i, j, k: (i, j))], _store_epilogue)[0]


def _matmul_nt_cols(name, a_list, b3_list, out_dtype, tm=1024, tn=512, after=None):
    M = a_list[0].shape[0]
    J, N, Kc = b3_list[0].shape
    tm, tn = _tile(M, tm, 16), _tile(N, tn, LANES)
    n = len(a_list)
    operands = [(a, (tm, Kc), lambda i, j, k: (i, k)) for a in a_list]
    operands += [(b, (None, tn, Kc), lambda i, j, k: (k, j, 0)) for b in b3_list]
    return _fused_matmul(
        name, (M // tm, N // tn, J), operands,
        [(p, n + p, 0, NT) for p in range(n)], [(tm, tn)], _order_after(after),
        [((M, N), out_dtype, (tm, tn), lambda i, j, k: (i, j))], _store_epilogue)[0]


def _matmul_tn(name, a, b_list, out_dtype, tka=512, tn=1024, tkm=4096):
    M, Ka = a.shape
    N = b_list[0].shape[1]
    tka, tn, tkm = _tile(Ka, tka, LANES), _tile(N, tn, LANES), _tile(M, tkm, 16)
    n = len(b_list)
    operands = [(a, (tkm, tka), lambda i, j, k: (k, i))]
    operands += [(b, (tkm, tn), lambda i, j, k: (k, j)) for b in b_list]
    return _fused_matmul(
        name, (Ka // tka, N // tn, M // tkm), operands,
        [(0, 1 + p, p, TN) for p in range(n)], [(tka, tn)] * n, [],
        [((Ka, N), out_dtype, (tka, tn), lambda i, j, k: (i, j))] * n, _store_epilogue)


def _matmul_tn_cols(name, a, b_list, n_shards, out_dtype, tka=512, tkm=4096, after=None):
    M, Ka = a.shape
    Nc = b_list[0].shape[1] // n_shards
    tka, tkm = _tile(Ka, tka, LANES), _tile(M, tkm, 16)
    n = len(b_list)
    operands = [(a, (tkm, tka), lambda i, j, k: (k, i))]
    operands += [(b, (tkm, Nc), lambda i, j, k: (k, j)) for b in b_list]
    return _fused_matmul(
        name, (Ka // tka, n_shards, M // tkm), operands,
        [(0, 1 + p, p, TN) for p in range(n)], [(tka, Nc)] * n, _order_after(after),
        [((n_shards, Ka, Nc), out_dtype, (None, tka, Nc), lambda i, j, k: (j, i, 0))] * n, _store_epilogue)


def _rmsnorm(name, x, g_row, out_dtype, tm=512, after=None):
    S, D = x.shape
    tm = _tile(S, tm, 16)

    def body(x_ref, g_ref, *rest):
        o_ref = rest[-1]
        xv = x_ref[...]
        o_ref[...] = (xv * _rsqrt_mean_sq(xv) * g_ref[...]).astype(o_ref.dtype)

    ordered = [] if after is None else [after]
    return pl.pallas_call(
        body, name=name, grid=(S // tm,),
        in_specs=[pl.BlockSpec((tm, D), lambda i: (i, 0)), pl.BlockSpec((1, D), lambda i: (0, 0))] + [ANY] * len(ordered),
        out_specs=pl.BlockSpec((tm, D), lambda i: (i, 0)),
        out_shape=jax.ShapeDtypeStruct((S, D), out_dtype),
        compiler_params=_params(1),
    )(x, g_row, *ordered)


def _rmsnorm_bwd_values(dh, xin, g):
    r = _rsqrt_mean_sq(xin)
    xhat = xin * r
    dhg = dh * g
    dx = r * (dhg - xhat * jnp.mean(dhg * xhat, axis=-1, keepdims=True))
    return dx, jnp.sum(dh * xhat, axis=0, keepdims=True)


def _rmsnorm_bwd(name, dh, xin, g_row, dres, low_dtype, tm=256):
    S, D = xin.shape
    tm = _tile(S, tm, 16)
    has_dx = dres is not None
    has_low = has_dx and low_dtype is not None

    def body(*refs):
        dh_ref, x_ref, g_ref = refs[:3]
        rest = refs[3:]
        dx, dg = _rmsnorm_bwd_values(dh_ref[...], x_ref[...], g_ref[...])
        if has_dx:
            dres_ref, rest = rest[0], rest[1:]
            dx = dx + dres_ref[...]
            rest[0][...] = dx
            if has_low:
                rest[1][...] = dx.astype(low_dtype)
        dg_ref = rest[-1]

        @pl.when(pl.program_id(0) == 0)
        def _():
            dg_ref[...] = jnp.zeros_like(dg_ref)

        dg_ref[...] += dg

    row = pl.BlockSpec((tm, D), lambda i: (i, 0))
    vec = pl.BlockSpec((1, D), lambda i: (0, 0))
    ins, in_specs = [dh, xin, g_row], [row, row, vec]
    out_shape, out_specs = [], []
    if has_dx:
        ins.append(dres)
        in_specs.append(row)
        out_shape.append(jax.ShapeDtypeStruct((S, D), jnp.float32))
        out_specs.append(row)
        if has_low:
            out_shape.append(jax.ShapeDtypeStruct((S, D), low_dtype))
            out_specs.append(row)
    out_shape.append(jax.ShapeDtypeStruct((1, D), jnp.float32))
    out_specs.append(vec)
    return pl.pallas_call(
        body, name=name, grid=(S // tm,), in_specs=in_specs, out_specs=out_specs, out_shape=out_shape,
        compiler_params=_params(1),
    )(*ins)


def _chunk_mask(blk):
    t = lax.broadcasted_iota(jnp.int32, (blk, blk), 0)
    s = lax.broadcasted_iota(jnp.int32, (blk, blk), 1)
    return (s // CHUNK) <= (t // CHUNK)


def _pool_inputs(a, halo, row0, tm, g, pg):
    cols = slice(g * pg, (g + 1) * pg)
    w = POOL_WINDOWS[g]
    s = jnp.concatenate([halo[:, cols], a[:, cols]], axis=0)
    span = 1
    while span < w:
        s = s + pltpu.roll(s, span, axis=0)
        span *= 2
    t = lax.broadcasted_iota(jnp.int32, (tm, 1), 0) + row0
    cnt = jnp.minimum(t + 1, w).astype(jnp.float32)
    return s[POOL_HALO:] / cnt - a[:, cols]


def _mixer_specs(S, tm, DP, DS):
    per_halo = tm // POOL_HALO
    n_halo = S // POOL_HALO
    a_spec = pl.BlockSpec((tm, DP), lambda i: (i, 0))
    prev_spec = pl.BlockSpec((POOL_HALO, DP), lambda i: (jnp.maximum(i * per_halo - 1, 0), 0))
    next_spec = pl.BlockSpec((POOL_HALO, DP), lambda i: (jnp.minimum((i + 1) * per_halo, n_halo - 1), 0))
    u_spec = pl.BlockSpec((tm, DS), lambda i: (i, 1))
    v_spec = pl.BlockSpec((tm, DS), lambda i: (i, 2))
    return a_spec, prev_spec, next_spec, u_spec, v_spec


def _whole(arr):
    nd = arr.ndim
    return pl.BlockSpec(arr.shape, lambda i: (0,) * nd)


def _mixer_fwd(proj, pw, ps_row, gs_row, ws, b_full, tm=512):
    S = proj.shape[0]
    G, PG, _ = pw.shape
    DP = G * PG
    H, BLK, _ = ws.shape
    DS = gs_row.shape[1]
    HD = DS // H
    assert DP == DS and proj.shape[1] == DP + 2 * DS
    tm = _tile(S, tm, BLK)

    def body(a_ref, prev_ref, u_ref, v_ref, pw_ref, ps_ref, gs_ref, ws_ref, b_ref, z_ref):
        i = pl.program_id(0)
        a = a_ref[...]
        halo = jnp.where(i > 0, prev_ref[...], 0.0)
        for g in range(G):
            cols = slice(g * PG, (g + 1) * PG)
            p = _pool_inputs(a, halo, i * tm, tm, g, PG)
            y = jnp.dot(p.astype(MXU_DTYPE), pw_ref[g], preferred_element_type=jnp.float32) * ps_ref[:, cols]
            z_ref[:, cols] = y.astype(z_ref.dtype)
        v = v_ref[...]
        vn = (v * _rsqrt_mean_sq(v) * gs_ref[...]).astype(MXU_DTYPE)
        u = u_ref[...]
        mask = _chunk_mask(BLK)
        for h in range(H):
            wm = jnp.where(mask, ws_ref[h], 0.0).astype(MXU_DTYPE)
            bias = b_ref[h]
            cols = slice(h * HD, (h + 1) * HD)
            for n in range(tm // BLK):
                rows = slice(n * BLK, (n + 1) * BLK)
                mixed = jnp.dot(wm, vn[rows, cols], preferred_element_type=jnp.float32) + bias
                z_ref[rows, DP + h * HD:DP + (h + 1) * HD] = (u[rows, cols] * mixed).astype(z_ref.dtype)

    a_spec, prev_spec, _, u_spec, v_spec = _mixer_specs(S, tm, DP, DS)
    return pl.pallas_call(
        body, name="mixer_fwd", grid=(S // tm,),
        in_specs=[a_spec, prev_spec, u_spec, v_spec, _whole(pw), _whole(ps_row), _whole(gs_row), _whole(ws), _whole(b_full)],
        out_specs=pl.BlockSpec((tm, DP + DS), lambda i: (i, 0)),
        out_shape=jax.ShapeDtypeStruct((S, DP + DS), MXU_DTYPE),
        compiler_params=_params(1),
    )(proj, proj, proj, proj, pw, ps_row, gs_row, ws, b_full)


def _mixer_bwd(proj, dz, pw, ps_row, gs_row, ws, b_full, tm=256):
    S = proj.shape[0]
    G, PG, _ = pw.shape
    DP = G * PG
    H, BLK, _ = ws.shape
    DS = gs_row.shape[1]
    HD = DS // H
    tm = _tile(S, tm, BLK)
    nb = S // tm
    ext = tm + POOL_HALO

    def body(a_ref, prev_ref, u_ref, v_ref, dzp_ref, dzp_next_ref, dzs_ref, pw_ref, ps_ref, gs_ref, ws_ref, b_ref,
             dproj_ref, dpw_ref, dps_ref, dgs_ref, dws_ref, db_ref, dvn_ref):
        i = pl.program_id(0)

        @pl.when(i == 0)
        def _():
            for r in (dpw_ref, dps_ref, dgs_ref, dws_ref, db_ref):
                r[...] = jnp.zeros_like(r)

        a = a_ref[...]
        halo = jnp.where(i > 0, prev_ref[...], 0.0)
        dyp = dzp_ref[...]
        dyp_next = jnp.where(i < nb - 1, dzp_next_ref[...], 0.0)
        dq_ext = jnp.concatenate([dyp, dyp_next], axis=0) * ps_ref[...]
        t_ext = lax.broadcasted_iota(jnp.int32, (ext, 1), 0) + i * tm
        for g in range(G):
            cols = slice(g * PG, (g + 1) * PG)
            w = POOL_WINDOWS[g]
            pb = _pool_inputs(a, halo, i * tm, tm, g, PG).astype(MXU_DTYPE)
            y_pre = jnp.dot(pb, pw_ref[g], preferred_element_type=jnp.float32)
            dps_ref[:, cols] += jnp.sum(dyp[:, cols] * y_pre, axis=0, keepdims=True)
            dqb = dq_ext[:, cols].astype(MXU_DTYPE)
            dpw_ref[g] += lax.dot_general(pb, dqb[:tm], TN, preferred_element_type=jnp.float32)
            dp = lax.dot_general(dqb, pw_ref[g], NT, preferred_element_type=jnp.float32)
            f = dp / jnp.minimum(t_ext + 1, w).astype(jnp.float32)
            span = 1
            while span < w:
                f = f + pltpu.roll(f, ext - span, axis=0)
                span *= 2
            dproj_ref[:, cols] = (f[:tm] - dp[:tm]).astype(dproj_ref.dtype)

        v = v_ref[...]
        rv = _rsqrt_mean_sq(v)
        vhat = v * rv
        gs = gs_ref[...]
        vnb = (vhat * gs).astype(MXU_DTYPE)
        u = u_ref[...]
        dys = dzs_ref[...]
        mask = _chunk_mask(BLK)
        for h in range(H):
            wm = jnp.where(mask, ws_ref[h], 0.0).astype(MXU_DTYPE)
            bias = b_ref[h]
            cols = slice(h * HD, (h + 1) * HD)
            dm_sum = jnp.zeros((BLK, HD), jnp.float32)
            dws_sum = jnp.zeros((BLK, BLK), jnp.float32)
            for n in range(tm // BLK):
                rows = slice(n * BLK, (n + 1) * BLK)
                vblk = vnb[rows, cols]
                mixed = jnp.dot(wm, vblk, preferred_element_type=jnp.float32) + bias
                dy = dys[rows, cols]
                dproj_ref[rows, DP + h * HD:DP + (h + 1) * HD] = (dy * mixed).astype(dproj_ref.dtype)
                dm = dy * u[rows, cols]
                dm_sum = dm_sum + dm
                dmb = dm.astype(MXU_DTYPE)
                dws_sum = dws_sum + lax.dot_general(dmb, vblk, NT, preferred_element_type=jnp.float32)
                dvn_ref[rows, cols] = lax.dot_general(wm, dmb, TN, preferred_element_type=jnp.float32)
            dws_ref[h] += jnp.where(mask, dws_sum, 0.0)
            db_ref[h:h + 1, :] += jnp.sum(dm_sum.T, axis=0, keepdims=True)
        dvn = dvn_ref[...]
        dvg = dvn * gs
        dv = rv * (dvg - vhat * jnp.mean(dvg * vhat, axis=-1, keepdims=True))
        dgs_ref[...] += jnp.sum(dvn * vhat, axis=0, keepdims=True)
        dproj_ref[:, DP + DS:] = dv.astype(dproj_ref.dtype)

    a_spec, prev_spec, next_spec, u_spec, v_spec = _mixer_specs(S, tm, DP, DS)
    dzp_spec = pl.BlockSpec((tm, DP), lambda i: (i, 0))
    dzs_spec = pl.BlockSpec((tm, DS), lambda i: (i, 1))
    small = [pw, ps_row, gs_row, ws, b_full]
    out_shape = [
        jax.ShapeDtypeStruct((S, DP + 2 * DS), MXU_DTYPE),
        jax.ShapeDtypeStruct(pw.shape, jnp.float32),
        jax.ShapeDtypeStruct(ps_row.shape, jnp.float32),
        jax.ShapeDtypeStruct(gs_row.shape, jnp.float32),
        jax.ShapeDtypeStruct(ws.shape, jnp.float32),
        jax.ShapeDtypeStruct((H, BLK), jnp.float32),
    ]
    out_specs = [pl.BlockSpec((tm, DP + 2 * DS), lambda i: (i, 0))] + [_whole(s) for s in out_shape[1:]]
    return pl.pallas_call(
        body, name="mixer_bwd", grid=(nb,),
        in_specs=[a_spec, prev_spec, u_spec, v_spec, dzp_spec, next_spec, dzs_spec] + [_whole(s) for s in small],
        out_specs=out_specs, out_shape=out_shape,
        scratch_shapes=[pltpu.VMEM((tm, DS), jnp.float32)],
        compiler_params=_params(1),
    )(proj, proj, proj, proj, dz, dz, dz, *small)


def _softmax_rows(q, k, scale):
    s = lax.dot_general(q, k, NT, preferred_element_type=jnp.float32) * scale
    e = jnp.exp(s - jnp.max(s, axis=-1, keepdims=True))
    return e / jnp.sum(e, axis=-1, keepdims=True)


def _attn_fwd(q, k, v, tm=512):
    S, D = q.shape
    M = k.shape[0]
    HD = D // N_XATTN_HEADS
    scale = HD ** -0.5
    tm = _tile(S, tm, 16)

    def body(q_ref, k_ref, v_ref, o_ref):
        for h in range(N_XATTN_HEADS):
            cols = slice(h * HD, (h + 1) * HD)
            p = _softmax_rows(q_ref[:, cols], k_ref[:, cols], scale)
            o = jnp.dot(p.astype(MXU_DTYPE), v_ref[:, cols], preferred_element_type=jnp.float32)
            o_ref[:, cols] = o.astype(o_ref.dtype)

    row = pl.BlockSpec((tm, D), lambda i: (i, 0))
    mem = pl.BlockSpec((M, D), lambda i: (0, 0))
    return pl.pallas_call(
        body, name="attn_fwd", grid=(S // tm,), in_specs=[row, mem, mem], out_specs=row,
        out_shape=jax.ShapeDtypeStruct((S, D), MXU_DTYPE), compiler_params=_params(1),
    )(q, k, v)


def _attn_bwd(q, k, v, do, tm=512):
    S, D = q.shape
    M = k.shape[0]
    HD = D // N_XATTN_HEADS
    scale = HD ** -0.5
    tm = _tile(S, tm, 16)

    def body(q_ref, k_ref, v_ref, do_ref, dq_ref, dk_ref, dv_ref):
        @pl.when(pl.program_id(0) == 0)
        def _():
            dk_ref[...] = jnp.zeros_like(dk_ref)
            dv_ref[...] = jnp.zeros_like(dv_ref)

        for h in range(N_XATTN_HEADS):
            cols = slice(h * HD, (h + 1) * HD)
            qh, kh, vh, doh = q_ref[:, cols], k_ref[:, cols], v_ref[:, cols], do_ref[:, cols]
            p = _softmax_rows(qh, kh, scale)
            dp = lax.dot_general(doh, vh, NT, preferred_element_type=jnp.float32)
            dv_ref[:, cols] += lax.dot_general(p.astype(MXU_DTYPE), doh, TN, preferred_element_type=jnp.float32)
            ds = (p * (dp - jnp.sum(dp * p, axis=-1, keepdims=True)) * scale).astype(MXU_DTYPE)
            dq_ref[:, cols] = jnp.dot(ds, kh, preferred_element_type=jnp.float32).astype(dq_ref.dtype)
            dk_ref[:, cols] += lax.dot_general(ds, qh, TN, preferred_element_type=jnp.float32)

    row = pl.BlockSpec((tm, D), lambda i: (i, 0))
    mem = pl.BlockSpec((M, D), lambda i: (0, 0))
    return pl.pallas_call(
        body, name="attn_bwd", grid=(S // tm,), in_specs=[row, mem, mem, row], out_specs=[row, mem, mem],
        out_shape=[jax.ShapeDtypeStruct((S, D), MXU_DTYPE), jax.ShapeDtypeStruct((M, D), jnp.float32),
                   jax.ShapeDtypeStruct((M, D), jnp.float32)],
        compiler_params=_params(1),
    )(q, k, v, do)


def _proj_residual_norm(name, a, w2d, resid, g_row, tm=256):
    M, K = a.shape
    N = w2d.shape[1]
    tm = _tile(M, tm, 16)

    def epilogue(accs, ex_refs, out_refs, i, j):
        x_new = ex_refs[0][...] + accs[0]
        out_refs[0][...] = x_new
        out_refs[1][...] = (x_new * _rsqrt_mean_sq(x_new) * ex_refs[1][...]).astype(MXU_DTYPE)

    row = lambda i, j, k: (i, 0)
    return _fused_matmul(
        name, (M // tm, 1, 1),
        [(a, (tm, K), row), (w2d, (K, N), lambda i, j, k: (0, 0))],
        [(0, 1, 0, NN)], [(tm, N)],
        [(resid, (tm, N), row), (g_row, (1, N), lambda i, j, k: (0, 0))],
        [((M, N), jnp.float32, (tm, N), row), ((M, N), MXU_DTYPE, (tm, N), row)], epilogue)


def _ffn_gate_up(h, wg3, wu3, tm=512):
    M, K = h.shape
    J, _, Nc = wg3.shape
    tm = _tile(M, tm, 16)

    def epilogue(accs, ex_refs, out_refs, i, j):
        gate, up = accs
        sg = _sigmoid(gate)
        silu = gate * sg
        out_refs[0][...] = (silu * up).astype(MXU_DTYPE)
        out_refs[1][...] = (up * (sg * (1.0 + gate * (1.0 - sg)))).astype(MXU_DTYPE)
        out_refs[2][...] = silu.astype(MXU_DTYPE)

    col = lambda i, j, k: (i, j)
    wspec = lambda i, j, k: (j, 0, 0)
    return _fused_matmul(
        "ffn_gate_up", (M // tm, J, 1),
        [(h, (tm, K), lambda i, j, k: (i, 0)), (wg3, (None, K, Nc), wspec), (wu3, (None, K, Nc), wspec)],
        [(0, 1, 0, NN), (0, 2, 1, NN)], [(tm, Nc)] * 2, [],
        [((M, J * Nc), MXU_DTYPE, (tm, Nc), col)] * 3, epilogue)


def _ffn_down(act, wd2d, x2, tm=512, tn=512):
    M, F = act.shape
    D = wd2d.shape[1]
    tm, tn = _tile(M, tm, 16), _tile(D, tn, LANES)

    def epilogue(accs, ex_refs, out_refs, i, j):
        out_refs[0][...] = ex_refs[0][...] + accs[0]

    tile = lambda i, j, k: (i, j)
    return _fused_matmul(
        "ffn_down", (M // tm, D // tn, 1),
        [(act, (tm, F), lambda i, j, k: (i, 0)), (wd2d, (F, tn), lambda i, j, k: (0, j))],
        [(0, 1, 0, NN)], [(tm, tn)], [(x2, (tm, tn), tile)],
        [((M, D), jnp.float32, (tm, tn), tile)], epilogue)[0]


def _final_norm_loss(x3, target, g_row, tm=256):
    S, D = x3.shape
    tm = _tile(S, tm, 16)

    def body(x_ref, t_ref, g_ref, dx_ref, dxl_ref, loss_ref, dg_ref):
        x3v = x_ref[...]
        g = g_ref[...]
        r = _rsqrt_mean_sq(x3v)
        xhat = x3v * r
        diff = xhat * g - t_ref[...]
        dy = diff / D
        dyg = dy * g
        dx = r * (dyg - xhat * jnp.mean(dyg * xhat, axis=-1, keepdims=True))
        dx_ref[...] = dx
        dxl_ref[...] = dx.astype(MXU_DTYPE)

        @pl.when(pl.program_id(0) == 0)
        def _():
            loss_ref[...] = jnp.zeros_like(loss_ref)
            dg_ref[...] = jnp.zeros_like(dg_ref)

        loss_ref[...] += 0.5 * _sum_all(jnp.mean(diff * diff, axis=-1, keepdims=True))
        dg_ref[...] += jnp.sum(dy * xhat, axis=0, keepdims=True)

    row = pl.BlockSpec((tm, D), lambda i: (i, 0))
    vec = pl.BlockSpec((1, D), lambda i: (0, 0))
    return pl.pallas_call(
        body, name="final_norm_loss", grid=(S // tm,), in_specs=[row, row, vec],
        out_specs=[row, row, pl.BlockSpec((1, 1), lambda i: (0, 0)), vec],
        out_shape=[jax.ShapeDtypeStruct((S, D), jnp.float32), jax.ShapeDtypeStruct((S, D), MXU_DTYPE),
                   jax.ShapeDtypeStruct((1, 1), jnp.float32), jax.ShapeDtypeStruct((1, D), jnp.float32)],
        compiler_params=_params(1),
    )(x3, target, g_row)


def _ffn_down_bwd(dx3_low, wd2d, act_by_gate, act_by_up, n_shards, tm=512):
    M, D = dx3_low.shape
    F = wd2d.shape[0]
    tn = F // n_shards
    tm = _tile(M, tm, 16)

    def epilogue(accs, ex_refs, out_refs, i, j):
        dact = accs[0]
        out_refs[0][...] = (dact * ex_refs[0][...].astype(jnp.float32)).astype(MXU_DTYPE)
        out_refs[1][...] = (dact * ex_refs[1][...].astype(jnp.float32)).astype(MXU_DTYPE)

    col = lambda i, j, k: (i, j)
    return _fused_matmul(
        "ffn_down_bwd", (M // tm, n_shards, 1),
        [(dx3_low, (tm, D), lambda i, j, k: (i, 0)), (wd2d, (tn, D), lambda i, j, k: (j, 0))],
        [(0, 1, 0, NT)], [(tm, tn)],
        [(act_by_gate, (tm, tn), col), (act_by_up, (tm, tn), col)],
        [((M, F), MXU_DTYPE, (tm, tn), col)] * 2, epilogue)


def _matmul_nt_norm_bwd(name, a, b, xin, g_row, dres, low_dtype, tm=256, after=None):
    M = a.shape[0]
    sharded = b.ndim == 3
    N, Kc = b.shape[-2:]
    J = b.shape[0] if sharded else 1
    tm = _tile(M, tm, 16)
    has_low = low_dtype is not None

    def epilogue(accs, ex_refs, out_refs, i, j):
        dx, dg = _rmsnorm_bwd_values(accs[0], ex_refs[0][...], ex_refs[1][...])
        dx = dx + ex_refs[2][...]
        out_refs[0][...] = dx
        if has_low:
            out_refs[1][...] = dx.astype(low_dtype)
        dg_ref = out_refs[-1]

        @pl.when(i == 0)
        def _():
            dg_ref[...] = jnp.zeros_like(dg_ref)

        dg_ref[...] += dg

    row = lambda i, j, k: (i, 0)
    const = lambda i, j, k: (0, 0)
    b_operand = (b, (None, N, Kc), lambda i, j, k: (k, 0, 0)) if sharded else (b, (N, Kc), const)
    outs = [((M, N), jnp.float32, (tm, N), row)]
    if has_low:
        outs.append(((M, N), low_dtype, (tm, N), row))
    outs.append(((1, N), jnp.float32, (1, N), const))
    return _fused_matmul(
        name, (M // tm, 1, J), [(a, (tm, Kc), lambda i, j, k: (i, k)), b_operand],
        [(0, 1, 0, NT)], [(tm, N)],
        [(xin, (tm, N), row), (g_row, (1, N), const), (dres, (tm, N), row)] + _order_after(after),
        outs, epilogue)


def _row_tile(R, C, target_bytes=2 ** 21):
    return _tile(R, max(16, target_bytes // (4 * C)), 16)


def _prefetch_call(body, name, grid, pos, in_specs, out_specs, out_shape, ins, after=None):
    ordered = [] if after is None else [after]
    grid_spec = pltpu.PrefetchScalarGridSpec(
        num_scalar_prefetch=1, grid=grid, in_specs=list(in_specs) + [ANY] * len(ordered), out_specs=out_specs)
    return pl.pallas_call(body, name=name, grid_spec=grid_spec, out_shape=out_shape,
                          compiler_params=_params(len(grid)))(pos, *ins, *ordered)


def _runs(arrays):
    runs = []
    for i, a in enumerate(arrays):
        if runs and arrays[runs[-1][-1]].shape == a.shape:
            runs[-1].append(i)
        else:
            runs.append([i])
    return runs


def _per_run(fn, names, *array_lists):
    out = [None] * len(names)
    for run in _runs(array_lists[0]):
        results = fn("_".join(names[i] for i in run), *[[arrays[i] for i in run] for arrays in array_lists])
        for i, r in zip(run, results):
            out[i] = r
    return out


def _cast_into_slabs(name, xs, dtype, pos, after=None):
    n = len(xs)
    R, C = xs[0].shape
    tr = _row_tile(R, C * n)

    def body(pos_ref, *refs):
        for t in range(n):
            refs[len(refs) - n + t][...] = refs[t][...].astype(dtype)

    return _prefetch_call(
        body, "cast_" + name, (R // tr,), pos, [pl.BlockSpec((tr, C), lambda i, pos: (i, 0))] * n,
        [pl.BlockSpec((None, tr, C), lambda i, pos: (pos[0], i, 0))] * n,
        [jax.ShapeDtypeStruct((N_CHIPS, R, C), dtype)] * n, xs, after)


def _chip_sum(name, partials, gots, dtype, pos):
    n = len(partials)
    J, R, C = partials[0].shape
    Rh = R // 2
    tr = _row_tile(Rh, C * n)

    def body(pos_ref, *refs):
        for t in range(n):
            refs[2 * n + t][...] = (refs[t][...].astype(jnp.float32) + refs[n + t][...].astype(jnp.float32)).astype(dtype)

    blk = pl.BlockSpec((None, tr, C), lambda j, i, pos: (j, i, 0))
    return _prefetch_call(
        body, "chip_sum_" + name, (J, Rh // tr), pos,
        [pl.BlockSpec((None, None, tr, C), lambda j, i, pos: (j, pos[1], i, 0))] * n + [blk] * n, [blk] * n,
        [jax.ShapeDtypeStruct((J, Rh, C), dtype)] * n, [p.reshape(J, 2, Rh, C) for p in partials] + list(gots))


def _owner_sum(name, chip_sums, others, pos):
    n = len(chip_sums)
    J, Rh, C = chip_sums[0].shape
    n_other = others[0].shape[0]
    tr = _row_tile(Rh, C * (n_other + 1) * n)

    def body(pos_ref, *refs):
        for t in range(n):
            acc = refs[t][...].astype(jnp.float32)
            for k in range(n_other):
                acc = acc + refs[n + t][k].astype(jnp.float32)
            refs[2 * n + t][...] = acc

    return _prefetch_call(
        body, "owner_sum_" + name, (Rh // tr,), pos,
        [pl.BlockSpec((None, tr, C), lambda i, pos: (pos[0], i, 0))] * n
        + [pl.BlockSpec((n_other, tr, C), lambda i, pos: (0, i, 0))] * n,
        [pl.BlockSpec((None, tr, C), lambda i, pos: (pos[1], i, 0))] * n,
        [jax.ShapeDtypeStruct((2, Rh, C), jnp.float32)] * n, list(chip_sums) + list(others))


def _adamw_values(w, g, m, v):
    m = ADAM_B1 * m + (1.0 - ADAM_B1) * g
    v = ADAM_B2 * v + (1.0 - ADAM_B2) * (g * g)
    m_hat = m / (1.0 - ADAM_B1 ** ADAM_STEP)
    v_hat = v / (1.0 - ADAM_B2 ** ADAM_STEP)
    delta = -ADAM_LR * (m_hat / (jnp.sqrt(v_hat) + ADAM_EPS) + ADAM_WD * w)
    return delta, m, v


def _adamw(name, ws, gs, ms, vs):
    n = len(ws)
    R, C = ws[0].shape
    tr = _row_tile(R, C * n, 2 ** 20)

    def body(*refs):
        w_refs, g_refs, m_refs, v_refs = (refs[q * n:(q + 1) * n] for q in range(4))
        outs = refs[4 * n:]
        for t in range(n):
            g = g_refs[t][...]
            outs[4 * t][...] = g
            outs[4 * t + 1][...], outs[4 * t + 2][...], outs[4 * t + 3][...] = _adamw_values(
                w_refs[t][...], g, m_refs[t][...], v_refs[t][...])

    blk = pl.BlockSpec((tr, C), lambda i: (i, 0))
    res = pl.pallas_call(
        body, name="adamw_" + name, grid=(R // tr,), in_specs=[blk] * (4 * n), out_specs=[blk] * (4 * n),
        out_shape=[jax.ShapeDtypeStruct((R, C), jnp.float32)] * (4 * n), compiler_params=_params(1),
    )(*ws, *gs, *ms, *vs)
    return [tuple(res[4 * t:4 * t + 4]) for t in range(n)]


def _adamw_small(g_packed, ws, ms, vs, row_counts):
    n = len(ws)

    def body(*refs):
        g_ref = refs[0]
        w_refs, m_refs, v_refs = refs[1:1 + n], refs[1 + n:1 + 2 * n], refs[1 + 2 * n:1 + 3 * n]
        outs = refs[1 + 3 * n:]
        off = 0
        for t in range(n):
            g = g_ref[off:off + row_counts[t], :]
            off += row_counts[t]
            delta, m_new, v_new = _adamw_values(w_refs[t][...], g, m_refs[t][...], v_refs[t][...])
            outs[4 * t][...] = g
            outs[4 * t + 1][...] = delta
            outs[4 * t + 2][...] = m_new
            outs[4 * t + 3][...] = v_new

    out_shape = []
    for t in range(n):
        out_shape += [jax.ShapeDtypeStruct((row_counts[t], LANES), jnp.float32)] * 4
    return pl.pallas_call(
        body, name="adamw_small", in_specs=[VMEM_WHOLE] * (1 + 3 * n), out_specs=[VMEM_WHOLE] * (4 * n),
        out_shape=out_shape, compiler_params=pltpu.CompilerParams(vmem_limit_bytes=VMEM_LIMIT_BYTES),
    )(g_packed, *ws, *ms, *vs)


def _position():
    x, y, c = lax.axis_index("x"), lax.axis_index("y"), lax.axis_index("c")
    other_chips = [(1 - x, y), (x, 1 - y), (1 - x, 1 - y)]
    return x, y, c, other_chips


def _half(rows, which):
    return pl.ds(which * (rows // 2), rows // 2)


def _chunk_index(shape, dtype):
    rows = shape[-2]
    total = math.prod(shape) * jnp.dtype(dtype).itemsize
    n = max(1, min(rows // COPY_ROW_ALIGN, total // COPY_CHUNK_BYTES))
    step = -(-rows // n)
    step = -(-step // COPY_ROW_ALIGN) * COPY_ROW_ALIGN
    lead = (slice(None),) * (len(shape) - 2)
    return [lead + (pl.ds(s, min(step, rows - s)),) for s in range(0, rows, step)]


def _start_in_chunks(make, src, dst, waited_here=True):
    for idx in _chunk_index(src.shape, src.dtype):
        make(src.at[idx], dst.at[idx]).start()
    return make(src, dst) if waited_here else None


def _remote_copier(send_sem, recv_sem, to):
    return lambda src, dst: pltpu.make_async_remote_copy(
        src_ref=src, dst_ref=dst, send_sem=send_sem, recv_sem=recv_sem, device_id=to, device_id_type=MESH)


def _slab_half(ref, chip, which):
    return ref.at[chip, _half(ref.shape[1], which)]


def _gather_start(name, groups, after=None):
    flat = [s for g in groups for s in g]
    T, G = len(flat), len(groups)
    ordered = [] if after is None else [after]

    def body(*refs):
        ins = refs[:T]
        sems = refs[T + len(ordered):T + len(ordered) + 2 * G]
        token = refs[-1]
        x, y, c, other_chips = _position()
        mine = 2 * x + y
        t0 = 0
        for gi, group in enumerate(groups):
            for t in range(len(group)):
                own = _slab_half(ins[t0 + t], mine, c)
                for k, (cx, cy) in enumerate(other_chips):
                    _start_in_chunks(_remote_copier(sems[2 * gi].at[3 * t + k], sems[2 * gi + 1].at[3 * t + k], (cx, cy, c)),
                                     own, own, waited_here=False)
            t0 += len(group)
        token[...] = jnp.zeros_like(token)

    sem_shapes = []
    for group in groups:
        sem_shapes += [pltpu.SemaphoreType.DMA((3 * len(group),))] * 2
    res = pl.pallas_call(
        body, name=name, in_specs=[HBM_SPEC] * T + [ANY] * len(ordered),
        out_specs=[SEM_SPEC] * (2 * G) + [HBM_SPEC] * T + [VMEM_WHOLE],
        out_shape=sem_shapes + [pltpu.HBM(s.shape, s.dtype) for s in flat] + [jax.ShapeDtypeStruct((8, LANES), jnp.float32)],
        input_output_aliases={t: 2 * G + t for t in range(T)},
        compiler_params=pltpu.CompilerParams(has_side_effects=SIDE_EFFECT),
    )(*[pltpu.with_memory_space_constraint(s, pltpu.HBM) for s in flat], *ordered)
    flights, t0 = [], 2 * G
    for gi, group in enumerate(groups):
        flights.append((res[2 * gi], res[2 * gi + 1], res[t0:t0 + len(group)]))
        t0 += len(group)
    return flights, res[-1]


def _gather_wait(name, flight, after):
    send_sems, recv_sems, slabs = flight
    T = len(slabs)

    def body(*refs):
        ins = refs[:T]
        send, recv = refs[T:T + 2]
        x, y, c, other_chips = _position()
        mine = 2 * x + y
        for t in range(T):
            own = _slab_half(ins[t], mine, c)
            for k, (cx, cy) in enumerate(other_chips):
                landed = _slab_half(ins[t], 2 * cx + cy, c)
                cp = _remote_copier(send.at[3 * t + k], recv.at[3 * t + k], (cx, cy, c))(own, landed)
                cp.wait_send()
                cp.wait_recv()

    return pl.pallas_call(
        body, name=name, in_specs=[HBM_SPEC] * T + [SEM_SPEC, SEM_SPEC, ANY], out_specs=[HBM_SPEC] * T,
        out_shape=[pltpu.HBM(s.shape, s.dtype) for s in slabs], input_output_aliases={t: t for t in range(T)},
        compiler_params=pltpu.CompilerParams(has_side_effects=SIDE_EFFECT),
    )(*slabs, send_sems, recv_sems, after)


def _share_with_sibling(name, slabs):
    T = len(slabs)

    def body(*refs):
        outs = refs[T:2 * T]
        send_sems, recv_sems = refs[2 * T:]
        x, y, c, other_chips = _position()
        sibling = (x, y, 1 - c)
        started = []
        for t in range(T):
            for k, (cx, cy) in enumerate(other_chips):
                landed = _slab_half(outs[t], 2 * cx + cy, c)
                started.append(_start_in_chunks(_remote_copier(send_sems.at[t, k], recv_sems.at[t, k], sibling), landed, landed))
        for t in range(T):
            for k, (cx, cy) in enumerate(other_chips):
                arriving = _slab_half(outs[t], 2 * cx + cy, 1 - c)
                _remote_copier(send_sems.at[t, k], recv_sems.at[t, k], sibling)(arriving, arriving).wait_recv()
        for cp in started:
            cp.wait_send()

    return pl.pallas_call(
        body, name=name, in_specs=[ANY] * T, out_specs=[ANY] * T,
        out_shape=[jax.ShapeDtypeStruct(s.shape, s.dtype) for s in slabs],
        input_output_aliases={t: t for t in range(T)},
        scratch_shapes=[pltpu.SemaphoreType.DMA((T, 3)), pltpu.SemaphoreType.DMA((T, 3))],
    )(*slabs)


def _swap_halves(name, grads):
    T = len(grads)

    def body(*refs):
        ins, got = refs[:T], refs[T:2 * T]
        send_sems, recv_sems = refs[2 * T:]
        x, y, c, _ = _position()
        remote = []
        for t in range(T):
            rows = ins[t].shape[1]
            remote.append(_start_in_chunks(
                _remote_copier(send_sems.at[t], recv_sems.at[t], (x, y, 1 - c)), ins[t].at[:, _half(rows, 1 - c)], got[t]))
        for cp in remote:
            cp.wait()

    return pl.pallas_call(
        body, name=name, in_specs=[ANY] * T, out_specs=[ANY] * T,
        out_shape=[jax.ShapeDtypeStruct((g.shape[0], g.shape[1] // 2, g.shape[2]), g.dtype) for g in grads],
        scratch_shapes=[pltpu.SemaphoreType.DMA((T,)), pltpu.SemaphoreType.DMA((T,))],
    )(*grads)


def _swap_start(name, grads):
    T = len(grads)
    lands = [lax.empty((g.shape[0], g.shape[1] // 2, g.shape[2]), g.dtype) for g in grads]

    def body(*refs):
        ins, got = refs[:T], refs[T:2 * T]
        send_sems, recv_sems = refs[2 * T:2 * T + 2]
        token = refs[-1]
        x, y, c, _ = _position()
        for t in range(T):
            _start_in_chunks(_remote_copier(send_sems.at[t], recv_sems.at[t], (x, y, 1 - c)),
                             ins[t].at[:, _half(ins[t].shape[1], 1 - c)], got[t], waited_here=False)
        token[...] = jnp.zeros_like(token)

    arrays = list(grads) + lands
    res = pl.pallas_call(
        body, name=name, in_specs=[HBM_SPEC] * (2 * T),
        out_specs=[SEM_SPEC, SEM_SPEC] + [HBM_SPEC] * (2 * T) + [VMEM_WHOLE],
        out_shape=[pltpu.SemaphoreType.DMA((T,)), pltpu.SemaphoreType.DMA((T,))]
        + [pltpu.HBM(a.shape, a.dtype) for a in arrays] + [jax.ShapeDtypeStruct((8, LANES), jnp.float32)],
        input_output_aliases={i: 2 + i for i in range(2 * T)},
        compiler_params=pltpu.CompilerParams(has_side_effects=SIDE_EFFECT),
    )(*[pltpu.with_memory_space_constraint(a, pltpu.HBM) for a in arrays])
    return res[0], res[1], res[2:2 + T], res[2 + T:2 + 2 * T], res[-1]


def _swap_wait(name, flight, after):
    send_sems, recv_sems, grads, lands, _ = flight
    T = len(grads)

    def body(*refs):
        ins, got = refs[:T], refs[T:2 * T]
        send, recv = refs[2 * T:2 * T + 2]
        x, y, c, _ = _position()
        for t in range(T):
            cp = _remote_copier(send.at[t], recv.at[t], (x, y, 1 - c))(ins[t].at[:, _half(ins[t].shape[1], 1 - c)], got[t])
            cp.wait_send()
            cp.wait_recv()

    arrays = list(grads) + list(lands)
    res = pl.pallas_call(
        body, name=name, in_specs=[HBM_SPEC] * (2 * T) + [SEM_SPEC, SEM_SPEC, ANY], out_specs=[HBM_SPEC] * (2 * T),
        out_shape=[pltpu.HBM(a.shape, a.dtype) for a in arrays], input_output_aliases={i: i for i in range(2 * T)},
        compiler_params=pltpu.CompilerParams(has_side_effects=SIDE_EFFECT),
    )(*arrays, send_sems, recv_sems, after)
    return res[:T], res[T:]


def _scatter_start(name, chip_sums):
    T = len(chip_sums)
    lands = [lax.empty((3,) + s.shape[1:], s.dtype) for s in chip_sums]

    def body(*refs):
        srcs, dsts = refs[:T], refs[T:2 * T]
        send_sems, recv_sems = refs[2 * T:2 * T + 2]
        token = refs[-1]
        x, y, c, other_chips = _position()
        for t in range(T):
            for k, (cx, cy) in enumerate(other_chips):
                _start_in_chunks(_remote_copier(send_sems.at[3 * t + k], recv_sems.at[3 * t + k], (cx, cy, c)),
                                 srcs[t].at[2 * cx + cy], dsts[t].at[k], waited_here=False)
        token[...] = jnp.zeros_like(token)

    hbm = [pltpu.HBM(a.shape, a.dtype) for a in list(chip_sums) + lands]
    res = pl.pallas_call(
        body, name=name, in_specs=[HBM_SPEC] * (2 * T),
        out_specs=[SEM_SPEC, SEM_SPEC] + [HBM_SPEC] * (2 * T) + [VMEM_WHOLE],
        out_shape=[pltpu.SemaphoreType.DMA((3 * T,)), pltpu.SemaphoreType.DMA((3 * T,))] + hbm
        + [jax.ShapeDtypeStruct((8, LANES), jnp.float32)],
        input_output_aliases={i: 2 + i for i in range(2 * T)},
        compiler_params=pltpu.CompilerParams(has_side_effects=SIDE_EFFECT),
    )(*[pltpu.with_memory_space_constraint(a, pltpu.HBM) for a in list(chip_sums) + lands])
    return res[0], res[1], res[2:2 + T], res[2 + T:2 + 2 * T], res[-1]


def _scatter_wait(name, send_sems, recv_sems, chip_sums, lands, after):
    T = len(chip_sums)

    def body(*refs):
        srcs, dsts = refs[:T], refs[T:2 * T]
        send, recv = refs[2 * T:2 * T + 2]
        x, y, c, other_chips = _position()
        for t in range(T):
            for k, (cx, cy) in enumerate(other_chips):
                cp = _remote_copier(send.at[3 * t + k], recv.at[3 * t + k], (cx, cy, c))(srcs[t].at[2 * cx + cy], dsts[t].at[k])
                cp.wait_send()
                cp.wait_recv()

    arrays = list(chip_sums) + list(lands)
    res = pl.pallas_call(
        body, name=name, in_specs=[HBM_SPEC] * (2 * T) + [SEM_SPEC, SEM_SPEC, ANY],
        out_specs=[HBM_SPEC] * (2 * T), out_shape=[pltpu.HBM(a.shape, a.dtype) for a in arrays],
        input_output_aliases={i: i for i in range(2 * T)},
        compiler_params=pltpu.CompilerParams(has_side_effects=SIDE_EFFECT),
    )(*arrays, send_sems, recv_sems, after)
    return res[:T], res[T:]


def _as_shards(partial):
    return [p.reshape((N_CHIPS, -1, p.shape[-1])) for p in partial]


def _reduce_begin(tag, names, partial, pos):
    parts = _as_shards(partial)
    received = _swap_halves("swap_grad_halves_" + tag, parts)
    return _reduce_scatter_start(tag, names, parts, received, pos)


def _reduce_scatter_start(tag, names, parts, received, pos):
    chip_sums = _per_run(lambda name, p, r: _chip_sum(name, p, r, WIRE_DTYPE, pos), names, list(parts), list(received))
    return _scatter_start("scatter_start_" + tag, chip_sums)


def _reduce_end(tag, names, flight, after, pos):
    send_sems, recv_sems, chip_sums, lands, _ = flight
    chip_sums, others = _scatter_wait("scatter_wait_" + tag, send_sems, recv_sems, chip_sums, lands, after)
    reduced = _per_run(lambda name, cs, ot: _owner_sum(name, cs, ot, pos), names, list(chip_sums), list(others))
    return dict(zip(names, reduced))


def _join_halves(halves):
    T = len(halves)

    def body(*refs):
        outs = refs[T:2 * T]
        send_sems, recv_sems = refs[2 * T:]
        x, y, c, _ = _position()

        def copier(t):
            return _remote_copier(send_sems.at[t], recv_sems.at[t], (x, y, 1 - c))

        remote = [_start_in_chunks(copier(t), outs[t].at[c], outs[t].at[c]) for t in range(T)]
        for t in range(T):
            copier(t)(outs[t].at[1 - c], outs[t].at[1 - c]).wait_recv()
        for cp in remote:
            cp.wait_send()

    return pl.pallas_call(
        body, name="join_grad_halves", in_specs=[ANY] * T, out_specs=[ANY] * T,
        out_shape=[jax.ShapeDtypeStruct(h.shape, h.dtype) for h in halves],
        input_output_aliases={t: t for t in range(T)},
        scratch_shapes=[pltpu.SemaphoreType.DMA((T,)), pltpu.SemaphoreType.DMA((T,))],
    )(*halves)


def _allreduce_small(part):
    rows = part.shape[0]

    def body(x_ref, out_ref, gath, send_sems, recv_sems):
        x, y, c, other_chips = _position()
        sibling = (x, y, 1 - c)

        def slot(px, py, pc):
            return gath.at[4 * px + 2 * py + pc]

        def copy(k, owner, to, src=None):
            return pltpu.make_async_remote_copy(
                src_ref=slot(*owner) if src is None else src, dst_ref=slot(*owner), send_sem=send_sems.at[k],
                recv_sem=recv_sems.at[k], device_id=to, device_id_type=MESH)

        gath[4 * x + 2 * y + c] = x_ref[...]
        first = [copy(0, (x, y, c), sibling, src=x_ref)]
        first += [copy(1 + k, (x, y, c), (cx, cy, c), src=x_ref) for k, (cx, cy) in enumerate(other_chips)]
        for cp in first:
            cp.start()
        passed = [copy(4 + k, (cx, cy, c), sibling) for k, (cx, cy) in enumerate(other_chips)]
        for k, (cx, cy) in enumerate(other_chips):
            copy(1 + k, (cx, cy, c), (x, y, c)).wait_recv()
            passed[k].start()
        copy(0, (x, y, 1 - c), (x, y, c)).wait_recv()
        for k, (cx, cy) in enumerate(other_chips):
            copy(4 + k, (cx, cy, 1 - c), (x, y, c)).wait_recv()
        for cp in first + passed:
            cp.wait_send()
        acc = gath[0]
        for d in range(1, N_DEV):
            acc = acc + gath[d]
        out_ref[...] = acc

    return pl.pallas_call(
        body, name="allreduce_small_grads", in_specs=[VMEM_WHOLE], out_specs=VMEM_WHOLE,
        out_shape=jax.ShapeDtypeStruct((rows, LANES), jnp.float32),
        scratch_shapes=[pltpu.VMEM((N_DEV, rows, LANES), jnp.float32), pltpu.SemaphoreType.DMA((7,)),
                        pltpu.SemaphoreType.DMA((7,))],
        compiler_params=pltpu.CompilerParams(vmem_limit_bytes=VMEM_LIMIT_BYTES),
    )(part)


BIG = ("w_in", "pool_w", "w_out", "w_q", "w_k", "w_v", "w_o", "w_gate", "w_up", "w_down")
SMALL = ("norm_mix_g", "pool_scale", "sgu_norm_g", "w_spatial", "b_spatial", "norm_xattn_g", "norm_mem_g",
         "norm_ffn_g", "final_norm_g")
GATHER_ORDER = (("w_in", "pool_w"), ("w_out",), ("w_q", "w_k", "w_v", "w_o"), ("w_gate", "w_up"), ("w_down",))
WEIGHTS = ("norm_mix_g", "w_in", "pool_w", "pool_scale", "sgu_norm_g", "w_spatial", "b_spatial", "w_out",
           "norm_xattn_g", "norm_mem_g", "w_q", "w_k", "w_v", "w_o", "norm_ffn_g", "w_gate", "w_up", "w_down",
           "final_norm_g")


def _as2d(a):
    return a.reshape(-1, a.shape[-1])


def _as_lanes(a):
    return a.reshape(-1, LANES)


def kernel(x, mem, norm_mix_g, w_in, pool_w, pool_scale, sgu_norm_g, w_spatial, b_spatial, w_out, norm_xattn_g, norm_mem_g, w_q, w_k, w_v, w_o, norm_ffn_g, w_gate, w_up, w_down, final_norm_g, loss_target, m_norm_mix_g, m_w_in, m_pool_w, m_pool_scale, m_sgu_norm_g, m_w_spatial, m_b_spatial, m_w_out, m_norm_xattn_g, m_norm_mem_g, m_w_q, m_w_k, m_w_v, m_w_o, m_norm_ffn_g, m_w_gate, m_w_up, m_w_down, m_final_norm_g, v_norm_mix_g, v_w_in, v_pool_w, v_pool_scale, v_sgu_norm_g, v_w_spatial, v_b_spatial, v_w_out, v_norm_xattn_g, v_norm_mem_g, v_w_q, v_w_k, v_w_v, v_w_o, v_norm_ffn_g, v_w_gate, v_w_up, v_w_down, v_final_norm_g):
    args = dict(locals())
    weights = {n: args[n] for n in WEIGHTS}
    moments_m = {n: args["m_" + n] for n in WEIGHTS}
    moments_v = {n: args["v_" + n] for n in WEIGHTS}
    low = MXU_DTYPE

    xs = x[0]
    mems = mem[0]
    target = loss_target[0]
    D = xs.shape[1]
    G, PG = pool_w.shape[1], pool_w.shape[3]

    pos = jnp.stack([2 * lax.axis_index("x") + lax.axis_index("y"), lax.axis_index("c")]).astype(jnp.int32)
    def cast(names, after=None):
        slabs = _per_run(lambda name, xs: _cast_into_slabs(name, xs, WIRE_DTYPE, pos, after), names,
                         [_as2d(weights[n]) for n in names])
        return dict(zip(names, slabs))

    first = cast(GATHER_ORDER[0])
    flights, started = _gather_start("gather_start_first", [[first[n] for n in GATHER_ORDER[0]]])
    rest = cast([n for group in GATHER_ORDER[1:] for n in group], after=started)
    more, started = _gather_start("gather_start_rest", [[rest[n] for n in group] for group in GATHER_ORDER[1:]], after=started)
    flights = flights + more

    def arrive(gi, after):
        landed = _gather_wait("gather_wait_%d" % gi, flights[gi], after)
        return dict(zip(GATHER_ORDER[gi], _share_with_sibling("gather_share_%d" % gi, landed)))

    def rows(a):
        return a.reshape(-1, a.shape[-1])

    g_mix, g_xattn, g_mem, g_ffn = norm_mix_g, norm_xattn_g, norm_mem_g, norm_ffn_g
    g_final = final_norm_g.reshape(1, D)
    ws = w_spatial[0]
    b_full = jnp.broadcast_to(b_spatial[0][:, :, None], ws.shape[:2] + (sgu_norm_g.shape[1] // ws.shape[0],))

    h1 = _rmsnorm("norm_mix", xs, g_mix, low, after=started)
    mn = _rmsnorm("norm_mem", mems, g_mem, low)
    got = arrive(0, mn)
    wg_in = got["w_in"]
    pw = got["pool_w"].reshape(N_CHIPS, G, PG // N_CHIPS, PG).transpose(1, 0, 2, 3).reshape(G, PG, PG)
    proj = _matmul_nn_cols("proj_in", h1, wg_in, jnp.float32)
    z = _mixer_fwd(proj, pw, pool_scale, sgu_norm_g, ws, b_full)
    full = {"w_out": rows(arrive(1, z)["w_out"])}
    x1, h2 = _proj_residual_norm("mix_out", z, full["w_out"], xs, g_xattn)
    full.update({n: rows(a) for n, a in arrive(2, h2).items()})
    q = _matmul_nn("xattn_q", h2, full["w_q"], low)
    k = _matmul_nn("xattn_k", mn, full["w_k"], low)
    vv = _matmul_nn("xattn_v", mn, full["w_v"], low)
    o = _attn_fwd(q, k, vv)
    x2, h3 = _proj_residual_norm("xattn_out", o, full["w_o"], x1, g_ffn)
    got = arrive(3, h3)
    wg_gate, wg_up = got["w_gate"], got["w_up"]
    act, act_by_gate, act_by_up = _ffn_gate_up(h3, wg_gate, wg_up)
    full["w_down"] = rows(arrive(4, act)["w_down"])
    x3 = _ffn_down(act, full["w_down"], x2)
    dx3, dx3_low, loss_part, d_final_g = _final_norm_loss(x3, target, g_final)

    dgate, dup = _ffn_down_bwd(dx3_low, full["w_down"], act_by_gate, act_by_up, N_CHIPS)
    (d_w_down,) = _matmul_tn("grad_w_down", act, [dx3_low], WIRE_DTYPE)
    (d_w_gate,) = _matmul_tn_cols("grad_w_gate", h3, [dgate], N_CHIPS, WIRE_DTYPE)
    (d_w_up,) = _matmul_tn_cols("grad_w_up", h3, [dup], N_CHIPS, WIRE_DTYPE)
    ffn_names = ("w_gate", "w_up", "w_down")
    ffn_swap = _swap_start("swap_start_ffn", _as_shards([d_w_gate, d_w_up, d_w_down]))
    dh3 = _matmul_nt_cols("ffn_dh", [dgate, dup], [wg_gate, wg_up], jnp.float32, after=ffn_swap[-1])
    ffn_flight = _reduce_scatter_start("ffn", ffn_names, *_swap_wait("swap_wait_ffn", ffn_swap, dh3), pos)
    dx2, dx2_low, d_ffn_g = _rmsnorm_bwd("norm_ffn_bwd", dh3, x2, g_ffn, dx3, low)

    do = _matmul_nt("xattn_do", [dx2_low], [full["w_o"]], low, after=ffn_flight[-1])
    (d_w_o,) = _matmul_tn("grad_w_o", o, [dx2_low], WIRE_DTYPE)
    dq, dk, dvv = _attn_bwd(q, k, vv, do)
    (d_w_q,) = _matmul_tn("grad_w_q", h2, [dq], WIRE_DTYPE)
    dx1, dx1_low, d_xattn_g = _matmul_nt_norm_bwd("xattn_dh_norm_bwd", dq, full["w_q"], x1, g_xattn, dx2, low)
    d_w_k, d_w_v = _matmul_tn("grad_w_kv", mn, [dk, dvv], WIRE_DTYPE)
    dmn = _matmul_nt("xattn_dmem", [dk, dvv], [full["w_k"], full["w_v"]], jnp.float32)
    (d_mem_g,) = _rmsnorm_bwd("norm_mem_bwd", dmn, mems, g_mem, None, None)

    (d_w_out,) = _matmul_tn("grad_w_out", z, [dx1_low], WIRE_DTYPE)
    attn_names = ("w_q", "w_k", "w_v", "w_o", "w_out")
    attn_swap = _swap_start("swap_start_attn", _as_shards([d_w_q, d_w_k, d_w_v, d_w_o, d_w_out]))
    dz = _matmul_nt("mix_dz", [dx1_low], [full["w_out"]], jnp.float32, after=attn_swap[-1])
    attn_flight = _reduce_scatter_start("attn", attn_names, *_swap_wait("swap_wait_attn", attn_swap, dz), pos)
    dproj, d_pw, d_ps, d_gs, d_ws, d_b = _mixer_bwd(proj, dz, pw, pool_scale, sgu_norm_g, ws, b_full)
    (d_w_in,) = _matmul_tn_cols("grad_w_in", h1, [dproj], N_CHIPS, WIRE_DTYPE, tka=1024, after=attn_flight[-1])
    d_pool_w = d_pw.reshape(G, N_CHIPS, PG // N_CHIPS, PG).transpose(1, 0, 2, 3).reshape(N_CHIPS, PG, PG).astype(WIRE_DTYPE)
    mix_names = ("w_in", "pool_w")
    mix_flight = _reduce_begin("mix", mix_names, [d_w_in, d_pool_w], pos)
    dh1 = _matmul_nt_cols("mix_dh", [dproj], [wg_in], jnp.float32, tn=1024, after=mix_flight[-1])
    grad_x, d_mix_g = _rmsnorm_bwd("norm_mix_bwd", dh1, xs, g_mix, dx1, None)

    reduced = _reduce_end("ffn", ffn_names, ffn_flight, grad_x, pos)
    reduced.update(_reduce_end("attn", attn_names, attn_flight, grad_x, pos))
    reduced.update(_reduce_end("mix", mix_names, mix_flight, grad_x, pos))
    grads = {n: g.reshape(-1, g.shape[-1]) for n, g in zip(BIG, _join_halves([reduced[n] for n in BIG]))}

    small_parts = {
        "norm_mix_g": d_mix_g, "pool_scale": d_ps, "sgu_norm_g": d_gs, "w_spatial": d_ws, "b_spatial": d_b,
        "norm_xattn_g": d_xattn_g, "norm_mem_g": d_mem_g, "norm_ffn_g": d_ffn_g, "final_norm_g": d_final_g,
    }
    row_counts = [small_parts[n].size // LANES for n in SMALL]
    packed = jnp.concatenate([_as_lanes(small_parts[n]) for n in SMALL], axis=0)
    small_sum = _allreduce_small(packed)

    out_grad, out_delta, out_m, out_v = {}, {}, {}, {}
    updates = _per_run(_adamw, BIG, [_as2d(weights[n]) for n in BIG], [grads[n] for n in BIG],
                       [_as2d(moments_m[n]) for n in BIG], [_as2d(moments_v[n]) for n in BIG])
    for n, update in zip(BIG, updates):
        shape = weights[n].shape
        out_grad[n], out_delta[n], out_m[n], out_v[n] = (a.reshape(shape) for a in update)
    small_out = _adamw_small(
        small_sum, [_as_lanes(weights[n]) for n in SMALL], [_as_lanes(moments_m[n]) for n in SMALL],
        [_as_lanes(moments_v[n]) for n in SMALL], row_counts)
    for t, n in enumerate(SMALL):
        shape = weights[n].shape
        out_grad[n], out_delta[n], out_m[n], out_v[n] = (a.reshape(shape) for a in small_out[4 * t:4 * t + 4])

    loss = lax.psum(loss_part[0, 0], ("x", "y", "c"))
    return (loss, grad_x[None], *[out_grad[n] for n in WEIGHTS], *[out_delta[n] for n in WEIGHTS],
            *[out_m[n] for n in WEIGHTS], *[out_v[n] for n in WEIGHTS])
```

```python
import math

import jax
import jax.numpy as jnp
from jax import lax
from jax.experimental import pallas as pl
from jax.experimental.pallas import tpu as pltpu

MXU_DTYPE = jnp.bfloat16
WIRE_DTYPE = jnp.bfloat16

EPS = 1e-6
CHUNK = 64
POOL_WINDOWS = (2, 4, 8, 16)
POOL_HALO = 16
N_XATTN_HEADS = 4
ADAM_LR = 0.001
ADAM_B1 = 0.9
ADAM_B2 = 0.999
ADAM_EPS = 1e-08
ADAM_WD = 0.01
ADAM_STEP = 10

N_CHIPS = 4
N_DEV = 8
LANES = 128
VMEM_LIMIT_BYTES = 56 * 2 ** 20
COPY_CHUNK_BYTES = 512 * 2 ** 10
COPY_ROW_ALIGN = 16

MESH = pl.DeviceIdType.MESH
ANY = pl.BlockSpec(memory_space=pl.ANY)
VMEM_WHOLE = pl.BlockSpec(memory_space=pltpu.VMEM)
HBM_SPEC = pl.BlockSpec(memory_space=pltpu.HBM)
SEM_SPEC = pl.BlockSpec(memory_space=pltpu.SEMAPHORE)
SIDE_EFFECT = pltpu.SideEffectType.DATAFLOW_SIDE_EFFECTING

NN = (((1,), (0,)), ((), ()))
NT = (((1,), (1,)), ((), ()))
TN = (((0,), (0,)), ((), ()))


def _tile(n, target, mult):
    best = None
    for t in range(mult, min(n, target) + 1, mult):
        if n % t == 0:
            best = t
    return n if best is None else best


def _params(n_grid_axes):
    return pltpu.CompilerParams(dimension_semantics=("arbitrary",) * n_grid_axes, vmem_limit_bytes=VMEM_LIMIT_BYTES)


def _rsqrt_mean_sq(x):
    return lax.rsqrt(jnp.mean(x * x, axis=-1, keepdims=True) + EPS)


def _sum_all(x):
    return jnp.sum(jnp.sum(x, axis=1, keepdims=True), axis=0, keepdims=True)


def _sigmoid(x):
    return 0.5 * jnp.tanh(0.5 * x) + 0.5


def _store_epilogue(accs, ex_refs, out_refs, i, j):
    for a, o in zip(accs, out_refs):
        o[...] = a.astype(o.dtype)


def _fused_matmul(name, grid, operands, pairs, acc_shapes, extras, outs, epilogue):
    nk = grid[2]
    n_op, n_ex, n_out, n_acc = len(operands), len(extras), len(outs), len(acc_shapes)
    in_place = nk > 1 and epilogue is _store_epilogue and all(o[1] == jnp.float32 for o in outs)

    def body(*refs):
        op_refs = refs[:n_op]
        ex_refs = refs[n_op:n_op + n_ex]
        out_refs = refs[n_op + n_ex:n_op + n_ex + n_out]
        acc_refs = out_refs if in_place else refs[n_op + n_ex + n_out:]
        i, j, k = pl.program_id(0), pl.program_id(1), pl.program_id(2)
        loaded = {}

        def operand(n):
            if n not in loaded:
                loaded[n] = op_refs[n][...].astype(MXU_DTYPE)
            return loaded[n]

        def product(ia, ib, dims):
            return lax.dot_general(operand(ia), operand(ib), dims, preferred_element_type=jnp.float32)

        def products():
            parts = [None] * n_acc
            for ia, ib, iacc, dims in pairs:
                d = product(ia, ib, dims)
                parts[iacc] = d if parts[iacc] is None else parts[iacc] + d
            return parts

        if nk == 1:
            epilogue(products(), ex_refs, out_refs, i, j)
            return

        @pl.when(k == 0)
        def _():
            for a in range(n_acc):
                acc_refs[a][...] = jnp.zeros_like(acc_refs[a])

        for a, part in enumerate(products()):
            acc_refs[a][...] += part

        if not in_place:
            @pl.when(k == nk - 1)
            def _():
                epilogue([acc_refs[a][...] for a in range(n_acc)], ex_refs, out_refs, i, j)

    arrays = [o[0] for o in operands] + [e[0] for e in extras]
    in_specs = [pl.BlockSpec(o[1], o[2]) for o in operands]
    in_specs += [ANY if e[1] is None else pl.BlockSpec(e[1], e[2]) for e in extras]
    return pl.pallas_call(
        body,
        name=name,
        grid=grid,
        in_specs=in_specs,
        out_specs=[pl.BlockSpec(o[2], o[3]) for o in outs],
        out_shape=[jax.ShapeDtypeStruct(o[0], o[1]) for o in outs],
        scratch_shapes=[pltpu.VMEM(s, jnp.float32) for s in acc_shapes] if nk > 1 and not in_place else [],
        compiler_params=_params(3),
    )(*arrays)


def _matmul_nn(name, a, b, out_dtype, tm=1024, tn=512):
    M, K = a.shape
    N = b.shape[1]
    tm, tn = _tile(M, tm, 16), _tile(N, tn, LANES)
    return _fused_matmul(
        name, (M // tm, N // tn, 1),
        [(a, (tm, K), lambda i, j, k: (i, 0)), (b, (K, tn), lambda i, j, k: (0, j))],
        [(0, 1, 0, NN)], [(tm, tn)], [],
        [((M, N), out_dtype, (tm, tn), lambda i, j, k: (i, j))], _store_epilogue)[0]


def _matmul_nn_cols(name, a, b3, out_dtype, tm=1024):
    M, K = a.shape
    J, _, Nc = b3.shape
    tm = _tile(M, tm, 16)
    return _fused_matmul(
        name, (M // tm, J, 1),
        [(a, (tm, K), lambda i, j, k: (i, 0)), (b3, (None, K, Nc), lambda i, j, k: (j, 0, 0))],
        [(0, 1, 0, NN)], [(tm, Nc)], [],
        [((M, J * Nc), out_dtype, (tm, Nc), lambda i, j, k: (i, j))], _store_epilogue)[0]


def _order_after(after):
    return [] if after is None else [(after, None, None)]


def _matmul_nt(name, a_list, b_list, out_dtype, tm=1024, tn=512, after=None):
    M, K = a_list[0].shape
    N = b_list[0].shape[0]
    tm, tn = _tile(M, tm, 16), _tile(N, tn, LANES)
    n = len(a_list)
    operands = [(a, (tm, K), lambda i, j, k: (i, 0)) for a in a_list]
    operands += [(b, (tn, K), lambda i, j, k: (j, 0)) for b in b_list]
    return _fused_matmul(
        name, (M // tm, N // tn, 1), operands,
        [(p, n + p, 0, NT) for p in range(n)], [(tm, tn)], _order_after(after),
        [((M, N), out_dtype, (tm, tn), lambda i, j, k: (i, j))], _store_epilogue)[0]


def _matmul_nt_cols(name, a_list, b3_list, out_dtype, tm=1024, tn=512, after=None):
    M = a_list[0].shape[0]
    J, N, Kc = b3_list[0].shape
    tm, tn = _tile(M, tm, 16), _tile(N, tn, LANES)
    n = len(a_list)
    operands = [(a, (tm, Kc), lambda i, j, k: (i, k)) for a in a_list]
    operands += [(b, (None, tn, Kc), lambda i, j, k: (k, j, 0)) for b in b3_list]
    return _fused_matmul(
        name, (M // tm, N // tn, J), operands,
        [(p, n + p, 0, NT) for p in range(n)], [(tm, tn)], _order_after(after),
        [((M, N), out_dtype, (tm, tn), lambda i, j, k: (i, j))], _store_epilogue)[0]


def _matmul_tn(name, a, b_list, out_dtype, tka=512, tn=1024, tkm=4096):
    M, Ka = a.shape
    N = b_list[0].shape[1]
    tka, tn, tkm = _tile(Ka, tka, LANES), _tile(N, tn, LANES), _tile(M, tkm, 16)
    n = len(b_list)
    operands = [(a, (tkm, tka), lambda i, j, k: (k, i))]
    operands += [(b, (tkm, tn), lambda i, j, k: (k, j)) for b in b_list]
    return _fused_matmul(
        name, (Ka // tka, N // tn, M // tkm), operands,
        [(0, 1 + p, p, TN) for p in range(n)], [(tka, tn)] * n, [],
        [((Ka, N), out_dtype, (tka, tn), lambda i, j, k: (i, j))] * n, _store_epilogue)


def _matmul_tn_cols(name, a, b_list, n_shards, out_dtype, tka=512, tkm=4096, after=None):
    M, Ka = a.shape
    Nc = b_list[0].shape[1] // n_shards
    tka, tkm = _tile(Ka, tka, LANES), _tile(M, tkm, 16)
    n = len(b_list)
    operands = [(a, (tkm, tka), lambda i, j, k: (k, i))]
    operands += [(b, (tkm, Nc), lambda i, j, k: (k, j)) for b in b_list]
    return _fused_matmul(
        name, (Ka // tka, n_shards, M // tkm), operands,
        [(0, 1 + p, p, TN) for p in range(n)], [(tka, Nc)] * n, _order_after(after),
        [((n_shards, Ka, Nc), out_dtype, (None, tka, Nc), lambda i, j, k: (j, i, 0))] * n, _store_epilogue)


def _rmsnorm(name, x, g_row, out_dtype, tm=512, after=None):
    S, D = x.shape
    tm = _tile(S, tm, 16)

    def body(x_ref, g_ref, *rest):
        o_ref = rest[-1]
        xv = x_ref[...]
        o_ref[...] = (xv * _rsqrt_mean_sq(xv) * g_ref[...]).astype(o_ref.dtype)

    ordered = [] if after is None else [after]
    return pl.pallas_call(
        body, name=name, grid=(S // tm,),
        in_specs=[pl.BlockSpec((tm, D), lambda i: (i, 0)), pl.BlockSpec((1, D), lambda i: (0, 0))] + [ANY] * len(ordered),
        out_specs=pl.BlockSpec((tm, D), lambda i: (i, 0)),
        out_shape=jax.ShapeDtypeStruct((S, D), out_dtype),
        compiler_params=_params(1),
    )(x, g_row, *ordered)


def _rmsnorm_bwd_values(dh, xin, g):
    r = _rsqrt_mean_sq(xin)
    xhat = xin * r
    dhg = dh * g
    dx = r * (dhg - xhat * jnp.mean(dhg * xhat, axis=-1, keepdims=True))
    return dx, jnp.sum(dh * xhat, axis=0, keepdims=True)


def _rmsnorm_bwd(name, dh, xin, g_row, dres, low_dtype, tm=256):
    S, D = xin.shape
    tm = _tile(S, tm, 16)
    has_dx = dres is not None
    has_low = has_dx and low_dtype is not None

    def body(*refs):
        dh_ref, x_ref, g_ref = refs[:3]
        rest = refs[3:]
        dx, dg = _rmsnorm_bwd_values(dh_ref[...], x_ref[...], g_ref[...])
        if has_dx:
            dres_ref, rest = rest[0], rest[1:]
            dx = dx + dres_ref[...]
            rest[0][...] = dx
            if has_low:
                rest[1][...] = dx.astype(low_dtype)
        dg_ref = rest[-1]

        @pl.when(pl.program_id(0) == 0)
        def _():
            dg_ref[...] = jnp.zeros_like(dg_ref)

        dg_ref[...] += dg

    row = pl.BlockSpec((tm, D), lambda i: (i, 0))
    vec = pl.BlockSpec((1, D), lambda i: (0, 0))
    ins, in_specs = [dh, xin, g_row], [row, row, vec]
    out_shape, out_specs = [], []
    if has_dx:
        ins.append(dres)
        in_specs.append(row)
        out_shape.append(jax.ShapeDtypeStruct((S, D), jnp.float32))
        out_specs.append(row)
        if has_low:
            out_shape.append(jax.ShapeDtypeStruct((S, D), low_dtype))
            out_specs.append(row)
    out_shape.append(jax.ShapeDtypeStruct((1, D), jnp.float32))
    out_specs.append(vec)
    return pl.pallas_call(
        body, name=name, grid=(S // tm,), in_specs=in_specs, out_specs=out_specs, out_shape=out_shape,
        compiler_params=_params(1),
    )(*ins)


def _chunk_mask(blk):
    t = lax.broadcasted_iota(jnp.int32, (blk, blk), 0)
    s = lax.broadcasted_iota(jnp.int32, (blk, blk), 1)
    return (s // CHUNK) <= (t // CHUNK)


def _pool_inputs(a, halo, row0, tm, g, pg):
    cols = slice(g * pg, (g + 1) * pg)
    w = POOL_WINDOWS[g]
    s = jnp.concatenate([halo[:, cols], a[:, cols]], axis=0)
    span = 1
    while span < w:
        s = s + pltpu.roll(s, span, axis=0)
        span *= 2
    t = lax.broadcasted_iota(jnp.int32, (tm, 1), 0) + row0
    cnt = jnp.minimum(t + 1, w).astype(jnp.float32)
    return s[POOL_HALO:] / cnt - a[:, cols]


def _mixer_specs(S, tm, DP, DS):
    per_halo = tm // POOL_HALO
    n_halo = S // POOL_HALO
    a_spec = pl.BlockSpec((tm, DP), lambda i: (i, 0))
    prev_spec = pl.BlockSpec((POOL_HALO, DP), lambda i: (jnp.maximum(i * per_halo - 1, 0), 0))
    next_spec = pl.BlockSpec((POOL_HALO, DP), lambda i: (jnp.minimum((i + 1) * per_halo, n_halo - 1), 0))
    u_spec = pl.BlockSpec((tm, DS), lambda i: (i, 1))
    v_spec = pl.BlockSpec((tm, DS), lambda i: (i, 2))
    return a_spec, prev_spec, next_spec, u_spec, v_spec


def _whole(arr):
    nd = arr.ndim
    return pl.BlockSpec(arr.shape, lambda i: (0,) * nd)


def _mixer_fwd(proj, pw, ps_row, gs_row, ws, b_full, tm=512):
    S = proj.shape[0]
    G, PG, _ = pw.shape
    DP = G * PG
    H, BLK, _ = ws.shape
    DS = gs_row.shape[1]
    HD = DS // H
    assert DP == DS and proj.shape[1] == DP + 2 * DS
    tm = _tile(S, tm, BLK)

    def body(a_ref, prev_ref, u_ref, v_ref, pw_ref, ps_ref, gs_ref, ws_ref, b_ref, z_ref):
        i = pl.program_id(0)
        a = a_ref[...]
        halo = jnp.where(i > 0, prev_ref[...], 0.0)
        for g in range(G):
            cols = slice(g * PG, (g + 1) * PG)
            p = _pool_inputs(a, halo, i * tm, tm, g, PG)
            y = jnp.dot(p.astype(MXU_DTYPE), pw_ref[g], preferred_element_type=jnp.float32) * ps_ref[:, cols]
            z_ref[:, cols] = y.astype(z_ref.dtype)
        v = v_ref[...]
        vn = (v * _rsqrt_mean_sq(v) * gs_ref[...]).astype(MXU_DTYPE)
        u = u_ref[...]
        mask = _chunk_mask(BLK)
        for h in range(H):
            wm = jnp.where(mask, ws_ref[h], 0.0).astype(MXU_DTYPE)
            bias = b_ref[h]
            cols = slice(h * HD, (h + 1) * HD)
            for n in range(tm // BLK):
                rows = slice(n * BLK, (n + 1) * BLK)
                mixed = jnp.dot(wm, vn[rows, cols], preferred_element_type=jnp.float32) + bias
                z_ref[rows, DP + h * HD:DP + (h + 1) * HD] = (u[rows, cols] * mixed).astype(z_ref.dtype)

    a_spec, prev_spec, _, u_spec, v_spec = _mixer_specs(S, tm, DP, DS)
    return pl.pallas_call(
        body, name="mixer_fwd", grid=(S // tm,),
        in_specs=[a_spec, prev_spec, u_spec, v_spec, _whole(pw), _whole(ps_row), _whole(gs_row), _whole(ws), _whole(b_full)],
        out_specs=pl.BlockSpec((tm, DP + DS), lambda i: (i, 0)),
        out_shape=jax.ShapeDtypeStruct((S, DP + DS), MXU_DTYPE),
        compiler_params=_params(1),
    )(proj, proj, proj, proj, pw, ps_row, gs_row, ws, b_full)


def _mixer_bwd(proj, dz, pw, ps_row, gs_row, ws, b_full, tm=256):
    S = proj.shape[0]
    G, PG, _ = pw.shape
    DP = G * PG
    H, BLK, _ = ws.shape
    DS = gs_row.shape[1]
    HD = DS // H
    tm = _tile(S, tm, BLK)
    nb = S // tm
    ext = tm + POOL_HALO

    def body(a_ref, prev_ref, u_ref, v_ref, dzp_ref, dzp_next_ref, dzs_ref, pw_ref, ps_ref, gs_ref, ws_ref, b_ref,
             dproj_ref, dpw_ref, dps_ref, dgs_ref, dws_ref, db_ref, dvn_ref):
        i = pl.program_id(0)

        @pl.when(i == 0)
        def _():
            for r in (dpw_ref, dps_ref, dgs_ref, dws_ref, db_ref):
                r[...] = jnp.zeros_like(r)

        a = a_ref[...]
        halo = jnp.where(i > 0, prev_ref[...], 0.0)
        dyp = dzp_ref[...]
        dyp_next = jnp.where(i < nb - 1, dzp_next_ref[...], 0.0)
        dq_ext = jnp.concatenate([dyp, dyp_next], axis=0) * ps_ref[...]
        t_ext = lax.broadcasted_iota(jnp.int32, (ext, 1), 0) + i * tm
        for g in range(G):
            cols = slice(g * PG, (g + 1) * PG)
            w = POOL_WINDOWS[g]
            pb = _pool_inputs(a, halo, i * tm, tm, g, PG).astype(MXU_DTYPE)
            y_pre = jnp.dot(pb, pw_ref[g], preferred_element_type=jnp.float32)
            dps_ref[:, cols] += jnp.sum(dyp[:, cols] * y_pre, axis=0, keepdims=True)
            dqb = dq_ext[:, cols].astype(MXU_DTYPE)
            dpw_ref[g] += lax.dot_general(pb, dqb[:tm], TN, preferred_element_type=jnp.float32)
            dp = lax.dot_general(dqb, pw_ref[g], NT, preferred_element_type=jnp.float32)
            f = dp / jnp.minimum(t_ext + 1, w).astype(jnp.float32)
            span = 1
            while span < w:
                f = f + pltpu.roll(f, ext - span, axis=0)
                span *= 2
            dproj_ref[:, cols] = (f[:tm] - dp[:tm]).astype(dproj_ref.dtype)

        v = v_ref[...]
        rv = _rsqrt_mean_sq(v)
        vhat = v * rv
        gs = gs_ref[...]
        vnb = (vhat * gs).astype(MXU_DTYPE)
        u = u_ref[...]
        dys = dzs_ref[...]
        mask = _chunk_mask(BLK)
        for h in range(H):
            wm = jnp.where(mask, ws_ref[h], 0.0).astype(MXU_DTYPE)
            bias = b_ref[h]
            cols = slice(h * HD, (h + 1) * HD)
            dm_sum = jnp.zeros((BLK, HD), jnp.float32)
            dws_sum = jnp.zeros((BLK, BLK), jnp.float32)
            for n in range(tm // BLK):
                rows = slice(n * BLK, (n + 1) * BLK)
                vblk = vnb[rows, cols]
                mixed = jnp.dot(wm, vblk, preferred_element_type=jnp.float32) + bias
                dy = dys[rows, cols]
                dproj_ref[rows, DP + h * HD:DP + (h + 1) * HD] = (dy * mixed).astype(dproj_ref.dtype)
                dm = dy * u[rows, cols]
                dm_sum = dm_sum + dm
                dmb = dm.astype(MXU_DTYPE)
                dws_sum = dws_sum + lax.dot_general(dmb, vblk, NT, preferred_element_type=jnp.float32)
                dvn_ref[rows, cols] = lax.dot_general(wm, dmb, TN, preferred_element_type=jnp.float32)
            dws_ref[h] += jnp.where(mask, dws_sum, 0.0)
            db_ref[h:h + 1, :] += jnp.sum(dm_sum.T, axis=0, keepdims=True)
        dvn = dvn_ref[...]
        dvg = dvn * gs
        dv = rv * (dvg - vhat * jnp.mean(dvg * vhat, axis=-1, keepdims=True))
        dgs_ref[...] += jnp.sum(dvn * vhat, axis=0, keepdims=True)
        dproj_ref[:, DP + DS:] = dv.astype(dproj_ref.dtype)

    a_spec, prev_spec, next_spec, u_spec, v_spec = _mixer_specs(S, tm, DP, DS)
    dzp_spec = pl.BlockSpec((tm, DP), lambda i: (i, 0))
    dzs_spec = pl.BlockSpec((tm, DS), lambda i: (i, 1))
    small = [pw, ps_row, gs_row, ws, b_full]
    out_shape = [
        jax.ShapeDtypeStruct((S, DP + 2 * DS), MXU_DTYPE),
        jax.ShapeDtypeStruct(pw.shape, jnp.float32),
        jax.ShapeDtypeStruct(ps_row.shape, jnp.float32),
        jax.ShapeDtypeStruct(gs_row.shape, jnp.float32),
        jax.ShapeDtypeStruct(ws.shape, jnp.float32),
        jax.ShapeDtypeStruct((H, BLK), jnp.float32),
    ]
    out_specs = [pl.BlockSpec((tm, DP + 2 * DS), lambda i: (i, 0))] + [_whole(s) for s in out_shape[1:]]
    return pl.pallas_call(
        body, name="mixer_bwd", grid=(nb,),
        in_specs=[a_spec, prev_spec, u_spec, v_spec, dzp_spec, next_spec, dzs_spec] + [_whole(s) for s in small],
        out_specs=out_specs, out_shape=out_shape,
        scratch_shapes=[pltpu.VMEM((tm, DS), jnp.float32)],
        compiler_params=_params(1),
    )(proj, proj, proj, proj, dz, dz, dz, *small)


def _softmax_rows(q, k, scale):
    s = lax.dot_general(q, k, NT, preferred_element_type=jnp.float32) * scale
    e = jnp.exp(s - jnp.max(s, axis=-1, keepdims=True))
    return e / jnp.sum(e, axis=-1, keepdims=True)


def _attn_fwd(q, k, v, tm=512):
    S, D = q.shape
    M = k.shape[0]
    HD = D // N_XATTN_HEADS
    scale = HD ** -0.5
    tm = _tile(S, tm, 16)

    def body(q_ref, k_ref, v_ref, o_ref):
        for h in range(N_XATTN_HEADS):
            cols = slice(h * HD, (h + 1) * HD)
            p = _softmax_rows(q_ref[:, cols], k_ref[:, cols], scale)
            o = jnp.dot(p.astype(MXU_DTYPE), v_ref[:, cols], preferred_element_type=jnp.float32)
            o_ref[:, cols] = o.astype(o_ref.dtype)

    row = pl.BlockSpec((tm, D), lambda i: (i, 0))
    mem = pl.BlockSpec((M, D), lambda i: (0, 0))
    return pl.pallas_call(
        body, name="attn_fwd", grid=(S // tm,), in_specs=[row, mem, mem], out_specs=row,
        out_shape=jax.ShapeDtypeStruct((S, D), MXU_DTYPE), compiler_params=_params(1),
    )(q, k, v)


def _attn_bwd(q, k, v, do, tm=512):
    S, D = q.shape
    M = k.shape[0]
    HD = D // N_XATTN_HEADS
    scale = HD ** -0.5
    tm = _tile(S, tm, 16)

    def body(q_ref, k_ref, v_ref, do_ref, dq_ref, dk_ref, dv_ref):
        @pl.when(pl.program_id(0) == 0)
        def _():
            dk_ref[...] = jnp.zeros_like(dk_ref)
            dv_ref[...] = jnp.zeros_like(dv_ref)

        for h in range(N_XATTN_HEADS):
            cols = slice(h * HD, (h + 1) * HD)
            qh, kh, vh, doh = q_ref[:, cols], k_ref[:, cols], v_ref[:, cols], do_ref[:, cols]
            p = _softmax_rows(qh, kh, scale)
            dp = lax.dot_general(doh, vh, NT, preferred_element_type=jnp.float32)
            dv_ref[:, cols] += lax.dot_general(p.astype(MXU_DTYPE), doh, TN, preferred_element_type=jnp.float32)
            ds = (p * (dp - jnp.sum(dp * p, axis=-1, keepdims=True)) * scale).astype(MXU_DTYPE)
            dq_ref[:, cols] = jnp.dot(ds, kh, preferred_element_type=jnp.float32).astype(dq_ref.dtype)
            dk_ref[:, cols] += lax.dot_general(ds, qh, TN, preferred_element_type=jnp.float32)

    row = pl.BlockSpec((tm, D), lambda i: (i, 0))
    mem = pl.BlockSpec((M, D), lambda i: (0, 0))
    return pl.pallas_call(
        body, name="attn_bwd", grid=(S // tm,), in_specs=[row, mem, mem, row], out_specs=[row, mem, mem],
        out_shape=[jax.ShapeDtypeStruct((S, D), MXU_DTYPE), jax.ShapeDtypeStruct((M, D), jnp.float32),
                   jax.ShapeDtypeStruct((M, D), jnp.float32)],
        compiler_params=_params(1),
    )(q, k, v, do)


def _proj_residual_norm(name, a, w2d, resid, g_row, tm=256):
    M, K = a.shape
    N = w2d.shape[1]
    tm = _tile(M, tm, 16)

    def epilogue(accs, ex_refs, out_refs, i, j):
        x_new = ex_refs[0][...] + accs[0]
        out_refs[0][...] = x_new
        out_refs[1][...] = (x_new * _rsqrt_mean_sq(x_new) * ex_refs[1][...]).astype(MXU_DTYPE)

    row = lambda i, j, k: (i, 0)
    return _fused_matmul(
        name, (M // tm, 1, 1),
        [(a, (tm, K), row), (w2d, (K, N), lambda i, j, k: (0, 0))],
        [(0, 1, 0, NN)], [(tm, N)],
        [(resid, (tm, N), row), (g_row, (1, N), lambda i, j, k: (0, 0))],
        [((M, N), jnp.float32, (tm, N), row), ((M, N), MXU_DTYPE, (tm, N), row)], epilogue)


def _ffn_gate_up(h, wg3, wu3, tm=512):
    M, K = h.shape
    J, _, Nc = wg3.shape
    tm = _tile(M, tm, 16)

    def epilogue(accs, ex_refs, out_refs, i, j):
        gate, up = accs
        sg = _sigmoid(gate)
        silu = gate * sg
        out_refs[0][...] = (silu * up).astype(MXU_DTYPE)
        out_refs[1][...] = (up * (sg * (1.0 + gate * (1.0 - sg)))).astype(MXU_DTYPE)
        out_refs[2][...] = silu.astype(MXU_DTYPE)

    col = lambda i, j, k: (i, j)
    wspec = lambda i, j, k: (j, 0, 0)
    return _fused_matmul(
        "ffn_gate_up", (M // tm, J, 1),
        [(h, (tm, K), lambda i, j, k: (i, 0)), (wg3, (None, K, Nc), wspec), (wu3, (None, K, Nc), wspec)],
        [(0, 1, 0, NN), (0, 2, 1, NN)], [(tm, Nc)] * 2, [],
        [((M, J * Nc), MXU_DTYPE, (tm, Nc), col)] * 3, epilogue)


def _ffn_down(act, wd2d, x2, tm=512, tn=512):
    M, F = act.shape
    D = wd2d.shape[1]
    tm, tn = _tile(M, tm, 16), _tile(D, tn, LANES)

    def epilogue(accs, ex_refs, out_refs, i, j):
        out_refs[0][...] = ex_refs[0][...] + accs[0]

    tile = lambda i, j, k: (i, j)
    return _fused_matmul(
        "ffn_down", (M // tm, D // tn, 1),
        [(act, (tm, F), lambda i, j, k: (i, 0)), (wd2d, (F, tn), lambda i, j, k: (0, j))],
        [(0, 1, 0, NN)], [(tm, tn)], [(x2, (tm, tn), tile)],
        [((M, D), jnp.float32, (tm, tn), tile)], epilogue)[0]


def _final_norm_loss(x3, target, g_row, tm=256):
    S, D = x3.shape
    tm = _tile(S, tm, 16)

    def body(x_ref, t_ref, g_ref, dx_ref, dxl_ref, loss_ref, dg_ref):
        x3v = x_ref[...]
        g = g_ref[...]
        r = _rsqrt_mean_sq(x3v)
        xhat = x3v * r
        diff = xhat * g - t_ref[...]
        dy = diff / D
        dyg = dy * g
        dx = r * (dyg - xhat * jnp.mean(dyg * xhat, axis=-1, keepdims=True))
        dx_ref[...] = dx
        dxl_ref[...] = dx.astype(MXU_DTYPE)

        @pl.when(pl.program_id(0) == 0)
        def _():
            loss_ref[...] = jnp.zeros_like(loss_ref)
            dg_ref[...] = jnp.zeros_like(dg_ref)

        loss_ref[...] += 0.5 * _sum_all(jnp.mean(diff * diff, axis=-1, keepdims=True))
        dg_ref[...] += jnp.sum(dy * xhat, axis=0, keepdims=True)

    row = pl.BlockSpec((tm, D), lambda i: (i, 0))
    vec = pl.BlockSpec((1, D), lambda i: (0, 0))
    return pl.pallas_call(
        body, name="final_norm_loss", grid=(S // tm,), in_specs=[row, row, vec],
        out_specs=[row, row, pl.BlockSpec((1, 1), lambda i: (0, 0)), vec],
        out_shape=[jax.ShapeDtypeStruct((S, D), jnp.float32), jax.ShapeDtypeStruct((S, D), MXU_DTYPE),
                   jax.ShapeDtypeStruct((1, 1), jnp.float32), jax.ShapeDtypeStruct((1, D), jnp.float32)],
        compiler_params=_params(1),
    )(x3, target, g_row)


def _ffn_down_bwd(dx3_low, wd2d, act_by_gate, act_by_up, n_shards, tm=512):
    M, D = dx3_low.shape
    F = wd2d.shape[0]
    tn = F // n_shards
    tm = _tile(M, tm, 16)

    def epilogue(accs, ex_refs, out_refs, i, j):
        dact = accs[0]
        out_refs[0][...] = (dact * ex_refs[0][...].astype(jnp.float32)).astype(MXU_DTYPE)
        out_refs[1][...] = (dact * ex_refs[1][...].astype(jnp.float32)).astype(MXU_DTYPE)

    col = lambda i, j, k: (i, j)
    return _fused_matmul(
        "ffn_down_bwd", (M // tm, n_shards, 1),
        [(dx3_low, (tm, D), lambda i, j, k: (i, 0)), (wd2d, (tn, D), lambda i, j, k: (j, 0))],
        [(0, 1, 0, NT)], [(tm, tn)],
        [(act_by_gate, (tm, tn), col), (act_by_up, (tm, tn), col)],
        [((M, F), MXU_DTYPE, (tm, tn), col)] * 2, epilogue)


def _matmul_nt_norm_bwd(name, a, b, xin, g_row, dres, low_dtype, tm=256, after=None):
    M = a.shape[0]
    sharded = b.ndim == 3
    N, Kc = b.shape[-2:]
    J = b.shape[0] if sharded else 1
    tm = _tile(M, tm, 16)
    has_low = low_dtype is not None

    def epilogue(accs, ex_refs, out_refs, i, j):
        dx, dg = _rmsnorm_bwd_values(accs[0], ex_refs[0][...], ex_refs[1][...])
        dx = dx + ex_refs[2][...]
        out_refs[0][...] = dx
        if has_low:
            out_refs[1][...] = dx.astype(low_dtype)
        dg_ref = out_refs[-1]

        @pl.when(i == 0)
        def _():
            dg_ref[...] = jnp.zeros_like(dg_ref)

        dg_ref[...] += dg

    row = lambda i, j, k: (i, 0)
    const = lambda i, j, k: (0, 0)
    b_operand = (b, (None, N, Kc), lambda i, j, k: (k, 0, 0)) if sharded else (b, (N, Kc), const)
    outs = [((M, N), jnp.float32, (tm, N), row)]
    if has_low:
        outs.append(((M, N), low_dtype, (tm, N), row))
    outs.append(((1, N), jnp.float32, (1, N), const))
    return _fused_matmul(
        name, (M // tm, 1, J), [(a, (tm, Kc), lambda i, j, k: (i, k)), b_operand],
        [(0, 1, 0, NT)], [(tm, N)],
        [(xin, (tm, N), row), (g_row, (1, N), const), (dres, (tm, N), row)] + _order_after(after),
        outs, epilogue)


def _row_tile(R, C, target_bytes=2 ** 21):
    return _tile(R, max(16, target_bytes // (4 * C)), 16)


def _prefetch_call(body, name, grid, pos, in_specs, out_specs, out_shape, ins, after=None):
    ordered = [] if after is None else [after]
    grid_spec = pltpu.PrefetchScalarGridSpec(
        num_scalar_prefetch=1, grid=grid, in_specs=list(in_specs) + [ANY] * len(ordered), out_specs=out_specs)
    return pl.pallas_call(body, name=name, grid_spec=grid_spec, out_shape=out_shape,
                          compiler_params=_params(len(grid)))(pos, *ins, *ordered)


def _runs(arrays):
    runs = []
    for i, a in enumerate(arrays):
        if runs and arrays[runs[-1][-1]].shape == a.shape:
            runs[-1].append(i)
        else:
            runs.append([i])
    return runs


def _per_run(fn, names, *array_lists):
    out = [None] * len(names)
    for run in _runs(array_lists[0]):
        results = fn("_".join(names[i] for i in run), *[[arrays[i] for i in run] for arrays in array_lists])
        for i, r in zip(run, results):
            out[i] = r
    return out


def _cast_into_slabs(name, xs, dtype, pos, after=None):
    n = len(xs)
    R, C = xs[0].shape
    tr = _row_tile(R, C * n, 2 ** 22)

    def body(pos_ref, *refs):
        for t in range(n):
            refs[len(refs) - n + t][...] = refs[t][...].astype(dtype)

    return _prefetch_call(
        body, "cast_" + name, (R // tr,), pos, [pl.BlockSpec((tr, C), lambda i, pos: (i, 0))] * n,
        [pl.BlockSpec((None, tr, C), lambda i, pos: (pos[0], i, 0))] * n,
        [jax.ShapeDtypeStruct((N_CHIPS, R, C), dtype)] * n, xs, after)


def _chip_sum(name, partials, gots, dtype, pos):
    n = len(partials)
    J, R, C = partials[0].shape
    Rh = R // 2
    tr = _row_tile(Rh, C * n, 2 ** 22)

    def body(pos_ref, *refs):
        for t in range(n):
            refs[2 * n + t][...] = (refs[t][...].astype(jnp.float32) + refs[n + t][...].astype(jnp.float32)).astype(dtype)

    blk = pl.BlockSpec((None, tr, C), lambda j, i, pos: (j, i, 0))
    return _prefetch_call(
        body, "chip_sum_" + name, (J, Rh // tr), pos,
        [pl.BlockSpec((None, None, tr, C), lambda j, i, pos: (j, pos[1], i, 0))] * n + [blk] * n, [blk] * n,
        [jax.ShapeDtypeStruct((J, Rh, C), dtype)] * n, [p.reshape(J, 2, Rh, C) for p in partials] + list(gots))


def _owner_sum(name, chip_sums, others, pos):
    n = len(chip_sums)
    J, Rh, C = chip_sums[0].shape
    n_other = others[0].shape[0]
    tr = _row_tile(Rh, C * (n_other + 1) * n, 2 ** 22)

    def body(pos_ref, *refs):
        for t in range(n):
            acc = refs[t][...].astype(jnp.float32)
            for k in range(n_other):
                acc = acc + refs[n + t][k].astype(jnp.float32)
            refs[2 * n + t][...] = acc

    return _prefetch_call(
        body, "owner_sum_" + name, (Rh // tr,), pos,
        [pl.BlockSpec((None, tr, C), lambda i, pos: (pos[0], i, 0))] * n
        + [pl.BlockSpec((n_other, tr, C), lambda i, pos: (0, i, 0))] * n,
        [pl.BlockSpec((None, tr, C), lambda i, pos: (pos[1], i, 0))] * n,
        [jax.ShapeDtypeStruct((2, Rh, C), jnp.float32)] * n, list(chip_sums) + list(others))


def _adamw_values(w, g, m, v):
    m = ADAM_B1 * m + (1.0 - ADAM_B1) * g
    v = ADAM_B2 * v + (1.0 - ADAM_B2) * (g * g)
    m_hat = m / (1.0 - ADAM_B1 ** ADAM_STEP)
    v_hat = v / (1.0 - ADAM_B2 ** ADAM_STEP)
    delta = -ADAM_LR * (m_hat / (jnp.sqrt(v_hat) + ADAM_EPS) + ADAM_WD * w)
    return delta, m, v


def _adamw(name, ws, gs, ms, vs):
    n = len(ws)
    R, C = ws[0].shape
    tr = _row_tile(R, C * n, 2 ** 20)

    def body(*refs):
        w_refs, g_refs, m_refs, v_refs = (refs[q * n:(q + 1) * n] for q in range(4))
        outs = refs[4 * n:]
        for t in range(n):
            g = g_refs[t][...]
            outs[4 * t][...] = g
            outs[4 * t + 1][...], outs[4 * t + 2][...], outs[4 * t + 3][...] = _adamw_values(
                w_refs[t][...], g, m_refs[t][...], v_refs[t][...])

    blk = pl.BlockSpec((tr, C), lambda i: (i, 0))
    res = pl.pallas_call(
        body, name="adamw_" + name, grid=(R // tr,), in_specs=[blk] * (4 * n), out_specs=[blk] * (4 * n),
        out_shape=[jax.ShapeDtypeStruct((R, C), jnp.float32)] * (4 * n), compiler_params=_params(1),
    )(*ws, *gs, *ms, *vs)
    return [tuple(res[4 * t:4 * t + 4]) for t in range(n)]


def _adamw_small(g_packed, ws, ms, vs, row_counts):
    n = len(ws)

    def body(*refs):
        g_ref = refs[0]
        w_refs, m_refs, v_refs = refs[1:1 + n], refs[1 + n:1 + 2 * n], refs[1 + 2 * n:1 + 3 * n]
        outs = refs[1 + 3 * n:]
        off = 0
        for t in range(n):
            g = g_ref[off:off + row_counts[t], :]
            off += row_counts[t]
            delta, m_new, v_new = _adamw_values(w_refs[t][...], g, m_refs[t][...], v_refs[t][...])
            outs[4 * t][...] = g
            outs[4 * t + 1][...] = delta
            outs[4 * t + 2][...] = m_new
            outs[4 * t + 3][...] = v_new

    out_shape = []
    for t in range(n):
        out_shape += [jax.ShapeDtypeStruct((row_counts[t], LANES), jnp.float32)] * 4
    return pl.pallas_call(
        body, name="adamw_small", in_specs=[VMEM_WHOLE] * (1 + 3 * n), out_specs=[VMEM_WHOLE] * (4 * n),
        out_shape=out_shape, compiler_params=pltpu.CompilerParams(vmem_limit_bytes=VMEM_LIMIT_BYTES),
    )(g_packed, *ws, *ms, *vs)


def _position():
    x, y, c = lax.axis_index("x"), lax.axis_index("y"), lax.axis_index("c")
    other_chips = [(1 - x, y), (x, 1 - y), (1 - x, 1 - y)]
    return x, y, c, other_chips


def _half(rows, which):
    return pl.ds(which * (rows // 2), rows // 2)


def _chunk_index(shape, dtype):
    rows = shape[-2]
    total = math.prod(shape) * jnp.dtype(dtype).itemsize
    n = max(1, min(rows // COPY_ROW_ALIGN, total // COPY_CHUNK_BYTES))
    step = -(-rows // n)
    step = -(-step // COPY_ROW_ALIGN) * COPY_ROW_ALIGN
    lead = (slice(None),) * (len(shape) - 2)
    return [lead + (pl.ds(s, min(step, rows - s)),) for s in range(0, rows, step)]


def _start_in_chunks(make, src, dst, waited_here=True):
    for idx in _chunk_index(src.shape, src.dtype):
        make(src.at[idx], dst.at[idx]).start()
    return make(src, dst) if waited_here else None


def _remote_copier(send_sem, recv_sem, to):
    return lambda src, dst: pltpu.make_async_remote_copy(
        src_ref=src, dst_ref=dst, send_sem=send_sem, recv_sem=recv_sem, device_id=to, device_id_type=MESH)


def _slab_half(ref, chip, which):
    return ref.at[chip, _half(ref.shape[1], which)]


def _gather_start(name, groups, after=None):
    flat = [s for g in groups for s in g]
    T, G = len(flat), len(groups)
    ordered = [] if after is None else [after]

    def body(*refs):
        ins = refs[:T]
        sems = refs[T + len(ordered):T + len(ordered) + 2 * G]
        token = refs[-1]
        x, y, c, other_chips = _position()
        mine = 2 * x + y
        t0 = 0
        for gi, group in enumerate(groups):
            for t in range(len(group)):
                own = _slab_half(ins[t0 + t], mine, c)
                for k, (cx, cy) in enumerate(other_chips):
                    _start_in_chunks(_remote_copier(sems[2 * gi].at[3 * t + k], sems[2 * gi + 1].at[3 * t + k], (cx, cy, c)),
                                     own, own, waited_here=False)
            t0 += len(group)
        token[...] = jnp.zeros_like(token)

    sem_shapes = []
    for group in groups:
        sem_shapes += [pltpu.SemaphoreType.DMA((3 * len(group),))] * 2
    res = pl.pallas_call(
        body, name=name, in_specs=[HBM_SPEC] * T + [ANY] * len(ordered),
        out_specs=[SEM_SPEC] * (2 * G) + [HBM_SPEC] * T + [VMEM_WHOLE],
        out_shape=sem_shapes + [pltpu.HBM(s.shape, s.dtype) for s in flat] + [jax.ShapeDtypeStruct((8, LANES), jnp.float32)],
        input_output_aliases={t: 2 * G + t for t in range(T)},
        compiler_params=pltpu.CompilerParams(has_side_effects=SIDE_EFFECT),
    )(*[pltpu.with_memory_space_constraint(s, pltpu.HBM) for s in flat], *ordered)
    flights, t0 = [], 2 * G
    for gi, group in enumerate(groups):
        flights.append((res[2 * gi], res[2 * gi + 1], res[t0:t0 + len(group)]))
        t0 += len(group)
    return flights, res[-1]


def _gather_wait(name, flight, after):
    send_sems, recv_sems, slabs = flight
    T = len(slabs)

    def body(*refs):
        ins = refs[:T]
        send, recv = refs[T:T + 2]
        x, y, c, other_chips = _position()
        mine = 2 * x + y
        for t in range(T):
            own = _slab_half(ins[t], mine, c)
            for k, (cx, cy) in enumerate(other_chips):
                landed = _slab_half(ins[t], 2 * cx + cy, c)
                cp = _remote_copier(send.at[3 * t + k], recv.at[3 * t + k], (cx, cy, c))(own, landed)
                cp.wait_send()
                cp.wait_recv()

    return pl.pallas_call(
        body, name=name, in_specs=[HBM_SPEC] * T + [SEM_SPEC, SEM_SPEC, ANY], out_specs=[HBM_SPEC] * T,
        out_shape=[pltpu.HBM(s.shape, s.dtype) for s in slabs], input_output_aliases={t: t for t in range(T)},
        compiler_params=pltpu.CompilerParams(has_side_effects=SIDE_EFFECT),
    )(*slabs, send_sems, recv_sems, after)


def _share_with_sibling(name, slabs):
    T = len(slabs)

    def body(*refs):
        outs = refs[T:2 * T]
        send_sems, recv_sems = refs[2 * T:]
        x, y, c, other_chips = _position()
        sibling = (x, y, 1 - c)
        started = []
        for t in range(T):
            for k, (cx, cy) in enumerate(other_chips):
                landed = _slab_half(outs[t], 2 * cx + cy, c)
                started.append(_start_in_chunks(_remote_copier(send_sems.at[t, k], recv_sems.at[t, k], sibling), landed, landed))
        for t in range(T):
            for k, (cx, cy) in enumerate(other_chips):
                arriving = _slab_half(outs[t], 2 * cx + cy, 1 - c)
                _remote_copier(send_sems.at[t, k], recv_sems.at[t, k], sibling)(arriving, arriving).wait_recv()
        for cp in started:
            cp.wait_send()

    return pl.pallas_call(
        body, name=name, in_specs=[ANY] * T, out_specs=[ANY] * T,
        out_shape=[jax.ShapeDtypeStruct(s.shape, s.dtype) for s in slabs],
        input_output_aliases={t: t for t in range(T)},
        scratch_shapes=[pltpu.SemaphoreType.DMA((T, 3)), pltpu.SemaphoreType.DMA((T, 3))],
    )(*slabs)


def _swap_halves(name, grads):
    T = len(grads)

    def body(*refs):
        ins, got = refs[:T], refs[T:2 * T]
        send_sems, recv_sems = refs[2 * T:]
        x, y, c, _ = _position()
        remote = []
        for t in range(T):
            rows = ins[t].shape[1]
            remote.append(_start_in_chunks(
                _remote_copier(send_sems.at[t], recv_sems.at[t], (x, y, 1 - c)), ins[t].at[:, _half(rows, 1 - c)], got[t]))
        for cp in remote:
            cp.wait()

    return pl.pallas_call(
        body, name=name, in_specs=[ANY] * T, out_specs=[ANY] * T,
        out_shape=[jax.ShapeDtypeStruct((g.shape[0], g.shape[1] // 2, g.shape[2]), g.dtype) for g in grads],
        scratch_shapes=[pltpu.SemaphoreType.DMA((T,)), pltpu.SemaphoreType.DMA((T,))],
    )(*grads)


def _swap_start(name, grads):
    T = len(grads)
    lands = [lax.empty((g.shape[0], g.shape[1] // 2, g.shape[2]), g.dtype) for g in grads]

    def body(*refs):
        ins, got = refs[:T], refs[T:2 * T]
        send_sems, recv_sems = refs[2 * T:2 * T + 2]
        token = refs[-1]
        x, y, c, _ = _position()
        for t in range(T):
            _start_in_chunks(_remote_copier(send_sems.at[t], recv_sems.at[t], (x, y, 1 - c)),
                             ins[t].at[:, _half(ins[t].shape[1], 1 - c)], got[t], waited_here=False)
        token[...] = jnp.zeros_like(token)

    arrays = list(grads) + lands
    res = pl.pallas_call(
        body, name=name, in_specs=[HBM_SPEC] * (2 * T),
        out_specs=[SEM_SPEC, SEM_SPEC] + [HBM_SPEC] * (2 * T) + [VMEM_WHOLE],
        out_shape=[pltpu.SemaphoreType.DMA((T,)), pltpu.SemaphoreType.DMA((T,))]
        + [pltpu.HBM(a.shape, a.dtype) for a in arrays] + [jax.ShapeDtypeStruct((8, LANES), jnp.float32)],
        input_output_aliases={i: 2 + i for i in range(2 * T)},
        compiler_params=pltpu.CompilerParams(has_side_effects=SIDE_EFFECT),
    )(*[pltpu.with_memory_space_constraint(a, pltpu.HBM) for a in arrays])
    return res[0], res[1], res[2:2 + T], res[2 + T:2 + 2 * T], res[-1]


def _swap_wait(name, flight, after):
    send_sems, recv_sems, grads, lands, _ = flight
    T = len(grads)

    def body(*refs):
        ins, got = refs[:T], refs[T:2 * T]
        send, recv = refs[2 * T:2 * T + 2]
        x, y, c, _ = _position()
        for t in range(T):
            cp = _remote_copier(send.at[t], recv.at[t], (x, y, 1 - c))(ins[t].at[:, _half(ins[t].shape[1], 1 - c)], got[t])
            cp.wait_send()
            cp.wait_recv()

    arrays = list(grads) + list(lands)
    res = pl.pallas_call(
        body, name=name, in_specs=[HBM_SPEC] * (2 * T) + [SEM_SPEC, SEM_SPEC, ANY], out_specs=[HBM_SPEC] * (2 * T),
        out_shape=[pltpu.HBM(a.shape, a.dtype) for a in arrays], input_output_aliases={i: i for i in range(2 * T)},
        compiler_params=pltpu.CompilerParams(has_side_effects=SIDE_EFFECT),
    )(*arrays, send_sems, recv_sems, after)
    return res[:T], res[T:]


def _scatter_start(name, chip_sums):
    T = len(chip_sums)
    lands = [lax.empty((3,) + s.shape[1:], s.dtype) for s in chip_sums]

    def body(*refs):
        srcs, dsts = refs[:T], refs[T:2 * T]
        send_sems, recv_sems = refs[2 * T:2 * T + 2]
        token = refs[-1]
        x, y, c, other_chips = _position()
        for t in range(T):
            for k, (cx, cy) in enumerate(other_chips):
                _start_in_chunks(_remote_copier(send_sems.at[3 * t + k], recv_sems.at[3 * t + k], (cx, cy, c)),
                                 srcs[t].at[2 * cx + cy], dsts[t].at[k], waited_here=False)
        token[...] = jnp.zeros_like(token)

    hbm = [pltpu.HBM(a.shape, a.dtype) for a in list(chip_sums) + lands]
    res = pl.pallas_call(
        body, name=name, in_specs=[HBM_SPEC] * (2 * T),
        out_specs=[SEM_SPEC, SEM_SPEC] + [HBM_SPEC] * (2 * T) + [VMEM_WHOLE],
        out_shape=[pltpu.SemaphoreType.DMA((3 * T,)), pltpu.SemaphoreType.DMA((3 * T,))] + hbm
        + [jax.ShapeDtypeStruct((8, LANES), jnp.float32)],
        input_output_aliases={i: 2 + i for i in range(2 * T)},
        compiler_params=pltpu.CompilerParams(has_side_effects=SIDE_EFFECT),
    )(*[pltpu.with_memory_space_constraint(a, pltpu.HBM) for a in list(chip_sums) + lands])
    return res[0], res[1], res[2:2 + T], res[2 + T:2 + 2 * T], res[-1]


def _scatter_wait(name, send_sems, recv_sems, chip_sums, lands, after):
    T = len(chip_sums)

    def body(*refs):
        srcs, dsts = refs[:T], refs[T:2 * T]
        send, recv = refs[2 * T:2 * T + 2]
        x, y, c, other_chips = _position()
        for t in range(T):
            for k, (cx, cy) in enumerate(other_chips):
                cp = _remote_copier(send.at[3 * t + k], recv.at[3 * t + k], (cx, cy, c))(srcs[t].at[2 * cx + cy], dsts[t].at[k])
                cp.wait_send()
                cp.wait_recv()

    arrays = list(chip_sums) + list(lands)
    res = pl.pallas_call(
        body, name=name, in_specs=[HBM_SPEC] * (2 * T) + [SEM_SPEC, SEM_SPEC, ANY],
        out_specs=[HBM_SPEC] * (2 * T), out_shape=[pltpu.HBM(a.shape, a.dtype) for a in arrays],
        input_output_aliases={i: i for i in range(2 * T)},
        compiler_params=pltpu.CompilerParams(has_side_effects=SIDE_EFFECT),
    )(*arrays, send_sems, recv_sems, after)
    return res[:T], res[T:]


def _as_shards(partial):
    return [p.reshape((N_CHIPS, -1, p.shape[-1])) for p in partial]


def _reduce_begin(tag, names, partial, pos):
    parts = _as_shards(partial)
    received = _swap_halves("swap_grad_halves_" + tag, parts)
    return _reduce_scatter_start(tag, names, parts, received, pos)


def _reduce_scatter_start(tag, names, parts, received, pos):
    chip_sums = _per_run(lambda name, p, r: _chip_sum(name, p, r, WIRE_DTYPE, pos), names, list(parts), list(received))
    return _scatter_start("scatter_start_" + tag, chip_sums)


def _reduce_end(tag, names, flight, after, pos):
    send_sems, recv_sems, chip_sums, lands, _ = flight
    chip_sums, others = _scatter_wait("scatter_wait_" + tag, send_sems, recv_sems, chip_sums, lands, after)
    reduced = _per_run(lambda name, cs, ot: _owner_sum(name, cs, ot, pos), names, list(chip_sums), list(others))
    return dict(zip(names, reduced))


def _join_start(name, halves):
    T = len(halves)

    def body(*refs):
        ins = refs[:T]
        send_sems, recv_sems = refs[T:T + 2]
        token = refs[-1]
        x, y, c, _ = _position()
        for t in range(T):
            _start_in_chunks(_remote_copier(send_sems.at[t], recv_sems.at[t], (x, y, 1 - c)),
                             ins[t].at[c], ins[t].at[c], waited_here=False)
        token[...] = jnp.zeros_like(token)

    res = pl.pallas_call(
        body, name=name, in_specs=[HBM_SPEC] * T,
        out_specs=[SEM_SPEC, SEM_SPEC] + [HBM_SPEC] * T + [VMEM_WHOLE],
        out_shape=[pltpu.SemaphoreType.DMA((T,)), pltpu.SemaphoreType.DMA((T,))]
        + [pltpu.HBM(h.shape, h.dtype) for h in halves] + [jax.ShapeDtypeStruct((8, LANES), jnp.float32)],
        input_output_aliases={t: 2 + t for t in range(T)},
        compiler_params=pltpu.CompilerParams(has_side_effects=SIDE_EFFECT),
    )(*[pltpu.with_memory_space_constraint(h, pltpu.HBM) for h in halves])
    return res[0], res[1], res[2:2 + T], res[-1]


def _join_wait(name, flight, after):
    send_sems, recv_sems, halves, _ = flight
    T = len(halves)

    def body(*refs):
        ins = refs[:T]
        send, recv = refs[T:T + 2]
        x, y, c, _ = _position()
        for t in range(T):
            cp = _remote_copier(send.at[t], recv.at[t], (x, y, 1 - c))(ins[t].at[c], ins[t].at[1 - c])
            cp.wait_send()
            cp.wait_recv()

    return pl.pallas_call(
        body, name=name, in_specs=[HBM_SPEC] * T + [SEM_SPEC, SEM_SPEC, ANY], out_specs=[HBM_SPEC] * T,
        out_shape=[pltpu.HBM(h.shape, h.dtype) for h in halves], input_output_aliases={t: t for t in range(T)},
        compiler_params=pltpu.CompilerParams(has_side_effects=SIDE_EFFECT),
    )(*halves, send_sems, recv_sems, after)


def _peers():
    x, y, c, _ = _position()
    flips = [(fx, fy, fc) for fx in (0, 1) for fy in (0, 1) for fc in (0, 1)][1:]
    return [(jnp.bitwise_xor(x, fx), jnp.bitwise_xor(y, fy), jnp.bitwise_xor(c, fc)) for fx, fy, fc in flips]


def _device_index(px, py, pc):
    return 4 * px + 2 * py + pc


def _exchange_start(name, slots):
    def body(slots_ref, send_sems, recv_sems, thru, token):
        x, y, c, _ = _position()
        own = slots_ref.at[_device_index(x, y, c)]
        for k, peer in enumerate(_peers()):
            _remote_copier(send_sems.at[k], recv_sems.at[k], peer)(own, own).start()
        token[...] = jnp.zeros_like(token)

    res = pl.pallas_call(
        body, name=name, in_specs=[HBM_SPEC], out_specs=[SEM_SPEC, SEM_SPEC, HBM_SPEC, VMEM_WHOLE],
        out_shape=[pltpu.SemaphoreType.DMA((N_DEV - 1,)), pltpu.SemaphoreType.DMA((N_DEV - 1,)),
                   pltpu.HBM(slots.shape, slots.dtype), jax.ShapeDtypeStruct((8, LANES), jnp.float32)],
        input_output_aliases={0: 2}, compiler_params=pltpu.CompilerParams(has_side_effects=SIDE_EFFECT),
    )(pltpu.with_memory_space_constraint(slots, pltpu.HBM))
    return res[0], res[1], res[2], res[3]


def _exchange_wait(name, flight, after):
    send_sems, recv_sems, slots, _ = flight

    def body(slots_ref, send, recv, after_ref, out_ref):
        x, y, c, _ = _position()
        own = slots_ref.at[_device_index(x, y, c)]
        for k, peer in enumerate(_peers()):
            cp = _remote_copier(send.at[k], recv.at[k], peer)(own, slots_ref.at[_device_index(*peer)])
            cp.wait_send()
            cp.wait_recv()

    return pl.pallas_call(
        body, name=name, in_specs=[HBM_SPEC, SEM_SPEC, SEM_SPEC, ANY], out_specs=HBM_SPEC,
        out_shape=pltpu.HBM(slots.shape, slots.dtype), input_output_aliases={0: 0},
        compiler_params=pltpu.CompilerParams(has_side_effects=SIDE_EFFECT),
    )(slots, send_sems, recv_sems, after)


def _sum_slots(name, slots):
    n, rows, lanes = slots.shape
    tr = _row_tile(rows, lanes * n)

    def body(s_ref, o_ref):
        acc = s_ref[0]
        for d in range(1, n):
            acc = acc + s_ref[d]
        o_ref[...] = acc

    return pl.pallas_call(
        body, name=name, grid=(rows // tr,), in_specs=[pl.BlockSpec((n, tr, lanes), lambda i: (0, i, 0))],
        out_specs=pl.BlockSpec((tr, lanes), lambda i: (i, 0)), out_shape=jax.ShapeDtypeStruct((rows, lanes), jnp.float32),
        compiler_params=_params(1))(slots)


BIG = ("w_in", "pool_w", "w_out", "w_q", "w_k", "w_v", "w_o", "w_gate", "w_up", "w_down")
SMALL = ("norm_mix_g", "pool_scale", "sgu_norm_g", "w_spatial", "b_spatial", "norm_xattn_g", "norm_mem_g",
         "norm_ffn_g", "final_norm_g")
GATHER_ORDER = (("w_in", "pool_w"), ("w_out",), ("w_q", "w_k", "w_v", "w_o"), ("w_gate", "w_up"), ("w_down",))
WEIGHTS = ("norm_mix_g", "w_in", "pool_w", "pool_scale", "sgu_norm_g", "w_spatial", "b_spatial", "w_out",
           "norm_xattn_g", "norm_mem_g", "w_q", "w_k", "w_v", "w_o", "norm_ffn_g", "w_gate", "w_up", "w_down",
           "final_norm_g")


def _as2d(a):
    return a.reshape(-1, a.shape[-1])


def _as_lanes(a):
    return a.reshape(-1, LANES)


def kernel(x, mem, norm_mix_g, w_in, pool_w, pool_scale, sgu_norm_g, w_spatial, b_spatial, w_out, norm_xattn_g, norm_mem_g, w_q, w_k, w_v, w_o, norm_ffn_g, w_gate, w_up, w_down, final_norm_g, loss_target, m_norm_mix_g, m_w_in, m_pool_w, m_pool_scale, m_sgu_norm_g, m_w_spatial, m_b_spatial, m_w_out, m_norm_xattn_g, m_norm_mem_g, m_w_q, m_w_k, m_w_v, m_w_o, m_norm_ffn_g, m_w_gate, m_w_up, m_w_down, m_final_norm_g, v_norm_mix_g, v_w_in, v_pool_w, v_pool_scale, v_sgu_norm_g, v_w_spatial, v_b_spatial, v_w_out, v_norm_xattn_g, v_norm_mem_g, v_w_q, v_w_k, v_w_v, v_w_o, v_norm_ffn_g, v_w_gate, v_w_up, v_w_down, v_final_norm_g):
    args = dict(locals())
    weights = {n: args[n] for n in WEIGHTS}
    moments_m = {n: args["m_" + n] for n in WEIGHTS}
    moments_v = {n: args["v_" + n] for n in WEIGHTS}
    low = MXU_DTYPE

    xs = x[0]
    mems = mem[0]
    target = loss_target[0]
    D = xs.shape[1]
    G, PG = pool_w.shape[1], pool_w.shape[3]

    pos = jnp.stack([2 * lax.axis_index("x") + lax.axis_index("y"), lax.axis_index("c")]).astype(jnp.int32)
    def cast(names, after=None):
        slabs = _per_run(lambda name, xs: _cast_into_slabs(name, xs, WIRE_DTYPE, pos, after), names,
                         [_as2d(weights[n]) for n in names])
        return dict(zip(names, slabs))

    first = cast(GATHER_ORDER[0])
    flights, started = _gather_start("gather_start_first", [[first[n] for n in GATHER_ORDER[0]]])
    rest = cast([n for group in GATHER_ORDER[1:] for n in group], after=started)
    more, started = _gather_start("gather_start_rest", [[rest[n] for n in group] for group in GATHER_ORDER[1:]], after=started)
    flights = flights + more

    def arrive(gi, after):
        landed = _gather_wait("gather_wait_%d" % gi, flights[gi], after)
        return dict(zip(GATHER_ORDER[gi], _share_with_sibling("gather_share_%d" % gi, landed)))

    def rows(a):
        return a.reshape(-1, a.shape[-1])

    g_mix, g_xattn, g_mem, g_ffn = norm_mix_g, norm_xattn_g, norm_mem_g, norm_ffn_g
    g_final = final_norm_g.reshape(1, D)
    ws = w_spatial[0]
    b_full = jnp.broadcast_to(b_spatial[0][:, :, None], ws.shape[:2] + (sgu_norm_g.shape[1] // ws.shape[0],))

    h1 = _rmsnorm("norm_mix", xs, g_mix, low, after=started)
    mn = _rmsnorm("norm_mem", mems, g_mem, low)
    got = arrive(0, mn)
    wg_in = got["w_in"]
    pw = got["pool_w"].reshape(N_CHIPS, G, PG // N_CHIPS, PG).transpose(1, 0, 2, 3).reshape(G, PG, PG)
    proj = _matmul_nn_cols("proj_in", h1, wg_in, jnp.float32)
    z = _mixer_fwd(proj, pw, pool_scale, sgu_norm_g, ws, b_full)
    full = {"w_out": rows(arrive(1, z)["w_out"])}
    x1, h2 = _proj_residual_norm("mix_out", z, full["w_out"], xs, g_xattn)
    full.update({n: rows(a) for n, a in arrive(2, h2).items()})
    q = _matmul_nn("xattn_q", h2, full["w_q"], low)
    k = _matmul_nn("xattn_k", mn, full["w_k"], low)
    vv = _matmul_nn("xattn_v", mn, full["w_v"], low)
    o = _attn_fwd(q, k, vv)
    x2, h3 = _proj_residual_norm("xattn_out", o, full["w_o"], x1, g_ffn)
    got = arrive(3, h3)
    wg_gate, wg_up = got["w_gate"], got["w_up"]
    act, act_by_gate, act_by_up = _ffn_gate_up(h3, wg_gate, wg_up)
    full["w_down"] = rows(arrive(4, act)["w_down"])
    x3 = _ffn_down(act, full["w_down"], x2)
    dx3, dx3_low, loss_part, d_final_g = _final_norm_loss(x3, target, g_final)

    dgate, dup = _ffn_down_bwd(dx3_low, full["w_down"], act_by_gate, act_by_up, N_CHIPS)
    (d_w_down,) = _matmul_tn("grad_w_down", act, [dx3_low], WIRE_DTYPE)
    (d_w_gate,) = _matmul_tn_cols("grad_w_gate", h3, [dgate], N_CHIPS, WIRE_DTYPE)
    (d_w_up,) = _matmul_tn_cols("grad_w_up", h3, [dup], N_CHIPS, WIRE_DTYPE)
    ffn_names = ("w_gate", "w_up", "w_down")
    ffn_swap = _swap_start("swap_start_ffn", _as_shards([d_w_gate, d_w_up, d_w_down]))
    dh3 = _matmul_nt_cols("ffn_dh", [dgate, dup], [wg_gate, wg_up], jnp.float32, after=ffn_swap[-1])
    ffn_flight = _reduce_scatter_start("ffn", ffn_names, *_swap_wait("swap_wait_ffn", ffn_swap, dh3), pos)
    dx2, dx2_low, d_ffn_g = _rmsnorm_bwd("norm_ffn_bwd", dh3, x2, g_ffn, dx3, low)

    do = _matmul_nt("xattn_do", [dx2_low], [full["w_o"]], low, after=ffn_flight[-1])
    (d_w_o,) = _matmul_tn("grad_w_o", o, [dx2_low], WIRE_DTYPE)
    dq, dk, dvv = _attn_bwd(q, k, vv, do)
    (d_w_q,) = _matmul_tn("grad_w_q", h2, [dq], WIRE_DTYPE)
    dx1, dx1_low, d_xattn_g = _matmul_nt_norm_bwd("xattn_dh_norm_bwd", dq, full["w_q"], x1, g_xattn, dx2, low)
    d_w_k, d_w_v = _matmul_tn("grad_w_kv", mn, [dk, dvv], WIRE_DTYPE)
    dmn = _matmul_nt("xattn_dmem", [dk, dvv], [full["w_k"], full["w_v"]], jnp.float32)
    (d_mem_g,) = _rmsnorm_bwd("norm_mem_bwd", dmn, mems, g_mem, None, None)

    (d_w_out,) = _matmul_tn("grad_w_out", z, [dx1_low], WIRE_DTYPE)
    attn_names = ("w_q", "w_k", "w_v", "w_o", "w_out")
    attn_swap = _swap_start("swap_start_attn", _as_shards([d_w_q, d_w_k, d_w_v, d_w_o, d_w_out]))
    dz = _matmul_nt("mix_dz", [dx1_low], [full["w_out"]], jnp.float32, after=attn_swap[-1])
    attn_flight = _reduce_scatter_start("attn", attn_names, *_swap_wait("swap_wait_attn", attn_swap, dz), pos)
    dproj, d_pw, d_ps, d_gs, d_ws, d_b = _mixer_bwd(proj, dz, pw, pool_scale, sgu_norm_g, ws, b_full)
    (d_w_in,) = _matmul_tn_cols("grad_w_in", h1, [dproj], N_CHIPS, WIRE_DTYPE, tka=1024, after=attn_flight[-1])
    d_pool_w = d_pw.reshape(G, N_CHIPS, PG // N_CHIPS, PG).transpose(1, 0, 2, 3).reshape(N_CHIPS, PG, PG).astype(WIRE_DTYPE)
    mix_names = ("w_in", "pool_w")
    mix_flight = _reduce_begin("mix", mix_names, [d_w_in, d_pool_w], pos)
    dh1 = _matmul_nt_cols("mix_dh", [dproj], [wg_in], jnp.float32, tn=1024, after=mix_flight[-1])
    grad_x, d_mix_g = _rmsnorm_bwd("norm_mix_bwd", dh1, xs, g_mix, dx1, None)

    small_parts = {
        "norm_mix_g": d_mix_g, "pool_scale": d_ps, "sgu_norm_g": d_gs, "w_spatial": d_ws, "b_spatial": d_b,
        "norm_xattn_g": d_xattn_g, "norm_mem_g": d_mem_g, "norm_ffn_g": d_ffn_g, "final_norm_g": d_final_g,
    }
    row_counts = [small_parts[n].size // LANES for n in SMALL]
    packed = jnp.concatenate([_as_lanes(small_parts[n]) for n in SMALL], axis=0)
    device = _device_index(lax.axis_index("x"), lax.axis_index("y"), lax.axis_index("c"))
    slots = lax.dynamic_update_slice(jnp.zeros((N_DEV,) + packed.shape, jnp.float32), packed[None], (device, 0, 0))
    small_flight = _exchange_start("small_grads_start", slots)

    groups = (("ffn", ffn_names, ffn_flight), ("attn", attn_names, attn_flight), ("mix", mix_names, mix_flight))
    joins, after = {}, small_flight[-1]
    for tag, names, flight in groups:
        reduced = _reduce_end(tag, names, flight, after, pos)
        joins[tag] = _join_start("join_start_" + tag, [reduced[n] for n in names])
        after = joins[tag][-1]
    small_sum = _sum_slots("small_grads_sum", _exchange_wait("small_grads_wait", small_flight, after))

    out_grad, out_delta, out_m, out_v = {}, {}, {}, {}
    after = small_sum
    for tag, names, _ in groups:
        grads = [g.reshape(-1, g.shape[-1]) for g in _join_wait("join_wait_" + tag, joins[tag], after)]
        updates = _per_run(_adamw, names, [_as2d(weights[n]) for n in names], grads,
                           [_as2d(moments_m[n]) for n in names], [_as2d(moments_v[n]) for n in names])
        for n, update in zip(names, updates):
            shape = weights[n].shape
            out_grad[n], out_delta[n], out_m[n], out_v[n] = (a.reshape(shape) for a in update)
        after = updates[-1][1]
    small_out = _adamw_small(
        small_sum, [_as_lanes(weights[n]) for n in SMALL], [_as_lanes(moments_m[n]) for n in SMALL],
        [_as_lanes(moments_v[n]) for n in SMALL], row_counts)
    for t, n in enumerate(SMALL):
        shape = weights[n].shape
        out_grad[n], out_delta[n], out_m[n], out_v[n] = (a.reshape(shape) for a in small_out[4 * t:4 * t + 4])

    loss = lax.psum(loss_part[0, 0], ("x", "y", "c"))
    return (loss, grad_x[None], *[out_grad[n] for n in WEIGHTS], *[out_delta[n] for n in WEIGHTS],
            *[out_m[n] for n in WEIGHTS], *[out_v[n] for n in WEIGHTS])
```

```python
import math

import jax
import jax.numpy as jnp
from jax import lax
from jax.experimental import pallas as pl
from jax.experimental.pallas import tpu as pltpu

MXU_DTYPE = jnp.bfloat16
WIRE_DTYPE = jnp.bfloat16

EPS = 1e-6
CHUNK = 64
POOL_WINDOWS = (2, 4, 8, 16)
POOL_HALO = 16
N_XATTN_HEADS = 4
ADAM_LR = 0.001
ADAM_B1 = 0.9
ADAM_B2 = 0.999
ADAM_EPS = 1e-08
ADAM_WD = 0.01
ADAM_STEP = 10

N_CHIPS = 4
N_DEV = 8
LANES = 128
VMEM_LIMIT_BYTES = 56 * 2 ** 20
COPY_CHUNK_BYTES = 512 * 2 ** 10
COPY_ROW_ALIGN = 16

MESH = pl.DeviceIdType.MESH
ANY = pl.BlockSpec(memory_space=pl.ANY)
VMEM_WHOLE = pl.BlockSpec(memory_space=pltpu.VMEM)
HBM_SPEC = pl.BlockSpec(memory_space=pltpu.HBM)
SEM_SPEC = pl.BlockSpec(memory_space=pltpu.SEMAPHORE)
SIDE_EFFECT = pltpu.SideEffectType.DATAFLOW_SIDE_EFFECTING

NN = (((1,), (0,)), ((), ()))
NT = (((1,), (1,)), ((), ()))
TN = (((0,), (0,)), ((), ()))


def _tile(n, target, mult):
    best = None
    for t in range(mult, min(n, target) + 1, mult):
        if n % t == 0:
            best = t
    return n if best is None else best


def _params(n_grid_axes):
    return pltpu.CompilerParams(dimension_semantics=("arbitrary",) * n_grid_axes, vmem_limit_bytes=VMEM_LIMIT_BYTES)


def _rsqrt_mean_sq(x):
    return lax.rsqrt(jnp.mean(x * x, axis=-1, keepdims=True) + EPS)


def _sum_all(x):
    return jnp.sum(jnp.sum(x, axis=1, keepdims=True), axis=0, keepdims=True)


def _sigmoid(x):
    return 0.5 * jnp.tanh(0.5 * x) + 0.5


def _store_epilogue(accs, ex_refs, out_refs, i, j):
    for a, o in zip(accs, out_refs):
        o[...] = a.astype(o.dtype)


def _fused_matmul(name, grid, operands, pairs, acc_shapes, extras, outs, epilogue):
    nk = grid[2]
    n_op, n_ex, n_out, n_acc = len(operands), len(extras), len(outs), len(acc_shapes)
    in_place = nk > 1 and epilogue is _store_epilogue and all(o[1] == jnp.float32 for o in outs)

    def body(*refs):
        op_refs = refs[:n_op]
        ex_refs = refs[n_op:n_op + n_ex]
        out_refs = refs[n_op + n_ex:n_op + n_ex + n_out]
        acc_refs = out_refs if in_place else refs[n_op + n_ex + n_out:]
        i, j, k = pl.program_id(0), pl.program_id(1), pl.program_id(2)
        loaded = {}

        def operand(n):
            if n not in loaded:
                loaded[n] = op_refs[n][...].astype(MXU_DTYPE)
            return loaded[n]

        def product(ia, ib, dims):
            return lax.dot_general(operand(ia), operand(ib), dims, preferred_element_type=jnp.float32)

        def products():
            parts = [None] * n_acc
            for ia, ib, iacc, dims in pairs:
                d = product(ia, ib, dims)
                parts[iacc] = d if parts[iacc] is None else parts[iacc] + d
            return parts

        if nk == 1:
            epilogue(products(), ex_refs, out_refs, i, j)
            return

        @pl.when(k == 0)
        def _():
            for a in range(n_acc):
                acc_refs[a][...] = jnp.zeros_like(acc_refs[a])

        for a, part in enumerate(products()):
            acc_refs[a][...] += part

        if not in_place:
            @pl.when(k == nk - 1)
            def _():
                epilogue([acc_refs[a][...] for a in range(n_acc)], ex_refs, out_refs, i, j)

    arrays = [o[0] for o in operands] + [e[0] for e in extras]
    in_specs = [pl.BlockSpec(o[1], o[2]) for o in operands]
    in_specs += [ANY if e[1] is None else pl.BlockSpec(e[1], e[2]) for e in extras]
    return pl.pallas_call(
        body,
        name=name,
        grid=grid,
        in_specs=in_specs,
        out_specs=[pl.BlockSpec(o[2], o[3]) for o in outs],
        out_shape=[jax.ShapeDtypeStruct(o[0], o[1]) for o in outs],
        scratch_shapes=[pltpu.VMEM(s, jnp.float32) for s in acc_shapes] if nk > 1 and not in_place else [],
        compiler_params=_params(3),
    )(*arrays)


def _matmul_nn(name, a, b, out_dtype, tm=1024, tn=512):
    M, K = a.shape
    N = b.shape[1]
    tm, tn = _tile(M, tm, 16), _tile(N, tn, LANES)
    return _fused_matmul(
        name, (M // tm, N // tn, 1),
        [(a, (tm, K), lambda i, j, k: (i, 0)), (b, (K, tn), lambda i, j, k: (0, j))],
        [(0, 1, 0, NN)], [(tm, tn)], [],
        [((M, N), out_dtype, (tm, tn), lambda i, j, k: (i, j))], _store_epilogue)[0]


def _matmul_nn_cols(name, a, b3, out_dtype, tm=1024):
    M, K = a.shape
    J, _, Nc = b3.shape
    tm = _tile(M, tm, 16)
    return _fused_matmul(
        name, (M // tm, J, 1),
        [(a, (tm, K), lambda i, j, k: (i, 0)), (b3, (None, K, Nc), lambda i, j, k: (j, 0, 0))],
        [(0, 1, 0, NN)], [(tm, Nc)], [],
        [((M, J * Nc), out_dtype, (tm, Nc), lambda i, j, k: (i, j))], _store_epilogue)[0]


def _order_after(after):
    return [] if after is None else [(after, None, None)]


def _matmul_nt(name, a_list, b_list, out_dtype, tm=1024, tn=512, after=None):
    M, K = a_list[0].shape
    N = b_list[0].shape[0]
    tm, tn = _tile(M, tm, 16), _tile(N, tn, LANES)
    n = len(a_list)
    operands = [(a, (tm, K), lambda i, j, k: (i, 0)) for a in a_list]
    operands += [(b, (tn, K), lambda i, j, k: (j, 0)) for b in b_list]
    return _fused_matmul(
        name, (M // tm, N // tn, 1), operands,
        [(p, n + p, 0, NT) for p in range(n)], [(tm, tn)], _order_after(after),
        [((M, N), out_dtype, (tm, tn), lambda i, j, k: (i, j))], _store_epilogue)[0]


def _matmul_nt_cols(name, a_list, b3_list, out_dtype, tm=1024, tn=512, after=None):
    M = a_list[0].shape[0]
    J, N, Kc = b3_list[0].shape
    tm, tn = _tile(M, tm, 16), _tile(N, tn, LANES)
    n = len(a_list)
    operands = [(a, (tm, Kc), lambda i, j, k: (i, k)) for a in a_list]
    operands += [(b, (None, tn, Kc), lambda i, j, k: (k, j, 0)) for b in b3_list]
    return _fused_matmul(
        name, (M // tm, N // tn, J), operands,
        [(p, n + p, 0, NT) for p in range(n)], [(tm, tn)], _order_after(after),
        [((M, N), out_dtype, (tm, tn), lambda i, j, k: (i, j))], _store_epilogue)[0]


def _matmul_tn(name, a, b_list, out_dtype, tka=512, tn=1024, tkm=4096):
    M, Ka = a.shape
    N = b_list[0].shape[1]
    tka, tn, tkm = _tile(Ka, tka, LANES), _tile(N, tn, LANES), _tile(M, tkm, 16)
    n = len(b_list)
    operands = [(a, (tkm, tka), lambda i, j, k: (k, i))]
    operands += [(b, (tkm, tn), lambda i, j, k: (k, j)) for b in b_list]
    return _fused_matmul(
        name, (Ka // tka, N // tn, M // tkm), operands,
        [(0, 1 + p, p, TN) for p in range(n)], [(tka, tn)] * n, [],
        [((Ka, N), out_dtype, (tka, tn), lambda i, j, k: (i, j))] * n, _store_epilogue)


def _matmul_tn_cols(name, a, b_list, n_shards, out_dtype, tka=512, tkm=4096, after=None):
    M, Ka = a.shape
    Nc = b_list[0].shape[1] // n_shards
    tka, tkm = _tile(Ka, tka, LANES), _tile(M, tkm, 16)
    n = len(b_list)
    operands = [(a, (tkm, tka), lambda i, j, k: (k, i))]
    operands += [(b, (tkm, Nc), lambda i, j, k: (k, j)) for b in b_list]
    return _fused_matmul(
        name, (Ka // tka, n_shards, M // tkm), operands,
        [(0, 1 + p, p, TN) for p in range(n)], [(tka, Nc)] * n, _order_after(after),
        [((n_shards, Ka, Nc), out_dtype, (None, tka, Nc), lambda i, j, k: (j, i, 0))] * n, _store_epilogue)


def _rmsnorm(name, x, g_row, out_dtype, tm=512, after=None):
    S, D = x.shape
    tm = _tile(S, tm, 16)

    def body(x_ref, g_ref, *rest):
        o_ref = rest[-1]
        xv = x_ref[...]
        o_ref[...] = (xv * _rsqrt_mean_sq(xv) * g_ref[...]).astype(o_ref.dtype)

    ordered = [] if after is None else [after]
    return pl.pallas_call(
        body, name=name, grid=(S // tm,),
        in_specs=[pl.BlockSpec((tm, D), lambda i: (i, 0)), pl.BlockSpec((1, D), lambda i: (0, 0))] + [ANY] * len(ordered),
        out_specs=pl.BlockSpec((tm, D), lambda i: (i, 0)),
        out_shape=jax.ShapeDtypeStruct((S, D), out_dtype),
        compiler_params=_params(1),
    )(x, g_row, *ordered)


def _rmsnorm_bwd_values(dh, xin, g):
    r = _rsqrt_mean_sq(xin)
    xhat = xin * r
    dhg = dh * g
    dx = r * (dhg - xhat * jnp.mean(dhg * xhat, axis=-1, keepdims=True))
    return dx, jnp.sum(dh * xhat, axis=0, keepdims=True)


def _rmsnorm_bwd(name, dh, xin, g_row, dres, low_dtype, tm=256):
    S, D = xin.shape
    tm = _tile(S, tm, 16)
    has_dx = dres is not None
    has_low = has_dx and low_dtype is not None

    def body(*refs):
        dh_ref, x_ref, g_ref = refs[:3]
        rest = refs[3:]
        dx, dg = _rmsnorm_bwd_values(dh_ref[...], x_ref[...], g_ref[...])
        if has_dx:
            dres_ref, rest = rest[0], rest[1:]
            dx = dx + dres_ref[...]
            rest[0][...] = dx
            if has_low:
                rest[1][...] = dx.astype(low_dtype)
        dg_ref = rest[-1]

        @pl.when(pl.program_id(0) == 0)
        def _():
            dg_ref[...] = jnp.zeros_like(dg_ref)

        dg_ref[...] += dg

    row = pl.BlockSpec((tm, D), lambda i: (i, 0))
    vec = pl.BlockSpec((1, D), lambda i: (0, 0))
    ins, in_specs = [dh, xin, g_row], [row, row, vec]
    out_shape, out_specs = [], []
    if has_dx:
        ins.append(dres)
        in_specs.append(row)
        out_shape.append(jax.ShapeDtypeStruct((S, D), jnp.float32))
        out_specs.append(row)
        if has_low:
            out_shape.append(jax.ShapeDtypeStruct((S, D), low_dtype))
            out_specs.append(row)
    out_shape.append(jax.ShapeDtypeStruct((1, D), jnp.float32))
    out_specs.append(vec)
    return pl.pallas_call(
        body, name=name, grid=(S // tm,), in_specs=in_specs, out_specs=out_specs, out_shape=out_shape,
        compiler_params=_params(1),
    )(*ins)


def _chunk_mask(blk):
    t = lax.broadcasted_iota(jnp.int32, (blk, blk), 0)
    s = lax.broadcasted_iota(jnp.int32, (blk, blk), 1)
    return (s // CHUNK) <= (t // CHUNK)


def _pool_inputs(a, halo, row0, tm, g, pg):
    cols = slice(g * pg, (g + 1) * pg)
    w = POOL_WINDOWS[g]
    s = jnp.concatenate([halo[:, cols], a[:, cols]], axis=0)
    span = 1
    while span < w:
        s = s + pltpu.roll(s, span, axis=0)
        span *= 2
    t = lax.broadcasted_iota(jnp.int32, (tm, 1), 0) + row0
    cnt = jnp.minimum(t + 1, w).astype(jnp.float32)
    return s[POOL_HALO:] / cnt - a[:, cols]


def _mixer_specs(S, tm, DP, DS):
    per_halo = tm // POOL_HALO
    n_halo = S // POOL_HALO
    a_spec = pl.BlockSpec((tm, DP), lambda i: (i, 0))
    prev_spec = pl.BlockSpec((POOL_HALO, DP), lambda i: (jnp.maximum(i * per_halo - 1, 0), 0))
    next_spec = pl.BlockSpec((POOL_HALO, DP), lambda i: (jnp.minimum((i + 1) * per_halo, n_halo - 1), 0))
    u_spec = pl.BlockSpec((tm, DS), lambda i: (i, 1))
    v_spec = pl.BlockSpec((tm, DS), lambda i: (i, 2))
    return a_spec, prev_spec, next_spec, u_spec, v_spec


def _whole(arr):
    nd = arr.ndim
    return pl.BlockSpec(arr.shape, lambda i: (0,) * nd)


def _mixer_fwd(proj, pw, ps_row, gs_row, ws, b_full, tm=512):
    S = proj.shape[0]
    G, PG, _ = pw.shape
    DP = G * PG
    H, BLK, _ = ws.shape
    DS = gs_row.shape[1]
    HD = DS // H
    assert DP == DS and proj.shape[1] == DP + 2 * DS
    tm = _tile(S, tm, BLK)

    def body(a_ref, prev_ref, u_ref, v_ref, pw_ref, ps_ref, gs_ref, ws_ref, b_ref, z_ref):
        i = pl.program_id(0)
        a = a_ref[...]
        halo = jnp.where(i > 0, prev_ref[...], 0.0)
        for g in range(G):
            cols = slice(g * PG, (g + 1) * PG)
            p = _pool_inputs(a, halo, i * tm, tm, g, PG)
            y = jnp.dot(p.astype(MXU_DTYPE), pw_ref[g], preferred_element_type=jnp.float32) * ps_ref[:, cols]
            z_ref[:, cols] = y.astype(z_ref.dtype)
        v = v_ref[...]
        vn = (v * _rsqrt_mean_sq(v) * gs_ref[...]).astype(MXU_DTYPE)
        u = u_ref[...]
        mask = _chunk_mask(BLK)
        for h in range(H):
            wm = jnp.where(mask, ws_ref[h], 0.0).astype(MXU_DTYPE)
            bias = b_ref[h]
            cols = slice(h * HD, (h + 1) * HD)
            for n in range(tm // BLK):
                rows = slice(n * BLK, (n + 1) * BLK)
                mixed = jnp.dot(wm, vn[rows, cols], preferred_element_type=jnp.float32) + bias
                z_ref[rows, DP + h * HD:DP + (h + 1) * HD] = (u[rows, cols] * mixed).astype(z_ref.dtype)

    a_spec, prev_spec, _, u_spec, v_spec = _mixer_specs(S, tm, DP, DS)
    return pl.pallas_call(
        body, name="mixer_fwd", grid=(S // tm,),
        in_specs=[a_spec, prev_spec, u_spec, v_spec, _whole(pw), _whole(ps_row), _whole(gs_row), _whole(ws), _whole(b_full)],
        out_specs=pl.BlockSpec((tm, DP + DS), lambda i: (i, 0)),
        out_shape=jax.ShapeDtypeStruct((S, DP + DS), MXU_DTYPE),
        compiler_params=_params(1),
    )(proj, proj, proj, proj, pw, ps_row, gs_row, ws, b_full)


def _mixer_bwd(proj, dz, pw, ps_row, gs_row, ws, b_full, tm=256):
    S = proj.shape[0]
    G, PG, _ = pw.shape
    DP = G * PG
    H, BLK, _ = ws.shape
    DS = gs_row.shape[1]
    HD = DS // H
    tm = _tile(S, tm, BLK)
    nb = S // tm
    ext = tm + POOL_HALO

    def body(a_ref, prev_ref, u_ref, v_ref, dzp_ref, dzp_next_ref, dzs_ref, pw_ref, ps_ref, gs_ref, ws_ref, b_ref,
             dproj_ref, dpw_ref, dps_ref, dgs_ref, dws_ref, db_ref, dvn_ref):
        i = pl.program_id(0)

        @pl.when(i == 0)
        def _():
            for r in (dpw_ref, dps_ref, dgs_ref, dws_ref, db_ref):
                r[...] = jnp.zeros_like(r)

        a = a_ref[...]
        halo = jnp.where(i > 0, prev_ref[...], 0.0)
        dyp = dzp_ref[...]
        dyp_next = jnp.where(i < nb - 1, dzp_next_ref[...], 0.0)
        dq_ext = jnp.concatenate([dyp, dyp_next], axis=0) * ps_ref[...]
        t_ext = lax.broadcasted_iota(jnp.int32, (ext, 1), 0) + i * tm
        for g in range(G):
            cols = slice(g * PG, (g + 1) * PG)
            w = POOL_WINDOWS[g]
            pb = _pool_inputs(a, halo, i * tm, tm, g, PG).astype(MXU_DTYPE)
            y_pre = jnp.dot(pb, pw_ref[g], preferred_element_type=jnp.float32)
            dps_ref[:, cols] += jnp.sum(dyp[:, cols] * y_pre, axis=0, keepdims=True)
            dqb = dq_ext[:, cols].astype(MXU_DTYPE)
            dpw_ref[g] += lax.dot_general(pb, dqb[:tm], TN, preferred_element_type=jnp.float32)
            dp = lax.dot_general(dqb, pw_ref[g], NT, preferred_element_type=jnp.float32)
            f = dp / jnp.minimum(t_ext + 1, w).astype(jnp.float32)
            span = 1
            while span < w:
                f = f + pltpu.roll(f, ext - span, axis=0)
                span *= 2
            dproj_ref[:, cols] = (f[:tm] - dp[:tm]).astype(dproj_ref.dtype)

        v = v_ref[...]
        rv = _rsqrt_mean_sq(v)
        vhat = v * rv
        gs = gs_ref[...]
        vnb = (vhat * gs).astype(MXU_DTYPE)
        u = u_ref[...]
        dys = dzs_ref[...]
        mask = _chunk_mask(BLK)
        for h in range(H):
            wm = jnp.where(mask, ws_ref[h], 0.0).astype(MXU_DTYPE)
            bias = b_ref[h]
            cols = slice(h * HD, (h + 1) * HD)
            dm_sum = jnp.zeros((BLK, HD), jnp.float32)
            dws_sum = jnp.zeros((BLK, BLK), jnp.float32)
            for n in range(tm // BLK):
                rows = slice(n * BLK, (n + 1) * BLK)
                vblk = vnb[rows, cols]
                mixed = jnp.dot(wm, vblk, preferred_element_type=jnp.float32) + bias
                dy = dys[rows, cols]
                dproj_ref[rows, DP + h * HD:DP + (h + 1) * HD] = (dy * mixed).astype(dproj_ref.dtype)
                dm = dy * u[rows, cols]
                dm_sum = dm_sum + dm
                dmb = dm.astype(MXU_DTYPE)
                dws_sum = dws_sum + lax.dot_general(dmb, vblk, NT, preferred_element_type=jnp.float32)
                dvn_ref[rows, cols] = lax.dot_general(wm, dmb, TN, preferred_element_type=jnp.float32)
            dws_ref[h] += jnp.where(mask, dws_sum, 0.0)
            db_ref[h:h + 1, :] += jnp.sum(dm_sum.T, axis=0, keepdims=True)
        dvn = dvn_ref[...]
        dvg = dvn * gs
        dv = rv * (dvg - vhat * jnp.mean(dvg * vhat, axis=-1, keepdims=True))
        dgs_ref[...] += jnp.sum(dvn * vhat, axis=0, keepdims=True)
        dproj_ref[:, DP + DS:] = dv.astype(dproj_ref.dtype)

    a_spec, prev_spec, next_spec, u_spec, v_spec = _mixer_specs(S, tm, DP, DS)
    dzp_spec = pl.BlockSpec((tm, DP), lambda i: (i, 0))
    dzs_spec = pl.BlockSpec((tm, DS), lambda i: (i, 1))
    small = [pw, ps_row, gs_row, ws, b_full]
    out_shape = [
        jax.ShapeDtypeStruct((S, DP + 2 * DS), MXU_DTYPE),
        jax.ShapeDtypeStruct(pw.shape, jnp.float32),
        jax.ShapeDtypeStruct(ps_row.shape, jnp.float32),
        jax.ShapeDtypeStruct(gs_row.shape, jnp.float32),
        jax.ShapeDtypeStruct(ws.shape, jnp.float32),
        jax.ShapeDtypeStruct((H, BLK), jnp.float32),
    ]
    out_specs = [pl.BlockSpec((tm, DP + 2 * DS), lambda i: (i, 0))] + [_whole(s) for s in out_shape[1:]]
    return pl.pallas_call(
        body, name="mixer_bwd", grid=(nb,),
        in_specs=[a_spec, prev_spec, u_spec, v_spec, dzp_spec, next_spec, dzs_spec] + [_whole(s) for s in small],
        out_specs=out_specs, out_shape=out_shape,
        scratch_shapes=[pltpu.VMEM((tm, DS), jnp.float32)],
        compiler_params=_params(1),
    )(proj, proj, proj, proj, dz, dz, dz, *small)


def _softmax_rows(q, k, scale):
    s = lax.dot_general(q, k, NT, preferred_element_type=jnp.float32) * scale
    e = jnp.exp(s - jnp.max(s, axis=-1, keepdims=True))
    return e / jnp.sum(e, axis=-1, keepdims=True)


def _attn_fwd(q, k, v, tm=512):
    S, D = q.shape
    M = k.shape[0]
    HD = D // N_XATTN_HEADS
    scale = HD ** -0.5
    tm = _tile(S, tm, 16)

    def body(q_ref, k_ref, v_ref, o_ref):
        for h in range(N_XATTN_HEADS):
            cols = slice(h * HD, (h + 1) * HD)
            p = _softmax_rows(q_ref[:, cols], k_ref[:, cols], scale)
            o = jnp.dot(p.astype(MXU_DTYPE), v_ref[:, cols], preferred_element_type=jnp.float32)
            o_ref[:, cols] = o.astype(o_ref.dtype)

    row = pl.BlockSpec((tm, D), lambda i: (i, 0))
    mem = pl.BlockSpec((M, D), lambda i: (0, 0))
    return pl.pallas_call(
        body, name="attn_fwd", grid=(S // tm,), in_specs=[row, mem, mem], out_specs=row,
        out_shape=jax.ShapeDtypeStruct((S, D), MXU_DTYPE), compiler_params=_params(1),
    )(q, k, v)


def _attn_bwd(q, k, v, do, tm=512):
    S, D = q.shape
    M = k.shape[0]
    HD = D // N_XATTN_HEADS
    scale = HD ** -0.5
    tm = _tile(S, tm, 16)

    def body(q_ref, k_ref, v_ref, do_ref, dq_ref, dk_ref, dv_ref):
        @pl.when(pl.program_id(0) == 0)
        def _():
            dk_ref[...] = jnp.zeros_like(dk_ref)
            dv_ref[...] = jnp.zeros_like(dv_ref)

        for h in range(N_XATTN_HEADS):
            cols = slice(h * HD, (h + 1) * HD)
            qh, kh, vh, doh = q_ref[:, cols], k_ref[:, cols], v_ref[:, cols], do_ref[:, cols]
            p = _softmax_rows(qh, kh, scale)
            dp = lax.dot_general(doh, vh, NT, preferred_element_type=jnp.float32)
            dv_ref[:, cols] += lax.dot_general(p.astype(MXU_DTYPE), doh, TN, preferred_element_type=jnp.float32)
            ds = (p * (dp - jnp.sum(dp * p, axis=-1, keepdims=True)) * scale).astype(MXU_DTYPE)
            dq_ref[:, cols] = jnp.dot(ds, kh, preferred_element_type=jnp.float32).astype(dq_ref.dtype)
            dk_ref[:, cols] += lax.dot_general(ds, qh, TN, preferred_element_type=jnp.float32)

    row = pl.BlockSpec((tm, D), lambda i: (i, 0))
    mem = pl.BlockSpec((M, D), lambda i: (0, 0))
    return pl.pallas_call(
        body, name="attn_bwd", grid=(S // tm,), in_specs=[row, mem, mem, row], out_specs=[row, mem, mem],
        out_shape=[jax.ShapeDtypeStruct((S, D), MXU_DTYPE), jax.ShapeDtypeStruct((M, D), jnp.float32),
                   jax.ShapeDtypeStruct((M, D), jnp.float32)],
        compiler_params=_params(1),
    )(q, k, v, do)


def _proj_residual_norm(name, a, w2d, resid, g_row, tm=256, after=None):
    M, K = a.shape
    N = w2d.shape[1]
    tm = _tile(M, tm, 16)

    def epilogue(accs, ex_refs, out_refs, i, j):
        x_new = ex_refs[0][...] + accs[0]
        out_refs[0][...] = x_new
        out_refs[1][...] = (x_new * _rsqrt_mean_sq(x_new) * ex_refs[1][...]).astype(MXU_DTYPE)

    row = lambda i, j, k: (i, 0)
    return _fused_matmul(
        name, (M // tm, 1, 1),
        [(a, (tm, K), row), (w2d, (K, N), lambda i, j, k: (0, 0))],
        [(0, 1, 0, NN)], [(tm, N)],
        [(resid, (tm, N), row), (g_row, (1, N), lambda i, j, k: (0, 0))] + _order_after(after),
        [((M, N), jnp.float32, (tm, N), row), ((M, N), MXU_DTYPE, (tm, N), row)], epilogue)


def _ffn_gate_up(h, wg3, wu3, tm=512):
    M, K = h.shape
    J, _, Nc = wg3.shape
    tm = _tile(M, tm, 16)

    def epilogue(accs, ex_refs, out_refs, i, j):
        gate, up = accs
        sg = _sigmoid(gate)
        silu = gate * sg
        out_refs[0][...] = (silu * up).astype(MXU_DTYPE)
        out_refs[1][...] = (up * (sg * (1.0 + gate * (1.0 - sg)))).astype(MXU_DTYPE)
        out_refs[2][...] = silu.astype(MXU_DTYPE)

    col = lambda i, j, k: (i, j)
    wspec = lambda i, j, k: (j, 0, 0)
    return _fused_matmul(
        "ffn_gate_up", (M // tm, J, 1),
        [(h, (tm, K), lambda i, j, k: (i, 0)), (wg3, (None, K, Nc), wspec), (wu3, (None, K, Nc), wspec)],
        [(0, 1, 0, NN), (0, 2, 1, NN)], [(tm, Nc)] * 2, [],
        [((M, J * Nc), MXU_DTYPE, (tm, Nc), col)] * 3, epilogue)


def _ffn_down(act, wd2d, x2, tm=512, tn=512):
    M, F = act.shape
    D = wd2d.shape[1]
    tm, tn = _tile(M, tm, 16), _tile(D, tn, LANES)

    def epilogue(accs, ex_refs, out_refs, i, j):
        out_refs[0][...] = ex_refs[0][...] + accs[0]

    tile = lambda i, j, k: (i, j)
    return _fused_matmul(
        "ffn_down", (M // tm, D // tn, 1),
        [(act, (tm, F), lambda i, j, k: (i, 0)), (wd2d, (F, tn), lambda i, j, k: (0, j))],
        [(0, 1, 0, NN)], [(tm, tn)], [(x2, (tm, tn), tile)],
        [((M, D), jnp.float32, (tm, tn), tile)], epilogue)[0]


def _final_norm_loss(x3, target, g_row, tm=256):
    S, D = x3.shape
    tm = _tile(S, tm, 16)

    def body(x_ref, t_ref, g_ref, dx_ref, dxl_ref, loss_ref, dg_ref):
        x3v = x_ref[...]
        g = g_ref[...]
        r = _rsqrt_mean_sq(x3v)
        xhat = x3v * r
        diff = xhat * g - t_ref[...]
        dy = diff / D
        dyg = dy * g
        dx = r * (dyg - xhat * jnp.mean(dyg * xhat, axis=-1, keepdims=True))
        dx_ref[...] = dx
        dxl_ref[...] = dx.astype(MXU_DTYPE)

        @pl.when(pl.program_id(0) == 0)
        def _():
            loss_ref[...] = jnp.zeros_like(loss_ref)
            dg_ref[...] = jnp.zeros_like(dg_ref)

        loss_ref[...] += 0.5 * _sum_all(jnp.mean(diff * diff, axis=-1, keepdims=True))
        dg_ref[...] += jnp.sum(dy * xhat, axis=0, keepdims=True)

    row = pl.BlockSpec((tm, D), lambda i: (i, 0))
    vec = pl.BlockSpec((1, D), lambda i: (0, 0))
    return pl.pallas_call(
        body, name="final_norm_loss", grid=(S // tm,), in_specs=[row, row, vec],
        out_specs=[row, row, pl.BlockSpec((1, 1), lambda i: (0, 0)), vec],
        out_shape=[jax.ShapeDtypeStruct((S, D), jnp.float32), jax.ShapeDtypeStruct((S, D), MXU_DTYPE),
                   jax.ShapeDtypeStruct((1, 1), jnp.float32), jax.ShapeDtypeStruct((1, D), jnp.float32)],
        compiler_params=_params(1),
    )(x3, target, g_row)


def _ffn_down_bwd(dx3_low, wd2d, act_by_gate, act_by_up, n_shards, tm=512):
    M, D = dx3_low.shape
    F = wd2d.shape[0]
    tn = F // n_shards
    tm = _tile(M, tm, 16)

    def epilogue(accs, ex_refs, out_refs, i, j):
        dact = accs[0]
        out_refs[0][...] = (dact * ex_refs[0][...].astype(jnp.float32)).astype(MXU_DTYPE)
        out_refs[1][...] = (dact * ex_refs[1][...].astype(jnp.float32)).astype(MXU_DTYPE)

    col = lambda i, j, k: (i, j)
    return _fused_matmul(
        "ffn_down_bwd", (M // tm, n_shards, 1),
        [(dx3_low, (tm, D), lambda i, j, k: (i, 0)), (wd2d, (tn, D), lambda i, j, k: (j, 0))],
        [(0, 1, 0, NT)], [(tm, tn)],
        [(act_by_gate, (tm, tn), col), (act_by_up, (tm, tn), col)],
        [((M, F), MXU_DTYPE, (tm, tn), col)] * 2, epilogue)


def _matmul_nt_norm_bwd(name, a, b, xin, g_row, dres, low_dtype, tm=256, after=None):
    M = a.shape[0]
    sharded = b.ndim == 3
    N, Kc = b.shape[-2:]
    J = b.shape[0] if sharded else 1
    tm = _tile(M, tm, 16)
    has_low = low_dtype is not None

    def epilogue(accs, ex_refs, out_refs, i, j):
        dx, dg = _rmsnorm_bwd_values(accs[0], ex_refs[0][...], ex_refs[1][...])
        dx = dx + ex_refs[2][...]
        out_refs[0][...] = dx
        if has_low:
            out_refs[1][...] = dx.astype(low_dtype)
        dg_ref = out_refs[-1]

        @pl.when(i == 0)
        def _():
            dg_ref[...] = jnp.zeros_like(dg_ref)

        dg_ref[...] += dg

    row = lambda i, j, k: (i, 0)
    const = lambda i, j, k: (0, 0)
    b_operand = (b, (None, N, Kc), lambda i, j, k: (k, 0, 0)) if sharded else (b, (N, Kc), const)
    outs = [((M, N), jnp.float32, (tm, N), row)]
    if has_low:
        outs.append(((M, N), low_dtype, (tm, N), row))
    outs.append(((1, N), jnp.float32, (1, N), const))
    return _fused_matmul(
        name, (M // tm, 1, J), [(a, (tm, Kc), lambda i, j, k: (i, k)), b_operand],
        [(0, 1, 0, NT)], [(tm, N)],
        [(xin, (tm, N), row), (g_row, (1, N), const), (dres, (tm, N), row)] + _order_after(after),
        outs, epilogue)


def _row_tile(R, C, target_bytes=2 ** 21):
    return _tile(R, max(16, target_bytes // (4 * C)), 16)


def _prefetch_call(body, name, grid, pos, in_specs, out_specs, out_shape, ins, after=None):
    ordered = [] if after is None else [after]
    grid_spec = pltpu.PrefetchScalarGridSpec(
        num_scalar_prefetch=1, grid=grid, in_specs=list(in_specs) + [ANY] * len(ordered), out_specs=out_specs)
    return pl.pallas_call(body, name=name, grid_spec=grid_spec, out_shape=out_shape,
                          compiler_params=_params(len(grid)))(pos, *ins, *ordered)


def _runs(arrays):
    runs = []
    for i, a in enumerate(arrays):
        if runs and arrays[runs[-1][-1]].shape == a.shape:
            runs[-1].append(i)
        else:
            runs.append([i])
    return runs


def _per_run(fn, names, *array_lists):
    out = [None] * len(names)
    for run in _runs(array_lists[0]):
        results = fn("_".join(names[i] for i in run), *[[arrays[i] for i in run] for arrays in array_lists])
        for i, r in zip(run, results):
            out[i] = r
    return out


def _cast_into_slabs(name, xs, dtype, pos, after=None):
    n = len(xs)
    R, C = xs[0].shape
    tr = _row_tile(R, C * n, 2 ** 22)

    def body(pos_ref, *refs):
        for t in range(n):
            refs[len(refs) - n + t][...] = refs[t][...].astype(dtype)

    return _prefetch_call(
        body, "cast_" + name, (R // tr,), pos, [pl.BlockSpec((tr, C), lambda i, pos: (i, 0))] * n,
        [pl.BlockSpec((None, tr, C), lambda i, pos: (pos[0], i, 0))] * n,
        [jax.ShapeDtypeStruct((N_CHIPS, R, C), dtype)] * n, xs, after)


def _chip_sum(name, partials, gots, dtype, pos):
    n = len(partials)
    J, R, C = partials[0].shape
    Rh = R // 2
    tr = _row_tile(Rh, C * n, 2 ** 22)

    def body(pos_ref, *refs):
        for t in range(n):
            refs[2 * n + t][...] = (refs[t][...].astype(jnp.float32) + refs[n + t][...].astype(jnp.float32)).astype(dtype)

    blk = pl.BlockSpec((None, tr, C), lambda j, i, pos: (j, i, 0))
    return _prefetch_call(
        body, "chip_sum_" + name, (J, Rh // tr), pos,
        [pl.BlockSpec((None, None, tr, C), lambda j, i, pos: (j, pos[1], i, 0))] * n + [blk] * n, [blk] * n,
        [jax.ShapeDtypeStruct((J, Rh, C), dtype)] * n, [p.reshape(J, 2, Rh, C) for p in partials] + list(gots))


def _owner_sum(name, chip_sums, others, pos):
    n = len(chip_sums)
    J, Rh, C = chip_sums[0].shape
    n_other = others[0].shape[0]
    tr = _row_tile(Rh, C * (n_other + 1) * n, 2 ** 22)

    def body(pos_ref, *refs):
        for t in range(n):
            acc = refs[t][...].astype(jnp.float32)
            for k in range(n_other):
                acc = acc + refs[n + t][k].astype(jnp.float32)
            refs[2 * n + t][...] = acc

    return _prefetch_call(
        body, "owner_sum_" + name, (Rh // tr,), pos,
        [pl.BlockSpec((None, tr, C), lambda i, pos: (pos[0], i, 0))] * n
        + [pl.BlockSpec((n_other, tr, C), lambda i, pos: (0, i, 0))] * n,
        [pl.BlockSpec((None, tr, C), lambda i, pos: (pos[1], i, 0))] * n,
        [jax.ShapeDtypeStruct((2, Rh, C), jnp.float32)] * n, list(chip_sums) + list(others))


def _adamw_values(w, g, m, v):
    m = ADAM_B1 * m + (1.0 - ADAM_B1) * g
    v = ADAM_B2 * v + (1.0 - ADAM_B2) * (g * g)
    m_hat = m / (1.0 - ADAM_B1 ** ADAM_STEP)
    v_hat = v / (1.0 - ADAM_B2 ** ADAM_STEP)
    delta = -ADAM_LR * (m_hat / (jnp.sqrt(v_hat) + ADAM_EPS) + ADAM_WD * w)
    return delta, m, v


def _adamw(name, ws, gs, ms, vs):
    n = len(ws)
    R, C = ws[0].shape
    tr = _row_tile(R, C * n, 2 ** 20)

    def body(*refs):
        w_refs, g_refs, m_refs, v_refs = (refs[q * n:(q + 1) * n] for q in range(4))
        outs = refs[4 * n:]
        for t in range(n):
            g = g_refs[t][...]
            outs[4 * t][...] = g
            outs[4 * t + 1][...], outs[4 * t + 2][...], outs[4 * t + 3][...] = _adamw_values(
                w_refs[t][...], g, m_refs[t][...], v_refs[t][...])

    blk = pl.BlockSpec((tr, C), lambda i: (i, 0))
    res = pl.pallas_call(
        body, name="adamw_" + name, grid=(R // tr,), in_specs=[blk] * (4 * n), out_specs=[blk] * (4 * n),
        out_shape=[jax.ShapeDtypeStruct((R, C), jnp.float32)] * (4 * n), compiler_params=_params(1),
    )(*ws, *gs, *ms, *vs)
    return [tuple(res[4 * t:4 * t + 4]) for t in range(n)]


def _adamw_small(g_packed, ws, ms, vs, row_counts):
    n = len(ws)

    def body(*refs):
        g_ref = refs[0]
        w_refs, m_refs, v_refs = refs[1:1 + n], refs[1 + n:1 + 2 * n], refs[1 + 2 * n:1 + 3 * n]
        outs = refs[1 + 3 * n:]
        off = 0
        for t in range(n):
            g = g_ref[off:off + row_counts[t], :]
            off += row_counts[t]
            delta, m_new, v_new = _adamw_values(w_refs[t][...], g, m_refs[t][...], v_refs[t][...])
            outs[4 * t][...] = g
            outs[4 * t + 1][...] = delta
            outs[4 * t + 2][...] = m_new
            outs[4 * t + 3][...] = v_new

    out_shape = []
    for t in range(n):
        out_shape += [jax.ShapeDtypeStruct((row_counts[t], LANES), jnp.float32)] * 4
    return pl.pallas_call(
        body, name="adamw_small", in_specs=[VMEM_WHOLE] * (1 + 3 * n), out_specs=[VMEM_WHOLE] * (4 * n),
        out_shape=out_shape, compiler_params=pltpu.CompilerParams(vmem_limit_bytes=VMEM_LIMIT_BYTES),
    )(g_packed, *ws, *ms, *vs)


def _position():
    x, y, c = lax.axis_index("x"), lax.axis_index("y"), lax.axis_index("c")
    other_chips = [(1 - x, y), (x, 1 - y), (1 - x, 1 - y)]
    return x, y, c, other_chips


def _half(rows, which):
    return pl.ds(which * (rows // 2), rows // 2)


def _chunk_index(shape, dtype):
    rows = shape[-2]
    total = math.prod(shape) * jnp.dtype(dtype).itemsize
    n = max(1, min(rows // COPY_ROW_ALIGN, total // COPY_CHUNK_BYTES))
    step = -(-rows // n)
    step = -(-step // COPY_ROW_ALIGN) * COPY_ROW_ALIGN
    lead = (slice(None),) * (len(shape) - 2)
    return [lead + (pl.ds(s, min(step, rows - s)),) for s in range(0, rows, step)]


def _start_in_chunks(make, src, dst, waited_here=True):
    for idx in _chunk_index(src.shape, src.dtype):
        make(src.at[idx], dst.at[idx]).start()
    return make(src, dst) if waited_here else None


def _remote_copier(send_sem, recv_sem, to):
    return lambda src, dst: pltpu.make_async_remote_copy(
        src_ref=src, dst_ref=dst, send_sem=send_sem, recv_sem=recv_sem, device_id=to, device_id_type=MESH)


def _slab_half(ref, chip, which):
    return ref.at[chip, _half(ref.shape[1], which)]


def _slab_quarter(ref, chip, which, q):
    rows = ref.shape[1] // 4
    return ref.at[chip, pl.ds(which * 2 * rows + q * rows, rows)]


def _gather_step(name, arrived=(), ring=(), direct=(), after=None):
    groups = [g[2] for g in arrived] + list(ring) + list(direct)
    per_tensor = [2] * (len(arrived) + len(ring)) + [3] * len(direct)
    flat = [a for g in groups for a in g]
    T, G, A = len(flat), len(groups), len(arrived)
    ordered = [] if after is None else [after]
    sems_in = [sem for g in arrived for sem in g[:2]]

    def body(*refs):
        ins = refs[:T]
        first_hop = refs[T:T + 2 * A]
        out_sems = refs[T + 2 * A + len(ordered):T + 2 * A + len(ordered) + 2 * G]
        token = refs[-1]
        x, y, c, other_chips = _position()
        mine = 2 * x + y
        x_nbr, y_nbr = (1 - x, y, c), (x, 1 - y, c)
        from_x, from_y = 2 * (1 - x) + y, 2 * x + (1 - y)
        t0 = 0
        for gi, group in enumerate(groups):
            send, recv = out_sems[2 * gi], out_sems[2 * gi + 1]
            for t in range(len(group)):
                ref = ins[t0 + t]
                own = _slab_half(ref, mine, c)
                if gi < A:
                    send1, recv1 = first_hop[2 * gi], first_hop[2 * gi + 1]
                    for k, (peer, chip) in enumerate(((x_nbr, from_x), (y_nbr, from_y))):
                        cp = _remote_copier(send1.at[2 * t + k], recv1.at[2 * t + k], peer)(own, _slab_half(ref, chip, c))
                        cp.wait_send()
                        cp.wait_recv()
                    for k, (peer, chip) in enumerate(((y_nbr, from_x), (x_nbr, from_y))):
                        piece = _slab_quarter(ref, chip, c, k)
                        _start_in_chunks(_remote_copier(send.at[2 * t + k], recv.at[2 * t + k], peer), piece, piece,
                                         waited_here=False)
                elif gi < A + len(ring):
                    for k, peer in enumerate((x_nbr, y_nbr)):
                        _start_in_chunks(_remote_copier(send.at[2 * t + k], recv.at[2 * t + k], peer), own, own,
                                         waited_here=False)
                else:
                    for k, (cx, cy) in enumerate(other_chips):
                        _start_in_chunks(_remote_copier(send.at[3 * t + k], recv.at[3 * t + k], (cx, cy, c)), own, own,
                                         waited_here=False)
            t0 += len(group)
        token[...] = jnp.zeros_like(token)

    sem_shapes = []
    for group, n in zip(groups, per_tensor):
        sem_shapes += [pltpu.SemaphoreType.DMA((n * len(group),))] * 2
    res = pl.pallas_call(
        body, name=name, in_specs=[HBM_SPEC] * T + [SEM_SPEC] * (2 * A) + [ANY] * len(ordered),
        out_specs=[SEM_SPEC] * (2 * G) + [HBM_SPEC] * T + [VMEM_WHOLE],
        out_shape=sem_shapes + [pltpu.HBM(a.shape, a.dtype) for a in flat] + [jax.ShapeDtypeStruct((8, LANES), jnp.float32)],
        input_output_aliases={t: 2 * G + t for t in range(T)},
        compiler_params=pltpu.CompilerParams(has_side_effects=SIDE_EFFECT),
    )(*[pltpu.with_memory_space_constraint(a, pltpu.HBM) for a in flat], *sems_in, *ordered)
    flights, t0 = [], 2 * G
    for gi, group in enumerate(groups):
        flights.append((res[2 * gi], res[2 * gi + 1], res[t0:t0 + len(group)]))
        t0 += len(group)
    return flights[:A], flights[A:A + len(ring)], flights[A + len(ring):], res[-1]


def _relayed_wait(name, flight, after):
    send_sems, recv_sems, slabs = flight
    T = len(slabs)

    def body(*refs):
        ins = refs[:T]
        send, recv = refs[T:T + 2]
        x, y, c, _ = _position()
        x_nbr, y_nbr = (1 - x, y, c), (x, 1 - y, c)
        from_x, from_y, diagonal = 2 * (1 - x) + y, 2 * x + (1 - y), 2 * (1 - x) + (1 - y)
        for t in range(T):
            for k, (peer, chip) in enumerate(((y_nbr, from_x), (x_nbr, from_y))):
                cp = _remote_copier(send.at[2 * t + k], recv.at[2 * t + k], peer)(
                    _slab_quarter(ins[t], chip, c, k), _slab_quarter(ins[t], diagonal, c, k))
                cp.wait_send()
                cp.wait_recv()

    return pl.pallas_call(
        body, name=name, in_specs=[HBM_SPEC] * T + [SEM_SPEC, SEM_SPEC, ANY], out_specs=[HBM_SPEC] * T,
        out_shape=[pltpu.HBM(a.shape, a.dtype) for a in slabs], input_output_aliases={t: t for t in range(T)},
        compiler_params=pltpu.CompilerParams(has_side_effects=SIDE_EFFECT),
    )(*slabs, send_sems, recv_sems, after)


def _gather_wait(name, flight, after):
    send_sems, recv_sems, slabs = flight
    T = len(slabs)

    def body(*refs):
        ins = refs[:T]
        send, recv = refs[T:T + 2]
        x, y, c, other_chips = _position()
        mine = 2 * x + y
        for t in range(T):
            own = _slab_half(ins[t], mine, c)
            for k, (cx, cy) in enumerate(other_chips):
                landed = _slab_half(ins[t], 2 * cx + cy, c)
                cp = _remote_copier(send.at[3 * t + k], recv.at[3 * t + k], (cx, cy, c))(own, landed)
                cp.wait_send()
                cp.wait_recv()

    return pl.pallas_call(
        body, name=name, in_specs=[HBM_SPEC] * T + [SEM_SPEC, SEM_SPEC, ANY], out_specs=[HBM_SPEC] * T,
        out_shape=[pltpu.HBM(s.shape, s.dtype) for s in slabs], input_output_aliases={t: t for t in range(T)},
        compiler_params=pltpu.CompilerParams(has_side_effects=SIDE_EFFECT),
    )(*slabs, send_sems, recv_sems, after)


def _share_with_sibling(name, slabs):
    T = len(slabs)

    def body(*refs):
        outs = refs[T:2 * T]
        send_sems, recv_sems = refs[2 * T:]
        x, y, c, other_chips = _position()
        sibling = (x, y, 1 - c)
        started = []
        for t in range(T):
            for k, (cx, cy) in enumerate(other_chips):
                landed = _slab_half(outs[t], 2 * cx + cy, c)
                started.append(_start_in_chunks(_remote_copier(send_sems.at[t, k], recv_sems.at[t, k], sibling), landed, landed))
        for t in range(T):
            for k, (cx, cy) in enumerate(other_chips):
                arriving = _slab_half(outs[t], 2 * cx + cy, 1 - c)
                _remote_copier(send_sems.at[t, k], recv_sems.at[t, k], sibling)(arriving, arriving).wait_recv()
        for cp in started:
            cp.wait_send()

    return pl.pallas_call(
        body, name=name, in_specs=[ANY] * T, out_specs=[ANY] * T,
        out_shape=[jax.ShapeDtypeStruct(s.shape, s.dtype) for s in slabs],
        input_output_aliases={t: t for t in range(T)},
        scratch_shapes=[pltpu.SemaphoreType.DMA((T, 3)), pltpu.SemaphoreType.DMA((T, 3))],
    )(*slabs)


def _swap_halves(name, grads):
    T = len(grads)

    def body(*refs):
        ins, got = refs[:T], refs[T:2 * T]
        send_sems, recv_sems = refs[2 * T:]
        x, y, c, _ = _position()
        remote = []
        for t in range(T):
            rows = ins[t].shape[1]
            remote.append(_start_in_chunks(
                _remote_copier(send_sems.at[t], recv_sems.at[t], (x, y, 1 - c)), ins[t].at[:, _half(rows, 1 - c)], got[t]))
        for cp in remote:
            cp.wait()

    return pl.pallas_call(
        body, name=name, in_specs=[ANY] * T, out_specs=[ANY] * T,
        out_shape=[jax.ShapeDtypeStruct((g.shape[0], g.shape[1] // 2, g.shape[2]), g.dtype) for g in grads],
        scratch_shapes=[pltpu.SemaphoreType.DMA((T,)), pltpu.SemaphoreType.DMA((T,))],
    )(*grads)


def _swap_start(name, grads):
    T = len(grads)
    lands = [lax.empty((g.shape[0], g.shape[1] // 2, g.shape[2]), g.dtype) for g in grads]

    def body(*refs):
        ins, got = refs[:T], refs[T:2 * T]
        send_sems, recv_sems = refs[2 * T:2 * T + 2]
        token = refs[-1]
        x, y, c, _ = _position()
        for t in range(T):
            _start_in_chunks(_remote_copier(send_sems.at[t], recv_sems.at[t], (x, y, 1 - c)),
                             ins[t].at[:, _half(ins[t].shape[1], 1 - c)], got[t], waited_here=False)
        token[...] = jnp.zeros_like(token)

    arrays = list(grads) + lands
    res = pl.pallas_call(
        body, name=name, in_specs=[HBM_SPEC] * (2 * T),
        out_specs=[SEM_SPEC, SEM_SPEC] + [HBM_SPEC] * (2 * T) + [VMEM_WHOLE],
        out_shape=[pltpu.SemaphoreType.DMA((T,)), pltpu.SemaphoreType.DMA((T,))]
        + [pltpu.HBM(a.shape, a.dtype) for a in arrays] + [jax.ShapeDtypeStruct((8, LANES), jnp.float32)],
        input_output_aliases={i: 2 + i for i in range(2 * T)},
        compiler_params=pltpu.CompilerParams(has_side_effects=SIDE_EFFECT),
    )(*[pltpu.with_memory_space_constraint(a, pltpu.HBM) for a in arrays])
    return res[0], res[1], res[2:2 + T], res[2 + T:2 + 2 * T], res[-1]


def _swap_wait(name, flight, after):
    send_sems, recv_sems, grads, lands, _ = flight
    T = len(grads)

    def body(*refs):
        ins, got = refs[:T], refs[T:2 * T]
        send, recv = refs[2 * T:2 * T + 2]
        x, y, c, _ = _position()
        for t in range(T):
            cp = _remote_copier(send.at[t], recv.at[t], (x, y, 1 - c))(ins[t].at[:, _half(ins[t].shape[1], 1 - c)], got[t])
            cp.wait_send()
            cp.wait_recv()

    arrays = list(grads) + list(lands)
    res = pl.pallas_call(
        body, name=name, in_specs=[HBM_SPEC] * (2 * T) + [SEM_SPEC, SEM_SPEC, ANY], out_specs=[HBM_SPEC] * (2 * T),
        out_shape=[pltpu.HBM(a.shape, a.dtype) for a in arrays], input_output_aliases={i: i for i in range(2 * T)},
        compiler_params=pltpu.CompilerParams(has_side_effects=SIDE_EFFECT),
    )(*arrays, send_sems, recv_sems, after)
    return res[:T], res[T:]


def _scatter_start(name, chip_sums):
    T = len(chip_sums)
    lands = [lax.empty((3,) + s.shape[1:], s.dtype) for s in chip_sums]

    def body(*refs):
        srcs, dsts = refs[:T], refs[T:2 * T]
        send_sems, recv_sems = refs[2 * T:2 * T + 2]
        token = refs[-1]
        x, y, c, other_chips = _position()
        for t in range(T):
            for k, (cx, cy) in enumerate(other_chips):
                _start_in_chunks(_remote_copier(send_sems.at[3 * t + k], recv_sems.at[3 * t + k], (cx, cy, c)),
                                 srcs[t].at[2 * cx + cy], dsts[t].at[k], waited_here=False)
        token[...] = jnp.zeros_like(token)

    hbm = [pltpu.HBM(a.shape, a.dtype) for a in list(chip_sums) + lands]
    res = pl.pallas_call(
        body, name=name, in_specs=[HBM_SPEC] * (2 * T),
        out_specs=[SEM_SPEC, SEM_SPEC] + [HBM_SPEC] * (2 * T) + [VMEM_WHOLE],
        out_shape=[pltpu.SemaphoreType.DMA((3 * T,)), pltpu.SemaphoreType.DMA((3 * T,))] + hbm
        + [jax.ShapeDtypeStruct((8, LANES), jnp.float32)],
        input_output_aliases={i: 2 + i for i in range(2 * T)},
        compiler_params=pltpu.CompilerParams(has_side_effects=SIDE_EFFECT),
    )(*[pltpu.with_memory_space_constraint(a, pltpu.HBM) for a in list(chip_sums) + lands])
    return res[0], res[1], res[2:2 + T], res[2 + T:2 + 2 * T], res[-1]


def _scatter_wait(name, send_sems, recv_sems, chip_sums, lands, after):
    T = len(chip_sums)

    def body(*refs):
        srcs, dsts = refs[:T], refs[T:2 * T]
        send, recv = refs[2 * T:2 * T + 2]
        x, y, c, other_chips = _position()
        for t in range(T):
            for k, (cx, cy) in enumerate(other_chips):
                cp = _remote_copier(send.at[3 * t + k], recv.at[3 * t + k], (cx, cy, c))(srcs[t].at[2 * cx + cy], dsts[t].at[k])
                cp.wait_send()
                cp.wait_recv()

    arrays = list(chip_sums) + list(lands)
    res = pl.pallas_call(
        body, name=name, in_specs=[HBM_SPEC] * (2 * T) + [SEM_SPEC, SEM_SPEC, ANY],
        out_specs=[HBM_SPEC] * (2 * T), out_shape=[pltpu.HBM(a.shape, a.dtype) for a in arrays],
        input_output_aliases={i: i for i in range(2 * T)},
        compiler_params=pltpu.CompilerParams(has_side_effects=SIDE_EFFECT),
    )(*arrays, send_sems, recv_sems, after)
    return res[:T], res[T:]


def _as_shards(partial):
    return [p.reshape((N_CHIPS, -1, p.shape[-1])) for p in partial]


def _reduce_begin(tag, names, partial, pos):
    parts = _as_shards(partial)
    received = _swap_halves("swap_grad_halves_" + tag, parts)
    return _reduce_scatter_start(tag, names, parts, received, pos)


def _reduce_scatter_start(tag, names, parts, received, pos):
    chip_sums = _per_run(lambda name, p, r: _chip_sum(name, p, r, WIRE_DTYPE, pos), names, list(parts), list(received))
    return _scatter_start("scatter_start_" + tag, chip_sums)


def _reduce_end(tag, names, flight, after, pos):
    send_sems, recv_sems, chip_sums, lands, _ = flight
    chip_sums, others = _scatter_wait("scatter_wait_" + tag, send_sems, recv_sems, chip_sums, lands, after)
    reduced = _per_run(lambda name, cs, ot: _owner_sum(name, cs, ot, pos), names, list(chip_sums), list(others))
    return dict(zip(names, reduced))


def _join_start(name, halves):
    T = len(halves)

    def body(*refs):
        ins = refs[:T]
        send_sems, recv_sems = refs[T:T + 2]
        token = refs[-1]
        x, y, c, _ = _position()
        for t in range(T):
            _start_in_chunks(_remote_copier(send_sems.at[t], recv_sems.at[t], (x, y, 1 - c)),
                             ins[t].at[c], ins[t].at[c], waited_here=False)
        token[...] = jnp.zeros_like(token)

    res = pl.pallas_call(
        body, name=name, in_specs=[HBM_SPEC] * T,
        out_specs=[SEM_SPEC, SEM_SPEC] + [HBM_SPEC] * T + [VMEM_WHOLE],
        out_shape=[pltpu.SemaphoreType.DMA((T,)), pltpu.SemaphoreType.DMA((T,))]
        + [pltpu.HBM(h.shape, h.dtype) for h in halves] + [jax.ShapeDtypeStruct((8, LANES), jnp.float32)],
        input_output_aliases={t: 2 + t for t in range(T)},
        compiler_params=pltpu.CompilerParams(has_side_effects=SIDE_EFFECT),
    )(*[pltpu.with_memory_space_constraint(h, pltpu.HBM) for h in halves])
    return res[0], res[1], res[2:2 + T], res[-1]


def _join_wait(name, flight, after):
    send_sems, recv_sems, halves, _ = flight
    T = len(halves)

    def body(*refs):
        ins = refs[:T]
        send, recv = refs[T:T + 2]
        x, y, c, _ = _position()
        for t in range(T):
            cp = _remote_copier(send.at[t], recv.at[t], (x, y, 1 - c))(ins[t].at[c], ins[t].at[1 - c])
            cp.wait_send()
            cp.wait_recv()

    return pl.pallas_call(
        body, name=name, in_specs=[HBM_SPEC] * T + [SEM_SPEC, SEM_SPEC, ANY], out_specs=[HBM_SPEC] * T,
        out_shape=[pltpu.HBM(h.shape, h.dtype) for h in halves], input_output_aliases={t: t for t in range(T)},
        compiler_params=pltpu.CompilerParams(has_side_effects=SIDE_EFFECT),
    )(*halves, send_sems, recv_sems, after)


def _peers():
    x, y, c, _ = _position()
    flips = [(fx, fy, fc) for fx in (0, 1) for fy in (0, 1) for fc in (0, 1)][1:]
    return [(jnp.bitwise_xor(x, fx), jnp.bitwise_xor(y, fy), jnp.bitwise_xor(c, fc)) for fx, fy, fc in flips]


def _device_index(px, py, pc):
    return 4 * px + 2 * py + pc


def _exchange_start(name, slots):
    def body(slots_ref, send_sems, recv_sems, thru, token):
        x, y, c, _ = _position()
        own = slots_ref.at[_device_index(x, y, c)]
        for k, peer in enumerate(_peers()):
            _remote_copier(send_sems.at[k], recv_sems.at[k], peer)(own, own).start()
        token[...] = jnp.zeros_like(token)

    res = pl.pallas_call(
        body, name=name, in_specs=[HBM_SPEC], out_specs=[SEM_SPEC, SEM_SPEC, HBM_SPEC, VMEM_WHOLE],
        out_shape=[pltpu.SemaphoreType.DMA((N_DEV - 1,)), pltpu.SemaphoreType.DMA((N_DEV - 1,)),
                   pltpu.HBM(slots.shape, slots.dtype), jax.ShapeDtypeStruct((8, LANES), jnp.float32)],
        input_output_aliases={0: 2}, compiler_params=pltpu.CompilerParams(has_side_effects=SIDE_EFFECT),
    )(pltpu.with_memory_space_constraint(slots, pltpu.HBM))
    return res[0], res[1], res[2], res[3]


def _exchange_wait(name, flight, after):
    send_sems, recv_sems, slots, _ = flight

    def body(slots_ref, send, recv, after_ref, out_ref):
        x, y, c, _ = _position()
        own = slots_ref.at[_device_index(x, y, c)]
        for k, peer in enumerate(_peers()):
            cp = _remote_copier(send.at[k], recv.at[k], peer)(own, slots_ref.at[_device_index(*peer)])
            cp.wait_send()
            cp.wait_recv()

    return pl.pallas_call(
        body, name=name, in_specs=[HBM_SPEC, SEM_SPEC, SEM_SPEC, ANY], out_specs=HBM_SPEC,
        out_shape=pltpu.HBM(slots.shape, slots.dtype), input_output_aliases={0: 0},
        compiler_params=pltpu.CompilerParams(has_side_effects=SIDE_EFFECT),
    )(slots, send_sems, recv_sems, after)


def _sum_slots(name, slots):
    n, rows, lanes = slots.shape
    tr = _row_tile(rows, lanes * n)

    def body(s_ref, o_ref):
        acc = s_ref[0]
        for d in range(1, n):
            acc = acc + s_ref[d]
        o_ref[...] = acc

    return pl.pallas_call(
        body, name=name, grid=(rows // tr,), in_specs=[pl.BlockSpec((n, tr, lanes), lambda i: (0, i, 0))],
        out_specs=pl.BlockSpec((tr, lanes), lambda i: (i, 0)), out_shape=jax.ShapeDtypeStruct((rows, lanes), jnp.float32),
        compiler_params=_params(1))(slots)


BIG = ("w_in", "pool_w", "w_out", "w_q", "w_k", "w_v", "w_o", "w_gate", "w_up", "w_down")
SMALL = ("norm_mix_g", "pool_scale", "sgu_norm_g", "w_spatial", "b_spatial", "norm_xattn_g", "norm_mem_g",
         "norm_ffn_g", "final_norm_g")
GATHER_ORDER = (("w_in", "pool_w"), ("w_out",), ("w_q", "w_k", "w_v", "w_o"), ("w_gate", "w_up"), ("w_down",))
WEIGHTS = ("norm_mix_g", "w_in", "pool_w", "pool_scale", "sgu_norm_g", "w_spatial", "b_spatial", "w_out",
           "norm_xattn_g", "norm_mem_g", "w_q", "w_k", "w_v", "w_o", "norm_ffn_g", "w_gate", "w_up", "w_down",
           "final_norm_g")


def _as2d(a):
    return a.reshape(-1, a.shape[-1])


def _as_lanes(a):
    return a.reshape(-1, LANES)


def kernel(x, mem, norm_mix_g, w_in, pool_w, pool_scale, sgu_norm_g, w_spatial, b_spatial, w_out, norm_xattn_g, norm_mem_g, w_q, w_k, w_v, w_o, norm_ffn_g, w_gate, w_up, w_down, final_norm_g, loss_target, m_norm_mix_g, m_w_in, m_pool_w, m_pool_scale, m_sgu_norm_g, m_w_spatial, m_b_spatial, m_w_out, m_norm_xattn_g, m_norm_mem_g, m_w_q, m_w_k, m_w_v, m_w_o, m_norm_ffn_g, m_w_gate, m_w_up, m_w_down, m_final_norm_g, v_norm_mix_g, v_w_in, v_pool_w, v_pool_scale, v_sgu_norm_g, v_w_spatial, v_b_spatial, v_w_out, v_norm_xattn_g, v_norm_mem_g, v_w_q, v_w_k, v_w_v, v_w_o, v_norm_ffn_g, v_w_gate, v_w_up, v_w_down, v_final_norm_g):
    args = dict(locals())
    weights = {n: args[n] for n in WEIGHTS}
    moments_m = {n: args["m_" + n] for n in WEIGHTS}
    moments_v = {n: args["v_" + n] for n in WEIGHTS}
    low = MXU_DTYPE

    xs = x[0]
    mems = mem[0]
    target = loss_target[0]
    D = xs.shape[1]
    G, PG = pool_w.shape[1], pool_w.shape[3]

    pos = jnp.stack([2 * lax.axis_index("x") + lax.axis_index("y"), lax.axis_index("c")]).astype(jnp.int32)
    def cast(names, after=None):
        slabs = _per_run(lambda name, xs: _cast_into_slabs(name, xs, WIRE_DTYPE, pos, after), names,
                         [_as2d(weights[n]) for n in names])
        return dict(zip(names, slabs))

    def land(gi, flight, after, relayed=True):
        wait = _relayed_wait if relayed else _gather_wait
        landed = wait("gather_wait_%d" % gi, flight, after)
        return dict(zip(GATHER_ORDER[gi], _share_with_sibling("gather_share_%d" % gi, landed)))

    def rows(a):
        return a.reshape(-1, a.shape[-1])

    g_mix, g_xattn, g_mem, g_ffn = norm_mix_g, norm_xattn_g, norm_mem_g, norm_ffn_g
    g_final = final_norm_g.reshape(1, D)
    ws = w_spatial[0]
    b_full = jnp.broadcast_to(b_spatial[0][:, :, None], ws.shape[:2] + (sgu_norm_g.shape[1] // ws.shape[0],))

    first = cast(GATHER_ORDER[0])
    _, (hop_0,), _, started = _gather_step("gather_step_0", ring=[[first[n] for n in GATHER_ORDER[0]]])
    rest = cast([n for group in GATHER_ORDER[1:] for n in group], after=started)
    slabs = [[rest[n] for n in group] for group in GATHER_ORDER[1:]]
    (relay_0,), (hop_1, hop_2), _, started = _gather_step("gather_step_1", arrived=[hop_0], ring=slabs[:2])
    h1 = _rmsnorm("norm_mix", xs, g_mix, low, after=started)
    mn = _rmsnorm("norm_mem", mems, g_mem, low)
    got = land(0, relay_0, mn)
    wg_in = got["w_in"]
    pw = got["pool_w"].reshape(N_CHIPS, G, PG // N_CHIPS, PG).transpose(1, 0, 2, 3).reshape(G, PG, PG)
    proj = _matmul_nn_cols("proj_in", h1, wg_in, jnp.float32)
    z = _mixer_fwd(proj, pw, pool_scale, sgu_norm_g, ws, b_full)
    (relay_1, relay_2), (hop_3,), _, started = _gather_step("gather_step_2", arrived=[hop_1, hop_2], ring=slabs[2:3], after=z)
    full = {"w_out": rows(land(1, relay_1, started)["w_out"])}
    x1, h2 = _proj_residual_norm("mix_out", z, full["w_out"], xs, g_xattn)
    full.update({n: rows(a) for n, a in land(2, relay_2, h2).items()})
    q = _matmul_nn("xattn_q", h2, full["w_q"], low)
    k = _matmul_nn("xattn_k", mn, full["w_k"], low)
    vv = _matmul_nn("xattn_v", mn, full["w_v"], low)
    o = _attn_fwd(q, k, vv)
    (relay_3,), _, (sent_4,), started = _gather_step("gather_step_3", arrived=[hop_3], direct=slabs[3:4], after=o)
    x2, h3 = _proj_residual_norm("xattn_out", o, full["w_o"], x1, g_ffn, after=started)
    got = land(3, relay_3, h3)
    wg_gate, wg_up = got["w_gate"], got["w_up"]
    act, act_by_gate, act_by_up = _ffn_gate_up(h3, wg_gate, wg_up)
    full["w_down"] = rows(land(4, sent_4, act, relayed=False)["w_down"])
    x3 = _ffn_down(act, full["w_down"], x2)
    dx3, dx3_low, loss_part, d_final_g = _final_norm_loss(x3, target, g_final)

    dgate, dup = _ffn_down_bwd(dx3_low, full["w_down"], act_by_gate, act_by_up, N_CHIPS)
    (d_w_down,) = _matmul_tn("grad_w_down", act, [dx3_low], WIRE_DTYPE)
    (d_w_gate,) = _matmul_tn_cols("grad_w_gate", h3, [dgate], N_CHIPS, WIRE_DTYPE)
    (d_w_up,) = _matmul_tn_cols("grad_w_up", h3, [dup], N_CHIPS, WIRE_DTYPE)
    ffn_names = ("w_gate", "w_up", "w_down")
    ffn_swap = _swap_start("swap_start_ffn", _as_shards([d_w_gate, d_w_up, d_w_down]))
    dh3 = _matmul_nt_cols("ffn_dh", [dgate, dup], [wg_gate, wg_up], jnp.float32, after=ffn_swap[-1])
    ffn_flight = _reduce_scatter_start("ffn", ffn_names, *_swap_wait("swap_wait_ffn", ffn_swap, dh3), pos)
    dx2, dx2_low, d_ffn_g = _rmsnorm_bwd("norm_ffn_bwd", dh3, x2, g_ffn, dx3, low)

    do = _matmul_nt("xattn_do", [dx2_low], [full["w_o"]], low, after=ffn_flight[-1])
    (d_w_o,) = _matmul_tn("grad_w_o", o, [dx2_low], WIRE_DTYPE)
    dq, dk, dvv = _attn_bwd(q, k, vv, do)
    (d_w_q,) = _matmul_tn("grad_w_q", h2, [dq], WIRE_DTYPE)
    dx1, dx1_low, d_xattn_g = _matmul_nt_norm_bwd("xattn_dh_norm_bwd", dq, full["w_q"], x1, g_xattn, dx2, low)
    d_w_k, d_w_v = _matmul_tn("grad_w_kv", mn, [dk, dvv], WIRE_DTYPE)
    dmn = _matmul_nt("xattn_dmem", [dk, dvv], [full["w_k"], full["w_v"]], jnp.float32)
    (d_mem_g,) = _rmsnorm_bwd("norm_mem_bwd", dmn, mems, g_mem, None, None)

    (d_w_out,) = _matmul_tn("grad_w_out", z, [dx1_low], WIRE_DTYPE)
    attn_names = ("w_q", "w_k", "w_v", "w_o", "w_out")
    attn_swap = _swap_start("swap_start_attn", _as_shards([d_w_q, d_w_k, d_w_v, d_w_o, d_w_out]))
    dz = _matmul_nt("mix_dz", [dx1_low], [full["w_out"]], jnp.float32, after=attn_swap[-1])
    attn_flight = _reduce_scatter_start("attn", attn_names, *_swap_wait("swap_wait_attn", attn_swap, dz), pos)
    dproj, d_pw, d_ps, d_gs, d_ws, d_b = _mixer_bwd(proj, dz, pw, pool_scale, sgu_norm_g, ws, b_full)
    (d_w_in,) = _matmul_tn_cols("grad_w_in", h1, [dproj], N_CHIPS, WIRE_DTYPE, tka=1024, after=attn_flight[-1])
    d_pool_w = d_pw.reshape(G, N_CHIPS, PG // N_CHIPS, PG).transpose(1, 0, 2, 3).reshape(N_CHIPS, PG, PG).astype(WIRE_DTYPE)
    mix_names = ("w_in", "pool_w")
    mix_flight = _reduce_begin("mix", mix_names, [d_w_in, d_pool_w], pos)
    dh1 = _matmul_nt_cols("mix_dh", [dproj], [wg_in], jnp.float32, tn=1024, after=mix_flight[-1])
    grad_x, d_mix_g = _rmsnorm_bwd("norm_mix_bwd", dh1, xs, g_mix, dx1, None)

    small_parts = {
        "norm_mix_g": d_mix_g, "pool_scale": d_ps, "sgu_norm_g": d_gs, "w_spatial": d_ws, "b_spatial": d_b,
        "norm_xattn_g": d_xattn_g, "norm_mem_g": d_mem_g, "norm_ffn_g": d_ffn_g, "final_norm_g": d_final_g,
    }
    row_counts = [small_parts[n].size // LANES for n in SMALL]
    packed = jnp.concatenate([_as_lanes(small_parts[n]) for n in SMALL], axis=0)
    device = _device_index(lax.axis_index("x"), lax.axis_index("y"), lax.axis_index("c"))
    slots = lax.dynamic_update_slice(jnp.zeros((N_DEV,) + packed.shape, jnp.float32), packed[None], (device, 0, 0))
    small_flight = _exchange_start("small_grads_start", slots)

    groups = (("ffn", ffn_names, ffn_flight), ("attn", attn_names, attn_flight), ("mix", mix_names, mix_flight))
    joins, after = {}, small_flight[-1]
    for tag, names, flight in groups:
        reduced = _reduce_end(tag, names, flight, after, pos)
        joins[tag] = _join_start("join_start_" + tag, [reduced[n] for n in names])
        after = joins[tag][-1]
    small_sum = _sum_slots("small_grads_sum", _exchange_wait("small_grads_wait", small_flight, after))

    out_grad, out_delta, out_m, out_v = {}, {}, {}, {}
    after = small_sum
    for tag, names, _ in groups:
        grads = [g.reshape(-1, g.shape[-1]) for g in _join_wait("join_wait_" + tag, joins[tag], after)]
        updates = _per_run(_adamw, names, [_as2d(weights[n]) for n in names], grads,
                           [_as2d(moments_m[n]) for n in names], [_as2d(moments_v[n]) for n in names])
        for n, update in zip(names, updates):
            shape = weights[n].shape
            out_grad[n], out_delta[n], out_m[n], out_v[n] = (a.reshape(shape) for a in update)
        after = updates[-1][1]
    small_out = _adamw_small(
        small_sum, [_as_lanes(weights[n]) for n in SMALL], [_as_lanes(moments_m[n]) for n in SMALL],
        [_as_lanes(moments_v[n]) for n in SMALL], row_counts)
    for t, n in enumerate(SMALL):
        shape = weights[n].shape
        out_grad[n], out_delta[n], out_m[n], out_v[n] = (a.reshape(shape) for a in small_out[4 * t:4 * t + 4])

    loss = lax.psum(loss_part[0, 0], ("x", "y", "c"))
    return (loss, grad_x[None], *[out_grad[n] for n in WEIGHTS], *[out_delta[n] for n in WEIGHTS],
            *[out_m[n] for n in WEIGHTS], *[out_v[n] for n in WEIGHTS])
```

```python
import math

import jax
import jax.numpy as jnp
from jax import lax
from jax.experimental import pallas as pl
from jax.experimental.pallas import tpu as pltpu

MXU_DTYPE = jnp.bfloat16
WIRE_DTYPE = jnp.bfloat16

EPS = 1e-6
CHUNK = 64
POOL_WINDOWS = (2, 4, 8, 16)
POOL_HALO = 16
N_XATTN_HEADS = 4
ADAM_LR = 0.001
ADAM_B1 = 0.9
ADAM_B2 = 0.999
ADAM_EPS = 1e-08
ADAM_WD = 0.01
ADAM_STEP = 10

N_CHIPS = 4
N_DEV = 8
LANES = 128
VMEM_LIMIT_BYTES = 56 * 2 ** 20
COPY_CHUNK_BYTES = 512 * 2 ** 10
COPY_ROW_ALIGN = 16

MESH = pl.DeviceIdType.MESH
ANY = pl.BlockSpec(memory_space=pl.ANY)
VMEM_WHOLE = pl.BlockSpec(memory_space=pltpu.VMEM)
HBM_SPEC = pl.BlockSpec(memory_space=pltpu.HBM)
SEM_SPEC = pl.BlockSpec(memory_space=pltpu.SEMAPHORE)
SIDE_EFFECT = pltpu.SideEffectType.DATAFLOW_SIDE_EFFECTING

NN = (((1,), (0,)), ((), ()))
NT = (((1,), (1,)), ((), ()))
TN = (((0,), (0,)), ((), ()))


def _tile(n, target, mult):
    best = None
    for t in range(mult, min(n, target) + 1, mult):
        if n % t == 0:
            best = t
    return n if best is None else best


def _params(n_grid_axes):
    return pltpu.CompilerParams(dimension_semantics=("arbitrary",) * n_grid_axes, vmem_limit_bytes=VMEM_LIMIT_BYTES)


def _rsqrt_mean_sq(x):
    return lax.rsqrt(jnp.mean(x * x, axis=-1, keepdims=True) + EPS)


def _sum_all(x):
    return jnp.sum(jnp.sum(x, axis=1, keepdims=True), axis=0, keepdims=True)


def _sigmoid(x):
    return 0.5 * jnp.tanh(0.5 * x) + 0.5


def _store_epilogue(accs, ex_refs, out_refs, i, j):
    for a, o in zip(accs, out_refs):
        o[...] = a.astype(o.dtype)


def _fused_matmul(name, grid, operands, pairs, acc_shapes, extras, outs, epilogue):
    nk = grid[2]
    n_op, n_ex, n_out, n_acc = len(operands), len(extras), len(outs), len(acc_shapes)
    in_place = nk > 1 and epilogue is _store_epilogue and all(o[1] == jnp.float32 for o in outs)

    def body(*refs):
        op_refs = refs[:n_op]
        ex_refs = refs[n_op:n_op + n_ex]
        out_refs = refs[n_op + n_ex:n_op + n_ex + n_out]
        acc_refs = out_refs if in_place else refs[n_op + n_ex + n_out:]
        i, j, k = pl.program_id(0), pl.program_id(1), pl.program_id(2)
        loaded = {}

        def operand(n):
            if n not in loaded:
                loaded[n] = op_refs[n][...].astype(MXU_DTYPE)
            return loaded[n]

        def product(ia, ib, dims):
            return lax.dot_general(operand(ia), operand(ib), dims, preferred_element_type=jnp.float32)

        def products():
            parts = [None] * n_acc
            for ia, ib, iacc, dims in pairs:
                d = product(ia, ib, dims)
                parts[iacc] = d if parts[iacc] is None else parts[iacc] + d
            return parts

        if nk == 1:
            epilogue(products(), ex_refs, out_refs, i, j)
            return

        @pl.when(k == 0)
        def _():
            for a in range(n_acc):
                acc_refs[a][...] = jnp.zeros_like(acc_refs[a])

        for a, part in enumerate(products()):
            acc_refs[a][...] += part

        if not in_place:
            @pl.when(k == nk - 1)
            def _():
                epilogue([acc_refs[a][...] for a in range(n_acc)], ex_refs, out_refs, i, j)

    arrays = [o[0] for o in operands] + [e[0] for e in extras]
    in_specs = [pl.BlockSpec(o[1], o[2]) for o in operands]
    in_specs += [ANY if e[1] is None else pl.BlockSpec(e[1], e[2]) for e in extras]
    return pl.pallas_call(
        body,
        name=name,
        grid=grid,
        in_specs=in_specs,
        out_specs=[pl.BlockSpec(o[2], o[3]) for o in outs],
        out_shape=[jax.ShapeDtypeStruct(o[0], o[1]) for o in outs],
        scratch_shapes=[pltpu.VMEM(s, jnp.float32) for s in acc_shapes] if nk > 1 and not in_place else [],
        compiler_params=_params(3),
    )(*arrays)


def _matmul_nn(name, a, b, out_dtype, tm=1024, tn=512):
    M, K = a.shape
    N = b.shape[1]
    tm, tn = _tile(M, tm, 16), _tile(N, tn, LANES)
    return _fused_matmul(
        name, (M // tm, N // tn, 1),
        [(a, (tm, K), lambda i, j, k: (i, 0)), (b, (K, tn), lambda i, j, k: (0, j))],
        [(0, 1, 0, NN)], [(tm, tn)], [],
        [((M, N), out_dtype, (tm, tn), lambda i, j, k: (i, j))], _store_epilogue)[0]


def _matmul_nn_cols(name, a, b3, out_dtype, tm=1024):
    M, K = a.shape
    J, _, Nc = b3.shape
    tm = _tile(M, tm, 16)
    return _fused_matmul(
        name, (M // tm, J, 1),
        [(a, (tm, K), lambda i, j, k: (i, 0)), (b3, (None, K, Nc), lambda i, j, k: (j, 0, 0))],
        [(0, 1, 0, NN)], [(tm, Nc)], [],
        [((M, J * Nc), out_dtype, (tm, Nc), lambda i, j, k: (i, j))], _store_epilogue)[0]


def _order_after(after):
    return [] if after is None else [(after, None, None)]


def _matmul_nt(name, a_list, b_list, out_dtype, tm=1024, tn=512, after=None):
    M, K = a_list[0].shape
    N = b_list[0].shape[0]
    tm, tn = _tile(M, tm, 16), _tile(N, tn, LANES)
    n = len(a_list)
    operands = [(a, (tm, K), lambda i, j, k: (i, 0)) for a in a_list]
    operands += [(b, (tn, K), lambda i, j, k: (j, 0)) for b in b_list]
    return _fused_matmul(
        name, (M // tm, N // tn, 1), operands,
        [(p, n + p, 0, NT) for p in range(n)], [(tm, tn)], _order_after(after),
        [((M, N), out_dtype, (tm, tn), lambda i, j, k: (i, j))], _store_epilogue)[0]


def _matmul_nt_cols(name, a_list, b3_list, out_dtype, tm=1024, tn=512, after=None):
    M = a_list[0].shape[0]
    J, N, Kc = b3_list[0].shape
    tm, tn = _tile(M, tm, 16), _tile(N, tn, LANES)
    n = len(a_list)
    operands = [(a, (tm, Kc), lambda i, j, k: (i, k)) for a in a_list]
    operands += [(b, (None, tn, Kc), lambda i, j, k: (k, j, 0)) for b in b3_list]
    return _fused_matmul(
        name, (M // tm, N // tn, J), operands,
        [(p, n + p, 0, NT) for p in range(n)], [(tm, tn)], _order_after(after),
        [((M, N), out_dtype, (tm, tn), lambda i, j, k: (i, j))], _store_epilogue)[0]


def _matmul_tn(name, a, b_list, out_dtype, tka=512, tn=1024, tkm=4096):
    M, Ka = a.shape
    N = b_list[0].shape[1]
    tka, tn, tkm = _tile(Ka, tka, LANES), _tile(N, tn, LANES), _tile(M, tkm, 16)
    n = len(b_list)
    operands = [(a, (tkm, tka), lambda i, j, k: (k, i))]
    operands += [(b, (tkm, tn), lambda i, j, k: (k, j)) for b in b_list]
    return _fused_matmul(
        name, (Ka // tka, N // tn, M // tkm), operands,
        [(0, 1 + p, p, TN) for p in range(n)], [(tka, tn)] * n, [],
        [((Ka, N), out_dtype, (tka, tn), lambda i, j, k: (i, j))] * n, _store_epilogue)


def _matmul_tn_cols(name, a, b_list, n_shards, out_dtype, tka=512, tkm=4096, after=None):
    M, Ka = a.shape
    Nc = b_list[0].shape[1] // n_shards
    tka, tkm = _tile(Ka, tka, LANES), _tile(M, tkm, 16)
    n = len(b_list)
    operands = [(a, (tkm, tka), lambda i, j, k: (k, i))]
    operands += [(b, (tkm, Nc), lambda i, j, k: (k, j)) for b in b_list]
    return _fused_matmul(
        name, (Ka // tka, n_shards, M // tkm), operands,
        [(0, 1 + p, p, TN) for p in range(n)], [(tka, Nc)] * n, _order_after(after),
        [((n_shards, Ka, Nc), out_dtype, (None, tka, Nc), lambda i, j, k: (j, i, 0))] * n, _store_epilogue)


def _rmsnorm(name, x, g_row, out_dtype, tm=512, after=None):
    S, D = x.shape
    tm = _tile(S, tm, 16)

    def body(x_ref, g_ref, *rest):
        o_ref = rest[-1]
        xv = x_ref[...]
        o_ref[...] = (xv * _rsqrt_mean_sq(xv) * g_ref[...]).astype(o_ref.dtype)

    ordered = [] if after is None else [after]
    return pl.pallas_call(
        body, name=name, grid=(S // tm,),
        in_specs=[pl.BlockSpec((tm, D), lambda i: (i, 0)), pl.BlockSpec((1, D), lambda i: (0, 0))] + [ANY] * len(ordered),
        out_specs=pl.BlockSpec((tm, D), lambda i: (i, 0)),
        out_shape=jax.ShapeDtypeStruct((S, D), out_dtype),
        compiler_params=_params(1),
    )(x, g_row, *ordered)


def _rmsnorm_bwd_values(dh, xin, g):
    r = _rsqrt_mean_sq(xin)
    xhat = xin * r
    dhg = dh * g
    dx = r * (dhg - xhat * jnp.mean(dhg * xhat, axis=-1, keepdims=True))
    return dx, jnp.sum(dh * xhat, axis=0, keepdims=True)


def _rmsnorm_bwd(name, dh, xin, g_row, dres, low_dtype, tm=256):
    S, D = xin.shape
    tm = _tile(S, tm, 16)
    has_dx = dres is not None
    has_low = has_dx and low_dtype is not None

    def body(*refs):
        dh_ref, x_ref, g_ref = refs[:3]
        rest = refs[3:]
        dx, dg = _rmsnorm_bwd_values(dh_ref[...], x_ref[...], g_ref[...])
        if has_dx:
            dres_ref, rest = rest[0], rest[1:]
            dx = dx + dres_ref[...]
            rest[0][...] = dx
            if has_low:
                rest[1][...] = dx.astype(low_dtype)
        dg_ref = rest[-1]

        @pl.when(pl.program_id(0) == 0)
        def _():
            dg_ref[...] = jnp.zeros_like(dg_ref)

        dg_ref[...] += dg

    row = pl.BlockSpec((tm, D), lambda i: (i, 0))
    vec = pl.BlockSpec((1, D), lambda i: (0, 0))
    ins, in_specs = [dh, xin, g_row], [row, row, vec]
    out_shape, out_specs = [], []
    if has_dx:
        ins.append(dres)
        in_specs.append(row)
        out_shape.append(jax.ShapeDtypeStruct((S, D), jnp.float32))
        out_specs.append(row)
        if has_low:
            out_shape.append(jax.ShapeDtypeStruct((S, D), low_dtype))
            out_specs.append(row)
    out_shape.append(jax.ShapeDtypeStruct((1, D), jnp.float32))
    out_specs.append(vec)
    return pl.pallas_call(
        body, name=name, grid=(S // tm,), in_specs=in_specs, out_specs=out_specs, out_shape=out_shape,
        compiler_params=_params(1),
    )(*ins)


def _chunk_mask(blk):
    t = lax.broadcasted_iota(jnp.int32, (blk, blk), 0)
    s = lax.broadcasted_iota(jnp.int32, (blk, blk), 1)
    return (s // CHUNK) <= (t // CHUNK)


def _pool_inputs(a, halo, row0, tm, g, pg):
    cols = slice(g * pg, (g + 1) * pg)
    w = POOL_WINDOWS[g]
    s = jnp.concatenate([halo[:, cols], a[:, cols]], axis=0)
    span = 1
    while span < w:
        s = s + pltpu.roll(s, span, axis=0)
        span *= 2
    t = lax.broadcasted_iota(jnp.int32, (tm, 1), 0) + row0
    cnt = jnp.minimum(t + 1, w).astype(jnp.float32)
    return s[POOL_HALO:] / cnt - a[:, cols]


def _mixer_specs(S, tm, DP, DS):
    per_halo = tm // POOL_HALO
    n_halo = S // POOL_HALO
    a_spec = pl.BlockSpec((tm, DP), lambda i: (i, 0))
    prev_spec = pl.BlockSpec((POOL_HALO, DP), lambda i: (jnp.maximum(i * per_halo - 1, 0), 0))
    next_spec = pl.BlockSpec((POOL_HALO, DP), lambda i: (jnp.minimum((i + 1) * per_halo, n_halo - 1), 0))
    u_spec = pl.BlockSpec((tm, DS), lambda i: (i, 1))
    v_spec = pl.BlockSpec((tm, DS), lambda i: (i, 2))
    return a_spec, prev_spec, next_spec, u_spec, v_spec


def _whole(arr):
    nd = arr.ndim
    return pl.BlockSpec(arr.shape, lambda i: (0,) * nd)


def _mixer_fwd(proj, pw, ps_row, gs_row, ws, b_full, tm=512):
    S = proj.shape[0]
    G, PG, _ = pw.shape
    DP = G * PG
    H, BLK, _ = ws.shape
    DS = gs_row.shape[1]
    HD = DS // H
    assert DP == DS and proj.shape[1] == DP + 2 * DS
    tm = _tile(S, tm, BLK)

    def body(a_ref, prev_ref, u_ref, v_ref, pw_ref, ps_ref, gs_ref, ws_ref, b_ref, z_ref):
        i = pl.program_id(0)
        a = a_ref[...]
        halo = jnp.where(i > 0, prev_ref[...], 0.0)
        for g in range(G):
            cols = slice(g * PG, (g + 1) * PG)
            p = _pool_inputs(a, halo, i * tm, tm, g, PG)
            y = jnp.dot(p.astype(MXU_DTYPE), pw_ref[g], preferred_element_type=jnp.float32) * ps_ref[:, cols]
            z_ref[:, cols] = y.astype(z_ref.dtype)
        v = v_ref[...]
        vn = (v * _rsqrt_mean_sq(v) * gs_ref[...]).astype(MXU_DTYPE)
        u = u_ref[...]
        mask = _chunk_mask(BLK)
        for h in range(H):
            wm = jnp.where(mask, ws_ref[h], 0.0).astype(MXU_DTYPE)
            bias = b_ref[h]
            cols = slice(h * HD, (h + 1) * HD)
            for n in range(tm // BLK):
                rows = slice(n * BLK, (n + 1) * BLK)
                mixed = jnp.dot(wm, vn[rows, cols], preferred_element_type=jnp.float32) + bias
                z_ref[rows, DP + h * HD:DP + (h + 1) * HD] = (u[rows, cols] * mixed).astype(z_ref.dtype)

    a_spec, prev_spec, _, u_spec, v_spec = _mixer_specs(S, tm, DP, DS)
    return pl.pallas_call(
        body, name="mixer_fwd", grid=(S // tm,),
        in_specs=[a_spec, prev_spec, u_spec, v_spec, _whole(pw), _whole(ps_row), _whole(gs_row), _whole(ws), _whole(b_full)],
        out_specs=pl.BlockSpec((tm, DP + DS), lambda i: (i, 0)),
        out_shape=jax.ShapeDtypeStruct((S, DP + DS), MXU_DTYPE),
        compiler_params=_params(1),
    )(proj, proj, proj, proj, pw, ps_row, gs_row, ws, b_full)


def _mixer_bwd(proj, dz, pw, ps_row, gs_row, ws, b_full, tm=256):
    S = proj.shape[0]
    G, PG, _ = pw.shape
    DP = G * PG
    H, BLK, _ = ws.shape
    DS = gs_row.shape[1]
    HD = DS // H
    tm = _tile(S, tm, BLK)
    nb = S // tm
    ext = tm + POOL_HALO

    def body(a_ref, prev_ref, u_ref, v_ref, dzp_ref, dzp_next_ref, dzs_ref, pw_ref, ps_ref, gs_ref, ws_ref, b_ref,
             dproj_ref, dpw_ref, dps_ref, dgs_ref, dws_ref, db_ref, dvn_ref):
        i = pl.program_id(0)

        @pl.when(i == 0)
        def _():
            for r in (dpw_ref, dps_ref, dgs_ref, dws_ref, db_ref):
                r[...] = jnp.zeros_like(r)

        a = a_ref[...]
        halo = jnp.where(i > 0, prev_ref[...], 0.0)
        dyp = dzp_ref[...]
        dyp_next = jnp.where(i < nb - 1, dzp_next_ref[...], 0.0)
        dq_ext = jnp.concatenate([dyp, dyp_next], axis=0) * ps_ref[...]
        t_ext = lax.broadcasted_iota(jnp.int32, (ext, 1), 0) + i * tm
        for g in range(G):
            cols = slice(g * PG, (g + 1) * PG)
            w = POOL_WINDOWS[g]
            pb = _pool_inputs(a, halo, i * tm, tm, g, PG).astype(MXU_DTYPE)
            y_pre = jnp.dot(pb, pw_ref[g], preferred_element_type=jnp.float32)
            dps_ref[:, cols] += jnp.sum(dyp[:, cols] * y_pre, axis=0, keepdims=True)
            dqb = dq_ext[:, cols].astype(MXU_DTYPE)
            dpw_ref[g] += lax.dot_general(pb, dqb[:tm], TN, preferred_element_type=jnp.float32)
            dp = lax.dot_general(dqb, pw_ref[g], NT, preferred_element_type=jnp.float32)
            f = dp / jnp.minimum(t_ext + 1, w).astype(jnp.float32)
            span = 1
            while span < w:
                f = f + pltpu.roll(f, ext - span, axis=0)
                span *= 2
            dproj_ref[:, cols] = (f[:tm] - dp[:tm]).astype(dproj_ref.dtype)

        v = v_ref[...]
        rv = _rsqrt_mean_sq(v)
        vhat = v * rv
        gs = gs_ref[...]
        vnb = (vhat * gs).astype(MXU_DTYPE)
        u = u_ref[...]
        dys = dzs_ref[...]
        mask = _chunk_mask(BLK)
        for h in range(H):
            wm = jnp.where(mask, ws_ref[h], 0.0).astype(MXU_DTYPE)
            bias = b_ref[h]
            cols = slice(h * HD, (h + 1) * HD)
            dm_sum = jnp.zeros((BLK, HD), jnp.float32)
            dws_sum = jnp.zeros((BLK, BLK), jnp.float32)
            for n in range(tm // BLK):
                rows = slice(n * BLK, (n + 1) * BLK)
                vblk = vnb[rows, cols]
                mixed = jnp.dot(wm, vblk, preferred_element_type=jnp.float32) + bias
                dy = dys[rows, cols]
                dproj_ref[rows, DP + h * HD:DP + (h + 1) * HD] = (dy * mixed).astype(dproj_ref.dtype)
                dm = dy * u[rows, cols]
                dm_sum = dm_sum + dm
                dmb = dm.astype(MXU_DTYPE)
                dws_sum = dws_sum + lax.dot_general(dmb, vblk, NT, preferred_element_type=jnp.float32)
                dvn_ref[rows, cols] = lax.dot_general(wm, dmb, TN, preferred_element_type=jnp.float32)
            dws_ref[h] += jnp.where(mask, dws_sum, 0.0)
            db_ref[h:h + 1, :] += jnp.sum(dm_sum.T, axis=0, keepdims=True)
        dvn = dvn_ref[...]
        dvg = dvn * gs
        dv = rv * (dvg - vhat * jnp.mean(dvg * vhat, axis=-1, keepdims=True))
        dgs_ref[...] += jnp.sum(dvn * vhat, axis=0, keepdims=True)
        dproj_ref[:, DP + DS:] = dv.astype(dproj_ref.dtype)

    a_spec, prev_spec, next_spec, u_spec, v_spec = _mixer_specs(S, tm, DP, DS)
    dzp_spec = pl.BlockSpec((tm, DP), lambda i: (i, 0))
    dzs_spec = pl.BlockSpec((tm, DS), lambda i: (i, 1))
    small = [pw, ps_row, gs_row, ws, b_full]
    out_shape = [
        jax.ShapeDtypeStruct((S, DP + 2 * DS), MXU_DTYPE),
        jax.ShapeDtypeStruct(pw.shape, jnp.float32),
        jax.ShapeDtypeStruct(ps_row.shape, jnp.float32),
        jax.ShapeDtypeStruct(gs_row.shape, jnp.float32),
        jax.ShapeDtypeStruct(ws.shape, jnp.float32),
        jax.ShapeDtypeStruct((H, BLK), jnp.float32),
    ]
    out_specs = [pl.BlockSpec((tm, DP + 2 * DS), lambda i: (i, 0))] + [_whole(s) for s in out_shape[1:]]
    return pl.pallas_call(
        body, name="mixer_bwd", grid=(nb,),
        in_specs=[a_spec, prev_spec, u_spec, v_spec, dzp_spec, next_spec, dzs_spec] + [_whole(s) for s in small],
        out_specs=out_specs, out_shape=out_shape,
        scratch_shapes=[pltpu.VMEM((tm, DS), jnp.float32)],
        compiler_params=_params(1),
    )(proj, proj, proj, proj, dz, dz, dz, *small)


def _softmax_rows(q, k, scale):
    s = lax.dot_general(q, k, NT, preferred_element_type=jnp.float32) * scale
    e = jnp.exp(s - jnp.max(s, axis=-1, keepdims=True))
    return e / jnp.sum(e, axis=-1, keepdims=True)


def _attn_fwd(q, k, v, tm=512):
    S, D = q.shape
    M = k.shape[0]
    HD = D // N_XATTN_HEADS
    scale = HD ** -0.5
    tm = _tile(S, tm, 16)

    def body(q_ref, k_ref, v_ref, o_ref):
        for h in range(N_XATTN_HEADS):
            cols = slice(h * HD, (h + 1) * HD)
            p = _softmax_rows(q_ref[:, cols], k_ref[:, cols], scale)
            o = jnp.dot(p.astype(MXU_DTYPE), v_ref[:, cols], preferred_element_type=jnp.float32)
            o_ref[:, cols] = o.astype(o_ref.dtype)

    row = pl.BlockSpec((tm, D), lambda i: (i, 0))
    mem = pl.BlockSpec((M, D), lambda i: (0, 0))
    return pl.pallas_call(
        body, name="attn_fwd", grid=(S // tm,), in_specs=[row, mem, mem], out_specs=row,
        out_shape=jax.ShapeDtypeStruct((S, D), MXU_DTYPE), compiler_params=_params(1),
    )(q, k, v)


def _attn_bwd(q, k, v, do, tm=512):
    S, D = q.shape
    M = k.shape[0]
    HD = D // N_XATTN_HEADS
    scale = HD ** -0.5
    tm = _tile(S, tm, 16)

    def body(q_ref, k_ref, v_ref, do_ref, dq_ref, dk_ref, dv_ref):
        @pl.when(pl.program_id(0) == 0)
        def _():
            dk_ref[...] = jnp.zeros_like(dk_ref)
            dv_ref[...] = jnp.zeros_like(dv_ref)

        for h in range(N_XATTN_HEADS):
            cols = slice(h * HD, (h + 1) * HD)
            qh, kh, vh, doh = q_ref[:, cols], k_ref[:, cols], v_ref[:, cols], do_ref[:, cols]
            p = _softmax_rows(qh, kh, scale)
            dp = lax.dot_general(doh, vh, NT, preferred_element_type=jnp.float32)
            dv_ref[:, cols] += lax.dot_general(p.astype(MXU_DTYPE), doh, TN, preferred_element_type=jnp.float32)
            ds = (p * (dp - jnp.sum(dp * p, axis=-1, keepdims=True)) * scale).astype(MXU_DTYPE)
            dq_ref[:, cols] = jnp.dot(ds, kh, preferred_element_type=jnp.float32).astype(dq_ref.dtype)
            dk_ref[:, cols] += lax.dot_general(ds, qh, TN, preferred_element_type=jnp.float32)

    row = pl.BlockSpec((tm, D), lambda i: (i, 0))
    mem = pl.BlockSpec((M, D), lambda i: (0, 0))
    return pl.pallas_call(
        body, name="attn_bwd", grid=(S // tm,), in_specs=[row, mem, mem, row], out_specs=[row, mem, mem],
        out_shape=[jax.ShapeDtypeStruct((S, D), MXU_DTYPE), jax.ShapeDtypeStruct((M, D), jnp.float32),
                   jax.ShapeDtypeStruct((M, D), jnp.float32)],
        compiler_params=_params(1),
    )(q, k, v, do)


def _proj_residual_norm(name, a, w2d, resid, g_row, tm=256, after=None):
    M, K = a.shape
    N = w2d.shape[1]
    tm = _tile(M, tm, 16)

    def epilogue(accs, ex_refs, out_refs, i, j):
        x_new = ex_refs[0][...] + accs[0]
        out_refs[0][...] = x_new
        out_refs[1][...] = (x_new * _rsqrt_mean_sq(x_new) * ex_refs[1][...]).astype(MXU_DTYPE)

    row = lambda i, j, k: (i, 0)
    return _fused_matmul(
        name, (M // tm, 1, 1),
        [(a, (tm, K), row), (w2d, (K, N), lambda i, j, k: (0, 0))],
        [(0, 1, 0, NN)], [(tm, N)],
        [(resid, (tm, N), row), (g_row, (1, N), lambda i, j, k: (0, 0))] + _order_after(after),
        [((M, N), jnp.float32, (tm, N), row), ((M, N), MXU_DTYPE, (tm, N), row)], epilogue)


def _ffn_gate_up(h, wg3, wu3, tm=512):
    M, K = h.shape
    J, _, Nc = wg3.shape
    tm = _tile(M, tm, 16)

    def epilogue(accs, ex_refs, out_refs, i, j):
        gate, up = accs
        sg = _sigmoid(gate)
        silu = gate * sg
        out_refs[0][...] = (silu * up).astype(MXU_DTYPE)
        out_refs[1][...] = (up * (sg * (1.0 + gate * (1.0 - sg)))).astype(MXU_DTYPE)
        out_refs[2][...] = silu.astype(MXU_DTYPE)

    col = lambda i, j, k: (i, j)
    wspec = lambda i, j, k: (j, 0, 0)
    return _fused_matmul(
        "ffn_gate_up", (M // tm, J, 1),
        [(h, (tm, K), lambda i, j, k: (i, 0)), (wg3, (None, K, Nc), wspec), (wu3, (None, K, Nc), wspec)],
        [(0, 1, 0, NN), (0, 2, 1, NN)], [(tm, Nc)] * 2, [],
        [((M, J * Nc), MXU_DTYPE, (tm, Nc), col)] * 3, epilogue)


def _ffn_down(act, wd2d, x2, tm=512, tn=512):
    M, F = act.shape
    D = wd2d.shape[1]
    tm, tn = _tile(M, tm, 16), _tile(D, tn, LANES)

    def epilogue(accs, ex_refs, out_refs, i, j):
        out_refs[0][...] = ex_refs[0][...] + accs[0]

    tile = lambda i, j, k: (i, j)
    return _fused_matmul(
        "ffn_down", (M // tm, D // tn, 1),
        [(act, (tm, F), lambda i, j, k: (i, 0)), (wd2d, (F, tn), lambda i, j, k: (0, j))],
        [(0, 1, 0, NN)], [(tm, tn)], [(x2, (tm, tn), tile)],
        [((M, D), jnp.float32, (tm, tn), tile)], epilogue)[0]


def _final_norm_loss(x3, target, g_row, tm=256):
    S, D = x3.shape
    tm = _tile(S, tm, 16)

    def body(x_ref, t_ref, g_ref, dx_ref, dxl_ref, loss_ref, dg_ref):
        x3v = x_ref[...]
        g = g_ref[...]
        r = _rsqrt_mean_sq(x3v)
        xhat = x3v * r
        diff = xhat * g - t_ref[...]
        dy = diff / D
        dyg = dy * g
        dx = r * (dyg - xhat * jnp.mean(dyg * xhat, axis=-1, keepdims=True))
        dx_ref[...] = dx
        dxl_ref[...] = dx.astype(MXU_DTYPE)

        @pl.when(pl.program_id(0) == 0)
        def _():
            loss_ref[...] = jnp.zeros_like(loss_ref)
            dg_ref[...] = jnp.zeros_like(dg_ref)

        loss_ref[...] += 0.5 * _sum_all(jnp.mean(diff * diff, axis=-1, keepdims=True))
        dg_ref[...] += jnp.sum(dy * xhat, axis=0, keepdims=True)

    row = pl.BlockSpec((tm, D), lambda i: (i, 0))
    vec = pl.BlockSpec((1, D), lambda i: (0, 0))
    return pl.pallas_call(
        body, name="final_norm_loss", grid=(S // tm,), in_specs=[row, row, vec],
        out_specs=[row, row, pl.BlockSpec((1, 1), lambda i: (0, 0)), vec],
        out_shape=[jax.ShapeDtypeStruct((S, D), jnp.float32), jax.ShapeDtypeStruct((S, D), MXU_DTYPE),
                   jax.ShapeDtypeStruct((1, 1), jnp.float32), jax.ShapeDtypeStruct((1, D), jnp.float32)],
        compiler_params=_params(1),
    )(x3, target, g_row)


def _ffn_down_bwd(dx3_low, wd2d, act_by_gate, act_by_up, n_shards, tm=512):
    M, D = dx3_low.shape
    F = wd2d.shape[0]
    tn = F // n_shards
    tm = _tile(M, tm, 16)

    def epilogue(accs, ex_refs, out_refs, i, j):
        dact = accs[0]
        out_refs[0][...] = (dact * ex_refs[0][...].astype(jnp.float32)).astype(MXU_DTYPE)
        out_refs[1][...] = (dact * ex_refs[1][...].astype(jnp.float32)).astype(MXU_DTYPE)

    col = lambda i, j, k: (i, j)
    return _fused_matmul(
        "ffn_down_bwd", (M // tm, n_shards, 1),
        [(dx3_low, (tm, D), lambda i, j, k: (i, 0)), (wd2d, (tn, D), lambda i, j, k: (j, 0))],
        [(0, 1, 0, NT)], [(tm, tn)],
        [(act_by_gate, (tm, tn), col), (act_by_up, (tm, tn), col)],
        [((M, F), MXU_DTYPE, (tm, tn), col)] * 2, epilogue)


def _matmul_nt_norm_bwd(name, a, b, xin, g_row, dres, low_dtype, tm=256, after=None):
    M = a.shape[0]
    sharded = b.ndim == 3
    N, Kc = b.shape[-2:]
    J = b.shape[0] if sharded else 1
    tm = _tile(M, tm, 16)
    has_low = low_dtype is not None

    def epilogue(accs, ex_refs, out_refs, i, j):
        dx, dg = _rmsnorm_bwd_values(accs[0], ex_refs[0][...], ex_refs[1][...])
        dx = dx + ex_refs[2][...]
        out_refs[0][...] = dx
        if has_low:
            out_refs[1][...] = dx.astype(low_dtype)
        dg_ref = out_refs[-1]

        @pl.when(i == 0)
        def _():
            dg_ref[...] = jnp.zeros_like(dg_ref)

        dg_ref[...] += dg

    row = lambda i, j, k: (i, 0)
    const = lambda i, j, k: (0, 0)
    b_operand = (b, (None, N, Kc), lambda i, j, k: (k, 0, 0)) if sharded else (b, (N, Kc), const)
    outs = [((M, N), jnp.float32, (tm, N), row)]
    if has_low:
        outs.append(((M, N), low_dtype, (tm, N), row))
    outs.append(((1, N), jnp.float32, (1, N), const))
    return _fused_matmul(
        name, (M // tm, 1, J), [(a, (tm, Kc), lambda i, j, k: (i, k)), b_operand],
        [(0, 1, 0, NT)], [(tm, N)],
        [(xin, (tm, N), row), (g_row, (1, N), const), (dres, (tm, N), row)] + _order_after(after),
        outs, epilogue)


def _row_tile(R, C, target_bytes=2 ** 21):
    return _tile(R, max(16, target_bytes // (4 * C)), 16)


def _prefetch_call(body, name, grid, pos, in_specs, out_specs, out_shape, ins, after=None):
    ordered = [] if after is None else [after]
    grid_spec = pltpu.PrefetchScalarGridSpec(
        num_scalar_prefetch=1, grid=grid, in_specs=list(in_specs) + [ANY] * len(ordered), out_specs=out_specs)
    return pl.pallas_call(body, name=name, grid_spec=grid_spec, out_shape=out_shape,
                          compiler_params=_params(len(grid)))(pos, *ins, *ordered)


def _runs(arrays):
    runs = []
    for i, a in enumerate(arrays):
        if runs and arrays[runs[-1][-1]].shape == a.shape:
            runs[-1].append(i)
        else:
            runs.append([i])
    return runs


def _per_run(fn, names, *array_lists):
    out = [None] * len(names)
    for run in _runs(array_lists[0]):
        results = fn("_".join(names[i] for i in run), *[[arrays[i] for i in run] for arrays in array_lists])
        for i, r in zip(run, results):
            out[i] = r
    return out


def _cast_into_slabs(name, xs, dtype, pos, after=None):
    n = len(xs)
    R, C = xs[0].shape
    tr = _row_tile(R, C * n, 2 ** 22)

    def body(pos_ref, *refs):
        for t in range(n):
            refs[len(refs) - n + t][...] = refs[t][...].astype(dtype)

    return _prefetch_call(
        body, "cast_" + name, (R // tr,), pos, [pl.BlockSpec((tr, C), lambda i, pos: (i, 0))] * n,
        [pl.BlockSpec((None, tr, C), lambda i, pos: (pos[0], i, 0))] * n,
        [jax.ShapeDtypeStruct((N_CHIPS, R, C), dtype)] * n, xs, after)


def _chip_sum(name, partials, gots, dtype, pos):
    n = len(partials)
    J, R, C = partials[0].shape
    Rh = R // 2
    tr = _row_tile(Rh, C * n, 2 ** 22)

    def body(pos_ref, *refs):
        for t in range(n):
            refs[2 * n + t][...] = (refs[t][...].astype(jnp.float32) + refs[n + t][...].astype(jnp.float32)).astype(dtype)

    blk = pl.BlockSpec((None, tr, C), lambda j, i, pos: (j, i, 0))
    return _prefetch_call(
        body, "chip_sum_" + name, (J, Rh // tr), pos,
        [pl.BlockSpec((None, None, tr, C), lambda j, i, pos: (j, pos[1], i, 0))] * n + [blk] * n, [blk] * n,
        [jax.ShapeDtypeStruct((J, Rh, C), dtype)] * n, [p.reshape(J, 2, Rh, C) for p in partials] + list(gots))


def _owner_sum(name, chip_sums, others, pos):
    n = len(chip_sums)
    J, Rh, C = chip_sums[0].shape
    n_other = others[0].shape[0]
    tr = _row_tile(Rh, C * (n_other + 1) * n, 2 ** 22)

    def body(pos_ref, *refs):
        for t in range(n):
            acc = refs[t][...].astype(jnp.float32)
            for k in range(n_other):
                acc = acc + refs[n + t][k].astype(jnp.float32)
            refs[2 * n + t][...] = acc

    return _prefetch_call(
        body, "owner_sum_" + name, (Rh // tr,), pos,
        [pl.BlockSpec((None, tr, C), lambda i, pos: (pos[0], i, 0))] * n
        + [pl.BlockSpec((n_other, tr, C), lambda i, pos: (0, i, 0))] * n,
        [pl.BlockSpec((None, tr, C), lambda i, pos: (pos[1], i, 0))] * n,
        [jax.ShapeDtypeStruct((2, Rh, C), jnp.float32)] * n, list(chip_sums) + list(others))


def _adamw_values(w, g, m, v):
    m = ADAM_B1 * m + (1.0 - ADAM_B1) * g
    v = ADAM_B2 * v + (1.0 - ADAM_B2) * (g * g)
    m_hat = m / (1.0 - ADAM_B1 ** ADAM_STEP)
    v_hat = v / (1.0 - ADAM_B2 ** ADAM_STEP)
    delta = -ADAM_LR * (m_hat / (jnp.sqrt(v_hat) + ADAM_EPS) + ADAM_WD * w)
    return delta, m, v


def _adamw(name, ws, gs, ms, vs):
    n = len(ws)
    R, C = ws[0].shape
    tr = _row_tile(R, C * n, 2 ** 20)

    def body(*refs):
        w_refs, g_refs, m_refs, v_refs = (refs[q * n:(q + 1) * n] for q in range(4))
        outs = refs[4 * n:]
        for t in range(n):
            g = g_refs[t][...]
            outs[4 * t][...] = g
            outs[4 * t + 1][...], outs[4 * t + 2][...], outs[4 * t + 3][...] = _adamw_values(
                w_refs[t][...], g, m_refs[t][...], v_refs[t][...])

    blk = pl.BlockSpec((tr, C), lambda i: (i, 0))
    res = pl.pallas_call(
        body, name="adamw_" + name, grid=(R // tr,), in_specs=[blk] * (4 * n), out_specs=[blk] * (4 * n),
        out_shape=[jax.ShapeDtypeStruct((R, C), jnp.float32)] * (4 * n), compiler_params=_params(1),
    )(*ws, *gs, *ms, *vs)
    return [tuple(res[4 * t:4 * t + 4]) for t in range(n)]


def _adamw_small(g_packed, ws, ms, vs, row_counts):
    n = len(ws)

    def body(*refs):
        g_ref = refs[0]
        w_refs, m_refs, v_refs = refs[1:1 + n], refs[1 + n:1 + 2 * n], refs[1 + 2 * n:1 + 3 * n]
        outs = refs[1 + 3 * n:]
        off = 0
        for t in range(n):
            g = g_ref[off:off + row_counts[t], :]
            off += row_counts[t]
            delta, m_new, v_new = _adamw_values(w_refs[t][...], g, m_refs[t][...], v_refs[t][...])
            outs[4 * t][...] = g
            outs[4 * t + 1][...] = delta
            outs[4 * t + 2][...] = m_new
            outs[4 * t + 3][...] = v_new

    out_shape = []
    for t in range(n):
        out_shape += [jax.ShapeDtypeStruct((row_counts[t], LANES), jnp.float32)] * 4
    return pl.pallas_call(
        body, name="adamw_small", in_specs=[VMEM_WHOLE] * (1 + 3 * n), out_specs=[VMEM_WHOLE] * (4 * n),
        out_shape=out_shape, compiler_params=pltpu.CompilerParams(vmem_limit_bytes=VMEM_LIMIT_BYTES),
    )(g_packed, *ws, *ms, *vs)


def _position():
    x, y, c = lax.axis_index("x"), lax.axis_index("y"), lax.axis_index("c")
    other_chips = [(1 - x, y), (x, 1 - y), (1 - x, 1 - y)]
    return x, y, c, other_chips


def _half(rows, which):
    return pl.ds(which * (rows // 2), rows // 2)


def _chunk_index(shape, dtype):
    rows = shape[-2]
    total = math.prod(shape) * jnp.dtype(dtype).itemsize
    n = max(1, min(rows // COPY_ROW_ALIGN, total // COPY_CHUNK_BYTES))
    step = -(-rows // n)
    step = -(-step // COPY_ROW_ALIGN) * COPY_ROW_ALIGN
    lead = (slice(None),) * (len(shape) - 2)
    return [lead + (pl.ds(s, min(step, rows - s)),) for s in range(0, rows, step)]


def _start_in_chunks(make, src, dst, waited_here=True):
    for idx in _chunk_index(src.shape, src.dtype):
        make(src.at[idx], dst.at[idx]).start()
    return make(src, dst) if waited_here else None


def _remote_copier(send_sem, recv_sem, to):
    return lambda src, dst: pltpu.make_async_remote_copy(
        src_ref=src, dst_ref=dst, send_sem=send_sem, recv_sem=recv_sem, device_id=to, device_id_type=MESH)


def _slab_half(ref, chip, which):
    return ref.at[chip, _half(ref.shape[1], which)]


def _slab_quarter(ref, chip, which, q):
    rows = ref.shape[1] // 4
    return ref.at[chip, pl.ds(which * 2 * rows + q * rows, rows)]


def _gather_step(name, arrived=(), ring=(), direct=(), after=None):
    groups = [g[2] for g in arrived] + list(direct) + list(ring)
    per_tensor = [2] * len(arrived) + [3] * len(direct) + [2] * len(ring)
    flat = [a for g in groups for a in g]
    T, G, A = len(flat), len(groups), len(arrived)
    ordered = [] if after is None else [after]
    sems_in = [sem for g in arrived for sem in g[:2]]

    def body(*refs):
        ins = refs[:T]
        first_hop = refs[T:T + 2 * A]
        out_sems = refs[T + 2 * A + len(ordered):T + 2 * A + len(ordered) + 2 * G]
        token = refs[-1]
        x, y, c, other_chips = _position()
        mine = 2 * x + y
        x_nbr, y_nbr = (1 - x, y, c), (x, 1 - y, c)
        from_x, from_y = 2 * (1 - x) + y, 2 * x + (1 - y)
        t0 = 0
        for gi, group in enumerate(groups):
            send, recv = out_sems[2 * gi], out_sems[2 * gi + 1]
            for t in range(len(group)):
                ref = ins[t0 + t]
                own = _slab_half(ref, mine, c)
                if gi < A:
                    send1, recv1 = first_hop[2 * gi], first_hop[2 * gi + 1]
                    for k, (peer, chip) in enumerate(((x_nbr, from_x), (y_nbr, from_y))):
                        cp = _remote_copier(send1.at[2 * t + k], recv1.at[2 * t + k], peer)(own, _slab_half(ref, chip, c))
                        cp.wait_send()
                        cp.wait_recv()
                    for k, (peer, chip) in enumerate(((y_nbr, from_x), (x_nbr, from_y))):
                        piece = _slab_quarter(ref, chip, c, k)
                        _start_in_chunks(_remote_copier(send.at[2 * t + k], recv.at[2 * t + k], peer), piece, piece,
                                         waited_here=False)
                elif gi < A + len(direct):
                    for k, (cx, cy) in enumerate(other_chips):
                        _start_in_chunks(_remote_copier(send.at[3 * t + k], recv.at[3 * t + k], (cx, cy, c)), own, own,
                                         waited_here=False)
                else:
                    for k, peer in enumerate((x_nbr, y_nbr)):
                        _start_in_chunks(_remote_copier(send.at[2 * t + k], recv.at[2 * t + k], peer), own, own,
                                         waited_here=False)
            t0 += len(group)
        token[...] = jnp.zeros_like(token)

    sem_shapes = []
    for group, n in zip(groups, per_tensor):
        sem_shapes += [pltpu.SemaphoreType.DMA((n * len(group),))] * 2
    res = pl.pallas_call(
        body, name=name, in_specs=[HBM_SPEC] * T + [SEM_SPEC] * (2 * A) + [ANY] * len(ordered),
        out_specs=[SEM_SPEC] * (2 * G) + [HBM_SPEC] * T + [VMEM_WHOLE],
        out_shape=sem_shapes + [pltpu.HBM(a.shape, a.dtype) for a in flat] + [jax.ShapeDtypeStruct((8, LANES), jnp.float32)],
        input_output_aliases={t: 2 * G + t for t in range(T)},
        compiler_params=pltpu.CompilerParams(has_side_effects=SIDE_EFFECT),
    )(*[pltpu.with_memory_space_constraint(a, pltpu.HBM) for a in flat], *sems_in, *ordered)
    flights, t0 = [], 2 * G
    for gi, group in enumerate(groups):
        flights.append((res[2 * gi], res[2 * gi + 1], res[t0:t0 + len(group)]))
        t0 += len(group)
    return flights[:A], flights[A:A + len(direct)], flights[A + len(direct):], res[-1]


def _relayed_wait(name, flight, after):
    send_sems, recv_sems, slabs = flight
    T = len(slabs)

    def body(*refs):
        ins = refs[:T]
        send, recv = refs[T:T + 2]
        x, y, c, _ = _position()
        x_nbr, y_nbr = (1 - x, y, c), (x, 1 - y, c)
        from_x, from_y, diagonal = 2 * (1 - x) + y, 2 * x + (1 - y), 2 * (1 - x) + (1 - y)
        for t in range(T):
            for k, (peer, chip) in enumerate(((y_nbr, from_x), (x_nbr, from_y))):
                cp = _remote_copier(send.at[2 * t + k], recv.at[2 * t + k], peer)(
                    _slab_quarter(ins[t], chip, c, k), _slab_quarter(ins[t], diagonal, c, k))
                cp.wait_send()
                cp.wait_recv()

    return pl.pallas_call(
        body, name=name, in_specs=[HBM_SPEC] * T + [SEM_SPEC, SEM_SPEC, ANY], out_specs=[HBM_SPEC] * T,
        out_shape=[pltpu.HBM(a.shape, a.dtype) for a in slabs], input_output_aliases={t: t for t in range(T)},
        compiler_params=pltpu.CompilerParams(has_side_effects=SIDE_EFFECT),
    )(*slabs, send_sems, recv_sems, after)


def _gather_wait(name, flight, after):
    send_sems, recv_sems, slabs = flight
    T = len(slabs)

    def body(*refs):
        ins = refs[:T]
        send, recv = refs[T:T + 2]
        x, y, c, other_chips = _position()
        mine = 2 * x + y
        for t in range(T):
            own = _slab_half(ins[t], mine, c)
            for k, (cx, cy) in enumerate(other_chips):
                landed = _slab_half(ins[t], 2 * cx + cy, c)
                cp = _remote_copier(send.at[3 * t + k], recv.at[3 * t + k], (cx, cy, c))(own, landed)
                cp.wait_send()
                cp.wait_recv()

    return pl.pallas_call(
        body, name=name, in_specs=[HBM_SPEC] * T + [SEM_SPEC, SEM_SPEC, ANY], out_specs=[HBM_SPEC] * T,
        out_shape=[pltpu.HBM(s.shape, s.dtype) for s in slabs], input_output_aliases={t: t for t in range(T)},
        compiler_params=pltpu.CompilerParams(has_side_effects=SIDE_EFFECT),
    )(*slabs, send_sems, recv_sems, after)


def _share_with_sibling(name, slabs):
    T = len(slabs)

    def body(*refs):
        outs = refs[T:2 * T]
        send_sems, recv_sems = refs[2 * T:]
        x, y, c, other_chips = _position()
        sibling = (x, y, 1 - c)
        started = []
        for t in range(T):
            for k, (cx, cy) in enumerate(other_chips):
                landed = _slab_half(outs[t], 2 * cx + cy, c)
                started.append(_start_in_chunks(_remote_copier(send_sems.at[t, k], recv_sems.at[t, k], sibling), landed, landed))
        for t in range(T):
            for k, (cx, cy) in enumerate(other_chips):
                arriving = _slab_half(outs[t], 2 * cx + cy, 1 - c)
                _remote_copier(send_sems.at[t, k], recv_sems.at[t, k], sibling)(arriving, arriving).wait_recv()
        for cp in started:
            cp.wait_send()

    return pl.pallas_call(
        body, name=name, in_specs=[ANY] * T, out_specs=[ANY] * T,
        out_shape=[jax.ShapeDtypeStruct(s.shape, s.dtype) for s in slabs],
        input_output_aliases={t: t for t in range(T)},
        scratch_shapes=[pltpu.SemaphoreType.DMA((T, 3)), pltpu.SemaphoreType.DMA((T, 3))],
    )(*slabs)


def _swap_halves(name, grads):
    T = len(grads)

    def body(*refs):
        ins, got = refs[:T], refs[T:2 * T]
        send_sems, recv_sems = refs[2 * T:]
        x, y, c, _ = _position()
        remote = []
        for t in range(T):
            rows = ins[t].shape[1]
            remote.append(_start_in_chunks(
                _remote_copier(send_sems.at[t], recv_sems.at[t], (x, y, 1 - c)), ins[t].at[:, _half(rows, 1 - c)], got[t]))
        for cp in remote:
            cp.wait()

    return pl.pallas_call(
        body, name=name, in_specs=[ANY] * T, out_specs=[ANY] * T,
        out_shape=[jax.ShapeDtypeStruct((g.shape[0], g.shape[1] // 2, g.shape[2]), g.dtype) for g in grads],
        scratch_shapes=[pltpu.SemaphoreType.DMA((T,)), pltpu.SemaphoreType.DMA((T,))],
    )(*grads)


def _swap_start(name, grads):
    T = len(grads)
    lands = [lax.empty((g.shape[0], g.shape[1] // 2, g.shape[2]), g.dtype) for g in grads]

    def body(*refs):
        ins, got = refs[:T], refs[T:2 * T]
        send_sems, recv_sems = refs[2 * T:2 * T + 2]
        token = refs[-1]
        x, y, c, _ = _position()
        for t in range(T):
            _start_in_chunks(_remote_copier(send_sems.at[t], recv_sems.at[t], (x, y, 1 - c)),
                             ins[t].at[:, _half(ins[t].shape[1], 1 - c)], got[t], waited_here=False)
        token[...] = jnp.zeros_like(token)

    arrays = list(grads) + lands
    res = pl.pallas_call(
        body, name=name, in_specs=[HBM_SPEC] * (2 * T),
        out_specs=[SEM_SPEC, SEM_SPEC] + [HBM_SPEC] * (2 * T) + [VMEM_WHOLE],
        out_shape=[pltpu.SemaphoreType.DMA((T,)), pltpu.SemaphoreType.DMA((T,))]
        + [pltpu.HBM(a.shape, a.dtype) for a in arrays] + [jax.ShapeDtypeStruct((8, LANES), jnp.float32)],
        input_output_aliases={i: 2 + i for i in range(2 * T)},
        compiler_params=pltpu.CompilerParams(has_side_effects=SIDE_EFFECT),
    )(*[pltpu.with_memory_space_constraint(a, pltpu.HBM) for a in arrays])
    return res[0], res[1], res[2:2 + T], res[2 + T:2 + 2 * T], res[-1]


def _swap_wait(name, flight, after):
    send_sems, recv_sems, grads, lands, _ = flight
    T = len(grads)

    def body(*refs):
        ins, got = refs[:T], refs[T:2 * T]
        send, recv = refs[2 * T:2 * T + 2]
        x, y, c, _ = _position()
        for t in range(T):
            cp = _remote_copier(send.at[t], recv.at[t], (x, y, 1 - c))(ins[t].at[:, _half(ins[t].shape[1], 1 - c)], got[t])
            cp.wait_send()
            cp.wait_recv()

    arrays = list(grads) + list(lands)
    res = pl.pallas_call(
        body, name=name, in_specs=[HBM_SPEC] * (2 * T) + [SEM_SPEC, SEM_SPEC, ANY], out_specs=[HBM_SPEC] * (2 * T),
        out_shape=[pltpu.HBM(a.shape, a.dtype) for a in arrays], input_output_aliases={i: i for i in range(2 * T)},
        compiler_params=pltpu.CompilerParams(has_side_effects=SIDE_EFFECT),
    )(*arrays, send_sems, recv_sems, after)
    return res[:T], res[T:]


def _scatter_start(name, chip_sums):
    T = len(chip_sums)
    lands = [lax.empty((3,) + s.shape[1:], s.dtype) for s in chip_sums]

    def body(*refs):
        srcs, dsts = refs[:T], refs[T:2 * T]
        send_sems, recv_sems = refs[2 * T:2 * T + 2]
        token = refs[-1]
        x, y, c, other_chips = _position()
        for t in range(T):
            for k, (cx, cy) in enumerate(other_chips):
                _start_in_chunks(_remote_copier(send_sems.at[3 * t + k], recv_sems.at[3 * t + k], (cx, cy, c)),
                                 srcs[t].at[2 * cx + cy], dsts[t].at[k], waited_here=False)
        token[...] = jnp.zeros_like(token)

    hbm = [pltpu.HBM(a.shape, a.dtype) for a in list(chip_sums) + lands]
    res = pl.pallas_call(
        body, name=name, in_specs=[HBM_SPEC] * (2 * T),
        out_specs=[SEM_SPEC, SEM_SPEC] + [HBM_SPEC] * (2 * T) + [VMEM_WHOLE],
        out_shape=[pltpu.SemaphoreType.DMA((3 * T,)), pltpu.SemaphoreType.DMA((3 * T,))] + hbm
        + [jax.ShapeDtypeStruct((8, LANES), jnp.float32)],
        input_output_aliases={i: 2 + i for i in range(2 * T)},
        compiler_params=pltpu.CompilerParams(has_side_effects=SIDE_EFFECT),
    )(*[pltpu.with_memory_space_constraint(a, pltpu.HBM) for a in list(chip_sums) + lands])
    return res[0], res[1], res[2:2 + T], res[2 + T:2 + 2 * T], res[-1]


def _scatter_wait(name, send_sems, recv_sems, chip_sums, lands, after):
    T = len(chip_sums)

    def body(*refs):
        srcs, dsts = refs[:T], refs[T:2 * T]
        send, recv = refs[2 * T:2 * T + 2]
        x, y, c, other_chips = _position()
        for t in range(T):
            for k, (cx, cy) in enumerate(other_chips):
                cp = _remote_copier(send.at[3 * t + k], recv.at[3 * t + k], (cx, cy, c))(srcs[t].at[2 * cx + cy], dsts[t].at[k])
                cp.wait_send()
                cp.wait_recv()

    arrays = list(chip_sums) + list(lands)
    res = pl.pallas_call(
        body, name=name, in_specs=[HBM_SPEC] * (2 * T) + [SEM_SPEC, SEM_SPEC, ANY],
        out_specs=[HBM_SPEC] * (2 * T), out_shape=[pltpu.HBM(a.shape, a.dtype) for a in arrays],
        input_output_aliases={i: i for i in range(2 * T)},
        compiler_params=pltpu.CompilerParams(has_side_effects=SIDE_EFFECT),
    )(*arrays, send_sems, recv_sems, after)
    return res[:T], res[T:]


def _as_shards(partial):
    return [p.reshape((N_CHIPS, -1, p.shape[-1])) for p in partial]


def _reduce_begin(tag, names, partial, pos):
    parts = _as_shards(partial)
    received = _swap_halves("swap_grad_halves_" + tag, parts)
    return _reduce_scatter_start(tag, names, parts, received, pos)


def _reduce_scatter_start(tag, names, parts, received, pos):
    chip_sums = _per_run(lambda name, p, r: _chip_sum(name, p, r, WIRE_DTYPE, pos), names, list(parts), list(received))
    return _scatter_start("scatter_start_" + tag, chip_sums)


def _reduce_end(tag, names, flight, after, pos):
    send_sems, recv_sems, chip_sums, lands, _ = flight
    chip_sums, others = _scatter_wait("scatter_wait_" + tag, send_sems, recv_sems, chip_sums, lands, after)
    reduced = _per_run(lambda name, cs, ot: _owner_sum(name, cs, ot, pos), names, list(chip_sums), list(others))
    return dict(zip(names, reduced))


def _join_start(name, halves):
    T = len(halves)

    def body(*refs):
        ins = refs[:T]
        send_sems, recv_sems = refs[T:T + 2]
        token = refs[-1]
        x, y, c, _ = _position()
        for t in range(T):
            _start_in_chunks(_remote_copier(send_sems.at[t], recv_sems.at[t], (x, y, 1 - c)),
                             ins[t].at[c], ins[t].at[c], waited_here=False)
        token[...] = jnp.zeros_like(token)

    res = pl.pallas_call(
        body, name=name, in_specs=[HBM_SPEC] * T,
        out_specs=[SEM_SPEC, SEM_SPEC] + [HBM_SPEC] * T + [VMEM_WHOLE],
        out_shape=[pltpu.SemaphoreType.DMA((T,)), pltpu.SemaphoreType.DMA((T,))]
        + [pltpu.HBM(h.shape, h.dtype) for h in halves] + [jax.ShapeDtypeStruct((8, LANES), jnp.float32)],
        input_output_aliases={t: 2 + t for t in range(T)},
        compiler_params=pltpu.CompilerParams(has_side_effects=SIDE_EFFECT),
    )(*[pltpu.with_memory_space_constraint(h, pltpu.HBM) for h in halves])
    return res[0], res[1], res[2:2 + T], res[-1]


def _join_wait(name, flight, after):
    send_sems, recv_sems, halves, _ = flight
    T = len(halves)

    def body(*refs):
        ins = refs[:T]
        send, recv = refs[T:T + 2]
        x, y, c, _ = _position()
        for t in range(T):
            cp = _remote_copier(send.at[t], recv.at[t], (x, y, 1 - c))(ins[t].at[c], ins[t].at[1 - c])
            cp.wait_send()
            cp.wait_recv()

    return pl.pallas_call(
        body, name=name, in_specs=[HBM_SPEC] * T + [SEM_SPEC, SEM_SPEC, ANY], out_specs=[HBM_SPEC] * T,
        out_shape=[pltpu.HBM(h.shape, h.dtype) for h in halves], input_output_aliases={t: t for t in range(T)},
        compiler_params=pltpu.CompilerParams(has_side_effects=SIDE_EFFECT),
    )(*halves, send_sems, recv_sems, after)


def _peers():
    x, y, c, _ = _position()
    flips = [(fx, fy, fc) for fx in (0, 1) for fy in (0, 1) for fc in (0, 1)][1:]
    return [(jnp.bitwise_xor(x, fx), jnp.bitwise_xor(y, fy), jnp.bitwise_xor(c, fc)) for fx, fy, fc in flips]


def _device_index(px, py, pc):
    return 4 * px + 2 * py + pc


def _exchange_start(name, slots):
    def body(slots_ref, send_sems, recv_sems, thru, token):
        x, y, c, _ = _position()
        own = slots_ref.at[_device_index(x, y, c)]
        for k, peer in enumerate(_peers()):
            _remote_copier(send_sems.at[k], recv_sems.at[k], peer)(own, own).start()
        token[...] = jnp.zeros_like(token)

    res = pl.pallas_call(
        body, name=name, in_specs=[HBM_SPEC], out_specs=[SEM_SPEC, SEM_SPEC, HBM_SPEC, VMEM_WHOLE],
        out_shape=[pltpu.SemaphoreType.DMA((N_DEV - 1,)), pltpu.SemaphoreType.DMA((N_DEV - 1,)),
                   pltpu.HBM(slots.shape, slots.dtype), jax.ShapeDtypeStruct((8, LANES), jnp.float32)],
        input_output_aliases={0: 2}, compiler_params=pltpu.CompilerParams(has_side_effects=SIDE_EFFECT),
    )(pltpu.with_memory_space_constraint(slots, pltpu.HBM))
    return res[0], res[1], res[2], res[3]


def _exchange_wait(name, flight, after):
    send_sems, recv_sems, slots, _ = flight

    def body(slots_ref, send, recv, after_ref, out_ref):
        x, y, c, _ = _position()
        own = slots_ref.at[_device_index(x, y, c)]
        for k, peer in enumerate(_peers()):
            cp = _remote_copier(send.at[k], recv.at[k], peer)(own, slots_ref.at[_device_index(*peer)])
            cp.wait_send()
            cp.wait_recv()

    return pl.pallas_call(
        body, name=name, in_specs=[HBM_SPEC, SEM_SPEC, SEM_SPEC, ANY], out_specs=HBM_SPEC,
        out_shape=pltpu.HBM(slots.shape, slots.dtype), input_output_aliases={0: 0},
        compiler_params=pltpu.CompilerParams(has_side_effects=SIDE_EFFECT),
    )(slots, send_sems, recv_sems, after)


def _sum_slots(name, slots):
    n, rows, lanes = slots.shape
    tr = _row_tile(rows, lanes * n)

    def body(s_ref, o_ref):
        acc = s_ref[0]
        for d in range(1, n):
            acc = acc + s_ref[d]
        o_ref[...] = acc

    return pl.pallas_call(
        body, name=name, grid=(rows // tr,), in_specs=[pl.BlockSpec((n, tr, lanes), lambda i: (0, i, 0))],
        out_specs=pl.BlockSpec((tr, lanes), lambda i: (i, 0)), out_shape=jax.ShapeDtypeStruct((rows, lanes), jnp.float32),
        compiler_params=_params(1))(slots)


BIG = ("w_in", "pool_w", "w_out", "w_q", "w_k", "w_v", "w_o", "w_gate", "w_up", "w_down")
SMALL = ("norm_mix_g", "pool_scale", "sgu_norm_g", "w_spatial", "b_spatial", "norm_xattn_g", "norm_mem_g",
         "norm_ffn_g", "final_norm_g")
GATHER_ORDER = (("w_in", "pool_w"), ("w_out",), ("w_q", "w_k", "w_v", "w_o"), ("w_gate", "w_up"), ("w_down",))
WEIGHTS = ("norm_mix_g", "w_in", "pool_w", "pool_scale", "sgu_norm_g", "w_spatial", "b_spatial", "w_out",
           "norm_xattn_g", "norm_mem_g", "w_q", "w_k", "w_v", "w_o", "norm_ffn_g", "w_gate", "w_up", "w_down",
           "final_norm_g")


def _as2d(a):
    return a.reshape(-1, a.shape[-1])


def _as_lanes(a):
    return a.reshape(-1, LANES)


def kernel(x, mem, norm_mix_g, w_in, pool_w, pool_scale, sgu_norm_g, w_spatial, b_spatial, w_out, norm_xattn_g, norm_mem_g, w_q, w_k, w_v, w_o, norm_ffn_g, w_gate, w_up, w_down, final_norm_g, loss_target, m_norm_mix_g, m_w_in, m_pool_w, m_pool_scale, m_sgu_norm_g, m_w_spatial, m_b_spatial, m_w_out, m_norm_xattn_g, m_norm_mem_g, m_w_q, m_w_k, m_w_v, m_w_o, m_norm_ffn_g, m_w_gate, m_w_up, m_w_down, m_final_norm_g, v_norm_mix_g, v_w_in, v_pool_w, v_pool_scale, v_sgu_norm_g, v_w_spatial, v_b_spatial, v_w_out, v_norm_xattn_g, v_norm_mem_g, v_w_q, v_w_k, v_w_v, v_w_o, v_norm_ffn_g, v_w_gate, v_w_up, v_w_down, v_final_norm_g):
    args = dict(locals())
    weights = {n: args[n] for n in WEIGHTS}
    moments_m = {n: args["m_" + n] for n in WEIGHTS}
    moments_v = {n: args["v_" + n] for n in WEIGHTS}
    low = MXU_DTYPE

    xs = x[0]
    mems = mem[0]
    target = loss_target[0]
    D = xs.shape[1]
    G, PG = pool_w.shape[1], pool_w.shape[3]

    pos = jnp.stack([2 * lax.axis_index("x") + lax.axis_index("y"), lax.axis_index("c")]).astype(jnp.int32)
    def cast(names, after=None):
        slabs = _per_run(lambda name, xs: _cast_into_slabs(name, xs, WIRE_DTYPE, pos, after), names,
                         [_as2d(weights[n]) for n in names])
        return dict(zip(names, slabs))

    def land(gi, flight, after, relayed=True):
        wait = _relayed_wait if relayed else _gather_wait
        landed = wait("gather_wait_%d" % gi, flight, after)
        return dict(zip(GATHER_ORDER[gi], _share_with_sibling("gather_share_%d" % gi, landed)))

    def rows(a):
        return a.reshape(-1, a.shape[-1])

    g_mix, g_xattn, g_mem, g_ffn = norm_mix_g, norm_xattn_g, norm_mem_g, norm_ffn_g
    g_final = final_norm_g.reshape(1, D)
    ws = w_spatial[0]
    b_full = jnp.broadcast_to(b_spatial[0][:, :, None], ws.shape[:2] + (sgu_norm_g.shape[1] // ws.shape[0],))

    first = cast(GATHER_ORDER[0])
    _, (sent_0,), _, started = _gather_step("gather_step_0", direct=[[first[n] for n in GATHER_ORDER[0]]])
    rest = cast([n for group in GATHER_ORDER[1:] for n in group], after=started)
    slabs = [[rest[n] for n in group] for group in GATHER_ORDER[1:]]
    _, (sent_1,), (hop_2,), started = _gather_step("gather_step_1", direct=slabs[:1], ring=slabs[1:2])
    h1 = _rmsnorm("norm_mix", xs, g_mix, low, after=started)
    mn = _rmsnorm("norm_mem", mems, g_mem, low)
    got = land(0, sent_0, mn, relayed=False)
    wg_in = got["w_in"]
    pw = got["pool_w"].reshape(N_CHIPS, G, PG // N_CHIPS, PG).transpose(1, 0, 2, 3).reshape(G, PG, PG)
    proj = _matmul_nn_cols("proj_in", h1, wg_in, jnp.float32)
    z = _mixer_fwd(proj, pw, pool_scale, sgu_norm_g, ws, b_full)
    (relay_2,), _, (hop_3,), started = _gather_step("gather_step_2", arrived=[hop_2], ring=slabs[2:3], after=z)
    full = {"w_out": rows(land(1, sent_1, started, relayed=False)["w_out"])}
    x1, h2 = _proj_residual_norm("mix_out", z, full["w_out"], xs, g_xattn)
    full.update({n: rows(a) for n, a in land(2, relay_2, h2).items()})
    q = _matmul_nn("xattn_q", h2, full["w_q"], low)
    k = _matmul_nn("xattn_k", mn, full["w_k"], low)
    vv = _matmul_nn("xattn_v", mn, full["w_v"], low)
    o = _attn_fwd(q, k, vv)
    (relay_3,), (sent_4,), _, started = _gather_step("gather_step_3", arrived=[hop_3], direct=slabs[3:4], after=o)
    x2, h3 = _proj_residual_norm("xattn_out", o, full["w_o"], x1, g_ffn, after=started)
    got = land(3, relay_3, h3)
    wg_gate, wg_up = got["w_gate"], got["w_up"]
    act, act_by_gate, act_by_up = _ffn_gate_up(h3, wg_gate, wg_up)
    full["w_down"] = rows(land(4, sent_4, act, relayed=False)["w_down"])
    x3 = _ffn_down(act, full["w_down"], x2)
    dx3, dx3_low, loss_part, d_final_g = _final_norm_loss(x3, target, g_final)

    dgate, dup = _ffn_down_bwd(dx3_low, full["w_down"], act_by_gate, act_by_up, N_CHIPS)
    (d_w_down,) = _matmul_tn("grad_w_down", act, [dx3_low], WIRE_DTYPE)
    (d_w_gate,) = _matmul_tn_cols("grad_w_gate", h3, [dgate], N_CHIPS, WIRE_DTYPE)
    (d_w_up,) = _matmul_tn_cols("grad_w_up", h3, [dup], N_CHIPS, WIRE_DTYPE)
    ffn_names = ("w_gate", "w_up", "w_down")
    ffn_swap = _swap_start("swap_start_ffn", _as_shards([d_w_gate, d_w_up, d_w_down]))
    dh3 = _matmul_nt_cols("ffn_dh", [dgate, dup], [wg_gate, wg_up], jnp.float32, tn=1024, after=ffn_swap[-1])
    ffn_flight = _reduce_scatter_start("ffn", ffn_names, *_swap_wait("swap_wait_ffn", ffn_swap, dh3), pos)
    dx2, dx2_low, d_ffn_g = _rmsnorm_bwd("norm_ffn_bwd", dh3, x2, g_ffn, dx3, low)

    do = _matmul_nt("xattn_do", [dx2_low], [full["w_o"]], low, after=ffn_flight[-1])
    (d_w_o,) = _matmul_tn("grad_w_o", o, [dx2_low], WIRE_DTYPE)
    dq, dk, dvv = _attn_bwd(q, k, vv, do)
    (d_w_q,) = _matmul_tn("grad_w_q", h2, [dq], WIRE_DTYPE)
    dx1, dx1_low, d_xattn_g = _matmul_nt_norm_bwd("xattn_dh_norm_bwd", dq, full["w_q"], x1, g_xattn, dx2, low)
    d_w_k, d_w_v = _matmul_tn("grad_w_kv", mn, [dk, dvv], WIRE_DTYPE)
    dmn = _matmul_nt("xattn_dmem", [dk, dvv], [full["w_k"], full["w_v"]], jnp.float32)
    (d_mem_g,) = _rmsnorm_bwd("norm_mem_bwd", dmn, mems, g_mem, None, None)

    (d_w_out,) = _matmul_tn("grad_w_out", z, [dx1_low], WIRE_DTYPE)
    attn_names = ("w_q", "w_k", "w_v", "w_o", "w_out")
    attn_swap = _swap_start("swap_start_attn", _as_shards([d_w_q, d_w_k, d_w_v, d_w_o, d_w_out]))
    dz = _matmul_nt("mix_dz", [dx1_low], [full["w_out"]], jnp.float32, after=attn_swap[-1])
    attn_flight = _reduce_scatter_start("attn", attn_names, *_swap_wait("swap_wait_attn", attn_swap, dz), pos)
    dproj, d_pw, d_ps, d_gs, d_ws, d_b = _mixer_bwd(proj, dz, pw, pool_scale, sgu_norm_g, ws, b_full)
    (d_w_in,) = _matmul_tn_cols("grad_w_in", h1, [dproj], N_CHIPS, WIRE_DTYPE, tka=1024, after=attn_flight[-1])
    d_pool_w = d_pw.reshape(G, N_CHIPS, PG // N_CHIPS, PG).transpose(1, 0, 2, 3).reshape(N_CHIPS, PG, PG).astype(WIRE_DTYPE)
    mix_names = ("w_in", "pool_w")
    mix_flight = _reduce_begin("mix", mix_names, [d_w_in, d_pool_w], pos)
    grad_x, d_mix_g = _matmul_nt_norm_bwd("mix_dh_norm_bwd", dproj, wg_in, xs, g_mix, dx1, None, tm=512, after=mix_flight[-1])

    small_parts = {
        "norm_mix_g": d_mix_g, "pool_scale": d_ps, "sgu_norm_g": d_gs, "w_spatial": d_ws, "b_spatial": d_b,
        "norm_xattn_g": d_xattn_g, "norm_mem_g": d_mem_g, "norm_ffn_g": d_ffn_g, "final_norm_g": d_final_g,
    }
    row_counts = [small_parts[n].size // LANES for n in SMALL]
    packed = jnp.concatenate([_as_lanes(small_parts[n]) for n in SMALL], axis=0)
    device = _device_index(lax.axis_index("x"), lax.axis_index("y"), lax.axis_index("c"))
    slots = lax.dynamic_update_slice(jnp.zeros((N_DEV,) + packed.shape, jnp.float32), packed[None], (device, 0, 0))
    small_flight = _exchange_start("small_grads_start", slots)

    groups = (("ffn", ffn_names, ffn_flight), ("attn", attn_names, attn_flight), ("mix", mix_names, mix_flight))
    joins, after = {}, small_flight[-1]
    for tag, names, flight in groups:
        reduced = _reduce_end(tag, names, flight, after, pos)
        joins[tag] = _join_start("join_start_" + tag, [reduced[n] for n in names])
        after = joins[tag][-1]
    small_sum = _sum_slots("small_grads_sum", _exchange_wait("small_grads_wait", small_flight, after))

    out_grad, out_delta, out_m, out_v = {}, {}, {}, {}
    after = small_sum
    for tag, names, _ in groups:
        grads = [g.reshape(-1, g.shape[-1]) for g in _join_wait("join_wait_" + tag, joins[tag], after)]
        updates = _per_run(_adamw, names, [_as2d(weights[n]) for n in names], grads,
                           [_as2d(moments_m[n]) for n in names], [_as2d(moments_v[n]) for n in names])
        for n, update in zip(names, updates):
            shape = weights[n].shape
            out_grad[n], out_delta[n], out_m[n], out_v[n] = (a.reshape(shape) for a in update)
        after = updates[-1][1]
    small_out = _adamw_small(
        small_sum, [_as_lanes(weights[n]) for n in SMALL], [_as_lanes(moments_m[n]) for n in SMALL],
        [_as_lanes(moments_v[n]) for n in SMALL], row_counts)
    for t, n in enumerate(SMALL):
        shape = weights[n].shape
        out_grad[n], out_delta[n], out_m[n], out_v[n] = (a.reshape(shape) for a in small_out[4 * t:4 * t + 4])

    loss = lax.psum(loss_part[0, 0], ("x", "y", "c"))
    return (loss, grad_x[None], *[out_grad[n] for n in WEIGHTS], *[out_delta[n] for n in WEIGHTS],
            *[out_m[n] for n in WEIGHTS], *[out_v[n] for n in WEIGHTS])
```

```python
import math

import jax
import jax.numpy as jnp
from jax import lax
from jax.experimental import pallas as pl
from jax.experimental.pallas import tpu as pltpu

MXU_DTYPE = jnp.bfloat16
WIRE_DTYPE = jnp.bfloat16

EPS = 1e-6
CHUNK = 64
POOL_WINDOWS = (2, 4, 8, 16)
POOL_HALO = 16
N_XATTN_HEADS = 4
ADAM_LR = 0.001
ADAM_B1 = 0.9
ADAM_B2 = 0.999
ADAM_EPS = 1e-08
ADAM_WD = 0.01
ADAM_STEP = 10

N_CHIPS = 4
N_DEV = 8
LANES = 128
VMEM_LIMIT_BYTES = 56 * 2 ** 20
COPY_CHUNK_BYTES = 512 * 2 ** 10
COPY_ROW_ALIGN = 16

MESH = pl.DeviceIdType.MESH
ANY = pl.BlockSpec(memory_space=pl.ANY)
VMEM_WHOLE = pl.BlockSpec(memory_space=pltpu.VMEM)
HBM_SPEC = pl.BlockSpec(memory_space=pltpu.HBM)
SEM_SPEC = pl.BlockSpec(memory_space=pltpu.SEMAPHORE)
SIDE_EFFECT = pltpu.SideEffectType.DATAFLOW_SIDE_EFFECTING

NN = (((1,), (0,)), ((), ()))
NT = (((1,), (1,)), ((), ()))
TN = (((0,), (0,)), ((), ()))


def _tile(n, target, mult):
    best = None
    for t in range(mult, min(n, target) + 1, mult):
        if n % t == 0:
            best = t
    return n if best is None else best


def _params(n_grid_axes):
    return pltpu.CompilerParams(dimension_semantics=("arbitrary",) * n_grid_axes, vmem_limit_bytes=VMEM_LIMIT_BYTES)


def _rsqrt_mean_sq(x):
    return lax.rsqrt(jnp.mean(x * x, axis=-1, keepdims=True) + EPS)


def _sum_all(x):
    return jnp.sum(jnp.sum(x, axis=1, keepdims=True), axis=0, keepdims=True)


def _sigmoid(x):
    return 0.5 * jnp.tanh(0.5 * x) + 0.5


def _store_epilogue(accs, ex_refs, out_refs, i, j):
    for a, o in zip(accs, out_refs):
        o[...] = a.astype(o.dtype)


def _fused_matmul(name, grid, operands, pairs, acc_shapes, extras, outs, epilogue):
    nk = grid[2]
    n_op, n_ex, n_out, n_acc = len(operands), len(extras), len(outs), len(acc_shapes)
    in_place = nk > 1 and epilogue is _store_epilogue and all(o[1] == jnp.float32 for o in outs)

    def body(*refs):
        op_refs = refs[:n_op]
        ex_refs = refs[n_op:n_op + n_ex]
        out_refs = refs[n_op + n_ex:n_op + n_ex + n_out]
        acc_refs = out_refs if in_place else refs[n_op + n_ex + n_out:]
        i, j, k = pl.program_id(0), pl.program_id(1), pl.program_id(2)
        loaded = {}

        def operand(n):
            if n not in loaded:
                loaded[n] = op_refs[n][...].astype(MXU_DTYPE)
            return loaded[n]

        def product(ia, ib, dims):
            return lax.dot_general(operand(ia), operand(ib), dims, preferred_element_type=jnp.float32)

        def products():
            parts = [None] * n_acc
            for ia, ib, iacc, dims in pairs:
                d = product(ia, ib, dims)
                parts[iacc] = d if parts[iacc] is None else parts[iacc] + d
            return parts

        if nk == 1:
            epilogue(products(), ex_refs, out_refs, i, j)
            return

        @pl.when(k == 0)
        def _():
            for a in range(n_acc):
                acc_refs[a][...] = jnp.zeros_like(acc_refs[a])

        for a, part in enumerate(products()):
            acc_refs[a][...] += part

        if not in_place:
            @pl.when(k == nk - 1)
            def _():
                epilogue([acc_refs[a][...] for a in range(n_acc)], ex_refs, out_refs, i, j)

    arrays = [o[0] for o in operands] + [e[0] for e in extras]
    in_specs = [pl.BlockSpec(o[1], o[2]) for o in operands]
    in_specs += [ANY if e[1] is None else pl.BlockSpec(e[1], e[2]) for e in extras]
    return pl.pallas_call(
        body,
        name=name,
        grid=grid,
        in_specs=in_specs,
        out_specs=[pl.BlockSpec(o[2], o[3]) for o in outs],
        out_shape=[jax.ShapeDtypeStruct(o[0], o[1]) for o in outs],
        scratch_shapes=[pltpu.VMEM(s, jnp.float32) for s in acc_shapes] if nk > 1 and not in_place else [],
        compiler_params=_params(3),
    )(*arrays)


def _matmul_nn(name, a, b, out_dtype, tm=1024, tn=1024):
    M, K = a.shape
    N = b.shape[1]
    tm, tn = _tile(M, tm, 16), _tile(N, tn, LANES)
    return _fused_matmul(
        name, (M // tm, N // tn, 1),
        [(a, (tm, K), lambda i, j, k: (i, 0)), (b, (K, tn), lambda i, j, k: (0, j))],
        [(0, 1, 0, NN)], [(tm, tn)], [],
        [((M, N), out_dtype, (tm, tn), lambda i, j, k: (i, j))], _store_epilogue)[0]


def _matmul_nn_cols(name, a, b3, out_dtype, tm=1024):
    M, K = a.shape
    J, _, Nc = b3.shape
    tm = _tile(M, tm, 16)
    return _fused_matmul(
        name, (M // tm, J, 1),
        [(a, (tm, K), lambda i, j, k: (i, 0)), (b3, (None, K, Nc), lambda i, j, k: (j, 0, 0))],
        [(0, 1, 0, NN)], [(tm, Nc)], [],
        [((M, J * Nc), out_dtype, (tm, Nc), lambda i, j, k: (i, j))], _store_epilogue)[0]


def _order_after(after):
    return [] if after is None else [(after, None, None)]


def _matmul_nt(name, a_list, b_list, out_dtype, tm=1024, tn=1024, after=None):
    M, K = a_list[0].shape
    N = b_list[0].shape[0]
    tm, tn = _tile(M, tm, 16), _tile(N, tn, LANES)
    n = len(a_list)
    operands = [(a, (tm, K), lambda i, j, k: (i, 0)) for a in a_list]
    operands += [(b, (tn, K), lambda i, j, k: (j, 0)) for b in b_list]
    return _fused_matmul(
        name, (M // tm, N // tn, 1), operands,
        [(p, n + p, 0, NT) for p in range(n)], [(tm, tn)], _order_after(after),
        [((M, N), out_dtype, (tm, tn), lambda i, j, k: (i, j))], _store_epilogue)[0]


def _matmul_nt_cols(name, a_list, b3_list, out_dtype, tm=1024, tn=512, after=None):
    M = a_list[0].shape[0]
    J, N, Kc = b3_list[0].shape
    tm, tn = _tile(M, tm, 16), _tile(N, tn, LANES)
    n = len(a_list)
    operands = [(a, (tm, Kc), lambda i, j, k: (i, k)) for a in a_list]
    operands += [(b, (None, tn, Kc), lambda i, j, k: (k, j, 0)) for b in b3_list]
    return _fused_matmul(
        name, (M // tm, N // tn, J), operands,
        [(p, n + p, 0, NT) for p in range(n)], [(tm, tn)], _order_after(after),
        [((M, N), out_dtype, (tm, tn), lambda i, j, k: (i, j))], _store_epilogue)[0]


def _matmul_tn(name, a, b_list, out_dtype, tka=1024, tn=1024, tkm=4096):
    M, Ka = a.shape
    N = b_list[0].shape[1]
    tka, tn, tkm = _tile(Ka, tka, LANES), _tile(N, tn, LANES), _tile(M, tkm, 16)
    n = len(b_list)
    operands = [(a, (tkm, tka), lambda i, j, k: (k, i))]
    operands += [(b, (tkm, tn), lambda i, j, k: (k, j)) for b in b_list]
    return _fused_matmul(
        name, (Ka // tka, N // tn, M // tkm), operands,
        [(0, 1 + p, p, TN) for p in range(n)], [(tka, tn)] * n, [],
        [((Ka, N), out_dtype, (tka, tn), lambda i, j, k: (i, j))] * n, _store_epilogue)


def _matmul_tn_cols(name, a, b_list, n_shards, out_dtype, tka=512, tkm=4096, after=None):
    M, Ka = a.shape
    Nc = b_list[0].shape[1] // n_shards
    tka, tkm = _tile(Ka, tka, LANES), _tile(M, tkm, 16)
    n = len(b_list)
    operands = [(a, (tkm, tka), lambda i, j, k: (k, i))]
    operands += [(b, (tkm, Nc), lambda i, j, k: (k, j)) for b in b_list]
    return _fused_matmul(
        name, (Ka // tka, n_shards, M // tkm), operands,
        [(0, 1 + p, p, TN) for p in range(n)], [(tka, Nc)] * n, _order_after(after),
        [((n_shards, Ka, Nc), out_dtype, (None, tka, Nc), lambda i, j, k: (j, i, 0))] * n, _store_epilogue)


def _rmsnorm(name, x, g_row, out_dtype, tm=512, after=None):
    S, D = x.shape
    tm = _tile(S, tm, 16)

    def body(x_ref, g_ref, *rest):
        o_ref = rest[-1]
        xv = x_ref[...]
        o_ref[...] = (xv * _rsqrt_mean_sq(xv) * g_ref[...]).astype(o_ref.dtype)

    ordered = [] if after is None else [after]
    return pl.pallas_call(
        body, name=name, grid=(S // tm,),
        in_specs=[pl.BlockSpec((tm, D), lambda i: (i, 0)), pl.BlockSpec((1, D), lambda i: (0, 0))] + [ANY] * len(ordered),
        out_specs=pl.BlockSpec((tm, D), lambda i: (i, 0)),
        out_shape=jax.ShapeDtypeStruct((S, D), out_dtype),
        compiler_params=_params(1),
    )(x, g_row, *ordered)


def _rmsnorm_bwd_values(dh, xin, g):
    r = _rsqrt_mean_sq(xin)
    xhat = xin * r
    dhg = dh * g
    dx = r * (dhg - xhat * jnp.mean(dhg * xhat, axis=-1, keepdims=True))
    return dx, jnp.sum(dh * xhat, axis=0, keepdims=True)


def _rmsnorm_bwd(name, dh, xin, g_row, dres, low_dtype, tm=256):
    S, D = xin.shape
    tm = _tile(S, tm, 16)
    has_dx = dres is not None
    has_low = has_dx and low_dtype is not None

    def body(*refs):
        dh_ref, x_ref, g_ref = refs[:3]
        rest = refs[3:]
        dx, dg = _rmsnorm_bwd_values(dh_ref[...], x_ref[...], g_ref[...])
        if has_dx:
            dres_ref, rest = rest[0], rest[1:]
            dx = dx + dres_ref[...]
            rest[0][...] = dx
            if has_low:
                rest[1][...] = dx.astype(low_dtype)
        dg_ref = rest[-1]

        @pl.when(pl.program_id(0) == 0)
        def _():
            dg_ref[...] = jnp.zeros_like(dg_ref)

        dg_ref[...] += dg

    row = pl.BlockSpec((tm, D), lambda i: (i, 0))
    vec = pl.BlockSpec((1, D), lambda i: (0, 0))
    ins, in_specs = [dh, xin, g_row], [row, row, vec]
    out_shape, out_specs = [], []
    if has_dx:
        ins.append(dres)
        in_specs.append(row)
        out_shape.append(jax.ShapeDtypeStruct((S, D), jnp.float32))
        out_specs.append(row)
        if has_low:
            out_shape.append(jax.ShapeDtypeStruct((S, D), low_dtype))
            out_specs.append(row)
    out_shape.append(jax.ShapeDtypeStruct((1, D), jnp.float32))
    out_specs.append(vec)
    return pl.pallas_call(
        body, name=name, grid=(S // tm,), in_specs=in_specs, out_specs=out_specs, out_shape=out_shape,
        compiler_params=_params(1),
    )(*ins)


def _chunk_mask(blk):
    t = lax.broadcasted_iota(jnp.int32, (blk, blk), 0)
    s = lax.broadcasted_iota(jnp.int32, (blk, blk), 1)
    return (s // CHUNK) <= (t // CHUNK)


def _pool_inputs(a, halo, row0, tm, g, pg):
    cols = slice(g * pg, (g + 1) * pg)
    w = POOL_WINDOWS[g]
    s = jnp.concatenate([halo[:, cols], a[:, cols]], axis=0)
    span = 1
    while span < w:
        s = s + pltpu.roll(s, span, axis=0)
        span *= 2
    t = lax.broadcasted_iota(jnp.int32, (tm, 1), 0) + row0
    cnt = jnp.minimum(t + 1, w).astype(jnp.float32)
    return s[POOL_HALO:] / cnt - a[:, cols]


def _mixer_specs(S, tm, DP, DS):
    per_halo = tm // POOL_HALO
    n_halo = S // POOL_HALO
    a_spec = pl.BlockSpec((tm, DP), lambda i: (i, 0))
    prev_spec = pl.BlockSpec((POOL_HALO, DP), lambda i: (jnp.maximum(i * per_halo - 1, 0), 0))
    next_spec = pl.BlockSpec((POOL_HALO, DP), lambda i: (jnp.minimum((i + 1) * per_halo, n_halo - 1), 0))
    u_spec = pl.BlockSpec((tm, DS), lambda i: (i, 1))
    v_spec = pl.BlockSpec((tm, DS), lambda i: (i, 2))
    return a_spec, prev_spec, next_spec, u_spec, v_spec


def _whole(arr):
    nd = arr.ndim
    return pl.BlockSpec(arr.shape, lambda i: (0,) * nd)


def _mixer_fwd(proj, pw, ps_row, gs_row, ws, b_full, tm=512):
    S = proj.shape[0]
    G, PG, _ = pw.shape
    DP = G * PG
    H, BLK, _ = ws.shape
    DS = gs_row.shape[1]
    HD = DS // H
    assert DP == DS and proj.shape[1] == DP + 2 * DS
    tm = _tile(S, tm, BLK)

    def body(a_ref, prev_ref, u_ref, v_ref, pw_ref, ps_ref, gs_ref, ws_ref, b_ref, z_ref):
        i = pl.program_id(0)
        a = a_ref[...]
        halo = jnp.where(i > 0, prev_ref[...], 0.0)
        for g in range(G):
            cols = slice(g * PG, (g + 1) * PG)
            p = _pool_inputs(a, halo, i * tm, tm, g, PG)
            y = jnp.dot(p.astype(MXU_DTYPE), pw_ref[g], preferred_element_type=jnp.float32) * ps_ref[:, cols]
            z_ref[:, cols] = y.astype(z_ref.dtype)
        v = v_ref[...]
        vn = (v * _rsqrt_mean_sq(v) * gs_ref[...]).astype(MXU_DTYPE)
        u = u_ref[...]
        mask = _chunk_mask(BLK)
        for h in range(H):
            wm = jnp.where(mask, ws_ref[h], 0.0).astype(MXU_DTYPE)
            bias = b_ref[h]
            cols = slice(h * HD, (h + 1) * HD)
            for n in range(tm // BLK):
                rows = slice(n * BLK, (n + 1) * BLK)
                mixed = jnp.dot(wm, vn[rows, cols], preferred_element_type=jnp.float32) + bias
                z_ref[rows, DP + h * HD:DP + (h + 1) * HD] = (u[rows, cols] * mixed).astype(z_ref.dtype)

    a_spec, prev_spec, _, u_spec, v_spec = _mixer_specs(S, tm, DP, DS)
    return pl.pallas_call(
        body, name="mixer_fwd", grid=(S // tm,),
        in_specs=[a_spec, prev_spec, u_spec, v_spec, _whole(pw), _whole(ps_row), _whole(gs_row), _whole(ws), _whole(b_full)],
        out_specs=pl.BlockSpec((tm, DP + DS), lambda i: (i, 0)),
        out_shape=jax.ShapeDtypeStruct((S, DP + DS), MXU_DTYPE),
        compiler_params=_params(1),
    )(proj, proj, proj, proj, pw, ps_row, gs_row, ws, b_full)


def _mixer_bwd(proj, dz, pw, ps_row, gs_row, ws, b_full, tm=256):
    S = proj.shape[0]
    G, PG, _ = pw.shape
    DP = G * PG
    H, BLK, _ = ws.shape
    DS = gs_row.shape[1]
    HD = DS // H
    tm = _tile(S, tm, BLK)
    nb = S // tm
    ext = tm + POOL_HALO

    def body(a_ref, prev_ref, u_ref, v_ref, dzp_ref, dzp_next_ref, dzs_ref, pw_ref, ps_ref, gs_ref, ws_ref, b_ref,
             dproj_ref, dpw_ref, dps_ref, dgs_ref, dws_ref, db_ref, dvn_ref):
        i = pl.program_id(0)

        @pl.when(i == 0)
        def _():
            for r in (dpw_ref, dps_ref, dgs_ref, dws_ref, db_ref):
                r[...] = jnp.zeros_like(r)

        a = a_ref[...]
        halo = jnp.where(i > 0, prev_ref[...], 0.0)
        dyp = dzp_ref[...]
        dyp_next = jnp.where(i < nb - 1, dzp_next_ref[...], 0.0)
        dq_ext = jnp.concatenate([dyp, dyp_next], axis=0) * ps_ref[...]
        t_ext = lax.broadcasted_iota(jnp.int32, (ext, 1), 0) + i * tm
        for g in range(G):
            cols = slice(g * PG, (g + 1) * PG)
            w = POOL_WINDOWS[g]
            pb = _pool_inputs(a, halo, i * tm, tm, g, PG).astype(MXU_DTYPE)
            y_pre = jnp.dot(pb, pw_ref[g], preferred_element_type=jnp.float32)
            dps_ref[:, cols] += jnp.sum(dyp[:, cols] * y_pre, axis=0, keepdims=True)
            dqb = dq_ext[:, cols].astype(MXU_DTYPE)
            dpw_ref[g] += lax.dot_general(pb, dqb[:tm], TN, preferred_element_type=jnp.float32)
            dp = lax.dot_general(dqb, pw_ref[g], NT, preferred_element_type=jnp.float32)
            f = dp / jnp.minimum(t_ext + 1, w).astype(jnp.float32)
            span = 1
            while span < w:
                f = f + pltpu.roll(f, ext - span, axis=0)
                span *= 2
            dproj_ref[:, cols] = (f[:tm] - dp[:tm]).astype(dproj_ref.dtype)

        v = v_ref[...]
        rv = _rsqrt_mean_sq(v)
        vhat = v * rv
        gs = gs_ref[...]
        vnb = (vhat * gs).astype(MXU_DTYPE)
        u = u_ref[...]
        dys = dzs_ref[...]
        mask = _chunk_mask(BLK)
        for h in range(H):
            wm = jnp.where(mask, ws_ref[h], 0.0).astype(MXU_DTYPE)
            bias = b_ref[h]
            cols = slice(h * HD, (h + 1) * HD)
            dm_sum = jnp.zeros((BLK, HD), jnp.float32)
            dws_sum = jnp.zeros((BLK, BLK), jnp.float32)
            for n in range(tm // BLK):
                rows = slice(n * BLK, (n + 1) * BLK)
                vblk = vnb[rows, cols]
                mixed = jnp.dot(wm, vblk, preferred_element_type=jnp.float32) + bias
                dy = dys[rows, cols]
                dproj_ref[rows, DP + h * HD:DP + (h + 1) * HD] = (dy * mixed).astype(dproj_ref.dtype)
                dm = dy * u[rows, cols]
                dm_sum = dm_sum + dm
                dmb = dm.astype(MXU_DTYPE)
                dws_sum = dws_sum + lax.dot_general(dmb, vblk, NT, preferred_element_type=jnp.float32)
                dvn_ref[rows, cols] = lax.dot_general(wm, dmb, TN, preferred_element_type=jnp.float32)
            dws_ref[h] += jnp.where(mask, dws_sum, 0.0)
            db_ref[h:h + 1, :] += jnp.sum(dm_sum.T, axis=0, keepdims=True)
        dvn = dvn_ref[...]
        dvg = dvn * gs
        dv = rv * (dvg - vhat * jnp.mean(dvg * vhat, axis=-1, keepdims=True))
        dgs_ref[...] += jnp.sum(dvn * vhat, axis=0, keepdims=True)
        dproj_ref[:, DP + DS:] = dv.astype(dproj_ref.dtype)

    a_spec, prev_spec, next_spec, u_spec, v_spec = _mixer_specs(S, tm, DP, DS)
    dzp_spec = pl.BlockSpec((tm, DP), lambda i: (i, 0))
    dzs_spec = pl.BlockSpec((tm, DS), lambda i: (i, 1))
    small = [pw, ps_row, gs_row, ws, b_full]
    out_shape = [
        jax.ShapeDtypeStruct((S, DP + 2 * DS), MXU_DTYPE),
        jax.ShapeDtypeStruct(pw.shape, jnp.float32),
        jax.ShapeDtypeStruct(ps_row.shape, jnp.float32),
        jax.ShapeDtypeStruct(gs_row.shape, jnp.float32),
        jax.ShapeDtypeStruct(ws.shape, jnp.float32),
        jax.ShapeDtypeStruct((H, BLK), jnp.float32),
    ]
    out_specs = [pl.BlockSpec((tm, DP + 2 * DS), lambda i: (i, 0))] + [_whole(s) for s in out_shape[1:]]
    return pl.pallas_call(
        body, name="mixer_bwd", grid=(nb,),
        in_specs=[a_spec, prev_spec, u_spec, v_spec, dzp_spec, next_spec, dzs_spec] + [_whole(s) for s in small],
        out_specs=out_specs, out_shape=out_shape,
        scratch_shapes=[pltpu.VMEM((tm, DS), jnp.float32)],
        compiler_params=_params(1),
    )(proj, proj, proj, proj, dz, dz, dz, *small)


def _softmax_rows(q, k, scale):
    s = lax.dot_general(q, k, NT, preferred_element_type=jnp.float32) * scale
    e = jnp.exp(s - jnp.max(s, axis=-1, keepdims=True))
    return e / jnp.sum(e, axis=-1, keepdims=True)


def _attn_fwd(q, k, v, tm=512):
    S, D = q.shape
    M = k.shape[0]
    HD = D // N_XATTN_HEADS
    scale = HD ** -0.5
    tm = _tile(S, tm, 16)

    def body(q_ref, k_ref, v_ref, o_ref):
        for h in range(N_XATTN_HEADS):
            cols = slice(h * HD, (h + 1) * HD)
            p = _softmax_rows(q_ref[:, cols], k_ref[:, cols], scale)
            o = jnp.dot(p.astype(MXU_DTYPE), v_ref[:, cols], preferred_element_type=jnp.float32)
            o_ref[:, cols] = o.astype(o_ref.dtype)

    row = pl.BlockSpec((tm, D), lambda i: (i, 0))
    mem = pl.BlockSpec((M, D), lambda i: (0, 0))
    return pl.pallas_call(
        body, name="attn_fwd", grid=(S // tm,), in_specs=[row, mem, mem], out_specs=row,
        out_shape=jax.ShapeDtypeStruct((S, D), MXU_DTYPE), compiler_params=_params(1),
    )(q, k, v)


def _attn_bwd(q, k, v, do, tm=512):
    S, D = q.shape
    M = k.shape[0]
    HD = D // N_XATTN_HEADS
    scale = HD ** -0.5
    tm = _tile(S, tm, 16)

    def body(q_ref, k_ref, v_ref, do_ref, dq_ref, dk_ref, dv_ref):
        @pl.when(pl.program_id(0) == 0)
        def _():
            dk_ref[...] = jnp.zeros_like(dk_ref)
            dv_ref[...] = jnp.zeros_like(dv_ref)

        for h in range(N_XATTN_HEADS):
            cols = slice(h * HD, (h + 1) * HD)
            qh, kh, vh, doh = q_ref[:, cols], k_ref[:, cols], v_ref[:, cols], do_ref[:, cols]
            p = _softmax_rows(qh, kh, scale)
            dp = lax.dot_general(doh, vh, NT, preferred_element_type=jnp.float32)
            dv_ref[:, cols] += lax.dot_general(p.astype(MXU_DTYPE), doh, TN, preferred_element_type=jnp.float32)
            ds = (p * (dp - jnp.sum(dp * p, axis=-1, keepdims=True)) * scale).astype(MXU_DTYPE)
            dq_ref[:, cols] = jnp.dot(ds, kh, preferred_element_type=jnp.float32).astype(dq_ref.dtype)
            dk_ref[:, cols] += lax.dot_general(ds, qh, TN, preferred_element_type=jnp.float32)

    row = pl.BlockSpec((tm, D), lambda i: (i, 0))
    mem = pl.BlockSpec((M, D), lambda i: (0, 0))
    return pl.pallas_call(
        body, name="attn_bwd", grid=(S // tm,), in_specs=[row, mem, mem, row], out_specs=[row, mem, mem],
        out_shape=[jax.ShapeDtypeStruct((S, D), MXU_DTYPE), jax.ShapeDtypeStruct((M, D), jnp.float32),
                   jax.ShapeDtypeStruct((M, D), jnp.float32)],
        compiler_params=_params(1),
    )(q, k, v, do)


def _proj_residual_norm(name, a, w2d, resid, g_row, tm=256, after=None):
    M, K = a.shape
    N = w2d.shape[1]
    tm = _tile(M, tm, 16)

    def epilogue(accs, ex_refs, out_refs, i, j):
        x_new = ex_refs[0][...] + accs[0]
        out_refs[0][...] = x_new
        out_refs[1][...] = (x_new * _rsqrt_mean_sq(x_new) * ex_refs[1][...]).astype(MXU_DTYPE)

    row = lambda i, j, k: (i, 0)
    return _fused_matmul(
        name, (M // tm, 1, 1),
        [(a, (tm, K), row), (w2d, (K, N), lambda i, j, k: (0, 0))],
        [(0, 1, 0, NN)], [(tm, N)],
        [(resid, (tm, N), row), (g_row, (1, N), lambda i, j, k: (0, 0))] + _order_after(after),
        [((M, N), jnp.float32, (tm, N), row), ((M, N), MXU_DTYPE, (tm, N), row)], epilogue)


def _ffn_gate_up(h, wg3, wu3, tm=512):
    M, K = h.shape
    J, _, Nc = wg3.shape
    tm = _tile(M, tm, 16)

    def epilogue(accs, ex_refs, out_refs, i, j):
        gate, up = accs
        sg = _sigmoid(gate)
        silu = gate * sg
        out_refs[0][...] = (silu * up).astype(MXU_DTYPE)
        out_refs[1][...] = (up * (sg * (1.0 + gate * (1.0 - sg)))).astype(MXU_DTYPE)
        out_refs[2][...] = silu.astype(MXU_DTYPE)

    col = lambda i, j, k: (i, j)
    wspec = lambda i, j, k: (j, 0, 0)
    return _fused_matmul(
        "ffn_gate_up", (M // tm, J, 1),
        [(h, (tm, K), lambda i, j, k: (i, 0)), (wg3, (None, K, Nc), wspec), (wu3, (None, K, Nc), wspec)],
        [(0, 1, 0, NN), (0, 2, 1, NN)], [(tm, Nc)] * 2, [],
        [((M, J * Nc), MXU_DTYPE, (tm, Nc), col)] * 3, epilogue)


def _ffn_down(act, wd2d, x2, tm=512, tn=1024):
    M, F = act.shape
    D = wd2d.shape[1]
    tm, tn = _tile(M, tm, 16), _tile(D, tn, LANES)

    def epilogue(accs, ex_refs, out_refs, i, j):
        out_refs[0][...] = ex_refs[0][...] + accs[0]

    tile = lambda i, j, k: (i, j)
    return _fused_matmul(
        "ffn_down", (M // tm, D // tn, 1),
        [(act, (tm, F), lambda i, j, k: (i, 0)), (wd2d, (F, tn), lambda i, j, k: (0, j))],
        [(0, 1, 0, NN)], [(tm, tn)], [(x2, (tm, tn), tile)],
        [((M, D), jnp.float32, (tm, tn), tile)], epilogue)[0]


def _final_norm_loss(x3, target, g_row, tm=256):
    S, D = x3.shape
    tm = _tile(S, tm, 16)

    def body(x_ref, t_ref, g_ref, dx_ref, dxl_ref, loss_ref, dg_ref):
        x3v = x_ref[...]
        g = g_ref[...]
        r = _rsqrt_mean_sq(x3v)
        xhat = x3v * r
        diff = xhat * g - t_ref[...]
        dy = diff / D
        dyg = dy * g
        dx = r * (dyg - xhat * jnp.mean(dyg * xhat, axis=-1, keepdims=True))
        dx_ref[...] = dx
        dxl_ref[...] = dx.astype(MXU_DTYPE)

        @pl.when(pl.program_id(0) == 0)
        def _():
            loss_ref[...] = jnp.zeros_like(loss_ref)
            dg_ref[...] = jnp.zeros_like(dg_ref)

        loss_ref[...] += 0.5 * _sum_all(jnp.mean(diff * diff, axis=-1, keepdims=True))
        dg_ref[...] += jnp.sum(dy * xhat, axis=0, keepdims=True)

    row = pl.BlockSpec((tm, D), lambda i: (i, 0))
    vec = pl.BlockSpec((1, D), lambda i: (0, 0))
    return pl.pallas_call(
        body, name="final_norm_loss", grid=(S // tm,), in_specs=[row, row, vec],
        out_specs=[row, row, pl.BlockSpec((1, 1), lambda i: (0, 0)), vec],
        out_shape=[jax.ShapeDtypeStruct((S, D), jnp.float32), jax.ShapeDtypeStruct((S, D), MXU_DTYPE),
                   jax.ShapeDtypeStruct((1, 1), jnp.float32), jax.ShapeDtypeStruct((1, D), jnp.float32)],
        compiler_params=_params(1),
    )(x3, target, g_row)


def _ffn_down_bwd(dx3_low, wd2d, act_by_gate, act_by_up, n_shards, tm=512):
    M, D = dx3_low.shape
    F = wd2d.shape[0]
    tn = F // n_shards
    tm = _tile(M, tm, 16)

    def epilogue(accs, ex_refs, out_refs, i, j):
        dact = accs[0]
        out_refs[0][...] = (dact * ex_refs[0][...].astype(jnp.float32)).astype(MXU_DTYPE)
        out_refs[1][...] = (dact * ex_refs[1][...].astype(jnp.float32)).astype(MXU_DTYPE)

    col = lambda i, j, k: (i, j)
    return _fused_matmul(
        "ffn_down_bwd", (M // tm, n_shards, 1),
        [(dx3_low, (tm, D), lambda i, j, k: (i, 0)), (wd2d, (tn, D), lambda i, j, k: (j, 0))],
        [(0, 1, 0, NT)], [(tm, tn)],
        [(act_by_gate, (tm, tn), col), (act_by_up, (tm, tn), col)],
        [((M, F), MXU_DTYPE, (tm, tn), col)] * 2, epilogue)


def _matmul_nt_norm_bwd(name, a, b, xin, g_row, dres, low_dtype, tm=256, after=None):
    M = a.shape[0]
    sharded = b.ndim == 3
    N, Kc = b.shape[-2:]
    J = b.shape[0] if sharded else 1
    tm = _tile(M, tm, 16)
    has_low = low_dtype is not None

    def epilogue(accs, ex_refs, out_refs, i, j):
        dx, dg = _rmsnorm_bwd_values(accs[0], ex_refs[0][...], ex_refs[1][...])
        dx = dx + ex_refs[2][...]
        out_refs[0][...] = dx
        if has_low:
            out_refs[1][...] = dx.astype(low_dtype)
        dg_ref = out_refs[-1]

        @pl.when(i == 0)
        def _():
            dg_ref[...] = jnp.zeros_like(dg_ref)

        dg_ref[...] += dg

    row = lambda i, j, k: (i, 0)
    const = lambda i, j, k: (0, 0)
    b_operand = (b, (None, N, Kc), lambda i, j, k: (k, 0, 0)) if sharded else (b, (N, Kc), const)
    outs = [((M, N), jnp.float32, (tm, N), row)]
    if has_low:
        outs.append(((M, N), low_dtype, (tm, N), row))
    outs.append(((1, N), jnp.float32, (1, N), const))
    return _fused_matmul(
        name, (M // tm, 1, J), [(a, (tm, Kc), lambda i, j, k: (i, k)), b_operand],
        [(0, 1, 0, NT)], [(tm, N)],
        [(xin, (tm, N), row), (g_row, (1, N), const), (dres, (tm, N), row)] + _order_after(after),
        outs, epilogue)


def _row_tile(R, C, target_bytes=2 ** 21):
    return _tile(R, max(16, target_bytes // (4 * C)), 16)


def _prefetch_call(body, name, grid, pos, in_specs, out_specs, out_shape, ins, after=None):
    ordered = [] if after is None else [after]
    grid_spec = pltpu.PrefetchScalarGridSpec(
        num_scalar_prefetch=1, grid=grid, in_specs=list(in_specs) + [ANY] * len(ordered), out_specs=out_specs)
    return pl.pallas_call(body, name=name, grid_spec=grid_spec, out_shape=out_shape,
                          compiler_params=_params(len(grid)))(pos, *ins, *ordered)


def _runs(arrays):
    runs = []
    for i, a in enumerate(arrays):
        if runs and arrays[runs[-1][-1]].shape == a.shape:
            runs[-1].append(i)
        else:
            runs.append([i])
    return runs


def _per_run(fn, names, *array_lists):
    out = [None] * len(names)
    for run in _runs(array_lists[0]):
        results = fn("_".join(names[i] for i in run), *[[arrays[i] for i in run] for arrays in array_lists])
        for i, r in zip(run, results):
            out[i] = r
    return out


def _cast_into_slabs(name, xs, dtype, pos, after=None):
    n = len(xs)
    R, C = xs[0].shape
    tr = _row_tile(R, C * n, 2 ** 22)

    def body(pos_ref, *refs):
        for t in range(n):
            refs[len(refs) - n + t][...] = refs[t][...].astype(dtype)

    return _prefetch_call(
        body, "cast_" + name, (R // tr,), pos, [pl.BlockSpec((tr, C), lambda i, pos: (i, 0))] * n,
        [pl.BlockSpec((None, tr, C), lambda i, pos: (pos[0], i, 0))] * n,
        [jax.ShapeDtypeStruct((N_CHIPS, R, C), dtype)] * n, xs, after)


def _chip_sum(name, partials, gots, dtype, pos):
    n = len(partials)
    J, R, C = partials[0].shape
    Rh = R // 2
    tr = _row_tile(Rh, C * n, 2 ** 22)

    def body(pos_ref, *refs):
        for t in range(n):
            refs[2 * n + t][...] = (refs[t][...].astype(jnp.float32) + refs[n + t][...].astype(jnp.float32)).astype(dtype)

    blk = pl.BlockSpec((None, tr, C), lambda j, i, pos: (j, i, 0))
    return _prefetch_call(
        body, "chip_sum_" + name, (J, Rh // tr), pos,
        [pl.BlockSpec((None, None, tr, C), lambda j, i, pos: (j, pos[1], i, 0))] * n + [blk] * n, [blk] * n,
        [jax.ShapeDtypeStruct((J, Rh, C), dtype)] * n, [p.reshape(J, 2, Rh, C) for p in partials] + list(gots))


def _owner_sum(name, chip_sums, others, pos):
    n = len(chip_sums)
    J, Rh, C = chip_sums[0].shape
    n_other = others[0].shape[0]
    tr = _row_tile(Rh, C * (n_other + 1) * n, 2 ** 22)

    def body(pos_ref, *refs):
        for t in range(n):
            acc = refs[t][...].astype(jnp.float32)
            for k in range(n_other):
                acc = acc + refs[n + t][k].astype(jnp.float32)
            refs[2 * n + t][...] = acc

    return _prefetch_call(
        body, "owner_sum_" + name, (Rh // tr,), pos,
        [pl.BlockSpec((None, tr, C), lambda i, pos: (pos[0], i, 0))] * n
        + [pl.BlockSpec((n_other, tr, C), lambda i, pos: (0, i, 0))] * n,
        [pl.BlockSpec((None, tr, C), lambda i, pos: (pos[1], i, 0))] * n,
        [jax.ShapeDtypeStruct((2, Rh, C), jnp.float32)] * n, list(chip_sums) + list(others))


def _adamw_values(w, g, m, v):
    m = ADAM_B1 * m + (1.0 - ADAM_B1) * g
    v = ADAM_B2 * v + (1.0 - ADAM_B2) * (g * g)
    m_hat = m / (1.0 - ADAM_B1 ** ADAM_STEP)
    v_hat = v / (1.0 - ADAM_B2 ** ADAM_STEP)
    delta = -ADAM_LR * (m_hat / (jnp.sqrt(v_hat) + ADAM_EPS) + ADAM_WD * w)
    return delta, m, v


def _adamw(name, ws, gs, ms, vs):
    n = len(ws)
    R, C = ws[0].shape
    tr = _row_tile(R, C * n, 2 ** 20)

    def body(*refs):
        w_refs, g_refs, m_refs, v_refs = (refs[q * n:(q + 1) * n] for q in range(4))
        outs = refs[4 * n:]
        for t in range(n):
            g = g_refs[t][...]
            outs[4 * t][...] = g
            outs[4 * t + 1][...], outs[4 * t + 2][...], outs[4 * t + 3][...] = _adamw_values(
                w_refs[t][...], g, m_refs[t][...], v_refs[t][...])

    blk = pl.BlockSpec((tr, C), lambda i: (i, 0))
    res = pl.pallas_call(
        body, name="adamw_" + name, grid=(R // tr,), in_specs=[blk] * (4 * n), out_specs=[blk] * (4 * n),
        out_shape=[jax.ShapeDtypeStruct((R, C), jnp.float32)] * (4 * n), compiler_params=_params(1),
    )(*ws, *gs, *ms, *vs)
    return [tuple(res[4 * t:4 * t + 4]) for t in range(n)]


def _adamw_small(g_packed, ws, ms, vs, row_counts):
    n = len(ws)

    def body(*refs):
        g_ref = refs[0]
        w_refs, m_refs, v_refs = refs[1:1 + n], refs[1 + n:1 + 2 * n], refs[1 + 2 * n:1 + 3 * n]
        outs = refs[1 + 3 * n:]
        off = 0
        for t in range(n):
            g = g_ref[off:off + row_counts[t], :]
            off += row_counts[t]
            delta, m_new, v_new = _adamw_values(w_refs[t][...], g, m_refs[t][...], v_refs[t][...])
            outs[4 * t][...] = g
            outs[4 * t + 1][...] = delta
            outs[4 * t + 2][...] = m_new
            outs[4 * t + 3][...] = v_new

    out_shape = []
    for t in range(n):
        out_shape += [jax.ShapeDtypeStruct((row_counts[t], LANES), jnp.float32)] * 4
    return pl.pallas_call(
        body, name="adamw_small", in_specs=[VMEM_WHOLE] * (1 + 3 * n), out_specs=[VMEM_WHOLE] * (4 * n),
        out_shape=out_shape, compiler_params=pltpu.CompilerParams(vmem_limit_bytes=VMEM_LIMIT_BYTES),
    )(g_packed, *ws, *ms, *vs)


def _position():
    x, y, c = lax.axis_index("x"), lax.axis_index("y"), lax.axis_index("c")
    other_chips = [(1 - x, y), (x, 1 - y), (1 - x, 1 - y)]
    return x, y, c, other_chips


def _half(rows, which):
    return pl.ds(which * (rows // 2), rows // 2)


def _chunk_index(shape, dtype):
    rows = shape[-2]
    total = math.prod(shape) * jnp.dtype(dtype).itemsize
    n = max(1, min(rows // COPY_ROW_ALIGN, total // COPY_CHUNK_BYTES))
    step = -(-rows // n)
    step = -(-step // COPY_ROW_ALIGN) * COPY_ROW_ALIGN
    lead = (slice(None),) * (len(shape) - 2)
    return [lead + (pl.ds(s, min(step, rows - s)),) for s in range(0, rows, step)]


def _start_in_chunks(make, src, dst, waited_here=True):
    for idx in _chunk_index(src.shape, src.dtype):
        make(src.at[idx], dst.at[idx]).start()
    return make(src, dst) if waited_here else None


def _remote_copier(send_sem, recv_sem, to):
    return lambda src, dst: pltpu.make_async_remote_copy(
        src_ref=src, dst_ref=dst, send_sem=send_sem, recv_sem=recv_sem, device_id=to, device_id_type=MESH)


def _slab_half(ref, chip, which):
    return ref.at[chip, _half(ref.shape[1], which)]


def _slab_quarter(ref, chip, which, q):
    rows = ref.shape[1] // 4
    return ref.at[chip, pl.ds(which * 2 * rows + q * rows, rows)]


def _gather_step(name, arrived=(), ring=(), direct=(), after=None):
    groups = [g[2] for g in arrived] + list(direct) + list(ring)
    per_tensor = [2] * len(arrived) + [3] * len(direct) + [2] * len(ring)
    flat = [a for g in groups for a in g]
    T, G, A = len(flat), len(groups), len(arrived)
    ordered = [] if after is None else [after]
    sems_in = [sem for g in arrived for sem in g[:2]]

    def body(*refs):
        ins = refs[:T]
        first_hop = refs[T:T + 2 * A]
        out_sems = refs[T + 2 * A + len(ordered):T + 2 * A + len(ordered) + 2 * G]
        token = refs[-1]
        x, y, c, other_chips = _position()
        mine = 2 * x + y
        x_nbr, y_nbr = (1 - x, y, c), (x, 1 - y, c)
        from_x, from_y = 2 * (1 - x) + y, 2 * x + (1 - y)
        t0 = 0
        for gi, group in enumerate(groups):
            send, recv = out_sems[2 * gi], out_sems[2 * gi + 1]
            for t in range(len(group)):
                ref = ins[t0 + t]
                own = _slab_half(ref, mine, c)
                if gi < A:
                    send1, recv1 = first_hop[2 * gi], first_hop[2 * gi + 1]
                    for k, (peer, chip) in enumerate(((x_nbr, from_x), (y_nbr, from_y))):
                        cp = _remote_copier(send1.at[2 * t + k], recv1.at[2 * t + k], peer)(own, _slab_half(ref, chip, c))
                        cp.wait_send()
                        cp.wait_recv()
                    for k, (peer, chip) in enumerate(((y_nbr, from_x), (x_nbr, from_y))):
                        piece = _slab_quarter(ref, chip, c, k)
                        _start_in_chunks(_remote_copier(send.at[2 * t + k], recv.at[2 * t + k], peer), piece, piece,
                                         waited_here=False)
                elif gi < A + len(direct):
                    for k, (cx, cy) in enumerate(other_chips):
                        _start_in_chunks(_remote_copier(send.at[3 * t + k], recv.at[3 * t + k], (cx, cy, c)), own, own,
                                         waited_here=False)
                else:
                    for k, peer in enumerate((x_nbr, y_nbr)):
                        _start_in_chunks(_remote_copier(send.at[2 * t + k], recv.at[2 * t + k], peer), own, own,
                                         waited_here=False)
            t0 += len(group)
        token[...] = jnp.zeros_like(token)

    sem_shapes = []
    for group, n in zip(groups, per_tensor):
        sem_shapes += [pltpu.SemaphoreType.DMA((n * len(group),))] * 2
    res = pl.pallas_call(
        body, name=name, in_specs=[HBM_SPEC] * T + [SEM_SPEC] * (2 * A) + [ANY] * len(ordered),
        out_specs=[SEM_SPEC] * (2 * G) + [HBM_SPEC] * T + [VMEM_WHOLE],
        out_shape=sem_shapes + [pltpu.HBM(a.shape, a.dtype) for a in flat] + [jax.ShapeDtypeStruct((8, LANES), jnp.float32)],
        input_output_aliases={t: 2 * G + t for t in range(T)},
        compiler_params=pltpu.CompilerParams(has_side_effects=SIDE_EFFECT),
    )(*[pltpu.with_memory_space_constraint(a, pltpu.HBM) for a in flat], *sems_in, *ordered)
    flights, t0 = [], 2 * G
    for gi, group in enumerate(groups):
        flights.append((res[2 * gi], res[2 * gi + 1], res[t0:t0 + len(group)]))
        t0 += len(group)
    return flights[:A], flights[A:A + len(direct)], flights[A + len(direct):], res[-1]


def _relayed_wait(name, flight, after):
    send_sems, recv_sems, slabs = flight
    T = len(slabs)

    def body(*refs):
        ins = refs[:T]
        send, recv = refs[T:T + 2]
        x, y, c, _ = _position()
        x_nbr, y_nbr = (1 - x, y, c), (x, 1 - y, c)
        from_x, from_y, diagonal = 2 * (1 - x) + y, 2 * x + (1 - y), 2 * (1 - x) + (1 - y)
        for t in range(T):
            for k, (peer, chip) in enumerate(((y_nbr, from_x), (x_nbr, from_y))):
                cp = _remote_copier(send.at[2 * t + k], recv.at[2 * t + k], peer)(
                    _slab_quarter(ins[t], chip, c, k), _slab_quarter(ins[t], diagonal, c, k))
                cp.wait_send()
                cp.wait_recv()

    return pl.pallas_call(
        body, name=name, in_specs=[HBM_SPEC] * T + [SEM_SPEC, SEM_SPEC, ANY], out_specs=[HBM_SPEC] * T,
        out_shape=[pltpu.HBM(a.shape, a.dtype) for a in slabs], input_output_aliases={t: t for t in range(T)},
        compiler_params=pltpu.CompilerParams(has_side_effects=SIDE_EFFECT),
    )(*slabs, send_sems, recv_sems, after)


def _gather_wait(name, flight, after):
    send_sems, recv_sems, slabs = flight
    T = len(slabs)

    def body(*refs):
        ins = refs[:T]
        send, recv = refs[T:T + 2]
        x, y, c, other_chips = _position()
        mine = 2 * x + y
        for t in range(T):
            own = _slab_half(ins[t], mine, c)
            for k, (cx, cy) in enumerate(other_chips):
                landed = _slab_half(ins[t], 2 * cx + cy, c)
                cp = _remote_copier(send.at[3 * t + k], recv.at[3 * t + k], (cx, cy, c))(own, landed)
                cp.wait_send()
                cp.wait_recv()

    return pl.pallas_call(
        body, name=name, in_specs=[HBM_SPEC] * T + [SEM_SPEC, SEM_SPEC, ANY], out_specs=[HBM_SPEC] * T,
        out_shape=[pltpu.HBM(s.shape, s.dtype) for s in slabs], input_output_aliases={t: t for t in range(T)},
        compiler_params=pltpu.CompilerParams(has_side_effects=SIDE_EFFECT),
    )(*slabs, send_sems, recv_sems, after)


def _share_with_sibling(name, slabs):
    T = len(slabs)

    def body(*refs):
        outs = refs[T:2 * T]
        send_sems, recv_sems = refs[2 * T:]
        x, y, c, other_chips = _position()
        sibling = (x, y, 1 - c)
        started = []
        for t in range(T):
            for k, (cx, cy) in enumerate(other_chips):
                landed = _slab_half(outs[t], 2 * cx + cy, c)
                started.append(_start_in_chunks(_remote_copier(send_sems.at[t, k], recv_sems.at[t, k], sibling), landed, landed))
        for t in range(T):
            for k, (cx, cy) in enumerate(other_chips):
                arriving = _slab_half(outs[t], 2 * cx + cy, 1 - c)
                _remote_copier(send_sems.at[t, k], recv_sems.at[t, k], sibling)(arriving, arriving).wait_recv()
        for cp in started:
            cp.wait_send()

    return pl.pallas_call(
        body, name=name, in_specs=[ANY] * T, out_specs=[ANY] * T,
        out_shape=[jax.ShapeDtypeStruct(s.shape, s.dtype) for s in slabs],
        input_output_aliases={t: t for t in range(T)},
        scratch_shapes=[pltpu.SemaphoreType.DMA((T, 3)), pltpu.SemaphoreType.DMA((T, 3))],
    )(*slabs)


def _swap_halves(name, grads):
    T = len(grads)

    def body(*refs):
        ins, got = refs[:T], refs[T:2 * T]
        send_sems, recv_sems = refs[2 * T:]
        x, y, c, _ = _position()
        remote = []
        for t in range(T):
            rows = ins[t].shape[1]
            remote.append(_start_in_chunks(
                _remote_copier(send_sems.at[t], recv_sems.at[t], (x, y, 1 - c)), ins[t].at[:, _half(rows, 1 - c)], got[t]))
        for cp in remote:
            cp.wait()

    return pl.pallas_call(
        body, name=name, in_specs=[ANY] * T, out_specs=[ANY] * T,
        out_shape=[jax.ShapeDtypeStruct((g.shape[0], g.shape[1] // 2, g.shape[2]), g.dtype) for g in grads],
        scratch_shapes=[pltpu.SemaphoreType.DMA((T,)), pltpu.SemaphoreType.DMA((T,))],
    )(*grads)


def _swap_start(name, grads):
    T = len(grads)
    lands = [lax.empty((g.shape[0], g.shape[1] // 2, g.shape[2]), g.dtype) for g in grads]

    def body(*refs):
        ins, got = refs[:T], refs[T:2 * T]
        send_sems, recv_sems = refs[2 * T:2 * T + 2]
        token = refs[-1]
        x, y, c, _ = _position()
        for t in range(T):
            _start_in_chunks(_remote_copier(send_sems.at[t], recv_sems.at[t], (x, y, 1 - c)),
                             ins[t].at[:, _half(ins[t].shape[1], 1 - c)], got[t], waited_here=False)
        token[...] = jnp.zeros_like(token)

    arrays = list(grads) + lands
    res = pl.pallas_call(
        body, name=name, in_specs=[HBM_SPEC] * (2 * T),
        out_specs=[SEM_SPEC, SEM_SPEC] + [HBM_SPEC] * (2 * T) + [VMEM_WHOLE],
        out_shape=[pltpu.SemaphoreType.DMA((T,)), pltpu.SemaphoreType.DMA((T,))]
        + [pltpu.HBM(a.shape, a.dtype) for a in arrays] + [jax.ShapeDtypeStruct((8, LANES), jnp.float32)],
        input_output_aliases={i: 2 + i for i in range(2 * T)},
        compiler_params=pltpu.CompilerParams(has_side_effects=SIDE_EFFECT),
    )(*[pltpu.with_memory_space_constraint(a, pltpu.HBM) for a in arrays])
    return res[0], res[1], res[2:2 + T], res[2 + T:2 + 2 * T], res[-1]


def _swap_wait(name, flight, after):
    send_sems, recv_sems, grads, lands, _ = flight
    T = len(grads)

    def body(*refs):
        ins, got = refs[:T], refs[T:2 * T]
        send, recv = refs[2 * T:2 * T + 2]
        x, y, c, _ = _position()
        for t in range(T):
            cp = _remote_copier(send.at[t], recv.at[t], (x, y, 1 - c))(ins[t].at[:, _half(ins[t].shape[1], 1 - c)], got[t])
            cp.wait_send()
            cp.wait_recv()

    arrays = list(grads) + list(lands)
    res = pl.pallas_call(
        body, name=name, in_specs=[HBM_SPEC] * (2 * T) + [SEM_SPEC, SEM_SPEC, ANY], out_specs=[HBM_SPEC] * (2 * T),
        out_shape=[pltpu.HBM(a.shape, a.dtype) for a in arrays], input_output_aliases={i: i for i in range(2 * T)},
        compiler_params=pltpu.CompilerParams(has_side_effects=SIDE_EFFECT),
    )(*arrays, send_sems, recv_sems, after)
    return res[:T], res[T:]


def _scatter_start(name, chip_sums):
    T = len(chip_sums)
    lands = [lax.empty((3,) + s.shape[1:], s.dtype) for s in chip_sums]

    def body(*refs):
        srcs, dsts = refs[:T], refs[T:2 * T]
        send_sems, recv_sems = refs[2 * T:2 * T + 2]
        token = refs[-1]
        x, y, c, other_chips = _position()
        for t in range(T):
            for k, (cx, cy) in enumerate(other_chips):
                _start_in_chunks(_remote_copier(send_sems.at[3 * t + k], recv_sems.at[3 * t + k], (cx, cy, c)),
                                 srcs[t].at[2 * cx + cy], dsts[t].at[k], waited_here=False)
        token[...] = jnp.zeros_like(token)

    hbm = [pltpu.HBM(a.shape, a.dtype) for a in list(chip_sums) + lands]
    res = pl.pallas_call(
        body, name=name, in_specs=[HBM_SPEC] * (2 * T),
        out_specs=[SEM_SPEC, SEM_SPEC] + [HBM_SPEC] * (2 * T) + [VMEM_WHOLE],
        out_shape=[pltpu.SemaphoreType.DMA((3 * T,)), pltpu.SemaphoreType.DMA((3 * T,))] + hbm
        + [jax.ShapeDtypeStruct((8, LANES), jnp.float32)],
        input_output_aliases={i: 2 + i for i in range(2 * T)},
        compiler_params=pltpu.CompilerParams(has_side_effects=SIDE_EFFECT),
    )(*[pltpu.with_memory_space_constraint(a, pltpu.HBM) for a in list(chip_sums) + lands])
    return res[0], res[1], res[2:2 + T], res[2 + T:2 + 2 * T], res[-1]


def _scatter_wait(name, send_sems, recv_sems, chip_sums, lands, after):
    T = len(chip_sums)

    def body(*refs):
        srcs, dsts = refs[:T], refs[T:2 * T]
        send, recv = refs[2 * T:2 * T + 2]
        x, y, c, other_chips = _position()
        for t in range(T):
            for k, (cx, cy) in enumerate(other_chips):
                cp = _remote_copier(send.at[3 * t + k], recv.at[3 * t + k], (cx, cy, c))(srcs[t].at[2 * cx + cy], dsts[t].at[k])
                cp.wait_send()
                cp.wait_recv()

    arrays = list(chip_sums) + list(lands)
    res = pl.pallas_call(
        body, name=name, in_specs=[HBM_SPEC] * (2 * T) + [SEM_SPEC, SEM_SPEC, ANY],
        out_specs=[HBM_SPEC] * (2 * T), out_shape=[pltpu.HBM(a.shape, a.dtype) for a in arrays],
        input_output_aliases={i: i for i in range(2 * T)},
        compiler_params=pltpu.CompilerParams(has_side_effects=SIDE_EFFECT),
    )(*arrays, send_sems, recv_sems, after)
    return res[:T], res[T:]


def _as_shards(partial):
    return [p.reshape((N_CHIPS, -1, p.shape[-1])) for p in partial]


def _reduce_begin(tag, names, partial, pos):
    parts = _as_shards(partial)
    received = _swap_halves("swap_grad_halves_" + tag, parts)
    return _reduce_scatter_start(tag, names, parts, received, pos)


def _reduce_scatter_start(tag, names, parts, received, pos):
    chip_sums = _per_run(lambda name, p, r: _chip_sum(name, p, r, WIRE_DTYPE, pos), names, list(parts), list(received))
    return _scatter_start("scatter_start_" + tag, chip_sums)


def _reduce_end(tag, names, flight, after, pos):
    send_sems, recv_sems, chip_sums, lands, _ = flight
    chip_sums, others = _scatter_wait("scatter_wait_" + tag, send_sems, recv_sems, chip_sums, lands, after)
    reduced = _per_run(lambda name, cs, ot: _owner_sum(name, cs, ot, pos), names, list(chip_sums), list(others))
    return dict(zip(names, reduced))


def _join_start(name, halves):
    T = len(halves)

    def body(*refs):
        ins = refs[:T]
        send_sems, recv_sems = refs[T:T + 2]
        token = refs[-1]
        x, y, c, _ = _position()
        for t in range(T):
            _start_in_chunks(_remote_copier(send_sems.at[t], recv_sems.at[t], (x, y, 1 - c)),
                             ins[t].at[c], ins[t].at[c], waited_here=False)
        token[...] = jnp.zeros_like(token)

    res = pl.pallas_call(
        body, name=name, in_specs=[HBM_SPEC] * T,
        out_specs=[SEM_SPEC, SEM_SPEC] + [HBM_SPEC] * T + [VMEM_WHOLE],
        out_shape=[pltpu.SemaphoreType.DMA((T,)), pltpu.SemaphoreType.DMA((T,))]
        + [pltpu.HBM(h.shape, h.dtype) for h in halves] + [jax.ShapeDtypeStruct((8, LANES), jnp.float32)],
        input_output_aliases={t: 2 + t for t in range(T)},
        compiler_params=pltpu.CompilerParams(has_side_effects=SIDE_EFFECT),
    )(*[pltpu.with_memory_space_constraint(h, pltpu.HBM) for h in halves])
    return res[0], res[1], res[2:2 + T], res[-1]


def _join_wait(name, flight, after):
    send_sems, recv_sems, halves, _ = flight
    T = len(halves)

    def body(*refs):
        ins = refs[:T]
        send, recv = refs[T:T + 2]
        x, y, c, _ = _position()
        for t in range(T):
            cp = _remote_copier(send.at[t], recv.at[t], (x, y, 1 - c))(ins[t].at[c], ins[t].at[1 - c])
            cp.wait_send()
            cp.wait_recv()

    return pl.pallas_call(
        body, name=name, in_specs=[HBM_SPEC] * T + [SEM_SPEC, SEM_SPEC, ANY], out_specs=[HBM_SPEC] * T,
        out_shape=[pltpu.HBM(h.shape, h.dtype) for h in halves], input_output_aliases={t: t for t in range(T)},
        compiler_params=pltpu.CompilerParams(has_side_effects=SIDE_EFFECT),
    )(*halves, send_sems, recv_sems, after)


def _peers():
    x, y, c, _ = _position()
    flips = [(fx, fy, fc) for fx in (0, 1) for fy in (0, 1) for fc in (0, 1)][1:]
    return [(jnp.bitwise_xor(x, fx), jnp.bitwise_xor(y, fy), jnp.bitwise_xor(c, fc)) for fx, fy, fc in flips]


def _device_index(px, py, pc):
    return 4 * px + 2 * py + pc


def _exchange_start(name, slots):
    def body(slots_ref, send_sems, recv_sems, thru, token):
        x, y, c, _ = _position()
        own = slots_ref.at[_device_index(x, y, c)]
        for k, peer in enumerate(_peers()):
            _remote_copier(send_sems.at[k], recv_sems.at[k], peer)(own, own).start()
        token[...] = jnp.zeros_like(token)

    res = pl.pallas_call(
        body, name=name, in_specs=[HBM_SPEC], out_specs=[SEM_SPEC, SEM_SPEC, HBM_SPEC, VMEM_WHOLE],
        out_shape=[pltpu.SemaphoreType.DMA((N_DEV - 1,)), pltpu.SemaphoreType.DMA((N_DEV - 1,)),
                   pltpu.HBM(slots.shape, slots.dtype), jax.ShapeDtypeStruct((8, LANES), jnp.float32)],
        input_output_aliases={0: 2}, compiler_params=pltpu.CompilerParams(has_side_effects=SIDE_EFFECT),
    )(pltpu.with_memory_space_constraint(slots, pltpu.HBM))
    return res[0], res[1], res[2], res[3]


def _exchange_wait(name, flight, after):
    send_sems, recv_sems, slots, _ = flight

    def body(slots_ref, send, recv, after_ref, out_ref):
        x, y, c, _ = _position()
        own = slots_ref.at[_device_index(x, y, c)]
        for k, peer in enumerate(_peers()):
            cp = _remote_copier(send.at[k], recv.at[k], peer)(own, slots_ref.at[_device_index(*peer)])
            cp.wait_send()
            cp.wait_recv()

    return pl.pallas_call(
        body, name=name, in_specs=[HBM_SPEC, SEM_SPEC, SEM_SPEC, ANY], out_specs=HBM_SPEC,
        out_shape=pltpu.HBM(slots.shape, slots.dtype), input_output_aliases={0: 0},
        compiler_params=pltpu.CompilerParams(has_side_effects=SIDE_EFFECT),
    )(slots, send_sems, recv_sems, after)


def _sum_slots(name, slots):
    n, rows, lanes = slots.shape
    tr = _row_tile(rows, lanes * n)

    def body(s_ref, o_ref):
        acc = s_ref[0]
        for d in range(1, n):
            acc = acc + s_ref[d]
        o_ref[...] = acc

    return pl.pallas_call(
        body, name=name, grid=(rows // tr,), in_specs=[pl.BlockSpec((n, tr, lanes), lambda i: (0, i, 0))],
        out_specs=pl.BlockSpec((tr, lanes), lambda i: (i, 0)), out_shape=jax.ShapeDtypeStruct((rows, lanes), jnp.float32),
        compiler_params=_params(1))(slots)


BIG = ("w_in", "pool_w", "w_out", "w_q", "w_k", "w_v", "w_o", "w_gate", "w_up", "w_down")
SMALL = ("norm_mix_g", "pool_scale", "sgu_norm_g", "w_spatial", "b_spatial", "norm_xattn_g", "norm_mem_g",
         "norm_ffn_g", "final_norm_g")
GATHER_ORDER = (("w_in", "pool_w"), ("w_out",), ("w_q", "w_k", "w_v", "w_o"), ("w_gate", "w_up"), ("w_down",))
WEIGHTS = ("norm_mix_g", "w_in", "pool_w", "pool_scale", "sgu_norm_g", "w_spatial", "b_spatial", "w_out",
           "norm_xattn_g", "norm_mem_g", "w_q", "w_k", "w_v", "w_o", "norm_ffn_g", "w_gate", "w_up", "w_down",
           "final_norm_g")


def _as2d(a):
    return a.reshape(-1, a.shape[-1])


def _as_lanes(a):
    return a.reshape(-1, LANES)


def kernel(x, mem, norm_mix_g, w_in, pool_w, pool_scale, sgu_norm_g, w_spatial, b_spatial, w_out, norm_xattn_g, norm_mem_g, w_q, w_k, w_v, w_o, norm_ffn_g, w_gate, w_up, w_down, final_norm_g, loss_target, m_norm_mix_g, m_w_in, m_pool_w, m_pool_scale, m_sgu_norm_g, m_w_spatial, m_b_spatial, m_w_out, m_norm_xattn_g, m_norm_mem_g, m_w_q, m_w_k, m_w_v, m_w_o, m_norm_ffn_g, m_w_gate, m_w_up, m_w_down, m_final_norm_g, v_norm_mix_g, v_w_in, v_pool_w, v_pool_scale, v_sgu_norm_g, v_w_spatial, v_b_spatial, v_w_out, v_norm_xattn_g, v_norm_mem_g, v_w_q, v_w_k, v_w_v, v_w_o, v_norm_ffn_g, v_w_gate, v_w_up, v_w_down, v_final_norm_g):
    args = dict(locals())
    weights = {n: args[n] for n in WEIGHTS}
    moments_m = {n: args["m_" + n] for n in WEIGHTS}
    moments_v = {n: args["v_" + n] for n in WEIGHTS}
    low = MXU_DTYPE

    xs = x[0]
    mems = mem[0]
    target = loss_target[0]
    D = xs.shape[1]
    G, PG = pool_w.shape[1], pool_w.shape[3]

    pos = jnp.stack([2 * lax.axis_index("x") + lax.axis_index("y"), lax.axis_index("c")]).astype(jnp.int32)
    def cast(names, after=None):
        slabs = _per_run(lambda name, xs: _cast_into_slabs(name, xs, WIRE_DTYPE, pos, after), names,
                         [_as2d(weights[n]) for n in names])
        return dict(zip(names, slabs))

    def land(gi, flight, after, relayed=True):
        wait = _relayed_wait if relayed else _gather_wait
        landed = wait("gather_wait_%d" % gi, flight, after)
        return dict(zip(GATHER_ORDER[gi], _share_with_sibling("gather_share_%d" % gi, landed)))

    def rows(a):
        return a.reshape(-1, a.shape[-1])

    g_mix, g_xattn, g_mem, g_ffn = norm_mix_g, norm_xattn_g, norm_mem_g, norm_ffn_g
    g_final = final_norm_g.reshape(1, D)
    ws = w_spatial[0]
    b_full = jnp.broadcast_to(b_spatial[0][:, :, None], ws.shape[:2] + (sgu_norm_g.shape[1] // ws.shape[0],))

    first = cast(GATHER_ORDER[0])
    _, (sent_0,), _, started = _gather_step("gather_step_0", direct=[[first[n] for n in GATHER_ORDER[0]]])
    rest = cast([n for group in GATHER_ORDER[1:] for n in group], after=started)
    slabs = [[rest[n] for n in group] for group in GATHER_ORDER[1:]]
    _, (sent_1,), (hop_2,), started = _gather_step("gather_step_1", direct=slabs[:1], ring=slabs[1:2])
    h1 = _rmsnorm("norm_mix", xs, g_mix, low, after=started)
    mn = _rmsnorm("norm_mem", mems, g_mem, low)
    got = land(0, sent_0, rest["w_down"], relayed=False)
    wg_in = got["w_in"]
    pw = got["pool_w"].reshape(N_CHIPS, G, PG // N_CHIPS, PG).transpose(1, 0, 2, 3).reshape(G, PG, PG)
    proj = _matmul_nn_cols("proj_in", h1, wg_in, jnp.float32)
    z = _mixer_fwd(proj, pw, pool_scale, sgu_norm_g, ws, b_full)
    (relay_2,), _, (hop_3,), started = _gather_step("gather_step_2", arrived=[hop_2], ring=slabs[2:3], after=z)
    full = {"w_out": rows(land(1, sent_1, started, relayed=False)["w_out"])}
    x1, h2 = _proj_residual_norm("mix_out", z, full["w_out"], xs, g_xattn)
    full.update({n: rows(a) for n, a in land(2, relay_2, h2).items()})
    q = _matmul_nn("xattn_q", h2, full["w_q"], low)
    k = _matmul_nn("xattn_k", mn, full["w_k"], low)
    vv = _matmul_nn("xattn_v", mn, full["w_v"], low)
    o = _attn_fwd(q, k, vv)
    (relay_3,), (sent_4,), _, started = _gather_step("gather_step_3", arrived=[hop_3], direct=slabs[3:4], after=o)
    x2, h3 = _proj_residual_norm("xattn_out", o, full["w_o"], x1, g_ffn, after=started)
    got = land(3, relay_3, h3)
    wg_gate, wg_up = got["w_gate"], got["w_up"]
    act, act_by_gate, act_by_up = _ffn_gate_up(h3, wg_gate, wg_up)
    full["w_down"] = rows(land(4, sent_4, act, relayed=False)["w_down"])
    x3 = _ffn_down(act, full["w_down"], x2)
    dx3, dx3_low, loss_part, d_final_g = _final_norm_loss(x3, target, g_final)

    dgate, dup = _ffn_down_bwd(dx3_low, full["w_down"], act_by_gate, act_by_up, N_CHIPS)
    (d_w_down,) = _matmul_tn("grad_w_down", act, [dx3_low], WIRE_DTYPE)
    (d_w_gate,) = _matmul_tn_cols("grad_w_gate", h3, [dgate], N_CHIPS, WIRE_DTYPE)
    (d_w_up,) = _matmul_tn_cols("grad_w_up", h3, [dup], N_CHIPS, WIRE_DTYPE)
    ffn_names = ("w_gate", "w_up", "w_down")
    ffn_swap = _swap_start("swap_start_ffn", _as_shards([d_w_gate, d_w_up, d_w_down]))
    dh3 = _matmul_nt_cols("ffn_dh", [dgate, dup], [wg_gate, wg_up], jnp.float32, tn=1024, after=ffn_swap[-1])
    ffn_flight = _reduce_scatter_start("ffn", ffn_names, *_swap_wait("swap_wait_ffn", ffn_swap, dh3), pos)
    dx2, dx2_low, d_ffn_g = _rmsnorm_bwd("norm_ffn_bwd", dh3, x2, g_ffn, dx3, low)

    do = _matmul_nt("xattn_do", [dx2_low], [full["w_o"]], low, after=ffn_flight[-1])
    (d_w_o,) = _matmul_tn("grad_w_o", o, [dx2_low], WIRE_DTYPE)
    dq, dk, dvv = _attn_bwd(q, k, vv, do)
    (d_w_q,) = _matmul_tn("grad_w_q", h2, [dq], WIRE_DTYPE)
    dx1, dx1_low, d_xattn_g = _matmul_nt_norm_bwd("xattn_dh_norm_bwd", dq, full["w_q"], x1, g_xattn, dx2, low)
    d_w_k, d_w_v = _matmul_tn("grad_w_kv", mn, [dk, dvv], WIRE_DTYPE)
    dmn = _matmul_nt("xattn_dmem", [dk, dvv], [full["w_k"], full["w_v"]], jnp.float32)
    (d_mem_g,) = _rmsnorm_bwd("norm_mem_bwd", dmn, mems, g_mem, None, None)

    (d_w_out,) = _matmul_tn("grad_w_out", z, [dx1_low], WIRE_DTYPE)
    attn_names = ("w_q", "w_k", "w_v", "w_o", "w_out")
    attn_swap = _swap_start("swap_start_attn", _as_shards([d_w_q, d_w_k, d_w_v, d_w_o, d_w_out]))
    dz = _matmul_nt("mix_dz", [dx1_low], [full["w_out"]], jnp.float32, after=attn_swap[-1])
    attn_flight = _reduce_scatter_start("attn", attn_names, *_swap_wait("swap_wait_attn", attn_swap, dz), pos)
    dproj, d_pw, d_ps, d_gs, d_ws, d_b = _mixer_bwd(proj, dz, pw, pool_scale, sgu_norm_g, ws, b_full)
    (d_w_in,) = _matmul_tn_cols("grad_w_in", h1, [dproj], N_CHIPS, WIRE_DTYPE, tka=1024, after=attn_flight[-1])
    d_pool_w = d_pw.reshape(G, N_CHIPS, PG // N_CHIPS, PG).transpose(1, 0, 2, 3).reshape(N_CHIPS, PG, PG).astype(WIRE_DTYPE)
    mix_names = ("w_in", "pool_w")
    mix_flight = _reduce_begin("mix", mix_names, [d_w_in, d_pool_w], pos)
    grad_x, d_mix_g = _matmul_nt_norm_bwd("mix_dh_norm_bwd", dproj, wg_in, xs, g_mix, dx1, None, tm=512, after=mix_flight[-1])

    small_parts = {
        "norm_mix_g": d_mix_g, "pool_scale": d_ps, "sgu_norm_g": d_gs, "w_spatial": d_ws, "b_spatial": d_b,
        "norm_xattn_g": d_xattn_g, "norm_mem_g": d_mem_g, "norm_ffn_g": d_ffn_g, "final_norm_g": d_final_g,
    }
    row_counts = [small_parts[n].size // LANES for n in SMALL]
    packed = jnp.concatenate([_as_lanes(small_parts[n]) for n in SMALL], axis=0)
    device = _device_index(lax.axis_index("x"), lax.axis_index("y"), lax.axis_index("c"))
    slots = lax.dynamic_update_slice(jnp.zeros((N_DEV,) + packed.shape, jnp.float32), packed[None], (device, 0, 0))
    small_flight = _exchange_start("small_grads_start", slots)

    groups = (("ffn", ffn_names, ffn_flight), ("attn", attn_names, attn_flight), ("mix", mix_names, mix_flight))
    joins, after = {}, small_flight[-1]
    for tag, names, flight in groups:
        reduced = _reduce_end(tag, names, flight, after, pos)
        joins[tag] = _join_start("join_start_" + tag, [reduced[n] for n in names])
        after = joins[tag][-1]
    small_sum = _sum_slots("small_grads_sum", _exchange_wait("small_grads_wait", small_flight, after))

    out_grad, out_delta, out_m, out_v = {}, {}, {}, {}
    after = small_sum
    for tag, names, _ in groups:
        grads = [g.reshape(-1, g.shape[-1]) for g in _join_wait("join_wait_" + tag, joins[tag], after)]
        updates = _per_run(_adamw, names, [_as2d(weights[n]) for n in names], grads,
                           [_as2d(moments_m[n]) for n in names], [_as2d(moments_v[n]) for n in names])
        for n, update in zip(names, updates):
            shape = weights[n].shape
            out_grad[n], out_delta[n], out_m[n], out_v[n] = (a.reshape(shape) for a in update)
        after = updates[-1][1]
    small_out = _adamw_small(
        small_sum, [_as_lanes(weights[n]) for n in SMALL], [_as_lanes(moments_m[n]) for n in SMALL],
        [_as_lanes(moments_v[n]) for n in SMALL], row_counts)
    for t, n in enumerate(SMALL):
        shape = weights[n].shape
        out_grad[n], out_delta[n], out_m[n], out_v[n] = (a.reshape(shape) for a in small_out[4 * t:4 * t + 4])

    loss = lax.psum(loss_part[0, 0], ("x", "y", "c"))
    return (loss, grad_x[None], *[out_grad[n] for n in WEIGHTS], *[out_delta[n] for n in WEIGHTS],
            *[out_m[n] for n in WEIGHTS], *[out_v[n] for n in WEIGHTS])
```

```python
import math

import jax
import jax.numpy as jnp
from jax import lax
from jax.experimental import pallas as pl
from jax.experimental.pallas import tpu as pltpu

MXU_DTYPE = jnp.bfloat16
WIRE_DTYPE = jnp.bfloat16

EPS = 1e-6
CHUNK = 64
POOL_WINDOWS = (2, 4, 8, 16)
POOL_HALO = 16
N_XATTN_HEADS = 4
ADAM_LR = 0.001
ADAM_B1 = 0.9
ADAM_B2 = 0.999
ADAM_EPS = 1e-08
ADAM_WD = 0.01
ADAM_STEP = 10

N_CHIPS = 4
N_DEV = 8
LANES = 128
VMEM_LIMIT_BYTES = 56 * 2 ** 20
COPY_CHUNK_BYTES = 512 * 2 ** 10
COPY_ROW_ALIGN = 16

MESH = pl.DeviceIdType.MESH
ANY = pl.BlockSpec(memory_space=pl.ANY)
VMEM_WHOLE = pl.BlockSpec(memory_space=pltpu.VMEM)
HBM_SPEC = pl.BlockSpec(memory_space=pltpu.HBM)
SEM_SPEC = pl.BlockSpec(memory_space=pltpu.SEMAPHORE)
SIDE_EFFECT = pltpu.SideEffectType.DATAFLOW_SIDE_EFFECTING

NN = (((1,), (0,)), ((), ()))
NT = (((1,), (1,)), ((), ()))
TN = (((0,), (0,)), ((), ()))


def _tile(n, target, mult):
    best = None
    for t in range(mult, min(n, target) + 1, mult):
        if n % t == 0:
            best = t
    return n if best is None else best


def _params(n_grid_axes):
    return pltpu.CompilerParams(dimension_semantics=("arbitrary",) * n_grid_axes, vmem_limit_bytes=VMEM_LIMIT_BYTES)


def _rsqrt_mean_sq(x):
    return lax.rsqrt(jnp.mean(x * x, axis=-1, keepdims=True) + EPS)


def _sum_all(x):
    return jnp.sum(jnp.sum(x, axis=1, keepdims=True), axis=0, keepdims=True)


def _sigmoid(x):
    return 0.5 * jnp.tanh(0.5 * x) + 0.5


def _store_epilogue(accs, ex_refs, out_refs, i, j):
    for a, o in zip(accs, out_refs):
        o[...] = a.astype(o.dtype)


def _fused_matmul(name, grid, operands, pairs, acc_shapes, extras, outs, epilogue):
    nk = grid[2]
    n_op, n_ex, n_out, n_acc = len(operands), len(extras), len(outs), len(acc_shapes)
    in_place = nk > 1 and epilogue is _store_epilogue and all(o[1] == jnp.float32 for o in outs)

    def body(*refs):
        op_refs = refs[:n_op]
        ex_refs = refs[n_op:n_op + n_ex]
        out_refs = refs[n_op + n_ex:n_op + n_ex + n_out]
        acc_refs = out_refs if in_place else refs[n_op + n_ex + n_out:]
        i, j, k = pl.program_id(0), pl.program_id(1), pl.program_id(2)
        loaded = {}

        def operand(n):
            if n not in loaded:
                loaded[n] = op_refs[n][...].astype(MXU_DTYPE)
            return loaded[n]

        def product(ia, ib, dims):
            return lax.dot_general(operand(ia), operand(ib), dims, preferred_element_type=jnp.float32)

        def products():
            parts = [None] * n_acc
            for ia, ib, iacc, dims in pairs:
                d = product(ia, ib, dims)
                parts[iacc] = d if parts[iacc] is None else parts[iacc] + d
            return parts

        if nk == 1:
            epilogue(products(), ex_refs, out_refs, i, j)
            return

        @pl.when(k == 0)
        def _():
            for a in range(n_acc):
                acc_refs[a][...] = jnp.zeros_like(acc_refs[a])

        for a, part in enumerate(products()):
            acc_refs[a][...] += part

        if not in_place:
            @pl.when(k == nk - 1)
            def _():
                epilogue([acc_refs[a][...] for a in range(n_acc)], ex_refs, out_refs, i, j)

    arrays = [o[0] for o in operands] + [e[0] for e in extras]
    in_specs = [pl.BlockSpec(o[1], o[2]) for o in operands]
    in_specs += [ANY if e[1] is None else pl.BlockSpec(e[1], e[2]) for e in extras]
    return pl.pallas_call(
        body,
        name=name,
        grid=grid,
        in_specs=in_specs,
        out_specs=[pl.BlockSpec(o[2], o[3]) for o in outs],
        out_shape=[jax.ShapeDtypeStruct(o[0], o[1]) for o in outs],
        scratch_shapes=[pltpu.VMEM(s, jnp.float32) for s in acc_shapes] if nk > 1 and not in_place else [],
        compiler_params=_params(3),
    )(*arrays)


def _matmul_nn(name, a, b, out_dtype, tm=1024, tn=1024):
    M, K = a.shape
    N = b.shape[1]
    tm, tn = _tile(M, tm, 16), _tile(N, tn, LANES)
    return _fused_matmul(
        name, (M // tm, N // tn, 1),
        [(a, (tm, K), lambda i, j, k: (i, 0)), (b, (K, tn), lambda i, j, k: (0, j))],
        [(0, 1, 0, NN)], [(tm, tn)], [],
        [((M, N), out_dtype, (tm, tn), lambda i, j, k: (i, j))], _store_epilogue)[0]


def _matmul_nn_cols(name, a, b3, out_dtype, tm=1024):
    M, K = a.shape
    J, _, Nc = b3.shape
    tm = _tile(M, tm, 16)
    return _fused_matmul(
        name, (M // tm, J, 1),
        [(a, (tm, K), lambda i, j, k: (i, 0)), (b3, (None, K, Nc), lambda i, j, k: (j, 0, 0))],
        [(0, 1, 0, NN)], [(tm, Nc)], [],
        [((M, J * Nc), out_dtype, (tm, Nc), lambda i, j, k: (i, j))], _store_epilogue)[0]


def _order_after(after):
    return [] if after is None else [(after, None, None)]


def _matmul_nt(name, a_list, b_list, out_dtype, tm=1024, tn=1024, after=None):
    M, K = a_list[0].shape
    N = b_list[0].shape[0]
    tm, tn = _tile(M, tm, 16), _tile(N, tn, LANES)
    n = len(a_list)
    operands = [(a, (tm, K), lambda i, j, k: (i, 0)) for a in a_list]
    operands += [(b, (tn, K), lambda i, j, k: (j, 0)) for b in b_list]
    return _fused_matmul(
        name, (M // tm, N // tn, 1), operands,
        [(p, n + p, 0, NT) for p in range(n)], [(tm, tn)], _order_after(after),
        [((M, N), out_dtype, (tm, tn), lambda i, j, k: (i, j))], _store_epilogue)[0]


def _matmul_nt_cols(name, a_list, b3_list, out_dtype, tm=1024, tn=512, after=None):
    M = a_list[0].shape[0]
    J, N, Kc = b3_list[0].shape
    tm, tn = _tile(M, tm, 16), _tile(N, tn, LANES)
    n = len(a_list)
    operands = [(a, (tm, Kc), lambda i, j, k: (i, k)) for a in a_list]
    operands += [(b, (None, tn, Kc), lambda i, j, k: (k, j, 0)) for b in b3_list]
    return _fused_matmul(
        name, (M // tm, N // tn, J), operands,
        [(p, n + p, 0, NT) for p in range(n)], [(tm, tn)], _order_after(after),
        [((M, N), out_dtype, (tm, tn), lambda i, j, k: (i, j))], _store_epilogue)[0]


def _matmul_tn(name, a, b_list, out_dtype, tka=1024, tn=1024, tkm=4096):
    M, Ka = a.shape
    N = b_list[0].shape[1]
    tka, tn, tkm = _tile(Ka, tka, LANES), _tile(N, tn, LANES), _tile(M, tkm, 16)
    n = len(b_list)
    operands = [(a, (tkm, tka), lambda i, j, k: (k, i))]
    operands += [(b, (tkm, tn), lambda i, j, k: (k, j)) for b in b_list]
    return _fused_matmul(
        name, (Ka // tka, N // tn, M // tkm), operands,
        [(0, 1 + p, p, TN) for p in range(n)], [(tka, tn)] * n, [],
        [((Ka, N), out_dtype, (tka, tn), lambda i, j, k: (i, j))] * n, _store_epilogue)


def _matmul_tn_cols(name, a, b_list, n_shards, out_dtype, tka=512, tkm=4096, after=None):
    M, Ka = a.shape
    Nc = b_list[0].shape[1] // n_shards
    tka, tkm = _tile(Ka, tka, LANES), _tile(M, tkm, 16)
    n = len(b_list)
    operands = [(a, (tkm, tka), lambda i, j, k: (k, i))]
    operands += [(b, (tkm, Nc), lambda i, j, k: (k, j)) for b in b_list]
    return _fused_matmul(
        name, (Ka // tka, n_shards, M // tkm), operands,
        [(0, 1 + p, p, TN) for p in range(n)], [(tka, Nc)] * n, _order_after(after),
        [((n_shards, Ka, Nc), out_dtype, (None, tka, Nc), lambda i, j, k: (j, i, 0))] * n, _store_epilogue)


def _rmsnorm(name, x, g_row, out_dtype, tm=512, after=None):
    S, D = x.shape
    tm = _tile(S, tm, 16)

    def body(x_ref, g_ref, *rest):
        o_ref = rest[-1]
        xv = x_ref[...]
        o_ref[...] = (xv * _rsqrt_mean_sq(xv) * g_ref[...]).astype(o_ref.dtype)

    ordered = [] if after is None else [after]
    return pl.pallas_call(
        body, name=name, grid=(S // tm,),
        in_specs=[pl.BlockSpec((tm, D), lambda i: (i, 0)), pl.BlockSpec((1, D), lambda i: (0, 0))] + [ANY] * len(ordered),
        out_specs=pl.BlockSpec((tm, D), lambda i: (i, 0)),
        out_shape=jax.ShapeDtypeStruct((S, D), out_dtype),
        compiler_params=_params(1),
    )(x, g_row, *ordered)


def _rmsnorm_bwd_values(dh, xin, g):
    r = _rsqrt_mean_sq(xin)
    xhat = xin * r
    dhg = dh * g
    dx = r * (dhg - xhat * jnp.mean(dhg * xhat, axis=-1, keepdims=True))
    return dx, jnp.sum(dh * xhat, axis=0, keepdims=True)


def _rmsnorm_bwd(name, dh, xin, g_row, dres, low_dtype, tm=256):
    S, D = xin.shape
    tm = _tile(S, tm, 16)
    has_dx = dres is not None
    has_low = has_dx and low_dtype is not None

    def body(*refs):
        dh_ref, x_ref, g_ref = refs[:3]
        rest = refs[3:]
        dx, dg = _rmsnorm_bwd_values(dh_ref[...], x_ref[...], g_ref[...])
        if has_dx:
            dres_ref, rest = rest[0], rest[1:]
            dx = dx + dres_ref[...]
            rest[0][...] = dx
            if has_low:
                rest[1][...] = dx.astype(low_dtype)
        dg_ref = rest[-1]

        @pl.when(pl.program_id(0) == 0)
        def _():
            dg_ref[...] = jnp.zeros_like(dg_ref)

        dg_ref[...] += dg

    row = pl.BlockSpec((tm, D), lambda i: (i, 0))
    vec = pl.BlockSpec((1, D), lambda i: (0, 0))
    ins, in_specs = [dh, xin, g_row], [row, row, vec]
    out_shape, out_specs = [], []
    if has_dx:
        ins.append(dres)
        in_specs.append(row)
        out_shape.append(jax.ShapeDtypeStruct((S, D), jnp.float32))
        out_specs.append(row)
        if has_low:
            out_shape.append(jax.ShapeDtypeStruct((S, D), low_dtype))
            out_specs.append(row)
    out_shape.append(jax.ShapeDtypeStruct((1, D), jnp.float32))
    out_specs.append(vec)
    return pl.pallas_call(
        body, name=name, grid=(S // tm,), in_specs=in_specs, out_specs=out_specs, out_shape=out_shape,
        compiler_params=_params(1),
    )(*ins)


def _chunk_mask(blk):
    t = lax.broadcasted_iota(jnp.int32, (blk, blk), 0)
    s = lax.broadcasted_iota(jnp.int32, (blk, blk), 1)
    return (s // CHUNK) <= (t // CHUNK)


def _pool_inputs(a, halo, row0, tm, g, pg):
    cols = slice(g * pg, (g + 1) * pg)
    w = POOL_WINDOWS[g]
    s = jnp.concatenate([halo[:, cols], a[:, cols]], axis=0)
    span = 1
    while span < w:
        s = s + pltpu.roll(s, span, axis=0)
        span *= 2
    t = lax.broadcasted_iota(jnp.int32, (tm, 1), 0) + row0
    cnt = jnp.minimum(t + 1, w).astype(jnp.float32)
    return s[POOL_HALO:] / cnt - a[:, cols]


def _mixer_specs(S, tm, DP, DS):
    per_halo = tm // POOL_HALO
    n_halo = S // POOL_HALO
    a_spec = pl.BlockSpec((tm, DP), lambda i: (i, 0))
    prev_spec = pl.BlockSpec((POOL_HALO, DP), lambda i: (jnp.maximum(i * per_halo - 1, 0), 0))
    next_spec = pl.BlockSpec((POOL_HALO, DP), lambda i: (jnp.minimum((i + 1) * per_halo, n_halo - 1), 0))
    u_spec = pl.BlockSpec((tm, DS), lambda i: (i, 1))
    v_spec = pl.BlockSpec((tm, DS), lambda i: (i, 2))
    return a_spec, prev_spec, next_spec, u_spec, v_spec


def _whole(arr):
    nd = arr.ndim
    return pl.BlockSpec(arr.shape, lambda i: (0,) * nd)


def _mixer_fwd(proj, pw, ps_row, gs_row, ws, b_full, tm=512):
    S = proj.shape[0]
    G, PG, _ = pw.shape
    DP = G * PG
    H, BLK, _ = ws.shape
    DS = gs_row.shape[1]
    HD = DS // H
    assert DP == DS and proj.shape[1] == DP + 2 * DS
    tm = _tile(S, tm, BLK)

    def body(a_ref, prev_ref, u_ref, v_ref, pw_ref, ps_ref, gs_ref, ws_ref, b_ref, z_ref):
        i = pl.program_id(0)
        a = a_ref[...]
        halo = jnp.where(i > 0, prev_ref[...], 0.0)
        for g in range(G):
            cols = slice(g * PG, (g + 1) * PG)
            p = _pool_inputs(a, halo, i * tm, tm, g, PG)
            y = jnp.dot(p.astype(MXU_DTYPE), pw_ref[g], preferred_element_type=jnp.float32) * ps_ref[:, cols]
            z_ref[:, cols] = y.astype(z_ref.dtype)
        v = v_ref[...]
        vn = (v * _rsqrt_mean_sq(v) * gs_ref[...]).astype(MXU_DTYPE)
        u = u_ref[...]
        mask = _chunk_mask(BLK)
        for h in range(H):
            wm = jnp.where(mask, ws_ref[h], 0.0).astype(MXU_DTYPE)
            bias = b_ref[h]
            cols = slice(h * HD, (h + 1) * HD)
            for n in range(tm // BLK):
                rows = slice(n * BLK, (n + 1) * BLK)
                mixed = jnp.dot(wm, vn[rows, cols], preferred_element_type=jnp.float32) + bias
                z_ref[rows, DP + h * HD:DP + (h + 1) * HD] = (u[rows, cols] * mixed).astype(z_ref.dtype)

    a_spec, prev_spec, _, u_spec, v_spec = _mixer_specs(S, tm, DP, DS)
    return pl.pallas_call(
        body, name="mixer_fwd", grid=(S // tm,),
        in_specs=[a_spec, prev_spec, u_spec, v_spec, _whole(pw), _whole(ps_row), _whole(gs_row), _whole(ws), _whole(b_full)],
        out_specs=pl.BlockSpec((tm, DP + DS), lambda i: (i, 0)),
        out_shape=jax.ShapeDtypeStruct((S, DP + DS), MXU_DTYPE),
        compiler_params=_params(1),
    )(proj, proj, proj, proj, pw, ps_row, gs_row, ws, b_full)


def _mixer_bwd(proj, dz, pw, ps_row, gs_row, ws, b_full, tm=256):
    S = proj.shape[0]
    G, PG, _ = pw.shape
    DP = G * PG
    H, BLK, _ = ws.shape
    DS = gs_row.shape[1]
    HD = DS // H
    tm = _tile(S, tm, BLK)
    nb = S // tm
    ext = tm + POOL_HALO

    def body(a_ref, prev_ref, u_ref, v_ref, dzp_ref, dzp_next_ref, dzs_ref, pw_ref, ps_ref, gs_ref, ws_ref, b_ref,
             dproj_ref, dpw_ref, dps_ref, dgs_ref, dws_ref, db_ref, dvn_ref):
        i = pl.program_id(0)

        @pl.when(i == 0)
        def _():
            for r in (dpw_ref, dps_ref, dgs_ref, dws_ref, db_ref):
                r[...] = jnp.zeros_like(r)

        a = a_ref[...]
        halo = jnp.where(i > 0, prev_ref[...], 0.0)
        dyp = dzp_ref[...]
        dyp_next = jnp.where(i < nb - 1, dzp_next_ref[...], 0.0)
        dq_ext = jnp.concatenate([dyp, dyp_next], axis=0) * ps_ref[...]
        t_ext = lax.broadcasted_iota(jnp.int32, (ext, 1), 0) + i * tm
        for g in range(G):
            cols = slice(g * PG, (g + 1) * PG)
            w = POOL_WINDOWS[g]
            pb = _pool_inputs(a, halo, i * tm, tm, g, PG).astype(MXU_DTYPE)
            y_pre = jnp.dot(pb, pw_ref[g], preferred_element_type=jnp.float32)
            dps_ref[:, cols] += jnp.sum(dyp[:, cols] * y_pre, axis=0, keepdims=True)
            dqb = dq_ext[:, cols].astype(MXU_DTYPE)
            dpw_ref[g] += lax.dot_general(pb, dqb[:tm], TN, preferred_element_type=jnp.float32)
            dp = lax.dot_general(dqb, pw_ref[g], NT, preferred_element_type=jnp.float32)
            f = dp / jnp.minimum(t_ext + 1, w).astype(jnp.float32)
            span = 1
            while span < w:
                f = f + pltpu.roll(f, ext - span, axis=0)
                span *= 2
            dproj_ref[:, cols] = (f[:tm] - dp[:tm]).astype(dproj_ref.dtype)

        v = v_ref[...]
        rv = _rsqrt_mean_sq(v)
        vhat = v * rv
        gs = gs_ref[...]
        vnb = (vhat * gs).astype(MXU_DTYPE)
        u = u_ref[...]
        dys = dzs_ref[...]
        mask = _chunk_mask(BLK)
        for h in range(H):
            wm = jnp.where(mask, ws_ref[h], 0.0).astype(MXU_DTYPE)
            bias = b_ref[h]
            cols = slice(h * HD, (h + 1) * HD)
            dm_sum = jnp.zeros((BLK, HD), jnp.float32)
            dws_sum = jnp.zeros((BLK, BLK), jnp.float32)
            for n in range(tm // BLK):
                rows = slice(n * BLK, (n + 1) * BLK)
                vblk = vnb[rows, cols]
                mixed = jnp.dot(wm, vblk, preferred_element_type=jnp.float32) + bias
                dy = dys[rows, cols]
                dproj_ref[rows, DP + h * HD:DP + (h + 1) * HD] = (dy * mixed).astype(dproj_ref.dtype)
                dm = dy * u[rows, cols]
                dm_sum = dm_sum + dm
                dmb = dm.astype(MXU_DTYPE)
                dws_sum = dws_sum + lax.dot_general(dmb, vblk, NT, preferred_element_type=jnp.float32)
                dvn_ref[rows, cols] = lax.dot_general(wm, dmb, TN, preferred_element_type=jnp.float32)
            dws_ref[h] += jnp.where(mask, dws_sum, 0.0)
            db_ref[h:h + 1, :] += jnp.sum(dm_sum.T, axis=0, keepdims=True)
        dvn = dvn_ref[...]
        dvg = dvn * gs
        dv = rv * (dvg - vhat * jnp.mean(dvg * vhat, axis=-1, keepdims=True))
        dgs_ref[...] += jnp.sum(dvn * vhat, axis=0, keepdims=True)
        dproj_ref[:, DP + DS:] = dv.astype(dproj_ref.dtype)

    a_spec, prev_spec, next_spec, u_spec, v_spec = _mixer_specs(S, tm, DP, DS)
    dzp_spec = pl.BlockSpec((tm, DP), lambda i: (i, 0))
    dzs_spec = pl.BlockSpec((tm, DS), lambda i: (i, 1))
    small = [pw, ps_row, gs_row, ws, b_full]
    out_shape = [
        jax.ShapeDtypeStruct((S, DP + 2 * DS), MXU_DTYPE),
        jax.ShapeDtypeStruct(pw.shape, jnp.float32),
        jax.ShapeDtypeStruct(ps_row.shape, jnp.float32),
        jax.ShapeDtypeStruct(gs_row.shape, jnp.float32),
        jax.ShapeDtypeStruct(ws.shape, jnp.float32),
        jax.ShapeDtypeStruct((H, BLK), jnp.float32),
    ]
    out_specs = [pl.BlockSpec((tm, DP + 2 * DS), lambda i: (i, 0))] + [_whole(s) for s in out_shape[1:]]
    return pl.pallas_call(
        body, name="mixer_bwd", grid=(nb,),
        in_specs=[a_spec, prev_spec, u_spec, v_spec, dzp_spec, next_spec, dzs_spec] + [_whole(s) for s in small],
        out_specs=out_specs, out_shape=out_shape,
        scratch_shapes=[pltpu.VMEM((tm, DS), jnp.float32)],
        compiler_params=_params(1),
    )(proj, proj, proj, proj, dz, dz, dz, *small)


def _softmax_rows(q, k, scale):
    s = lax.dot_general(q, k, NT, preferred_element_type=jnp.float32) * scale
    e = jnp.exp(s - jnp.max(s, axis=-1, keepdims=True))
    return e / jnp.sum(e, axis=-1, keepdims=True)


def _attn_fwd(q, k, v, tm=512):
    S, D = q.shape
    M = k.shape[0]
    HD = D // N_XATTN_HEADS
    scale = HD ** -0.5
    tm = _tile(S, tm, 16)

    def body(q_ref, k_ref, v_ref, o_ref):
        for h in range(N_XATTN_HEADS):
            cols = slice(h * HD, (h + 1) * HD)
            p = _softmax_rows(q_ref[:, cols], k_ref[:, cols], scale)
            o = jnp.dot(p.astype(MXU_DTYPE), v_ref[:, cols], preferred_element_type=jnp.float32)
            o_ref[:, cols] = o.astype(o_ref.dtype)

    row = pl.BlockSpec((tm, D), lambda i: (i, 0))
    mem = pl.BlockSpec((M, D), lambda i: (0, 0))
    return pl.pallas_call(
        body, name="attn_fwd", grid=(S // tm,), in_specs=[row, mem, mem], out_specs=row,
        out_shape=jax.ShapeDtypeStruct((S, D), MXU_DTYPE), compiler_params=_params(1),
    )(q, k, v)


def _attn_bwd(q, k, v, do, tm=512):
    S, D = q.shape
    M = k.shape[0]
    HD = D // N_XATTN_HEADS
    scale = HD ** -0.5
    tm = _tile(S, tm, 16)

    def body(q_ref, k_ref, v_ref, do_ref, dq_ref, dk_ref, dv_ref):
        @pl.when(pl.program_id(0) == 0)
        def _():
            dk_ref[...] = jnp.zeros_like(dk_ref)
            dv_ref[...] = jnp.zeros_like(dv_ref)

        for h in range(N_XATTN_HEADS):
            cols = slice(h * HD, (h + 1) * HD)
            qh, kh, vh, doh = q_ref[:, cols], k_ref[:, cols], v_ref[:, cols], do_ref[:, cols]
            p = _softmax_rows(qh, kh, scale)
            dp = lax.dot_general(doh, vh, NT, preferred_element_type=jnp.float32)
            dv_ref[:, cols] += lax.dot_general(p.astype(MXU_DTYPE), doh, TN, preferred_element_type=jnp.float32)
            ds = (p * (dp - jnp.sum(dp * p, axis=-1, keepdims=True)) * scale).astype(MXU_DTYPE)
            dq_ref[:, cols] = jnp.dot(ds, kh, preferred_element_type=jnp.float32).astype(dq_ref.dtype)
            dk_ref[:, cols] += lax.dot_general(ds, qh, TN, preferred_element_type=jnp.float32)

    row = pl.BlockSpec((tm, D), lambda i: (i, 0))
    mem = pl.BlockSpec((M, D), lambda i: (0, 0))
    return pl.pallas_call(
        body, name="attn_bwd", grid=(S // tm,), in_specs=[row, mem, mem, row], out_specs=[row, mem, mem],
        out_shape=[jax.ShapeDtypeStruct((S, D), MXU_DTYPE), jax.ShapeDtypeStruct((M, D), jnp.float32),
                   jax.ShapeDtypeStruct((M, D), jnp.float32)],
        compiler_params=_params(1),
    )(q, k, v, do)


def _proj_residual_norm(name, a, w2d, resid, g_row, tm=256, after=None):
    M, K = a.shape
    N = w2d.shape[1]
    tm = _tile(M, tm, 16)

    def epilogue(accs, ex_refs, out_refs, i, j):
        x_new = ex_refs[0][...] + accs[0]
        out_refs[0][...] = x_new
        out_refs[1][...] = (x_new * _rsqrt_mean_sq(x_new) * ex_refs[1][...]).astype(MXU_DTYPE)

    row = lambda i, j, k: (i, 0)
    return _fused_matmul(
        name, (M // tm, 1, 1),
        [(a, (tm, K), row), (w2d, (K, N), lambda i, j, k: (0, 0))],
        [(0, 1, 0, NN)], [(tm, N)],
        [(resid, (tm, N), row), (g_row, (1, N), lambda i, j, k: (0, 0))] + _order_after(after),
        [((M, N), jnp.float32, (tm, N), row), ((M, N), MXU_DTYPE, (tm, N), row)], epilogue)


def _ffn_gate_up(h, wg3, wu3, tm=512):
    M, K = h.shape
    J, _, Nc = wg3.shape
    tm = _tile(M, tm, 16)

    def epilogue(accs, ex_refs, out_refs, i, j):
        gate, up = accs
        sg = _sigmoid(gate)
        silu = gate * sg
        out_refs[0][...] = (silu * up).astype(MXU_DTYPE)
        out_refs[1][...] = (up * (sg * (1.0 + gate * (1.0 - sg)))).astype(MXU_DTYPE)
        out_refs[2][...] = silu.astype(MXU_DTYPE)

    col = lambda i, j, k: (i, j)
    wspec = lambda i, j, k: (j, 0, 0)
    return _fused_matmul(
        "ffn_gate_up", (M // tm, J, 1),
        [(h, (tm, K), lambda i, j, k: (i, 0)), (wg3, (None, K, Nc), wspec), (wu3, (None, K, Nc), wspec)],
        [(0, 1, 0, NN), (0, 2, 1, NN)], [(tm, Nc)] * 2, [],
        [((M, J * Nc), MXU_DTYPE, (tm, Nc), col)] * 3, epilogue)


def _ffn_down(act, wd2d, x2, tm=512, tn=1024):
    M, F = act.shape
    D = wd2d.shape[1]
    tm, tn = _tile(M, tm, 16), _tile(D, tn, LANES)

    def epilogue(accs, ex_refs, out_refs, i, j):
        out_refs[0][...] = ex_refs[0][...] + accs[0]

    tile = lambda i, j, k: (i, j)
    return _fused_matmul(
        "ffn_down", (M // tm, D // tn, 1),
        [(act, (tm, F), lambda i, j, k: (i, 0)), (wd2d, (F, tn), lambda i, j, k: (0, j))],
        [(0, 1, 0, NN)], [(tm, tn)], [(x2, (tm, tn), tile)],
        [((M, D), jnp.float32, (tm, tn), tile)], epilogue)[0]


def _final_norm_loss(x3, target, g_row, tm=256):
    S, D = x3.shape
    tm = _tile(S, tm, 16)

    def body(x_ref, t_ref, g_ref, dx_ref, dxl_ref, loss_ref, dg_ref):
        x3v = x_ref[...]
        g = g_ref[...]
        r = _rsqrt_mean_sq(x3v)
        xhat = x3v * r
        diff = xhat * g - t_ref[...]
        dy = diff / D
        dyg = dy * g
        dx = r * (dyg - xhat * jnp.mean(dyg * xhat, axis=-1, keepdims=True))
        dx_ref[...] = dx
        dxl_ref[...] = dx.astype(MXU_DTYPE)

        @pl.when(pl.program_id(0) == 0)
        def _():
            loss_ref[...] = jnp.zeros_like(loss_ref)
            dg_ref[...] = jnp.zeros_like(dg_ref)

        loss_ref[...] += 0.5 * _sum_all(jnp.mean(diff * diff, axis=-1, keepdims=True))
        dg_ref[...] += jnp.sum(dy * xhat, axis=0, keepdims=True)

    row = pl.BlockSpec((tm, D), lambda i: (i, 0))
    vec = pl.BlockSpec((1, D), lambda i: (0, 0))
    return pl.pallas_call(
        body, name="final_norm_loss", grid=(S // tm,), in_specs=[row, row, vec],
        out_specs=[row, row, pl.BlockSpec((1, 1), lambda i: (0, 0)), vec],
        out_shape=[jax.ShapeDtypeStruct((S, D), jnp.float32), jax.ShapeDtypeStruct((S, D), MXU_DTYPE),
                   jax.ShapeDtypeStruct((1, 1), jnp.float32), jax.ShapeDtypeStruct((1, D), jnp.float32)],
        compiler_params=_params(1),
    )(x3, target, g_row)


def _ffn_down_bwd(dx3_low, wd2d, act_by_gate, act_by_up, n_shards, tm=512):
    M, D = dx3_low.shape
    F = wd2d.shape[0]
    tn = F // n_shards
    tm = _tile(M, tm, 16)

    def epilogue(accs, ex_refs, out_refs, i, j):
        dact = accs[0]
        out_refs[0][...] = (dact * ex_refs[0][...].astype(jnp.float32)).astype(MXU_DTYPE)
        out_refs[1][...] = (dact * ex_refs[1][...].astype(jnp.float32)).astype(MXU_DTYPE)

    col = lambda i, j, k: (i, j)
    return _fused_matmul(
        "ffn_down_bwd", (M // tm, n_shards, 1),
        [(dx3_low, (tm, D), lambda i, j, k: (i, 0)), (wd2d, (tn, D), lambda i, j, k: (j, 0))],
        [(0, 1, 0, NT)], [(tm, tn)],
        [(act_by_gate, (tm, tn), col), (act_by_up, (tm, tn), col)],
        [((M, F), MXU_DTYPE, (tm, tn), col)] * 2, epilogue)


def _matmul_nt_norm_bwd(name, a, b, xin, g_row, dres, low_dtype, tm=256, after=None):
    M = a.shape[0]
    sharded = b.ndim == 3
    N, Kc = b.shape[-2:]
    J = b.shape[0] if sharded else 1
    tm = _tile(M, tm, 16)
    has_low = low_dtype is not None

    def epilogue(accs, ex_refs, out_refs, i, j):
        dx, dg = _rmsnorm_bwd_values(accs[0], ex_refs[0][...], ex_refs[1][...])
        dx = dx + ex_refs[2][...]
        out_refs[0][...] = dx
        if has_low:
            out_refs[1][...] = dx.astype(low_dtype)
        dg_ref = out_refs[-1]

        @pl.when(i == 0)
        def _():
            dg_ref[...] = jnp.zeros_like(dg_ref)

        dg_ref[...] += dg

    row = lambda i, j, k: (i, 0)
    const = lambda i, j, k: (0, 0)
    b_operand = (b, (None, N, Kc), lambda i, j, k: (k, 0, 0)) if sharded else (b, (N, Kc), const)
    outs = [((M, N), jnp.float32, (tm, N), row)]
    if has_low:
        outs.append(((M, N), low_dtype, (tm, N), row))
    outs.append(((1, N), jnp.float32, (1, N), const))
    return _fused_matmul(
        name, (M // tm, 1, J), [(a, (tm, Kc), lambda i, j, k: (i, k)), b_operand],
        [(0, 1, 0, NT)], [(tm, N)],
        [(xin, (tm, N), row), (g_row, (1, N), const), (dres, (tm, N), row)] + _order_after(after),
        outs, epilogue)


def _row_tile(R, C, target_bytes=2 ** 21):
    return _tile(R, max(16, target_bytes // (4 * C)), 16)


def _prefetch_call(body, name, grid, pos, in_specs, out_specs, out_shape, ins, after=None):
    ordered = [] if after is None else [after]
    grid_spec = pltpu.PrefetchScalarGridSpec(
        num_scalar_prefetch=1, grid=grid, in_specs=list(in_specs) + [ANY] * len(ordered), out_specs=out_specs)
    return pl.pallas_call(body, name=name, grid_spec=grid_spec, out_shape=out_shape,
                          compiler_params=_params(len(grid)))(pos, *ins, *ordered)


def _runs(arrays):
    runs = []
    for i, a in enumerate(arrays):
        if runs and arrays[runs[-1][-1]].shape == a.shape:
            runs[-1].append(i)
        else:
            runs.append([i])
    return runs


def _per_run(fn, names, *array_lists):
    out = [None] * len(names)
    for run in _runs(array_lists[0]):
        results = fn("_".join(names[i] for i in run), *[[arrays[i] for i in run] for arrays in array_lists])
        for i, r in zip(run, results):
            out[i] = r
    return out


def _cast_into_slabs(name, xs, dtype, pos, after=None):
    n = len(xs)
    R, C = xs[0].shape
    tr = _row_tile(R, C * n, 2 ** 22)

    def body(pos_ref, *refs):
        for t in range(n):
            refs[len(refs) - n + t][...] = refs[t][...].astype(dtype)

    return _prefetch_call(
        body, "cast_" + name, (R // tr,), pos, [pl.BlockSpec((tr, C), lambda i, pos: (i, 0))] * n,
        [pl.BlockSpec((None, tr, C), lambda i, pos: (pos[0], i, 0))] * n,
        [jax.ShapeDtypeStruct((N_CHIPS, R, C), dtype)] * n, xs, after)


def _chip_sum(name, partials, gots, dtype, pos):
    n = len(partials)
    J, R, C = partials[0].shape
    Rh = R // 2
    tr = _row_tile(Rh, C * n, 2 ** 22)

    def body(pos_ref, *refs):
        for t in range(n):
            refs[2 * n + t][...] = (refs[t][...].astype(jnp.float32) + refs[n + t][...].astype(jnp.float32)).astype(dtype)

    blk = pl.BlockSpec((None, tr, C), lambda j, i, pos: (j, i, 0))
    return _prefetch_call(
        body, "chip_sum_" + name, (J, Rh // tr), pos,
        [pl.BlockSpec((None, None, tr, C), lambda j, i, pos: (j, pos[1], i, 0))] * n + [blk] * n, [blk] * n,
        [jax.ShapeDtypeStruct((J, Rh, C), dtype)] * n, [p.reshape(J, 2, Rh, C) for p in partials] + list(gots))


def _owner_sum(name, chip_sums, others, pos):
    n = len(chip_sums)
    J, Rh, C = chip_sums[0].shape
    n_other = others[0].shape[0]
    tr = _row_tile(Rh, C * (n_other + 1) * n, 2 ** 22)

    def body(pos_ref, *refs):
        for t in range(n):
            acc = refs[t][...].astype(jnp.float32)
            for k in range(n_other):
                acc = acc + refs[n + t][k].astype(jnp.float32)
            refs[2 * n + t][...] = acc

    return _prefetch_call(
        body, "owner_sum_" + name, (Rh // tr,), pos,
        [pl.BlockSpec((None, tr, C), lambda i, pos: (pos[0], i, 0))] * n
        + [pl.BlockSpec((n_other, tr, C), lambda i, pos: (0, i, 0))] * n,
        [pl.BlockSpec((None, tr, C), lambda i, pos: (pos[1], i, 0))] * n,
        [jax.ShapeDtypeStruct((2, Rh, C), jnp.float32)] * n, list(chip_sums) + list(others))


def _adamw_values(w, g, m, v):
    m = ADAM_B1 * m + (1.0 - ADAM_B1) * g
    v = ADAM_B2 * v + (1.0 - ADAM_B2) * (g * g)
    m_hat = m / (1.0 - ADAM_B1 ** ADAM_STEP)
    v_hat = v / (1.0 - ADAM_B2 ** ADAM_STEP)
    delta = -ADAM_LR * (m_hat / (jnp.sqrt(v_hat) + ADAM_EPS) + ADAM_WD * w)
    return delta, m, v


def _adamw(name, ws, gs, ms, vs):
    n = len(ws)
    R, C = ws[0].shape
    tr = _row_tile(R, C * n, 2 ** 20)

    def body(*refs):
        w_refs, g_refs, m_refs, v_refs = (refs[q * n:(q + 1) * n] for q in range(4))
        outs = refs[4 * n:]
        for t in range(n):
            g = g_refs[t][...]
            outs[4 * t][...] = g
            outs[4 * t + 1][...], outs[4 * t + 2][...], outs[4 * t + 3][...] = _adamw_values(
                w_refs[t][...], g, m_refs[t][...], v_refs[t][...])

    blk = pl.BlockSpec((tr, C), lambda i: (i, 0))
    res = pl.pallas_call(
        body, name="adamw_" + name, grid=(R // tr,), in_specs=[blk] * (4 * n), out_specs=[blk] * (4 * n),
        out_shape=[jax.ShapeDtypeStruct((R, C), jnp.float32)] * (4 * n), compiler_params=_params(1),
    )(*ws, *gs, *ms, *vs)
    return [tuple(res[4 * t:4 * t + 4]) for t in range(n)]


def _adamw_small(g_packed, ws, ms, vs, row_counts):
    n = len(ws)

    def body(*refs):
        g_ref = refs[0]
        w_refs, m_refs, v_refs = refs[1:1 + n], refs[1 + n:1 + 2 * n], refs[1 + 2 * n:1 + 3 * n]
        outs = refs[1 + 3 * n:]
        off = 0
        for t in range(n):
            g = g_ref[off:off + row_counts[t], :]
            off += row_counts[t]
            delta, m_new, v_new = _adamw_values(w_refs[t][...], g, m_refs[t][...], v_refs[t][...])
            outs[4 * t][...] = g
            outs[4 * t + 1][...] = delta
            outs[4 * t + 2][...] = m_new
            outs[4 * t + 3][...] = v_new

    out_shape = []
    for t in range(n):
        out_shape += [jax.ShapeDtypeStruct((row_counts[t], LANES), jnp.float32)] * 4
    return pl.pallas_call(
        body, name="adamw_small", in_specs=[VMEM_WHOLE] * (1 + 3 * n), out_specs=[VMEM_WHOLE] * (4 * n),
        out_shape=out_shape, compiler_params=pltpu.CompilerParams(vmem_limit_bytes=VMEM_LIMIT_BYTES),
    )(g_packed, *ws, *ms, *vs)


def _position():
    x, y, c = lax.axis_index("x"), lax.axis_index("y"), lax.axis_index("c")
    other_chips = [(1 - x, y), (x, 1 - y), (1 - x, 1 - y)]
    return x, y, c, other_chips


def _half(rows, which):
    return pl.ds(which * (rows // 2), rows // 2)


def _chunk_index(shape, dtype):
    rows = shape[-2]
    total = math.prod(shape) * jnp.dtype(dtype).itemsize
    n = max(1, min(rows // COPY_ROW_ALIGN, total // COPY_CHUNK_BYTES))
    step = -(-rows // n)
    step = -(-step // COPY_ROW_ALIGN) * COPY_ROW_ALIGN
    lead = (slice(None),) * (len(shape) - 2)
    return [lead + (pl.ds(s, min(step, rows - s)),) for s in range(0, rows, step)]


def _start_in_chunks(make, src, dst, waited_here=True):
    for idx in _chunk_index(src.shape, src.dtype):
        make(src.at[idx], dst.at[idx]).start()
    return make(src, dst) if waited_here else None


def _remote_copier(send_sem, recv_sem, to):
    return lambda src, dst: pltpu.make_async_remote_copy(
        src_ref=src, dst_ref=dst, send_sem=send_sem, recv_sem=recv_sem, device_id=to, device_id_type=MESH)


def _slab_half(ref, chip, which):
    return ref.at[chip, _half(ref.shape[1], which)]


def _slab_quarter(ref, chip, which, q):
    rows = ref.shape[1] // 4
    return ref.at[chip, pl.ds(which * 2 * rows + q * rows, rows)]


def _gather_step(name, arrived=(), ring=(), direct=(), after=None):
    groups = [g[2] for g in arrived] + list(direct) + list(ring)
    per_tensor = [2] * len(arrived) + [3] * len(direct) + [2] * len(ring)
    flat = [a for g in groups for a in g]
    T, G, A = len(flat), len(groups), len(arrived)
    ordered = [] if after is None else [after]
    sems_in = [sem for g in arrived for sem in g[:2]]

    def body(*refs):
        ins = refs[:T]
        first_hop = refs[T:T + 2 * A]
        out_sems = refs[T + 2 * A + len(ordered):T + 2 * A + len(ordered) + 2 * G]
        token = refs[-1]
        x, y, c, other_chips = _position()
        mine = 2 * x + y
        x_nbr, y_nbr = (1 - x, y, c), (x, 1 - y, c)
        from_x, from_y = 2 * (1 - x) + y, 2 * x + (1 - y)
        t0 = 0
        for gi, group in enumerate(groups):
            send, recv = out_sems[2 * gi], out_sems[2 * gi + 1]
            for t in range(len(group)):
                ref = ins[t0 + t]
                own = _slab_half(ref, mine, c)
                if gi < A:
                    send1, recv1 = first_hop[2 * gi], first_hop[2 * gi + 1]
                    for k, (peer, chip) in enumerate(((x_nbr, from_x), (y_nbr, from_y))):
                        cp = _remote_copier(send1.at[2 * t + k], recv1.at[2 * t + k], peer)(own, _slab_half(ref, chip, c))
                        cp.wait_send()
                        cp.wait_recv()
                    for k, (peer, chip) in enumerate(((y_nbr, from_x), (x_nbr, from_y))):
                        piece = _slab_quarter(ref, chip, c, k)
                        _start_in_chunks(_remote_copier(send.at[2 * t + k], recv.at[2 * t + k], peer), piece, piece,
                                         waited_here=False)
                elif gi < A + len(direct):
                    for k, (cx, cy) in enumerate(other_chips):
                        _start_in_chunks(_remote_copier(send.at[3 * t + k], recv.at[3 * t + k], (cx, cy, c)), own, own,
                                         waited_here=False)
                else:
                    for k, peer in enumerate((x_nbr, y_nbr)):
                        _start_in_chunks(_remote_copier(send.at[2 * t + k], recv.at[2 * t + k], peer), own, own,
                                         waited_here=False)
            t0 += len(group)
        token[...] = jnp.zeros_like(token)

    sem_shapes = []
    for group, n in zip(groups, per_tensor):
        sem_shapes += [pltpu.SemaphoreType.DMA((n * len(group),))] * 2
    res = pl.pallas_call(
        body, name=name, in_specs=[HBM_SPEC] * T + [SEM_SPEC] * (2 * A) + [ANY] * len(ordered),
        out_specs=[SEM_SPEC] * (2 * G) + [HBM_SPEC] * T + [VMEM_WHOLE],
        out_shape=sem_shapes + [pltpu.HBM(a.shape, a.dtype) for a in flat] + [jax.ShapeDtypeStruct((8, LANES), jnp.float32)],
        input_output_aliases={t: 2 * G + t for t in range(T)},
        compiler_params=pltpu.CompilerParams(has_side_effects=SIDE_EFFECT),
    )(*[pltpu.with_memory_space_constraint(a, pltpu.HBM) for a in flat], *sems_in, *ordered)
    flights, t0 = [], 2 * G
    for gi, group in enumerate(groups):
        flights.append((res[2 * gi], res[2 * gi + 1], res[t0:t0 + len(group)]))
        t0 += len(group)
    return flights[:A], flights[A:A + len(direct)], flights[A + len(direct):], res[-1]


def _relayed_wait(name, flight, after):
    send_sems, recv_sems, slabs = flight
    T = len(slabs)

    def body(*refs):
        ins = refs[:T]
        send, recv = refs[T:T + 2]
        x, y, c, _ = _position()
        x_nbr, y_nbr = (1 - x, y, c), (x, 1 - y, c)
        from_x, from_y, diagonal = 2 * (1 - x) + y, 2 * x + (1 - y), 2 * (1 - x) + (1 - y)
        for t in range(T):
            for k, (peer, chip) in enumerate(((y_nbr, from_x), (x_nbr, from_y))):
                cp = _remote_copier(send.at[2 * t + k], recv.at[2 * t + k], peer)(
                    _slab_quarter(ins[t], chip, c, k), _slab_quarter(ins[t], diagonal, c, k))
                cp.wait_send()
                cp.wait_recv()

    return pl.pallas_call(
        body, name=name, in_specs=[HBM_SPEC] * T + [SEM_SPEC, SEM_SPEC, ANY], out_specs=[HBM_SPEC] * T,
        out_shape=[pltpu.HBM(a.shape, a.dtype) for a in slabs], input_output_aliases={t: t for t in range(T)},
        compiler_params=pltpu.CompilerParams(has_side_effects=SIDE_EFFECT),
    )(*slabs, send_sems, recv_sems, after)


def _gather_wait(name, flight, after):
    send_sems, recv_sems, slabs = flight
    T = len(slabs)

    def body(*refs):
        ins = refs[:T]
        send, recv = refs[T:T + 2]
        x, y, c, other_chips = _position()
        mine = 2 * x + y
        for t in range(T):
            own = _slab_half(ins[t], mine, c)
            for k, (cx, cy) in enumerate(other_chips):
                landed = _slab_half(ins[t], 2 * cx + cy, c)
                cp = _remote_copier(send.at[3 * t + k], recv.at[3 * t + k], (cx, cy, c))(own, landed)
                cp.wait_send()
                cp.wait_recv()

    return pl.pallas_call(
        body, name=name, in_specs=[HBM_SPEC] * T + [SEM_SPEC, SEM_SPEC, ANY], out_specs=[HBM_SPEC] * T,
        out_shape=[pltpu.HBM(s.shape, s.dtype) for s in slabs], input_output_aliases={t: t for t in range(T)},
        compiler_params=pltpu.CompilerParams(has_side_effects=SIDE_EFFECT),
    )(*slabs, send_sems, recv_sems, after)


def _share_with_sibling(name, slabs):
    T = len(slabs)

    def body(*refs):
        outs = refs[T:2 * T]
        send_sems, recv_sems = refs[2 * T:]
        x, y, c, other_chips = _position()
        sibling = (x, y, 1 - c)
        started = []
        for t in range(T):
            for k, (cx, cy) in enumerate(other_chips):
                landed = _slab_half(outs[t], 2 * cx + cy, c)
                started.append(_start_in_chunks(_remote_copier(send_sems.at[t, k], recv_sems.at[t, k], sibling), landed, landed))
        for t in range(T):
            for k, (cx, cy) in enumerate(other_chips):
                arriving = _slab_half(outs[t], 2 * cx + cy, 1 - c)
                _remote_copier(send_sems.at[t, k], recv_sems.at[t, k], sibling)(arriving, arriving).wait_recv()
        for cp in started:
            cp.wait_send()

    return pl.pallas_call(
        body, name=name, in_specs=[ANY] * T, out_specs=[ANY] * T,
        out_shape=[jax.ShapeDtypeStruct(s.shape, s.dtype) for s in slabs],
        input_output_aliases={t: t for t in range(T)},
        scratch_shapes=[pltpu.SemaphoreType.DMA((T, 3)), pltpu.SemaphoreType.DMA((T, 3))],
    )(*slabs)


def _swap_halves(name, grads):
    T = len(grads)

    def body(*refs):
        ins, got = refs[:T], refs[T:2 * T]
        send_sems, recv_sems = refs[2 * T:]
        x, y, c, _ = _position()
        remote = []
        for t in range(T):
            rows = ins[t].shape[1]
            remote.append(_start_in_chunks(
                _remote_copier(send_sems.at[t], recv_sems.at[t], (x, y, 1 - c)), ins[t].at[:, _half(rows, 1 - c)], got[t]))
        for cp in remote:
            cp.wait()

    return pl.pallas_call(
        body, name=name, in_specs=[ANY] * T, out_specs=[ANY] * T,
        out_shape=[jax.ShapeDtypeStruct((g.shape[0], g.shape[1] // 2, g.shape[2]), g.dtype) for g in grads],
        scratch_shapes=[pltpu.SemaphoreType.DMA((T,)), pltpu.SemaphoreType.DMA((T,))],
    )(*grads)


def _swap_start(name, grads):
    T = len(grads)
    lands = [lax.empty((g.shape[0], g.shape[1] // 2, g.shape[2]), g.dtype) for g in grads]

    def body(*refs):
        ins, got = refs[:T], refs[T:2 * T]
        send_sems, recv_sems = refs[2 * T:2 * T + 2]
        token = refs[-1]
        x, y, c, _ = _position()
        for t in range(T):
            _start_in_chunks(_remote_copier(send_sems.at[t], recv_sems.at[t], (x, y, 1 - c)),
                             ins[t].at[:, _half(ins[t].shape[1], 1 - c)], got[t], waited_here=False)
        token[...] = jnp.zeros_like(token)

    arrays = list(grads) + lands
    res = pl.pallas_call(
        body, name=name, in_specs=[HBM_SPEC] * (2 * T),
        out_specs=[SEM_SPEC, SEM_SPEC] + [HBM_SPEC] * (2 * T) + [VMEM_WHOLE],
        out_shape=[pltpu.SemaphoreType.DMA((T,)), pltpu.SemaphoreType.DMA((T,))]
        + [pltpu.HBM(a.shape, a.dtype) for a in arrays] + [jax.ShapeDtypeStruct((8, LANES), jnp.float32)],
        input_output_aliases={i: 2 + i for i in range(2 * T)},
        compiler_params=pltpu.CompilerParams(has_side_effects=SIDE_EFFECT),
    )(*[pltpu.with_memory_space_constraint(a, pltpu.HBM) for a in arrays])
    return res[0], res[1], res[2:2 + T], res[2 + T:2 + 2 * T], res[-1]


def _swap_wait(name, flight, after):
    send_sems, recv_sems, grads, lands, _ = flight
    T = len(grads)

    def body(*refs):
        ins, got = refs[:T], refs[T:2 * T]
        send, recv = refs[2 * T:2 * T + 2]
        x, y, c, _ = _position()
        for t in range(T):
            cp = _remote_copier(send.at[t], recv.at[t], (x, y, 1 - c))(ins[t].at[:, _half(ins[t].shape[1], 1 - c)], got[t])
            cp.wait_send()
            cp.wait_recv()

    arrays = list(grads) + list(lands)
    res = pl.pallas_call(
        body, name=name, in_specs=[HBM_SPEC] * (2 * T) + [SEM_SPEC, SEM_SPEC, ANY], out_specs=[HBM_SPEC] * (2 * T),
        out_shape=[pltpu.HBM(a.shape, a.dtype) for a in arrays], input_output_aliases={i: i for i in range(2 * T)},
        compiler_params=pltpu.CompilerParams(has_side_effects=SIDE_EFFECT),
    )(*arrays, send_sems, recv_sems, after)
    return res[:T], res[T:]


def _scatter_start(name, chip_sums):
    T = len(chip_sums)
    lands = [lax.empty((3,) + s.shape[1:], s.dtype) for s in chip_sums]

    def body(*refs):
        srcs, dsts = refs[:T], refs[T:2 * T]
        send_sems, recv_sems = refs[2 * T:2 * T + 2]
        token = refs[-1]
        x, y, c, other_chips = _position()
        for t in range(T):
            for k, (cx, cy) in enumerate(other_chips):
                _start_in_chunks(_remote_copier(send_sems.at[3 * t + k], recv_sems.at[3 * t + k], (cx, cy, c)),
                                 srcs[t].at[2 * cx + cy], dsts[t].at[k], waited_here=False)
        token[...] = jnp.zeros_like(token)

    hbm = [pltpu.HBM(a.shape, a.dtype) for a in list(chip_sums) + lands]
    res = pl.pallas_call(
        body, name=name, in_specs=[HBM_SPEC] * (2 * T),
        out_specs=[SEM_SPEC, SEM_SPEC] + [HBM_SPEC] * (2 * T) + [VMEM_WHOLE],
        out_shape=[pltpu.SemaphoreType.DMA((3 * T,)), pltpu.SemaphoreType.DMA((3 * T,))] + hbm
        + [jax.ShapeDtypeStruct((8, LANES), jnp.float32)],
        input_output_aliases={i: 2 + i for i in range(2 * T)},
        compiler_params=pltpu.CompilerParams(has_side_effects=SIDE_EFFECT),
    )(*[pltpu.with_memory_space_constraint(a, pltpu.HBM) for a in list(chip_sums) + lands])
    return res[0], res[1], res[2:2 + T], res[2 + T:2 + 2 * T], res[-1]


def _scatter_wait(name, send_sems, recv_sems, chip_sums, lands, after):
    T = len(chip_sums)

    def body(*refs):
        srcs, dsts = refs[:T], refs[T:2 * T]
        send, recv = refs[2 * T:2 * T + 2]
        x, y, c, other_chips = _position()
        for t in range(T):
            for k, (cx, cy) in enumerate(other_chips):
                cp = _remote_copier(send.at[3 * t + k], recv.at[3 * t + k], (cx, cy, c))(srcs[t].at[2 * cx + cy], dsts[t].at[k])
                cp.wait_send()
                cp.wait_recv()

    arrays = list(chip_sums) + list(lands)
    res = pl.pallas_call(
        body, name=name, in_specs=[HBM_SPEC] * (2 * T) + [SEM_SPEC, SEM_SPEC, ANY],
        out_specs=[HBM_SPEC] * (2 * T), out_shape=[pltpu.HBM(a.shape, a.dtype) for a in arrays],
        input_output_aliases={i: i for i in range(2 * T)},
        compiler_params=pltpu.CompilerParams(has_side_effects=SIDE_EFFECT),
    )(*arrays, send_sems, recv_sems, after)
    return res[:T], res[T:]


def _as_shards(partial):
    return [p.reshape((N_CHIPS, -1, p.shape[-1])) for p in partial]


def _reduce_begin(tag, names, partial, pos):
    parts = _as_shards(partial)
    received = _swap_halves("swap_grad_halves_" + tag, parts)
    return _reduce_scatter_start(tag, names, parts, received, pos)


def _reduce_scatter_start(tag, names, parts, received, pos):
    chip_sums = _per_run(lambda name, p, r: _chip_sum(name, p, r, WIRE_DTYPE, pos), names, list(parts), list(received))
    return _scatter_start("scatter_start_" + tag, chip_sums)


def _reduce_end(tag, names, flight, after, pos):
    send_sems, recv_sems, chip_sums, lands, _ = flight
    chip_sums, others = _scatter_wait("scatter_wait_" + tag, send_sems, recv_sems, chip_sums, lands, after)
    reduced = _per_run(lambda name, cs, ot: _owner_sum(name, cs, ot, pos), names, list(chip_sums), list(others))
    return dict(zip(names, reduced))


def _join_start(name, halves):
    T = len(halves)

    def body(*refs):
        ins = refs[:T]
        send_sems, recv_sems = refs[T:T + 2]
        token = refs[-1]
        x, y, c, _ = _position()
        for t in range(T):
            _start_in_chunks(_remote_copier(send_sems.at[t], recv_sems.at[t], (x, y, 1 - c)),
                             ins[t].at[c], ins[t].at[c], waited_here=False)
        token[...] = jnp.zeros_like(token)

    res = pl.pallas_call(
        body, name=name, in_specs=[HBM_SPEC] * T,
        out_specs=[SEM_SPEC, SEM_SPEC] + [HBM_SPEC] * T + [VMEM_WHOLE],
        out_shape=[pltpu.SemaphoreType.DMA((T,)), pltpu.SemaphoreType.DMA((T,))]
        + [pltpu.HBM(h.shape, h.dtype) for h in halves] + [jax.ShapeDtypeStruct((8, LANES), jnp.float32)],
        input_output_aliases={t: 2 + t for t in range(T)},
        compiler_params=pltpu.CompilerParams(has_side_effects=SIDE_EFFECT),
    )(*[pltpu.with_memory_space_constraint(h, pltpu.HBM) for h in halves])
    return res[0], res[1], res[2:2 + T], res[-1]


def _join_wait(name, flight, after):
    send_sems, recv_sems, halves, _ = flight
    T = len(halves)

    def body(*refs):
        ins = refs[:T]
        send, recv = refs[T:T + 2]
        x, y, c, _ = _position()
        for t in range(T):
            cp = _remote_copier(send.at[t], recv.at[t], (x, y, 1 - c))(ins[t].at[c], ins[t].at[1 - c])
            cp.wait_send()
            cp.wait_recv()

    return pl.pallas_call(
        body, name=name, in_specs=[HBM_SPEC] * T + [SEM_SPEC, SEM_SPEC, ANY], out_specs=[HBM_SPEC] * T,
        out_shape=[pltpu.HBM(h.shape, h.dtype) for h in halves], input_output_aliases={t: t for t in range(T)},
        compiler_params=pltpu.CompilerParams(has_side_effects=SIDE_EFFECT),
    )(*halves, send_sems, recv_sems, after)


def _peers():
    x, y, c, _ = _position()
    flips = [(fx, fy, fc) for fx in (0, 1) for fy in (0, 1) for fc in (0, 1)][1:]
    return [(jnp.bitwise_xor(x, fx), jnp.bitwise_xor(y, fy), jnp.bitwise_xor(c, fc)) for fx, fy, fc in flips]


def _device_index(px, py, pc):
    return 4 * px + 2 * py + pc


def _exchange_start(name, slots):
    def body(slots_ref, send_sems, recv_sems, thru, token):
        x, y, c, _ = _position()
        own = slots_ref.at[_device_index(x, y, c)]
        for k, peer in enumerate(_peers()):
            _remote_copier(send_sems.at[k], recv_sems.at[k], peer)(own, own).start()
        token[...] = jnp.zeros_like(token)

    res = pl.pallas_call(
        body, name=name, in_specs=[HBM_SPEC], out_specs=[SEM_SPEC, SEM_SPEC, HBM_SPEC, VMEM_WHOLE],
        out_shape=[pltpu.SemaphoreType.DMA((N_DEV - 1,)), pltpu.SemaphoreType.DMA((N_DEV - 1,)),
                   pltpu.HBM(slots.shape, slots.dtype), jax.ShapeDtypeStruct((8, LANES), jnp.float32)],
        input_output_aliases={0: 2}, compiler_params=pltpu.CompilerParams(has_side_effects=SIDE_EFFECT),
    )(pltpu.with_memory_space_constraint(slots, pltpu.HBM))
    return res[0], res[1], res[2], res[3]


def _exchange_wait(name, flight, after):
    send_sems, recv_sems, slots, _ = flight

    def body(slots_ref, send, recv, after_ref, out_ref):
        x, y, c, _ = _position()
        own = slots_ref.at[_device_index(x, y, c)]
        for k, peer in enumerate(_peers()):
            cp = _remote_copier(send.at[k], recv.at[k], peer)(own, slots_ref.at[_device_index(*peer)])
            cp.wait_send()
            cp.wait_recv()

    return pl.pallas_call(
        body, name=name, in_specs=[HBM_SPEC, SEM_SPEC, SEM_SPEC, ANY], out_specs=HBM_SPEC,
        out_shape=pltpu.HBM(slots.shape, slots.dtype), input_output_aliases={0: 0},
        compiler_params=pltpu.CompilerParams(has_side_effects=SIDE_EFFECT),
    )(slots, send_sems, recv_sems, after)


def _sum_slots(name, slots):
    n, rows, lanes = slots.shape
    tr = _row_tile(rows, lanes * n)

    def body(s_ref, o_ref):
        acc = s_ref[0]
        for d in range(1, n):
            acc = acc + s_ref[d]
        o_ref[...] = acc

    return pl.pallas_call(
        body, name=name, grid=(rows // tr,), in_specs=[pl.BlockSpec((n, tr, lanes), lambda i: (0, i, 0))],
        out_specs=pl.BlockSpec((tr, lanes), lambda i: (i, 0)), out_shape=jax.ShapeDtypeStruct((rows, lanes), jnp.float32),
        compiler_params=_params(1))(slots)


BIG = ("w_in", "pool_w", "w_out", "w_q", "w_k", "w_v", "w_o", "w_gate", "w_up", "w_down")
SMALL = ("norm_mix_g", "pool_scale", "sgu_norm_g", "w_spatial", "b_spatial", "norm_xattn_g", "norm_mem_g",
         "norm_ffn_g", "final_norm_g")
GATHER_ORDER = (("w_in", "pool_w"), ("w_out",), ("w_q", "w_k", "w_v", "w_o"), ("w_gate", "w_up"), ("w_down",))
WEIGHTS = ("norm_mix_g", "w_in", "pool_w", "pool_scale", "sgu_norm_g", "w_spatial", "b_spatial", "w_out",
           "norm_xattn_g", "norm_mem_g", "w_q", "w_k", "w_v", "w_o", "norm_ffn_g", "w_gate", "w_up", "w_down",
           "final_norm_g")


def _as2d(a):
    return a.reshape(-1, a.shape[-1])


def _as_lanes(a):
    return a.reshape(-1, LANES)


def kernel(x, mem, norm_mix_g, w_in, pool_w, pool_scale, sgu_norm_g, w_spatial, b_spatial, w_out, norm_xattn_g, norm_mem_g, w_q, w_k, w_v, w_o, norm_ffn_g, w_gate, w_up, w_down, final_norm_g, loss_target, m_norm_mix_g, m_w_in, m_pool_w, m_pool_scale, m_sgu_norm_g, m_w_spatial, m_b_spatial, m_w_out, m_norm_xattn_g, m_norm_mem_g, m_w_q, m_w_k, m_w_v, m_w_o, m_norm_ffn_g, m_w_gate, m_w_up, m_w_down, m_final_norm_g, v_norm_mix_g, v_w_in, v_pool_w, v_pool_scale, v_sgu_norm_g, v_w_spatial, v_b_spatial, v_w_out, v_norm_xattn_g, v_norm_mem_g, v_w_q, v_w_k, v_w_v, v_w_o, v_norm_ffn_g, v_w_gate, v_w_up, v_w_down, v_final_norm_g):
    args = dict(locals())
    weights = {n: args[n] for n in WEIGHTS}
    moments_m = {n: args["m_" + n] for n in WEIGHTS}
    moments_v = {n: args["v_" + n] for n in WEIGHTS}
    low = MXU_DTYPE

    xs = x[0]
    mems = mem[0]
    target = loss_target[0]
    D = xs.shape[1]
    G, PG = pool_w.shape[1], pool_w.shape[3]

    pos = jnp.stack([2 * lax.axis_index("x") + lax.axis_index("y"), lax.axis_index("c")]).astype(jnp.int32)
    def cast(names, after=None):
        slabs = _per_run(lambda name, xs: _cast_into_slabs(name, xs, WIRE_DTYPE, pos, after), names,
                         [_as2d(weights[n]) for n in names])
        return dict(zip(names, slabs))

    def land(gi, flight, after, relayed=True):
        wait = _relayed_wait if relayed else _gather_wait
        landed = wait("gather_wait_%d" % gi, flight, after)
        return dict(zip(GATHER_ORDER[gi], _share_with_sibling("gather_share_%d" % gi, landed)))

    def rows(a):
        return a.reshape(-1, a.shape[-1])

    g_mix, g_xattn, g_mem, g_ffn = norm_mix_g, norm_xattn_g, norm_mem_g, norm_ffn_g
    g_final = final_norm_g.reshape(1, D)
    ws = w_spatial[0]
    b_full = jnp.broadcast_to(b_spatial[0][:, :, None], ws.shape[:2] + (sgu_norm_g.shape[1] // ws.shape[0],))

    first = cast(GATHER_ORDER[0])
    _, (sent_0,), _, started = _gather_step("gather_step_0", direct=[[first[n] for n in GATHER_ORDER[0]]])
    rest = cast([n for group in GATHER_ORDER[1:3] for n in group], after=started)
    _, (sent_1,), (hop_2,), started = _gather_step(
        "gather_step_1", direct=[[rest[n] for n in GATHER_ORDER[1]]], ring=[[rest[n] for n in GATHER_ORDER[2]]])
    h1 = _rmsnorm("norm_mix", xs, g_mix, low, after=started)
    mn = _rmsnorm("norm_mem", mems, g_mem, low)
    rest.update(cast(GATHER_ORDER[3], after=mn))
    rest.update(cast(GATHER_ORDER[4], after=rest[GATHER_ORDER[3][-1]]))
    slabs = [[rest[n] for n in group] for group in GATHER_ORDER[1:]]
    got = land(0, sent_0, rest["w_down"], relayed=False)
    wg_in = got["w_in"]
    pw = got["pool_w"].reshape(N_CHIPS, G, PG // N_CHIPS, PG).transpose(1, 0, 2, 3).reshape(G, PG, PG)
    proj = _matmul_nn_cols("proj_in", h1, wg_in, jnp.float32)
    z = _mixer_fwd(proj, pw, pool_scale, sgu_norm_g, ws, b_full)
    (relay_2,), _, (hop_3,), started = _gather_step("gather_step_2", arrived=[hop_2], ring=slabs[2:3], after=z)
    full = {"w_out": rows(land(1, sent_1, started, relayed=False)["w_out"])}
    x1, h2 = _proj_residual_norm("mix_out", z, full["w_out"], xs, g_xattn)
    full.update({n: rows(a) for n, a in land(2, relay_2, h2).items()})
    q = _matmul_nn("xattn_q", h2, full["w_q"], low)
    k = _matmul_nn("xattn_k", mn, full["w_k"], low)
    vv = _matmul_nn("xattn_v", mn, full["w_v"], low)
    o = _attn_fwd(q, k, vv)
    (relay_3,), (sent_4,), _, started = _gather_step("gather_step_3", arrived=[hop_3], direct=slabs[3:4], after=o)
    x2, h3 = _proj_residual_norm("xattn_out", o, full["w_o"], x1, g_ffn, after=started)
    got = land(3, relay_3, h3)
    wg_gate, wg_up = got["w_gate"], got["w_up"]
    act, act_by_gate, act_by_up = _ffn_gate_up(h3, wg_gate, wg_up)
    full["w_down"] = rows(land(4, sent_4, act, relayed=False)["w_down"])
    x3 = _ffn_down(act, full["w_down"], x2)
    dx3, dx3_low, loss_part, d_final_g = _final_norm_loss(x3, target, g_final)

    dgate, dup = _ffn_down_bwd(dx3_low, full["w_down"], act_by_gate, act_by_up, N_CHIPS)
    (d_w_down,) = _matmul_tn("grad_w_down", act, [dx3_low], WIRE_DTYPE)
    (d_w_gate,) = _matmul_tn_cols("grad_w_gate", h3, [dgate], N_CHIPS, WIRE_DTYPE)
    (d_w_up,) = _matmul_tn_cols("grad_w_up", h3, [dup], N_CHIPS, WIRE_DTYPE)
    ffn_names = ("w_gate", "w_up", "w_down")
    ffn_swap = _swap_start("swap_start_ffn", _as_shards([d_w_gate, d_w_up, d_w_down]))
    dh3 = _matmul_nt_cols("ffn_dh", [dgate, dup], [wg_gate, wg_up], jnp.float32, tn=1024, after=ffn_swap[-1])
    ffn_flight = _reduce_scatter_start("ffn", ffn_names, *_swap_wait("swap_wait_ffn", ffn_swap, dh3), pos)
    dx2, dx2_low, d_ffn_g = _rmsnorm_bwd("norm_ffn_bwd", dh3, x2, g_ffn, dx3, low)

    do = _matmul_nt("xattn_do", [dx2_low], [full["w_o"]], low, after=ffn_flight[-1])
    (d_w_o,) = _matmul_tn("grad_w_o", o, [dx2_low], WIRE_DTYPE)
    dq, dk, dvv = _attn_bwd(q, k, vv, do)
    (d_w_q,) = _matmul_tn("grad_w_q", h2, [dq], WIRE_DTYPE)
    dx1, dx1_low, d_xattn_g = _matmul_nt_norm_bwd("xattn_dh_norm_bwd", dq, full["w_q"], x1, g_xattn, dx2, low)
    d_w_k, d_w_v = _matmul_tn("grad_w_kv", mn, [dk, dvv], WIRE_DTYPE)
    dmn = _matmul_nt("xattn_dmem", [dk, dvv], [full["w_k"], full["w_v"]], jnp.float32)
    (d_mem_g,) = _rmsnorm_bwd("norm_mem_bwd", dmn, mems, g_mem, None, None)

    (d_w_out,) = _matmul_tn("grad_w_out", z, [dx1_low], WIRE_DTYPE)
    attn_names = ("w_q", "w_k", "w_v", "w_o", "w_out")
    attn_swap = _swap_start("swap_start_attn", _as_shards([d_w_q, d_w_k, d_w_v, d_w_o, d_w_out]))
    dz = _matmul_nt("mix_dz", [dx1_low], [full["w_out"]], jnp.float32, after=attn_swap[-1])
    attn_flight = _reduce_scatter_start("attn", attn_names, *_swap_wait("swap_wait_attn", attn_swap, dz), pos)
    dproj, d_pw, d_ps, d_gs, d_ws, d_b = _mixer_bwd(proj, dz, pw, pool_scale, sgu_norm_g, ws, b_full)
    (d_w_in,) = _matmul_tn_cols("grad_w_in", h1, [dproj], N_CHIPS, WIRE_DTYPE, tka=1024, after=attn_flight[-1])
    d_pool_w = d_pw.reshape(G, N_CHIPS, PG // N_CHIPS, PG).transpose(1, 0, 2, 3).reshape(N_CHIPS, PG, PG).astype(WIRE_DTYPE)
    mix_names = ("w_in", "pool_w")
    mix_flight = _reduce_begin("mix", mix_names, [d_w_in, d_pool_w], pos)
    grad_x, d_mix_g = _matmul_nt_norm_bwd("mix_dh_norm_bwd", dproj, wg_in, xs, g_mix, dx1, None, tm=512, after=mix_flight[-1])

    small_parts = {
        "norm_mix_g": d_mix_g, "pool_scale": d_ps, "sgu_norm_g": d_gs, "w_spatial": d_ws, "b_spatial": d_b,
        "norm_xattn_g": d_xattn_g, "norm_mem_g": d_mem_g, "norm_ffn_g": d_ffn_g, "final_norm_g": d_final_g,
    }
    row_counts = [small_parts[n].size // LANES for n in SMALL]
    packed = jnp.concatenate([_as_lanes(small_parts[n]) for n in SMALL], axis=0)
    device = _device_index(lax.axis_index("x"), lax.axis_index("y"), lax.axis_index("c"))
    slots = lax.dynamic_update_slice(jnp.zeros((N_DEV,) + packed.shape, jnp.float32), packed[None], (device, 0, 0))
    small_flight = _exchange_start("small_grads_start", slots)

    groups = (("ffn", ffn_names, ffn_flight), ("attn", attn_names, attn_flight), ("mix", mix_names, mix_flight))
    joins, after = {}, small_flight[-1]
    for tag, names, flight in groups:
        reduced = _reduce_end(tag, names, flight, after, pos)
        joins[tag] = _join_start("join_start_" + tag, [reduced[n] for n in names])
        after = joins[tag][-1]
    small_sum = _sum_slots("small_grads_sum", _exchange_wait("small_grads_wait", small_flight, after))

    out_grad, out_delta, out_m, out_v = {}, {}, {}, {}
    after = small_sum
    for tag, names, _ in groups:
        grads = [g.reshape(-1, g.shape[-1]) for g in _join_wait("join_wait_" + tag, joins[tag], after)]
        updates = _per_run(_adamw, names, [_as2d(weights[n]) for n in names], grads,
                           [_as2d(moments_m[n]) for n in names], [_as2d(moments_v[n]) for n in names])
        for n, update in zip(names, updates):
            shape = weights[n].shape
            out_grad[n], out_delta[n], out_m[n], out_v[n] = (a.reshape(shape) for a in update)
        after = updates[-1][1]
    small_out = _adamw_small(
        small_sum, [_as_lanes(weights[n]) for n in SMALL], [_as_lanes(moments_m[n]) for n in SMALL],
        [_as_lanes(moments_v[n]) for n in SMALL], row_counts)
    for t, n in enumerate(SMALL):
        shape = weights[n].shape
        out_grad[n], out_delta[n], out_m[n], out_v[n] = (a.reshape(shape) for a in small_out[4 * t:4 * t + 4])

    loss = lax.psum(loss_part[0, 0], ("x", "y", "c"))
    return (loss, grad_x[None], *[out_grad[n] for n in WEIGHTS], *[out_delta[n] for n in WEIGHTS],
            *[out_m[n] for n in WEIGHTS], *[out_v[n] for n in WEIGHTS])
```

```python
import math

import jax
import jax.numpy as jnp
from jax import lax
from jax.experimental import pallas as pl
from jax.experimental.pallas import tpu as pltpu

MXU_DTYPE = jnp.bfloat16
WIRE_DTYPE = jnp.bfloat16

EPS = 1e-6
CHUNK = 64
POOL_WINDOWS = (2, 4, 8, 16)
POOL_HALO = 16
N_XATTN_HEADS = 4
ADAM_LR = 0.001
ADAM_B1 = 0.9
ADAM_B2 = 0.999
ADAM_EPS = 1e-08
ADAM_WD = 0.01
ADAM_STEP = 10

N_CHIPS = 4
N_DEV = 8
LANES = 128
VMEM_LIMIT_BYTES = 56 * 2 ** 20
COPY_CHUNK_BYTES = 512 * 2 ** 10
COPY_ROW_ALIGN = 16

MESH = pl.DeviceIdType.MESH
ANY = pl.BlockSpec(memory_space=pl.ANY)
VMEM_WHOLE = pl.BlockSpec(memory_space=pltpu.VMEM)
HBM_SPEC = pl.BlockSpec(memory_space=pltpu.HBM)
SEM_SPEC = pl.BlockSpec(memory_space=pltpu.SEMAPHORE)
SIDE_EFFECT = pltpu.SideEffectType.DATAFLOW_SIDE_EFFECTING

NN = (((1,), (0,)), ((), ()))
NT = (((1,), (1,)), ((), ()))
TN = (((0,), (0,)), ((), ()))


def _tile(n, target, mult):
    best = None
    for t in range(mult, min(n, target) + 1, mult):
        if n % t == 0:
            best = t
    return n if best is None else best


def _params(n_grid_axes):
    return pltpu.CompilerParams(dimension_semantics=("arbitrary",) * n_grid_axes, vmem_limit_bytes=VMEM_LIMIT_BYTES)


def _rsqrt_mean_sq(x):
    return lax.rsqrt(jnp.mean(x * x, axis=-1, keepdims=True) + EPS)


def _sum_all(x):
    return jnp.sum(jnp.sum(x, axis=1, keepdims=True), axis=0, keepdims=True)


def _sigmoid(x):
    return 0.5 * jnp.tanh(0.5 * x) + 0.5


def _store_epilogue(accs, ex_refs, out_refs, i, j):
    for a, o in zip(accs, out_refs):
        o[...] = a.astype(o.dtype)


def _fused_matmul(name, grid, operands, pairs, acc_shapes, extras, outs, epilogue):
    nk = grid[2]
    n_op, n_ex, n_out, n_acc = len(operands), len(extras), len(outs), len(acc_shapes)
    in_place = nk > 1 and epilogue is _store_epilogue and all(o[1] == jnp.float32 for o in outs)

    def body(*refs):
        op_refs = refs[:n_op]
        ex_refs = refs[n_op:n_op + n_ex]
        out_refs = refs[n_op + n_ex:n_op + n_ex + n_out]
        acc_refs = out_refs if in_place else refs[n_op + n_ex + n_out:]
        i, j, k = pl.program_id(0), pl.program_id(1), pl.program_id(2)
        loaded = {}

        def operand(n):
            if n not in loaded:
                loaded[n] = op_refs[n][...].astype(MXU_DTYPE)
            return loaded[n]

        def product(ia, ib, dims):
            return lax.dot_general(operand(ia), operand(ib), dims, preferred_element_type=jnp.float32)

        def products():
            parts = [None] * n_acc
            for ia, ib, iacc, dims in pairs:
                d = product(ia, ib, dims)
                parts[iacc] = d if parts[iacc] is None else parts[iacc] + d
            return parts

        if nk == 1:
            epilogue(products(), ex_refs, out_refs, i, j)
            return

        @pl.when(k == 0)
        def _():
            for a in range(n_acc):
                acc_refs[a][...] = jnp.zeros_like(acc_refs[a])

        for a, part in enumerate(products()):
            acc_refs[a][...] += part

        if not in_place:
            @pl.when(k == nk - 1)
            def _():
                epilogue([acc_refs[a][...] for a in range(n_acc)], ex_refs, out_refs, i, j)

    arrays = [o[0] for o in operands] + [e[0] for e in extras]
    in_specs = [pl.BlockSpec(o[1], o[2]) for o in operands]
    in_specs += [ANY if e[1] is None else pl.BlockSpec(e[1], e[2]) for e in extras]
    return pl.pallas_call(
        body,
        name=name,
        grid=grid,
        in_specs=in_specs,
        out_specs=[pl.BlockSpec(o[2], o[3]) for o in outs],
        out_shape=[jax.ShapeDtypeStruct(o[0], o[1]) for o in outs],
        scratch_shapes=[pltpu.VMEM(s, jnp.float32) for s in acc_shapes] if nk > 1 and not in_place else [],
        compiler_params=_params(3),
    )(*arrays)


def _matmul_nn(name, a, b, out_dtype, tm=1024, tn=1024):
    M, K = a.shape
    N = b.shape[1]
    tm, tn = _tile(M, tm, 16), _tile(N, tn, LANES)
    return _fused_matmul(
        name, (M // tm, N // tn, 1),
        [(a, (tm, K), lambda i, j, k: (i, 0)), (b, (K, tn), lambda i, j, k: (0, j))],
        [(0, 1, 0, NN)], [(tm, tn)], [],
        [((M, N), out_dtype, (tm, tn), lambda i, j, k: (i, j))], _store_epilogue)[0]


def _matmul_nn_cols(name, a, b3, out_dtype, tm=1024):
    M, K = a.shape
    J, _, Nc = b3.shape
    tm = _tile(M, tm, 16)
    return _fused_matmul(
        name, (M // tm, J, 1),
        [(a, (tm, K), lambda i, j, k: (i, 0)), (b3, (None, K, Nc), lambda i, j, k: (j, 0, 0))],
        [(0, 1, 0, NN)], [(tm, Nc)], [],
        [((M, J * Nc), out_dtype, (tm, Nc), lambda i, j, k: (i, j))], _store_epilogue)[0]


def _order_after(after):
    return [] if after is None else [(after, None, None)]


def _matmul_nt(name, a_list, b_list, out_dtype, tm=1024, tn=1024, after=None):
    M, K = a_list[0].shape
    N = b_list[0].shape[0]
    tm, tn = _tile(M, tm, 16), _tile(N, tn, LANES)
    n = len(a_list)
    operands = [(a, (tm, K), lambda i, j, k: (i, 0)) for a in a_list]
    operands += [(b, (tn, K), lambda i, j, k: (j, 0)) for b in b_list]
    return _fused_matmul(
        name, (M // tm, N // tn, 1), operands,
        [(p, n + p, 0, NT) for p in range(n)], [(tm, tn)], _order_after(after),
        [((M, N), out_dtype, (tm, tn), lambda i, j, k: (i, j))], _store_epilogue)[0]


def _matmul_nt_cols(name, a_list, b3_list, out_dtype, tm=1024, tn=512, after=None):
    M = a_list[0].shape[0]
    J, N, Kc = b3_list[0].shape
    tm, tn = _tile(M, tm, 16), _tile(N, tn, LANES)
    n = len(a_list)
    operands = [(a, (tm, Kc), lambda i, j, k: (i, k)) for a in a_list]
    operands += [(b, (None, tn, Kc), lambda i, j, k: (k, j, 0)) for b in b3_list]
    return _fused_matmul(
        name, (M // tm, N // tn, J), operands,
        [(p, n + p, 0, NT) for p in range(n)], [(tm, tn)], _order_after(after),
        [((M, N), out_dtype, (tm, tn), lambda i, j, k: (i, j))], _store_epilogue)[0]


def _matmul_tn(name, a, b_list, out_dtype, tka=1024, tn=1024, tkm=4096):
    M, Ka = a.shape
    N = b_list[0].shape[1]
    tka, tn, tkm = _tile(Ka, tka, LANES), _tile(N, tn, LANES), _tile(M, tkm, 16)
    n = len(b_list)
    operands = [(a, (tkm, tka), lambda i, j, k: (k, i))]
    operands += [(b, (tkm, tn), lambda i, j, k: (k, j)) for b in b_list]
    return _fused_matmul(
        name, (Ka // tka, N // tn, M // tkm), operands,
        [(0, 1 + p, p, TN) for p in range(n)], [(tka, tn)] * n, [],
        [((Ka, N), out_dtype, (tka, tn), lambda i, j, k: (i, j))] * n, _store_epilogue)


def _matmul_tn_cols(name, a, b_list, n_shards, out_dtype, tka=512, tkm=4096, after=None):
    M, Ka = a.shape
    Nc = b_list[0].shape[1] // n_shards
    tka, tkm = _tile(Ka, tka, LANES), _tile(M, tkm, 16)
    n = len(b_list)
    operands = [(a, (tkm, tka), lambda i, j, k: (k, i))]
    operands += [(b, (tkm, Nc), lambda i, j, k: (k, j)) for b in b_list]
    return _fused_matmul(
        name, (Ka // tka, n_shards, M // tkm), operands,
        [(0, 1 + p, p, TN) for p in range(n)], [(tka, Nc)] * n, _order_after(after),
        [((n_shards, Ka, Nc), out_dtype, (None, tka, Nc), lambda i, j, k: (j, i, 0))] * n, _store_epilogue)


def _rmsnorm(name, x, g_row, out_dtype, tm=512, after=None):
    S, D = x.shape
    tm = _tile(S, tm, 16)

    def body(x_ref, g_ref, *rest):
        o_ref = rest[-1]
        xv = x_ref[...]
        o_ref[...] = (xv * _rsqrt_mean_sq(xv) * g_ref[...]).astype(o_ref.dtype)

    ordered = [] if after is None else [after]
    return pl.pallas_call(
        body, name=name, grid=(S // tm,),
        in_specs=[pl.BlockSpec((tm, D), lambda i: (i, 0)), pl.BlockSpec((1, D), lambda i: (0, 0))] + [ANY] * len(ordered),
        out_specs=pl.BlockSpec((tm, D), lambda i: (i, 0)),
        out_shape=jax.ShapeDtypeStruct((S, D), out_dtype),
        compiler_params=_params(1),
    )(x, g_row, *ordered)


def _rmsnorm_bwd_values(dh, xin, g):
    r = _rsqrt_mean_sq(xin)
    xhat = xin * r
    dhg = dh * g
    dx = r * (dhg - xhat * jnp.mean(dhg * xhat, axis=-1, keepdims=True))
    return dx, jnp.sum(dh * xhat, axis=0, keepdims=True)


def _rmsnorm_bwd(name, dh, xin, g_row, dres, low_dtype, tm=256):
    S, D = xin.shape
    tm = _tile(S, tm, 16)
    has_dx = dres is not None
    has_low = has_dx and low_dtype is not None

    def body(*refs):
        dh_ref, x_ref, g_ref = refs[:3]
        rest = refs[3:]
        dx, dg = _rmsnorm_bwd_values(dh_ref[...], x_ref[...], g_ref[...])
        if has_dx:
            dres_ref, rest = rest[0], rest[1:]
            dx = dx + dres_ref[...]
            rest[0][...] = dx
            if has_low:
                rest[1][...] = dx.astype(low_dtype)
        dg_ref = rest[-1]

        @pl.when(pl.program_id(0) == 0)
        def _():
            dg_ref[...] = jnp.zeros_like(dg_ref)

        dg_ref[...] += dg

    row = pl.BlockSpec((tm, D), lambda i: (i, 0))
    vec = pl.BlockSpec((1, D), lambda i: (0, 0))
    ins, in_specs = [dh, xin, g_row], [row, row, vec]
    out_shape, out_specs = [], []
    if has_dx:
        ins.append(dres)
        in_specs.append(row)
        out_shape.append(jax.ShapeDtypeStruct((S, D), jnp.float32))
        out_specs.append(row)
        if has_low:
            out_shape.append(jax.ShapeDtypeStruct((S, D), low_dtype))
            out_specs.append(row)
    out_shape.append(jax.ShapeDtypeStruct((1, D), jnp.float32))
    out_specs.append(vec)
    return pl.pallas_call(
        body, name=name, grid=(S // tm,), in_specs=in_specs, out_specs=out_specs, out_shape=out_shape,
        compiler_params=_params(1),
    )(*ins)


def _chunk_mask(blk):
    t = lax.broadcasted_iota(jnp.int32, (blk, blk), 0)
    s = lax.broadcasted_iota(jnp.int32, (blk, blk), 1)
    return (s // CHUNK) <= (t // CHUNK)


def _pool_inputs(a, halo, row0, tm, g, pg):
    cols = slice(g * pg, (g + 1) * pg)
    w = POOL_WINDOWS[g]
    s = jnp.concatenate([halo[:, cols], a[:, cols]], axis=0)
    span = 1
    while span < w:
        s = s + pltpu.roll(s, span, axis=0)
        span *= 2
    t = lax.broadcasted_iota(jnp.int32, (tm, 1), 0) + row0
    cnt = jnp.minimum(t + 1, w).astype(jnp.float32)
    return s[POOL_HALO:] / cnt - a[:, cols]


def _mixer_specs(S, tm, DP, DS):
    per_halo = tm // POOL_HALO
    n_halo = S // POOL_HALO
    a_spec = pl.BlockSpec((tm, DP), lambda i: (i, 0))
    prev_spec = pl.BlockSpec((POOL_HALO, DP), lambda i: (jnp.maximum(i * per_halo - 1, 0), 0))
    next_spec = pl.BlockSpec((POOL_HALO, DP), lambda i: (jnp.minimum((i + 1) * per_halo, n_halo - 1), 0))
    u_spec = pl.BlockSpec((tm, DS), lambda i: (i, 1))
    v_spec = pl.BlockSpec((tm, DS), lambda i: (i, 2))
    return a_spec, prev_spec, next_spec, u_spec, v_spec


def _whole(arr):
    nd = arr.ndim
    return pl.BlockSpec(arr.shape, lambda i: (0,) * nd)


def _mixer_fwd(proj, pw, ps_row, gs_row, ws, b_full, tm=512):
    S = proj.shape[0]
    G, PG, _ = pw.shape
    DP = G * PG
    H, BLK, _ = ws.shape
    DS = gs_row.shape[1]
    HD = DS // H
    assert DP == DS and proj.shape[1] == DP + 2 * DS
    tm = _tile(S, tm, BLK)

    def body(a_ref, prev_ref, u_ref, v_ref, pw_ref, ps_ref, gs_ref, ws_ref, b_ref, z_ref):
        i = pl.program_id(0)
        a = a_ref[...]
        halo = jnp.where(i > 0, prev_ref[...], 0.0)
        for g in range(G):
            cols = slice(g * PG, (g + 1) * PG)
            p = _pool_inputs(a, halo, i * tm, tm, g, PG)
            y = jnp.dot(p.astype(MXU_DTYPE), pw_ref[g], preferred_element_type=jnp.float32) * ps_ref[:, cols]
            z_ref[:, cols] = y.astype(z_ref.dtype)
        v = v_ref[...]
        vn = (v * _rsqrt_mean_sq(v) * gs_ref[...]).astype(MXU_DTYPE)
        u = u_ref[...]
        mask = _chunk_mask(BLK)
        for h in range(H):
            wm = jnp.where(mask, ws_ref[h], 0.0).astype(MXU_DTYPE)
            bias = b_ref[h]
            cols = slice(h * HD, (h + 1) * HD)
            for n in range(tm // BLK):
                rows = slice(n * BLK, (n + 1) * BLK)
                mixed = jnp.dot(wm, vn[rows, cols], preferred_element_type=jnp.float32) + bias
                z_ref[rows, DP + h * HD:DP + (h + 1) * HD] = (u[rows, cols] * mixed).astype(z_ref.dtype)

    a_spec, prev_spec, _, u_spec, v_spec = _mixer_specs(S, tm, DP, DS)
    return pl.pallas_call(
        body, name="mixer_fwd", grid=(S // tm,),
        in_specs=[a_spec, prev_spec, u_spec, v_spec, _whole(pw), _whole(ps_row), _whole(gs_row), _whole(ws), _whole(b_full)],
        out_specs=pl.BlockSpec((tm, DP + DS), lambda i: (i, 0)),
        out_shape=jax.ShapeDtypeStruct((S, DP + DS), MXU_DTYPE),
        compiler_params=_params(1),
    )(proj, proj, proj, proj, pw, ps_row, gs_row, ws, b_full)


def _mixer_bwd(proj, dz, pw, ps_row, gs_row, ws, b_full, tm=256):
    S = proj.shape[0]
    G, PG, _ = pw.shape
    DP = G * PG
    H, BLK, _ = ws.shape
    DS = gs_row.shape[1]
    HD = DS // H
    tm = _tile(S, tm, BLK)
    nb = S // tm
    ext = tm + POOL_HALO

    def body(a_ref, prev_ref, u_ref, v_ref, dzp_ref, dzp_next_ref, dzs_ref, pw_ref, ps_ref, gs_ref, ws_ref, b_ref,
             dproj_ref, dpw_ref, dps_ref, dgs_ref, dws_ref, db_ref, dvn_ref):
        i = pl.program_id(0)

        @pl.when(i == 0)
        def _():
            for r in (dpw_ref, dps_ref, dgs_ref, dws_ref, db_ref):
                r[...] = jnp.zeros_like(r)

        a = a_ref[...]
        halo = jnp.where(i > 0, prev_ref[...], 0.0)
        dyp = dzp_ref[...]
        dyp_next = jnp.where(i < nb - 1, dzp_next_ref[...], 0.0)
        dq_ext = jnp.concatenate([dyp, dyp_next], axis=0) * ps_ref[...]
        t_ext = lax.broadcasted_iota(jnp.int32, (ext, 1), 0) + i * tm
        for g in range(G):
            cols = slice(g * PG, (g + 1) * PG)
            w = POOL_WINDOWS[g]
            pb = _pool_inputs(a, halo, i * tm, tm, g, PG).astype(MXU_DTYPE)
            y_pre = jnp.dot(pb, pw_ref[g], preferred_element_type=jnp.float32)
            dps_ref[:, cols] += jnp.sum(dyp[:, cols] * y_pre, axis=0, keepdims=True)
            dqb = dq_ext[:, cols].astype(MXU_DTYPE)
            dpw_ref[g] += lax.dot_general(pb, dqb[:tm], TN, preferred_element_type=jnp.float32)
            dp = lax.dot_general(dqb, pw_ref[g], NT, preferred_element_type=jnp.float32)
            f = dp / jnp.minimum(t_ext + 1, w).astype(jnp.float32)
            span = 1
            while span < w:
                f = f + pltpu.roll(f, ext - span, axis=0)
                span *= 2
            dproj_ref[:, cols] = (f[:tm] - dp[:tm]).astype(dproj_ref.dtype)

        v = v_ref[...]
        rv = _rsqrt_mean_sq(v)
        vhat = v * rv
        gs = gs_ref[...]
        vnb = (vhat * gs).astype(MXU_DTYPE)
        u = u_ref[...]
        dys = dzs_ref[...]
        mask = _chunk_mask(BLK)
        for h in range(H):
            wm = jnp.where(mask, ws_ref[h], 0.0).astype(MXU_DTYPE)
            bias = b_ref[h]
            cols = slice(h * HD, (h + 1) * HD)
            dm_sum = jnp.zeros((BLK, HD), jnp.float32)
            dws_sum = jnp.zeros((BLK, BLK), jnp.float32)
            for n in range(tm // BLK):
                rows = slice(n * BLK, (n + 1) * BLK)
                vblk = vnb[rows, cols]
                mixed = jnp.dot(wm, vblk, preferred_element_type=jnp.float32) + bias
                dy = dys[rows, cols]
                dproj_ref[rows, DP + h * HD:DP + (h + 1) * HD] = (dy * mixed).astype(dproj_ref.dtype)
                dm = dy * u[rows, cols]
                dm_sum = dm_sum + dm
                dmb = dm.astype(MXU_DTYPE)
                dws_sum = dws_sum + lax.dot_general(dmb, vblk, NT, preferred_element_type=jnp.float32)
                dvn_ref[rows, cols] = lax.dot_general(wm, dmb, TN, preferred_element_type=jnp.float32)
            dws_ref[h] += jnp.where(mask, dws_sum, 0.0)
            db_ref[h:h + 1, :] += jnp.sum(dm_sum.T, axis=0, keepdims=True)
        dvn = dvn_ref[...]
        dvg = dvn * gs
        dv = rv * (dvg - vhat * jnp.mean(dvg * vhat, axis=-1, keepdims=True))
        dgs_ref[...] += jnp.sum(dvn * vhat, axis=0, keepdims=True)
        dproj_ref[:, DP + DS:] = dv.astype(dproj_ref.dtype)

    a_spec, prev_spec, next_spec, u_spec, v_spec = _mixer_specs(S, tm, DP, DS)
    dzp_spec = pl.BlockSpec((tm, DP), lambda i: (i, 0))
    dzs_spec = pl.BlockSpec((tm, DS), lambda i: (i, 1))
    small = [pw, ps_row, gs_row, ws, b_full]
    out_shape = [
        jax.ShapeDtypeStruct((S, DP + 2 * DS), MXU_DTYPE),
        jax.ShapeDtypeStruct(pw.shape, jnp.float32),
        jax.ShapeDtypeStruct(ps_row.shape, jnp.float32),
        jax.ShapeDtypeStruct(gs_row.shape, jnp.float32),
        jax.ShapeDtypeStruct(ws.shape, jnp.float32),
        jax.ShapeDtypeStruct((H, BLK), jnp.float32),
    ]
    out_specs = [pl.BlockSpec((tm, DP + 2 * DS), lambda i: (i, 0))] + [_whole(s) for s in out_shape[1:]]
    return pl.pallas_call(
        body, name="mixer_bwd", grid=(nb,),
        in_specs=[a_spec, prev_spec, u_spec, v_spec, dzp_spec, next_spec, dzs_spec] + [_whole(s) for s in small],
        out_specs=out_specs, out_shape=out_shape,
        scratch_shapes=[pltpu.VMEM((tm, DS), jnp.float32)],
        compiler_params=_params(1),
    )(proj, proj, proj, proj, dz, dz, dz, *small)


def _softmax_rows(q, k, scale):
    s = lax.dot_general(q, k, NT, preferred_element_type=jnp.float32) * scale
    e = jnp.exp(s - jnp.max(s, axis=-1, keepdims=True))
    return e / jnp.sum(e, axis=-1, keepdims=True)


def _attn_fwd(q, k, v, tm=512):
    S, D = q.shape
    M = k.shape[0]
    HD = D // N_XATTN_HEADS
    scale = HD ** -0.5
    tm = _tile(S, tm, 16)

    def body(q_ref, k_ref, v_ref, o_ref):
        for h in range(N_XATTN_HEADS):
            cols = slice(h * HD, (h + 1) * HD)
            p = _softmax_rows(q_ref[:, cols], k_ref[:, cols], scale)
            o = jnp.dot(p.astype(MXU_DTYPE), v_ref[:, cols], preferred_element_type=jnp.float32)
            o_ref[:, cols] = o.astype(o_ref.dtype)

    row = pl.BlockSpec((tm, D), lambda i: (i, 0))
    mem = pl.BlockSpec((M, D), lambda i: (0, 0))
    return pl.pallas_call(
        body, name="attn_fwd", grid=(S // tm,), in_specs=[row, mem, mem], out_specs=row,
        out_shape=jax.ShapeDtypeStruct((S, D), MXU_DTYPE), compiler_params=_params(1),
    )(q, k, v)


def _attn_bwd(q, k, v, do, tm=512):
    S, D = q.shape
    M = k.shape[0]
    HD = D // N_XATTN_HEADS
    scale = HD ** -0.5
    tm = _tile(S, tm, 16)

    def body(q_ref, k_ref, v_ref, do_ref, dq_ref, dk_ref, dv_ref):
        @pl.when(pl.program_id(0) == 0)
        def _():
            dk_ref[...] = jnp.zeros_like(dk_ref)
            dv_ref[...] = jnp.zeros_like(dv_ref)

        for h in range(N_XATTN_HEADS):
            cols = slice(h * HD, (h + 1) * HD)
            qh, kh, vh, doh = q_ref[:, cols], k_ref[:, cols], v_ref[:, cols], do_ref[:, cols]
            p = _softmax_rows(qh, kh, scale)
            dp = lax.dot_general(doh, vh, NT, preferred_element_type=jnp.float32)
            dv_ref[:, cols] += lax.dot_general(p.astype(MXU_DTYPE), doh, TN, preferred_element_type=jnp.float32)
            ds = (p * (dp - jnp.sum(dp * p, axis=-1, keepdims=True)) * scale).astype(MXU_DTYPE)
            dq_ref[:, cols] = jnp.dot(ds, kh, preferred_element_type=jnp.float32).astype(dq_ref.dtype)
            dk_ref[:, cols] += lax.dot_general(ds, qh, TN, preferred_element_type=jnp.float32)

    row = pl.BlockSpec((tm, D), lambda i: (i, 0))
    mem = pl.BlockSpec((M, D), lambda i: (0, 0))
    return pl.pallas_call(
        body, name="attn_bwd", grid=(S // tm,), in_specs=[row, mem, mem, row], out_specs=[row, mem, mem],
        out_shape=[jax.ShapeDtypeStruct((S, D), MXU_DTYPE), jax.ShapeDtypeStruct((M, D), jnp.float32),
                   jax.ShapeDtypeStruct((M, D), jnp.float32)],
        compiler_params=_params(1),
    )(q, k, v, do)


def _proj_residual_norm(name, a, w2d, resid, g_row, tm=256, after=None):
    M, K = a.shape
    N = w2d.shape[1]
    tm = _tile(M, tm, 16)

    def epilogue(accs, ex_refs, out_refs, i, j):
        x_new = ex_refs[0][...] + accs[0]
        out_refs[0][...] = x_new
        out_refs[1][...] = (x_new * _rsqrt_mean_sq(x_new) * ex_refs[1][...]).astype(MXU_DTYPE)

    row = lambda i, j, k: (i, 0)
    return _fused_matmul(
        name, (M // tm, 1, 1),
        [(a, (tm, K), row), (w2d, (K, N), lambda i, j, k: (0, 0))],
        [(0, 1, 0, NN)], [(tm, N)],
        [(resid, (tm, N), row), (g_row, (1, N), lambda i, j, k: (0, 0))] + _order_after(after),
        [((M, N), jnp.float32, (tm, N), row), ((M, N), MXU_DTYPE, (tm, N), row)], epilogue)


def _ffn_gate_up(h, wg3, wu3, tm=512):
    M, K = h.shape
    J, _, Nc = wg3.shape
    tm = _tile(M, tm, 16)

    def epilogue(accs, ex_refs, out_refs, i, j):
        gate, up = accs
        sg = _sigmoid(gate)
        silu = gate * sg
        out_refs[0][...] = (silu * up).astype(MXU_DTYPE)
        out_refs[1][...] = (up * (sg * (1.0 + gate * (1.0 - sg)))).astype(MXU_DTYPE)
        out_refs[2][...] = silu.astype(MXU_DTYPE)

    col = lambda i, j, k: (i, j)
    wspec = lambda i, j, k: (j, 0, 0)
    return _fused_matmul(
        "ffn_gate_up", (M // tm, J, 1),
        [(h, (tm, K), lambda i, j, k: (i, 0)), (wg3, (None, K, Nc), wspec), (wu3, (None, K, Nc), wspec)],
        [(0, 1, 0, NN), (0, 2, 1, NN)], [(tm, Nc)] * 2, [],
        [((M, J * Nc), MXU_DTYPE, (tm, Nc), col)] * 3, epilogue)


def _ffn_down(act, wd2d, x2, tm=512, tn=1024):
    M, F = act.shape
    D = wd2d.shape[1]
    tm, tn = _tile(M, tm, 16), _tile(D, tn, LANES)

    def epilogue(accs, ex_refs, out_refs, i, j):
        out_refs[0][...] = ex_refs[0][...] + accs[0]

    tile = lambda i, j, k: (i, j)
    return _fused_matmul(
        "ffn_down", (M // tm, D // tn, 1),
        [(act, (tm, F), lambda i, j, k: (i, 0)), (wd2d, (F, tn), lambda i, j, k: (0, j))],
        [(0, 1, 0, NN)], [(tm, tn)], [(x2, (tm, tn), tile)],
        [((M, D), jnp.float32, (tm, tn), tile)], epilogue)[0]


def _final_norm_loss(x3, target, g_row, tm=256):
    S, D = x3.shape
    tm = _tile(S, tm, 16)

    def body(x_ref, t_ref, g_ref, dx_ref, dxl_ref, loss_ref, dg_ref):
        x3v = x_ref[...]
        g = g_ref[...]
        r = _rsqrt_mean_sq(x3v)
        xhat = x3v * r
        diff = xhat * g - t_ref[...]
        dy = diff / D
        dyg = dy * g
        dx = r * (dyg - xhat * jnp.mean(dyg * xhat, axis=-1, keepdims=True))
        dx_ref[...] = dx
        dxl_ref[...] = dx.astype(MXU_DTYPE)

        @pl.when(pl.program_id(0) == 0)
        def _():
            loss_ref[...] = jnp.zeros_like(loss_ref)
            dg_ref[...] = jnp.zeros_like(dg_ref)

        loss_ref[...] += 0.5 * _sum_all(jnp.mean(diff * diff, axis=-1, keepdims=True))
        dg_ref[...] += jnp.sum(dy * xhat, axis=0, keepdims=True)

    row = pl.BlockSpec((tm, D), lambda i: (i, 0))
    vec = pl.BlockSpec((1, D), lambda i: (0, 0))
    return pl.pallas_call(
        body, name="final_norm_loss", grid=(S // tm,), in_specs=[row, row, vec],
        out_specs=[row, row, pl.BlockSpec((1, 1), lambda i: (0, 0)), vec],
        out_shape=[jax.ShapeDtypeStruct((S, D), jnp.float32), jax.ShapeDtypeStruct((S, D), MXU_DTYPE),
                   jax.ShapeDtypeStruct((1, 1), jnp.float32), jax.ShapeDtypeStruct((1, D), jnp.float32)],
        compiler_params=_params(1),
    )(x3, target, g_row)


def _ffn_down_bwd(dx3_low, wd2d, act_by_gate, act_by_up, n_shards, tm=512):
    M, D = dx3_low.shape
    F = wd2d.shape[0]
    tn = F // n_shards
    tm = _tile(M, tm, 16)

    def epilogue(accs, ex_refs, out_refs, i, j):
        dact = accs[0]
        out_refs[0][...] = (dact * ex_refs[0][...].astype(jnp.float32)).astype(MXU_DTYPE)
        out_refs[1][...] = (dact * ex_refs[1][...].astype(jnp.float32)).astype(MXU_DTYPE)

    col = lambda i, j, k: (i, j)
    return _fused_matmul(
        "ffn_down_bwd", (M // tm, n_shards, 1),
        [(dx3_low, (tm, D), lambda i, j, k: (i, 0)), (wd2d, (tn, D), lambda i, j, k: (j, 0))],
        [(0, 1, 0, NT)], [(tm, tn)],
        [(act_by_gate, (tm, tn), col), (act_by_up, (tm, tn), col)],
        [((M, F), MXU_DTYPE, (tm, tn), col)] * 2, epilogue)


def _matmul_nt_norm_bwd(name, a, b, xin, g_row, dres, low_dtype, tm=256, after=None):
    M = a.shape[0]
    sharded = b.ndim == 3
    N, Kc = b.shape[-2:]
    J = b.shape[0] if sharded else 1
    tm = _tile(M, tm, 16)
    has_low = low_dtype is not None

    def epilogue(accs, ex_refs, out_refs, i, j):
        dx, dg = _rmsnorm_bwd_values(accs[0], ex_refs[0][...], ex_refs[1][...])
        dx = dx + ex_refs[2][...]
        out_refs[0][...] = dx
        if has_low:
            out_refs[1][...] = dx.astype(low_dtype)
        dg_ref = out_refs[-1]

        @pl.when(i == 0)
        def _():
            dg_ref[...] = jnp.zeros_like(dg_ref)

        dg_ref[...] += dg

    row = lambda i, j, k: (i, 0)
    const = lambda i, j, k: (0, 0)
    b_operand = (b, (None, N, Kc), lambda i, j, k: (k, 0, 0)) if sharded else (b, (N, Kc), const)
    outs = [((M, N), jnp.float32, (tm, N), row)]
    if has_low:
        outs.append(((M, N), low_dtype, (tm, N), row))
    outs.append(((1, N), jnp.float32, (1, N), const))
    return _fused_matmul(
        name, (M // tm, 1, J), [(a, (tm, Kc), lambda i, j, k: (i, k)), b_operand],
        [(0, 1, 0, NT)], [(tm, N)],
        [(xin, (tm, N), row), (g_row, (1, N), const), (dres, (tm, N), row)] + _order_after(after),
        outs, epilogue)


def _row_tile(R, C, target_bytes=2 ** 21):
    return _tile(R, max(16, target_bytes // (4 * C)), 16)


def _prefetch_call(body, name, grid, pos, in_specs, out_specs, out_shape, ins, after=None):
    ordered = [] if after is None else [after]
    grid_spec = pltpu.PrefetchScalarGridSpec(
        num_scalar_prefetch=1, grid=grid, in_specs=list(in_specs) + [ANY] * len(ordered), out_specs=out_specs)
    return pl.pallas_call(body, name=name, grid_spec=grid_spec, out_shape=out_shape,
                          compiler_params=_params(len(grid)))(pos, *ins, *ordered)


def _runs(arrays):
    runs = []
    for i, a in enumerate(arrays):
        if runs and arrays[runs[-1][-1]].shape == a.shape:
            runs[-1].append(i)
        else:
            runs.append([i])
    return runs


def _per_run(fn, names, *array_lists):
    out = [None] * len(names)
    for run in _runs(array_lists[0]):
        results = fn("_".join(names[i] for i in run), *[[arrays[i] for i in run] for arrays in array_lists])
        for i, r in zip(run, results):
            out[i] = r
    return out


def _cast_into_slabs(name, xs, dtype, pos, after=None):
    n = len(xs)
    R, C = xs[0].shape
    tr = _row_tile(R, C * n, 2 ** 22)

    def body(pos_ref, *refs):
        for t in range(n):
            refs[len(refs) - n + t][...] = refs[t][...].astype(dtype)

    return _prefetch_call(
        body, "cast_" + name, (R // tr,), pos, [pl.BlockSpec((tr, C), lambda i, pos: (i, 0))] * n,
        [pl.BlockSpec((None, tr, C), lambda i, pos: (pos[0], i, 0))] * n,
        [jax.ShapeDtypeStruct((N_CHIPS, R, C), dtype)] * n, xs, after)


def _chip_sum(name, partials, gots, dtype, pos):
    n = len(partials)
    J, R, C = partials[0].shape
    Rh = R // 2
    tr = _row_tile(Rh, C * n, 2 ** 22)

    def body(pos_ref, *refs):
        for t in range(n):
            refs[2 * n + t][...] = (refs[t][...].astype(jnp.float32) + refs[n + t][...].astype(jnp.float32)).astype(dtype)

    blk = pl.BlockSpec((None, tr, C), lambda j, i, pos: (j, i, 0))
    return _prefetch_call(
        body, "chip_sum_" + name, (J, Rh // tr), pos,
        [pl.BlockSpec((None, None, tr, C), lambda j, i, pos: (j, pos[1], i, 0))] * n + [blk] * n, [blk] * n,
        [jax.ShapeDtypeStruct((J, Rh, C), dtype)] * n, [p.reshape(J, 2, Rh, C) for p in partials] + list(gots))


def _owner_sum(name, chip_sums, others, pos):
    n = len(chip_sums)
    J, Rh, C = chip_sums[0].shape
    n_other = others[0].shape[0]
    tr = _row_tile(Rh, C * (n_other + 1) * n, 2 ** 22)

    def body(pos_ref, *refs):
        for t in range(n):
            acc = refs[t][...].astype(jnp.float32)
            for k in range(n_other):
                acc = acc + refs[n + t][k].astype(jnp.float32)
            refs[2 * n + t][...] = acc

    return _prefetch_call(
        body, "owner_sum_" + name, (Rh // tr,), pos,
        [pl.BlockSpec((None, tr, C), lambda i, pos: (pos[0], i, 0))] * n
        + [pl.BlockSpec((n_other, tr, C), lambda i, pos: (0, i, 0))] * n,
        [pl.BlockSpec((None, tr, C), lambda i, pos: (pos[1], i, 0))] * n,
        [jax.ShapeDtypeStruct((2, Rh, C), jnp.float32)] * n, list(chip_sums) + list(others))


def _adamw_values(w, g, m, v):
    m = ADAM_B1 * m + (1.0 - ADAM_B1) * g
    v = ADAM_B2 * v + (1.0 - ADAM_B2) * (g * g)
    m_hat = m / (1.0 - ADAM_B1 ** ADAM_STEP)
    v_hat = v / (1.0 - ADAM_B2 ** ADAM_STEP)
    delta = -ADAM_LR * (m_hat / (jnp.sqrt(v_hat) + ADAM_EPS) + ADAM_WD * w)
    return delta, m, v


def _adamw(name, ws, gs, ms, vs):
    n = len(ws)
    R, C = ws[0].shape
    tr = _row_tile(R, C * n, 2 ** 20)

    def body(*refs):
        w_refs, g_refs, m_refs, v_refs = (refs[q * n:(q + 1) * n] for q in range(4))
        outs = refs[4 * n:]
        for t in range(n):
            g = g_refs[t][...]
            outs[4 * t][...] = g
            outs[4 * t + 1][...], outs[4 * t + 2][...], outs[4 * t + 3][...] = _adamw_values(
                w_refs[t][...], g, m_refs[t][...], v_refs[t][...])

    blk = pl.BlockSpec((tr, C), lambda i: (i, 0))
    res = pl.pallas_call(
        body, name="adamw_" + name, grid=(R // tr,), in_specs=[blk] * (4 * n), out_specs=[blk] * (4 * n),
        out_shape=[jax.ShapeDtypeStruct((R, C), jnp.float32)] * (4 * n), compiler_params=_params(1),
    )(*ws, *gs, *ms, *vs)
    return [tuple(res[4 * t:4 * t + 4]) for t in range(n)]


def _adamw_small(g_packed, ws, ms, vs, row_counts):
    n = len(ws)

    def body(*refs):
        g_ref = refs[0]
        w_refs, m_refs, v_refs = refs[1:1 + n], refs[1 + n:1 + 2 * n], refs[1 + 2 * n:1 + 3 * n]
        outs = refs[1 + 3 * n:]
        off = 0
        for t in range(n):
            g = g_ref[off:off + row_counts[t], :]
            off += row_counts[t]
            delta, m_new, v_new = _adamw_values(w_refs[t][...], g, m_refs[t][...], v_refs[t][...])
            outs[4 * t][...] = g
            outs[4 * t + 1][...] = delta
            outs[4 * t + 2][...] = m_new
            outs[4 * t + 3][...] = v_new

    out_shape = []
    for t in range(n):
        out_shape += [jax.ShapeDtypeStruct((row_counts[t], LANES), jnp.float32)] * 4
    return pl.pallas_call(
        body, name="adamw_small", in_specs=[VMEM_WHOLE] * (1 + 3 * n), out_specs=[VMEM_WHOLE] * (4 * n),
        out_shape=out_shape, compiler_params=pltpu.CompilerParams(vmem_limit_bytes=VMEM_LIMIT_BYTES),
    )(g_packed, *ws, *ms, *vs)


def _position():
    x, y, c = lax.axis_index("x"), lax.axis_index("y"), lax.axis_index("c")
    other_chips = [(1 - x, y), (x, 1 - y), (1 - x, 1 - y)]
    return x, y, c, other_chips


def _half(rows, which):
    return pl.ds(which * (rows // 2), rows // 2)


def _chunk_index(shape, dtype):
    rows = shape[-2]
    total = math.prod(shape) * jnp.dtype(dtype).itemsize
    n = max(1, min(rows // COPY_ROW_ALIGN, total // COPY_CHUNK_BYTES))
    step = -(-rows // n)
    step = -(-step // COPY_ROW_ALIGN) * COPY_ROW_ALIGN
    lead = (slice(None),) * (len(shape) - 2)
    return [lead + (pl.ds(s, min(step, rows - s)),) for s in range(0, rows, step)]


def _start_in_chunks(make, src, dst, waited_here=True):
    for idx in _chunk_index(src.shape, src.dtype):
        make(src.at[idx], dst.at[idx]).start()
    return make(src, dst) if waited_here else None


def _remote_copier(send_sem, recv_sem, to):
    return lambda src, dst: pltpu.make_async_remote_copy(
        src_ref=src, dst_ref=dst, send_sem=send_sem, recv_sem=recv_sem, device_id=to, device_id_type=MESH)


def _slab_half(ref, chip, which):
    return ref.at[chip, _half(ref.shape[1], which)]


def _slab_quarter(ref, chip, which, q):
    rows = ref.shape[1] // 4
    return ref.at[chip, pl.ds(which * 2 * rows + q * rows, rows)]


def _gather_step(name, arrived=(), ring=(), direct=(), after=None):
    groups = [g[2] for g in arrived] + list(direct) + list(ring)
    per_tensor = [2] * len(arrived) + [3] * len(direct) + [2] * len(ring)
    flat = [a for g in groups for a in g]
    T, G, A = len(flat), len(groups), len(arrived)
    ordered = [] if after is None else [after]
    sems_in = [sem for g in arrived for sem in g[:2]]

    def body(*refs):
        ins = refs[:T]
        first_hop = refs[T:T + 2 * A]
        out_sems = refs[T + 2 * A + len(ordered):T + 2 * A + len(ordered) + 2 * G]
        token = refs[-1]
        x, y, c, other_chips = _position()
        mine = 2 * x + y
        x_nbr, y_nbr = (1 - x, y, c), (x, 1 - y, c)
        from_x, from_y = 2 * (1 - x) + y, 2 * x + (1 - y)
        t0 = 0
        for gi, group in enumerate(groups):
            send, recv = out_sems[2 * gi], out_sems[2 * gi + 1]
            for t in range(len(group)):
                ref = ins[t0 + t]
                own = _slab_half(ref, mine, c)
                if gi < A:
                    send1, recv1 = first_hop[2 * gi], first_hop[2 * gi + 1]
                    for k, (peer, chip) in enumerate(((x_nbr, from_x), (y_nbr, from_y))):
                        cp = _remote_copier(send1.at[2 * t + k], recv1.at[2 * t + k], peer)(own, _slab_half(ref, chip, c))
                        cp.wait_send()
                        cp.wait_recv()
                    for k, (peer, chip) in enumerate(((y_nbr, from_x), (x_nbr, from_y))):
                        piece = _slab_quarter(ref, chip, c, k)
                        _start_in_chunks(_remote_copier(send.at[2 * t + k], recv.at[2 * t + k], peer), piece, piece,
                                         waited_here=False)
                elif gi < A + len(direct):
                    for k, (cx, cy) in enumerate(other_chips):
                        _start_in_chunks(_remote_copier(send.at[3 * t + k], recv.at[3 * t + k], (cx, cy, c)), own, own,
                                         waited_here=False)
                else:
                    for k, peer in enumerate((x_nbr, y_nbr)):
                        _start_in_chunks(_remote_copier(send.at[2 * t + k], recv.at[2 * t + k], peer), own, own,
                                         waited_here=False)
            t0 += len(group)
        token[...] = jnp.zeros_like(token)

    sem_shapes = []
    for group, n in zip(groups, per_tensor):
        sem_shapes += [pltpu.SemaphoreType.DMA((n * len(group),))] * 2
    res = pl.pallas_call(
        body, name=name, in_specs=[HBM_SPEC] * T + [SEM_SPEC] * (2 * A) + [ANY] * len(ordered),
        out_specs=[SEM_SPEC] * (2 * G) + [HBM_SPEC] * T + [VMEM_WHOLE],
        out_shape=sem_shapes + [pltpu.HBM(a.shape, a.dtype) for a in flat] + [jax.ShapeDtypeStruct((8, LANES), jnp.float32)],
        input_output_aliases={t: 2 * G + t for t in range(T)},
        compiler_params=pltpu.CompilerParams(has_side_effects=SIDE_EFFECT),
    )(*[pltpu.with_memory_space_constraint(a, pltpu.HBM) for a in flat], *sems_in, *ordered)
    flights, t0 = [], 2 * G
    for gi, group in enumerate(groups):
        flights.append((res[2 * gi], res[2 * gi + 1], res[t0:t0 + len(group)]))
        t0 += len(group)
    return flights[:A], flights[A:A + len(direct)], flights[A + len(direct):], res[-1]


def _relayed_wait(name, flight, after):
    send_sems, recv_sems, slabs = flight
    T = len(slabs)

    def body(*refs):
        ins = refs[:T]
        send, recv = refs[T:T + 2]
        x, y, c, _ = _position()
        x_nbr, y_nbr = (1 - x, y, c), (x, 1 - y, c)
        from_x, from_y, diagonal = 2 * (1 - x) + y, 2 * x + (1 - y), 2 * (1 - x) + (1 - y)
        for t in range(T):
            for k, (peer, chip) in enumerate(((y_nbr, from_x), (x_nbr, from_y))):
                cp = _remote_copier(send.at[2 * t + k], recv.at[2 * t + k], peer)(
                    _slab_quarter(ins[t], chip, c, k), _slab_quarter(ins[t], diagonal, c, k))
                cp.wait_send()
                cp.wait_recv()

    return pl.pallas_call(
        body, name=name, in_specs=[HBM_SPEC] * T + [SEM_SPEC, SEM_SPEC, ANY], out_specs=[HBM_SPEC] * T,
        out_shape=[pltpu.HBM(a.shape, a.dtype) for a in slabs], input_output_aliases={t: t for t in range(T)},
        compiler_params=pltpu.CompilerParams(has_side_effects=SIDE_EFFECT),
    )(*slabs, send_sems, recv_sems, after)


def _gather_wait(name, flight, after):
    send_sems, recv_sems, slabs = flight
    T = len(slabs)

    def body(*refs):
        ins = refs[:T]
        send, recv = refs[T:T + 2]
        x, y, c, other_chips = _position()
        mine = 2 * x + y
        for t in range(T):
            own = _slab_half(ins[t], mine, c)
            for k, (cx, cy) in enumerate(other_chips):
                landed = _slab_half(ins[t], 2 * cx + cy, c)
                cp = _remote_copier(send.at[3 * t + k], recv.at[3 * t + k], (cx, cy, c))(own, landed)
                cp.wait_send()
                cp.wait_recv()

    return pl.pallas_call(
        body, name=name, in_specs=[HBM_SPEC] * T + [SEM_SPEC, SEM_SPEC, ANY], out_specs=[HBM_SPEC] * T,
        out_shape=[pltpu.HBM(s.shape, s.dtype) for s in slabs], input_output_aliases={t: t for t in range(T)},
        compiler_params=pltpu.CompilerParams(has_side_effects=SIDE_EFFECT),
    )(*slabs, send_sems, recv_sems, after)


def _share_with_sibling(name, slabs):
    T = len(slabs)

    def body(*refs):
        outs = refs[T:2 * T]
        send_sems, recv_sems = refs[2 * T:]
        x, y, c, other_chips = _position()
        sibling = (x, y, 1 - c)
        started = []
        for t in range(T):
            for k, (cx, cy) in enumerate(other_chips):
                landed = _slab_half(outs[t], 2 * cx + cy, c)
                started.append(_start_in_chunks(_remote_copier(send_sems.at[t, k], recv_sems.at[t, k], sibling), landed, landed))
        for t in range(T):
            for k, (cx, cy) in enumerate(other_chips):
                arriving = _slab_half(outs[t], 2 * cx + cy, 1 - c)
                _remote_copier(send_sems.at[t, k], recv_sems.at[t, k], sibling)(arriving, arriving).wait_recv()
        for cp in started:
            cp.wait_send()

    return pl.pallas_call(
        body, name=name, in_specs=[ANY] * T, out_specs=[ANY] * T,
        out_shape=[jax.ShapeDtypeStruct(s.shape, s.dtype) for s in slabs],
        input_output_aliases={t: t for t in range(T)},
        scratch_shapes=[pltpu.SemaphoreType.DMA((T, 3)), pltpu.SemaphoreType.DMA((T, 3))],
    )(*slabs)


def _swap_halves(name, grads):
    T = len(grads)

    def body(*refs):
        ins, got = refs[:T], refs[T:2 * T]
        send_sems, recv_sems = refs[2 * T:]
        x, y, c, _ = _position()
        remote = []
        for t in range(T):
            rows = ins[t].shape[1]
            remote.append(_start_in_chunks(
                _remote_copier(send_sems.at[t], recv_sems.at[t], (x, y, 1 - c)), ins[t].at[:, _half(rows, 1 - c)], got[t]))
        for cp in remote:
            cp.wait()

    return pl.pallas_call(
        body, name=name, in_specs=[ANY] * T, out_specs=[ANY] * T,
        out_shape=[jax.ShapeDtypeStruct((g.shape[0], g.shape[1] // 2, g.shape[2]), g.dtype) for g in grads],
        scratch_shapes=[pltpu.SemaphoreType.DMA((T,)), pltpu.SemaphoreType.DMA((T,))],
    )(*grads)


def _swap_start(name, grads):
    T = len(grads)
    lands = [lax.empty((g.shape[0], g.shape[1] // 2, g.shape[2]), g.dtype) for g in grads]

    def body(*refs):
        ins, got = refs[:T], refs[T:2 * T]
        send_sems, recv_sems = refs[2 * T:2 * T + 2]
        token = refs[-1]
        x, y, c, _ = _position()
        for t in range(T):
            _start_in_chunks(_remote_copier(send_sems.at[t], recv_sems.at[t], (x, y, 1 - c)),
                             ins[t].at[:, _half(ins[t].shape[1], 1 - c)], got[t], waited_here=False)
        token[...] = jnp.zeros_like(token)

    arrays = list(grads) + lands
    res = pl.pallas_call(
        body, name=name, in_specs=[HBM_SPEC] * (2 * T),
        out_specs=[SEM_SPEC, SEM_SPEC] + [HBM_SPEC] * (2 * T) + [VMEM_WHOLE],
        out_shape=[pltpu.SemaphoreType.DMA((T,)), pltpu.SemaphoreType.DMA((T,))]
        + [pltpu.HBM(a.shape, a.dtype) for a in arrays] + [jax.ShapeDtypeStruct((8, LANES), jnp.float32)],
        input_output_aliases={i: 2 + i for i in range(2 * T)},
        compiler_params=pltpu.CompilerParams(has_side_effects=SIDE_EFFECT),
    )(*[pltpu.with_memory_space_constraint(a, pltpu.HBM) for a in arrays])
    return res[0], res[1], res[2:2 + T], res[2 + T:2 + 2 * T], res[-1]


def _swap_wait(name, flight, after):
    send_sems, recv_sems, grads, lands, _ = flight
    T = len(grads)

    def body(*refs):
        ins, got = refs[:T], refs[T:2 * T]
        send, recv = refs[2 * T:2 * T + 2]
        x, y, c, _ = _position()
        for t in range(T):
            cp = _remote_copier(send.at[t], recv.at[t], (x, y, 1 - c))(ins[t].at[:, _half(ins[t].shape[1], 1 - c)], got[t])
            cp.wait_send()
            cp.wait_recv()

    arrays = list(grads) + list(lands)
    res = pl.pallas_call(
        body, name=name, in_specs=[HBM_SPEC] * (2 * T) + [SEM_SPEC, SEM_SPEC, ANY], out_specs=[HBM_SPEC] * (2 * T),
        out_shape=[pltpu.HBM(a.shape, a.dtype) for a in arrays], input_output_aliases={i: i for i in range(2 * T)},
        compiler_params=pltpu.CompilerParams(has_side_effects=SIDE_EFFECT),
    )(*arrays, send_sems, recv_sems, after)
    return res[:T], res[T:]


def _scatter_start(name, chip_sums):
    T = len(chip_sums)
    lands = [lax.empty((3,) + s.shape[1:], s.dtype) for s in chip_sums]

    def body(*refs):
        srcs, dsts = refs[:T], refs[T:2 * T]
        send_sems, recv_sems = refs[2 * T:2 * T + 2]
        token = refs[-1]
        x, y, c, other_chips = _position()
        for t in range(T):
            for k, (cx, cy) in enumerate(other_chips):
                _start_in_chunks(_remote_copier(send_sems.at[3 * t + k], recv_sems.at[3 * t + k], (cx, cy, c)),
                                 srcs[t].at[2 * cx + cy], dsts[t].at[k], waited_here=False)
        token[...] = jnp.zeros_like(token)

    hbm = [pltpu.HBM(a.shape, a.dtype) for a in list(chip_sums) + lands]
    res = pl.pallas_call(
        body, name=name, in_specs=[HBM_SPEC] * (2 * T),
        out_specs=[SEM_SPEC, SEM_SPEC] + [HBM_SPEC] * (2 * T) + [VMEM_WHOLE],
        out_shape=[pltpu.SemaphoreType.DMA((3 * T,)), pltpu.SemaphoreType.DMA((3 * T,))] + hbm
        + [jax.ShapeDtypeStruct((8, LANES), jnp.float32)],
        input_output_aliases={i: 2 + i for i in range(2 * T)},
        compiler_params=pltpu.CompilerParams(has_side_effects=SIDE_EFFECT),
    )(*[pltpu.with_memory_space_constraint(a, pltpu.HBM) for a in list(chip_sums) + lands])
    return res[0], res[1], res[2:2 + T], res[2 + T:2 + 2 * T], res[-1]


def _scatter_wait(name, send_sems, recv_sems, chip_sums, lands, after):
    T = len(chip_sums)

    def body(*refs):
        srcs, dsts = refs[:T], refs[T:2 * T]
        send, recv = refs[2 * T:2 * T + 2]
        x, y, c, other_chips = _position()
        for t in range(T):
            for k, (cx, cy) in enumerate(other_chips):
                cp = _remote_copier(send.at[3 * t + k], recv.at[3 * t + k], (cx, cy, c))(srcs[t].at[2 * cx + cy], dsts[t].at[k])
                cp.wait_send()
                cp.wait_recv()

    arrays = list(chip_sums) + list(lands)
    res = pl.pallas_call(
        body, name=name, in_specs=[HBM_SPEC] * (2 * T) + [SEM_SPEC, SEM_SPEC, ANY],
        out_specs=[HBM_SPEC] * (2 * T), out_shape=[pltpu.HBM(a.shape, a.dtype) for a in arrays],
        input_output_aliases={i: i for i in range(2 * T)},
        compiler_params=pltpu.CompilerParams(has_side_effects=SIDE_EFFECT),
    )(*arrays, send_sems, recv_sems, after)
    return res[:T], res[T:]


def _as_shards(partial):
    return [p.reshape((N_CHIPS, -1, p.shape[-1])) for p in partial]


def _reduce_begin(tag, names, partial, pos):
    parts = _as_shards(partial)
    received = _swap_halves("swap_grad_halves_" + tag, parts)
    return _reduce_scatter_start(tag, names, parts, received, pos)


def _reduce_scatter_start(tag, names, parts, received, pos):
    chip_sums = _per_run(lambda name, p, r: _chip_sum(name, p, r, WIRE_DTYPE, pos), names, list(parts), list(received))
    return _scatter_start("scatter_start_" + tag, chip_sums)


def _reduce_end(tag, names, flight, after, pos):
    send_sems, recv_sems, chip_sums, lands, _ = flight
    chip_sums, others = _scatter_wait("scatter_wait_" + tag, send_sems, recv_sems, chip_sums, lands, after)
    reduced = _per_run(lambda name, cs, ot: _owner_sum(name, cs, ot, pos), names, list(chip_sums), list(others))
    return dict(zip(names, reduced))


def _join_start(name, halves):
    T = len(halves)

    def body(*refs):
        ins = refs[:T]
        send_sems, recv_sems = refs[T:T + 2]
        token = refs[-1]
        x, y, c, _ = _position()
        for t in range(T):
            _start_in_chunks(_remote_copier(send_sems.at[t], recv_sems.at[t], (x, y, 1 - c)),
                             ins[t].at[c], ins[t].at[c], waited_here=False)
        token[...] = jnp.zeros_like(token)

    res = pl.pallas_call(
        body, name=name, in_specs=[HBM_SPEC] * T,
        out_specs=[SEM_SPEC, SEM_SPEC] + [HBM_SPEC] * T + [VMEM_WHOLE],
        out_shape=[pltpu.SemaphoreType.DMA((T,)), pltpu.SemaphoreType.DMA((T,))]
        + [pltpu.HBM(h.shape, h.dtype) for h in halves] + [jax.ShapeDtypeStruct((8, LANES), jnp.float32)],
        input_output_aliases={t: 2 + t for t in range(T)},
        compiler_params=pltpu.CompilerParams(has_side_effects=SIDE_EFFECT),
    )(*[pltpu.with_memory_space_constraint(h, pltpu.HBM) for h in halves])
    return res[0], res[1], res[2:2 + T], res[-1]


def _join_wait(name, flight, after):
    send_sems, recv_sems, halves, _ = flight
    T = len(halves)

    def body(*refs):
        ins = refs[:T]
        send, recv = refs[T:T + 2]
        x, y, c, _ = _position()
        for t in range(T):
            cp = _remote_copier(send.at[t], recv.at[t], (x, y, 1 - c))(ins[t].at[c], ins[t].at[1 - c])
            cp.wait_send()
            cp.wait_recv()

    return pl.pallas_call(
        body, name=name, in_specs=[HBM_SPEC] * T + [SEM_SPEC, SEM_SPEC, ANY], out_specs=[HBM_SPEC] * T,
        out_shape=[pltpu.HBM(h.shape, h.dtype) for h in halves], input_output_aliases={t: t for t in range(T)},
        compiler_params=pltpu.CompilerParams(has_side_effects=SIDE_EFFECT),
    )(*halves, send_sems, recv_sems, after)


def _peers():
    x, y, c, _ = _position()
    flips = [(fx, fy, fc) for fx in (0, 1) for fy in (0, 1) for fc in (0, 1)][1:]
    return [(jnp.bitwise_xor(x, fx), jnp.bitwise_xor(y, fy), jnp.bitwise_xor(c, fc)) for fx, fy, fc in flips]


def _device_index(px, py, pc):
    return 4 * px + 2 * py + pc


def _exchange_start(name, slots):
    def body(slots_ref, send_sems, recv_sems, thru, token):
        x, y, c, _ = _position()
        own = slots_ref.at[_device_index(x, y, c)]
        for k, peer in enumerate(_peers()):
            _remote_copier(send_sems.at[k], recv_sems.at[k], peer)(own, own).start()
        token[...] = jnp.zeros_like(token)

    res = pl.pallas_call(
        body, name=name, in_specs=[HBM_SPEC], out_specs=[SEM_SPEC, SEM_SPEC, HBM_SPEC, VMEM_WHOLE],
        out_shape=[pltpu.SemaphoreType.DMA((N_DEV - 1,)), pltpu.SemaphoreType.DMA((N_DEV - 1,)),
                   pltpu.HBM(slots.shape, slots.dtype), jax.ShapeDtypeStruct((8, LANES), jnp.float32)],
        input_output_aliases={0: 2}, compiler_params=pltpu.CompilerParams(has_side_effects=SIDE_EFFECT),
    )(pltpu.with_memory_space_constraint(slots, pltpu.HBM))
    return res[0], res[1], res[2], res[3]


def _exchange_wait(name, flight, after):
    send_sems, recv_sems, slots, _ = flight

    def body(slots_ref, send, recv, after_ref, out_ref):
        x, y, c, _ = _position()
        own = slots_ref.at[_device_index(x, y, c)]
        for k, peer in enumerate(_peers()):
            cp = _remote_copier(send.at[k], recv.at[k], peer)(own, slots_ref.at[_device_index(*peer)])
            cp.wait_send()
            cp.wait_recv()

    return pl.pallas_call(
        body, name=name, in_specs=[HBM_SPEC, SEM_SPEC, SEM_SPEC, ANY], out_specs=HBM_SPEC,
        out_shape=pltpu.HBM(slots.shape, slots.dtype), input_output_aliases={0: 0},
        compiler_params=pltpu.CompilerParams(has_side_effects=SIDE_EFFECT),
    )(slots, send_sems, recv_sems, after)


def _sum_slots(name, slots):
    n, rows, lanes = slots.shape
    tr = _row_tile(rows, lanes * n)

    def body(s_ref, o_ref):
        acc = s_ref[0]
        for d in range(1, n):
            acc = acc + s_ref[d]
        o_ref[...] = acc

    return pl.pallas_call(
        body, name=name, grid=(rows // tr,), in_specs=[pl.BlockSpec((n, tr, lanes), lambda i: (0, i, 0))],
        out_specs=pl.BlockSpec((tr, lanes), lambda i: (i, 0)), out_shape=jax.ShapeDtypeStruct((rows, lanes), jnp.float32),
        compiler_params=_params(1))(slots)


BIG = ("w_in", "pool_w", "w_out", "w_q", "w_k", "w_v", "w_o", "w_gate", "w_up", "w_down")
SMALL = ("norm_mix_g", "pool_scale", "sgu_norm_g", "w_spatial", "b_spatial", "norm_xattn_g", "norm_mem_g",
         "norm_ffn_g", "final_norm_g")
GATHER_ORDER = (("w_in", "pool_w"), ("w_out",), ("w_q", "w_k", "w_v", "w_o"), ("w_gate", "w_up"), ("w_down",))
WEIGHTS = ("norm_mix_g", "w_in", "pool_w", "pool_scale", "sgu_norm_g", "w_spatial", "b_spatial", "w_out",
           "norm_xattn_g", "norm_mem_g", "w_q", "w_k", "w_v", "w_o", "norm_ffn_g", "w_gate", "w_up", "w_down",
           "final_norm_g")


def _as2d(a):
    return a.reshape(-1, a.shape[-1])


def _as_lanes(a):
    return a.reshape(-1, LANES)


def kernel(x, mem, norm_mix_g, w_in, pool_w, pool_scale, sgu_norm_g, w_spatial, b_spatial, w_out, norm_xattn_g, norm_mem_g, w_q, w_k, w_v, w_o, norm_ffn_g, w_gate, w_up, w_down, final_norm_g, loss_target, m_norm_mix_g, m_w_in, m_pool_w, m_pool_scale, m_sgu_norm_g, m_w_spatial, m_b_spatial, m_w_out, m_norm_xattn_g, m_norm_mem_g, m_w_q, m_w_k, m_w_v, m_w_o, m_norm_ffn_g, m_w_gate, m_w_up, m_w_down, m_final_norm_g, v_norm_mix_g, v_w_in, v_pool_w, v_pool_scale, v_sgu_norm_g, v_w_spatial, v_b_spatial, v_w_out, v_norm_xattn_g, v_norm_mem_g, v_w_q, v_w_k, v_w_v, v_w_o, v_norm_ffn_g, v_w_gate, v_w_up, v_w_down, v_final_norm_g):
    args = dict(locals())
    weights = {n: args[n] for n in WEIGHTS}
    moments_m = {n: args["m_" + n] for n in WEIGHTS}
    moments_v = {n: args["v_" + n] for n in WEIGHTS}
    low = MXU_DTYPE

    xs = x[0]
    mems = mem[0]
    target = loss_target[0]
    D = xs.shape[1]
    G, PG = pool_w.shape[1], pool_w.shape[3]

    pos = jnp.stack([2 * lax.axis_index("x") + lax.axis_index("y"), lax.axis_index("c")]).astype(jnp.int32)
    def cast(names, after=None):
        slabs = _per_run(lambda name, xs: _cast_into_slabs(name, xs, WIRE_DTYPE, pos, after), names,
                         [_as2d(weights[n]) for n in names])
        return dict(zip(names, slabs))

    def land(gi, flight, after, relayed=True):
        wait = _relayed_wait if relayed else _gather_wait
        landed = wait("gather_wait_%d" % gi, flight, after)
        return dict(zip(GATHER_ORDER[gi], _share_with_sibling("gather_share_%d" % gi, landed)))

    def rows(a):
        return a.reshape(-1, a.shape[-1])

    g_mix, g_xattn, g_mem, g_ffn = norm_mix_g, norm_xattn_g, norm_mem_g, norm_ffn_g
    g_final = final_norm_g.reshape(1, D)
    ws = w_spatial[0]
    b_full = jnp.broadcast_to(b_spatial[0][:, :, None], ws.shape[:2] + (sgu_norm_g.shape[1] // ws.shape[0],))

    first = cast(GATHER_ORDER[0])
    _, (sent_0,), _, started = _gather_step("gather_step_0", direct=[[first[n] for n in GATHER_ORDER[0]]])
    rest = cast([n for group in GATHER_ORDER[1:3] for n in group], after=started)
    _, (sent_1,), (hop_2,), started = _gather_step(
        "gather_step_1", direct=[[rest[n] for n in GATHER_ORDER[1]]], ring=[[rest[n] for n in GATHER_ORDER[2]]])
    h1 = _rmsnorm("norm_mix", xs, g_mix, low, after=started)
    mn = _rmsnorm("norm_mem", mems, g_mem, low)
    rest.update(cast(GATHER_ORDER[3], after=h1))
    rest.update(cast(GATHER_ORDER[4], after=rest[GATHER_ORDER[3][-1]]))
    slabs = [[rest[n] for n in group] for group in GATHER_ORDER[1:]]
    got = land(0, sent_0, rest["w_down"], relayed=False)
    wg_in = got["w_in"]
    pw = got["pool_w"].reshape(N_CHIPS, G, PG // N_CHIPS, PG).transpose(1, 0, 2, 3).reshape(G, PG, PG)
    proj = _matmul_nn_cols("proj_in", h1, wg_in, jnp.float32)
    z = _mixer_fwd(proj, pw, pool_scale, sgu_norm_g, ws, b_full)
    (relay_2,), _, (hop_3,), started = _gather_step("gather_step_2", arrived=[hop_2], ring=slabs[2:3], after=z)
    full = {"w_out": rows(land(1, sent_1, started, relayed=False)["w_out"])}
    x1, h2 = _proj_residual_norm("mix_out", z, full["w_out"], xs, g_xattn)
    full.update({n: rows(a) for n, a in land(2, relay_2, h2).items()})
    q = _matmul_nn("xattn_q", h2, full["w_q"], low)
    k = _matmul_nn("xattn_k", mn, full["w_k"], low)
    vv = _matmul_nn("xattn_v", mn, full["w_v"], low)
    o = _attn_fwd(q, k, vv)
    (relay_3,), (sent_4,), _, started = _gather_step("gather_step_3", arrived=[hop_3], direct=slabs[3:4], after=o)
    x2, h3 = _proj_residual_norm("xattn_out", o, full["w_o"], x1, g_ffn, after=started)
    got = land(3, relay_3, h3)
    wg_gate, wg_up = got["w_gate"], got["w_up"]
    act, act_by_gate, act_by_up = _ffn_gate_up(h3, wg_gate, wg_up)
    full["w_down"] = rows(land(4, sent_4, act, relayed=False)["w_down"])
    x3 = _ffn_down(act, full["w_down"], x2)
    dx3, dx3_low, loss_part, d_final_g = _final_norm_loss(x3, target, g_final)

    dgate, dup = _ffn_down_bwd(dx3_low, full["w_down"], act_by_gate, act_by_up, N_CHIPS)
    (d_w_down,) = _matmul_tn("grad_w_down", act, [dx3_low], WIRE_DTYPE)
    (d_w_gate,) = _matmul_tn_cols("grad_w_gate", h3, [dgate], N_CHIPS, WIRE_DTYPE)
    (d_w_up,) = _matmul_tn_cols("grad_w_up", h3, [dup], N_CHIPS, WIRE_DTYPE)
    ffn_names = ("w_gate", "w_up", "w_down")
    ffn_swap = _swap_start("swap_start_ffn", _as_shards([d_w_gate, d_w_up, d_w_down]))
    dh3 = _matmul_nt_cols("ffn_dh", [dgate, dup], [wg_gate, wg_up], jnp.float32, tn=1024, after=ffn_swap[-1])
    ffn_flight = _reduce_scatter_start("ffn", ffn_names, *_swap_wait("swap_wait_ffn", ffn_swap, dh3), pos)
    dx2, dx2_low, d_ffn_g = _rmsnorm_bwd("norm_ffn_bwd", dh3, x2, g_ffn, dx3, low)

    do = _matmul_nt("xattn_do", [dx2_low], [full["w_o"]], low, after=ffn_flight[-1])
    (d_w_o,) = _matmul_tn("grad_w_o", o, [dx2_low], WIRE_DTYPE)
    dq, dk, dvv = _attn_bwd(q, k, vv, do)
    (d_w_q,) = _matmul_tn("grad_w_q", h2, [dq], WIRE_DTYPE)
    dx1, dx1_low, d_xattn_g = _matmul_nt_norm_bwd("xattn_dh_norm_bwd", dq, full["w_q"], x1, g_xattn, dx2, low)
    d_w_k, d_w_v = _matmul_tn("grad_w_kv", mn, [dk, dvv], WIRE_DTYPE)
    dmn = _matmul_nt("xattn_dmem", [dk, dvv], [full["w_k"], full["w_v"]], jnp.float32)
    (d_mem_g,) = _rmsnorm_bwd("norm_mem_bwd", dmn, mems, g_mem, None, None)

    (d_w_out,) = _matmul_tn("grad_w_out", z, [dx1_low], WIRE_DTYPE)
    attn_names = ("w_q", "w_k", "w_v", "w_o", "w_out")
    attn_swap = _swap_start("swap_start_attn", _as_shards([d_w_q, d_w_k, d_w_v, d_w_o, d_w_out]))
    dz = _matmul_nt("mix_dz", [dx1_low], [full["w_out"]], jnp.float32, after=attn_swap[-1])
    attn_flight = _reduce_scatter_start("attn", attn_names, *_swap_wait("swap_wait_attn", attn_swap, dz), pos)
    dproj, d_pw, d_ps, d_gs, d_ws, d_b = _mixer_bwd(proj, dz, pw, pool_scale, sgu_norm_g, ws, b_full)
    (d_w_in,) = _matmul_tn_cols("grad_w_in", h1, [dproj], N_CHIPS, WIRE_DTYPE, tka=1024, after=attn_flight[-1])
    d_pool_w = d_pw.reshape(G, N_CHIPS, PG // N_CHIPS, PG).transpose(1, 0, 2, 3).reshape(N_CHIPS, PG, PG).astype(WIRE_DTYPE)
    mix_names = ("w_in", "pool_w")
    mix_flight = _reduce_begin("mix", mix_names, [d_w_in, d_pool_w], pos)
    grad_x, d_mix_g = _matmul_nt_norm_bwd("mix_dh_norm_bwd", dproj, wg_in, xs, g_mix, dx1, None, tm=512, after=mix_flight[-1])

    small_parts = {
        "norm_mix_g": d_mix_g, "pool_scale": d_ps, "sgu_norm_g": d_gs, "w_spatial": d_ws, "b_spatial": d_b,
        "norm_xattn_g": d_xattn_g, "norm_mem_g": d_mem_g, "norm_ffn_g": d_ffn_g, "final_norm_g": d_final_g,
    }
    row_counts = [small_parts[n].size // LANES for n in SMALL]
    packed = jnp.concatenate([_as_lanes(small_parts[n]) for n in SMALL], axis=0)
    device = _device_index(lax.axis_index("x"), lax.axis_index("y"), lax.axis_index("c"))
    slots = lax.dynamic_update_slice(jnp.zeros((N_DEV,) + packed.shape, jnp.float32), packed[None], (device, 0, 0))
    small_flight = _exchange_start("small_grads_start", slots)

    groups = (("ffn", ffn_names, ffn_flight), ("attn", attn_names, attn_flight), ("mix", mix_names, mix_flight))
    joins, after = {}, small_flight[-1]
    for tag, names, flight in groups:
        reduced = _reduce_end(tag, names, flight, after, pos)
        joins[tag] = _join_start("join_start_" + tag, [reduced[n] for n in names])
        after = joins[tag][-1]
    small_sum = _sum_slots("small_grads_sum", _exchange_wait("small_grads_wait", small_flight, after))

    out_grad, out_delta, out_m, out_v = {}, {}, {}, {}
    after = small_sum
    for tag, names, _ in groups:
        grads = [g.reshape(-1, g.shape[-1]) for g in _join_wait("join_wait_" + tag, joins[tag], after)]
        updates = _per_run(_adamw, names, [_as2d(weights[n]) for n in names], grads,
                           [_as2d(moments_m[n]) for n in names], [_as2d(moments_v[n]) for n in names])
        for n, update in zip(names, updates):
            shape = weights[n].shape
            out_grad[n], out_delta[n], out_m[n], out_v[n] = (a.reshape(shape) for a in update)
        after = updates[-1][1]
    small_out = _adamw_small(
        small_sum, [_as_lanes(weights[n]) for n in SMALL], [_as_lanes(moments_m[n]) for n in SMALL],
        [_as_lanes(moments_v[n]) for n in SMALL], row_counts)
    for t, n in enumerate(SMALL):
        shape = weights[n].shape
        out_grad[n], out_delta[n], out_m[n], out_v[n] = (a.reshape(shape) for a in small_out[4 * t:4 * t + 4])

    loss = lax.psum(loss_part[0, 0], ("x", "y", "c"))
    return (loss, grad_x[None], *[out_grad[n] for n in WEIGHTS], *[out_delta[n] for n in WEIGHTS],
            *[out_m[n] for n in WEIGHTS], *[out_v[n] for n in WEIGHTS])
```

```python
import math

import jax
import jax.numpy as jnp
from jax import lax
from jax.experimental import pallas as pl
from jax.experimental.pallas import tpu as pltpu

MXU_DTYPE = jnp.bfloat16
WIRE_DTYPE = jnp.bfloat16

EPS = 1e-6
CHUNK = 64
POOL_WINDOWS = (2, 4, 8, 16)
POOL_HALO = 16
N_XATTN_HEADS = 4
ADAM_LR = 0.001
ADAM_B1 = 0.9
ADAM_B2 = 0.999
ADAM_EPS = 1e-08
ADAM_WD = 0.01
ADAM_STEP = 10

N_CHIPS = 4
N_DEV = 8
LANES = 128
VMEM_LIMIT_BYTES = 56 * 2 ** 20
COPY_CHUNK_BYTES = 512 * 2 ** 10
COPY_ROW_ALIGN = 16

MESH = pl.DeviceIdType.MESH
ANY = pl.BlockSpec(memory_space=pl.ANY)
VMEM_WHOLE = pl.BlockSpec(memory_space=pltpu.VMEM)
HBM_SPEC = pl.BlockSpec(memory_space=pltpu.HBM)
SEM_SPEC = pl.BlockSpec(memory_space=pltpu.SEMAPHORE)
SIDE_EFFECT = pltpu.SideEffectType.DATAFLOW_SIDE_EFFECTING

NN = (((1,), (0,)), ((), ()))
NT = (((1,), (1,)), ((), ()))
TN = (((0,), (0,)), ((), ()))


def _tile(n, target, mult):
    best = None
    for t in range(mult, min(n, target) + 1, mult):
        if n % t == 0:
            best = t
    return n if best is None else best


def _params(n_grid_axes):
    return pltpu.CompilerParams(dimension_semantics=("arbitrary",) * n_grid_axes, vmem_limit_bytes=VMEM_LIMIT_BYTES)


def _rsqrt_mean_sq(x):
    return lax.rsqrt(jnp.mean(x * x, axis=-1, keepdims=True) + EPS)


def _sum_all(x):
    return jnp.sum(jnp.sum(x, axis=1, keepdims=True), axis=0, keepdims=True)


def _sigmoid(x):
    return 0.5 * jnp.tanh(0.5 * x) + 0.5


def _store_epilogue(accs, ex_refs, out_refs, i, j):
    for a, o in zip(accs, out_refs):
        o[...] = a.astype(o.dtype)


def _fused_matmul(name, grid, operands, pairs, acc_shapes, extras, outs, epilogue):
    nk = grid[2]
    n_op, n_ex, n_out, n_acc = len(operands), len(extras), len(outs), len(acc_shapes)
    in_place = nk > 1 and epilogue is _store_epilogue and all(o[1] == jnp.float32 for o in outs)

    def body(*refs):
        op_refs = refs[:n_op]
        ex_refs = refs[n_op:n_op + n_ex]
        out_refs = refs[n_op + n_ex:n_op + n_ex + n_out]
        acc_refs = out_refs if in_place else refs[n_op + n_ex + n_out:]
        i, j, k = pl.program_id(0), pl.program_id(1), pl.program_id(2)
        loaded = {}

        def operand(n):
            if n not in loaded:
                loaded[n] = op_refs[n][...].astype(MXU_DTYPE)
            return loaded[n]

        def product(ia, ib, dims):
            return lax.dot_general(operand(ia), operand(ib), dims, preferred_element_type=jnp.float32)

        def products():
            parts = [None] * n_acc
            for ia, ib, iacc, dims in pairs:
                d = product(ia, ib, dims)
                parts[iacc] = d if parts[iacc] is None else parts[iacc] + d
            return parts

        if nk == 1:
            epilogue(products(), ex_refs, out_refs, i, j)
            return

        @pl.when(k == 0)
        def _():
            for a in range(n_acc):
                acc_refs[a][...] = jnp.zeros_like(acc_refs[a])

        for a, part in enumerate(products()):
            acc_refs[a][...] += part

        if not in_place:
            @pl.when(k == nk - 1)
            def _():
                epilogue([acc_refs[a][...] for a in range(n_acc)], ex_refs, out_refs, i, j)

    arrays = [o[0] for o in operands] + [e[0] for e in extras]
    in_specs = [pl.BlockSpec(o[1], o[2]) for o in operands]
    in_specs += [ANY if e[1] is None else pl.BlockSpec(e[1], e[2]) for e in extras]
    return pl.pallas_call(
        body,
        name=name,
        grid=grid,
        in_specs=in_specs,
        out_specs=[pl.BlockSpec(o[2], o[3]) for o in outs],
        out_shape=[jax.ShapeDtypeStruct(o[0], o[1]) for o in outs],
        scratch_shapes=[pltpu.VMEM(s, jnp.float32) for s in acc_shapes] if nk > 1 and not in_place else [],
        compiler_params=_params(3),
    )(*arrays)


def _matmul_nn(name, a, b, out_dtype, tm=1024, tn=1024):
    M, K = a.shape
    N = b.shape[1]
    tm, tn = _tile(M, tm, 16), _tile(N, tn, LANES)
    return _fused_matmul(
        name, (M // tm, N // tn, 1),
        [(a, (tm, K), lambda i, j, k: (i, 0)), (b, (K, tn), lambda i, j, k: (0, j))],
        [(0, 1, 0, NN)], [(tm, tn)], [],
        [((M, N), out_dtype, (tm, tn), lambda i, j, k: (i, j))], _store_epilogue)[0]


def _matmul_nn_cols(name, a, b3, out_dtype, tm=1024):
    M, K = a.shape
    J, _, Nc = b3.shape
    tm = _tile(M, tm, 16)
    return _fused_matmul(
        name, (M // tm, J, 1),
        [(a, (tm, K), lambda i, j, k: (i, 0)), (b3, (None, K, Nc), lambda i, j, k: (j, 0, 0))],
        [(0, 1, 0, NN)], [(tm, Nc)], [],
        [((M, J * Nc), out_dtype, (tm, Nc), lambda i, j, k: (i, j))], _store_epilogue)[0]


def _order_after(after):
    return [] if after is None else [(after, None, None)]


def _matmul_nt(name, a_list, b_list, out_dtype, tm=1024, tn=1024, after=None):
    M, K = a_list[0].shape
    N = b_list[0].shape[0]
    tm, tn = _tile(M, tm, 16), _tile(N, tn, LANES)
    n = len(a_list)
    operands = [(a, (tm, K), lambda i, j, k: (i, 0)) for a in a_list]
    operands += [(b, (tn, K), lambda i, j, k: (j, 0)) for b in b_list]
    return _fused_matmul(
        name, (M // tm, N // tn, 1), operands,
        [(p, n + p, 0, NT) for p in range(n)], [(tm, tn)], _order_after(after),
        [((M, N), out_dtype, (tm, tn), lambda i, j, k: (i, j))], _store_epilogue)[0]


def _matmul_nt_cols(name, a_list, b3_list, out_dtype, tm=1024, tn=512, after=None):
    M = a_list[0].shape[0]
    J, N, Kc = b3_list[0].shape
    tm, tn = _tile(M, tm, 16), _tile(N, tn, LANES)
    n = len(a_list)
    operands = [(a, (tm, Kc), lambda i, j, k: (i, k)) for a in a_list]
    operands += [(b, (None, tn, Kc), lambda i, j, k: (k, j, 0)) for b in b3_list]
    return _fused_matmul(
        name, (M // tm, N // tn, J), operands,
        [(p, n + p, 0, NT) for p in range(n)], [(tm, tn)], _order_after(after),
        [((M, N), out_dtype, (tm, tn), lambda i, j, k: (i, j))], _store_epilogue)[0]


def _matmul_tn(name, a, b_list, out_dtype, tka=1024, tn=1024, tkm=4096):
    M, Ka = a.shape
    N = b_list[0].shape[1]
    tka, tn, tkm = _tile(Ka, tka, LANES), _tile(N, tn, LANES), _tile(M, tkm, 16)
    n = len(b_list)
    operands = [(a, (tkm, tka), lambda i, j, k: (k, i))]
    operands += [(b, (tkm, tn), lambda i, j, k: (k, j)) for b in b_list]
    return _fused_matmul(
        name, (Ka // tka, N // tn, M // tkm), operands,
        [(0, 1 + p, p, TN) for p in range(n)], [(tka, tn)] * n, [],
        [((Ka, N), out_dtype, (tka, tn), lambda i, j, k: (i, j))] * n, _store_epilogue)


def _matmul_tn_cols(name, a, b_list, n_shards, out_dtype, tka=512, tkm=4096, after=None):
    M, Ka = a.shape
    Nc = b_list[0].shape[1] // n_shards
    tka, tkm = _tile(Ka, tka, LANES), _tile(M, tkm, 16)
    n = len(b_list)
    operands = [(a, (tkm, tka), lambda i, j, k: (k, i))]
    operands += [(b, (tkm, Nc), lambda i, j, k: (k, j)) for b in b_list]
    return _fused_matmul(
        name, (Ka // tka, n_shards, M // tkm), operands,
        [(0, 1 + p, p, TN) for p in range(n)], [(tka, Nc)] * n, _order_after(after),
        [((n_shards, Ka, Nc), out_dtype, (None, tka, Nc), lambda i, j, k: (j, i, 0))] * n, _store_epilogue)


def _rmsnorm(name, x, g_row, out_dtype, tm=512, after=None):
    S, D = x.shape
    tm = _tile(S, tm, 16)

    def body(x_ref, g_ref, *rest):
        o_ref = rest[-1]
        xv = x_ref[...]
        o_ref[...] = (xv * _rsqrt_mean_sq(xv) * g_ref[...]).astype(o_ref.dtype)

    ordered = [] if after is None else [after]
    return pl.pallas_call(
        body, name=name, grid=(S // tm,),
        in_specs=[pl.BlockSpec((tm, D), lambda i: (i, 0)), pl.BlockSpec((1, D), lambda i: (0, 0))] + [ANY] * len(ordered),
        out_specs=pl.BlockSpec((tm, D), lambda i: (i, 0)),
        out_shape=jax.ShapeDtypeStruct((S, D), out_dtype),
        compiler_params=_params(1),
    )(x, g_row, *ordered)


def _rmsnorm_bwd_values(dh, xin, g):
    r = _rsqrt_mean_sq(xin)
    xhat = xin * r
    dhg = dh * g
    dx = r * (dhg - xhat * jnp.mean(dhg * xhat, axis=-1, keepdims=True))
    return dx, jnp.sum(dh * xhat, axis=0, keepdims=True)


def _rmsnorm_bwd(name, dh, xin, g_row, dres, low_dtype, tm=256):
    S, D = xin.shape
    tm = _tile(S, tm, 16)
    has_dx = dres is not None
    has_low = has_dx and low_dtype is not None

    def body(*refs):
        dh_ref, x_ref, g_ref = refs[:3]
        rest = refs[3:]
        dx, dg = _rmsnorm_bwd_values(dh_ref[...], x_ref[...], g_ref[...])
        if has_dx:
            dres_ref, rest = rest[0], rest[1:]
            dx = dx + dres_ref[...]
            rest[0][...] = dx
            if has_low:
                rest[1][...] = dx.astype(low_dtype)
        dg_ref = rest[-1]

        @pl.when(pl.program_id(0) == 0)
        def _():
            dg_ref[...] = jnp.zeros_like(dg_ref)

        dg_ref[...] += dg

    row = pl.BlockSpec((tm, D), lambda i: (i, 0))
    vec = pl.BlockSpec((1, D), lambda i: (0, 0))
    ins, in_specs = [dh, xin, g_row], [row, row, vec]
    out_shape, out_specs = [], []
    if has_dx:
        ins.append(dres)
        in_specs.append(row)
        out_shape.append(jax.ShapeDtypeStruct((S, D), jnp.float32))
        out_specs.append(row)
        if has_low:
            out_shape.append(jax.ShapeDtypeStruct((S, D), low_dtype))
            out_specs.append(row)
    out_shape.append(jax.ShapeDtypeStruct((1, D), jnp.float32))
    out_specs.append(vec)
    return pl.pallas_call(
        body, name=name, grid=(S // tm,), in_specs=in_specs, out_specs=out_specs, out_shape=out_shape,
        compiler_params=_params(1),
    )(*ins)


def _chunk_mask(blk):
    t = lax.broadcasted_iota(jnp.int32, (blk, blk), 0)
    s = lax.broadcasted_iota(jnp.int32, (blk, blk), 1)
    return (s // CHUNK) <= (t // CHUNK)


def _pool_inputs(a, halo, row0, tm, g, pg):
    cols = slice(g * pg, (g + 1) * pg)
    w = POOL_WINDOWS[g]
    s = jnp.concatenate([halo[:, cols], a[:, cols]], axis=0)
    span = 1
    while span < w:
        s = s + pltpu.roll(s, span, axis=0)
        span *= 2
    t = lax.broadcasted_iota(jnp.int32, (tm, 1), 0) + row0
    cnt = jnp.minimum(t + 1, w).astype(jnp.float32)
    return s[POOL_HALO:] / cnt - a[:, cols]


def _mixer_specs(S, tm, DP, DS):
    per_halo = tm // POOL_HALO
    n_halo = S // POOL_HALO
    a_spec = pl.BlockSpec((tm, DP), lambda i: (i, 0))
    prev_spec = pl.BlockSpec((POOL_HALO, DP), lambda i: (jnp.maximum(i * per_halo - 1, 0), 0))
    next_spec = pl.BlockSpec((POOL_HALO, DP), lambda i: (jnp.minimum((i + 1) * per_halo, n_halo - 1), 0))
    u_spec = pl.BlockSpec((tm, DS), lambda i: (i, 1))
    v_spec = pl.BlockSpec((tm, DS), lambda i: (i, 2))
    return a_spec, prev_spec, next_spec, u_spec, v_spec


def _whole(arr):
    nd = arr.ndim
    return pl.BlockSpec(arr.shape, lambda i: (0,) * nd)


def _mixer_fwd(proj, pw, ps_row, gs_row, ws, b_full, tm=512):
    S = proj.shape[0]
    G, PG, _ = pw.shape
    DP = G * PG
    H, BLK, _ = ws.shape
    DS = gs_row.shape[1]
    HD = DS // H
    assert DP == DS and proj.shape[1] == DP + 2 * DS
    tm = _tile(S, tm, BLK)

    def body(a_ref, prev_ref, u_ref, v_ref, pw_ref, ps_ref, gs_ref, ws_ref, b_ref, z_ref):
        i = pl.program_id(0)
        a = a_ref[...]
        halo = jnp.where(i > 0, prev_ref[...], 0.0)
        for g in range(G):
            cols = slice(g * PG, (g + 1) * PG)
            p = _pool_inputs(a, halo, i * tm, tm, g, PG)
            y = jnp.dot(p.astype(MXU_DTYPE), pw_ref[g], preferred_element_type=jnp.float32) * ps_ref[:, cols]
            z_ref[:, cols] = y.astype(z_ref.dtype)
        v = v_ref[...]
        vn = (v * _rsqrt_mean_sq(v) * gs_ref[...]).astype(MXU_DTYPE)
        u = u_ref[...]
        mask = _chunk_mask(BLK)
        for h in range(H):
            wm = jnp.where(mask, ws_ref[h], 0.0).astype(MXU_DTYPE)
            bias = b_ref[h]
            cols = slice(h * HD, (h + 1) * HD)
            for n in range(tm // BLK):
                rows = slice(n * BLK, (n + 1) * BLK)
                mixed = jnp.dot(wm, vn[rows, cols], preferred_element_type=jnp.float32) + bias
                z_ref[rows, DP + h * HD:DP + (h + 1) * HD] = (u[rows, cols] * mixed).astype(z_ref.dtype)

    a_spec, prev_spec, _, u_spec, v_spec = _mixer_specs(S, tm, DP, DS)
    return pl.pallas_call(
        body, name="mixer_fwd", grid=(S // tm,),
        in_specs=[a_spec, prev_spec, u_spec, v_spec, _whole(pw), _whole(ps_row), _whole(gs_row), _whole(ws), _whole(b_full)],
        out_specs=pl.BlockSpec((tm, DP + DS), lambda i: (i, 0)),
        out_shape=jax.ShapeDtypeStruct((S, DP + DS), MXU_DTYPE),
        compiler_params=_params(1),
    )(proj, proj, proj, proj, pw, ps_row, gs_row, ws, b_full)


def _mixer_bwd(proj, dz, pw, ps_row, gs_row, ws, b_full, tm=256):
    S = proj.shape[0]
    G, PG, _ = pw.shape
    DP = G * PG
    H, BLK, _ = ws.shape
    DS = gs_row.shape[1]
    HD = DS // H
    tm = _tile(S, tm, BLK)
    nb = S // tm
    ext = tm + POOL_HALO

    def body(a_ref, prev_ref, u_ref, v_ref, dzp_ref, dzp_next_ref, dzs_ref, pw_ref, ps_ref, gs_ref, ws_ref, b_ref,
             dproj_ref, dpw_ref, dps_ref, dgs_ref, dws_ref, db_ref, dvn_ref):
        i = pl.program_id(0)

        @pl.when(i == 0)
        def _():
            for r in (dpw_ref, dps_ref, dgs_ref, dws_ref, db_ref):
                r[...] = jnp.zeros_like(r)

        a = a_ref[...]
        halo = jnp.where(i > 0, prev_ref[...], 0.0)
        dyp = dzp_ref[...]
        dyp_next = jnp.where(i < nb - 1, dzp_next_ref[...], 0.0)
        dq_ext = jnp.concatenate([dyp, dyp_next], axis=0) * ps_ref[...]
        t_ext = lax.broadcasted_iota(jnp.int32, (ext, 1), 0) + i * tm
        for g in range(G):
            cols = slice(g * PG, (g + 1) * PG)
            w = POOL_WINDOWS[g]
            pb = _pool_inputs(a, halo, i * tm, tm, g, PG).astype(MXU_DTYPE)
            y_pre = jnp.dot(pb, pw_ref[g], preferred_element_type=jnp.float32)
            dps_ref[:, cols] += jnp.sum(dyp[:, cols] * y_pre, axis=0, keepdims=True)
            dqb = dq_ext[:, cols].astype(MXU_DTYPE)
            dpw_ref[g] += lax.dot_general(pb, dqb[:tm], TN, preferred_element_type=jnp.float32)
            dp = lax.dot_general(dqb, pw_ref[g], NT, preferred_element_type=jnp.float32)
            f = dp / jnp.minimum(t_ext + 1, w).astype(jnp.float32)
            span = 1
            while span < w:
                f = f + pltpu.roll(f, ext - span, axis=0)
                span *= 2
            dproj_ref[:, cols] = (f[:tm] - dp[:tm]).astype(dproj_ref.dtype)

        v = v_ref[...]
        rv = _rsqrt_mean_sq(v)
        vhat = v * rv
        gs = gs_ref[...]
        vnb = (vhat * gs).astype(MXU_DTYPE)
        u = u_ref[...]
        dys = dzs_ref[...]
        mask = _chunk_mask(BLK)
        for h in range(H):
            wm = jnp.where(mask, ws_ref[h], 0.0).astype(MXU_DTYPE)
            bias = b_ref[h]
            cols = slice(h * HD, (h + 1) * HD)
            dm_sum = jnp.zeros((BLK, HD), jnp.float32)
            dws_sum = jnp.zeros((BLK, BLK), jnp.float32)
            for n in range(tm // BLK):
                rows = slice(n * BLK, (n + 1) * BLK)
                vblk = vnb[rows, cols]
                mixed = jnp.dot(wm, vblk, preferred_element_type=jnp.float32) + bias
                dy = dys[rows, cols]
                dproj_ref[rows, DP + h * HD:DP + (h + 1) * HD] = (dy * mixed).astype(dproj_ref.dtype)
                dm = dy * u[rows, cols]
                dm_sum = dm_sum + dm
                dmb = dm.astype(MXU_DTYPE)
                dws_sum = dws_sum + lax.dot_general(dmb, vblk, NT, preferred_element_type=jnp.float32)
                dvn_ref[rows, cols] = lax.dot_general(wm, dmb, TN, preferred_element_type=jnp.float32)
            dws_ref[h] += jnp.where(mask, dws_sum, 0.0)
            db_ref[h:h + 1, :] += jnp.sum(dm_sum.T, axis=0, keepdims=True)
        dvn = dvn_ref[...]
        dvg = dvn * gs
        dv = rv * (dvg - vhat * jnp.mean(dvg * vhat, axis=-1, keepdims=True))
        dgs_ref[...] += jnp.sum(dvn * vhat, axis=0, keepdims=True)
        dproj_ref[:, DP + DS:] = dv.astype(dproj_ref.dtype)

    a_spec, prev_spec, next_spec, u_spec, v_spec = _mixer_specs(S, tm, DP, DS)
    dzp_spec = pl.BlockSpec((tm, DP), lambda i: (i, 0))
    dzs_spec = pl.BlockSpec((tm, DS), lambda i: (i, 1))
    small = [pw, ps_row, gs_row, ws, b_full]
    out_shape = [
        jax.ShapeDtypeStruct((S, DP + 2 * DS), MXU_DTYPE),
        jax.ShapeDtypeStruct(pw.shape, jnp.float32),
        jax.ShapeDtypeStruct(ps_row.shape, jnp.float32),
        jax.ShapeDtypeStruct(gs_row.shape, jnp.float32),
        jax.ShapeDtypeStruct(ws.shape, jnp.float32),
        jax.ShapeDtypeStruct((H, BLK), jnp.float32),
    ]
    out_specs = [pl.BlockSpec((tm, DP + 2 * DS), lambda i: (i, 0))] + [_whole(s) for s in out_shape[1:]]
    return pl.pallas_call(
        body, name="mixer_bwd", grid=(nb,),
        in_specs=[a_spec, prev_spec, u_spec, v_spec, dzp_spec, next_spec, dzs_spec] + [_whole(s) for s in small],
        out_specs=out_specs, out_shape=out_shape,
        scratch_shapes=[pltpu.VMEM((tm, DS), jnp.float32)],
        compiler_params=_params(1),
    )(proj, proj, proj, proj, dz, dz, dz, *small)


def _softmax_rows(q, k, scale):
    s = lax.dot_general(q, k, NT, preferred_element_type=jnp.float32) * scale
    e = jnp.exp(s - jnp.max(s, axis=-1, keepdims=True))
    return e / jnp.sum(e, axis=-1, keepdims=True)


def _attn_fwd(q, k, v, tm=512):
    S, D = q.shape
    M = k.shape[0]
    HD = D // N_XATTN_HEADS
    scale = HD ** -0.5
    tm = _tile(S, tm, 16)

    def body(q_ref, k_ref, v_ref, o_ref):
        for h in range(N_XATTN_HEADS):
            cols = slice(h * HD, (h + 1) * HD)
            p = _softmax_rows(q_ref[:, cols], k_ref[:, cols], scale)
            o = jnp.dot(p.astype(MXU_DTYPE), v_ref[:, cols], preferred_element_type=jnp.float32)
            o_ref[:, cols] = o.astype(o_ref.dtype)

    row = pl.BlockSpec((tm, D), lambda i: (i, 0))
    mem = pl.BlockSpec((M, D), lambda i: (0, 0))
    return pl.pallas_call(
        body, name="attn_fwd", grid=(S // tm,), in_specs=[row, mem, mem], out_specs=row,
        out_shape=jax.ShapeDtypeStruct((S, D), MXU_DTYPE), compiler_params=_params(1),
    )(q, k, v)


def _attn_bwd(q, k, v, do, tm=512):
    S, D = q.shape
    M = k.shape[0]
    HD = D // N_XATTN_HEADS
    scale = HD ** -0.5
    tm = _tile(S, tm, 16)

    def body(q_ref, k_ref, v_ref, do_ref, dq_ref, dk_ref, dv_ref):
        @pl.when(pl.program_id(0) == 0)
        def _():
            dk_ref[...] = jnp.zeros_like(dk_ref)
            dv_ref[...] = jnp.zeros_like(dv_ref)

        for h in range(N_XATTN_HEADS):
            cols = slice(h * HD, (h + 1) * HD)
            qh, kh, vh, doh = q_ref[:, cols], k_ref[:, cols], v_ref[:, cols], do_ref[:, cols]
            p = _softmax_rows(qh, kh, scale)
            dp = lax.dot_general(doh, vh, NT, preferred_element_type=jnp.float32)
            dv_ref[:, cols] += lax.dot_general(p.astype(MXU_DTYPE), doh, TN, preferred_element_type=jnp.float32)
            ds = (p * (dp - jnp.sum(dp * p, axis=-1, keepdims=True)) * scale).astype(MXU_DTYPE)
            dq_ref[:, cols] = jnp.dot(ds, kh, preferred_element_type=jnp.float32).astype(dq_ref.dtype)
            dk_ref[:, cols] += lax.dot_general(ds, qh, TN, preferred_element_type=jnp.float32)

    row = pl.BlockSpec((tm, D), lambda i: (i, 0))
    mem = pl.BlockSpec((M, D), lambda i: (0, 0))
    return pl.pallas_call(
        body, name="attn_bwd", grid=(S // tm,), in_specs=[row, mem, mem, row], out_specs=[row, mem, mem],
        out_shape=[jax.ShapeDtypeStruct((S, D), MXU_DTYPE), jax.ShapeDtypeStruct((M, D), jnp.float32),
                   jax.ShapeDtypeStruct((M, D), jnp.float32)],
        compiler_params=_params(1),
    )(q, k, v, do)


def _proj_residual_norm(name, a, w2d, resid, g_row, tm=512, after=None):
    M, K = a.shape
    N = w2d.shape[1]
    tm = _tile(M, tm, 16)

    def epilogue(accs, ex_refs, out_refs, i, j):
        x_new = ex_refs[0][...] + accs[0]
        out_refs[0][...] = x_new
        out_refs[1][...] = (x_new * _rsqrt_mean_sq(x_new) * ex_refs[1][...]).astype(MXU_DTYPE)

    row = lambda i, j, k: (i, 0)
    return _fused_matmul(
        name, (M // tm, 1, 1),
        [(a, (tm, K), row), (w2d, (K, N), lambda i, j, k: (0, 0))],
        [(0, 1, 0, NN)], [(tm, N)],
        [(resid, (tm, N), row), (g_row, (1, N), lambda i, j, k: (0, 0))] + _order_after(after),
        [((M, N), jnp.float32, (tm, N), row), ((M, N), MXU_DTYPE, (tm, N), row)], epilogue)


def _ffn_gate_up(h, wg3, wu3, tm=512):
    M, K = h.shape
    J, _, Nc = wg3.shape
    tm = _tile(M, tm, 16)

    def epilogue(accs, ex_refs, out_refs, i, j):
        gate, up = accs
        sg = _sigmoid(gate)
        silu = gate * sg
        out_refs[0][...] = (silu * up).astype(MXU_DTYPE)
        out_refs[1][...] = (up * (sg * (1.0 + gate * (1.0 - sg)))).astype(MXU_DTYPE)
        out_refs[2][...] = silu.astype(MXU_DTYPE)

    col = lambda i, j, k: (i, j)
    wspec = lambda i, j, k: (j, 0, 0)
    return _fused_matmul(
        "ffn_gate_up", (M // tm, J, 1),
        [(h, (tm, K), lambda i, j, k: (i, 0)), (wg3, (None, K, Nc), wspec), (wu3, (None, K, Nc), wspec)],
        [(0, 1, 0, NN), (0, 2, 1, NN)], [(tm, Nc)] * 2, [],
        [((M, J * Nc), MXU_DTYPE, (tm, Nc), col)] * 3, epilogue)


def _ffn_down(act, wd2d, x2, tm=512, tn=1024):
    M, F = act.shape
    D = wd2d.shape[1]
    tm, tn = _tile(M, tm, 16), _tile(D, tn, LANES)

    def epilogue(accs, ex_refs, out_refs, i, j):
        out_refs[0][...] = ex_refs[0][...] + accs[0]

    tile = lambda i, j, k: (i, j)
    return _fused_matmul(
        "ffn_down", (M // tm, D // tn, 1),
        [(act, (tm, F), lambda i, j, k: (i, 0)), (wd2d, (F, tn), lambda i, j, k: (0, j))],
        [(0, 1, 0, NN)], [(tm, tn)], [(x2, (tm, tn), tile)],
        [((M, D), jnp.float32, (tm, tn), tile)], epilogue)[0]


def _final_norm_loss(x3, target, g_row, tm=256):
    S, D = x3.shape
    tm = _tile(S, tm, 16)

    def body(x_ref, t_ref, g_ref, dx_ref, dxl_ref, loss_ref, dg_ref):
        x3v = x_ref[...]
        g = g_ref[...]
        r = _rsqrt_mean_sq(x3v)
        xhat = x3v * r
        diff = xhat * g - t_ref[...]
        dy = diff / D
        dyg = dy * g
        dx = r * (dyg - xhat * jnp.mean(dyg * xhat, axis=-1, keepdims=True))
        dx_ref[...] = dx
        dxl_ref[...] = dx.astype(MXU_DTYPE)

        @pl.when(pl.program_id(0) == 0)
        def _():
            loss_ref[...] = jnp.zeros_like(loss_ref)
            dg_ref[...] = jnp.zeros_like(dg_ref)

        loss_ref[...] += 0.5 * _sum_all(jnp.mean(diff * diff, axis=-1, keepdims=True))
        dg_ref[...] += jnp.sum(dy * xhat, axis=0, keepdims=True)

    row = pl.BlockSpec((tm, D), lambda i: (i, 0))
    vec = pl.BlockSpec((1, D), lambda i: (0, 0))
    return pl.pallas_call(
        body, name="final_norm_loss", grid=(S // tm,), in_specs=[row, row, vec],
        out_specs=[row, row, pl.BlockSpec((1, 1), lambda i: (0, 0)), vec],
        out_shape=[jax.ShapeDtypeStruct((S, D), jnp.float32), jax.ShapeDtypeStruct((S, D), MXU_DTYPE),
                   jax.ShapeDtypeStruct((1, 1), jnp.float32), jax.ShapeDtypeStruct((1, D), jnp.float32)],
        compiler_params=_params(1),
    )(x3, target, g_row)


def _ffn_down_bwd(dx3_low, wd2d, act_by_gate, act_by_up, n_shards, tm=1024):
    M, D = dx3_low.shape
    F = wd2d.shape[0]
    tn = F // n_shards
    tm = _tile(M, tm, 16)

    def epilogue(accs, ex_refs, out_refs, i, j):
        dact = accs[0]
        out_refs[0][...] = (dact * ex_refs[0][...].astype(jnp.float32)).astype(MXU_DTYPE)
        out_refs[1][...] = (dact * ex_refs[1][...].astype(jnp.float32)).astype(MXU_DTYPE)

    col = lambda i, j, k: (i, j)
    return _fused_matmul(
        "ffn_down_bwd", (M // tm, n_shards, 1),
        [(dx3_low, (tm, D), lambda i, j, k: (i, 0)), (wd2d, (tn, D), lambda i, j, k: (j, 0))],
        [(0, 1, 0, NT)], [(tm, tn)],
        [(act_by_gate, (tm, tn), col), (act_by_up, (tm, tn), col)],
        [((M, F), MXU_DTYPE, (tm, tn), col)] * 2, epilogue)


def _matmul_nt_norm_bwd(name, a, b, xin, g_row, dres, low_dtype, tm=256, after=None):
    M = a.shape[0]
    sharded = b.ndim == 3
    N, Kc = b.shape[-2:]
    J = b.shape[0] if sharded else 1
    tm = _tile(M, tm, 16)
    has_low = low_dtype is not None

    def epilogue(accs, ex_refs, out_refs, i, j):
        dx, dg = _rmsnorm_bwd_values(accs[0], ex_refs[0][...], ex_refs[1][...])
        dx = dx + ex_refs[2][...]
        out_refs[0][...] = dx
        if has_low:
            out_refs[1][...] = dx.astype(low_dtype)
        dg_ref = out_refs[-1]

        @pl.when(i == 0)
        def _():
            dg_ref[...] = jnp.zeros_like(dg_ref)

        dg_ref[...] += dg

    row = lambda i, j, k: (i, 0)
    const = lambda i, j, k: (0, 0)
    b_operand = (b, (None, N, Kc), lambda i, j, k: (k, 0, 0)) if sharded else (b, (N, Kc), const)
    outs = [((M, N), jnp.float32, (tm, N), row)]
    if has_low:
        outs.append(((M, N), low_dtype, (tm, N), row))
    outs.append(((1, N), jnp.float32, (1, N), const))
    return _fused_matmul(
        name, (M // tm, 1, J), [(a, (tm, Kc), lambda i, j, k: (i, k)), b_operand],
        [(0, 1, 0, NT)], [(tm, N)],
        [(xin, (tm, N), row), (g_row, (1, N), const), (dres, (tm, N), row)] + _order_after(after),
        outs, epilogue)


def _row_tile(R, C, target_bytes=2 ** 21):
    return _tile(R, max(16, target_bytes // (4 * C)), 16)


def _prefetch_call(body, name, grid, pos, in_specs, out_specs, out_shape, ins, after=None):
    ordered = [] if after is None else [after]
    grid_spec = pltpu.PrefetchScalarGridSpec(
        num_scalar_prefetch=1, grid=grid, in_specs=list(in_specs) + [ANY] * len(ordered), out_specs=out_specs)
    return pl.pallas_call(body, name=name, grid_spec=grid_spec, out_shape=out_shape,
                          compiler_params=_params(len(grid)))(pos, *ins, *ordered)


def _runs(arrays):
    runs = []
    for i, a in enumerate(arrays):
        if runs and arrays[runs[-1][-1]].shape == a.shape:
            runs[-1].append(i)
        else:
            runs.append([i])
    return runs


def _per_run(fn, names, *array_lists):
    out = [None] * len(names)
    for run in _runs(array_lists[0]):
        results = fn("_".join(names[i] for i in run), *[[arrays[i] for i in run] for arrays in array_lists])
        for i, r in zip(run, results):
            out[i] = r
    return out


def _cast_into_slabs(name, xs, dtype, pos, after=None):
    n = len(xs)
    R, C = xs[0].shape
    tr = _row_tile(R, C * n, 2 ** 22)

    def body(pos_ref, *refs):
        for t in range(n):
            refs[len(refs) - n + t][...] = refs[t][...].astype(dtype)

    return _prefetch_call(
        body, "cast_" + name, (R // tr,), pos, [pl.BlockSpec((tr, C), lambda i, pos: (i, 0))] * n,
        [pl.BlockSpec((None, tr, C), lambda i, pos: (pos[0], i, 0))] * n,
        [jax.ShapeDtypeStruct((N_CHIPS, R, C), dtype)] * n, xs, after)


def _chip_sum(name, partials, gots, dtype, pos):
    n = len(partials)
    J, R, C = partials[0].shape
    Rh = R // 2
    tr = _row_tile(Rh, C * n, 2 ** 22)

    def body(pos_ref, *refs):
        for t in range(n):
            refs[2 * n + t][...] = (refs[t][...].astype(jnp.float32) + refs[n + t][...].astype(jnp.float32)).astype(dtype)

    blk = pl.BlockSpec((None, tr, C), lambda j, i, pos: (j, i, 0))
    return _prefetch_call(
        body, "chip_sum_" + name, (J, Rh // tr), pos,
        [pl.BlockSpec((None, None, tr, C), lambda j, i, pos: (j, pos[1], i, 0))] * n + [blk] * n, [blk] * n,
        [jax.ShapeDtypeStruct((J, Rh, C), dtype)] * n, [p.reshape(J, 2, Rh, C) for p in partials] + list(gots))


def _owner_sum(name, chip_sums, others, pos):
    n = len(chip_sums)
    J, Rh, C = chip_sums[0].shape
    n_other = others[0].shape[0]
    tr = _row_tile(Rh, C * (n_other + 1) * n, 2 ** 22)

    def body(pos_ref, *refs):
        for t in range(n):
            acc = refs[t][...].astype(jnp.float32)
            for k in range(n_other):
                acc = acc + refs[n + t][k].astype(jnp.float32)
            refs[2 * n + t][...] = acc

    return _prefetch_call(
        body, "owner_sum_" + name, (Rh // tr,), pos,
        [pl.BlockSpec((None, tr, C), lambda i, pos: (pos[0], i, 0))] * n
        + [pl.BlockSpec((n_other, tr, C), lambda i, pos: (0, i, 0))] * n,
        [pl.BlockSpec((None, tr, C), lambda i, pos: (pos[1], i, 0))] * n,
        [jax.ShapeDtypeStruct((2, Rh, C), jnp.float32)] * n, list(chip_sums) + list(others))


def _adamw_values(w, g, m, v):
    m = ADAM_B1 * m + (1.0 - ADAM_B1) * g
    v = ADAM_B2 * v + (1.0 - ADAM_B2) * (g * g)
    m_hat = m / (1.0 - ADAM_B1 ** ADAM_STEP)
    v_hat = v / (1.0 - ADAM_B2 ** ADAM_STEP)
    delta = -ADAM_LR * (m_hat / (jnp.sqrt(v_hat) + ADAM_EPS) + ADAM_WD * w)
    return delta, m, v


def _adamw(name, ws, gs, ms, vs):
    n = len(ws)
    R, C = ws[0].shape
    tr = _row_tile(R, C * n, 2 ** 20)

    def body(*refs):
        w_refs, g_refs, m_refs, v_refs = (refs[q * n:(q + 1) * n] for q in range(4))
        outs = refs[4 * n:]
        for t in range(n):
            g = g_refs[t][...]
            outs[4 * t][...] = g
            outs[4 * t + 1][...], outs[4 * t + 2][...], outs[4 * t + 3][...] = _adamw_values(
                w_refs[t][...], g, m_refs[t][...], v_refs[t][...])

    blk = pl.BlockSpec((tr, C), lambda i: (i, 0))
    res = pl.pallas_call(
        body, name="adamw_" + name, grid=(R // tr,), in_specs=[blk] * (4 * n), out_specs=[blk] * (4 * n),
        out_shape=[jax.ShapeDtypeStruct((R, C), jnp.float32)] * (4 * n), compiler_params=_params(1),
    )(*ws, *gs, *ms, *vs)
    return [tuple(res[4 * t:4 * t + 4]) for t in range(n)]


def _adamw_small(g_packed, ws, ms, vs, row_counts):
    n = len(ws)

    def body(*refs):
        g_ref = refs[0]
        w_refs, m_refs, v_refs = refs[1:1 + n], refs[1 + n:1 + 2 * n], refs[1 + 2 * n:1 + 3 * n]
        outs = refs[1 + 3 * n:]
        off = 0
        for t in range(n):
            g = g_ref[off:off + row_counts[t], :]
            off += row_counts[t]
            delta, m_new, v_new = _adamw_values(w_refs[t][...], g, m_refs[t][...], v_refs[t][...])
            outs[4 * t][...] = g
            outs[4 * t + 1][...] = delta
            outs[4 * t + 2][...] = m_new
            outs[4 * t + 3][...] = v_new

    out_shape = []
    for t in range(n):
        out_shape += [jax.ShapeDtypeStruct((row_counts[t], LANES), jnp.float32)] * 4
    return pl.pallas_call(
        body, name="adamw_small", in_specs=[VMEM_WHOLE] * (1 + 3 * n), out_specs=[VMEM_WHOLE] * (4 * n),
        out_shape=out_shape, compiler_params=pltpu.CompilerParams(vmem_limit_bytes=VMEM_LIMIT_BYTES),
    )(g_packed, *ws, *ms, *vs)


def _position():
    x, y, c = lax.axis_index("x"), lax.axis_index("y"), lax.axis_index("c")
    other_chips = [(1 - x, y), (x, 1 - y), (1 - x, 1 - y)]
    return x, y, c, other_chips


def _half(rows, which):
    return pl.ds(which * (rows // 2), rows // 2)


def _chunk_index(shape, dtype):
    rows = shape[-2]
    total = math.prod(shape) * jnp.dtype(dtype).itemsize
    n = max(1, min(rows // COPY_ROW_ALIGN, total // COPY_CHUNK_BYTES))
    step = -(-rows // n)
    step = -(-step // COPY_ROW_ALIGN) * COPY_ROW_ALIGN
    lead = (slice(None),) * (len(shape) - 2)
    return [lead + (pl.ds(s, min(step, rows - s)),) for s in range(0, rows, step)]


def _start_in_chunks(make, src, dst, waited_here=True):
    for idx in _chunk_index(src.shape, src.dtype):
        make(src.at[idx], dst.at[idx]).start()
    return make(src, dst) if waited_here else None


def _remote_copier(send_sem, recv_sem, to):
    return lambda src, dst: pltpu.make_async_remote_copy(
        src_ref=src, dst_ref=dst, send_sem=send_sem, recv_sem=recv_sem, device_id=to, device_id_type=MESH)


def _slab_half(ref, chip, which):
    return ref.at[chip, _half(ref.shape[1], which)]


def _slab_quarter(ref, chip, which, q):
    rows = ref.shape[1] // 4
    return ref.at[chip, pl.ds(which * 2 * rows + q * rows, rows)]


def _gather_step(name, arrived=(), ring=(), direct=(), after=None):
    groups = [g[2] for g in arrived] + list(direct) + list(ring)
    per_tensor = [2] * len(arrived) + [3] * len(direct) + [2] * len(ring)
    flat = [a for g in groups for a in g]
    T, G, A = len(flat), len(groups), len(arrived)
    ordered = [] if after is None else [after]
    sems_in = [sem for g in arrived for sem in g[:2]]

    def body(*refs):
        ins = refs[:T]
        first_hop = refs[T:T + 2 * A]
        out_sems = refs[T + 2 * A + len(ordered):T + 2 * A + len(ordered) + 2 * G]
        token = refs[-1]
        x, y, c, other_chips = _position()
        mine = 2 * x + y
        x_nbr, y_nbr = (1 - x, y, c), (x, 1 - y, c)
        from_x, from_y = 2 * (1 - x) + y, 2 * x + (1 - y)
        t0 = 0
        for gi, group in enumerate(groups):
            send, recv = out_sems[2 * gi], out_sems[2 * gi + 1]
            for t in range(len(group)):
                ref = ins[t0 + t]
                own = _slab_half(ref, mine, c)
                if gi < A:
                    send1, recv1 = first_hop[2 * gi], first_hop[2 * gi + 1]
                    for k, (peer, chip) in enumerate(((x_nbr, from_x), (y_nbr, from_y))):
                        cp = _remote_copier(send1.at[2 * t + k], recv1.at[2 * t + k], peer)(own, _slab_half(ref, chip, c))
                        cp.wait_send()
                        cp.wait_recv()
                    for k, (peer, chip) in enumerate(((y_nbr, from_x), (x_nbr, from_y))):
                        piece = _slab_quarter(ref, chip, c, k)
                        _start_in_chunks(_remote_copier(send.at[2 * t + k], recv.at[2 * t + k], peer), piece, piece,
                                         waited_here=False)
                elif gi < A + len(direct):
                    for k, (cx, cy) in enumerate(other_chips):
                        _start_in_chunks(_remote_copier(send.at[3 * t + k], recv.at[3 * t + k], (cx, cy, c)), own, own,
                                         waited_here=False)
                else:
                    for k, peer in enumerate((x_nbr, y_nbr)):
                        _start_in_chunks(_remote_copier(send.at[2 * t + k], recv.at[2 * t + k], peer), own, own,
                                         waited_here=False)
            t0 += len(group)
        token[...] = jnp.zeros_like(token)

    sem_shapes = []
    for group, n in zip(groups, per_tensor):
        sem_shapes += [pltpu.SemaphoreType.DMA((n * len(group),))] * 2
    res = pl.pallas_call(
        body, name=name, in_specs=[HBM_SPEC] * T + [SEM_SPEC] * (2 * A) + [ANY] * len(ordered),
        out_specs=[SEM_SPEC] * (2 * G) + [HBM_SPEC] * T + [VMEM_WHOLE],
        out_shape=sem_shapes + [pltpu.HBM(a.shape, a.dtype) for a in flat] + [jax.ShapeDtypeStruct((8, LANES), jnp.float32)],
        input_output_aliases={t: 2 * G + t for t in range(T)},
        compiler_params=pltpu.CompilerParams(has_side_effects=SIDE_EFFECT),
    )(*[pltpu.with_memory_space_constraint(a, pltpu.HBM) for a in flat], *sems_in, *ordered)
    flights, t0 = [], 2 * G
    for gi, group in enumerate(groups):
        flights.append((res[2 * gi], res[2 * gi + 1], res[t0:t0 + len(group)]))
        t0 += len(group)
    return flights[:A], flights[A:A + len(direct)], flights[A + len(direct):], res[-1]


def _relayed_wait(name, flight, after):
    send_sems, recv_sems, slabs = flight
    T = len(slabs)

    def body(*refs):
        ins = refs[:T]
        send, recv = refs[T:T + 2]
        x, y, c, _ = _position()
        x_nbr, y_nbr = (1 - x, y, c), (x, 1 - y, c)
        from_x, from_y, diagonal = 2 * (1 - x) + y, 2 * x + (1 - y), 2 * (1 - x) + (1 - y)
        for t in range(T):
            for k, (peer, chip) in enumerate(((y_nbr, from_x), (x_nbr, from_y))):
                cp = _remote_copier(send.at[2 * t + k], recv.at[2 * t + k], peer)(
                    _slab_quarter(ins[t], chip, c, k), _slab_quarter(ins[t], diagonal, c, k))
                cp.wait_send()
                cp.wait_recv()

    return pl.pallas_call(
        body, name=name, in_specs=[HBM_SPEC] * T + [SEM_SPEC, SEM_SPEC, ANY], out_specs=[HBM_SPEC] * T,
        out_shape=[pltpu.HBM(a.shape, a.dtype) for a in slabs], input_output_aliases={t: t for t in range(T)},
        compiler_params=pltpu.CompilerParams(has_side_effects=SIDE_EFFECT),
    )(*slabs, send_sems, recv_sems, after)


def _gather_wait(name, flight, after):
    send_sems, recv_sems, slabs = flight
    T = len(slabs)

    def body(*refs):
        ins = refs[:T]
        send, recv = refs[T:T + 2]
        x, y, c, other_chips = _position()
        mine = 2 * x + y
        for t in range(T):
            own = _slab_half(ins[t], mine, c)
            for k, (cx, cy) in enumerate(other_chips):
                landed = _slab_half(ins[t], 2 * cx + cy, c)
                cp = _remote_copier(send.at[3 * t + k], recv.at[3 * t + k], (cx, cy, c))(own, landed)
                cp.wait_send()
                cp.wait_recv()

    return pl.pallas_call(
        body, name=name, in_specs=[HBM_SPEC] * T + [SEM_SPEC, SEM_SPEC, ANY], out_specs=[HBM_SPEC] * T,
        out_shape=[pltpu.HBM(s.shape, s.dtype) for s in slabs], input_output_aliases={t: t for t in range(T)},
        compiler_params=pltpu.CompilerParams(has_side_effects=SIDE_EFFECT),
    )(*slabs, send_sems, recv_sems, after)


def _share_with_sibling(name, slabs):
    T = len(slabs)

    def body(*refs):
        outs = refs[T:2 * T]
        send_sems, recv_sems = refs[2 * T:]
        x, y, c, other_chips = _position()
        sibling = (x, y, 1 - c)
        started = []
        for t in range(T):
            for k, (cx, cy) in enumerate(other_chips):
                landed = _slab_half(outs[t], 2 * cx + cy, c)
                started.append(_start_in_chunks(_remote_copier(send_sems.at[t, k], recv_sems.at[t, k], sibling), landed, landed))
        for t in range(T):
            for k, (cx, cy) in enumerate(other_chips):
                arriving = _slab_half(outs[t], 2 * cx + cy, 1 - c)
                _remote_copier(send_sems.at[t, k], recv_sems.at[t, k], sibling)(arriving, arriving).wait_recv()
        for cp in started:
            cp.wait_send()

    return pl.pallas_call(
        body, name=name, in_specs=[ANY] * T, out_specs=[ANY] * T,
        out_shape=[jax.ShapeDtypeStruct(s.shape, s.dtype) for s in slabs],
        input_output_aliases={t: t for t in range(T)},
        scratch_shapes=[pltpu.SemaphoreType.DMA((T, 3)), pltpu.SemaphoreType.DMA((T, 3))],
    )(*slabs)


def _swap_halves(name, grads):
    T = len(grads)

    def body(*refs):
        ins, got = refs[:T], refs[T:2 * T]
        send_sems, recv_sems = refs[2 * T:]
        x, y, c, _ = _position()
        remote = []
        for t in range(T):
            rows = ins[t].shape[1]
            remote.append(_start_in_chunks(
                _remote_copier(send_sems.at[t], recv_sems.at[t], (x, y, 1 - c)), ins[t].at[:, _half(rows, 1 - c)], got[t]))
        for cp in remote:
            cp.wait()

    return pl.pallas_call(
        body, name=name, in_specs=[ANY] * T, out_specs=[ANY] * T,
        out_shape=[jax.ShapeDtypeStruct((g.shape[0], g.shape[1] // 2, g.shape[2]), g.dtype) for g in grads],
        scratch_shapes=[pltpu.SemaphoreType.DMA((T,)), pltpu.SemaphoreType.DMA((T,))],
    )(*grads)


def _swap_start(name, grads):
    T = len(grads)
    lands = [lax.empty((g.shape[0], g.shape[1] // 2, g.shape[2]), g.dtype) for g in grads]

    def body(*refs):
        ins, got = refs[:T], refs[T:2 * T]
        send_sems, recv_sems = refs[2 * T:2 * T + 2]
        token = refs[-1]
        x, y, c, _ = _position()
        for t in range(T):
            _start_in_chunks(_remote_copier(send_sems.at[t], recv_sems.at[t], (x, y, 1 - c)),
                             ins[t].at[:, _half(ins[t].shape[1], 1 - c)], got[t], waited_here=False)
        token[...] = jnp.zeros_like(token)

    arrays = list(grads) + lands
    res = pl.pallas_call(
        body, name=name, in_specs=[HBM_SPEC] * (2 * T),
        out_specs=[SEM_SPEC, SEM_SPEC] + [HBM_SPEC] * (2 * T) + [VMEM_WHOLE],
        out_shape=[pltpu.SemaphoreType.DMA((T,)), pltpu.SemaphoreType.DMA((T,))]
        + [pltpu.HBM(a.shape, a.dtype) for a in arrays] + [jax.ShapeDtypeStruct((8, LANES), jnp.float32)],
        input_output_aliases={i: 2 + i for i in range(2 * T)},
        compiler_params=pltpu.CompilerParams(has_side_effects=SIDE_EFFECT),
    )(*[pltpu.with_memory_space_constraint(a, pltpu.HBM) for a in arrays])
    return res[0], res[1], res[2:2 + T], res[2 + T:2 + 2 * T], res[-1]


def _swap_wait(name, flight, after):
    send_sems, recv_sems, grads, lands, _ = flight
    T = len(grads)

    def body(*refs):
        ins, got = refs[:T], refs[T:2 * T]
        send, recv = refs[2 * T:2 * T + 2]
        x, y, c, _ = _position()
        for t in range(T):
            cp = _remote_copier(send.at[t], recv.at[t], (x, y, 1 - c))(ins[t].at[:, _half(ins[t].shape[1], 1 - c)], got[t])
            cp.wait_send()
            cp.wait_recv()

    arrays = list(grads) + list(lands)
    res = pl.pallas_call(
        body, name=name, in_specs=[HBM_SPEC] * (2 * T) + [SEM_SPEC, SEM_SPEC, ANY], out_specs=[HBM_SPEC] * (2 * T),
        out_shape=[pltpu.HBM(a.shape, a.dtype) for a in arrays], input_output_aliases={i: i for i in range(2 * T)},
        compiler_params=pltpu.CompilerParams(has_side_effects=SIDE_EFFECT),
    )(*arrays, send_sems, recv_sems, after)
    return res[:T], res[T:]


def _scatter_start(name, chip_sums):
    T = len(chip_sums)
    lands = [lax.empty((3,) + s.shape[1:], s.dtype) for s in chip_sums]

    def body(*refs):
        srcs, dsts = refs[:T], refs[T:2 * T]
        send_sems, recv_sems = refs[2 * T:2 * T + 2]
        token = refs[-1]
        x, y, c, other_chips = _position()
        for t in range(T):
            for k, (cx, cy) in enumerate(other_chips):
                _start_in_chunks(_remote_copier(send_sems.at[3 * t + k], recv_sems.at[3 * t + k], (cx, cy, c)),
                                 srcs[t].at[2 * cx + cy], dsts[t].at[k], waited_here=False)
        token[...] = jnp.zeros_like(token)

    hbm = [pltpu.HBM(a.shape, a.dtype) for a in list(chip_sums) + lands]
    res = pl.pallas_call(
        body, name=name, in_specs=[HBM_SPEC] * (2 * T),
        out_specs=[SEM_SPEC, SEM_SPEC] + [HBM_SPEC] * (2 * T) + [VMEM_WHOLE],
        out_shape=[pltpu.SemaphoreType.DMA((3 * T,)), pltpu.SemaphoreType.DMA((3 * T,))] + hbm
        + [jax.ShapeDtypeStruct((8, LANES), jnp.float32)],
        input_output_aliases={i: 2 + i for i in range(2 * T)},
        compiler_params=pltpu.CompilerParams(has_side_effects=SIDE_EFFECT),
    )(*[pltpu.with_memory_space_constraint(a, pltpu.HBM) for a in list(chip_sums) + lands])
    return res[0], res[1], res[2:2 + T], res[2 + T:2 + 2 * T], res[-1]


def _scatter_wait(name, send_sems, recv_sems, chip_sums, lands, after):
    T = len(chip_sums)

    def body(*refs):
        srcs, dsts = refs[:T], refs[T:2 * T]
        send, recv = refs[2 * T:2 * T + 2]
        x, y, c, other_chips = _position()
        for t in range(T):
            for k, (cx, cy) in enumerate(other_chips):
                cp = _remote_copier(send.at[3 * t + k], recv.at[3 * t + k], (cx, cy, c))(srcs[t].at[2 * cx + cy], dsts[t].at[k])
                cp.wait_send()
                cp.wait_recv()

    arrays = list(chip_sums) + list(lands)
    res = pl.pallas_call(
        body, name=name, in_specs=[HBM_SPEC] * (2 * T) + [SEM_SPEC, SEM_SPEC, ANY],
        out_specs=[HBM_SPEC] * (2 * T), out_shape=[pltpu.HBM(a.shape, a.dtype) for a in arrays],
        input_output_aliases={i: i for i in range(2 * T)},
        compiler_params=pltpu.CompilerParams(has_side_effects=SIDE_EFFECT),
    )(*arrays, send_sems, recv_sems, after)
    return res[:T], res[T:]


def _as_shards(partial):
    return [p.reshape((N_CHIPS, -1, p.shape[-1])) for p in partial]


def _reduce_begin(tag, names, partial, pos):
    parts = _as_shards(partial)
    received = _swap_halves("swap_grad_halves_" + tag, parts)
    return _reduce_scatter_start(tag, names, parts, received, pos)


def _reduce_scatter_start(tag, names, parts, received, pos):
    chip_sums = _per_run(lambda name, p, r: _chip_sum(name, p, r, WIRE_DTYPE, pos), names, list(parts), list(received))
    return _scatter_start("scatter_start_" + tag, chip_sums)


def _reduce_end(tag, names, flight, after, pos):
    send_sems, recv_sems, chip_sums, lands, _ = flight
    chip_sums, others = _scatter_wait("scatter_wait_" + tag, send_sems, recv_sems, chip_sums, lands, after)
    reduced = _per_run(lambda name, cs, ot: _owner_sum(name, cs, ot, pos), names, list(chip_sums), list(others))
    return dict(zip(names, reduced))


def _join_start(name, halves):
    T = len(halves)

    def body(*refs):
        ins = refs[:T]
        send_sems, recv_sems = refs[T:T + 2]
        token = refs[-1]
        x, y, c, _ = _position()
        for t in range(T):
            _start_in_chunks(_remote_copier(send_sems.at[t], recv_sems.at[t], (x, y, 1 - c)),
                             ins[t].at[c], ins[t].at[c], waited_here=False)
        token[...] = jnp.zeros_like(token)

    res = pl.pallas_call(
        body, name=name, in_specs=[HBM_SPEC] * T,
        out_specs=[SEM_SPEC, SEM_SPEC] + [HBM_SPEC] * T + [VMEM_WHOLE],
        out_shape=[pltpu.SemaphoreType.DMA((T,)), pltpu.SemaphoreType.DMA((T,))]
        + [pltpu.HBM(h.shape, h.dtype) for h in halves] + [jax.ShapeDtypeStruct((8, LANES), jnp.float32)],
        input_output_aliases={t: 2 + t for t in range(T)},
        compiler_params=pltpu.CompilerParams(has_side_effects=SIDE_EFFECT),
    )(*[pltpu.with_memory_space_constraint(h, pltpu.HBM) for h in halves])
    return res[0], res[1], res[2:2 + T], res[-1]


def _join_wait(name, flight, after):
    send_sems, recv_sems, halves, _ = flight
    T = len(halves)

    def body(*refs):
        ins = refs[:T]
        send, recv = refs[T:T + 2]
        x, y, c, _ = _position()
        for t in range(T):
            cp = _remote_copier(send.at[t], recv.at[t], (x, y, 1 - c))(ins[t].at[c], ins[t].at[1 - c])
            cp.wait_send()
            cp.wait_recv()

    return pl.pallas_call(
        body, name=name, in_specs=[HBM_SPEC] * T + [SEM_SPEC, SEM_SPEC, ANY], out_specs=[HBM_SPEC] * T,
        out_shape=[pltpu.HBM(h.shape, h.dtype) for h in halves], input_output_aliases={t: t for t in range(T)},
        compiler_params=pltpu.CompilerParams(has_side_effects=SIDE_EFFECT),
    )(*halves, send_sems, recv_sems, after)


def _peers():
    x, y, c, _ = _position()
    flips = [(fx, fy, fc) for fx in (0, 1) for fy in (0, 1) for fc in (0, 1)][1:]
    return [(jnp.bitwise_xor(x, fx), jnp.bitwise_xor(y, fy), jnp.bitwise_xor(c, fc)) for fx, fy, fc in flips]


def _device_index(px, py, pc):
    return 4 * px + 2 * py + pc


def _exchange_start(name, slots):
    def body(slots_ref, send_sems, recv_sems, thru, token):
        x, y, c, _ = _position()
        own = slots_ref.at[_device_index(x, y, c)]
        for k, peer in enumerate(_peers()):
            _remote_copier(send_sems.at[k], recv_sems.at[k], peer)(own, own).start()
        token[...] = jnp.zeros_like(token)

    res = pl.pallas_call(
        body, name=name, in_specs=[HBM_SPEC], out_specs=[SEM_SPEC, SEM_SPEC, HBM_SPEC, VMEM_WHOLE],
        out_shape=[pltpu.SemaphoreType.DMA((N_DEV - 1,)), pltpu.SemaphoreType.DMA((N_DEV - 1,)),
                   pltpu.HBM(slots.shape, slots.dtype), jax.ShapeDtypeStruct((8, LANES), jnp.float32)],
        input_output_aliases={0: 2}, compiler_params=pltpu.CompilerParams(has_side_effects=SIDE_EFFECT),
    )(pltpu.with_memory_space_constraint(slots, pltpu.HBM))
    return res[0], res[1], res[2], res[3]


def _exchange_wait(name, flight, after):
    send_sems, recv_sems, slots, _ = flight

    def body(slots_ref, send, recv, after_ref, out_ref):
        x, y, c, _ = _position()
        own = slots_ref.at[_device_index(x, y, c)]
        for k, peer in enumerate(_peers()):
            cp = _remote_copier(send.at[k], recv.at[k], peer)(own, slots_ref.at[_device_index(*peer)])
            cp.wait_send()
            cp.wait_recv()

    return pl.pallas_call(
        body, name=name, in_specs=[HBM_SPEC, SEM_SPEC, SEM_SPEC, ANY], out_specs=HBM_SPEC,
        out_shape=pltpu.HBM(slots.shape, slots.dtype), input_output_aliases={0: 0},
        compiler_params=pltpu.CompilerParams(has_side_effects=SIDE_EFFECT),
    )(slots, send_sems, recv_sems, after)


def _sum_slots(name, slots):
    n, rows, lanes = slots.shape
    tr = _row_tile(rows, lanes * n)

    def body(s_ref, o_ref):
        acc = s_ref[0]
        for d in range(1, n):
            acc = acc + s_ref[d]
        o_ref[...] = acc

    return pl.pallas_call(
        body, name=name, grid=(rows // tr,), in_specs=[pl.BlockSpec((n, tr, lanes), lambda i: (0, i, 0))],
        out_specs=pl.BlockSpec((tr, lanes), lambda i: (i, 0)), out_shape=jax.ShapeDtypeStruct((rows, lanes), jnp.float32),
        compiler_params=_params(1))(slots)


BIG = ("w_in", "pool_w", "w_out", "w_q", "w_k", "w_v", "w_o", "w_gate", "w_up", "w_down")
SMALL = ("norm_mix_g", "pool_scale", "sgu_norm_g", "w_spatial", "b_spatial", "norm_xattn_g", "norm_mem_g",
         "norm_ffn_g", "final_norm_g")
GATHER_ORDER = (("w_in", "pool_w"), ("w_out",), ("w_q", "w_k", "w_v", "w_o"), ("w_gate", "w_up"), ("w_down",))
WEIGHTS = ("norm_mix_g", "w_in", "pool_w", "pool_scale", "sgu_norm_g", "w_spatial", "b_spatial", "w_out",
           "norm_xattn_g", "norm_mem_g", "w_q", "w_k", "w_v", "w_o", "norm_ffn_g", "w_gate", "w_up", "w_down",
           "final_norm_g")


def _as2d(a):
    return a.reshape(-1, a.shape[-1])


def _as_lanes(a):
    return a.reshape(-1, LANES)


def kernel(x, mem, norm_mix_g, w_in, pool_w, pool_scale, sgu_norm_g, w_spatial, b_spatial, w_out, norm_xattn_g, norm_mem_g, w_q, w_k, w_v, w_o, norm_ffn_g, w_gate, w_up, w_down, final_norm_g, loss_target, m_norm_mix_g, m_w_in, m_pool_w, m_pool_scale, m_sgu_norm_g, m_w_spatial, m_b_spatial, m_w_out, m_norm_xattn_g, m_norm_mem_g, m_w_q, m_w_k, m_w_v, m_w_o, m_norm_ffn_g, m_w_gate, m_w_up, m_w_down, m_final_norm_g, v_norm_mix_g, v_w_in, v_pool_w, v_pool_scale, v_sgu_norm_g, v_w_spatial, v_b_spatial, v_w_out, v_norm_xattn_g, v_norm_mem_g, v_w_q, v_w_k, v_w_v, v_w_o, v_norm_ffn_g, v_w_gate, v_w_up, v_w_down, v_final_norm_g):
    args = dict(locals())
    weights = {n: args[n] for n in WEIGHTS}
    moments_m = {n: args["m_" + n] for n in WEIGHTS}
    moments_v = {n: args["v_" + n] for n in WEIGHTS}
    low = MXU_DTYPE

    xs = x[0]
    mems = mem[0]
    target = loss_target[0]
    D = xs.shape[1]
    G, PG = pool_w.shape[1], pool_w.shape[3]

    pos = jnp.stack([2 * lax.axis_index("x") + lax.axis_index("y"), lax.axis_index("c")]).astype(jnp.int32)
    def cast(names, after=None):
        slabs = _per_run(lambda name, xs: _cast_into_slabs(name, xs, WIRE_DTYPE, pos, after), names,
                         [_as2d(weights[n]) for n in names])
        return dict(zip(names, slabs))

    def land(gi, flight, after, relayed=True):
        wait = _relayed_wait if relayed else _gather_wait
        landed = wait("gather_wait_%d" % gi, flight, after)
        return dict(zip(GATHER_ORDER[gi], _share_with_sibling("gather_share_%d" % gi, landed)))

    def rows(a):
        return a.reshape(-1, a.shape[-1])

    g_mix, g_xattn, g_mem, g_ffn = norm_mix_g, norm_xattn_g, norm_mem_g, norm_ffn_g
    g_final = final_norm_g.reshape(1, D)
    ws = w_spatial[0]
    b_full = jnp.broadcast_to(b_spatial[0][:, :, None], ws.shape[:2] + (sgu_norm_g.shape[1] // ws.shape[0],))

    first = cast(GATHER_ORDER[0])
    _, (sent_0,), _, started = _gather_step("gather_step_0", direct=[[first[n] for n in GATHER_ORDER[0]]])
    rest = cast([n for group in GATHER_ORDER[1:3] for n in group], after=started)
    _, (sent_1,), (hop_2,), started = _gather_step(
        "gather_step_1", direct=[[rest[n] for n in GATHER_ORDER[1]]], ring=[[rest[n] for n in GATHER_ORDER[2]]])
    h1 = _rmsnorm("norm_mix", xs, g_mix, low, after=started)
    mn = _rmsnorm("norm_mem", mems, g_mem, low)
    rest.update(cast(GATHER_ORDER[3], after=h1))
    rest.update(cast(GATHER_ORDER[4], after=rest[GATHER_ORDER[3][-1]]))
    slabs = [[rest[n] for n in group] for group in GATHER_ORDER[1:]]
    got = land(0, sent_0, rest["w_down"], relayed=False)
    wg_in = got["w_in"]
    pw = got["pool_w"].reshape(N_CHIPS, G, PG // N_CHIPS, PG).transpose(1, 0, 2, 3).reshape(G, PG, PG)
    proj = _matmul_nn_cols("proj_in", h1, wg_in, jnp.float32)
    z = _mixer_fwd(proj, pw, pool_scale, sgu_norm_g, ws, b_full)
    full = {"w_out": rows(land(1, sent_1, z, relayed=False)["w_out"])}
    (relay_2,), _, (hop_3,), started = _gather_step("gather_step_2", arrived=[hop_2], ring=slabs[2:3], after=full["w_out"])
    x1, h2 = _proj_residual_norm("mix_out", z, full["w_out"], xs, g_xattn, after=started)
    full.update({n: rows(a) for n, a in land(2, relay_2, h2).items()})
    q = _matmul_nn("xattn_q", h2, full["w_q"], low)
    k = _matmul_nn("xattn_k", mn, full["w_k"], low)
    vv = _matmul_nn("xattn_v", mn, full["w_v"], low)
    o = _attn_fwd(q, k, vv)
    (relay_3,), (sent_4,), _, started = _gather_step("gather_step_3", arrived=[hop_3], direct=slabs[3:4], after=o)
    x2, h3 = _proj_residual_norm("xattn_out", o, full["w_o"], x1, g_ffn, after=started)
    got = land(3, relay_3, h3)
    wg_gate, wg_up = got["w_gate"], got["w_up"]
    act, act_by_gate, act_by_up = _ffn_gate_up(h3, wg_gate, wg_up)
    full["w_down"] = rows(land(4, sent_4, act, relayed=False)["w_down"])
    x3 = _ffn_down(act, full["w_down"], x2)
    dx3, dx3_low, loss_part, d_final_g = _final_norm_loss(x3, target, g_final)

    dgate, dup = _ffn_down_bwd(dx3_low, full["w_down"], act_by_gate, act_by_up, N_CHIPS)
    (d_w_down,) = _matmul_tn("grad_w_down", act, [dx3_low], WIRE_DTYPE)
    (d_w_gate,) = _matmul_tn_cols("grad_w_gate", h3, [dgate], N_CHIPS, WIRE_DTYPE)
    (d_w_up,) = _matmul_tn_cols("grad_w_up", h3, [dup], N_CHIPS, WIRE_DTYPE)
    ffn_names = ("w_gate", "w_up", "w_down")
    ffn_swap = _swap_start("swap_start_ffn", _as_shards([d_w_gate, d_w_up, d_w_down]))
    dh3 = _matmul_nt_cols("ffn_dh", [dgate, dup], [wg_gate, wg_up], jnp.float32, tn=1024, after=ffn_swap[-1])
    ffn_flight = _reduce_scatter_start("ffn", ffn_names, *_swap_wait("swap_wait_ffn", ffn_swap, dh3), pos)
    dx2, dx2_low, d_ffn_g = _rmsnorm_bwd("norm_ffn_bwd", dh3, x2, g_ffn, dx3, low)

    do = _matmul_nt("xattn_do", [dx2_low], [full["w_o"]], low, after=ffn_flight[-1])
    (d_w_o,) = _matmul_tn("grad_w_o", o, [dx2_low], WIRE_DTYPE)
    dq, dk, dvv = _attn_bwd(q, k, vv, do)
    (d_w_q,) = _matmul_tn("grad_w_q", h2, [dq], WIRE_DTYPE)
    dx1, dx1_low, d_xattn_g = _matmul_nt_norm_bwd("xattn_dh_norm_bwd", dq, full["w_q"], x1, g_xattn, dx2, low)
    d_w_k, d_w_v = _matmul_tn("grad_w_kv", mn, [dk, dvv], WIRE_DTYPE)
    dmn = _matmul_nt("xattn_dmem", [dk, dvv], [full["w_k"], full["w_v"]], jnp.float32)
    (d_mem_g,) = _rmsnorm_bwd("norm_mem_bwd", dmn, mems, g_mem, None, None)

    (d_w_out,) = _matmul_tn("grad_w_out", z, [dx1_low], WIRE_DTYPE)
    attn_names = ("w_q", "w_k", "w_v", "w_o", "w_out")
    attn_swap = _swap_start("swap_start_attn", _as_shards([d_w_q, d_w_k, d_w_v, d_w_o, d_w_out]))
    dz = _matmul_nt("mix_dz", [dx1_low], [full["w_out"]], jnp.float32, after=attn_swap[-1])
    attn_flight = _reduce_scatter_start("attn", attn_names, *_swap_wait("swap_wait_attn", attn_swap, dz), pos)
    dproj, d_pw, d_ps, d_gs, d_ws, d_b = _mixer_bwd(proj, dz, pw, pool_scale, sgu_norm_g, ws, b_full)
    (d_w_in,) = _matmul_tn_cols("grad_w_in", h1, [dproj], N_CHIPS, WIRE_DTYPE, tka=1024, after=attn_flight[-1])
    d_pool_w = d_pw.reshape(G, N_CHIPS, PG // N_CHIPS, PG).transpose(1, 0, 2, 3).reshape(N_CHIPS, PG, PG).astype(WIRE_DTYPE)
    mix_names = ("w_in", "pool_w")
    mix_flight = _reduce_begin("mix", mix_names, [d_w_in, d_pool_w], pos)
    grad_x, d_mix_g = _matmul_nt_norm_bwd("mix_dh_norm_bwd", dproj, wg_in, xs, g_mix, dx1, None, tm=512, after=mix_flight[-1])

    small_parts = {
        "norm_mix_g": d_mix_g, "pool_scale": d_ps, "sgu_norm_g": d_gs, "w_spatial": d_ws, "b_spatial": d_b,
        "norm_xattn_g": d_xattn_g, "norm_mem_g": d_mem_g, "norm_ffn_g": d_ffn_g, "final_norm_g": d_final_g,
    }
    row_counts = [small_parts[n].size // LANES for n in SMALL]
    packed = jnp.concatenate([_as_lanes(small_parts[n]) for n in SMALL], axis=0)
    device = _device_index(lax.axis_index("x"), lax.axis_index("y"), lax.axis_index("c"))
    slots = lax.dynamic_update_slice(jnp.zeros((N_DEV,) + packed.shape, jnp.float32), packed[None], (device, 0, 0))
    small_flight = _exchange_start("small_grads_start", slots)

    groups = (("ffn", ffn_names, ffn_flight), ("attn", attn_names, attn_flight), ("mix", mix_names, mix_flight))
    joins, after = {}, small_flight[-1]
    for tag, names, flight in groups:
        reduced = _reduce_end(tag, names, flight, after, pos)
        joins[tag] = _join_start("join_start_" + tag, [reduced[n] for n in names])
        after = joins[tag][-1]
    small_sum = _sum_slots("small_grads_sum", _exchange_wait("small_grads_wait", small_flight, after))

    out_grad, out_delta, out_m, out_v = {}, {}, {}, {}
    after = small_sum
    for tag, names, _ in groups:
        grads = [g.reshape(-1, g.shape[-1]) for g in _join_wait("join_wait_" + tag, joins[tag], after)]
        updates = _per_run(_adamw, names, [_as2d(weights[n]) for n in names], grads,
                           [_as2d(moments_m[n]) for n in names], [_as2d(moments_v[n]) for n in names])
        for n, update in zip(names, updates):
            shape = weights[n].shape
            out_grad[n], out_delta[n], out_m[n], out_v[n] = (a.reshape(shape) for a in update)
        after = updates[-1][1]
    small_out = _adamw_small(
        small_sum, [_as_lanes(weights[n]) for n in SMALL], [_as_lanes(moments_m[n]) for n in SMALL],
        [_as_lanes(moments_v[n]) for n in SMALL], row_counts)
    for t, n in enumerate(SMALL):
        shape = weights[n].shape
        out_grad[n], out_delta[n], out_m[n], out_v[n] = (a.reshape(shape) for a in small_out[4 * t:4 * t + 4])

    loss = lax.psum(loss_part[0, 0], ("x", "y", "c"))
    return (loss, grad_x[None], *[out_grad[n] for n in WEIGHTS], *[out_delta[n] for n in WEIGHTS],
            *[out_m[n] for n in WEIGHTS], *[out_v[n] for n in WEIGHTS])
```

```python
import math

import jax
import jax.numpy as jnp
from jax import lax
from jax.experimental import pallas as pl
from jax.experimental.pallas import tpu as pltpu

MXU_DTYPE = jnp.bfloat16
WIRE_DTYPE = jnp.bfloat16

EPS = 1e-6
CHUNK = 64
POOL_WINDOWS = (2, 4, 8, 16)
POOL_HALO = 16
N_XATTN_HEADS = 4
ADAM_LR = 0.001
ADAM_B1 = 0.9
ADAM_B2 = 0.999
ADAM_EPS = 1e-08
ADAM_WD = 0.01
ADAM_STEP = 10

N_CHIPS = 4
N_DEV = 8
LANES = 128
VMEM_LIMIT_BYTES = 56 * 2 ** 20
COPY_CHUNK_BYTES = 512 * 2 ** 10
COPY_ROW_ALIGN = 16

MESH = pl.DeviceIdType.MESH
ANY = pl.BlockSpec(memory_space=pl.ANY)
VMEM_WHOLE = pl.BlockSpec(memory_space=pltpu.VMEM)
HBM_SPEC = pl.BlockSpec(memory_space=pltpu.HBM)
SEM_SPEC = pl.BlockSpec(memory_space=pltpu.SEMAPHORE)
SIDE_EFFECT = pltpu.SideEffectType.DATAFLOW_SIDE_EFFECTING

NN = (((1,), (0,)), ((), ()))
NT = (((1,), (1,)), ((), ()))
TN = (((0,), (0,)), ((), ()))


def _tile(n, target, mult):
    best = None
    for t in range(mult, min(n, target) + 1, mult):
        if n % t == 0:
            best = t
    return n if best is None else best


def _params(n_grid_axes):
    return pltpu.CompilerParams(dimension_semantics=("arbitrary",) * n_grid_axes, vmem_limit_bytes=VMEM_LIMIT_BYTES)


def _rsqrt_mean_sq(x):
    return lax.rsqrt(jnp.mean(x * x, axis=-1, keepdims=True) + EPS)


def _sum_all(x):
    return jnp.sum(jnp.sum(x, axis=1, keepdims=True), axis=0, keepdims=True)


def _sigmoid(x):
    return 0.5 * jnp.tanh(0.5 * x) + 0.5


def _store_epilogue(accs, ex_refs, out_refs, i, j):
    for a, o in zip(accs, out_refs):
        o[...] = a.astype(o.dtype)


def _fused_matmul(name, grid, operands, pairs, acc_shapes, extras, outs, epilogue):
    nk = grid[2]
    n_op, n_ex, n_out, n_acc = len(operands), len(extras), len(outs), len(acc_shapes)
    in_place = nk > 1 and epilogue is _store_epilogue and all(o[1] == jnp.float32 for o in outs)

    def body(*refs):
        op_refs = refs[:n_op]
        ex_refs = refs[n_op:n_op + n_ex]
        out_refs = refs[n_op + n_ex:n_op + n_ex + n_out]
        acc_refs = out_refs if in_place else refs[n_op + n_ex + n_out:]
        i, j, k = pl.program_id(0), pl.program_id(1), pl.program_id(2)
        loaded = {}

        def operand(n):
            if n not in loaded:
                loaded[n] = op_refs[n][...].astype(MXU_DTYPE)
            return loaded[n]

        def product(ia, ib, dims):
            return lax.dot_general(operand(ia), operand(ib), dims, preferred_element_type=jnp.float32)

        def products():
            parts = [None] * n_acc
            for ia, ib, iacc, dims in pairs:
                d = product(ia, ib, dims)
                parts[iacc] = d if parts[iacc] is None else parts[iacc] + d
            return parts

        if nk == 1:
            epilogue(products(), ex_refs, out_refs, i, j)
            return

        @pl.when(k == 0)
        def _():
            for a in range(n_acc):
                acc_refs[a][...] = jnp.zeros_like(acc_refs[a])

        for a, part in enumerate(products()):
            acc_refs[a][...] += part

        if not in_place:
            @pl.when(k == nk - 1)
            def _():
                epilogue([acc_refs[a][...] for a in range(n_acc)], ex_refs, out_refs, i, j)

    arrays = [o[0] for o in operands] + [e[0] for e in extras]
    in_specs = [pl.BlockSpec(o[1], o[2]) for o in operands]
    in_specs += [ANY if e[1] is None else pl.BlockSpec(e[1], e[2]) for e in extras]
    return pl.pallas_call(
        body,
        name=name,
        grid=grid,
        in_specs=in_specs,
        out_specs=[pl.BlockSpec(o[2], o[3]) for o in outs],
        out_shape=[jax.ShapeDtypeStruct(o[0], o[1]) for o in outs],
        scratch_shapes=[pltpu.VMEM(s, jnp.float32) for s in acc_shapes] if nk > 1 and not in_place else [],
        compiler_params=_params(3),
    )(*arrays)


def _matmul_nn(name, a, b, out_dtype, tm=2048, tn=1024):
    M, K = a.shape
    N = b.shape[1]
    tm, tn = _tile(M, tm, 16), _tile(N, tn, LANES)
    return _fused_matmul(
        name, (M // tm, N // tn, 1),
        [(a, (tm, K), lambda i, j, k: (i, 0)), (b, (K, tn), lambda i, j, k: (0, j))],
        [(0, 1, 0, NN)], [(tm, tn)], [],
        [((M, N), out_dtype, (tm, tn), lambda i, j, k: (i, j))], _store_epilogue)[0]


def _matmul_nn_cols(name, a, b3, out_dtype, tm=2048):
    M, K = a.shape
    J, _, Nc = b3.shape
    tm = _tile(M, tm, 16)
    return _fused_matmul(
        name, (M // tm, J, 1),
        [(a, (tm, K), lambda i, j, k: (i, 0)), (b3, (None, K, Nc), lambda i, j, k: (j, 0, 0))],
        [(0, 1, 0, NN)], [(tm, Nc)], [],
        [((M, J * Nc), out_dtype, (tm, Nc), lambda i, j, k: (i, j))], _store_epilogue)[0]


def _order_after(after):
    return [] if after is None else [(after, None, None)]


def _matmul_nt(name, a_list, b_list, out_dtype, tm=2048, tn=1024, after=None):
    M, K = a_list[0].shape
    N = b_list[0].shape[0]
    tm, tn = _tile(M, tm, 16), _tile(N, tn, LANES)
    n = len(a_list)
    operands = [(a, (tm, K), lambda i, j, k: (i, 0)) for a in a_list]
    operands += [(b, (tn, K), lambda i, j, k: (j, 0)) for b in b_list]
    return _fused_matmul(
        name, (M // tm, N // tn, 1), operands,
        [(p, n + p, 0, NT) for p in range(n)], [(tm, tn)], _order_after(after),
        [((M, N), out_dtype, (tm, tn), lambda i, j, k: (i, j))], _store_epilogue)[0]


def _matmul_nt_cols(name, a_list, b3_list, out_dtype, tm=1024, tn=512, after=None):
    M = a_list[0].shape[0]
    J, N, Kc = b3_list[0].shape
    tm, tn = _tile(M, tm, 16), _tile(N, tn, LANES)
    n = len(a_list)
    operands = [(a, (tm, Kc), lambda i, j, k: (i, k)) for a in a_list]
    operands += [(b, (None, tn, Kc), lambda i, j, k: (k, j, 0)) for b in b3_list]
    return _fused_matmul(
        name, (M // tm, N // tn, J), operands,
        [(p, n + p, 0, NT) for p in range(n)], [(tm, tn)], _order_after(after),
        [((M, N), out_dtype, (tm, tn), lambda i, j, k: (i, j))], _store_epilogue)[0]


def _matmul_tn(name, a, b_list, out_dtype, tka=1024, tn=1024, tkm=4096):
    M, Ka = a.shape
    N = b_list[0].shape[1]
    tka, tn, tkm = _tile(Ka, tka, LANES), _tile(N, tn, LANES), _tile(M, tkm, 16)
    n = len(b_list)
    operands = [(a, (tkm, tka), lambda i, j, k: (k, i))]
    operands += [(b, (tkm, tn), lambda i, j, k: (k, j)) for b in b_list]
    return _fused_matmul(
        name, (Ka // tka, N // tn, M // tkm), operands,
        [(0, 1 + p, p, TN) for p in range(n)], [(tka, tn)] * n, [],
        [((Ka, N), out_dtype, (tka, tn), lambda i, j, k: (i, j))] * n, _store_epilogue)


def _matmul_tn_cols(name, a, b_list, n_shards, out_dtype, tka=512, tkm=4096, after=None):
    M, Ka = a.shape
    Nc = b_list[0].shape[1] // n_shards
    tka, tkm = _tile(Ka, tka, LANES), _tile(M, tkm, 16)
    n = len(b_list)
    operands = [(a, (tkm, tka), lambda i, j, k: (k, i))]
    operands += [(b, (tkm, Nc), lambda i, j, k: (k, j)) for b in b_list]
    return _fused_matmul(
        name, (Ka // tka, n_shards, M // tkm), operands,
        [(0, 1 + p, p, TN) for p in range(n)], [(tka, Nc)] * n, _order_after(after),
        [((n_shards, Ka, Nc), out_dtype, (None, tka, Nc), lambda i, j, k: (j, i, 0))] * n, _store_epilogue)


def _rmsnorm(name, x, g_row, out_dtype, tm=512, after=None):
    S, D = x.shape
    tm = _tile(S, tm, 16)

    def body(x_ref, g_ref, *rest):
        o_ref = rest[-1]
        xv = x_ref[...]
        o_ref[...] = (xv * _rsqrt_mean_sq(xv) * g_ref[...]).astype(o_ref.dtype)

    ordered = [] if after is None else [after]
    return pl.pallas_call(
        body, name=name, grid=(S // tm,),
        in_specs=[pl.BlockSpec((tm, D), lambda i: (i, 0)), pl.BlockSpec((1, D), lambda i: (0, 0))] + [ANY] * len(ordered),
        out_specs=pl.BlockSpec((tm, D), lambda i: (i, 0)),
        out_shape=jax.ShapeDtypeStruct((S, D), out_dtype),
        compiler_params=_params(1),
    )(x, g_row, *ordered)


def _rmsnorm_bwd_values(dh, xin, g):
    r = _rsqrt_mean_sq(xin)
    xhat = xin * r
    dhg = dh * g
    dx = r * (dhg - xhat * jnp.mean(dhg * xhat, axis=-1, keepdims=True))
    return dx, jnp.sum(dh * xhat, axis=0, keepdims=True)


def _rmsnorm_bwd(name, dh, xin, g_row, dres, low_dtype, tm=256):
    S, D = xin.shape
    tm = _tile(S, tm, 16)
    has_dx = dres is not None
    has_low = has_dx and low_dtype is not None

    def body(*refs):
        dh_ref, x_ref, g_ref = refs[:3]
        rest = refs[3:]
        dx, dg = _rmsnorm_bwd_values(dh_ref[...], x_ref[...], g_ref[...])
        if has_dx:
            dres_ref, rest = rest[0], rest[1:]
            dx = dx + dres_ref[...]
            rest[0][...] = dx
            if has_low:
                rest[1][...] = dx.astype(low_dtype)
        dg_ref = rest[-1]

        @pl.when(pl.program_id(0) == 0)
        def _():
            dg_ref[...] = jnp.zeros_like(dg_ref)

        dg_ref[...] += dg

    row = pl.BlockSpec((tm, D), lambda i: (i, 0))
    vec = pl.BlockSpec((1, D), lambda i: (0, 0))
    ins, in_specs = [dh, xin, g_row], [row, row, vec]
    out_shape, out_specs = [], []
    if has_dx:
        ins.append(dres)
        in_specs.append(row)
        out_shape.append(jax.ShapeDtypeStruct((S, D), jnp.float32))
        out_specs.append(row)
        if has_low:
            out_shape.append(jax.ShapeDtypeStruct((S, D), low_dtype))
            out_specs.append(row)
    out_shape.append(jax.ShapeDtypeStruct((1, D), jnp.float32))
    out_specs.append(vec)
    return pl.pallas_call(
        body, name=name, grid=(S // tm,), in_specs=in_specs, out_specs=out_specs, out_shape=out_shape,
        compiler_params=_params(1),
    )(*ins)


def _chunk_mask(blk):
    t = lax.broadcasted_iota(jnp.int32, (blk, blk), 0)
    s = lax.broadcasted_iota(jnp.int32, (blk, blk), 1)
    return (s // CHUNK) <= (t // CHUNK)


def _pool_inputs(a, halo, row0, tm, g, pg):
    cols = slice(g * pg, (g + 1) * pg)
    w = POOL_WINDOWS[g]
    s = jnp.concatenate([halo[:, cols], a[:, cols]], axis=0)
    span = 1
    while span < w:
        s = s + pltpu.roll(s, span, axis=0)
        span *= 2
    t = lax.broadcasted_iota(jnp.int32, (tm, 1), 0) + row0
    cnt = jnp.minimum(t + 1, w).astype(jnp.float32)
    return s[POOL_HALO:] / cnt - a[:, cols]


def _mixer_specs(S, tm, DP, DS):
    per_halo = tm // POOL_HALO
    n_halo = S // POOL_HALO
    a_spec = pl.BlockSpec((tm, DP), lambda i: (i, 0))
    prev_spec = pl.BlockSpec((POOL_HALO, DP), lambda i: (jnp.maximum(i * per_halo - 1, 0), 0))
    next_spec = pl.BlockSpec((POOL_HALO, DP), lambda i: (jnp.minimum((i + 1) * per_halo, n_halo - 1), 0))
    u_spec = pl.BlockSpec((tm, DS), lambda i: (i, 1))
    v_spec = pl.BlockSpec((tm, DS), lambda i: (i, 2))
    return a_spec, prev_spec, next_spec, u_spec, v_spec


def _whole(arr):
    nd = arr.ndim
    return pl.BlockSpec(arr.shape, lambda i: (0,) * nd)


def _mixer_fwd(proj, pw, ps_row, gs_row, ws, b_full, tm=512):
    S = proj.shape[0]
    G, PG, _ = pw.shape
    DP = G * PG
    H, BLK, _ = ws.shape
    DS = gs_row.shape[1]
    HD = DS // H
    assert DP == DS and proj.shape[1] == DP + 2 * DS
    tm = _tile(S, tm, BLK)

    def body(a_ref, prev_ref, u_ref, v_ref, pw_ref, ps_ref, gs_ref, ws_ref, b_ref, z_ref):
        i = pl.program_id(0)
        a = a_ref[...]
        halo = jnp.where(i > 0, prev_ref[...], 0.0)
        for g in range(G):
            cols = slice(g * PG, (g + 1) * PG)
            p = _pool_inputs(a, halo, i * tm, tm, g, PG)
            y = jnp.dot(p.astype(MXU_DTYPE), pw_ref[g], preferred_element_type=jnp.float32) * ps_ref[:, cols]
            z_ref[:, cols] = y.astype(z_ref.dtype)
        v = v_ref[...]
        vn = (v * _rsqrt_mean_sq(v) * gs_ref[...]).astype(MXU_DTYPE)
        u = u_ref[...]
        mask = _chunk_mask(BLK)
        for h in range(H):
            wm = jnp.where(mask, ws_ref[h], 0.0).astype(MXU_DTYPE)
            bias = b_ref[h]
            cols = slice(h * HD, (h + 1) * HD)
            for n in range(tm // BLK):
                rows = slice(n * BLK, (n + 1) * BLK)
                mixed = jnp.dot(wm, vn[rows, cols], preferred_element_type=jnp.float32) + bias
                z_ref[rows, DP + h * HD:DP + (h + 1) * HD] = (u[rows, cols] * mixed).astype(z_ref.dtype)

    a_spec, prev_spec, _, u_spec, v_spec = _mixer_specs(S, tm, DP, DS)
    return pl.pallas_call(
        body, name="mixer_fwd", grid=(S // tm,),
        in_specs=[a_spec, prev_spec, u_spec, v_spec, _whole(pw), _whole(ps_row), _whole(gs_row), _whole(ws), _whole(b_full)],
        out_specs=pl.BlockSpec((tm, DP + DS), lambda i: (i, 0)),
        out_shape=jax.ShapeDtypeStruct((S, DP + DS), MXU_DTYPE),
        compiler_params=_params(1),
    )(proj, proj, proj, proj, pw, ps_row, gs_row, ws, b_full)


def _mixer_bwd(proj, dz, pw, ps_row, gs_row, ws, b_full, tm=256):
    S = proj.shape[0]
    G, PG, _ = pw.shape
    DP = G * PG
    H, BLK, _ = ws.shape
    DS = gs_row.shape[1]
    HD = DS // H
    tm = _tile(S, tm, BLK)
    nb = S // tm
    ext = tm + POOL_HALO

    def body(a_ref, prev_ref, u_ref, v_ref, dzp_ref, dzp_next_ref, dzs_ref, pw_ref, ps_ref, gs_ref, ws_ref, b_ref,
             dproj_ref, dpw_ref, dps_ref, dgs_ref, dws_ref, db_ref, dvn_ref):
        i = pl.program_id(0)

        @pl.when(i == 0)
        def _():
            for r in (dpw_ref, dps_ref, dgs_ref, dws_ref, db_ref):
                r[...] = jnp.zeros_like(r)

        a = a_ref[...]
        halo = jnp.where(i > 0, prev_ref[...], 0.0)
        dyp = dzp_ref[...]
        dyp_next = jnp.where(i < nb - 1, dzp_next_ref[...], 0.0)
        dq_ext = jnp.concatenate([dyp, dyp_next], axis=0) * ps_ref[...]
        t_ext = lax.broadcasted_iota(jnp.int32, (ext, 1), 0) + i * tm
        for g in range(G):
            cols = slice(g * PG, (g + 1) * PG)
            w = POOL_WINDOWS[g]
            pb = _pool_inputs(a, halo, i * tm, tm, g, PG).astype(MXU_DTYPE)
            y_pre = jnp.dot(pb, pw_ref[g], preferred_element_type=jnp.float32)
            dps_ref[:, cols] += jnp.sum(dyp[:, cols] * y_pre, axis=0, keepdims=True)
            dqb = dq_ext[:, cols].astype(MXU_DTYPE)
            dpw_ref[g] += lax.dot_general(pb, dqb[:tm], TN, preferred_element_type=jnp.float32)
            dp = lax.dot_general(dqb, pw_ref[g], NT, preferred_element_type=jnp.float32)
            f = dp / jnp.minimum(t_ext + 1, w).astype(jnp.float32)
            span = 1
            while span < w:
                f = f + pltpu.roll(f, ext - span, axis=0)
                span *= 2
            dproj_ref[:, cols] = (f[:tm] - dp[:tm]).astype(dproj_ref.dtype)

        v = v_ref[...]
        rv = _rsqrt_mean_sq(v)
        vhat = v * rv
        gs = gs_ref[...]
        vnb = (vhat * gs).astype(MXU_DTYPE)
        u = u_ref[...]
        dys = dzs_ref[...]
        mask = _chunk_mask(BLK)
        for h in range(H):
            wm = jnp.where(mask, ws_ref[h], 0.0).astype(MXU_DTYPE)
            bias = b_ref[h]
            cols = slice(h * HD, (h + 1) * HD)
            dm_sum = jnp.zeros((BLK, HD), jnp.float32)
            dws_sum = jnp.zeros((BLK, BLK), jnp.float32)
            for n in range(tm // BLK):
                rows = slice(n * BLK, (n + 1) * BLK)
                vblk = vnb[rows, cols]
                mixed = jnp.dot(wm, vblk, preferred_element_type=jnp.float32) + bias
                dy = dys[rows, cols]
                dproj_ref[rows, DP + h * HD:DP + (h + 1) * HD] = (dy * mixed).astype(dproj_ref.dtype)
                dm = dy * u[rows, cols]
                dm_sum = dm_sum + dm
                dmb = dm.astype(MXU_DTYPE)
                dws_sum = dws_sum + lax.dot_general(dmb, vblk, NT, preferred_element_type=jnp.float32)
                dvn_ref[rows, cols] = lax.dot_general(wm, dmb, TN, preferred_element_type=jnp.float32)
            dws_ref[h] += jnp.where(mask, dws_sum, 0.0)
            db_ref[h:h + 1, :] += jnp.sum(dm_sum.T, axis=0, keepdims=True)
        dvn = dvn_ref[...]
        dvg = dvn * gs
        dv = rv * (dvg - vhat * jnp.mean(dvg * vhat, axis=-1, keepdims=True))
        dgs_ref[...] += jnp.sum(dvn * vhat, axis=0, keepdims=True)
        dproj_ref[:, DP + DS:] = dv.astype(dproj_ref.dtype)

    a_spec, prev_spec, next_spec, u_spec, v_spec = _mixer_specs(S, tm, DP, DS)
    dzp_spec = pl.BlockSpec((tm, DP), lambda i: (i, 0))
    dzs_spec = pl.BlockSpec((tm, DS), lambda i: (i, 1))
    small = [pw, ps_row, gs_row, ws, b_full]
    out_shape = [
        jax.ShapeDtypeStruct((S, DP + 2 * DS), MXU_DTYPE),
        jax.ShapeDtypeStruct(pw.shape, jnp.float32),
        jax.ShapeDtypeStruct(ps_row.shape, jnp.float32),
        jax.ShapeDtypeStruct(gs_row.shape, jnp.float32),
        jax.ShapeDtypeStruct(ws.shape, jnp.float32),
        jax.ShapeDtypeStruct((H, BLK), jnp.float32),
    ]
    out_specs = [pl.BlockSpec((tm, DP + 2 * DS), lambda i: (i, 0))] + [_whole(s) for s in out_shape[1:]]
    return pl.pallas_call(
        body, name="mixer_bwd", grid=(nb,),
        in_specs=[a_spec, prev_spec, u_spec, v_spec, dzp_spec, next_spec, dzs_spec] + [_whole(s) for s in small],
        out_specs=out_specs, out_shape=out_shape,
        scratch_shapes=[pltpu.VMEM((tm, DS), jnp.float32)],
        compiler_params=_params(1),
    )(proj, proj, proj, proj, dz, dz, dz, *small)


def _softmax_rows(q, k, scale):
    s = lax.dot_general(q, k, NT, preferred_element_type=jnp.float32) * scale
    e = jnp.exp(s - jnp.max(s, axis=-1, keepdims=True))
    return e / jnp.sum(e, axis=-1, keepdims=True)


def _attn_fwd(q, k, v, tm=512):
    S, D = q.shape
    M = k.shape[0]
    HD = D // N_XATTN_HEADS
    scale = HD ** -0.5
    tm = _tile(S, tm, 16)

    def body(q_ref, k_ref, v_ref, o_ref):
        for h in range(N_XATTN_HEADS):
            cols = slice(h * HD, (h + 1) * HD)
            p = _softmax_rows(q_ref[:, cols], k_ref[:, cols], scale)
            o = jnp.dot(p.astype(MXU_DTYPE), v_ref[:, cols], preferred_element_type=jnp.float32)
            o_ref[:, cols] = o.astype(o_ref.dtype)

    row = pl.BlockSpec((tm, D), lambda i: (i, 0))
    mem = pl.BlockSpec((M, D), lambda i: (0, 0))
    return pl.pallas_call(
        body, name="attn_fwd", grid=(S // tm,), in_specs=[row, mem, mem], out_specs=row,
        out_shape=jax.ShapeDtypeStruct((S, D), MXU_DTYPE), compiler_params=_params(1),
    )(q, k, v)


def _attn_bwd(q, k, v, do, tm=512):
    S, D = q.shape
    M = k.shape[0]
    HD = D // N_XATTN_HEADS
    scale = HD ** -0.5
    tm = _tile(S, tm, 16)

    def body(q_ref, k_ref, v_ref, do_ref, dq_ref, dk_ref, dv_ref):
        @pl.when(pl.program_id(0) == 0)
        def _():
            dk_ref[...] = jnp.zeros_like(dk_ref)
            dv_ref[...] = jnp.zeros_like(dv_ref)

        for h in range(N_XATTN_HEADS):
            cols = slice(h * HD, (h + 1) * HD)
            qh, kh, vh, doh = q_ref[:, cols], k_ref[:, cols], v_ref[:, cols], do_ref[:, cols]
            p = _softmax_rows(qh, kh, scale)
            dp = lax.dot_general(doh, vh, NT, preferred_element_type=jnp.float32)
            dv_ref[:, cols] += lax.dot_general(p.astype(MXU_DTYPE), doh, TN, preferred_element_type=jnp.float32)
            ds = (p * (dp - jnp.sum(dp * p, axis=-1, keepdims=True)) * scale).astype(MXU_DTYPE)
            dq_ref[:, cols] = jnp.dot(ds, kh, preferred_element_type=jnp.float32).astype(dq_ref.dtype)
            dk_ref[:, cols] += lax.dot_general(ds, qh, TN, preferred_element_type=jnp.float32)

    row = pl.BlockSpec((tm, D), lambda i: (i, 0))
    mem = pl.BlockSpec((M, D), lambda i: (0, 0))
    return pl.pallas_call(
        body, name="attn_bwd", grid=(S // tm,), in_specs=[row, mem, mem, row], out_specs=[row, mem, mem],
        out_shape=[jax.ShapeDtypeStruct((S, D), MXU_DTYPE), jax.ShapeDtypeStruct((M, D), jnp.float32),
                   jax.ShapeDtypeStruct((M, D), jnp.float32)],
        compiler_params=_params(1),
    )(q, k, v, do)


def _proj_residual_norm(name, a, w2d, resid, g_row, tm=512, after=None):
    M, K = a.shape
    N = w2d.shape[1]
    tm = _tile(M, tm, 16)

    def epilogue(accs, ex_refs, out_refs, i, j):
        x_new = ex_refs[0][...] + accs[0]
        out_refs[0][...] = x_new
        out_refs[1][...] = (x_new * _rsqrt_mean_sq(x_new) * ex_refs[1][...]).astype(MXU_DTYPE)

    row = lambda i, j, k: (i, 0)
    return _fused_matmul(
        name, (M // tm, 1, 1),
        [(a, (tm, K), row), (w2d, (K, N), lambda i, j, k: (0, 0))],
        [(0, 1, 0, NN)], [(tm, N)],
        [(resid, (tm, N), row), (g_row, (1, N), lambda i, j, k: (0, 0))] + _order_after(after),
        [((M, N), jnp.float32, (tm, N), row), ((M, N), MXU_DTYPE, (tm, N), row)], epilogue)


def _ffn_gate_up(h, wg3, wu3, tm=512):
    M, K = h.shape
    J, _, Nc = wg3.shape
    tm = _tile(M, tm, 16)

    def epilogue(accs, ex_refs, out_refs, i, j):
        gate, up = accs
        sg = _sigmoid(gate)
        silu = gate * sg
        out_refs[0][...] = (silu * up).astype(MXU_DTYPE)
        out_refs[1][...] = (up * (sg * (1.0 + gate * (1.0 - sg)))).astype(MXU_DTYPE)
        out_refs[2][...] = silu.astype(MXU_DTYPE)

    col = lambda i, j, k: (i, j)
    wspec = lambda i, j, k: (j, 0, 0)
    return _fused_matmul(
        "ffn_gate_up", (M // tm, J, 1),
        [(h, (tm, K), lambda i, j, k: (i, 0)), (wg3, (None, K, Nc), wspec), (wu3, (None, K, Nc), wspec)],
        [(0, 1, 0, NN), (0, 2, 1, NN)], [(tm, Nc)] * 2, [],
        [((M, J * Nc), MXU_DTYPE, (tm, Nc), col)] * 3, epilogue)


def _ffn_down(act, wd2d, x2, tm=512, tn=1024):
    M, F = act.shape
    D = wd2d.shape[1]
    tm, tn = _tile(M, tm, 16), _tile(D, tn, LANES)

    def epilogue(accs, ex_refs, out_refs, i, j):
        out_refs[0][...] = ex_refs[0][...] + accs[0]

    tile = lambda i, j, k: (i, j)
    return _fused_matmul(
        "ffn_down", (M // tm, D // tn, 1),
        [(act, (tm, F), lambda i, j, k: (i, 0)), (wd2d, (F, tn), lambda i, j, k: (0, j))],
        [(0, 1, 0, NN)], [(tm, tn)], [(x2, (tm, tn), tile)],
        [((M, D), jnp.float32, (tm, tn), tile)], epilogue)[0]


def _final_norm_loss(x3, target, g_row, tm=256):
    S, D = x3.shape
    tm = _tile(S, tm, 16)

    def body(x_ref, t_ref, g_ref, dx_ref, dxl_ref, loss_ref, dg_ref):
        x3v = x_ref[...]
        g = g_ref[...]
        r = _rsqrt_mean_sq(x3v)
        xhat = x3v * r
        diff = xhat * g - t_ref[...]
        dy = diff / D
        dyg = dy * g
        dx = r * (dyg - xhat * jnp.mean(dyg * xhat, axis=-1, keepdims=True))
        dx_ref[...] = dx
        dxl_ref[...] = dx.astype(MXU_DTYPE)

        @pl.when(pl.program_id(0) == 0)
        def _():
            loss_ref[...] = jnp.zeros_like(loss_ref)
            dg_ref[...] = jnp.zeros_like(dg_ref)

        loss_ref[...] += 0.5 * _sum_all(jnp.mean(diff * diff, axis=-1, keepdims=True))
        dg_ref[...] += jnp.sum(dy * xhat, axis=0, keepdims=True)

    row = pl.BlockSpec((tm, D), lambda i: (i, 0))
    vec = pl.BlockSpec((1, D), lambda i: (0, 0))
    return pl.pallas_call(
        body, name="final_norm_loss", grid=(S // tm,), in_specs=[row, row, vec],
        out_specs=[row, row, pl.BlockSpec((1, 1), lambda i: (0, 0)), vec],
        out_shape=[jax.ShapeDtypeStruct((S, D), jnp.float32), jax.ShapeDtypeStruct((S, D), MXU_DTYPE),
                   jax.ShapeDtypeStruct((1, 1), jnp.float32), jax.ShapeDtypeStruct((1, D), jnp.float32)],
        compiler_params=_params(1),
    )(x3, target, g_row)


def _ffn_down_bwd(dx3_low, wd2d, act_by_gate, act_by_up, n_shards, tm=1024):
    M, D = dx3_low.shape
    F = wd2d.shape[0]
    tn = F // n_shards
    tm = _tile(M, tm, 16)

    def epilogue(accs, ex_refs, out_refs, i, j):
        dact = accs[0]
        out_refs[0][...] = (dact * ex_refs[0][...].astype(jnp.float32)).astype(MXU_DTYPE)
        out_refs[1][...] = (dact * ex_refs[1][...].astype(jnp.float32)).astype(MXU_DTYPE)

    col = lambda i, j, k: (i, j)
    return _fused_matmul(
        "ffn_down_bwd", (M // tm, n_shards, 1),
        [(dx3_low, (tm, D), lambda i, j, k: (i, 0)), (wd2d, (tn, D), lambda i, j, k: (j, 0))],
        [(0, 1, 0, NT)], [(tm, tn)],
        [(act_by_gate, (tm, tn), col), (act_by_up, (tm, tn), col)],
        [((M, F), MXU_DTYPE, (tm, tn), col)] * 2, epilogue)


def _matmul_nt_norm_bwd(name, a, b, xin, g_row, dres, low_dtype, tm=256, after=None):
    M = a.shape[0]
    sharded = b.ndim == 3
    N, Kc = b.shape[-2:]
    J = b.shape[0] if sharded else 1
    tm = _tile(M, tm, 16)
    has_low = low_dtype is not None

    def epilogue(accs, ex_refs, out_refs, i, j):
        dx, dg = _rmsnorm_bwd_values(accs[0], ex_refs[0][...], ex_refs[1][...])
        dx = dx + ex_refs[2][...]
        out_refs[0][...] = dx
        if has_low:
            out_refs[1][...] = dx.astype(low_dtype)
        dg_ref = out_refs[-1]

        @pl.when(i == 0)
        def _():
            dg_ref[...] = jnp.zeros_like(dg_ref)

        dg_ref[...] += dg

    row = lambda i, j, k: (i, 0)
    const = lambda i, j, k: (0, 0)
    b_operand = (b, (None, N, Kc), lambda i, j, k: (k, 0, 0)) if sharded else (b, (N, Kc), const)
    outs = [((M, N), jnp.float32, (tm, N), row)]
    if has_low:
        outs.append(((M, N), low_dtype, (tm, N), row))
    outs.append(((1, N), jnp.float32, (1, N), const))
    return _fused_matmul(
        name, (M // tm, 1, J), [(a, (tm, Kc), lambda i, j, k: (i, k)), b_operand],
        [(0, 1, 0, NT)], [(tm, N)],
        [(xin, (tm, N), row), (g_row, (1, N), const), (dres, (tm, N), row)] + _order_after(after),
        outs, epilogue)


def _row_tile(R, C, target_bytes=2 ** 21):
    return _tile(R, max(16, target_bytes // (4 * C)), 16)


def _prefetch_call(body, name, grid, pos, in_specs, out_specs, out_shape, ins, after=None):
    ordered = [] if after is None else [after]
    grid_spec = pltpu.PrefetchScalarGridSpec(
        num_scalar_prefetch=1, grid=grid, in_specs=list(in_specs) + [ANY] * len(ordered), out_specs=out_specs)
    return pl.pallas_call(body, name=name, grid_spec=grid_spec, out_shape=out_shape,
                          compiler_params=_params(len(grid)))(pos, *ins, *ordered)


def _runs(arrays):
    runs = []
    for i, a in enumerate(arrays):
        if runs and arrays[runs[-1][-1]].shape == a.shape:
            runs[-1].append(i)
        else:
            runs.append([i])
    return runs


def _per_run(fn, names, *array_lists):
    out = [None] * len(names)
    for run in _runs(array_lists[0]):
        results = fn("_".join(names[i] for i in run), *[[arrays[i] for i in run] for arrays in array_lists])
        for i, r in zip(run, results):
            out[i] = r
    return out


def _cast_into_slabs(name, xs, dtype, pos, after=None):
    n = len(xs)
    R, C = xs[0].shape
    tr = _row_tile(R, C * n, 2 ** 22)

    def body(pos_ref, *refs):
        for t in range(n):
            refs[len(refs) - n + t][...] = refs[t][...].astype(dtype)

    return _prefetch_call(
        body, "cast_" + name, (R // tr,), pos, [pl.BlockSpec((tr, C), lambda i, pos: (i, 0))] * n,
        [pl.BlockSpec((None, tr, C), lambda i, pos: (pos[0], i, 0))] * n,
        [jax.ShapeDtypeStruct((N_CHIPS, R, C), dtype)] * n, xs, after)


def _chip_sum(name, partials, gots, dtype, pos):
    n = len(partials)
    J, R, C = partials[0].shape
    Rh = R // 2
    tr = _row_tile(Rh, C * n, 2 ** 22)

    def body(pos_ref, *refs):
        for t in range(n):
            refs[2 * n + t][...] = (refs[t][...].astype(jnp.float32) + refs[n + t][...].astype(jnp.float32)).astype(dtype)

    blk = pl.BlockSpec((None, tr, C), lambda j, i, pos: (j, i, 0))
    return _prefetch_call(
        body, "chip_sum_" + name, (J, Rh // tr), pos,
        [pl.BlockSpec((None, None, tr, C), lambda j, i, pos: (j, pos[1], i, 0))] * n + [blk] * n, [blk] * n,
        [jax.ShapeDtypeStruct((J, Rh, C), dtype)] * n, [p.reshape(J, 2, Rh, C) for p in partials] + list(gots))


def _owner_sum(name, chip_sums, others, pos):
    n = len(chip_sums)
    J, Rh, C = chip_sums[0].shape
    n_other = others[0].shape[0]
    tr = _row_tile(Rh, C * (n_other + 1) * n, 2 ** 22)

    def body(pos_ref, *refs):
        for t in range(n):
            acc = refs[t][...].astype(jnp.float32)
            for k in range(n_other):
                acc = acc + refs[n + t][k].astype(jnp.float32)
            refs[2 * n + t][...] = acc

    return _prefetch_call(
        body, "owner_sum_" + name, (Rh // tr,), pos,
        [pl.BlockSpec((None, tr, C), lambda i, pos: (pos[0], i, 0))] * n
        + [pl.BlockSpec((n_other, tr, C), lambda i, pos: (0, i, 0))] * n,
        [pl.BlockSpec((None, tr, C), lambda i, pos: (pos[1], i, 0))] * n,
        [jax.ShapeDtypeStruct((2, Rh, C), jnp.float32)] * n, list(chip_sums) + list(others))


def _adamw_values(w, g, m, v):
    m = ADAM_B1 * m + (1.0 - ADAM_B1) * g
    v = ADAM_B2 * v + (1.0 - ADAM_B2) * (g * g)
    m_hat = m / (1.0 - ADAM_B1 ** ADAM_STEP)
    v_hat = v / (1.0 - ADAM_B2 ** ADAM_STEP)
    delta = -ADAM_LR * (m_hat / (jnp.sqrt(v_hat) + ADAM_EPS) + ADAM_WD * w)
    return delta, m, v


def _adamw(name, ws, gs, ms, vs):
    n = len(ws)
    R, C = ws[0].shape
    tr = _row_tile(R, C * n, 2 ** 20)

    def body(*refs):
        w_refs, g_refs, m_refs, v_refs = (refs[q * n:(q + 1) * n] for q in range(4))
        outs = refs[4 * n:]
        for t in range(n):
            g = g_refs[t][...]
            outs[4 * t][...] = g
            outs[4 * t + 1][...], outs[4 * t + 2][...], outs[4 * t + 3][...] = _adamw_values(
                w_refs[t][...], g, m_refs[t][...], v_refs[t][...])

    blk = pl.BlockSpec((tr, C), lambda i: (i, 0))
    res = pl.pallas_call(
        body, name="adamw_" + name, grid=(R // tr,), in_specs=[blk] * (4 * n), out_specs=[blk] * (4 * n),
        out_shape=[jax.ShapeDtypeStruct((R, C), jnp.float32)] * (4 * n), compiler_params=_params(1),
    )(*ws, *gs, *ms, *vs)
    return [tuple(res[4 * t:4 * t + 4]) for t in range(n)]


def _adamw_small(g_packed, ws, ms, vs, row_counts):
    n = len(ws)

    def body(*refs):
        g_ref = refs[0]
        w_refs, m_refs, v_refs = refs[1:1 + n], refs[1 + n:1 + 2 * n], refs[1 + 2 * n:1 + 3 * n]
        outs = refs[1 + 3 * n:]
        off = 0
        for t in range(n):
            g = g_ref[off:off + row_counts[t], :]
            off += row_counts[t]
            delta, m_new, v_new = _adamw_values(w_refs[t][...], g, m_refs[t][...], v_refs[t][...])
            outs[4 * t][...] = g
            outs[4 * t + 1][...] = delta
            outs[4 * t + 2][...] = m_new
            outs[4 * t + 3][...] = v_new

    out_shape = []
    for t in range(n):
        out_shape += [jax.ShapeDtypeStruct((row_counts[t], LANES), jnp.float32)] * 4
    return pl.pallas_call(
        body, name="adamw_small", in_specs=[VMEM_WHOLE] * (1 + 3 * n), out_specs=[VMEM_WHOLE] * (4 * n),
        out_shape=out_shape, compiler_params=pltpu.CompilerParams(vmem_limit_bytes=VMEM_LIMIT_BYTES),
    )(g_packed, *ws, *ms, *vs)


def _position():
    x, y, c = lax.axis_index("x"), lax.axis_index("y"), lax.axis_index("c")
    other_chips = [(1 - x, y), (x, 1 - y), (1 - x, 1 - y)]
    return x, y, c, other_chips


def _half(rows, which):
    return pl.ds(which * (rows // 2), rows // 2)


def _chunk_index(shape, dtype):
    rows = shape[-2]
    total = math.prod(shape) * jnp.dtype(dtype).itemsize
    n = max(1, min(rows // COPY_ROW_ALIGN, total // COPY_CHUNK_BYTES))
    step = -(-rows // n)
    step = -(-step // COPY_ROW_ALIGN) * COPY_ROW_ALIGN
    lead = (slice(None),) * (len(shape) - 2)
    return [lead + (pl.ds(s, min(step, rows - s)),) for s in range(0, rows, step)]


def _start_in_chunks(make, src, dst, waited_here=True):
    for idx in _chunk_index(src.shape, src.dtype):
        make(src.at[idx], dst.at[idx]).start()
    return make(src, dst) if waited_here else None


def _remote_copier(send_sem, recv_sem, to):
    return lambda src, dst: pltpu.make_async_remote_copy(
        src_ref=src, dst_ref=dst, send_sem=send_sem, recv_sem=recv_sem, device_id=to, device_id_type=MESH)


def _slab_half(ref, chip, which):
    return ref.at[chip, _half(ref.shape[1], which)]


def _slab_quarter(ref, chip, which, q):
    rows = ref.shape[1] // 4
    return ref.at[chip, pl.ds(which * 2 * rows + q * rows, rows)]


def _gather_step(name, arrived=(), ring=(), direct=(), after=None):
    groups = [g[2] for g in arrived] + list(direct) + list(ring)
    per_tensor = [2] * len(arrived) + [3] * len(direct) + [2] * len(ring)
    flat = [a for g in groups for a in g]
    T, G, A = len(flat), len(groups), len(arrived)
    ordered = [] if after is None else [after]
    sems_in = [sem for g in arrived for sem in g[:2]]

    def body(*refs):
        ins = refs[:T]
        first_hop = refs[T:T + 2 * A]
        out_sems = refs[T + 2 * A + len(ordered):T + 2 * A + len(ordered) + 2 * G]
        token = refs[-1]
        x, y, c, other_chips = _position()
        mine = 2 * x + y
        x_nbr, y_nbr = (1 - x, y, c), (x, 1 - y, c)
        from_x, from_y = 2 * (1 - x) + y, 2 * x + (1 - y)
        t0 = 0
        for gi, group in enumerate(groups):
            send, recv = out_sems[2 * gi], out_sems[2 * gi + 1]
            for t in range(len(group)):
                ref = ins[t0 + t]
                own = _slab_half(ref, mine, c)
                if gi < A:
                    send1, recv1 = first_hop[2 * gi], first_hop[2 * gi + 1]
                    for k, (peer, chip) in enumerate(((x_nbr, from_x), (y_nbr, from_y))):
                        cp = _remote_copier(send1.at[2 * t + k], recv1.at[2 * t + k], peer)(own, _slab_half(ref, chip, c))
                        cp.wait_send()
                        cp.wait_recv()
                    for k, (peer, chip) in enumerate(((y_nbr, from_x), (x_nbr, from_y))):
                        piece = _slab_quarter(ref, chip, c, k)
                        _start_in_chunks(_remote_copier(send.at[2 * t + k], recv.at[2 * t + k], peer), piece, piece,
                                         waited_here=False)
                elif gi < A + len(direct):
                    for k, (cx, cy) in enumerate(other_chips):
                        _start_in_chunks(_remote_copier(send.at[3 * t + k], recv.at[3 * t + k], (cx, cy, c)), own, own,
                                         waited_here=False)
                else:
                    for k, peer in enumerate((x_nbr, y_nbr)):
                        _start_in_chunks(_remote_copier(send.at[2 * t + k], recv.at[2 * t + k], peer), own, own,
                                         waited_here=False)
            t0 += len(group)
        token[...] = jnp.zeros_like(token)

    sem_shapes = []
    for group, n in zip(groups, per_tensor):
        sem_shapes += [pltpu.SemaphoreType.DMA((n * len(group),))] * 2
    res = pl.pallas_call(
        body, name=name, in_specs=[HBM_SPEC] * T + [SEM_SPEC] * (2 * A) + [ANY] * len(ordered),
        out_specs=[SEM_SPEC] * (2 * G) + [HBM_SPEC] * T + [VMEM_WHOLE],
        out_shape=sem_shapes + [pltpu.HBM(a.shape, a.dtype) for a in flat] + [jax.ShapeDtypeStruct((8, LANES), jnp.float32)],
        input_output_aliases={t: 2 * G + t for t in range(T)},
        compiler_params=pltpu.CompilerParams(has_side_effects=SIDE_EFFECT),
    )(*[pltpu.with_memory_space_constraint(a, pltpu.HBM) for a in flat], *sems_in, *ordered)
    flights, t0 = [], 2 * G
    for gi, group in enumerate(groups):
        flights.append((res[2 * gi], res[2 * gi + 1], res[t0:t0 + len(group)]))
        t0 += len(group)
    return flights[:A], flights[A:A + len(direct)], flights[A + len(direct):], res[-1]


def _relayed_wait(name, flight, after):
    send_sems, recv_sems, slabs = flight
    T = len(slabs)

    def body(*refs):
        ins = refs[:T]
        send, recv = refs[T:T + 2]
        x, y, c, _ = _position()
        x_nbr, y_nbr = (1 - x, y, c), (x, 1 - y, c)
        from_x, from_y, diagonal = 2 * (1 - x) + y, 2 * x + (1 - y), 2 * (1 - x) + (1 - y)
        for t in range(T):
            for k, (peer, chip) in enumerate(((y_nbr, from_x), (x_nbr, from_y))):
                cp = _remote_copier(send.at[2 * t + k], recv.at[2 * t + k], peer)(
                    _slab_quarter(ins[t], chip, c, k), _slab_quarter(ins[t], diagonal, c, k))
                cp.wait_send()
                cp.wait_recv()

    return pl.pallas_call(
        body, name=name, in_specs=[HBM_SPEC] * T + [SEM_SPEC, SEM_SPEC, ANY], out_specs=[HBM_SPEC] * T,
        out_shape=[pltpu.HBM(a.shape, a.dtype) for a in slabs], input_output_aliases={t: t for t in range(T)},
        compiler_params=pltpu.CompilerParams(has_side_effects=SIDE_EFFECT),
    )(*slabs, send_sems, recv_sems, after)


def _gather_wait(name, flight, after):
    send_sems, recv_sems, slabs = flight
    T = len(slabs)

    def body(*refs):
        ins = refs[:T]
        send, recv = refs[T:T + 2]
        x, y, c, other_chips = _position()
        mine = 2 * x + y
        for t in range(T):
            own = _slab_half(ins[t], mine, c)
            for k, (cx, cy) in enumerate(other_chips):
                landed = _slab_half(ins[t], 2 * cx + cy, c)
                cp = _remote_copier(send.at[3 * t + k], recv.at[3 * t + k], (cx, cy, c))(own, landed)
                cp.wait_send()
                cp.wait_recv()

    return pl.pallas_call(
        body, name=name, in_specs=[HBM_SPEC] * T + [SEM_SPEC, SEM_SPEC, ANY], out_specs=[HBM_SPEC] * T,
        out_shape=[pltpu.HBM(s.shape, s.dtype) for s in slabs], input_output_aliases={t: t for t in range(T)},
        compiler_params=pltpu.CompilerParams(has_side_effects=SIDE_EFFECT),
    )(*slabs, send_sems, recv_sems, after)


def _share_with_sibling(name, slabs):
    T = len(slabs)

    def body(*refs):
        outs = refs[T:2 * T]
        send_sems, recv_sems = refs[2 * T:]
        x, y, c, other_chips = _position()
        sibling = (x, y, 1 - c)
        started = []
        for t in range(T):
            for k, (cx, cy) in enumerate(other_chips):
                landed = _slab_half(outs[t], 2 * cx + cy, c)
                started.append(_start_in_chunks(_remote_copier(send_sems.at[t, k], recv_sems.at[t, k], sibling), landed, landed))
        for t in range(T):
            for k, (cx, cy) in enumerate(other_chips):
                arriving = _slab_half(outs[t], 2 * cx + cy, 1 - c)
                _remote_copier(send_sems.at[t, k], recv_sems.at[t, k], sibling)(arriving, arriving).wait_recv()
        for cp in started:
            cp.wait_send()

    return pl.pallas_call(
        body, name=name, in_specs=[ANY] * T, out_specs=[ANY] * T,
        out_shape=[jax.ShapeDtypeStruct(s.shape, s.dtype) for s in slabs],
        input_output_aliases={t: t for t in range(T)},
        scratch_shapes=[pltpu.SemaphoreType.DMA((T, 3)), pltpu.SemaphoreType.DMA((T, 3))],
    )(*slabs)


def _swap_halves(name, grads):
    T = len(grads)

    def body(*refs):
        ins, got = refs[:T], refs[T:2 * T]
        send_sems, recv_sems = refs[2 * T:]
        x, y, c, _ = _position()
        remote = []
        for t in range(T):
            rows = ins[t].shape[1]
            remote.append(_start_in_chunks(
                _remote_copier(send_sems.at[t], recv_sems.at[t], (x, y, 1 - c)), ins[t].at[:, _half(rows, 1 - c)], got[t]))
        for cp in remote:
            cp.wait()

    return pl.pallas_call(
        body, name=name, in_specs=[ANY] * T, out_specs=[ANY] * T,
        out_shape=[jax.ShapeDtypeStruct((g.shape[0], g.shape[1] // 2, g.shape[2]), g.dtype) for g in grads],
        scratch_shapes=[pltpu.SemaphoreType.DMA((T,)), pltpu.SemaphoreType.DMA((T,))],
    )(*grads)


def _swap_start(name, grads):
    T = len(grads)
    lands = [lax.empty((g.shape[0], g.shape[1] // 2, g.shape[2]), g.dtype) for g in grads]

    def body(*refs):
        ins, got = refs[:T], refs[T:2 * T]
        send_sems, recv_sems = refs[2 * T:2 * T + 2]
        token = refs[-1]
        x, y, c, _ = _position()
        for t in range(T):
            _start_in_chunks(_remote_copier(send_sems.at[t], recv_sems.at[t], (x, y, 1 - c)),
                             ins[t].at[:, _half(ins[t].shape[1], 1 - c)], got[t], waited_here=False)
        token[...] = jnp.zeros_like(token)

    arrays = list(grads) + lands
    res = pl.pallas_call(
        body, name=name, in_specs=[HBM_SPEC] * (2 * T),
        out_specs=[SEM_SPEC, SEM_SPEC] + [HBM_SPEC] * (2 * T) + [VMEM_WHOLE],
        out_shape=[pltpu.SemaphoreType.DMA((T,)), pltpu.SemaphoreType.DMA((T,))]
        + [pltpu.HBM(a.shape, a.dtype) for a in arrays] + [jax.ShapeDtypeStruct((8, LANES), jnp.float32)],
        input_output_aliases={i: 2 + i for i in range(2 * T)},
        compiler_params=pltpu.CompilerParams(has_side_effects=SIDE_EFFECT),
    )(*[pltpu.with_memory_space_constraint(a, pltpu.HBM) for a in arrays])
    return res[0], res[1], res[2:2 + T], res[2 + T:2 + 2 * T], res[-1]


def _swap_wait(name, flight, after):
    send_sems, recv_sems, grads, lands, _ = flight
    T = len(grads)

    def body(*refs):
        ins, got = refs[:T], refs[T:2 * T]
        send, recv = refs[2 * T:2 * T + 2]
        x, y, c, _ = _position()
        for t in range(T):
            cp = _remote_copier(send.at[t], recv.at[t], (x, y, 1 - c))(ins[t].at[:, _half(ins[t].shape[1], 1 - c)], got[t])
            cp.wait_send()
            cp.wait_recv()

    arrays = list(grads) + list(lands)
    res = pl.pallas_call(
        body, name=name, in_specs=[HBM_SPEC] * (2 * T) + [SEM_SPEC, SEM_SPEC, ANY], out_specs=[HBM_SPEC] * (2 * T),
        out_shape=[pltpu.HBM(a.shape, a.dtype) for a in arrays], input_output_aliases={i: i for i in range(2 * T)},
        compiler_params=pltpu.CompilerParams(has_side_effects=SIDE_EFFECT),
    )(*arrays, send_sems, recv_sems, after)
    return res[:T], res[T:]


def _scatter_start(name, chip_sums):
    T = len(chip_sums)
    lands = [lax.empty((3,) + s.shape[1:], s.dtype) for s in chip_sums]

    def body(*refs):
        srcs, dsts = refs[:T], refs[T:2 * T]
        send_sems, recv_sems = refs[2 * T:2 * T + 2]
        token = refs[-1]
        x, y, c, other_chips = _position()
        for t in range(T):
            for k, (cx, cy) in enumerate(other_chips):
                _start_in_chunks(_remote_copier(send_sems.at[3 * t + k], recv_sems.at[3 * t + k], (cx, cy, c)),
                                 srcs[t].at[2 * cx + cy], dsts[t].at[k], waited_here=False)
        token[...] = jnp.zeros_like(token)

    hbm = [pltpu.HBM(a.shape, a.dtype) for a in list(chip_sums) + lands]
    res = pl.pallas_call(
        body, name=name, in_specs=[HBM_SPEC] * (2 * T),
        out_specs=[SEM_SPEC, SEM_SPEC] + [HBM_SPEC] * (2 * T) + [VMEM_WHOLE],
        out_shape=[pltpu.SemaphoreType.DMA((3 * T,)), pltpu.SemaphoreType.DMA((3 * T,))] + hbm
        + [jax.ShapeDtypeStruct((8, LANES), jnp.float32)],
        input_output_aliases={i: 2 + i for i in range(2 * T)},
        compiler_params=pltpu.CompilerParams(has_side_effects=SIDE_EFFECT),
    )(*[pltpu.with_memory_space_constraint(a, pltpu.HBM) for a in list(chip_sums) + lands])
    return res[0], res[1], res[2:2 + T], res[2 + T:2 + 2 * T], res[-1]


def _scatter_wait(name, send_sems, recv_sems, chip_sums, lands, after):
    T = len(chip_sums)

    def body(*refs):
        srcs, dsts = refs[:T], refs[T:2 * T]
        send, recv = refs[2 * T:2 * T + 2]
        x, y, c, other_chips = _position()
        for t in range(T):
            for k, (cx, cy) in enumerate(other_chips):
                cp = _remote_copier(send.at[3 * t + k], recv.at[3 * t + k], (cx, cy, c))(srcs[t].at[2 * cx + cy], dsts[t].at[k])
                cp.wait_send()
                cp.wait_recv()

    arrays = list(chip_sums) + list(lands)
    res = pl.pallas_call(
        body, name=name, in_specs=[HBM_SPEC] * (2 * T) + [SEM_SPEC, SEM_SPEC, ANY],
        out_specs=[HBM_SPEC] * (2 * T), out_shape=[pltpu.HBM(a.shape, a.dtype) for a in arrays],
        input_output_aliases={i: i for i in range(2 * T)},
        compiler_params=pltpu.CompilerParams(has_side_effects=SIDE_EFFECT),
    )(*arrays, send_sems, recv_sems, after)
    return res[:T], res[T:]


def _as_shards(partial):
    return [p.reshape((N_CHIPS, -1, p.shape[-1])) for p in partial]


def _reduce_begin(tag, names, partial, pos):
    parts = _as_shards(partial)
    received = _swap_halves("swap_grad_halves_" + tag, parts)
    return _reduce_scatter_start(tag, names, parts, received, pos)


def _reduce_scatter_start(tag, names, parts, received, pos):
    chip_sums = _per_run(lambda name, p, r: _chip_sum(name, p, r, WIRE_DTYPE, pos), names, list(parts), list(received))
    return _scatter_start("scatter_start_" + tag, chip_sums)


def _reduce_end(tag, names, flight, after, pos):
    send_sems, recv_sems, chip_sums, lands, _ = flight
    chip_sums, others = _scatter_wait("scatter_wait_" + tag, send_sems, recv_sems, chip_sums, lands, after)
    reduced = _per_run(lambda name, cs, ot: _owner_sum(name, cs, ot, pos), names, list(chip_sums), list(others))
    return dict(zip(names, reduced))


def _join_start(name, halves):
    T = len(halves)

    def body(*refs):
        ins = refs[:T]
        send_sems, recv_sems = refs[T:T + 2]
        token = refs[-1]
        x, y, c, _ = _position()
        for t in range(T):
            _start_in_chunks(_remote_copier(send_sems.at[t], recv_sems.at[t], (x, y, 1 - c)),
                             ins[t].at[c], ins[t].at[c], waited_here=False)
        token[...] = jnp.zeros_like(token)

    res = pl.pallas_call(
        body, name=name, in_specs=[HBM_SPEC] * T,
        out_specs=[SEM_SPEC, SEM_SPEC] + [HBM_SPEC] * T + [VMEM_WHOLE],
        out_shape=[pltpu.SemaphoreType.DMA((T,)), pltpu.SemaphoreType.DMA((T,))]
        + [pltpu.HBM(h.shape, h.dtype) for h in halves] + [jax.ShapeDtypeStruct((8, LANES), jnp.float32)],
        input_output_aliases={t: 2 + t for t in range(T)},
        compiler_params=pltpu.CompilerParams(has_side_effects=SIDE_EFFECT),
    )(*[pltpu.with_memory_space_constraint(h, pltpu.HBM) for h in halves])
    return res[0], res[1], res[2:2 + T], res[-1]


def _join_wait(name, flight, after):
    send_sems, recv_sems, halves, _ = flight
    T = len(halves)

    def body(*refs):
        ins = refs[:T]
        send, recv = refs[T:T + 2]
        x, y, c, _ = _position()
        for t in range(T):
            cp = _remote_copier(send.at[t], recv.at[t], (x, y, 1 - c))(ins[t].at[c], ins[t].at[1 - c])
            cp.wait_send()
            cp.wait_recv()

    return pl.pallas_call(
        body, name=name, in_specs=[HBM_SPEC] * T + [SEM_SPEC, SEM_SPEC, ANY], out_specs=[HBM_SPEC] * T,
        out_shape=[pltpu.HBM(h.shape, h.dtype) for h in halves], input_output_aliases={t: t for t in range(T)},
        compiler_params=pltpu.CompilerParams(has_side_effects=SIDE_EFFECT),
    )(*halves, send_sems, recv_sems, after)


def _peers():
    x, y, c, _ = _position()
    flips = [(fx, fy, fc) for fx in (0, 1) for fy in (0, 1) for fc in (0, 1)][1:]
    return [(jnp.bitwise_xor(x, fx), jnp.bitwise_xor(y, fy), jnp.bitwise_xor(c, fc)) for fx, fy, fc in flips]


def _device_index(px, py, pc):
    return 4 * px + 2 * py + pc


def _exchange_start(name, slots):
    def body(slots_ref, send_sems, recv_sems, thru, token):
        x, y, c, _ = _position()
        own = slots_ref.at[_device_index(x, y, c)]
        for k, peer in enumerate(_peers()):
            _remote_copier(send_sems.at[k], recv_sems.at[k], peer)(own, own).start()
        token[...] = jnp.zeros_like(token)

    res = pl.pallas_call(
        body, name=name, in_specs=[HBM_SPEC], out_specs=[SEM_SPEC, SEM_SPEC, HBM_SPEC, VMEM_WHOLE],
        out_shape=[pltpu.SemaphoreType.DMA((N_DEV - 1,)), pltpu.SemaphoreType.DMA((N_DEV - 1,)),
                   pltpu.HBM(slots.shape, slots.dtype), jax.ShapeDtypeStruct((8, LANES), jnp.float32)],
        input_output_aliases={0: 2}, compiler_params=pltpu.CompilerParams(has_side_effects=SIDE_EFFECT),
    )(pltpu.with_memory_space_constraint(slots, pltpu.HBM))
    return res[0], res[1], res[2], res[3]


def _exchange_wait(name, flight, after):
    send_sems, recv_sems, slots, _ = flight

    def body(slots_ref, send, recv, after_ref, out_ref):
        x, y, c, _ = _position()
        own = slots_ref.at[_device_index(x, y, c)]
        for k, peer in enumerate(_peers()):
            cp = _remote_copier(send.at[k], recv.at[k], peer)(own, slots_ref.at[_device_index(*peer)])
            cp.wait_send()
            cp.wait_recv()

    return pl.pallas_call(
        body, name=name, in_specs=[HBM_SPEC, SEM_SPEC, SEM_SPEC, ANY], out_specs=HBM_SPEC,
        out_shape=pltpu.HBM(slots.shape, slots.dtype), input_output_aliases={0: 0},
        compiler_params=pltpu.CompilerParams(has_side_effects=SIDE_EFFECT),
    )(slots, send_sems, recv_sems, after)


def _sum_slots(name, slots):
    n, rows, lanes = slots.shape
    tr = _row_tile(rows, lanes * n)

    def body(s_ref, o_ref):
        acc = s_ref[0]
        for d in range(1, n):
            acc = acc + s_ref[d]
        o_ref[...] = acc

    return pl.pallas_call(
        body, name=name, grid=(rows // tr,), in_specs=[pl.BlockSpec((n, tr, lanes), lambda i: (0, i, 0))],
        out_specs=pl.BlockSpec((tr, lanes), lambda i: (i, 0)), out_shape=jax.ShapeDtypeStruct((rows, lanes), jnp.float32),
        compiler_params=_params(1))(slots)


BIG = ("w_in", "pool_w", "w_out", "w_q", "w_k", "w_v", "w_o", "w_gate", "w_up", "w_down")
SMALL = ("norm_mix_g", "pool_scale", "sgu_norm_g", "w_spatial", "b_spatial", "norm_xattn_g", "norm_mem_g",
         "norm_ffn_g", "final_norm_g")
GATHER_ORDER = (("w_in", "pool_w"), ("w_out",), ("w_q", "w_k", "w_v", "w_o"), ("w_gate", "w_up"), ("w_down",))
WEIGHTS = ("norm_mix_g", "w_in", "pool_w", "pool_scale", "sgu_norm_g", "w_spatial", "b_spatial", "w_out",
           "norm_xattn_g", "norm_mem_g", "w_q", "w_k", "w_v", "w_o", "norm_ffn_g", "w_gate", "w_up", "w_down",
           "final_norm_g")


def _as2d(a):
    return a.reshape(-1, a.shape[-1])


def _as_lanes(a):
    return a.reshape(-1, LANES)


def kernel(x, mem, norm_mix_g, w_in, pool_w, pool_scale, sgu_norm_g, w_spatial, b_spatial, w_out, norm_xattn_g, norm_mem_g, w_q, w_k, w_v, w_o, norm_ffn_g, w_gate, w_up, w_down, final_norm_g, loss_target, m_norm_mix_g, m_w_in, m_pool_w, m_pool_scale, m_sgu_norm_g, m_w_spatial, m_b_spatial, m_w_out, m_norm_xattn_g, m_norm_mem_g, m_w_q, m_w_k, m_w_v, m_w_o, m_norm_ffn_g, m_w_gate, m_w_up, m_w_down, m_final_norm_g, v_norm_mix_g, v_w_in, v_pool_w, v_pool_scale, v_sgu_norm_g, v_w_spatial, v_b_spatial, v_w_out, v_norm_xattn_g, v_norm_mem_g, v_w_q, v_w_k, v_w_v, v_w_o, v_norm_ffn_g, v_w_gate, v_w_up, v_w_down, v_final_norm_g):
    args = dict(locals())
    weights = {n: args[n] for n in WEIGHTS}
    moments_m = {n: args["m_" + n] for n in WEIGHTS}
    moments_v = {n: args["v_" + n] for n in WEIGHTS}
    low = MXU_DTYPE

    xs = x[0]
    mems = mem[0]
    target = loss_target[0]
    D = xs.shape[1]
    G, PG = pool_w.shape[1], pool_w.shape[3]

    pos = jnp.stack([2 * lax.axis_index("x") + lax.axis_index("y"), lax.axis_index("c")]).astype(jnp.int32)
    def cast(names, after=None):
        slabs = _per_run(lambda name, xs: _cast_into_slabs(name, xs, WIRE_DTYPE, pos, after), names,
                         [_as2d(weights[n]) for n in names])
        return dict(zip(names, slabs))

    def land(gi, flight, after, relayed=True):
        wait = _relayed_wait if relayed else _gather_wait
        landed = wait("gather_wait_%d" % gi, flight, after)
        return dict(zip(GATHER_ORDER[gi], _share_with_sibling("gather_share_%d" % gi, landed)))

    def rows(a):
        return a.reshape(-1, a.shape[-1])

    g_mix, g_xattn, g_mem, g_ffn = norm_mix_g, norm_xattn_g, norm_mem_g, norm_ffn_g
    g_final = final_norm_g.reshape(1, D)
    ws = w_spatial[0]
    b_full = jnp.broadcast_to(b_spatial[0][:, :, None], ws.shape[:2] + (sgu_norm_g.shape[1] // ws.shape[0],))

    first = cast(GATHER_ORDER[0])
    _, (sent_0,), _, started = _gather_step("gather_step_0", direct=[[first[n] for n in GATHER_ORDER[0]]])
    rest = cast([n for group in GATHER_ORDER[1:3] for n in group], after=started)
    _, (sent_1,), (hop_2,), started = _gather_step(
        "gather_step_1", direct=[[rest[n] for n in GATHER_ORDER[1]]], ring=[[rest[n] for n in GATHER_ORDER[2]]])
    h1 = _rmsnorm("norm_mix", xs, g_mix, low, after=started)
    mn = _rmsnorm("norm_mem", mems, g_mem, low)
    rest.update(cast(GATHER_ORDER[3], after=h1))
    rest.update(cast(GATHER_ORDER[4], after=rest[GATHER_ORDER[3][-1]]))
    slabs = [[rest[n] for n in group] for group in GATHER_ORDER[1:]]
    got = land(0, sent_0, rest["w_down"], relayed=False)
    wg_in = got["w_in"]
    pw = got["pool_w"].reshape(N_CHIPS, G, PG // N_CHIPS, PG).transpose(1, 0, 2, 3).reshape(G, PG, PG)
    proj = _matmul_nn_cols("proj_in", h1, wg_in, jnp.float32)
    z = _mixer_fwd(proj, pw, pool_scale, sgu_norm_g, ws, b_full)
    full = {"w_out": rows(land(1, sent_1, z, relayed=False)["w_out"])}
    (relay_2,), _, (hop_3,), started = _gather_step("gather_step_2", arrived=[hop_2], ring=slabs[2:3], after=full["w_out"])
    x1, h2 = _proj_residual_norm("mix_out", z, full["w_out"], xs, g_xattn, after=started)
    full.update({n: rows(a) for n, a in land(2, relay_2, h2).items()})
    q = _matmul_nn("xattn_q", h2, full["w_q"], low)
    k = _matmul_nn("xattn_k", mn, full["w_k"], low)
    vv = _matmul_nn("xattn_v", mn, full["w_v"], low)
    o = _attn_fwd(q, k, vv)
    (relay_3,), (sent_4,), _, started = _gather_step("gather_step_3", arrived=[hop_3], direct=slabs[3:4], after=o)
    x2, h3 = _proj_residual_norm("xattn_out", o, full["w_o"], x1, g_ffn, after=started)
    got = land(3, relay_3, h3)
    wg_gate, wg_up = got["w_gate"], got["w_up"]
    act, act_by_gate, act_by_up = _ffn_gate_up(h3, wg_gate, wg_up)
    full["w_down"] = rows(land(4, sent_4, act, relayed=False)["w_down"])
    x3 = _ffn_down(act, full["w_down"], x2)
    dx3, dx3_low, loss_part, d_final_g = _final_norm_loss(x3, target, g_final)

    dgate, dup = _ffn_down_bwd(dx3_low, full["w_down"], act_by_gate, act_by_up, N_CHIPS)
    (d_w_down,) = _matmul_tn("grad_w_down", act, [dx3_low], WIRE_DTYPE)
    (d_w_gate,) = _matmul_tn_cols("grad_w_gate", h3, [dgate], N_CHIPS, WIRE_DTYPE, tka=1024)
    (d_w_up,) = _matmul_tn_cols("grad_w_up", h3, [dup], N_CHIPS, WIRE_DTYPE, tka=1024)
    ffn_names = ("w_gate", "w_up", "w_down")
    ffn_swap = _swap_start("swap_start_ffn", _as_shards([d_w_gate, d_w_up, d_w_down]))
    dh3 = _matmul_nt_cols("ffn_dh", [dgate, dup], [wg_gate, wg_up], jnp.float32, tn=1024, after=ffn_swap[-1])
    ffn_flight = _reduce_scatter_start("ffn", ffn_names, *_swap_wait("swap_wait_ffn", ffn_swap, dh3), pos)
    dx2, dx2_low, d_ffn_g = _rmsnorm_bwd("norm_ffn_bwd", dh3, x2, g_ffn, dx3, low)

    do = _matmul_nt("xattn_do", [dx2_low], [full["w_o"]], low, after=ffn_flight[-1])
    (d_w_o,) = _matmul_tn("grad_w_o", o, [dx2_low], WIRE_DTYPE)
    dq, dk, dvv = _attn_bwd(q, k, vv, do)
    (d_w_q,) = _matmul_tn("grad_w_q", h2, [dq], WIRE_DTYPE)
    dx1, dx1_low, d_xattn_g = _matmul_nt_norm_bwd("xattn_dh_norm_bwd", dq, full["w_q"], x1, g_xattn, dx2, low)
    d_w_k, d_w_v = _matmul_tn("grad_w_kv", mn, [dk, dvv], WIRE_DTYPE)
    dmn = _matmul_nt("xattn_dmem", [dk, dvv], [full["w_k"], full["w_v"]], jnp.float32)
    (d_mem_g,) = _rmsnorm_bwd("norm_mem_bwd", dmn, mems, g_mem, None, None)

    (d_w_out,) = _matmul_tn("grad_w_out", z, [dx1_low], WIRE_DTYPE)
    attn_names = ("w_q", "w_k", "w_v", "w_o", "w_out")
    attn_swap = _swap_start("swap_start_attn", _as_shards([d_w_q, d_w_k, d_w_v, d_w_o, d_w_out]))
    dz = _matmul_nt("mix_dz", [dx1_low], [full["w_out"]], jnp.float32, after=attn_swap[-1])
    attn_flight = _reduce_scatter_start("attn", attn_names, *_swap_wait("swap_wait_attn", attn_swap, dz), pos)
    dproj, d_pw, d_ps, d_gs, d_ws, d_b = _mixer_bwd(proj, dz, pw, pool_scale, sgu_norm_g, ws, b_full)
    (d_w_in,) = _matmul_tn_cols("grad_w_in", h1, [dproj], N_CHIPS, WIRE_DTYPE, tka=1024, after=attn_flight[-1])
    d_pool_w = d_pw.reshape(G, N_CHIPS, PG // N_CHIPS, PG).transpose(1, 0, 2, 3).reshape(N_CHIPS, PG, PG).astype(WIRE_DTYPE)
    mix_names = ("w_in", "pool_w")
    mix_flight = _reduce_begin("mix", mix_names, [d_w_in, d_pool_w], pos)
    grad_x, d_mix_g = _matmul_nt_norm_bwd("mix_dh_norm_bwd", dproj, wg_in, xs, g_mix, dx1, None, tm=512, after=mix_flight[-1])

    small_parts = {
        "norm_mix_g": d_mix_g, "pool_scale": d_ps, "sgu_norm_g": d_gs, "w_spatial": d_ws, "b_spatial": d_b,
        "norm_xattn_g": d_xattn_g, "norm_mem_g": d_mem_g, "norm_ffn_g": d_ffn_g, "final_norm_g": d_final_g,
    }
    row_counts = [small_parts[n].size // LANES for n in SMALL]
    packed = jnp.concatenate([_as_lanes(small_parts[n]) for n in SMALL], axis=0)
    device = _device_index(lax.axis_index("x"), lax.axis_index("y"), lax.axis_index("c"))
    slots = lax.dynamic_update_slice(jnp.zeros((N_DEV,) + packed.shape, jnp.float32), packed[None], (device, 0, 0))
    small_flight = _exchange_start("small_grads_start", slots)

    groups = (("ffn", ffn_names, ffn_flight), ("attn", attn_names, attn_flight), ("mix", mix_names, mix_flight))
    joins, after = {}, small_flight[-1]
    for tag, names, flight in groups:
        reduced = _reduce_end(tag, names, flight, after, pos)
        joins[tag] = _join_start("join_start_" + tag, [reduced[n] for n in names])
        after = joins[tag][-1]
    small_sum = _sum_slots("small_grads_sum", _exchange_wait("small_grads_wait", small_flight, after))

    out_grad, out_delta, out_m, out_v = {}, {}, {}, {}
    after = small_sum
    for tag, names, _ in groups:
        grads = [g.reshape(-1, g.shape[-1]) for g in _join_wait("join_wait_" + tag, joins[tag], after)]
        updates = _per_run(_adamw, names, [_as2d(weights[n]) for n in names], grads,
                           [_as2d(moments_m[n]) for n in names], [_as2d(moments_v[n]) for n in names])
        for n, update in zip(names, updates):
            shape = weights[n].shape
            out_grad[n], out_delta[n], out_m[n], out_v[n] = (a.reshape(shape) for a in update)
        after = updates[-1][1]
    small_out = _adamw_small(
        small_sum, [_as_lanes(weights[n]) for n in SMALL], [_as_lanes(moments_m[n]) for n in SMALL],
        [_as_lanes(moments_v[n]) for n in SMALL], row_counts)
    for t, n in enumerate(SMALL):
        shape = weights[n].shape
        out_grad[n], out_delta[n], out_m[n], out_v[n] = (a.reshape(shape) for a in small_out[4 * t:4 * t + 4])

    loss = lax.psum(loss_part[0, 0], ("x", "y", "c"))
    return (loss, grad_x[None], *[out_grad[n] for n in WEIGHTS], *[out_delta[n] for n in WEIGHTS],
            *[out_m[n] for n in WEIGHTS], *[out_v[n] for n in WEIGHTS])
```

```python
import math

import jax
import jax.numpy as jnp
from jax import lax
from jax.experimental import pallas as pl
from jax.experimental.pallas import tpu as pltpu

MXU_DTYPE = jnp.bfloat16
WIRE_DTYPE = jnp.bfloat16

EPS = 1e-6
CHUNK = 64
POOL_WINDOWS = (2, 4, 8, 16)
POOL_HALO = 16
N_XATTN_HEADS = 4
ADAM_LR = 0.001
ADAM_B1 = 0.9
ADAM_B2 = 0.999
ADAM_EPS = 1e-08
ADAM_WD = 0.01
ADAM_STEP = 10

N_CHIPS = 4
N_DEV = 8
LANES = 128
VMEM_LIMIT_BYTES = 56 * 2 ** 20
COPY_CHUNK_BYTES = 512 * 2 ** 10
COPY_ROW_ALIGN = 16

MESH = pl.DeviceIdType.MESH
ANY = pl.BlockSpec(memory_space=pl.ANY)
VMEM_WHOLE = pl.BlockSpec(memory_space=pltpu.VMEM)
HBM_SPEC = pl.BlockSpec(memory_space=pltpu.HBM)
SEM_SPEC = pl.BlockSpec(memory_space=pltpu.SEMAPHORE)
SIDE_EFFECT = pltpu.SideEffectType.DATAFLOW_SIDE_EFFECTING

NN = (((1,), (0,)), ((), ()))
NT = (((1,), (1,)), ((), ()))
TN = (((0,), (0,)), ((), ()))


def _tile(n, target, mult):
    best = None
    for t in range(mult, min(n, target) + 1, mult):
        if n % t == 0:
            best = t
    return n if best is None else best


def _params(n_grid_axes):
    return pltpu.CompilerParams(dimension_semantics=("arbitrary",) * n_grid_axes, vmem_limit_bytes=VMEM_LIMIT_BYTES)


def _rsqrt_mean_sq(x):
    return lax.rsqrt(jnp.mean(x * x, axis=-1, keepdims=True) + EPS)


def _sum_all(x):
    return jnp.sum(jnp.sum(x, axis=1, keepdims=True), axis=0, keepdims=True)


def _sigmoid(x):
    return 0.5 * jnp.tanh(0.5 * x) + 0.5


def _store_epilogue(accs, ex_refs, out_refs, i, j):
    for a, o in zip(accs, out_refs):
        o[...] = a.astype(o.dtype)


def _fused_matmul(name, grid, operands, pairs, acc_shapes, extras, outs, epilogue):
    nk = grid[2]
    n_op, n_ex, n_out, n_acc = len(operands), len(extras), len(outs), len(acc_shapes)
    in_place = nk > 1 and epilogue is _store_epilogue and all(o[1] == jnp.float32 for o in outs)

    def body(*refs):
        op_refs = refs[:n_op]
        ex_refs = refs[n_op:n_op + n_ex]
        out_refs = refs[n_op + n_ex:n_op + n_ex + n_out]
        acc_refs = out_refs if in_place else refs[n_op + n_ex + n_out:]
        i, j, k = pl.program_id(0), pl.program_id(1), pl.program_id(2)
        loaded = {}

        def operand(n):
            if n not in loaded:
                loaded[n] = op_refs[n][...].astype(MXU_DTYPE)
            return loaded[n]

        def product(ia, ib, dims):
            return lax.dot_general(operand(ia), operand(ib), dims, preferred_element_type=jnp.float32)

        def products():
            parts = [None] * n_acc
            for ia, ib, iacc, dims in pairs:
                d = product(ia, ib, dims)
                parts[iacc] = d if parts[iacc] is None else parts[iacc] + d
            return parts

        if nk == 1:
            epilogue(products(), ex_refs, out_refs, i, j)
            return

        @pl.when(k == 0)
        def _():
            for a in range(n_acc):
                acc_refs[a][...] = jnp.zeros_like(acc_refs[a])

        for a, part in enumerate(products()):
            acc_refs[a][...] += part

        if not in_place:
            @pl.when(k == nk - 1)
            def _():
                epilogue([acc_refs[a][...] for a in range(n_acc)], ex_refs, out_refs, i, j)

    arrays = [o[0] for o in operands] + [e[0] for e in extras]
    in_specs = [pl.BlockSpec(o[1], o[2]) for o in operands]
    in_specs += [ANY if e[1] is None else pl.BlockSpec(e[1], e[2]) for e in extras]
    return pl.pallas_call(
        body,
        name=name,
        grid=grid,
        in_specs=in_specs,
        out_specs=[pl.BlockSpec(o[2], o[3]) for o in outs],
        out_shape=[jax.ShapeDtypeStruct(o[0], o[1]) for o in outs],
        scratch_shapes=[pltpu.VMEM(s, jnp.float32) for s in acc_shapes] if nk > 1 and not in_place else [],
        compiler_params=_params(3),
    )(*arrays)


def _matmul_nn(name, a, b, out_dtype, tm=2048, tn=1024):
    M, K = a.shape
    N = b.shape[1]
    tm, tn = _tile(M, tm, 16), _tile(N, tn, LANES)
    return _fused_matmul(
        name, (M // tm, N // tn, 1),
        [(a, (tm, K), lambda i, j, k: (i, 0)), (b, (K, tn), lambda i, j, k: (0, j))],
        [(0, 1, 0, NN)], [(tm, tn)], [],
        [((M, N), out_dtype, (tm, tn), lambda i, j, k: (i, j))], _store_epilogue)[0]


def _matmul_nn_cols(name, a, b3, out_dtype, tm=2048):
    M, K = a.shape
    J, _, Nc = b3.shape
    tm = _tile(M, tm, 16)
    return _fused_matmul(
        name, (M // tm, J, 1),
        [(a, (tm, K), lambda i, j, k: (i, 0)), (b3, (None, K, Nc), lambda i, j, k: (j, 0, 0))],
        [(0, 1, 0, NN)], [(tm, Nc)], [],
        [((M, J * Nc), out_dtype, (tm, Nc), lambda i, j, k: (i, j))], _store_epilogue)[0]


def _order_after(after):
    return [] if after is None else [(after, None, None)]


def _matmul_nt(name, a_list, b_list, out_dtype, tm=2048, tn=1024, after=None):
    M, K = a_list[0].shape
    N = b_list[0].shape[0]
    tm, tn = _tile(M, tm, 16), _tile(N, tn, LANES)
    n = len(a_list)
    operands = [(a, (tm, K), lambda i, j, k: (i, 0)) for a in a_list]
    operands += [(b, (tn, K), lambda i, j, k: (j, 0)) for b in b_list]
    return _fused_matmul(
        name, (M // tm, N // tn, 1), operands,
        [(p, n + p, 0, NT) for p in range(n)], [(tm, tn)], _order_after(after),
        [((M, N), out_dtype, (tm, tn), lambda i, j, k: (i, j))], _store_epilogue)[0]


def _matmul_nt_cols(name, a_list, b3_list, out_dtype, tm=1024, tn=512, after=None):
    M = a_list[0].shape[0]
    J, N, Kc = b3_list[0].shape
    tm, tn = _tile(M, tm, 16), _tile(N, tn, LANES)
    n = len(a_list)
    operands = [(a, (tm, Kc), lambda i, j, k: (i, k)) for a in a_list]
    operands += [(b, (None, tn, Kc), lambda i, j, k: (k, j, 0)) for b in b3_list]
    return _fused_matmul(
        name, (M // tm, N // tn, J), operands,
        [(p, n + p, 0, NT) for p in range(n)], [(tm, tn)], _order_after(after),
        [((M, N), out_dtype, (tm, tn), lambda i, j, k: (i, j))], _store_epilogue)[0]


def _matmul_tn(name, a, b_list, out_dtype, tka=1024, tn=1024, tkm=4096):
    M, Ka = a.shape
    N = b_list[0].shape[1]
    tka, tn, tkm = _tile(Ka, tka, LANES), _tile(N, tn, LANES), _tile(M, tkm, 16)
    n = len(b_list)
    operands = [(a, (tkm, tka), lambda i, j, k: (k, i))]
    operands += [(b, (tkm, tn), lambda i, j, k: (k, j)) for b in b_list]
    return _fused_matmul(
        name, (Ka // tka, N // tn, M // tkm), operands,
        [(0, 1 + p, p, TN) for p in range(n)], [(tka, tn)] * n, [],
        [((Ka, N), out_dtype, (tka, tn), lambda i, j, k: (i, j))] * n, _store_epilogue)


def _matmul_tn_cols(name, a, b_list, n_shards, out_dtype, tka=512, tkm=4096, after=None):
    M, Ka = a.shape
    Nc = b_list[0].shape[1] // n_shards
    tka, tkm = _tile(Ka, tka, LANES), _tile(M, tkm, 16)
    n = len(b_list)
    operands = [(a, (tkm, tka), lambda i, j, k: (k, i))]
    operands += [(b, (tkm, Nc), lambda i, j, k: (k, j)) for b in b_list]
    return _fused_matmul(
        name, (Ka // tka, n_shards, M // tkm), operands,
        [(0, 1 + p, p, TN) for p in range(n)], [(tka, Nc)] * n, _order_after(after),
        [((n_shards, Ka, Nc), out_dtype, (None, tka, Nc), lambda i, j, k: (j, i, 0))] * n, _store_epilogue)


def _rmsnorm(name, x, g_row, out_dtype, tm=512, after=None):
    S, D = x.shape
    tm = _tile(S, tm, 16)

    def body(x_ref, g_ref, *rest):
        o_ref = rest[-1]
        xv = x_ref[...]
        o_ref[...] = (xv * _rsqrt_mean_sq(xv) * g_ref[...]).astype(o_ref.dtype)

    ordered = [] if after is None else [after]
    return pl.pallas_call(
        body, name=name, grid=(S // tm,),
        in_specs=[pl.BlockSpec((tm, D), lambda i: (i, 0)), pl.BlockSpec((1, D), lambda i: (0, 0))] + [ANY] * len(ordered),
        out_specs=pl.BlockSpec((tm, D), lambda i: (i, 0)),
        out_shape=jax.ShapeDtypeStruct((S, D), out_dtype),
        compiler_params=_params(1),
    )(x, g_row, *ordered)


def _rmsnorm_bwd_values(dh, xin, g):
    r = _rsqrt_mean_sq(xin)
    xhat = xin * r
    dhg = dh * g
    dx = r * (dhg - xhat * jnp.mean(dhg * xhat, axis=-1, keepdims=True))
    return dx, jnp.sum(dh * xhat, axis=0, keepdims=True)


def _rmsnorm_bwd(name, dh, xin, g_row, dres, low_dtype, tm=256):
    S, D = xin.shape
    tm = _tile(S, tm, 16)
    has_dx = dres is not None
    has_low = has_dx and low_dtype is not None

    def body(*refs):
        dh_ref, x_ref, g_ref = refs[:3]
        rest = refs[3:]
        dx, dg = _rmsnorm_bwd_values(dh_ref[...], x_ref[...], g_ref[...])
        if has_dx:
            dres_ref, rest = rest[0], rest[1:]
            dx = dx + dres_ref[...]
            rest[0][...] = dx
            if has_low:
                rest[1][...] = dx.astype(low_dtype)
        dg_ref = rest[-1]

        @pl.when(pl.program_id(0) == 0)
        def _():
            dg_ref[...] = jnp.zeros_like(dg_ref)

        dg_ref[...] += dg

    row = pl.BlockSpec((tm, D), lambda i: (i, 0))
    vec = pl.BlockSpec((1, D), lambda i: (0, 0))
    ins, in_specs = [dh, xin, g_row], [row, row, vec]
    out_shape, out_specs = [], []
    if has_dx:
        ins.append(dres)
        in_specs.append(row)
        out_shape.append(jax.ShapeDtypeStruct((S, D), jnp.float32))
        out_specs.append(row)
        if has_low:
            out_shape.append(jax.ShapeDtypeStruct((S, D), low_dtype))
            out_specs.append(row)
    out_shape.append(jax.ShapeDtypeStruct((1, D), jnp.float32))
    out_specs.append(vec)
    return pl.pallas_call(
        body, name=name, grid=(S // tm,), in_specs=in_specs, out_specs=out_specs, out_shape=out_shape,
        compiler_params=_params(1),
    )(*ins)


def _chunk_mask(blk):
    t = lax.broadcasted_iota(jnp.int32, (blk, blk), 0)
    s = lax.broadcasted_iota(jnp.int32, (blk, blk), 1)
    return (s // CHUNK) <= (t // CHUNK)


def _pool_inputs(a, halo, row0, tm, g, pg):
    cols = slice(g * pg, (g + 1) * pg)
    w = POOL_WINDOWS[g]
    s = jnp.concatenate([halo[:, cols], a[:, cols]], axis=0)
    span = 1
    while span < w:
        s = s + pltpu.roll(s, span, axis=0)
        span *= 2
    t = lax.broadcasted_iota(jnp.int32, (tm, 1), 0) + row0
    cnt = jnp.minimum(t + 1, w).astype(jnp.float32)
    return s[POOL_HALO:] / cnt - a[:, cols]


def _mixer_specs(S, tm, DP, DS):
    per_halo = tm // POOL_HALO
    n_halo = S // POOL_HALO
    a_spec = pl.BlockSpec((tm, DP), lambda i: (i, 0))
    prev_spec = pl.BlockSpec((POOL_HALO, DP), lambda i: (jnp.maximum(i * per_halo - 1, 0), 0))
    next_spec = pl.BlockSpec((POOL_HALO, DP), lambda i: (jnp.minimum((i + 1) * per_halo, n_halo - 1), 0))
    u_spec = pl.BlockSpec((tm, DS), lambda i: (i, 1))
    v_spec = pl.BlockSpec((tm, DS), lambda i: (i, 2))
    return a_spec, prev_spec, next_spec, u_spec, v_spec


def _whole(arr):
    nd = arr.ndim
    return pl.BlockSpec(arr.shape, lambda i: (0,) * nd)


def _mixer_fwd(proj, pw, ps_row, gs_row, ws, b_full, tm=512):
    S = proj.shape[0]
    G, PG, _ = pw.shape
    DP = G * PG
    H, BLK, _ = ws.shape
    DS = gs_row.shape[1]
    HD = DS // H
    assert DP == DS and proj.shape[1] == DP + 2 * DS
    tm = _tile(S, tm, BLK)

    def body(a_ref, prev_ref, u_ref, v_ref, pw_ref, ps_ref, gs_ref, ws_ref, b_ref, z_ref):
        i = pl.program_id(0)
        a = a_ref[...]
        halo = jnp.where(i > 0, prev_ref[...], 0.0)
        for g in range(G):
            cols = slice(g * PG, (g + 1) * PG)
            p = _pool_inputs(a, halo, i * tm, tm, g, PG)
            y = jnp.dot(p.astype(MXU_DTYPE), pw_ref[g], preferred_element_type=jnp.float32) * ps_ref[:, cols]
            z_ref[:, cols] = y.astype(z_ref.dtype)
        v = v_ref[...]
        vn = (v * _rsqrt_mean_sq(v) * gs_ref[...]).astype(MXU_DTYPE)
        u = u_ref[...]
        mask = _chunk_mask(BLK)
        for h in range(H):
            wm = jnp.where(mask, ws_ref[h], 0.0).astype(MXU_DTYPE)
            bias = b_ref[h]
            cols = slice(h * HD, (h + 1) * HD)
            for n in range(tm // BLK):
                rows = slice(n * BLK, (n + 1) * BLK)
                mixed = jnp.dot(wm, vn[rows, cols], preferred_element_type=jnp.float32) + bias
                z_ref[rows, DP + h * HD:DP + (h + 1) * HD] = (u[rows, cols] * mixed).astype(z_ref.dtype)

    a_spec, prev_spec, _, u_spec, v_spec = _mixer_specs(S, tm, DP, DS)
    return pl.pallas_call(
        body, name="mixer_fwd", grid=(S // tm,),
        in_specs=[a_spec, prev_spec, u_spec, v_spec, _whole(pw), _whole(ps_row), _whole(gs_row), _whole(ws), _whole(b_full)],
        out_specs=pl.BlockSpec((tm, DP + DS), lambda i: (i, 0)),
        out_shape=jax.ShapeDtypeStruct((S, DP + DS), MXU_DTYPE),
        compiler_params=_params(1),
    )(proj, proj, proj, proj, pw, ps_row, gs_row, ws, b_full)


def _mixer_bwd(proj, dz, pw, ps_row, gs_row, ws, b_full, tm=256):
    S = proj.shape[0]
    G, PG, _ = pw.shape
    DP = G * PG
    H, BLK, _ = ws.shape
    DS = gs_row.shape[1]
    HD = DS // H
    tm = _tile(S, tm, BLK)
    nb = S // tm
    ext = tm + POOL_HALO

    def body(a_ref, prev_ref, u_ref, v_ref, dzp_ref, dzp_next_ref, dzs_ref, pw_ref, ps_ref, gs_ref, ws_ref, b_ref,
             dproj_ref, dpw_ref, dps_ref, dgs_ref, dws_ref, db_ref, dvn_ref):
        i = pl.program_id(0)

        @pl.when(i == 0)
        def _():
            for r in (dpw_ref, dps_ref, dgs_ref, dws_ref, db_ref):
                r[...] = jnp.zeros_like(r)

        a = a_ref[...]
        halo = jnp.where(i > 0, prev_ref[...], 0.0)
        dyp = dzp_ref[...]
        dyp_next = jnp.where(i < nb - 1, dzp_next_ref[...], 0.0)
        dq_ext = jnp.concatenate([dyp, dyp_next], axis=0) * ps_ref[...]
        t_ext = lax.broadcasted_iota(jnp.int32, (ext, 1), 0) + i * tm
        for g in range(G):
            cols = slice(g * PG, (g + 1) * PG)
            w = POOL_WINDOWS[g]
            pb = _pool_inputs(a, halo, i * tm, tm, g, PG).astype(MXU_DTYPE)
            y_pre = jnp.dot(pb, pw_ref[g], preferred_element_type=jnp.float32)
            dps_ref[:, cols] += jnp.sum(dyp[:, cols] * y_pre, axis=0, keepdims=True)
            dqb = dq_ext[:, cols].astype(MXU_DTYPE)
            dpw_ref[g] += lax.dot_general(pb, dqb[:tm], TN, preferred_element_type=jnp.float32)
            dp = lax.dot_general(dqb, pw_ref[g], NT, preferred_element_type=jnp.float32)
            f = dp / jnp.minimum(t_ext + 1, w).astype(jnp.float32)
            span = 1
            while span < w:
                f = f + pltpu.roll(f, ext - span, axis=0)
                span *= 2
            dproj_ref[:, cols] = (f[:tm] - dp[:tm]).astype(dproj_ref.dtype)

        v = v_ref[...]
        rv = _rsqrt_mean_sq(v)
        vhat = v * rv
        gs = gs_ref[...]
        vnb = (vhat * gs).astype(MXU_DTYPE)
        u = u_ref[...]
        dys = dzs_ref[...]
        mask = _chunk_mask(BLK)
        for h in range(H):
            wm = jnp.where(mask, ws_ref[h], 0.0).astype(MXU_DTYPE)
            bias = b_ref[h]
            cols = slice(h * HD, (h + 1) * HD)
            dm_sum = jnp.zeros((BLK, HD), jnp.float32)
            dws_sum = jnp.zeros((BLK, BLK), jnp.float32)
            for n in range(tm // BLK):
                rows = slice(n * BLK, (n + 1) * BLK)
                vblk = vnb[rows, cols]
                mixed = jnp.dot(wm, vblk, preferred_element_type=jnp.float32) + bias
                dy = dys[rows, cols]
                dproj_ref[rows, DP + h * HD:DP + (h + 1) * HD] = (dy * mixed).astype(dproj_ref.dtype)
                dm = dy * u[rows, cols]
                dm_sum = dm_sum + dm
                dmb = dm.astype(MXU_DTYPE)
                dws_sum = dws_sum + lax.dot_general(dmb, vblk, NT, preferred_element_type=jnp.float32)
                dvn_ref[rows, cols] = lax.dot_general(wm, dmb, TN, preferred_element_type=jnp.float32)
            dws_ref[h] += jnp.where(mask, dws_sum, 0.0)
            db_ref[h:h + 1, :] += jnp.sum(dm_sum.T, axis=0, keepdims=True)
        dvn = dvn_ref[...]
        dvg = dvn * gs
        dv = rv * (dvg - vhat * jnp.mean(dvg * vhat, axis=-1, keepdims=True))
        dgs_ref[...] += jnp.sum(dvn * vhat, axis=0, keepdims=True)
        dproj_ref[:, DP + DS:] = dv.astype(dproj_ref.dtype)

    a_spec, prev_spec, next_spec, u_spec, v_spec = _mixer_specs(S, tm, DP, DS)
    dzp_spec = pl.BlockSpec((tm, DP), lambda i: (i, 0))
    dzs_spec = pl.BlockSpec((tm, DS), lambda i: (i, 1))
    small = [pw, ps_row, gs_row, ws, b_full]
    out_shape = [
        jax.ShapeDtypeStruct((S, DP + 2 * DS), MXU_DTYPE),
        jax.ShapeDtypeStruct(pw.shape, jnp.float32),
        jax.ShapeDtypeStruct(ps_row.shape, jnp.float32),
        jax.ShapeDtypeStruct(gs_row.shape, jnp.float32),
        jax.ShapeDtypeStruct(ws.shape, jnp.float32),
        jax.ShapeDtypeStruct((H, BLK), jnp.float32),
    ]
    out_specs = [pl.BlockSpec((tm, DP + 2 * DS), lambda i: (i, 0))] + [_whole(s) for s in out_shape[1:]]
    return pl.pallas_call(
        body, name="mixer_bwd", grid=(nb,),
        in_specs=[a_spec, prev_spec, u_spec, v_spec, dzp_spec, next_spec, dzs_spec] + [_whole(s) for s in small],
        out_specs=out_specs, out_shape=out_shape,
        scratch_shapes=[pltpu.VMEM((tm, DS), jnp.float32)],
        compiler_params=_params(1),
    )(proj, proj, proj, proj, dz, dz, dz, *small)


def _softmax_rows(q, k, scale):
    s = lax.dot_general(q, k, NT, preferred_element_type=jnp.float32) * scale
    e = jnp.exp(s - jnp.max(s, axis=-1, keepdims=True))
    return e / jnp.sum(e, axis=-1, keepdims=True)


def _attn_fwd(q, k, v, tm=512):
    S, D = q.shape
    M = k.shape[0]
    HD = D // N_XATTN_HEADS
    scale = HD ** -0.5
    tm = _tile(S, tm, 16)

    def body(q_ref, k_ref, v_ref, o_ref):
        for h in range(N_XATTN_HEADS):
            cols = slice(h * HD, (h + 1) * HD)
            p = _softmax_rows(q_ref[:, cols], k_ref[:, cols], scale)
            o = jnp.dot(p.astype(MXU_DTYPE), v_ref[:, cols], preferred_element_type=jnp.float32)
            o_ref[:, cols] = o.astype(o_ref.dtype)

    row = pl.BlockSpec((tm, D), lambda i: (i, 0))
    mem = pl.BlockSpec((M, D), lambda i: (0, 0))
    return pl.pallas_call(
        body, name="attn_fwd", grid=(S // tm,), in_specs=[row, mem, mem], out_specs=row,
        out_shape=jax.ShapeDtypeStruct((S, D), MXU_DTYPE), compiler_params=_params(1),
    )(q, k, v)


def _attn_bwd(q, k, v, do, tm=512):
    S, D = q.shape
    M = k.shape[0]
    HD = D // N_XATTN_HEADS
    scale = HD ** -0.5
    tm = _tile(S, tm, 16)

    def body(q_ref, k_ref, v_ref, do_ref, dq_ref, dk_ref, dv_ref):
        @pl.when(pl.program_id(0) == 0)
        def _():
            dk_ref[...] = jnp.zeros_like(dk_ref)
            dv_ref[...] = jnp.zeros_like(dv_ref)

        for h in range(N_XATTN_HEADS):
            cols = slice(h * HD, (h + 1) * HD)
            qh, kh, vh, doh = q_ref[:, cols], k_ref[:, cols], v_ref[:, cols], do_ref[:, cols]
            p = _softmax_rows(qh, kh, scale)
            dp = lax.dot_general(doh, vh, NT, preferred_element_type=jnp.float32)
            dv_ref[:, cols] += lax.dot_general(p.astype(MXU_DTYPE), doh, TN, preferred_element_type=jnp.float32)
            ds = (p * (dp - jnp.sum(dp * p, axis=-1, keepdims=True)) * scale).astype(MXU_DTYPE)
            dq_ref[:, cols] = jnp.dot(ds, kh, preferred_element_type=jnp.float32).astype(dq_ref.dtype)
            dk_ref[:, cols] += lax.dot_general(ds, qh, TN, preferred_element_type=jnp.float32)

    row = pl.BlockSpec((tm, D), lambda i: (i, 0))
    mem = pl.BlockSpec((M, D), lambda i: (0, 0))
    return pl.pallas_call(
        body, name="attn_bwd", grid=(S // tm,), in_specs=[row, mem, mem, row], out_specs=[row, mem, mem],
        out_shape=[jax.ShapeDtypeStruct((S, D), MXU_DTYPE), jax.ShapeDtypeStruct((M, D), jnp.float32),
                   jax.ShapeDtypeStruct((M, D), jnp.float32)],
        compiler_params=_params(1),
    )(q, k, v, do)


def _proj_residual_norm(name, a, w2d, resid, g_row, tm=512, after=None):
    M, K = a.shape
    N = w2d.shape[1]
    tm = _tile(M, tm, 16)

    def epilogue(accs, ex_refs, out_refs, i, j):
        x_new = ex_refs[0][...] + accs[0]
        out_refs[0][...] = x_new
        out_refs[1][...] = (x_new * _rsqrt_mean_sq(x_new) * ex_refs[1][...]).astype(MXU_DTYPE)

    row = lambda i, j, k: (i, 0)
    return _fused_matmul(
        name, (M // tm, 1, 1),
        [(a, (tm, K), row), (w2d, (K, N), lambda i, j, k: (0, 0))],
        [(0, 1, 0, NN)], [(tm, N)],
        [(resid, (tm, N), row), (g_row, (1, N), lambda i, j, k: (0, 0))] + _order_after(after),
        [((M, N), jnp.float32, (tm, N), row), ((M, N), MXU_DTYPE, (tm, N), row)], epilogue)


def _ffn_gate_up(h, wg3, wu3, tm=512):
    M, K = h.shape
    J, _, Nc = wg3.shape
    tm = _tile(M, tm, 16)

    def epilogue(accs, ex_refs, out_refs, i, j):
        gate, up = accs
        sg = _sigmoid(gate)
        silu = gate * sg
        out_refs[0][...] = (silu * up).astype(MXU_DTYPE)
        out_refs[1][...] = (up * (sg * (1.0 + gate * (1.0 - sg)))).astype(MXU_DTYPE)
        out_refs[2][...] = silu.astype(MXU_DTYPE)

    col = lambda i, j, k: (i, j)
    wspec = lambda i, j, k: (j, 0, 0)
    return _fused_matmul(
        "ffn_gate_up", (M // tm, J, 1),
        [(h, (tm, K), lambda i, j, k: (i, 0)), (wg3, (None, K, Nc), wspec), (wu3, (None, K, Nc), wspec)],
        [(0, 1, 0, NN), (0, 2, 1, NN)], [(tm, Nc)] * 2, [],
        [((M, J * Nc), MXU_DTYPE, (tm, Nc), col)] * 3, epilogue)


def _ffn_down(act, wd2d, x2, tm=512, tn=1024):
    M, F = act.shape
    D = wd2d.shape[1]
    tm, tn = _tile(M, tm, 16), _tile(D, tn, LANES)

    def epilogue(accs, ex_refs, out_refs, i, j):
        out_refs[0][...] = ex_refs[0][...] + accs[0]

    tile = lambda i, j, k: (i, j)
    return _fused_matmul(
        "ffn_down", (M // tm, D // tn, 1),
        [(act, (tm, F), lambda i, j, k: (i, 0)), (wd2d, (F, tn), lambda i, j, k: (0, j))],
        [(0, 1, 0, NN)], [(tm, tn)], [(x2, (tm, tn), tile)],
        [((M, D), jnp.float32, (tm, tn), tile)], epilogue)[0]


def _final_norm_loss(x3, target, g_row, tm=256):
    S, D = x3.shape
    tm = _tile(S, tm, 16)

    def body(x_ref, t_ref, g_ref, dx_ref, dxl_ref, loss_ref, dg_ref):
        x3v = x_ref[...]
        g = g_ref[...]
        r = _rsqrt_mean_sq(x3v)
        xhat = x3v * r
        diff = xhat * g - t_ref[...]
        dy = diff / D
        dyg = dy * g
        dx = r * (dyg - xhat * jnp.mean(dyg * xhat, axis=-1, keepdims=True))
        dx_ref[...] = dx
        dxl_ref[...] = dx.astype(MXU_DTYPE)

        @pl.when(pl.program_id(0) == 0)
        def _():
            loss_ref[...] = jnp.zeros_like(loss_ref)
            dg_ref[...] = jnp.zeros_like(dg_ref)

        loss_ref[...] += 0.5 * _sum_all(jnp.mean(diff * diff, axis=-1, keepdims=True))
        dg_ref[...] += jnp.sum(dy * xhat, axis=0, keepdims=True)

    row = pl.BlockSpec((tm, D), lambda i: (i, 0))
    vec = pl.BlockSpec((1, D), lambda i: (0, 0))
    return pl.pallas_call(
        body, name="final_norm_loss", grid=(S // tm,), in_specs=[row, row, vec],
        out_specs=[row, row, pl.BlockSpec((1, 1), lambda i: (0, 0)), vec],
        out_shape=[jax.ShapeDtypeStruct((S, D), jnp.float32), jax.ShapeDtypeStruct((S, D), MXU_DTYPE),
                   jax.ShapeDtypeStruct((1, 1), jnp.float32), jax.ShapeDtypeStruct((1, D), jnp.float32)],
        compiler_params=_params(1),
    )(x3, target, g_row)


def _ffn_down_bwd(dx3_low, wd2d, act_by_gate, act_by_up, n_shards, tm=1024):
    M, D = dx3_low.shape
    F = wd2d.shape[0]
    tn = F // n_shards
    tm = _tile(M, tm, 16)

    def epilogue(accs, ex_refs, out_refs, i, j):
        dact = accs[0]
        out_refs[0][...] = (dact * ex_refs[0][...].astype(jnp.float32)).astype(MXU_DTYPE)
        out_refs[1][...] = (dact * ex_refs[1][...].astype(jnp.float32)).astype(MXU_DTYPE)

    col = lambda i, j, k: (i, j)
    return _fused_matmul(
        "ffn_down_bwd", (M // tm, n_shards, 1),
        [(dx3_low, (tm, D), lambda i, j, k: (i, 0)), (wd2d, (tn, D), lambda i, j, k: (j, 0))],
        [(0, 1, 0, NT)], [(tm, tn)],
        [(act_by_gate, (tm, tn), col), (act_by_up, (tm, tn), col)],
        [((M, F), MXU_DTYPE, (tm, tn), col)] * 2, epilogue)


def _matmul_nt_norm_bwd(name, a, b, xin, g_row, dres, low_dtype, tm=256, after=None):
    M = a.shape[0]
    sharded = b.ndim == 3
    N, Kc = b.shape[-2:]
    J = b.shape[0] if sharded else 1
    tm = _tile(M, tm, 16)
    has_low = low_dtype is not None

    def epilogue(accs, ex_refs, out_refs, i, j):
        dx, dg = _rmsnorm_bwd_values(accs[0], ex_refs[0][...], ex_refs[1][...])
        dx = dx + ex_refs[2][...]
        out_refs[0][...] = dx
        if has_low:
            out_refs[1][...] = dx.astype(low_dtype)
        dg_ref = out_refs[-1]

        @pl.when(i == 0)
        def _():
            dg_ref[...] = jnp.zeros_like(dg_ref)

        dg_ref[...] += dg

    row = lambda i, j, k: (i, 0)
    const = lambda i, j, k: (0, 0)
    b_operand = (b, (None, N, Kc), lambda i, j, k: (k, 0, 0)) if sharded else (b, (N, Kc), const)
    outs = [((M, N), jnp.float32, (tm, N), row)]
    if has_low:
        outs.append(((M, N), low_dtype, (tm, N), row))
    outs.append(((1, N), jnp.float32, (1, N), const))
    return _fused_matmul(
        name, (M // tm, 1, J), [(a, (tm, Kc), lambda i, j, k: (i, k)), b_operand],
        [(0, 1, 0, NT)], [(tm, N)],
        [(xin, (tm, N), row), (g_row, (1, N), const), (dres, (tm, N), row)] + _order_after(after),
        outs, epilogue)


def _row_tile(R, C, target_bytes=2 ** 21):
    return _tile(R, max(16, target_bytes // (4 * C)), 16)


def _prefetch_call(body, name, grid, pos, in_specs, out_specs, out_shape, ins, after=None):
    ordered = [] if after is None else [after]
    grid_spec = pltpu.PrefetchScalarGridSpec(
        num_scalar_prefetch=1, grid=grid, in_specs=list(in_specs) + [ANY] * len(ordered), out_specs=out_specs)
    return pl.pallas_call(body, name=name, grid_spec=grid_spec, out_shape=out_shape,
                          compiler_params=_params(len(grid)))(pos, *ins, *ordered)


def _runs(arrays):
    runs = []
    for i, a in enumerate(arrays):
        if runs and arrays[runs[-1][-1]].shape == a.shape:
            runs[-1].append(i)
        else:
            runs.append([i])
    return runs


def _per_run(fn, names, *array_lists):
    out = [None] * len(names)
    for run in _runs(array_lists[0]):
        results = fn("_".join(names[i] for i in run), *[[arrays[i] for i in run] for arrays in array_lists])
        for i, r in zip(run, results):
            out[i] = r
    return out


def _cast_into_slabs(name, xs, dtype, pos, after=None):
    n = len(xs)
    R, C = xs[0].shape
    tr = _row_tile(R, C * n, 2 ** 22)

    def body(pos_ref, *refs):
        for t in range(n):
            refs[len(refs) - n + t][...] = refs[t][...].astype(dtype)

    return _prefetch_call(
        body, "cast_" + name, (R // tr,), pos, [pl.BlockSpec((tr, C), lambda i, pos: (i, 0))] * n,
        [pl.BlockSpec((None, tr, C), lambda i, pos: (pos[0], i, 0))] * n,
        [jax.ShapeDtypeStruct((N_CHIPS, R, C), dtype)] * n, xs, after)


def _chip_sum(name, partials, gots, dtype, pos):
    n = len(partials)
    J, R, C = partials[0].shape
    Rh = R // 2
    tr = _row_tile(Rh, C * n, 2 ** 22)

    def body(pos_ref, *refs):
        for t in range(n):
            refs[2 * n + t][...] = (refs[t][...].astype(jnp.float32) + refs[n + t][...].astype(jnp.float32)).astype(dtype)

    blk = pl.BlockSpec((None, tr, C), lambda j, i, pos: (j, i, 0))
    return _prefetch_call(
        body, "chip_sum_" + name, (J, Rh // tr), pos,
        [pl.BlockSpec((None, None, tr, C), lambda j, i, pos: (j, pos[1], i, 0))] * n + [blk] * n, [blk] * n,
        [jax.ShapeDtypeStruct((J, Rh, C), dtype)] * n, [p.reshape(J, 2, Rh, C) for p in partials] + list(gots))


def _owner_sum(name, chip_sums, others, pos):
    n = len(chip_sums)
    J, Rh, C = chip_sums[0].shape
    n_other = others[0].shape[0]
    tr = _row_tile(Rh, C * (n_other + 1) * n, 2 ** 22)

    def body(pos_ref, *refs):
        for t in range(n):
            acc = refs[t][...].astype(jnp.float32)
            for k in range(n_other):
                acc = acc + refs[n + t][k].astype(jnp.float32)
            refs[2 * n + t][...] = acc

    return _prefetch_call(
        body, "owner_sum_" + name, (Rh // tr,), pos,
        [pl.BlockSpec((None, tr, C), lambda i, pos: (pos[0], i, 0))] * n
        + [pl.BlockSpec((n_other, tr, C), lambda i, pos: (0, i, 0))] * n,
        [pl.BlockSpec((None, tr, C), lambda i, pos: (pos[1], i, 0))] * n,
        [jax.ShapeDtypeStruct((2, Rh, C), jnp.float32)] * n, list(chip_sums) + list(others))


def _adamw_values(w, g, m, v):
    m = ADAM_B1 * m + (1.0 - ADAM_B1) * g
    v = ADAM_B2 * v + (1.0 - ADAM_B2) * (g * g)
    m_hat = m / (1.0 - ADAM_B1 ** ADAM_STEP)
    v_hat = v / (1.0 - ADAM_B2 ** ADAM_STEP)
    delta = -ADAM_LR * (m_hat / (jnp.sqrt(v_hat) + ADAM_EPS) + ADAM_WD * w)
    return delta, m, v


def _adamw(name, ws, gs, ms, vs):
    n = len(ws)
    R, C = ws[0].shape
    tr = _row_tile(R, C * n, 2 ** 20)

    def body(*refs):
        w_refs, g_refs, m_refs, v_refs = (refs[q * n:(q + 1) * n] for q in range(4))
        outs = refs[4 * n:]
        for t in range(n):
            g = g_refs[t][...]
            outs[4 * t][...] = g
            outs[4 * t + 1][...], outs[4 * t + 2][...], outs[4 * t + 3][...] = _adamw_values(
                w_refs[t][...], g, m_refs[t][...], v_refs[t][...])

    blk = pl.BlockSpec((tr, C), lambda i: (i, 0))
    res = pl.pallas_call(
        body, name="adamw_" + name, grid=(R // tr,), in_specs=[blk] * (4 * n), out_specs=[blk] * (4 * n),
        out_shape=[jax.ShapeDtypeStruct((R, C), jnp.float32)] * (4 * n), compiler_params=_params(1),
    )(*ws, *gs, *ms, *vs)
    return [tuple(res[4 * t:4 * t + 4]) for t in range(n)]


def _adamw_small(g_packed, ws, ms, vs, row_counts):
    n = len(ws)

    def body(*refs):
        g_ref = refs[0]
        w_refs, m_refs, v_refs = refs[1:1 + n], refs[1 + n:1 + 2 * n], refs[1 + 2 * n:1 + 3 * n]
        outs = refs[1 + 3 * n:]
        off = 0
        for t in range(n):
            g = g_ref[off:off + row_counts[t], :]
            off += row_counts[t]
            delta, m_new, v_new = _adamw_values(w_refs[t][...], g, m_refs[t][...], v_refs[t][...])
            outs[4 * t][...] = g
            outs[4 * t + 1][...] = delta
            outs[4 * t + 2][...] = m_new
            outs[4 * t + 3][...] = v_new

    out_shape = []
    for t in range(n):
        out_shape += [jax.ShapeDtypeStruct((row_counts[t], LANES), jnp.float32)] * 4
    return pl.pallas_call(
        body, name="adamw_small", in_specs=[VMEM_WHOLE] * (1 + 3 * n), out_specs=[VMEM_WHOLE] * (4 * n),
        out_shape=out_shape, compiler_params=pltpu.CompilerParams(vmem_limit_bytes=VMEM_LIMIT_BYTES),
    )(g_packed, *ws, *ms, *vs)


def _position():
    x, y, c = lax.axis_index("x"), lax.axis_index("y"), lax.axis_index("c")
    other_chips = [(1 - x, y), (x, 1 - y), (1 - x, 1 - y)]
    return x, y, c, other_chips


def _half(rows, which):
    return pl.ds(which * (rows // 2), rows // 2)


def _chunk_index(shape, dtype):
    rows = shape[-2]
    total = math.prod(shape) * jnp.dtype(dtype).itemsize
    n = max(1, min(rows // COPY_ROW_ALIGN, total // COPY_CHUNK_BYTES))
    step = -(-rows // n)
    step = -(-step // COPY_ROW_ALIGN) * COPY_ROW_ALIGN
    lead = (slice(None),) * (len(shape) - 2)
    return [lead + (pl.ds(s, min(step, rows - s)),) for s in range(0, rows, step)]


def _start_in_chunks(make, src, dst, waited_here=True):
    for idx in _chunk_index(src.shape, src.dtype):
        make(src.at[idx], dst.at[idx]).start()
    return make(src, dst) if waited_here else None


def _remote_copier(send_sem, recv_sem, to):
    return lambda src, dst: pltpu.make_async_remote_copy(
        src_ref=src, dst_ref=dst, send_sem=send_sem, recv_sem=recv_sem, device_id=to, device_id_type=MESH)


def _slab_half(ref, chip, which):
    return ref.at[chip, _half(ref.shape[1], which)]


def _slab_quarter(ref, chip, which, q):
    rows = ref.shape[1] // 4
    return ref.at[chip, pl.ds(which * 2 * rows + q * rows, rows)]


def _gather_step(name, arrived=(), ring=(), direct=(), after=None):
    groups = [g[2] for g in arrived] + list(direct) + list(ring)
    per_tensor = [2] * len(arrived) + [3] * len(direct) + [2] * len(ring)
    flat = [a for g in groups for a in g]
    T, G, A = len(flat), len(groups), len(arrived)
    ordered = [] if after is None else [after]
    sems_in = [sem for g in arrived for sem in g[:2]]

    def body(*refs):
        ins = refs[:T]
        first_hop = refs[T:T + 2 * A]
        out_sems = refs[T + 2 * A + len(ordered):T + 2 * A + len(ordered) + 2 * G]
        token = refs[-1]
        x, y, c, other_chips = _position()
        mine = 2 * x + y
        x_nbr, y_nbr = (1 - x, y, c), (x, 1 - y, c)
        from_x, from_y = 2 * (1 - x) + y, 2 * x + (1 - y)
        t0 = 0
        for gi, group in enumerate(groups):
            send, recv = out_sems[2 * gi], out_sems[2 * gi + 1]
            for t in range(len(group)):
                ref = ins[t0 + t]
                own = _slab_half(ref, mine, c)
                if gi < A:
                    send1, recv1 = first_hop[2 * gi], first_hop[2 * gi + 1]
                    for k, (peer, chip) in enumerate(((x_nbr, from_x), (y_nbr, from_y))):
                        cp = _remote_copier(send1.at[2 * t + k], recv1.at[2 * t + k], peer)(own, _slab_half(ref, chip, c))
                        cp.wait_send()
                        cp.wait_recv()
                    for k, (peer, chip) in enumerate(((y_nbr, from_x), (x_nbr, from_y))):
                        piece = _slab_quarter(ref, chip, c, k)
                        _start_in_chunks(_remote_copier(send.at[2 * t + k], recv.at[2 * t + k], peer), piece, piece,
                                         waited_here=False)
                elif gi < A + len(direct):
                    for k, (cx, cy) in enumerate(other_chips):
                        _start_in_chunks(_remote_copier(send.at[3 * t + k], recv.at[3 * t + k], (cx, cy, c)), own, own,
                                         waited_here=False)
                else:
                    for k, peer in enumerate((x_nbr, y_nbr)):
                        _start_in_chunks(_remote_copier(send.at[2 * t + k], recv.at[2 * t + k], peer), own, own,
                                         waited_here=False)
            t0 += len(group)
        token[...] = jnp.zeros_like(token)

    sem_shapes = []
    for group, n in zip(groups, per_tensor):
        sem_shapes += [pltpu.SemaphoreType.DMA((n * len(group),))] * 2
    res = pl.pallas_call(
        body, name=name, in_specs=[HBM_SPEC] * T + [SEM_SPEC] * (2 * A) + [ANY] * len(ordered),
        out_specs=[SEM_SPEC] * (2 * G) + [HBM_SPEC] * T + [VMEM_WHOLE],
        out_shape=sem_shapes + [pltpu.HBM(a.shape, a.dtype) for a in flat] + [jax.ShapeDtypeStruct((8, LANES), jnp.float32)],
        input_output_aliases={t: 2 * G + t for t in range(T)},
        compiler_params=pltpu.CompilerParams(has_side_effects=SIDE_EFFECT),
    )(*[pltpu.with_memory_space_constraint(a, pltpu.HBM) for a in flat], *sems_in, *ordered)
    flights, t0 = [], 2 * G
    for gi, group in enumerate(groups):
        flights.append((res[2 * gi], res[2 * gi + 1], res[t0:t0 + len(group)]))
        t0 += len(group)
    return flights[:A], flights[A:A + len(direct)], flights[A + len(direct):], res[-1]


def _relayed_wait(name, flight, after):
    send_sems, recv_sems, slabs = flight
    T = len(slabs)

    def body(*refs):
        ins = refs[:T]
        send, recv = refs[T:T + 2]
        x, y, c, _ = _position()
        x_nbr, y_nbr = (1 - x, y, c), (x, 1 - y, c)
        from_x, from_y, diagonal = 2 * (1 - x) + y, 2 * x + (1 - y), 2 * (1 - x) + (1 - y)
        for t in range(T):
            for k, (peer, chip) in enumerate(((y_nbr, from_x), (x_nbr, from_y))):
                cp = _remote_copier(send.at[2 * t + k], recv.at[2 * t + k], peer)(
                    _slab_quarter(ins[t], chip, c, k), _slab_quarter(ins[t], diagonal, c, k))
                cp.wait_send()
                cp.wait_recv()

    return pl.pallas_call(
        body, name=name, in_specs=[HBM_SPEC] * T + [SEM_SPEC, SEM_SPEC, ANY], out_specs=[HBM_SPEC] * T,
        out_shape=[pltpu.HBM(a.shape, a.dtype) for a in slabs], input_output_aliases={t: t for t in range(T)},
        compiler_params=pltpu.CompilerParams(has_side_effects=SIDE_EFFECT),
    )(*slabs, send_sems, recv_sems, after)


def _gather_wait(name, flight, after):
    send_sems, recv_sems, slabs = flight
    T = len(slabs)

    def body(*refs):
        ins = refs[:T]
        send, recv = refs[T:T + 2]
        x, y, c, other_chips = _position()
        mine = 2 * x + y
        for t in range(T):
            own = _slab_half(ins[t], mine, c)
            for k, (cx, cy) in enumerate(other_chips):
                landed = _slab_half(ins[t], 2 * cx + cy, c)
                cp = _remote_copier(send.at[3 * t + k], recv.at[3 * t + k], (cx, cy, c))(own, landed)
                cp.wait_send()
                cp.wait_recv()

    return pl.pallas_call(
        body, name=name, in_specs=[HBM_SPEC] * T + [SEM_SPEC, SEM_SPEC, ANY], out_specs=[HBM_SPEC] * T,
        out_shape=[pltpu.HBM(s.shape, s.dtype) for s in slabs], input_output_aliases={t: t for t in range(T)},
        compiler_params=pltpu.CompilerParams(has_side_effects=SIDE_EFFECT),
    )(*slabs, send_sems, recv_sems, after)


def _share_with_sibling(name, slabs):
    T = len(slabs)

    def body(*refs):
        outs = refs[T:2 * T]
        send_sems, recv_sems = refs[2 * T:]
        x, y, c, other_chips = _position()
        sibling = (x, y, 1 - c)
        started = []
        for t in range(T):
            for k, (cx, cy) in enumerate(other_chips):
                landed = _slab_half(outs[t], 2 * cx + cy, c)
                started.append(_start_in_chunks(_remote_copier(send_sems.at[t, k], recv_sems.at[t, k], sibling), landed, landed))
        for t in range(T):
            for k, (cx, cy) in enumerate(other_chips):
                arriving = _slab_half(outs[t], 2 * cx + cy, 1 - c)
                _remote_copier(send_sems.at[t, k], recv_sems.at[t, k], sibling)(arriving, arriving).wait_recv()
        for cp in started:
            cp.wait_send()

    return pl.pallas_call(
        body, name=name, in_specs=[ANY] * T, out_specs=[ANY] * T,
        out_shape=[jax.ShapeDtypeStruct(s.shape, s.dtype) for s in slabs],
        input_output_aliases={t: t for t in range(T)},
        scratch_shapes=[pltpu.SemaphoreType.DMA((T, 3)), pltpu.SemaphoreType.DMA((T, 3))],
    )(*slabs)


def _swap_halves(name, grads):
    T = len(grads)

    def body(*refs):
        ins, got = refs[:T], refs[T:2 * T]
        send_sems, recv_sems = refs[2 * T:]
        x, y, c, _ = _position()
        remote = []
        for t in range(T):
            rows = ins[t].shape[1]
            remote.append(_start_in_chunks(
                _remote_copier(send_sems.at[t], recv_sems.at[t], (x, y, 1 - c)), ins[t].at[:, _half(rows, 1 - c)], got[t]))
        for cp in remote:
            cp.wait()

    return pl.pallas_call(
        body, name=name, in_specs=[ANY] * T, out_specs=[ANY] * T,
        out_shape=[jax.ShapeDtypeStruct((g.shape[0], g.shape[1] // 2, g.shape[2]), g.dtype) for g in grads],
        scratch_shapes=[pltpu.SemaphoreType.DMA((T,)), pltpu.SemaphoreType.DMA((T,))],
    )(*grads)


def _swap_start(name, grads):
    T = len(grads)
    lands = [lax.empty((g.shape[0], g.shape[1] // 2, g.shape[2]), g.dtype) for g in grads]

    def body(*refs):
        ins, got = refs[:T], refs[T:2 * T]
        send_sems, recv_sems = refs[2 * T:2 * T + 2]
        token = refs[-1]
        x, y, c, _ = _position()
        for t in range(T):
            _start_in_chunks(_remote_copier(send_sems.at[t], recv_sems.at[t], (x, y, 1 - c)),
                             ins[t].at[:, _half(ins[t].shape[1], 1 - c)], got[t], waited_here=False)
        token[...] = jnp.zeros_like(token)

    arrays = list(grads) + lands
    res = pl.pallas_call(
        body, name=name, in_specs=[HBM_SPEC] * (2 * T),
        out_specs=[SEM_SPEC, SEM_SPEC] + [HBM_SPEC] * (2 * T) + [VMEM_WHOLE],
        out_shape=[pltpu.SemaphoreType.DMA((T,)), pltpu.SemaphoreType.DMA((T,))]
        + [pltpu.HBM(a.shape, a.dtype) for a in arrays] + [jax.ShapeDtypeStruct((8, LANES), jnp.float32)],
        input_output_aliases={i: 2 + i for i in range(2 * T)},
        compiler_params=pltpu.CompilerParams(has_side_effects=SIDE_EFFECT),
    )(*[pltpu.with_memory_space_constraint(a, pltpu.HBM) for a in arrays])
    return res[0], res[1], res[2:2 + T], res[2 + T:2 + 2 * T], res[-1]


def _swap_wait(name, flight, after):
    send_sems, recv_sems, grads, lands, _ = flight
    T = len(grads)

    def body(*refs):
        ins, got = refs[:T], refs[T:2 * T]
        send, recv = refs[2 * T:2 * T + 2]
        x, y, c, _ = _position()
        for t in range(T):
            cp = _remote_copier(send.at[t], recv.at[t], (x, y, 1 - c))(ins[t].at[:, _half(ins[t].shape[1], 1 - c)], got[t])
            cp.wait_send()
            cp.wait_recv()

    arrays = list(grads) + list(lands)
    res = pl.pallas_call(
        body, name=name, in_specs=[HBM_SPEC] * (2 * T) + [SEM_SPEC, SEM_SPEC, ANY], out_specs=[HBM_SPEC] * (2 * T),
        out_shape=[pltpu.HBM(a.shape, a.dtype) for a in arrays], input_output_aliases={i: i for i in range(2 * T)},
        compiler_params=pltpu.CompilerParams(has_side_effects=SIDE_EFFECT),
    )(*arrays, send_sems, recv_sems, after)
    return res[:T], res[T:]


def _scatter_start(name, chip_sums):
    T = len(chip_sums)
    lands = [lax.empty((3,) + s.shape[1:], s.dtype) for s in chip_sums]

    def body(*refs):
        srcs, dsts = refs[:T], refs[T:2 * T]
        send_sems, recv_sems = refs[2 * T:2 * T + 2]
        token = refs[-1]
        x, y, c, other_chips = _position()
        for t in range(T):
            for k, (cx, cy) in enumerate(other_chips):
                _start_in_chunks(_remote_copier(send_sems.at[3 * t + k], recv_sems.at[3 * t + k], (cx, cy, c)),
                                 srcs[t].at[2 * cx + cy], dsts[t].at[k], waited_here=False)
        token[...] = jnp.zeros_like(token)

    hbm = [pltpu.HBM(a.shape, a.dtype) for a in list(chip_sums) + lands]
    res = pl.pallas_call(
        body, name=name, in_specs=[HBM_SPEC] * (2 * T),
        out_specs=[SEM_SPEC, SEM_SPEC] + [HBM_SPEC] * (2 * T) + [VMEM_WHOLE],
        out_shape=[pltpu.SemaphoreType.DMA((3 * T,)), pltpu.SemaphoreType.DMA((3 * T,))] + hbm
        + [jax.ShapeDtypeStruct((8, LANES), jnp.float32)],
        input_output_aliases={i: 2 + i for i in range(2 * T)},
        compiler_params=pltpu.CompilerParams(has_side_effects=SIDE_EFFECT),
    )(*[pltpu.with_memory_space_constraint(a, pltpu.HBM) for a in list(chip_sums) + lands])
    return res[0], res[1], res[2:2 + T], res[2 + T:2 + 2 * T], res[-1]


def _scatter_wait(name, send_sems, recv_sems, chip_sums, lands, after):
    T = len(chip_sums)

    def body(*refs):
        srcs, dsts = refs[:T], refs[T:2 * T]
        send, recv = refs[2 * T:2 * T + 2]
        x, y, c, other_chips = _position()
        for t in range(T):
            for k, (cx, cy) in enumerate(other_chips):
                cp = _remote_copier(send.at[3 * t + k], recv.at[3 * t + k], (cx, cy, c))(srcs[t].at[2 * cx + cy], dsts[t].at[k])
                cp.wait_send()
                cp.wait_recv()

    arrays = list(chip_sums) + list(lands)
    res = pl.pallas_call(
        body, name=name, in_specs=[HBM_SPEC] * (2 * T) + [SEM_SPEC, SEM_SPEC, ANY],
        out_specs=[HBM_SPEC] * (2 * T), out_shape=[pltpu.HBM(a.shape, a.dtype) for a in arrays],
        input_output_aliases={i: i for i in range(2 * T)},
        compiler_params=pltpu.CompilerParams(has_side_effects=SIDE_EFFECT),
    )(*arrays, send_sems, recv_sems, after)
    return res[:T], res[T:]


def _as_shards(partial):
    return [p.reshape((N_CHIPS, -1, p.shape[-1])) for p in partial]


def _reduce_begin(tag, names, partial, pos):
    parts = _as_shards(partial)
    received = _swap_halves("swap_grad_halves_" + tag, parts)
    return _reduce_scatter_start(tag, names, parts, received, pos)


def _reduce_scatter_start(tag, names, parts, received, pos):
    chip_sums = _per_run(lambda name, p, r: _chip_sum(name, p, r, WIRE_DTYPE, pos), names, list(parts), list(received))
    return _scatter_start("scatter_start_" + tag, chip_sums)


def _reduce_end(tag, names, flight, after, pos):
    send_sems, recv_sems, chip_sums, lands, _ = flight
    chip_sums, others = _scatter_wait("scatter_wait_" + tag, send_sems, recv_sems, chip_sums, lands, after)
    reduced = _per_run(lambda name, cs, ot: _owner_sum(name, cs, ot, pos), names, list(chip_sums), list(others))
    return dict(zip(names, reduced))


def _join_start(name, halves):
    T = len(halves)

    def body(*refs):
        ins = refs[:T]
        send_sems, recv_sems = refs[T:T + 2]
        token = refs[-1]
        x, y, c, _ = _position()
        for t in range(T):
            _start_in_chunks(_remote_copier(send_sems.at[t], recv_sems.at[t], (x, y, 1 - c)),
                             ins[t].at[c], ins[t].at[c], waited_here=False)
        token[...] = jnp.zeros_like(token)

    res = pl.pallas_call(
        body, name=name, in_specs=[HBM_SPEC] * T,
        out_specs=[SEM_SPEC, SEM_SPEC] + [HBM_SPEC] * T + [VMEM_WHOLE],
        out_shape=[pltpu.SemaphoreType.DMA((T,)), pltpu.SemaphoreType.DMA((T,))]
        + [pltpu.HBM(h.shape, h.dtype) for h in halves] + [jax.ShapeDtypeStruct((8, LANES), jnp.float32)],
        input_output_aliases={t: 2 + t for t in range(T)},
        compiler_params=pltpu.CompilerParams(has_side_effects=SIDE_EFFECT),
    )(*[pltpu.with_memory_space_constraint(h, pltpu.HBM) for h in halves])
    return res[0], res[1], res[2:2 + T], res[-1]


def _join_wait(name, flight, after):
    send_sems, recv_sems, halves, _ = flight
    T = len(halves)

    def body(*refs):
        ins = refs[:T]
        send, recv = refs[T:T + 2]
        x, y, c, _ = _position()
        for t in range(T):
            cp = _remote_copier(send.at[t], recv.at[t], (x, y, 1 - c))(ins[t].at[c], ins[t].at[1 - c])
            cp.wait_send()
            cp.wait_recv()

    return pl.pallas_call(
        body, name=name, in_specs=[HBM_SPEC] * T + [SEM_SPEC, SEM_SPEC, ANY], out_specs=[HBM_SPEC] * T,
        out_shape=[pltpu.HBM(h.shape, h.dtype) for h in halves], input_output_aliases={t: t for t in range(T)},
        compiler_params=pltpu.CompilerParams(has_side_effects=SIDE_EFFECT),
    )(*halves, send_sems, recv_sems, after)


def _peers():
    x, y, c, _ = _position()
    flips = [(fx, fy, fc) for fx in (0, 1) for fy in (0, 1) for fc in (0, 1)][1:]
    return [(jnp.bitwise_xor(x, fx), jnp.bitwise_xor(y, fy), jnp.bitwise_xor(c, fc)) for fx, fy, fc in flips]


def _device_index(px, py, pc):
    return 4 * px + 2 * py + pc


def _exchange_start(name, slots):
    def body(slots_ref, send_sems, recv_sems, thru, token):
        x, y, c, _ = _position()
        own = slots_ref.at[_device_index(x, y, c)]
        for k, peer in enumerate(_peers()):
            _remote_copier(send_sems.at[k], recv_sems.at[k], peer)(own, own).start()
        token[...] = jnp.zeros_like(token)

    res = pl.pallas_call(
        body, name=name, in_specs=[HBM_SPEC], out_specs=[SEM_SPEC, SEM_SPEC, HBM_SPEC, VMEM_WHOLE],
        out_shape=[pltpu.SemaphoreType.DMA((N_DEV - 1,)), pltpu.SemaphoreType.DMA((N_DEV - 1,)),
                   pltpu.HBM(slots.shape, slots.dtype), jax.ShapeDtypeStruct((8, LANES), jnp.float32)],
        input_output_aliases={0: 2}, compiler_params=pltpu.CompilerParams(has_side_effects=SIDE_EFFECT),
    )(pltpu.with_memory_space_constraint(slots, pltpu.HBM))
    return res[0], res[1], res[2], res[3]


def _exchange_wait(name, flight, after):
    send_sems, recv_sems, slots, _ = flight

    def body(slots_ref, send, recv, after_ref, out_ref):
        x, y, c, _ = _position()
        own = slots_ref.at[_device_index(x, y, c)]
        for k, peer in enumerate(_peers()):
            cp = _remote_copier(send.at[k], recv.at[k], peer)(own, slots_ref.at[_device_index(*peer)])
            cp.wait_send()
            cp.wait_recv()

    return pl.pallas_call(
        body, name=name, in_specs=[HBM_SPEC, SEM_SPEC, SEM_SPEC, ANY], out_specs=HBM_SPEC,
        out_shape=pltpu.HBM(slots.shape, slots.dtype), input_output_aliases={0: 0},
        compiler_params=pltpu.CompilerParams(has_side_effects=SIDE_EFFECT),
    )(slots, send_sems, recv_sems, after)


def _sum_slots(name, slots):
    n, rows, lanes = slots.shape
    tr = rows if slots.size * 4 <= 2 ** 23 else _row_tile(rows, lanes * n)

    def body(s_ref, o_ref):
        acc = s_ref[0]
        for d in range(1, n):
            acc = acc + s_ref[d]
        o_ref[...] = acc

    return pl.pallas_call(
        body, name=name, grid=(rows // tr,), in_specs=[pl.BlockSpec((n, tr, lanes), lambda i: (0, i, 0))],
        out_specs=pl.BlockSpec((tr, lanes), lambda i: (i, 0)), out_shape=jax.ShapeDtypeStruct((rows, lanes), jnp.float32),
        compiler_params=_params(1))(slots)


BIG = ("w_in", "pool_w", "w_out", "w_q", "w_k", "w_v", "w_o", "w_gate", "w_up", "w_down")
SMALL = ("norm_mix_g", "pool_scale", "sgu_norm_g", "w_spatial", "b_spatial", "norm_xattn_g", "norm_mem_g",
         "norm_ffn_g", "final_norm_g")
GATHER_ORDER = (("w_in", "pool_w"), ("w_out",), ("w_q", "w_k", "w_v", "w_o"), ("w_gate", "w_up"), ("w_down",))
WEIGHTS = ("norm_mix_g", "w_in", "pool_w", "pool_scale", "sgu_norm_g", "w_spatial", "b_spatial", "w_out",
           "norm_xattn_g", "norm_mem_g", "w_q", "w_k", "w_v", "w_o", "norm_ffn_g", "w_gate", "w_up", "w_down",
           "final_norm_g")


def _as2d(a):
    return a.reshape(-1, a.shape[-1])


def _as_lanes(a):
    return a.reshape(-1, LANES)


def kernel(x, mem, norm_mix_g, w_in, pool_w, pool_scale, sgu_norm_g, w_spatial, b_spatial, w_out, norm_xattn_g, norm_mem_g, w_q, w_k, w_v, w_o, norm_ffn_g, w_gate, w_up, w_down, final_norm_g, loss_target, m_norm_mix_g, m_w_in, m_pool_w, m_pool_scale, m_sgu_norm_g, m_w_spatial, m_b_spatial, m_w_out, m_norm_xattn_g, m_norm_mem_g, m_w_q, m_w_k, m_w_v, m_w_o, m_norm_ffn_g, m_w_gate, m_w_up, m_w_down, m_final_norm_g, v_norm_mix_g, v_w_in, v_pool_w, v_pool_scale, v_sgu_norm_g, v_w_spatial, v_b_spatial, v_w_out, v_norm_xattn_g, v_norm_mem_g, v_w_q, v_w_k, v_w_v, v_w_o, v_norm_ffn_g, v_w_gate, v_w_up, v_w_down, v_final_norm_g):
    args = dict(locals())
    weights = {n: args[n] for n in WEIGHTS}
    moments_m = {n: args["m_" + n] for n in WEIGHTS}
    moments_v = {n: args["v_" + n] for n in WEIGHTS}
    low = MXU_DTYPE

    xs = x[0]
    mems = mem[0]
    target = loss_target[0]
    D = xs.shape[1]
    G, PG = pool_w.shape[1], pool_w.shape[3]

    pos = jnp.stack([2 * lax.axis_index("x") + lax.axis_index("y"), lax.axis_index("c")]).astype(jnp.int32)
    def cast(names, after=None):
        slabs = _per_run(lambda name, xs: _cast_into_slabs(name, xs, WIRE_DTYPE, pos, after), names,
                         [_as2d(weights[n]) for n in names])
        return dict(zip(names, slabs))

    def land(gi, flight, after, relayed=True):
        wait = _relayed_wait if relayed else _gather_wait
        landed = wait("gather_wait_%d" % gi, flight, after)
        return dict(zip(GATHER_ORDER[gi], _share_with_sibling("gather_share_%d" % gi, landed)))

    def rows(a):
        return a.reshape(-1, a.shape[-1])

    g_mix, g_xattn, g_mem, g_ffn = norm_mix_g, norm_xattn_g, norm_mem_g, norm_ffn_g
    g_final = final_norm_g.reshape(1, D)
    ws = w_spatial[0]
    b_full = jnp.broadcast_to(b_spatial[0][:, :, None], ws.shape[:2] + (sgu_norm_g.shape[1] // ws.shape[0],))

    first = cast(GATHER_ORDER[0])
    _, (sent_0,), _, started = _gather_step("gather_step_0", direct=[[first[n] for n in GATHER_ORDER[0]]])
    rest = cast([n for group in GATHER_ORDER[1:3] for n in group], after=started)
    _, (sent_1,), (hop_2,), started = _gather_step(
        "gather_step_1", direct=[[rest[n] for n in GATHER_ORDER[1]]], ring=[[rest[n] for n in GATHER_ORDER[2]]])
    h1 = _rmsnorm("norm_mix", xs, g_mix, low, after=started)
    mn = _rmsnorm("norm_mem", mems, g_mem, low)
    rest.update(cast(GATHER_ORDER[3], after=h1))
    rest.update(cast(GATHER_ORDER[4], after=rest[GATHER_ORDER[3][-1]]))
    slabs = [[rest[n] for n in group] for group in GATHER_ORDER[1:]]
    got = land(0, sent_0, rest["w_down"], relayed=False)
    wg_in = got["w_in"]
    pw = got["pool_w"].reshape(N_CHIPS, G, PG // N_CHIPS, PG).transpose(1, 0, 2, 3).reshape(G, PG, PG)
    proj = _matmul_nn_cols("proj_in", h1, wg_in, jnp.float32)
    z = _mixer_fwd(proj, pw, pool_scale, sgu_norm_g, ws, b_full)
    full = {"w_out": rows(land(1, sent_1, z, relayed=False)["w_out"])}
    (relay_2,), _, (hop_3,), started = _gather_step("gather_step_2", arrived=[hop_2], ring=slabs[2:3], after=full["w_out"])
    x1, h2 = _proj_residual_norm("mix_out", z, full["w_out"], xs, g_xattn, after=started)
    full.update({n: rows(a) for n, a in land(2, relay_2, h2).items()})
    q = _matmul_nn("xattn_q", h2, full["w_q"], low)
    k = _matmul_nn("xattn_k", mn, full["w_k"], low)
    vv = _matmul_nn("xattn_v", mn, full["w_v"], low)
    o = _attn_fwd(q, k, vv)
    (relay_3,), (sent_4,), _, started = _gather_step("gather_step_3", arrived=[hop_3], direct=slabs[3:4], after=o)
    x2, h3 = _proj_residual_norm("xattn_out", o, full["w_o"], x1, g_ffn, after=started)
    got = land(3, relay_3, h3)
    wg_gate, wg_up = got["w_gate"], got["w_up"]
    act, act_by_gate, act_by_up = _ffn_gate_up(h3, wg_gate, wg_up)
    full["w_down"] = rows(land(4, sent_4, act, relayed=False)["w_down"])
    x3 = _ffn_down(act, full["w_down"], x2)
    dx3, dx3_low, loss_part, d_final_g = _final_norm_loss(x3, target, g_final)

    dgate, dup = _ffn_down_bwd(dx3_low, full["w_down"], act_by_gate, act_by_up, N_CHIPS)
    (d_w_down,) = _matmul_tn("grad_w_down", act, [dx3_low], WIRE_DTYPE)
    (d_w_gate,) = _matmul_tn_cols("grad_w_gate", h3, [dgate], N_CHIPS, WIRE_DTYPE, tka=1024)
    (d_w_up,) = _matmul_tn_cols("grad_w_up", h3, [dup], N_CHIPS, WIRE_DTYPE, tka=1024)
    ffn_names = ("w_gate", "w_up", "w_down")
    ffn_swap = _swap_start("swap_start_ffn", _as_shards([d_w_gate, d_w_up, d_w_down]))
    dh3 = _matmul_nt_cols("ffn_dh", [dgate, dup], [wg_gate, wg_up], jnp.float32, tn=1024, after=ffn_swap[-1])
    ffn_flight = _reduce_scatter_start("ffn", ffn_names, *_swap_wait("swap_wait_ffn", ffn_swap, dh3), pos)
    dx2, dx2_low, d_ffn_g = _rmsnorm_bwd("norm_ffn_bwd", dh3, x2, g_ffn, dx3, low)

    do = _matmul_nt("xattn_do", [dx2_low], [full["w_o"]], low, after=ffn_flight[-1])
    (d_w_o,) = _matmul_tn("grad_w_o", o, [dx2_low], WIRE_DTYPE)
    dq, dk, dvv = _attn_bwd(q, k, vv, do)
    (d_w_q,) = _matmul_tn("grad_w_q", h2, [dq], WIRE_DTYPE)
    dx1, dx1_low, d_xattn_g = _matmul_nt_norm_bwd("xattn_dh_norm_bwd", dq, full["w_q"], x1, g_xattn, dx2, low)
    d_w_k, d_w_v = _matmul_tn("grad_w_kv", mn, [dk, dvv], WIRE_DTYPE)
    dmn = _matmul_nt("xattn_dmem", [dk, dvv], [full["w_k"], full["w_v"]], jnp.float32)
    (d_mem_g,) = _rmsnorm_bwd("norm_mem_bwd", dmn, mems, g_mem, None, None)

    (d_w_out,) = _matmul_tn("grad_w_out", z, [dx1_low], WIRE_DTYPE)
    attn_names = ("w_q", "w_k", "w_v", "w_o", "w_out")
    attn_swap = _swap_start("swap_start_attn", _as_shards([d_w_q, d_w_k, d_w_v, d_w_o, d_w_out]))
    dz = _matmul_nt("mix_dz", [dx1_low], [full["w_out"]], jnp.float32, after=attn_swap[-1])
    attn_flight = _reduce_scatter_start("attn", attn_names, *_swap_wait("swap_wait_attn", attn_swap, dz), pos)
    dproj, d_pw, d_ps, d_gs, d_ws, d_b = _mixer_bwd(proj, dz, pw, pool_scale, sgu_norm_g, ws, b_full)
    (d_w_in,) = _matmul_tn_cols("grad_w_in", h1, [dproj], N_CHIPS, WIRE_DTYPE, tka=1024, after=attn_flight[-1])
    d_pool_w = d_pw.reshape(G, N_CHIPS, PG // N_CHIPS, PG).transpose(1, 0, 2, 3).reshape(N_CHIPS, PG, PG).astype(WIRE_DTYPE)
    mix_names = ("w_in", "pool_w")
    mix_flight = _reduce_begin("mix", mix_names, [d_w_in, d_pool_w], pos)
    grad_x, d_mix_g = _matmul_nt_norm_bwd("mix_dh_norm_bwd", dproj, wg_in, xs, g_mix, dx1, None, tm=512, after=mix_flight[-1])

    small_parts = {
        "norm_mix_g": d_mix_g, "pool_scale": d_ps, "sgu_norm_g": d_gs, "w_spatial": d_ws, "b_spatial": d_b,
        "norm_xattn_g": d_xattn_g, "norm_mem_g": d_mem_g, "norm_ffn_g": d_ffn_g, "final_norm_g": d_final_g,
    }
    row_counts = [small_parts[n].size // LANES for n in SMALL]
    loss_rows = jnp.pad(loss_part, ((0, 7), (0, LANES - 1)))
    packed = jnp.concatenate([_as_lanes(small_parts[n]) for n in SMALL] + [loss_rows], axis=0)
    device = _device_index(lax.axis_index("x"), lax.axis_index("y"), lax.axis_index("c"))
    slots = lax.dynamic_update_slice(jnp.zeros((N_DEV,) + packed.shape, jnp.float32), packed[None], (device, 0, 0))
    small_flight = _exchange_start("small_grads_start", slots)

    groups = (("ffn", ffn_names, ffn_flight), ("attn", attn_names, attn_flight), ("mix", mix_names, mix_flight))
    joins, after = {}, small_flight[-1]
    for tag, names, flight in groups:
        reduced = _reduce_end(tag, names, flight, after, pos)
        joins[tag] = _join_start("join_start_" + tag, [reduced[n] for n in names])
        after = joins[tag][-1]
    small_sum = _sum_slots("small_grads_sum", _exchange_wait("small_grads_wait", small_flight, after))

    out_grad, out_delta, out_m, out_v = {}, {}, {}, {}
    after = small_sum
    for tag, names, _ in groups:
        grads = [g.reshape(-1, g.shape[-1]) for g in _join_wait("join_wait_" + tag, joins[tag], after)]
        updates = _per_run(_adamw, names, [_as2d(weights[n]) for n in names], grads,
                           [_as2d(moments_m[n]) for n in names], [_as2d(moments_v[n]) for n in names])
        for n, update in zip(names, updates):
            shape = weights[n].shape
            out_grad[n], out_delta[n], out_m[n], out_v[n] = (a.reshape(shape) for a in update)
        after = updates[-1][1]
    small_out = _adamw_small(
        small_sum, [_as_lanes(weights[n]) for n in SMALL], [_as_lanes(moments_m[n]) for n in SMALL],
        [_as_lanes(moments_v[n]) for n in SMALL], row_counts)
    for t, n in enumerate(SMALL):
        shape = weights[n].shape
        out_grad[n], out_delta[n], out_m[n], out_v[n] = (a.reshape(shape) for a in small_out[4 * t:4 * t + 4])

    loss = small_sum[sum(row_counts), 0]
    return (loss, grad_x[None], *[out_grad[n] for n in WEIGHTS], *[out_delta[n] for n in WEIGHTS],
            *[out_m[n] for n in WEIGHTS], *[out_v[n] for n in WEIGHTS])
```

```python
import math

import jax
import jax.numpy as jnp
from jax import lax
from jax.experimental import pallas as pl
from jax.experimental.pallas import tpu as pltpu

MXU_DTYPE = jnp.bfloat16
WIRE_DTYPE = jnp.bfloat16

EPS = 1e-6
CHUNK = 64
POOL_WINDOWS = (2, 4, 8, 16)
POOL_HALO = 16
N_XATTN_HEADS = 4
ADAM_LR = 0.001
ADAM_B1 = 0.9
ADAM_B2 = 0.999
ADAM_EPS = 1e-08
ADAM_WD = 0.01
ADAM_STEP = 10

N_CHIPS = 4
N_DEV = 8
LANES = 128
VMEM_LIMIT_BYTES = 56 * 2 ** 20
COPY_CHUNK_BYTES = 512 * 2 ** 10
COPY_ROW_ALIGN = 16

MESH = pl.DeviceIdType.MESH
ANY = pl.BlockSpec(memory_space=pl.ANY)
VMEM_WHOLE = pl.BlockSpec(memory_space=pltpu.VMEM)
HBM_SPEC = pl.BlockSpec(memory_space=pltpu.HBM)
SEM_SPEC = pl.BlockSpec(memory_space=pltpu.SEMAPHORE)
SIDE_EFFECT = pltpu.SideEffectType.DATAFLOW_SIDE_EFFECTING

NN = (((1,), (0,)), ((), ()))
NT = (((1,), (1,)), ((), ()))
TN = (((0,), (0,)), ((), ()))


def _tile(n, target, mult):
    best = None
    for t in range(mult, min(n, target) + 1, mult):
        if n % t == 0:
            best = t
    return n if best is None else best


def _params(n_grid_axes):
    return pltpu.CompilerParams(dimension_semantics=("arbitrary",) * n_grid_axes, vmem_limit_bytes=VMEM_LIMIT_BYTES)


def _rsqrt_mean_sq(x):
    return lax.rsqrt(jnp.mean(x * x, axis=-1, keepdims=True) + EPS)


def _sum_all(x):
    return jnp.sum(jnp.sum(x, axis=1, keepdims=True), axis=0, keepdims=True)


def _sigmoid(x):
    return 0.5 * jnp.tanh(0.5 * x) + 0.5


def _store_epilogue(accs, ex_refs, out_refs, i, j):
    for a, o in zip(accs, out_refs):
        o[...] = a.astype(o.dtype)


def _fused_matmul(name, grid, operands, pairs, acc_shapes, extras, outs, epilogue):
    nk = grid[2]
    n_op, n_ex, n_out, n_acc = len(operands), len(extras), len(outs), len(acc_shapes)
    in_place = nk > 1 and epilogue is _store_epilogue and all(o[1] == jnp.float32 for o in outs)

    def body(*refs):
        op_refs = refs[:n_op]
        ex_refs = refs[n_op:n_op + n_ex]
        out_refs = refs[n_op + n_ex:n_op + n_ex + n_out]
        acc_refs = out_refs if in_place else refs[n_op + n_ex + n_out:]
        i, j, k = pl.program_id(0), pl.program_id(1), pl.program_id(2)
        loaded = {}

        def operand(n):
            if n not in loaded:
                loaded[n] = op_refs[n][...].astype(MXU_DTYPE)
            return loaded[n]

        def product(ia, ib, dims):
            return lax.dot_general(operand(ia), operand(ib), dims, preferred_element_type=jnp.float32)

        def products():
            parts = [None] * n_acc
            for ia, ib, iacc, dims in pairs:
                d = product(ia, ib, dims)
                parts[iacc] = d if parts[iacc] is None else parts[iacc] + d
            return parts

        if nk == 1:
            epilogue(products(), ex_refs, out_refs, i, j)
            return

        @pl.when(k == 0)
        def _():
            for a in range(n_acc):
                acc_refs[a][...] = jnp.zeros_like(acc_refs[a])

        for a, part in enumerate(products()):
            acc_refs[a][...] += part

        if not in_place:
            @pl.when(k == nk - 1)
            def _():
                epilogue([acc_refs[a][...] for a in range(n_acc)], ex_refs, out_refs, i, j)

    arrays = [o[0] for o in operands] + [e[0] for e in extras]
    in_specs = [pl.BlockSpec(o[1], o[2]) for o in operands]
    in_specs += [ANY if e[1] is None else pl.BlockSpec(e[1], e[2]) for e in extras]
    return pl.pallas_call(
        body,
        name=name,
        grid=grid,
        in_specs=in_specs,
        out_specs=[pl.BlockSpec(o[2], o[3]) for o in outs],
        out_shape=[jax.ShapeDtypeStruct(o[0], o[1]) for o in outs],
        scratch_shapes=[pltpu.VMEM(s, jnp.float32) for s in acc_shapes] if nk > 1 and not in_place else [],
        compiler_params=_params(3),
    )(*arrays)


def _matmul_nn(name, a, b, out_dtype, tm=2048, tn=1024, after=None):
    M, K = a.shape
    N = b.shape[1]
    tm, tn = _tile(M, tm, 16), _tile(N, tn, LANES)
    return _fused_matmul(
        name, (M // tm, N // tn, 1),
        [(a, (tm, K), lambda i, j, k: (i, 0)), (b, (K, tn), lambda i, j, k: (0, j))],
        [(0, 1, 0, NN)], [(tm, tn)], _order_after(after),
        [((M, N), out_dtype, (tm, tn), lambda i, j, k: (i, j))], _store_epilogue)[0]


def _matmul_nn_cols(name, a, b3, out_dtype, tm=2048):
    M, K = a.shape
    J, _, Nc = b3.shape
    tm = _tile(M, tm, 16)
    return _fused_matmul(
        name, (M // tm, J, 1),
        [(a, (tm, K), lambda i, j, k: (i, 0)), (b3, (None, K, Nc), lambda i, j, k: (j, 0, 0))],
        [(0, 1, 0, NN)], [(tm, Nc)], [],
        [((M, J * Nc), out_dtype, (tm, Nc), lambda i, j, k: (i, j))], _store_epilogue)[0]


def _order_after(after):
    return [] if after is None else [(after, None, None)]


def _matmul_nt(name, a_list, b_list, out_dtype, tm=2048, tn=1024, after=None):
    M, K = a_list[0].shape
    N = b_list[0].shape[0]
    tm, tn = _tile(M, tm, 16), _tile(N, tn, LANES)
    n = len(a_list)
    operands = [(a, (tm, K), lambda i, j, k: (i, 0)) for a in a_list]
    operands += [(b, (tn, K), lambda i, j, k: (j, 0)) for b in b_list]
    return _fused_matmul(
        name, (M // tm, N // tn, 1), operands,
        [(p, n + p, 0, NT) for p in range(n)], [(tm, tn)], _order_after(after),
        [((M, N), out_dtype, (tm, tn), lambda i, j, k: (i, j))], _store_epilogue)[0]


def _matmul_nt_cols(name, a_list, b3_list, out_dtype, tm=1024, tn=512, after=None):
    M = a_list[0].shape[0]
    J, N, Kc = b3_list[0].shape
    tm, tn = _tile(M, tm, 16), _tile(N, tn, LANES)
    n = len(a_list)
    operands = [(a, (tm, Kc), lambda i, j, k: (i, k)) for a in a_list]
    operands += [(b, (None, tn, Kc), lambda i, j, k: (k, j, 0)) for b in b3_list]
    return _fused_matmul(
        name, (M // tm, N // tn, J), operands,
        [(p, n + p, 0, NT) for p in range(n)], [(tm, tn)], _order_after(after),
        [((M, N), out_dtype, (tm, tn), lambda i, j, k: (i, j))], _store_epilogue)[0]


def _matmul_tn(name, a, b_list, out_dtype, tka=1024, tn=1024, tkm=4096):
    M, Ka = a.shape
    N = b_list[0].shape[1]
    tka, tn, tkm = _tile(Ka, tka, LANES), _tile(N, tn, LANES), _tile(M, tkm, 16)
    n = len(b_list)
    operands = [(a, (tkm, tka), lambda i, j, k: (k, i))]
    operands += [(b, (tkm, tn), lambda i, j, k: (k, j)) for b in b_list]
    return _fused_matmul(
        name, (Ka // tka, N // tn, M // tkm), operands,
        [(0, 1 + p, p, TN) for p in range(n)], [(tka, tn)] * n, [],
        [((Ka, N), out_dtype, (tka, tn), lambda i, j, k: (i, j))] * n, _store_epilogue)


def _matmul_tn_cols(name, a, b_list, n_shards, out_dtype, tka=512, tkm=4096, after=None):
    M, Ka = a.shape
    Nc = b_list[0].shape[1] // n_shards
    tka, tkm = _tile(Ka, tka, LANES), _tile(M, tkm, 16)
    n = len(b_list)
    operands = [(a, (tkm, tka), lambda i, j, k: (k, i))]
    operands += [(b, (tkm, Nc), lambda i, j, k: (k, j)) for b in b_list]
    return _fused_matmul(
        name, (Ka // tka, n_shards, M // tkm), operands,
        [(0, 1 + p, p, TN) for p in range(n)], [(tka, Nc)] * n, _order_after(after),
        [((n_shards, Ka, Nc), out_dtype, (None, tka, Nc), lambda i, j, k: (j, i, 0))] * n, _store_epilogue)


def _rmsnorm(name, x, g_row, out_dtype, tm=512, after=None):
    S, D = x.shape
    tm = _tile(S, tm, 16)

    def body(x_ref, g_ref, *rest):
        o_ref = rest[-1]
        xv = x_ref[...]
        o_ref[...] = (xv * _rsqrt_mean_sq(xv) * g_ref[...]).astype(o_ref.dtype)

    ordered = [] if after is None else [after]
    return pl.pallas_call(
        body, name=name, grid=(S // tm,),
        in_specs=[pl.BlockSpec((tm, D), lambda i: (i, 0)), pl.BlockSpec((1, D), lambda i: (0, 0))] + [ANY] * len(ordered),
        out_specs=pl.BlockSpec((tm, D), lambda i: (i, 0)),
        out_shape=jax.ShapeDtypeStruct((S, D), out_dtype),
        compiler_params=_params(1),
    )(x, g_row, *ordered)


def _rmsnorm_bwd_values(dh, xin, g):
    r = _rsqrt_mean_sq(xin)
    xhat = xin * r
    dhg = dh * g
    dx = r * (dhg - xhat * jnp.mean(dhg * xhat, axis=-1, keepdims=True))
    return dx, jnp.sum(dh * xhat, axis=0, keepdims=True)


def _rmsnorm_bwd(name, dh, xin, g_row, dres, low_dtype, tm=256):
    S, D = xin.shape
    tm = _tile(S, tm, 16)
    has_dx = dres is not None
    has_low = has_dx and low_dtype is not None

    def body(*refs):
        dh_ref, x_ref, g_ref = refs[:3]
        rest = refs[3:]
        dx, dg = _rmsnorm_bwd_values(dh_ref[...], x_ref[...], g_ref[...])
        if has_dx:
            dres_ref, rest = rest[0], rest[1:]
            dx = dx + dres_ref[...]
            rest[0][...] = dx
            if has_low:
                rest[1][...] = dx.astype(low_dtype)
        dg_ref = rest[-1]

        @pl.when(pl.program_id(0) == 0)
        def _():
            dg_ref[...] = jnp.zeros_like(dg_ref)

        dg_ref[...] += dg

    row = pl.BlockSpec((tm, D), lambda i: (i, 0))
    vec = pl.BlockSpec((1, D), lambda i: (0, 0))
    ins, in_specs = [dh, xin, g_row], [row, row, vec]
    out_shape, out_specs = [], []
    if has_dx:
        ins.append(dres)
        in_specs.append(row)
        out_shape.append(jax.ShapeDtypeStruct((S, D), jnp.float32))
        out_specs.append(row)
        if has_low:
            out_shape.append(jax.ShapeDtypeStruct((S, D), low_dtype))
            out_specs.append(row)
    out_shape.append(jax.ShapeDtypeStruct((1, D), jnp.float32))
    out_specs.append(vec)
    return pl.pallas_call(
        body, name=name, grid=(S // tm,), in_specs=in_specs, out_specs=out_specs, out_shape=out_shape,
        compiler_params=_params(1),
    )(*ins)


def _chunk_mask(blk):
    t = lax.broadcasted_iota(jnp.int32, (blk, blk), 0)
    s = lax.broadcasted_iota(jnp.int32, (blk, blk), 1)
    return (s // CHUNK) <= (t // CHUNK)


def _pool_inputs(a, halo, row0, tm, g, pg):
    cols = slice(g * pg, (g + 1) * pg)
    w = POOL_WINDOWS[g]
    s = jnp.concatenate([halo[:, cols], a[:, cols]], axis=0)
    span = 1
    while span < w:
        s = s + pltpu.roll(s, span, axis=0)
        span *= 2
    t = lax.broadcasted_iota(jnp.int32, (tm, 1), 0) + row0
    cnt = jnp.minimum(t + 1, w).astype(jnp.float32)
    return s[POOL_HALO:] / cnt - a[:, cols]


def _mixer_specs(S, tm, DP, DS):
    per_halo = tm // POOL_HALO
    n_halo = S // POOL_HALO
    a_spec = pl.BlockSpec((tm, DP), lambda i: (i, 0))
    prev_spec = pl.BlockSpec((POOL_HALO, DP), lambda i: (jnp.maximum(i * per_halo - 1, 0), 0))
    next_spec = pl.BlockSpec((POOL_HALO, DP), lambda i: (jnp.minimum((i + 1) * per_halo, n_halo - 1), 0))
    u_spec = pl.BlockSpec((tm, DS), lambda i: (i, 1))
    v_spec = pl.BlockSpec((tm, DS), lambda i: (i, 2))
    return a_spec, prev_spec, next_spec, u_spec, v_spec


def _whole(arr):
    nd = arr.ndim
    return pl.BlockSpec(arr.shape, lambda i: (0,) * nd)


def _mixer_fwd(proj, pw, ps_row, gs_row, ws, b_full, tm=512):
    S = proj.shape[0]
    G, PG, _ = pw.shape
    DP = G * PG
    H, BLK, _ = ws.shape
    DS = gs_row.shape[1]
    HD = DS // H
    assert DP == DS and proj.shape[1] == DP + 2 * DS
    tm = _tile(S, tm, BLK)

    def body(a_ref, prev_ref, u_ref, v_ref, pw_ref, ps_ref, gs_ref, ws_ref, b_ref, z_ref):
        i = pl.program_id(0)
        a = a_ref[...]
        halo = jnp.where(i > 0, prev_ref[...], 0.0)
        for g in range(G):
            cols = slice(g * PG, (g + 1) * PG)
            p = _pool_inputs(a, halo, i * tm, tm, g, PG)
            y = jnp.dot(p.astype(MXU_DTYPE), pw_ref[g], preferred_element_type=jnp.float32) * ps_ref[:, cols]
            z_ref[:, cols] = y.astype(z_ref.dtype)
        v = v_ref[...]
        vn = (v * _rsqrt_mean_sq(v) * gs_ref[...]).astype(MXU_DTYPE)
        u = u_ref[...]
        mask = _chunk_mask(BLK)
        for h in range(H):
            wm = jnp.where(mask, ws_ref[h], 0.0).astype(MXU_DTYPE)
            bias = b_ref[h]
            cols = slice(h * HD, (h + 1) * HD)
            for n in range(tm // BLK):
                rows = slice(n * BLK, (n + 1) * BLK)
                mixed = jnp.dot(wm, vn[rows, cols], preferred_element_type=jnp.float32) + bias
                z_ref[rows, DP + h * HD:DP + (h + 1) * HD] = (u[rows, cols] * mixed).astype(z_ref.dtype)

    a_spec, prev_spec, _, u_spec, v_spec = _mixer_specs(S, tm, DP, DS)
    return pl.pallas_call(
        body, name="mixer_fwd", grid=(S // tm,),
        in_specs=[a_spec, prev_spec, u_spec, v_spec, _whole(pw), _whole(ps_row), _whole(gs_row), _whole(ws), _whole(b_full)],
        out_specs=pl.BlockSpec((tm, DP + DS), lambda i: (i, 0)),
        out_shape=jax.ShapeDtypeStruct((S, DP + DS), MXU_DTYPE),
        compiler_params=_params(1),
    )(proj, proj, proj, proj, pw, ps_row, gs_row, ws, b_full)


def _mixer_bwd(proj, dz, pw, ps_row, gs_row, ws, b_full, tm=256):
    S = proj.shape[0]
    G, PG, _ = pw.shape
    DP = G * PG
    H, BLK, _ = ws.shape
    DS = gs_row.shape[1]
    HD = DS // H
    tm = _tile(S, tm, BLK)
    nb = S // tm
    ext = tm + POOL_HALO

    def body(a_ref, prev_ref, u_ref, v_ref, dzp_ref, dzp_next_ref, dzs_ref, pw_ref, ps_ref, gs_ref, ws_ref, b_ref,
             dproj_ref, dpw_ref, dps_ref, dgs_ref, dws_ref, db_ref, dvn_ref):
        i = pl.program_id(0)

        @pl.when(i == 0)
        def _():
            for r in (dpw_ref, dps_ref, dgs_ref, dws_ref, db_ref):
                r[...] = jnp.zeros_like(r)

        a = a_ref[...]
        halo = jnp.where(i > 0, prev_ref[...], 0.0)
        dyp = dzp_ref[...]
        dyp_next = jnp.where(i < nb - 1, dzp_next_ref[...], 0.0)
        dq_ext = jnp.concatenate([dyp, dyp_next], axis=0) * ps_ref[...]
        t_ext = lax.broadcasted_iota(jnp.int32, (ext, 1), 0) + i * tm
        for g in range(G):
            cols = slice(g * PG, (g + 1) * PG)
            w = POOL_WINDOWS[g]
            pb = _pool_inputs(a, halo, i * tm, tm, g, PG).astype(MXU_DTYPE)
            y_pre = jnp.dot(pb, pw_ref[g], preferred_element_type=jnp.float32)
            dps_ref[:, cols] += jnp.sum(dyp[:, cols] * y_pre, axis=0, keepdims=True)
            dqb = dq_ext[:, cols].astype(MXU_DTYPE)
            dpw_ref[g] += lax.dot_general(pb, dqb[:tm], TN, preferred_element_type=jnp.float32)
            dp = lax.dot_general(dqb, pw_ref[g], NT, preferred_element_type=jnp.float32)
            f = dp / jnp.minimum(t_ext + 1, w).astype(jnp.float32)
            span = 1
            while span < w:
                f = f + pltpu.roll(f, ext - span, axis=0)
                span *= 2
            dproj_ref[:, cols] = (f[:tm] - dp[:tm]).astype(dproj_ref.dtype)

        v = v_ref[...]
        rv = _rsqrt_mean_sq(v)
        vhat = v * rv
        gs = gs_ref[...]
        vnb = (vhat * gs).astype(MXU_DTYPE)
        u = u_ref[...]
        dys = dzs_ref[...]
        mask = _chunk_mask(BLK)
        for h in range(H):
            wm = jnp.where(mask, ws_ref[h], 0.0).astype(MXU_DTYPE)
            bias = b_ref[h]
            cols = slice(h * HD, (h + 1) * HD)
            dm_sum = jnp.zeros((BLK, HD), jnp.float32)
            dws_sum = jnp.zeros((BLK, BLK), jnp.float32)
            for n in range(tm // BLK):
                rows = slice(n * BLK, (n + 1) * BLK)
                vblk = vnb[rows, cols]
                mixed = jnp.dot(wm, vblk, preferred_element_type=jnp.float32) + bias
                dy = dys[rows, cols]
                dproj_ref[rows, DP + h * HD:DP + (h + 1) * HD] = (dy * mixed).astype(dproj_ref.dtype)
                dm = dy * u[rows, cols]
                dm_sum = dm_sum + dm
                dmb = dm.astype(MXU_DTYPE)
                dws_sum = dws_sum + lax.dot_general(dmb, vblk, NT, preferred_element_type=jnp.float32)
                dvn_ref[rows, cols] = lax.dot_general(wm, dmb, TN, preferred_element_type=jnp.float32)
            dws_ref[h] += jnp.where(mask, dws_sum, 0.0)
            db_ref[h:h + 1, :] += jnp.sum(dm_sum.T, axis=0, keepdims=True)
        dvn = dvn_ref[...]
        dvg = dvn * gs
        dv = rv * (dvg - vhat * jnp.mean(dvg * vhat, axis=-1, keepdims=True))
        dgs_ref[...] += jnp.sum(dvn * vhat, axis=0, keepdims=True)
        dproj_ref[:, DP + DS:] = dv.astype(dproj_ref.dtype)

    a_spec, prev_spec, next_spec, u_spec, v_spec = _mixer_specs(S, tm, DP, DS)
    dzp_spec = pl.BlockSpec((tm, DP), lambda i: (i, 0))
    dzs_spec = pl.BlockSpec((tm, DS), lambda i: (i, 1))
    small = [pw, ps_row, gs_row, ws, b_full]
    out_shape = [
        jax.ShapeDtypeStruct((S, DP + 2 * DS), MXU_DTYPE),
        jax.ShapeDtypeStruct(pw.shape, jnp.float32),
        jax.ShapeDtypeStruct(ps_row.shape, jnp.float32),
        jax.ShapeDtypeStruct(gs_row.shape, jnp.float32),
        jax.ShapeDtypeStruct(ws.shape, jnp.float32),
        jax.ShapeDtypeStruct((H, BLK), jnp.float32),
    ]
    out_specs = [pl.BlockSpec((tm, DP + 2 * DS), lambda i: (i, 0))] + [_whole(s) for s in out_shape[1:]]
    return pl.pallas_call(
        body, name="mixer_bwd", grid=(nb,),
        in_specs=[a_spec, prev_spec, u_spec, v_spec, dzp_spec, next_spec, dzs_spec] + [_whole(s) for s in small],
        out_specs=out_specs, out_shape=out_shape,
        scratch_shapes=[pltpu.VMEM((tm, DS), jnp.float32)],
        compiler_params=_params(1),
    )(proj, proj, proj, proj, dz, dz, dz, *small)


def _softmax_rows(q, k, scale):
    s = lax.dot_general(q, k, NT, preferred_element_type=jnp.float32) * scale
    e = jnp.exp(s - jnp.max(s, axis=-1, keepdims=True))
    return e / jnp.sum(e, axis=-1, keepdims=True)


def _attn_fwd(q, k, v, tm=512):
    S, D = q.shape
    M = k.shape[0]
    HD = D // N_XATTN_HEADS
    scale = HD ** -0.5
    tm = _tile(S, tm, 16)

    def body(q_ref, k_ref, v_ref, o_ref):
        for h in range(N_XATTN_HEADS):
            cols = slice(h * HD, (h + 1) * HD)
            p = _softmax_rows(q_ref[:, cols], k_ref[:, cols], scale)
            o = jnp.dot(p.astype(MXU_DTYPE), v_ref[:, cols], preferred_element_type=jnp.float32)
            o_ref[:, cols] = o.astype(o_ref.dtype)

    row = pl.BlockSpec((tm, D), lambda i: (i, 0))
    mem = pl.BlockSpec((M, D), lambda i: (0, 0))
    return pl.pallas_call(
        body, name="attn_fwd", grid=(S // tm,), in_specs=[row, mem, mem], out_specs=row,
        out_shape=jax.ShapeDtypeStruct((S, D), MXU_DTYPE), compiler_params=_params(1),
    )(q, k, v)


def _attn_bwd(q, k, v, do, tm=512):
    S, D = q.shape
    M = k.shape[0]
    HD = D // N_XATTN_HEADS
    scale = HD ** -0.5
    tm = _tile(S, tm, 16)

    def body(q_ref, k_ref, v_ref, do_ref, dq_ref, dk_ref, dv_ref):
        @pl.when(pl.program_id(0) == 0)
        def _():
            dk_ref[...] = jnp.zeros_like(dk_ref)
            dv_ref[...] = jnp.zeros_like(dv_ref)

        for h in range(N_XATTN_HEADS):
            cols = slice(h * HD, (h + 1) * HD)
            qh, kh, vh, doh = q_ref[:, cols], k_ref[:, cols], v_ref[:, cols], do_ref[:, cols]
            p = _softmax_rows(qh, kh, scale)
            dp = lax.dot_general(doh, vh, NT, preferred_element_type=jnp.float32)
            dv_ref[:, cols] += lax.dot_general(p.astype(MXU_DTYPE), doh, TN, preferred_element_type=jnp.float32)
            ds = (p * (dp - jnp.sum(dp * p, axis=-1, keepdims=True)) * scale).astype(MXU_DTYPE)
            dq_ref[:, cols] = jnp.dot(ds, kh, preferred_element_type=jnp.float32).astype(dq_ref.dtype)
            dk_ref[:, cols] += lax.dot_general(ds, qh, TN, preferred_element_type=jnp.float32)

    row = pl.BlockSpec((tm, D), lambda i: (i, 0))
    mem = pl.BlockSpec((M, D), lambda i: (0, 0))
    return pl.pallas_call(
        body, name="attn_bwd", grid=(S // tm,), in_specs=[row, mem, mem, row], out_specs=[row, mem, mem],
        out_shape=[jax.ShapeDtypeStruct((S, D), MXU_DTYPE), jax.ShapeDtypeStruct((M, D), jnp.float32),
                   jax.ShapeDtypeStruct((M, D), jnp.float32)],
        compiler_params=_params(1),
    )(q, k, v, do)


def _proj_residual_norm(name, a, w2d, resid, g_row, tm=512, after=None):
    M, K = a.shape
    N = w2d.shape[1]
    tm = _tile(M, tm, 16)

    def epilogue(accs, ex_refs, out_refs, i, j):
        x_new = ex_refs[0][...] + accs[0]
        out_refs[0][...] = x_new
        out_refs[1][...] = (x_new * _rsqrt_mean_sq(x_new) * ex_refs[1][...]).astype(MXU_DTYPE)

    row = lambda i, j, k: (i, 0)
    return _fused_matmul(
        name, (M // tm, 1, 1),
        [(a, (tm, K), row), (w2d, (K, N), lambda i, j, k: (0, 0))],
        [(0, 1, 0, NN)], [(tm, N)],
        [(resid, (tm, N), row), (g_row, (1, N), lambda i, j, k: (0, 0))] + _order_after(after),
        [((M, N), jnp.float32, (tm, N), row), ((M, N), MXU_DTYPE, (tm, N), row)], epilogue)


def _ffn_gate_up(h, wg3, wu3, tm=512):
    M, K = h.shape
    J, _, Nc = wg3.shape
    tm = _tile(M, tm, 16)

    def epilogue(accs, ex_refs, out_refs, i, j):
        gate, up = accs
        sg = _sigmoid(gate)
        silu = gate * sg
        out_refs[0][...] = (silu * up).astype(MXU_DTYPE)
        out_refs[1][...] = (up * (sg * (1.0 + gate * (1.0 - sg)))).astype(MXU_DTYPE)
        out_refs[2][...] = silu.astype(MXU_DTYPE)

    col = lambda i, j, k: (i, j)
    wspec = lambda i, j, k: (j, 0, 0)
    return _fused_matmul(
        "ffn_gate_up", (M // tm, J, 1),
        [(h, (tm, K), lambda i, j, k: (i, 0)), (wg3, (None, K, Nc), wspec), (wu3, (None, K, Nc), wspec)],
        [(0, 1, 0, NN), (0, 2, 1, NN)], [(tm, Nc)] * 2, [],
        [((M, J * Nc), MXU_DTYPE, (tm, Nc), col)] * 3, epilogue)


def _ffn_down(act, wd2d, x2, tm=512, tn=1024):
    M, F = act.shape
    D = wd2d.shape[1]
    tm, tn = _tile(M, tm, 16), _tile(D, tn, LANES)

    def epilogue(accs, ex_refs, out_refs, i, j):
        out_refs[0][...] = ex_refs[0][...] + accs[0]

    tile = lambda i, j, k: (i, j)
    return _fused_matmul(
        "ffn_down", (M // tm, D // tn, 1),
        [(act, (tm, F), lambda i, j, k: (i, 0)), (wd2d, (F, tn), lambda i, j, k: (0, j))],
        [(0, 1, 0, NN)], [(tm, tn)], [(x2, (tm, tn), tile)],
        [((M, D), jnp.float32, (tm, tn), tile)], epilogue)[0]


def _final_norm_loss(x3, target, g_row, tm=256):
    S, D = x3.shape
    tm = _tile(S, tm, 16)

    def body(x_ref, t_ref, g_ref, dx_ref, dxl_ref, loss_ref, dg_ref):
        x3v = x_ref[...]
        g = g_ref[...]
        r = _rsqrt_mean_sq(x3v)
        xhat = x3v * r
        diff = xhat * g - t_ref[...]
        dy = diff / D
        dyg = dy * g
        dx = r * (dyg - xhat * jnp.mean(dyg * xhat, axis=-1, keepdims=True))
        dx_ref[...] = dx
        dxl_ref[...] = dx.astype(MXU_DTYPE)

        @pl.when(pl.program_id(0) == 0)
        def _():
            loss_ref[...] = jnp.zeros_like(loss_ref)
            dg_ref[...] = jnp.zeros_like(dg_ref)

        loss_ref[...] += 0.5 * _sum_all(jnp.mean(diff * diff, axis=-1, keepdims=True))
        dg_ref[...] += jnp.sum(dy * xhat, axis=0, keepdims=True)

    row = pl.BlockSpec((tm, D), lambda i: (i, 0))
    vec = pl.BlockSpec((1, D), lambda i: (0, 0))
    return pl.pallas_call(
        body, name="final_norm_loss", grid=(S // tm,), in_specs=[row, row, vec],
        out_specs=[row, row, pl.BlockSpec((1, 1), lambda i: (0, 0)), vec],
        out_shape=[jax.ShapeDtypeStruct((S, D), jnp.float32), jax.ShapeDtypeStruct((S, D), MXU_DTYPE),
                   jax.ShapeDtypeStruct((1, 1), jnp.float32), jax.ShapeDtypeStruct((1, D), jnp.float32)],
        compiler_params=_params(1),
    )(x3, target, g_row)


def _ffn_down_bwd(dx3_low, wd2d, act_by_gate, act_by_up, n_shards, tm=1024):
    M, D = dx3_low.shape
    F = wd2d.shape[0]
    tn = F // n_shards
    tm = _tile(M, tm, 16)

    def epilogue(accs, ex_refs, out_refs, i, j):
        dact = accs[0]
        out_refs[0][...] = (dact * ex_refs[0][...].astype(jnp.float32)).astype(MXU_DTYPE)
        out_refs[1][...] = (dact * ex_refs[1][...].astype(jnp.float32)).astype(MXU_DTYPE)

    col = lambda i, j, k: (i, j)
    return _fused_matmul(
        "ffn_down_bwd", (M // tm, n_shards, 1),
        [(dx3_low, (tm, D), lambda i, j, k: (i, 0)), (wd2d, (tn, D), lambda i, j, k: (j, 0))],
        [(0, 1, 0, NT)], [(tm, tn)],
        [(act_by_gate, (tm, tn), col), (act_by_up, (tm, tn), col)],
        [((M, F), MXU_DTYPE, (tm, tn), col)] * 2, epilogue)


def _matmul_nt_norm_bwd(name, a, b, xin, g_row, dres, low_dtype, tm=256, after=None):
    M = a.shape[0]
    sharded = b.ndim == 3
    N, Kc = b.shape[-2:]
    J = b.shape[0] if sharded else 1
    tm = _tile(M, tm, 16)
    has_low = low_dtype is not None

    def epilogue(accs, ex_refs, out_refs, i, j):
        dx, dg = _rmsnorm_bwd_values(accs[0], ex_refs[0][...], ex_refs[1][...])
        dx = dx + ex_refs[2][...]
        out_refs[0][...] = dx
        if has_low:
            out_refs[1][...] = dx.astype(low_dtype)
        dg_ref = out_refs[-1]

        @pl.when(i == 0)
        def _():
            dg_ref[...] = jnp.zeros_like(dg_ref)

        dg_ref[...] += dg

    row = lambda i, j, k: (i, 0)
    const = lambda i, j, k: (0, 0)
    b_operand = (b, (None, N, Kc), lambda i, j, k: (k, 0, 0)) if sharded else (b, (N, Kc), const)
    outs = [((M, N), jnp.float32, (tm, N), row)]
    if has_low:
        outs.append(((M, N), low_dtype, (tm, N), row))
    outs.append(((1, N), jnp.float32, (1, N), const))
    return _fused_matmul(
        name, (M // tm, 1, J), [(a, (tm, Kc), lambda i, j, k: (i, k)), b_operand],
        [(0, 1, 0, NT)], [(tm, N)],
        [(xin, (tm, N), row), (g_row, (1, N), const), (dres, (tm, N), row)] + _order_after(after),
        outs, epilogue)


def _row_tile(R, C, target_bytes=2 ** 21):
    return _tile(R, max(16, target_bytes // (4 * C)), 16)


def _prefetch_call(body, name, grid, pos, in_specs, out_specs, out_shape, ins, after=None):
    ordered = [] if after is None else [after]
    grid_spec = pltpu.PrefetchScalarGridSpec(
        num_scalar_prefetch=1, grid=grid, in_specs=list(in_specs) + [ANY] * len(ordered), out_specs=out_specs)
    return pl.pallas_call(body, name=name, grid_spec=grid_spec, out_shape=out_shape,
                          compiler_params=_params(len(grid)))(pos, *ins, *ordered)


def _runs(arrays):
    runs = []
    for i, a in enumerate(arrays):
        if runs and arrays[runs[-1][-1]].shape == a.shape:
            runs[-1].append(i)
        else:
            runs.append([i])
    return runs


def _per_run(fn, names, *array_lists):
    out = [None] * len(names)
    for run in _runs(array_lists[0]):
        results = fn("_".join(names[i] for i in run), *[[arrays[i] for i in run] for arrays in array_lists])
        for i, r in zip(run, results):
            out[i] = r
    return out


def _cast_into_slabs(name, xs, dtype, pos, after=None):
    n = len(xs)
    R, C = xs[0].shape
    tr = _row_tile(R, C * n, 2 ** 22)

    def body(pos_ref, *refs):
        for t in range(n):
            refs[len(refs) - n + t][...] = refs[t][...].astype(dtype)

    return _prefetch_call(
        body, "cast_" + name, (R // tr,), pos, [pl.BlockSpec((tr, C), lambda i, pos: (i, 0))] * n,
        [pl.BlockSpec((None, tr, C), lambda i, pos: (pos[0], i, 0))] * n,
        [jax.ShapeDtypeStruct((N_CHIPS, R, C), dtype)] * n, xs, after)


def _chip_sum(name, partials, gots, dtype, pos):
    n = len(partials)
    J, R, C = partials[0].shape
    Rh = R // 2
    tr = _row_tile(Rh, C * n, 2 ** 22)

    def body(pos_ref, *refs):
        for t in range(n):
            refs[2 * n + t][...] = (refs[t][...].astype(jnp.float32) + refs[n + t][...].astype(jnp.float32)).astype(dtype)

    blk = pl.BlockSpec((None, tr, C), lambda j, i, pos: (j, i, 0))
    return _prefetch_call(
        body, "chip_sum_" + name, (J, Rh // tr), pos,
        [pl.BlockSpec((None, None, tr, C), lambda j, i, pos: (j, pos[1], i, 0))] * n + [blk] * n, [blk] * n,
        [jax.ShapeDtypeStruct((J, Rh, C), dtype)] * n, [p.reshape(J, 2, Rh, C) for p in partials] + list(gots))


def _owner_sum(name, chip_sums, others, pos):
    n = len(chip_sums)
    J, Rh, C = chip_sums[0].shape
    n_other = others[0].shape[0]
    tr = _row_tile(Rh, C * (n_other + 1) * n, 2 ** 22)

    def body(pos_ref, *refs):
        for t in range(n):
            acc = refs[t][...].astype(jnp.float32)
            for k in range(n_other):
                acc = acc + refs[n + t][k].astype(jnp.float32)
            refs[2 * n + t][...] = acc

    return _prefetch_call(
        body, "owner_sum_" + name, (Rh // tr,), pos,
        [pl.BlockSpec((None, tr, C), lambda i, pos: (pos[0], i, 0))] * n
        + [pl.BlockSpec((n_other, tr, C), lambda i, pos: (0, i, 0))] * n,
        [pl.BlockSpec((None, tr, C), lambda i, pos: (pos[1], i, 0))] * n,
        [jax.ShapeDtypeStruct((2, Rh, C), jnp.float32)] * n, list(chip_sums) + list(others))


def _adamw_values(w, g, m, v):
    m = ADAM_B1 * m + (1.0 - ADAM_B1) * g
    v = ADAM_B2 * v + (1.0 - ADAM_B2) * (g * g)
    m_hat = m / (1.0 - ADAM_B1 ** ADAM_STEP)
    v_hat = v / (1.0 - ADAM_B2 ** ADAM_STEP)
    delta = -ADAM_LR * (m_hat / (jnp.sqrt(v_hat) + ADAM_EPS) + ADAM_WD * w)
    return delta, m, v


def _adamw(name, ws, gs, ms, vs):
    n = len(ws)
    R, C = ws[0].shape
    tr = _row_tile(R, C * n, 2 ** 21)

    def body(*refs):
        w_refs, g_refs, m_refs, v_refs = (refs[q * n:(q + 1) * n] for q in range(4))
        outs = refs[4 * n:]
        for t in range(n):
            g = g_refs[t][...]
            outs[4 * t][...] = g
            outs[4 * t + 1][...], outs[4 * t + 2][...], outs[4 * t + 3][...] = _adamw_values(
                w_refs[t][...], g, m_refs[t][...], v_refs[t][...])

    blk = pl.BlockSpec((tr, C), lambda i: (i, 0))
    res = pl.pallas_call(
        body, name="adamw_" + name, grid=(R // tr,), in_specs=[blk] * (4 * n), out_specs=[blk] * (4 * n),
        out_shape=[jax.ShapeDtypeStruct((R, C), jnp.float32)] * (4 * n), compiler_params=_params(1),
    )(*ws, *gs, *ms, *vs)
    return [tuple(res[4 * t:4 * t + 4]) for t in range(n)]


def _adamw_small(g_packed, ws, ms, vs, row_counts):
    n = len(ws)

    def body(*refs):
        g_ref = refs[0]
        w_refs, m_refs, v_refs = refs[1:1 + n], refs[1 + n:1 + 2 * n], refs[1 + 2 * n:1 + 3 * n]
        outs = refs[1 + 3 * n:]
        off = 0
        for t in range(n):
            g = g_ref[off:off + row_counts[t], :]
            off += row_counts[t]
            delta, m_new, v_new = _adamw_values(w_refs[t][...], g, m_refs[t][...], v_refs[t][...])
            outs[4 * t][...] = g
            outs[4 * t + 1][...] = delta
            outs[4 * t + 2][...] = m_new
            outs[4 * t + 3][...] = v_new

    out_shape = []
    for t in range(n):
        out_shape += [jax.ShapeDtypeStruct((row_counts[t], LANES), jnp.float32)] * 4
    return pl.pallas_call(
        body, name="adamw_small", in_specs=[VMEM_WHOLE] * (1 + 3 * n), out_specs=[VMEM_WHOLE] * (4 * n),
        out_shape=out_shape, compiler_params=pltpu.CompilerParams(vmem_limit_bytes=VMEM_LIMIT_BYTES),
    )(g_packed, *ws, *ms, *vs)


def _position():
    x, y, c = lax.axis_index("x"), lax.axis_index("y"), lax.axis_index("c")
    other_chips = [(1 - x, y), (x, 1 - y), (1 - x, 1 - y)]
    return x, y, c, other_chips


def _half(rows, which):
    return pl.ds(which * (rows // 2), rows // 2)


def _chunk_index(shape, dtype):
    rows = shape[-2]
    total = math.prod(shape) * jnp.dtype(dtype).itemsize
    n = max(1, min(rows // COPY_ROW_ALIGN, total // COPY_CHUNK_BYTES))
    step = -(-rows // n)
    step = -(-step // COPY_ROW_ALIGN) * COPY_ROW_ALIGN
    lead = (slice(None),) * (len(shape) - 2)
    return [lead + (pl.ds(s, min(step, rows - s)),) for s in range(0, rows, step)]


def _start_in_chunks(make, src, dst, waited_here=True):
    for idx in _chunk_index(src.shape, src.dtype):
        make(src.at[idx], dst.at[idx]).start()
    return make(src, dst) if waited_here else None


def _remote_copier(send_sem, recv_sem, to):
    return lambda src, dst: pltpu.make_async_remote_copy(
        src_ref=src, dst_ref=dst, send_sem=send_sem, recv_sem=recv_sem, device_id=to, device_id_type=MESH)


def _slab_half(ref, chip, which):
    return ref.at[chip, _half(ref.shape[1], which)]


def _slab_quarter(ref, chip, which, q):
    rows = ref.shape[1] // 4
    return ref.at[chip, pl.ds(which * 2 * rows + q * rows, rows)]


def _gather_step(name, arrived=(), ring=(), direct=(), after=None):
    groups = [g[2] for g in arrived] + list(direct) + list(ring)
    per_tensor = [2] * len(arrived) + [3] * len(direct) + [2] * len(ring)
    flat = [a for g in groups for a in g]
    T, G, A = len(flat), len(groups), len(arrived)
    ordered = [] if after is None else [after]
    sems_in = [sem for g in arrived for sem in g[:2]]

    def body(*refs):
        ins = refs[:T]
        first_hop = refs[T:T + 2 * A]
        out_sems = refs[T + 2 * A + len(ordered):T + 2 * A + len(ordered) + 2 * G]
        token = refs[-1]
        x, y, c, other_chips = _position()
        mine = 2 * x + y
        x_nbr, y_nbr = (1 - x, y, c), (x, 1 - y, c)
        from_x, from_y = 2 * (1 - x) + y, 2 * x + (1 - y)
        t0 = 0
        for gi, group in enumerate(groups):
            send, recv = out_sems[2 * gi], out_sems[2 * gi + 1]
            for t in range(len(group)):
                ref = ins[t0 + t]
                own = _slab_half(ref, mine, c)
                if gi < A:
                    send1, recv1 = first_hop[2 * gi], first_hop[2 * gi + 1]
                    for k, (peer, chip) in enumerate(((x_nbr, from_x), (y_nbr, from_y))):
                        cp = _remote_copier(send1.at[2 * t + k], recv1.at[2 * t + k], peer)(own, _slab_half(ref, chip, c))
                        cp.wait_send()
                        cp.wait_recv()
                    for k, (peer, chip) in enumerate(((y_nbr, from_x), (x_nbr, from_y))):
                        piece = _slab_quarter(ref, chip, c, k)
                        _start_in_chunks(_remote_copier(send.at[2 * t + k], recv.at[2 * t + k], peer), piece, piece,
                                         waited_here=False)
                elif gi < A + len(direct):
                    for k, (cx, cy) in enumerate(other_chips):
                        _start_in_chunks(_remote_copier(send.at[3 * t + k], recv.at[3 * t + k], (cx, cy, c)), own, own,
                                         waited_here=False)
                else:
                    for k, peer in enumerate((x_nbr, y_nbr)):
                        _start_in_chunks(_remote_copier(send.at[2 * t + k], recv.at[2 * t + k], peer), own, own,
                                         waited_here=False)
            t0 += len(group)
        token[...] = jnp.zeros_like(token)

    sem_shapes = []
    for group, n in zip(groups, per_tensor):
        sem_shapes += [pltpu.SemaphoreType.DMA((n * len(group),))] * 2
    res = pl.pallas_call(
        body, name=name, in_specs=[HBM_SPEC] * T + [SEM_SPEC] * (2 * A) + [ANY] * len(ordered),
        out_specs=[SEM_SPEC] * (2 * G) + [HBM_SPEC] * T + [VMEM_WHOLE],
        out_shape=sem_shapes + [pltpu.HBM(a.shape, a.dtype) for a in flat] + [jax.ShapeDtypeStruct((8, LANES), jnp.float32)],
        input_output_aliases={t: 2 * G + t for t in range(T)},
        compiler_params=pltpu.CompilerParams(has_side_effects=SIDE_EFFECT),
    )(*[pltpu.with_memory_space_constraint(a, pltpu.HBM) for a in flat], *sems_in, *ordered)
    flights, t0 = [], 2 * G
    for gi, group in enumerate(groups):
        flights.append((res[2 * gi], res[2 * gi + 1], res[t0:t0 + len(group)]))
        t0 += len(group)
    return flights[:A], flights[A:A + len(direct)], flights[A + len(direct):], res[-1]


def _relayed_wait(name, flight, after):
    send_sems, recv_sems, slabs = flight
    T = len(slabs)

    def body(*refs):
        ins = refs[:T]
        send, recv = refs[T:T + 2]
        x, y, c, _ = _position()
        x_nbr, y_nbr = (1 - x, y, c), (x, 1 - y, c)
        from_x, from_y, diagonal = 2 * (1 - x) + y, 2 * x + (1 - y), 2 * (1 - x) + (1 - y)
        for t in range(T):
            for k, (peer, chip) in enumerate(((y_nbr, from_x), (x_nbr, from_y))):
                cp = _remote_copier(send.at[2 * t + k], recv.at[2 * t + k], peer)(
                    _slab_quarter(ins[t], chip, c, k), _slab_quarter(ins[t], diagonal, c, k))
                cp.wait_send()
                cp.wait_recv()

    return pl.pallas_call(
        body, name=name, in_specs=[HBM_SPEC] * T + [SEM_SPEC, SEM_SPEC, ANY], out_specs=[HBM_SPEC] * T,
        out_shape=[pltpu.HBM(a.shape, a.dtype) for a in slabs], input_output_aliases={t: t for t in range(T)},
        compiler_params=pltpu.CompilerParams(has_side_effects=SIDE_EFFECT),
    )(*slabs, send_sems, recv_sems, after)


def _gather_wait(name, flight, after):
    send_sems, recv_sems, slabs = flight
    T = len(slabs)

    def body(*refs):
        ins = refs[:T]
        send, recv = refs[T:T + 2]
        x, y, c, other_chips = _position()
        mine = 2 * x + y
        for t in range(T):
            own = _slab_half(ins[t], mine, c)
            for k, (cx, cy) in enumerate(other_chips):
                landed = _slab_half(ins[t], 2 * cx + cy, c)
                cp = _remote_copier(send.at[3 * t + k], recv.at[3 * t + k], (cx, cy, c))(own, landed)
                cp.wait_send()
                cp.wait_recv()

    return pl.pallas_call(
        body, name=name, in_specs=[HBM_SPEC] * T + [SEM_SPEC, SEM_SPEC, ANY], out_specs=[HBM_SPEC] * T,
        out_shape=[pltpu.HBM(s.shape, s.dtype) for s in slabs], input_output_aliases={t: t for t in range(T)},
        compiler_params=pltpu.CompilerParams(has_side_effects=SIDE_EFFECT),
    )(*slabs, send_sems, recv_sems, after)


def _share_with_sibling(name, slabs):
    T = len(slabs)

    def body(*refs):
        outs = refs[T:2 * T]
        send_sems, recv_sems = refs[2 * T:]
        x, y, c, other_chips = _position()
        sibling = (x, y, 1 - c)
        started = []
        for t in range(T):
            for k, (cx, cy) in enumerate(other_chips):
                landed = _slab_half(outs[t], 2 * cx + cy, c)
                started.append(_start_in_chunks(_remote_copier(send_sems.at[t, k], recv_sems.at[t, k], sibling), landed, landed))
        for t in range(T):
            for k, (cx, cy) in enumerate(other_chips):
                arriving = _slab_half(outs[t], 2 * cx + cy, 1 - c)
                _remote_copier(send_sems.at[t, k], recv_sems.at[t, k], sibling)(arriving, arriving).wait_recv()
        for cp in started:
            cp.wait_send()

    return pl.pallas_call(
        body, name=name, in_specs=[ANY] * T, out_specs=[ANY] * T,
        out_shape=[jax.ShapeDtypeStruct(s.shape, s.dtype) for s in slabs],
        input_output_aliases={t: t for t in range(T)},
        scratch_shapes=[pltpu.SemaphoreType.DMA((T, 3)), pltpu.SemaphoreType.DMA((T, 3))],
    )(*slabs)


def _share_start(name, slabs):
    T = len(slabs)

    def body(*refs):
        ins = refs[:T]
        send_sems, recv_sems = refs[T:T + 2]
        token = refs[-1]
        x, y, c, other_chips = _position()
        for t in range(T):
            for k, (cx, cy) in enumerate(other_chips):
                landed = _slab_half(ins[t], 2 * cx + cy, c)
                _start_in_chunks(_remote_copier(send_sems.at[3 * t + k], recv_sems.at[3 * t + k], (x, y, 1 - c)),
                                 landed, landed, waited_here=False)
        token[...] = jnp.zeros_like(token)

    res = pl.pallas_call(
        body, name=name, in_specs=[HBM_SPEC] * T,
        out_specs=[SEM_SPEC, SEM_SPEC] + [HBM_SPEC] * T + [VMEM_WHOLE],
        out_shape=[pltpu.SemaphoreType.DMA((3 * T,)), pltpu.SemaphoreType.DMA((3 * T,))]
        + [pltpu.HBM(a.shape, a.dtype) for a in slabs] + [jax.ShapeDtypeStruct((8, LANES), jnp.float32)],
        input_output_aliases={t: 2 + t for t in range(T)},
        compiler_params=pltpu.CompilerParams(has_side_effects=SIDE_EFFECT),
    )(*[pltpu.with_memory_space_constraint(a, pltpu.HBM) for a in slabs])
    return res[0], res[1], res[2:2 + T], res[-1]


def _share_wait(name, flight, after):
    send_sems, recv_sems, slabs, _ = flight
    T = len(slabs)

    def body(*refs):
        ins = refs[:T]
        send, recv = refs[T:T + 2]
        x, y, c, other_chips = _position()
        for t in range(T):
            for k, (cx, cy) in enumerate(other_chips):
                cp = _remote_copier(send.at[3 * t + k], recv.at[3 * t + k], (x, y, 1 - c))(
                    _slab_half(ins[t], 2 * cx + cy, c), _slab_half(ins[t], 2 * cx + cy, 1 - c))
                cp.wait_send()
                cp.wait_recv()

    return pl.pallas_call(
        body, name=name, in_specs=[HBM_SPEC] * T + [SEM_SPEC, SEM_SPEC, ANY], out_specs=[HBM_SPEC] * T,
        out_shape=[pltpu.HBM(a.shape, a.dtype) for a in slabs], input_output_aliases={t: t for t in range(T)},
        compiler_params=pltpu.CompilerParams(has_side_effects=SIDE_EFFECT),
    )(*slabs, send_sems, recv_sems, after)


def _swap_halves(name, grads):
    T = len(grads)

    def body(*refs):
        ins, got = refs[:T], refs[T:2 * T]
        send_sems, recv_sems = refs[2 * T:]
        x, y, c, _ = _position()
        remote = []
        for t in range(T):
            rows = ins[t].shape[1]
            remote.append(_start_in_chunks(
                _remote_copier(send_sems.at[t], recv_sems.at[t], (x, y, 1 - c)), ins[t].at[:, _half(rows, 1 - c)], got[t]))
        for cp in remote:
            cp.wait()

    return pl.pallas_call(
        body, name=name, in_specs=[ANY] * T, out_specs=[ANY] * T,
        out_shape=[jax.ShapeDtypeStruct((g.shape[0], g.shape[1] // 2, g.shape[2]), g.dtype) for g in grads],
        scratch_shapes=[pltpu.SemaphoreType.DMA((T,)), pltpu.SemaphoreType.DMA((T,))],
    )(*grads)


def _swap_start(name, grads):
    T = len(grads)
    lands = [lax.empty((g.shape[0], g.shape[1] // 2, g.shape[2]), g.dtype) for g in grads]

    def body(*refs):
        ins, got = refs[:T], refs[T:2 * T]
        send_sems, recv_sems = refs[2 * T:2 * T + 2]
        token = refs[-1]
        x, y, c, _ = _position()
        for t in range(T):
            _start_in_chunks(_remote_copier(send_sems.at[t], recv_sems.at[t], (x, y, 1 - c)),
                             ins[t].at[:, _half(ins[t].shape[1], 1 - c)], got[t], waited_here=False)
        token[...] = jnp.zeros_like(token)

    arrays = list(grads) + lands
    res = pl.pallas_call(
        body, name=name, in_specs=[HBM_SPEC] * (2 * T),
        out_specs=[SEM_SPEC, SEM_SPEC] + [HBM_SPEC] * (2 * T) + [VMEM_WHOLE],
        out_shape=[pltpu.SemaphoreType.DMA((T,)), pltpu.SemaphoreType.DMA((T,))]
        + [pltpu.HBM(a.shape, a.dtype) for a in arrays] + [jax.ShapeDtypeStruct((8, LANES), jnp.float32)],
        input_output_aliases={i: 2 + i for i in range(2 * T)},
        compiler_params=pltpu.CompilerParams(has_side_effects=SIDE_EFFECT),
    )(*[pltpu.with_memory_space_constraint(a, pltpu.HBM) for a in arrays])
    return res[0], res[1], res[2:2 + T], res[2 + T:2 + 2 * T], res[-1]


def _swap_wait(name, flight, after):
    send_sems, recv_sems, grads, lands, _ = flight
    T = len(grads)

    def body(*refs):
        ins, got = refs[:T], refs[T:2 * T]
        send, recv = refs[2 * T:2 * T + 2]
        x, y, c, _ = _position()
        for t in range(T):
            cp = _remote_copier(send.at[t], recv.at[t], (x, y, 1 - c))(ins[t].at[:, _half(ins[t].shape[1], 1 - c)], got[t])
            cp.wait_send()
            cp.wait_recv()

    arrays = list(grads) + list(lands)
    res = pl.pallas_call(
        body, name=name, in_specs=[HBM_SPEC] * (2 * T) + [SEM_SPEC, SEM_SPEC, ANY], out_specs=[HBM_SPEC] * (2 * T),
        out_shape=[pltpu.HBM(a.shape, a.dtype) for a in arrays], input_output_aliases={i: i for i in range(2 * T)},
        compiler_params=pltpu.CompilerParams(has_side_effects=SIDE_EFFECT),
    )(*arrays, send_sems, recv_sems, after)
    return res[:T], res[T:]


def _scatter_start(name, chip_sums):
    T = len(chip_sums)
    lands = [lax.empty((3,) + s.shape[1:], s.dtype) for s in chip_sums]

    def body(*refs):
        srcs, dsts = refs[:T], refs[T:2 * T]
        send_sems, recv_sems = refs[2 * T:2 * T + 2]
        token = refs[-1]
        x, y, c, other_chips = _position()
        for t in range(T):
            for k, (cx, cy) in enumerate(other_chips):
                _start_in_chunks(_remote_copier(send_sems.at[3 * t + k], recv_sems.at[3 * t + k], (cx, cy, c)),
                                 srcs[t].at[2 * cx + cy], dsts[t].at[k], waited_here=False)
        token[...] = jnp.zeros_like(token)

    hbm = [pltpu.HBM(a.shape, a.dtype) for a in list(chip_sums) + lands]
    res = pl.pallas_call(
        body, name=name, in_specs=[HBM_SPEC] * (2 * T),
        out_specs=[SEM_SPEC, SEM_SPEC] + [HBM_SPEC] * (2 * T) + [VMEM_WHOLE],
        out_shape=[pltpu.SemaphoreType.DMA((3 * T,)), pltpu.SemaphoreType.DMA((3 * T,))] + hbm
        + [jax.ShapeDtypeStruct((8, LANES), jnp.float32)],
        input_output_aliases={i: 2 + i for i in range(2 * T)},
        compiler_params=pltpu.CompilerParams(has_side_effects=SIDE_EFFECT),
    )(*[pltpu.with_memory_space_constraint(a, pltpu.HBM) for a in list(chip_sums) + lands])
    return res[0], res[1], res[2:2 + T], res[2 + T:2 + 2 * T], res[-1]


def _scatter_wait(name, send_sems, recv_sems, chip_sums, lands, after):
    T = len(chip_sums)

    def body(*refs):
        srcs, dsts = refs[:T], refs[T:2 * T]
        send, recv = refs[2 * T:2 * T + 2]
        x, y, c, other_chips = _position()
        for t in range(T):
            for k, (cx, cy) in enumerate(other_chips):
                cp = _remote_copier(send.at[3 * t + k], recv.at[3 * t + k], (cx, cy, c))(srcs[t].at[2 * cx + cy], dsts[t].at[k])
                cp.wait_send()
                cp.wait_recv()

    arrays = list(chip_sums) + list(lands)
    res = pl.pallas_call(
        body, name=name, in_specs=[HBM_SPEC] * (2 * T) + [SEM_SPEC, SEM_SPEC, ANY],
        out_specs=[HBM_SPEC] * (2 * T), out_shape=[pltpu.HBM(a.shape, a.dtype) for a in arrays],
        input_output_aliases={i: i for i in range(2 * T)},
        compiler_params=pltpu.CompilerParams(has_side_effects=SIDE_EFFECT),
    )(*arrays, send_sems, recv_sems, after)
    return res[:T], res[T:]


def _as_shards(partial):
    return [p.reshape((N_CHIPS, -1, p.shape[-1])) for p in partial]


def _reduce_begin(tag, names, partial, pos):
    parts = _as_shards(partial)
    received = _swap_halves("swap_grad_halves_" + tag, parts)
    return _reduce_scatter_start(tag, names, parts, received, pos)


def _reduce_scatter_start(tag, names, parts, received, pos):
    chip_sums = _per_run(lambda name, p, r: _chip_sum(name, p, r, WIRE_DTYPE, pos), names, list(parts), list(received))
    return _scatter_start("scatter_start_" + tag, chip_sums)


def _reduce_end(tag, names, flight, after, pos):
    send_sems, recv_sems, chip_sums, lands, _ = flight
    chip_sums, others = _scatter_wait("scatter_wait_" + tag, send_sems, recv_sems, chip_sums, lands, after)
    reduced = _per_run(lambda name, cs, ot: _owner_sum(name, cs, ot, pos), names, list(chip_sums), list(others))
    return dict(zip(names, reduced))


def _join_start(name, halves):
    T = len(halves)

    def body(*refs):
        ins = refs[:T]
        send_sems, recv_sems = refs[T:T + 2]
        token = refs[-1]
        x, y, c, _ = _position()
        for t in range(T):
            _start_in_chunks(_remote_copier(send_sems.at[t], recv_sems.at[t], (x, y, 1 - c)),
                             ins[t].at[c], ins[t].at[c], waited_here=False)
        token[...] = jnp.zeros_like(token)

    res = pl.pallas_call(
        body, name=name, in_specs=[HBM_SPEC] * T,
        out_specs=[SEM_SPEC, SEM_SPEC] + [HBM_SPEC] * T + [VMEM_WHOLE],
        out_shape=[pltpu.SemaphoreType.DMA((T,)), pltpu.SemaphoreType.DMA((T,))]
        + [pltpu.HBM(h.shape, h.dtype) for h in halves] + [jax.ShapeDtypeStruct((8, LANES), jnp.float32)],
        input_output_aliases={t: 2 + t for t in range(T)},
        compiler_params=pltpu.CompilerParams(has_side_effects=SIDE_EFFECT),
    )(*[pltpu.with_memory_space_constraint(h, pltpu.HBM) for h in halves])
    return res[0], res[1], res[2:2 + T], res[-1]


def _join_wait(name, flight, after):
    send_sems, recv_sems, halves, _ = flight
    T = len(halves)

    def body(*refs):
        ins = refs[:T]
        send, recv = refs[T:T + 2]
        x, y, c, _ = _position()
        for t in range(T):
            cp = _remote_copier(send.at[t], recv.at[t], (x, y, 1 - c))(ins[t].at[c], ins[t].at[1 - c])
            cp.wait_send()
            cp.wait_recv()

    return pl.pallas_call(
        body, name=name, in_specs=[HBM_SPEC] * T + [SEM_SPEC, SEM_SPEC, ANY], out_specs=[HBM_SPEC] * T,
        out_shape=[pltpu.HBM(h.shape, h.dtype) for h in halves], input_output_aliases={t: t for t in range(T)},
        compiler_params=pltpu.CompilerParams(has_side_effects=SIDE_EFFECT),
    )(*halves, send_sems, recv_sems, after)


def _peers():
    x, y, c, _ = _position()
    flips = [(fx, fy, fc) for fx in (0, 1) for fy in (0, 1) for fc in (0, 1)][1:]
    return [(jnp.bitwise_xor(x, fx), jnp.bitwise_xor(y, fy), jnp.bitwise_xor(c, fc)) for fx, fy, fc in flips]


def _device_index(px, py, pc):
    return 4 * px + 2 * py + pc


def _exchange_start(name, slots):
    def body(slots_ref, send_sems, recv_sems, thru, token):
        x, y, c, _ = _position()
        own = slots_ref.at[_device_index(x, y, c)]
        for k, peer in enumerate(_peers()):
            _remote_copier(send_sems.at[k], recv_sems.at[k], peer)(own, own).start()
        token[...] = jnp.zeros_like(token)

    res = pl.pallas_call(
        body, name=name, in_specs=[HBM_SPEC], out_specs=[SEM_SPEC, SEM_SPEC, HBM_SPEC, VMEM_WHOLE],
        out_shape=[pltpu.SemaphoreType.DMA((N_DEV - 1,)), pltpu.SemaphoreType.DMA((N_DEV - 1,)),
                   pltpu.HBM(slots.shape, slots.dtype), jax.ShapeDtypeStruct((8, LANES), jnp.float32)],
        input_output_aliases={0: 2}, compiler_params=pltpu.CompilerParams(has_side_effects=SIDE_EFFECT),
    )(pltpu.with_memory_space_constraint(slots, pltpu.HBM))
    return res[0], res[1], res[2], res[3]


def _exchange_wait(name, flight, after):
    send_sems, recv_sems, slots, _ = flight

    def body(slots_ref, send, recv, after_ref, out_ref):
        x, y, c, _ = _position()
        own = slots_ref.at[_device_index(x, y, c)]
        for k, peer in enumerate(_peers()):
            cp = _remote_copier(send.at[k], recv.at[k], peer)(own, slots_ref.at[_device_index(*peer)])
            cp.wait_send()
            cp.wait_recv()

    return pl.pallas_call(
        body, name=name, in_specs=[HBM_SPEC, SEM_SPEC, SEM_SPEC, ANY], out_specs=HBM_SPEC,
        out_shape=pltpu.HBM(slots.shape, slots.dtype), input_output_aliases={0: 0},
        compiler_params=pltpu.CompilerParams(has_side_effects=SIDE_EFFECT),
    )(slots, send_sems, recv_sems, after)


def _sum_slots(name, slots):
    n, rows, lanes = slots.shape
    tr = rows if slots.size * 4 <= 2 ** 23 else _row_tile(rows, lanes * n)

    def body(s_ref, o_ref):
        acc = s_ref[0]
        for d in range(1, n):
            acc = acc + s_ref[d]
        o_ref[...] = acc

    return pl.pallas_call(
        body, name=name, grid=(rows // tr,), in_specs=[pl.BlockSpec((n, tr, lanes), lambda i: (0, i, 0))],
        out_specs=pl.BlockSpec((tr, lanes), lambda i: (i, 0)), out_shape=jax.ShapeDtypeStruct((rows, lanes), jnp.float32),
        compiler_params=_params(1))(slots)


BIG = ("w_in", "pool_w", "w_out", "w_q", "w_k", "w_v", "w_o", "w_gate", "w_up", "w_down")
SMALL = ("norm_mix_g", "pool_scale", "sgu_norm_g", "w_spatial", "b_spatial", "norm_xattn_g", "norm_mem_g",
         "norm_ffn_g", "final_norm_g")
GATHER_ORDER = (("w_in", "pool_w"), ("w_out",), ("w_q", "w_k", "w_v", "w_o"), ("w_gate", "w_up"), ("w_down",))
WEIGHTS = ("norm_mix_g", "w_in", "pool_w", "pool_scale", "sgu_norm_g", "w_spatial", "b_spatial", "w_out",
           "norm_xattn_g", "norm_mem_g", "w_q", "w_k", "w_v", "w_o", "norm_ffn_g", "w_gate", "w_up", "w_down",
           "final_norm_g")


def _as2d(a):
    return a.reshape(-1, a.shape[-1])


def _as_lanes(a):
    return a.reshape(-1, LANES)


def kernel(x, mem, norm_mix_g, w_in, pool_w, pool_scale, sgu_norm_g, w_spatial, b_spatial, w_out, norm_xattn_g, norm_mem_g, w_q, w_k, w_v, w_o, norm_ffn_g, w_gate, w_up, w_down, final_norm_g, loss_target, m_norm_mix_g, m_w_in, m_pool_w, m_pool_scale, m_sgu_norm_g, m_w_spatial, m_b_spatial, m_w_out, m_norm_xattn_g, m_norm_mem_g, m_w_q, m_w_k, m_w_v, m_w_o, m_norm_ffn_g, m_w_gate, m_w_up, m_w_down, m_final_norm_g, v_norm_mix_g, v_w_in, v_pool_w, v_pool_scale, v_sgu_norm_g, v_w_spatial, v_b_spatial, v_w_out, v_norm_xattn_g, v_norm_mem_g, v_w_q, v_w_k, v_w_v, v_w_o, v_norm_ffn_g, v_w_gate, v_w_up, v_w_down, v_final_norm_g):
    args = dict(locals())
    weights = {n: args[n] for n in WEIGHTS}
    moments_m = {n: args["m_" + n] for n in WEIGHTS}
    moments_v = {n: args["v_" + n] for n in WEIGHTS}
    low = MXU_DTYPE

    xs = x[0]
    mems = mem[0]
    target = loss_target[0]
    D = xs.shape[1]
    G, PG = pool_w.shape[1], pool_w.shape[3]

    pos = jnp.stack([2 * lax.axis_index("x") + lax.axis_index("y"), lax.axis_index("c")]).astype(jnp.int32)
    def cast(names, after=None):
        slabs = _per_run(lambda name, xs: _cast_into_slabs(name, xs, WIRE_DTYPE, pos, after), names,
                         [_as2d(weights[n]) for n in names])
        return dict(zip(names, slabs))

    def land(gi, flight, after, relayed=True):
        wait = _relayed_wait if relayed else _gather_wait
        landed = wait("gather_wait_%d" % gi, flight, after)
        return dict(zip(GATHER_ORDER[gi], _share_with_sibling("gather_share_%d" % gi, landed)))

    def rows(a):
        return a.reshape(-1, a.shape[-1])

    g_mix, g_xattn, g_mem, g_ffn = norm_mix_g, norm_xattn_g, norm_mem_g, norm_ffn_g
    g_final = final_norm_g.reshape(1, D)
    ws = w_spatial[0]
    b_full = jnp.broadcast_to(b_spatial[0][:, :, None], ws.shape[:2] + (sgu_norm_g.shape[1] // ws.shape[0],))

    first = cast(GATHER_ORDER[0])
    _, (sent_0,), _, started = _gather_step("gather_step_0", direct=[[first[n] for n in GATHER_ORDER[0]]])
    rest = cast([n for group in GATHER_ORDER[1:3] for n in group], after=started)
    _, (sent_1,), (hop_2,), started = _gather_step(
        "gather_step_1", direct=[[rest[n] for n in GATHER_ORDER[1]]], ring=[[rest[n] for n in GATHER_ORDER[2]]])
    h1 = _rmsnorm("norm_mix", xs, g_mix, low, after=started)
    mn = _rmsnorm("norm_mem", mems, g_mem, low)
    rest.update(cast(GATHER_ORDER[3], after=h1))
    rest.update(cast(GATHER_ORDER[4], after=rest[GATHER_ORDER[3][-1]]))
    slabs = [[rest[n] for n in group] for group in GATHER_ORDER[1:]]
    got = land(0, sent_0, rest["w_down"], relayed=False)
    wg_in = got["w_in"]
    pw = got["pool_w"].reshape(N_CHIPS, G, PG // N_CHIPS, PG).transpose(1, 0, 2, 3).reshape(G, PG, PG)
    proj = _matmul_nn_cols("proj_in", h1, wg_in, jnp.float32)
    z = _mixer_fwd(proj, pw, pool_scale, sgu_norm_g, ws, b_full)
    full = {"w_out": rows(land(1, sent_1, z, relayed=False)["w_out"])}
    (relay_2,), _, (hop_3,), started = _gather_step("gather_step_2", arrived=[hop_2], ring=slabs[2:3], after=full["w_out"])
    x1, h2 = _proj_residual_norm("mix_out", z, full["w_out"], xs, g_xattn, after=started)
    landed = _relayed_wait("gather_wait_2", relay_2, h2)
    full["w_q"] = rows(_share_with_sibling("gather_share_2_q", landed[:1])[0])
    sharing = _share_start("gather_share_2_start", landed[1:])
    q = _matmul_nn("xattn_q", h2, full["w_q"], low, after=sharing[-1])
    full.update({n: rows(a) for n, a in zip(GATHER_ORDER[2][1:], _share_wait("gather_share_2_wait", sharing, q))})
    k = _matmul_nn("xattn_k", mn, full["w_k"], low)
    vv = _matmul_nn("xattn_v", mn, full["w_v"], low)
    o = _attn_fwd(q, k, vv)
    (relay_3,), (sent_4,), _, started = _gather_step("gather_step_3", arrived=[hop_3], direct=slabs[3:4], after=o)
    x2, h3 = _proj_residual_norm("xattn_out", o, full["w_o"], x1, g_ffn, after=started)
    got = land(3, relay_3, h3)
    wg_gate, wg_up = got["w_gate"], got["w_up"]
    act, act_by_gate, act_by_up = _ffn_gate_up(h3, wg_gate, wg_up)
    full["w_down"] = rows(land(4, sent_4, act, relayed=False)["w_down"])
    x3 = _ffn_down(act, full["w_down"], x2)
    dx3, dx3_low, loss_part, d_final_g = _final_norm_loss(x3, target, g_final)

    dgate, dup = _ffn_down_bwd(dx3_low, full["w_down"], act_by_gate, act_by_up, N_CHIPS)
    (d_w_down,) = _matmul_tn("grad_w_down", act, [dx3_low], WIRE_DTYPE)
    (d_w_gate,) = _matmul_tn_cols("grad_w_gate", h3, [dgate], N_CHIPS, WIRE_DTYPE, tka=1024)
    (d_w_up,) = _matmul_tn_cols("grad_w_up", h3, [dup], N_CHIPS, WIRE_DTYPE, tka=1024)
    ffn_names = ("w_gate", "w_up", "w_down")
    ffn_swap = _swap_start("swap_start_ffn", _as_shards([d_w_gate, d_w_up, d_w_down]))
    dh3 = _matmul_nt_cols("ffn_dh", [dgate, dup], [wg_gate, wg_up], jnp.float32, tn=1024, after=ffn_swap[-1])
    ffn_flight = _reduce_scatter_start("ffn", ffn_names, *_swap_wait("swap_wait_ffn", ffn_swap, dh3), pos)
    dx2, dx2_low, d_ffn_g = _rmsnorm_bwd("norm_ffn_bwd", dh3, x2, g_ffn, dx3, low)

    do = _matmul_nt("xattn_do", [dx2_low], [full["w_o"]], low, after=ffn_flight[-1])
    (d_w_o,) = _matmul_tn("grad_w_o", o, [dx2_low], WIRE_DTYPE)
    dq, dk, dvv = _attn_bwd(q, k, vv, do)
    (d_w_q,) = _matmul_tn("grad_w_q", h2, [dq], WIRE_DTYPE)
    dx1, dx1_low, d_xattn_g = _matmul_nt_norm_bwd("xattn_dh_norm_bwd", dq, full["w_q"], x1, g_xattn, dx2, low)
    d_w_k, d_w_v = _matmul_tn("grad_w_kv", mn, [dk, dvv], WIRE_DTYPE)
    dmn = _matmul_nt("xattn_dmem", [dk, dvv], [full["w_k"], full["w_v"]], jnp.float32)
    (d_mem_g,) = _rmsnorm_bwd("norm_mem_bwd", dmn, mems, g_mem, None, None)

    (d_w_out,) = _matmul_tn("grad_w_out", z, [dx1_low], WIRE_DTYPE)
    attn_names = ("w_q", "w_k", "w_v", "w_o", "w_out")
    attn_swap = _swap_start("swap_start_attn", _as_shards([d_w_q, d_w_k, d_w_v, d_w_o, d_w_out]))
    dz = _matmul_nt("mix_dz", [dx1_low], [full["w_out"]], jnp.float32, after=attn_swap[-1])
    attn_flight = _reduce_scatter_start("attn", attn_names, *_swap_wait("swap_wait_attn", attn_swap, dz), pos)
    dproj, d_pw, d_ps, d_gs, d_ws, d_b = _mixer_bwd(proj, dz, pw, pool_scale, sgu_norm_g, ws, b_full)
    (d_w_in,) = _matmul_tn_cols("grad_w_in", h1, [dproj], N_CHIPS, WIRE_DTYPE, tka=1024, after=attn_flight[-1])
    d_pool_w = d_pw.reshape(G, N_CHIPS, PG // N_CHIPS, PG).transpose(1, 0, 2, 3).reshape(N_CHIPS, PG, PG).astype(WIRE_DTYPE)
    mix_names = ("w_in", "pool_w")
    mix_flight = _reduce_begin("mix", mix_names, [d_w_in, d_pool_w], pos)
    grad_x, d_mix_g = _matmul_nt_norm_bwd("mix_dh_norm_bwd", dproj, wg_in, xs, g_mix, dx1, None, tm=512, after=mix_flight[-1])

    small_parts = {
        "norm_mix_g": d_mix_g, "pool_scale": d_ps, "sgu_norm_g": d_gs, "w_spatial": d_ws, "b_spatial": d_b,
        "norm_xattn_g": d_xattn_g, "norm_mem_g": d_mem_g, "norm_ffn_g": d_ffn_g, "final_norm_g": d_final_g,
    }
    row_counts = [small_parts[n].size // LANES for n in SMALL]
    loss_rows = jnp.pad(loss_part, ((0, 7), (0, LANES - 1)))
    packed = jnp.concatenate([_as_lanes(small_parts[n]) for n in SMALL] + [loss_rows], axis=0)
    device = _device_index(lax.axis_index("x"), lax.axis_index("y"), lax.axis_index("c"))
    slots = lax.dynamic_update_slice(jnp.zeros((N_DEV,) + packed.shape, jnp.float32), packed[None], (device, 0, 0))
    small_flight = _exchange_start("small_grads_start", slots)

    groups = (("ffn", ffn_names, ffn_flight), ("attn", attn_names, attn_flight), ("mix", mix_names, mix_flight))
    joins, after = {}, small_flight[-1]
    for tag, names, flight in groups:
        reduced = _reduce_end(tag, names, flight, after, pos)
        joins[tag] = _join_start("join_start_" + tag, [reduced[n] for n in names])
        after = joins[tag][-1]
    small_sum = _sum_slots("small_grads_sum", _exchange_wait("small_grads_wait", small_flight, after))

    out_grad, out_delta, out_m, out_v = {}, {}, {}, {}
    after = small_sum
    for tag, names, _ in groups:
        grads = [g.reshape(-1, g.shape[-1]) for g in _join_wait("join_wait_" + tag, joins[tag], after)]
        updates = _per_run(_adamw, names, [_as2d(weights[n]) for n in names], grads,
                           [_as2d(moments_m[n]) for n in names], [_as2d(moments_v[n]) for n in names])
        for n, update in zip(names, updates):
            shape = weights[n].shape
            out_grad[n], out_delta[n], out_m[n], out_v[n] = (a.reshape(shape) for a in update)
        after = updates[-1][1]
    small_out = _adamw_small(
        small_sum, [_as_lanes(weights[n]) for n in SMALL], [_as_lanes(moments_m[n]) for n in SMALL],
        [_as_lanes(moments_v[n]) for n in SMALL], row_counts)
    for t, n in enumerate(SMALL):
        shape = weights[n].shape
        out_grad[n], out_delta[n], out_m[n], out_v[n] = (a.reshape(shape) for a in small_out[4 * t:4 * t + 4])

    loss = small_sum[sum(row_counts), 0]
    return (loss, grad_x[None], *[out_grad[n] for n in WEIGHTS], *[out_delta[n] for n in WEIGHTS],
            *[out_m[n] for n in WEIGHTS], *[out_v[n] for n in WEIGHTS])
```

```python
import math

import jax
import jax.numpy as jnp
from jax import lax
from jax.experimental import pallas as pl
from jax.experimental.pallas import tpu as pltpu

MXU_DTYPE = jnp.bfloat16
WIRE_DTYPE = jnp.bfloat16

EPS = 1e-6
CHUNK = 64
POOL_WINDOWS = (2, 4, 8, 16)
POOL_HALO = 16
N_XATTN_HEADS = 4
ADAM_LR = 0.001
ADAM_B1 = 0.9
ADAM_B2 = 0.999
ADAM_EPS = 1e-08
ADAM_WD = 0.01
ADAM_STEP = 10

N_CHIPS = 4
N_DEV = 8
LANES = 128
VMEM_LIMIT_BYTES = 56 * 2 ** 20
COPY_CHUNK_BYTES = 512 * 2 ** 10
COPY_ROW_ALIGN = 16

MESH = pl.DeviceIdType.MESH
ANY = pl.BlockSpec(memory_space=pl.ANY)
VMEM_WHOLE = pl.BlockSpec(memory_space=pltpu.VMEM)
HBM_SPEC = pl.BlockSpec(memory_space=pltpu.HBM)
SEM_SPEC = pl.BlockSpec(memory_space=pltpu.SEMAPHORE)
SIDE_EFFECT = pltpu.SideEffectType.DATAFLOW_SIDE_EFFECTING

NN = (((1,), (0,)), ((), ()))
NT = (((1,), (1,)), ((), ()))
TN = (((0,), (0,)), ((), ()))


def _tile(n, target, mult):
    best = None
    for t in range(mult, min(n, target) + 1, mult):
        if n % t == 0:
            best = t
    return n if best is None else best


def _params(n_grid_axes):
    return pltpu.CompilerParams(dimension_semantics=("arbitrary",) * n_grid_axes, vmem_limit_bytes=VMEM_LIMIT_BYTES)


def _rsqrt_mean_sq(x):
    return lax.rsqrt(jnp.mean(x * x, axis=-1, keepdims=True) + EPS)


def _sum_all(x):
    return jnp.sum(jnp.sum(x, axis=1, keepdims=True), axis=0, keepdims=True)


def _sigmoid(x):
    return 0.5 * jnp.tanh(0.5 * x) + 0.5


def _store_epilogue(accs, ex_refs, out_refs, i, j):
    for a, o in zip(accs, out_refs):
        o[...] = a.astype(o.dtype)


def _fused_matmul(name, grid, operands, pairs, acc_shapes, extras, outs, epilogue):
    nk = grid[2]
    n_op, n_ex, n_out, n_acc = len(operands), len(extras), len(outs), len(acc_shapes)
    in_place = nk > 1 and epilogue is _store_epilogue and all(o[1] == jnp.float32 for o in outs)

    def body(*refs):
        op_refs = refs[:n_op]
        ex_refs = refs[n_op:n_op + n_ex]
        out_refs = refs[n_op + n_ex:n_op + n_ex + n_out]
        acc_refs = out_refs if in_place else refs[n_op + n_ex + n_out:]
        i, j, k = pl.program_id(0), pl.program_id(1), pl.program_id(2)
        loaded = {}

        def operand(n):
            if n not in loaded:
                loaded[n] = op_refs[n][...].astype(MXU_DTYPE)
            return loaded[n]

        def product(ia, ib, dims):
            return lax.dot_general(operand(ia), operand(ib), dims, preferred_element_type=jnp.float32)

        def products():
            parts = [None] * n_acc
            for ia, ib, iacc, dims in pairs:
                d = product(ia, ib, dims)
                parts[iacc] = d if parts[iacc] is None else parts[iacc] + d
            return parts

        if nk == 1:
            epilogue(products(), ex_refs, out_refs, i, j)
            return

        @pl.when(k == 0)
        def _():
            for a in range(n_acc):
                acc_refs[a][...] = jnp.zeros_like(acc_refs[a])

        for a, part in enumerate(products()):
            acc_refs[a][...] += part

        if not in_place:
            @pl.when(k == nk - 1)
            def _():
                epilogue([acc_refs[a][...] for a in range(n_acc)], ex_refs, out_refs, i, j)

    arrays = [o[0] for o in operands] + [e[0] for e in extras]
    in_specs = [pl.BlockSpec(o[1], o[2]) for o in operands]
    in_specs += [ANY if e[1] is None else pl.BlockSpec(e[1], e[2]) for e in extras]
    return pl.pallas_call(
        body,
        name=name,
        grid=grid,
        in_specs=in_specs,
        out_specs=[pl.BlockSpec(o[2], o[3]) for o in outs],
        out_shape=[jax.ShapeDtypeStruct(o[0], o[1]) for o in outs],
        scratch_shapes=[pltpu.VMEM(s, jnp.float32) for s in acc_shapes] if nk > 1 and not in_place else [],
        compiler_params=_params(3),
    )(*arrays)


def _matmul_nn(name, a, b, out_dtype, tm=2048, tn=1024, after=None):
    M, K = a.shape
    N = b.shape[1]
    tm, tn = _tile(M, tm, 16), _tile(N, tn, LANES)
    return _fused_matmul(
        name, (M // tm, N // tn, 1),
        [(a, (tm, K), lambda i, j, k: (i, 0)), (b, (K, tn), lambda i, j, k: (0, j))],
        [(0, 1, 0, NN)], [(tm, tn)], _order_after(after),
        [((M, N), out_dtype, (tm, tn), lambda i, j, k: (i, j))], _store_epilogue)[0]


def _matmul_nn_cols(name, a, b3, out_dtype, tm=2048):
    M, K = a.shape
    J, _, Nc = b3.shape
    tm = _tile(M, tm, 16)
    return _fused_matmul(
        name, (M // tm, J, 1),
        [(a, (tm, K), lambda i, j, k: (i, 0)), (b3, (None, K, Nc), lambda i, j, k: (j, 0, 0))],
        [(0, 1, 0, NN)], [(tm, Nc)], [],
        [((M, J * Nc), out_dtype, (tm, Nc), lambda i, j, k: (i, j))], _store_epilogue)[0]


def _order_after(after):
    return [] if after is None else [(after, None, None)]


def _matmul_nt(name, a_list, b_list, out_dtype, tm=2048, tn=1024, after=None):
    M, K = a_list[0].shape
    N = b_list[0].shape[0]
    tm, tn = _tile(M, tm, 16), _tile(N, tn, LANES)
    n = len(a_list)
    operands = [(a, (tm, K), lambda i, j, k: (i, 0)) for a in a_list]
    operands += [(b, (tn, K), lambda i, j, k: (j, 0)) for b in b_list]
    return _fused_matmul(
        name, (M // tm, N // tn, 1), operands,
        [(p, n + p, 0, NT) for p in range(n)], [(tm, tn)], _order_after(after),
        [((M, N), out_dtype, (tm, tn), lambda i, j, k: (i, j))], _store_epilogue)[0]


def _matmul_nt_cols(name, a_list, b3_list, out_dtype, tm=1024, tn=512, after=None):
    M = a_list[0].shape[0]
    J, N, Kc = b3_list[0].shape
    tm, tn = _tile(M, tm, 16), _tile(N, tn, LANES)
    n = len(a_list)
    operands = [(a, (tm, Kc), lambda i, j, k: (i, k)) for a in a_list]
    operands += [(b, (None, tn, Kc), lambda i, j, k: (k, j, 0)) for b in b3_list]
    return _fused_matmul(
        name, (M // tm, N // tn, J), operands,
        [(p, n + p, 0, NT) for p in range(n)], [(tm, tn)], _order_after(after),
        [((M, N), out_dtype, (tm, tn), lambda i, j, k: (i, j))], _store_epilogue)[0]


def _matmul_tn(name, a, b_list, out_dtype, tka=1024, tn=1024, tkm=4096):
    M, Ka = a.shape
    N = b_list[0].shape[1]
    tka, tn, tkm = _tile(Ka, tka, LANES), _tile(N, tn, LANES), _tile(M, tkm, 16)
    n = len(b_list)
    operands = [(a, (tkm, tka), lambda i, j, k: (k, i))]
    operands += [(b, (tkm, tn), lambda i, j, k: (k, j)) for b in b_list]
    return _fused_matmul(
        name, (Ka // tka, N // tn, M // tkm), operands,
        [(0, 1 + p, p, TN) for p in range(n)], [(tka, tn)] * n, [],
        [((Ka, N), out_dtype, (tka, tn), lambda i, j, k: (i, j))] * n, _store_epilogue)


def _matmul_tn_cols(name, a, b_list, n_shards, out_dtype, tka=512, tkm=4096, after=None):
    M, Ka = a.shape
    Nc = b_list[0].shape[1] // n_shards
    tka, tkm = _tile(Ka, tka, LANES), _tile(M, tkm, 16)
    n = len(b_list)
    operands = [(a, (tkm, tka), lambda i, j, k: (k, i))]
    operands += [(b, (tkm, Nc), lambda i, j, k: (k, j)) for b in b_list]
    return _fused_matmul(
        name, (Ka // tka, n_shards, M // tkm), operands,
        [(0, 1 + p, p, TN) for p in range(n)], [(tka, Nc)] * n, _order_after(after),
        [((n_shards, Ka, Nc), out_dtype, (None, tka, Nc), lambda i, j, k: (j, i, 0))] * n, _store_epilogue)


def _rmsnorm(name, x, g_row, out_dtype, tm=512, after=None):
    S, D = x.shape
    tm = _tile(S, tm, 16)

    def body(x_ref, g_ref, *rest):
        o_ref = rest[-1]
        xv = x_ref[...]
        o_ref[...] = (xv * _rsqrt_mean_sq(xv) * g_ref[...]).astype(o_ref.dtype)

    ordered = [] if after is None else [after]
    return pl.pallas_call(
        body, name=name, grid=(S // tm,),
        in_specs=[pl.BlockSpec((tm, D), lambda i: (i, 0)), pl.BlockSpec((1, D), lambda i: (0, 0))] + [ANY] * len(ordered),
        out_specs=pl.BlockSpec((tm, D), lambda i: (i, 0)),
        out_shape=jax.ShapeDtypeStruct((S, D), out_dtype),
        compiler_params=_params(1),
    )(x, g_row, *ordered)


def _rmsnorm_bwd_values(dh, xin, g):
    r = _rsqrt_mean_sq(xin)
    xhat = xin * r
    dhg = dh * g
    dx = r * (dhg - xhat * jnp.mean(dhg * xhat, axis=-1, keepdims=True))
    return dx, jnp.sum(dh * xhat, axis=0, keepdims=True)


def _rmsnorm_bwd(name, dh, xin, g_row, dres, low_dtype, tm=256):
    S, D = xin.shape
    tm = _tile(S, tm, 16)
    has_dx = dres is not None
    has_low = has_dx and low_dtype is not None

    def body(*refs):
        dh_ref, x_ref, g_ref = refs[:3]
        rest = refs[3:]
        dx, dg = _rmsnorm_bwd_values(dh_ref[...], x_ref[...], g_ref[...])
        if has_dx:
            dres_ref, rest = rest[0], rest[1:]
            dx = dx + dres_ref[...]
            rest[0][...] = dx
            if has_low:
                rest[1][...] = dx.astype(low_dtype)
        dg_ref = rest[-1]

        @pl.when(pl.program_id(0) == 0)
        def _():
            dg_ref[...] = jnp.zeros_like(dg_ref)

        dg_ref[...] += dg

    row = pl.BlockSpec((tm, D), lambda i: (i, 0))
    vec = pl.BlockSpec((1, D), lambda i: (0, 0))
    ins, in_specs = [dh, xin, g_row], [row, row, vec]
    out_shape, out_specs = [], []
    if has_dx:
        ins.append(dres)
        in_specs.append(row)
        out_shape.append(jax.ShapeDtypeStruct((S, D), jnp.float32))
        out_specs.append(row)
        if has_low:
            out_shape.append(jax.ShapeDtypeStruct((S, D), low_dtype))
            out_specs.append(row)
    out_shape.append(jax.ShapeDtypeStruct((1, D), jnp.float32))
    out_specs.append(vec)
    return pl.pallas_call(
        body, name=name, grid=(S // tm,), in_specs=in_specs, out_specs=out_specs, out_shape=out_shape,
        compiler_params=_params(1),
    )(*ins)


def _chunk_mask(blk):
    t = lax.broadcasted_iota(jnp.int32, (blk, blk), 0)
    s = lax.broadcasted_iota(jnp.int32, (blk, blk), 1)
    return (s // CHUNK) <= (t // CHUNK)


def _pool_inputs(a, halo, row0, tm, g, pg):
    cols = slice(g * pg, (g + 1) * pg)
    w = POOL_WINDOWS[g]
    s = jnp.concatenate([halo[:, cols], a[:, cols]], axis=0)
    span = 1
    while span < w:
        s = s + pltpu.roll(s, span, axis=0)
        span *= 2
    t = lax.broadcasted_iota(jnp.int32, (tm, 1), 0) + row0
    cnt = jnp.minimum(t + 1, w).astype(jnp.float32)
    return s[POOL_HALO:] / cnt - a[:, cols]


def _mixer_specs(S, tm, DP, DS):
    per_halo = tm // POOL_HALO
    n_halo = S // POOL_HALO
    a_spec = pl.BlockSpec((tm, DP), lambda i: (i, 0))
    prev_spec = pl.BlockSpec((POOL_HALO, DP), lambda i: (jnp.maximum(i * per_halo - 1, 0), 0))
    next_spec = pl.BlockSpec((POOL_HALO, DP), lambda i: (jnp.minimum((i + 1) * per_halo, n_halo - 1), 0))
    u_spec = pl.BlockSpec((tm, DS), lambda i: (i, 1))
    v_spec = pl.BlockSpec((tm, DS), lambda i: (i, 2))
    return a_spec, prev_spec, next_spec, u_spec, v_spec


def _whole(arr):
    nd = arr.ndim
    return pl.BlockSpec(arr.shape, lambda i: (0,) * nd)


def _mixer_fwd(proj, pw, ps_row, gs_row, ws, b_full, tm=512):
    S = proj.shape[0]
    G, PG, _ = pw.shape
    DP = G * PG
    H, BLK, _ = ws.shape
    DS = gs_row.shape[1]
    HD = DS // H
    assert DP == DS and proj.shape[1] == DP + 2 * DS
    tm = _tile(S, tm, BLK)

    def body(a_ref, prev_ref, u_ref, v_ref, pw_ref, ps_ref, gs_ref, ws_ref, b_ref, z_ref):
        i = pl.program_id(0)
        a = a_ref[...]
        halo = jnp.where(i > 0, prev_ref[...], 0.0)
        for g in range(G):
            cols = slice(g * PG, (g + 1) * PG)
            p = _pool_inputs(a, halo, i * tm, tm, g, PG)
            y = jnp.dot(p.astype(MXU_DTYPE), pw_ref[g], preferred_element_type=jnp.float32) * ps_ref[:, cols]
            z_ref[:, cols] = y.astype(z_ref.dtype)
        v = v_ref[...]
        vn = (v * _rsqrt_mean_sq(v) * gs_ref[...]).astype(MXU_DTYPE)
        u = u_ref[...]
        mask = _chunk_mask(BLK)
        for h in range(H):
            wm = jnp.where(mask, ws_ref[h], 0.0).astype(MXU_DTYPE)
            bias = b_ref[h]
            cols = slice(h * HD, (h + 1) * HD)
            for n in range(tm // BLK):
                rows = slice(n * BLK, (n + 1) * BLK)
                mixed = jnp.dot(wm, vn[rows, cols], preferred_element_type=jnp.float32) + bias
                z_ref[rows, DP + h * HD:DP + (h + 1) * HD] = (u[rows, cols] * mixed).astype(z_ref.dtype)

    a_spec, prev_spec, _, u_spec, v_spec = _mixer_specs(S, tm, DP, DS)
    return pl.pallas_call(
        body, name="mixer_fwd", grid=(S // tm,),
        in_specs=[a_spec, prev_spec, u_spec, v_spec, _whole(pw), _whole(ps_row), _whole(gs_row), _whole(ws), _whole(b_full)],
        out_specs=pl.BlockSpec((tm, DP + DS), lambda i: (i, 0)),
        out_shape=jax.ShapeDtypeStruct((S, DP + DS), MXU_DTYPE),
        compiler_params=_params(1),
    )(proj, proj, proj, proj, pw, ps_row, gs_row, ws, b_full)


def _mixer_bwd(proj, dz, pw, ps_row, gs_row, ws, b_full, tm=256):
    S = proj.shape[0]
    G, PG, _ = pw.shape
    DP = G * PG
    H, BLK, _ = ws.shape
    DS = gs_row.shape[1]
    HD = DS // H
    tm = _tile(S, tm, BLK)
    nb = S // tm
    ext = tm + POOL_HALO

    def body(a_ref, prev_ref, u_ref, v_ref, dzp_ref, dzp_next_ref, dzs_ref, pw_ref, ps_ref, gs_ref, ws_ref, b_ref,
             dproj_ref, dpw_ref, dps_ref, dgs_ref, dws_ref, db_ref, dvn_ref):
        i = pl.program_id(0)

        @pl.when(i == 0)
        def _():
            for r in (dpw_ref, dps_ref, dgs_ref, dws_ref, db_ref):
                r[...] = jnp.zeros_like(r)

        a = a_ref[...]
        halo = jnp.where(i > 0, prev_ref[...], 0.0)
        dyp = dzp_ref[...]
        dyp_next = jnp.where(i < nb - 1, dzp_next_ref[...], 0.0)
        dq_ext = jnp.concatenate([dyp, dyp_next], axis=0) * ps_ref[...]
        t_ext = lax.broadcasted_iota(jnp.int32, (ext, 1), 0) + i * tm
        for g in range(G):
            cols = slice(g * PG, (g + 1) * PG)
            w = POOL_WINDOWS[g]
            pb = _pool_inputs(a, halo, i * tm, tm, g, PG).astype(MXU_DTYPE)
            y_pre = jnp.dot(pb, pw_ref[g], preferred_element_type=jnp.float32)
            dps_ref[:, cols] += jnp.sum(dyp[:, cols] * y_pre, axis=0, keepdims=True)
            dqb = dq_ext[:, cols].astype(MXU_DTYPE)
            dpw_ref[g] += lax.dot_general(pb, dqb[:tm], TN, preferred_element_type=jnp.float32)
            dp = lax.dot_general(dqb, pw_ref[g], NT, preferred_element_type=jnp.float32)
            f = dp / jnp.minimum(t_ext + 1, w).astype(jnp.float32)
            span = 1
            while span < w:
                f = f + pltpu.roll(f, ext - span, axis=0)
                span *= 2
            dproj_ref[:, cols] = (f[:tm] - dp[:tm]).astype(dproj_ref.dtype)

        v = v_ref[...]
        rv = _rsqrt_mean_sq(v)
        vhat = v * rv
        gs = gs_ref[...]
        vnb = (vhat * gs).astype(MXU_DTYPE)
        u = u_ref[...]
        dys = dzs_ref[...]
        mask = _chunk_mask(BLK)
        for h in range(H):
            wm = jnp.where(mask, ws_ref[h], 0.0).astype(MXU_DTYPE)
            bias = b_ref[h]
            cols = slice(h * HD, (h + 1) * HD)
            dm_sum = jnp.zeros((BLK, HD), jnp.float32)
            dws_sum = jnp.zeros((BLK, BLK), jnp.float32)
            for n in range(tm // BLK):
                rows = slice(n * BLK, (n + 1) * BLK)
                vblk = vnb[rows, cols]
                mixed = jnp.dot(wm, vblk, preferred_element_type=jnp.float32) + bias
                dy = dys[rows, cols]
                dproj_ref[rows, DP + h * HD:DP + (h + 1) * HD] = (dy * mixed).astype(dproj_ref.dtype)
                dm = dy * u[rows, cols]
                dm_sum = dm_sum + dm
                dmb = dm.astype(MXU_DTYPE)
                dws_sum = dws_sum + lax.dot_general(dmb, vblk, NT, preferred_element_type=jnp.float32)
                dvn_ref[rows, cols] = lax.dot_general(wm, dmb, TN, preferred_element_type=jnp.float32)
            dws_ref[h] += jnp.where(mask, dws_sum, 0.0)
            db_ref[h:h + 1, :] += jnp.sum(dm_sum.T, axis=0, keepdims=True)
        dvn = dvn_ref[...]
        dvg = dvn * gs
        dv = rv * (dvg - vhat * jnp.mean(dvg * vhat, axis=-1, keepdims=True))
        dgs_ref[...] += jnp.sum(dvn * vhat, axis=0, keepdims=True)
        dproj_ref[:, DP + DS:] = dv.astype(dproj_ref.dtype)

    a_spec, prev_spec, next_spec, u_spec, v_spec = _mixer_specs(S, tm, DP, DS)
    dzp_spec = pl.BlockSpec((tm, DP), lambda i: (i, 0))
    dzs_spec = pl.BlockSpec((tm, DS), lambda i: (i, 1))
    small = [pw, ps_row, gs_row, ws, b_full]
    out_shape = [
        jax.ShapeDtypeStruct((S, DP + 2 * DS), MXU_DTYPE),
        jax.ShapeDtypeStruct(pw.shape, jnp.float32),
        jax.ShapeDtypeStruct(ps_row.shape, jnp.float32),
        jax.ShapeDtypeStruct(gs_row.shape, jnp.float32),
        jax.ShapeDtypeStruct(ws.shape, jnp.float32),
        jax.ShapeDtypeStruct((H, BLK), jnp.float32),
    ]
    out_specs = [pl.BlockSpec((tm, DP + 2 * DS), lambda i: (i, 0))] + [_whole(s) for s in out_shape[1:]]
    return pl.pallas_call(
        body, name="mixer_bwd", grid=(nb,),
        in_specs=[a_spec, prev_spec, u_spec, v_spec, dzp_spec, next_spec, dzs_spec] + [_whole(s) for s in small],
        out_specs=out_specs, out_shape=out_shape,
        scratch_shapes=[pltpu.VMEM((tm, DS), jnp.float32)],
        compiler_params=_params(1),
    )(proj, proj, proj, proj, dz, dz, dz, *small)


def _softmax_rows(q, k, scale):
    s = lax.dot_general(q, k, NT, preferred_element_type=jnp.float32) * scale
    e = jnp.exp(s - jnp.max(s, axis=-1, keepdims=True))
    return e / jnp.sum(e, axis=-1, keepdims=True)


def _attn_fwd(q, k, v, tm=512):
    S, D = q.shape
    M = k.shape[0]
    HD = D // N_XATTN_HEADS
    scale = HD ** -0.5
    tm = _tile(S, tm, 16)

    def body(q_ref, k_ref, v_ref, o_ref):
        for h in range(N_XATTN_HEADS):
            cols = slice(h * HD, (h + 1) * HD)
            p = _softmax_rows(q_ref[:, cols], k_ref[:, cols], scale)
            o = jnp.dot(p.astype(MXU_DTYPE), v_ref[:, cols], preferred_element_type=jnp.float32)
            o_ref[:, cols] = o.astype(o_ref.dtype)

    row = pl.BlockSpec((tm, D), lambda i: (i, 0))
    mem = pl.BlockSpec((M, D), lambda i: (0, 0))
    return pl.pallas_call(
        body, name="attn_fwd", grid=(S // tm,), in_specs=[row, mem, mem], out_specs=row,
        out_shape=jax.ShapeDtypeStruct((S, D), MXU_DTYPE), compiler_params=_params(1),
    )(q, k, v)


def _attn_bwd(q, k, v, do, tm=512):
    S, D = q.shape
    M = k.shape[0]
    HD = D // N_XATTN_HEADS
    scale = HD ** -0.5
    tm = _tile(S, tm, 16)

    def body(q_ref, k_ref, v_ref, do_ref, dq_ref, dk_ref, dv_ref):
        @pl.when(pl.program_id(0) == 0)
        def _():
            dk_ref[...] = jnp.zeros_like(dk_ref)
            dv_ref[...] = jnp.zeros_like(dv_ref)

        for h in range(N_XATTN_HEADS):
            cols = slice(h * HD, (h + 1) * HD)
            qh, kh, vh, doh = q_ref[:, cols], k_ref[:, cols], v_ref[:, cols], do_ref[:, cols]
            p = _softmax_rows(qh, kh, scale)
            dp = lax.dot_general(doh, vh, NT, preferred_element_type=jnp.float32)
            dv_ref[:, cols] += lax.dot_general(p.astype(MXU_DTYPE), doh, TN, preferred_element_type=jnp.float32)
            ds = (p * (dp - jnp.sum(dp * p, axis=-1, keepdims=True)) * scale).astype(MXU_DTYPE)
            dq_ref[:, cols] = jnp.dot(ds, kh, preferred_element_type=jnp.float32).astype(dq_ref.dtype)
            dk_ref[:, cols] += lax.dot_general(ds, qh, TN, preferred_element_type=jnp.float32)

    row = pl.BlockSpec((tm, D), lambda i: (i, 0))
    mem = pl.BlockSpec((M, D), lambda i: (0, 0))
    return pl.pallas_call(
        body, name="attn_bwd", grid=(S // tm,), in_specs=[row, mem, mem, row], out_specs=[row, mem, mem],
        out_shape=[jax.ShapeDtypeStruct((S, D), MXU_DTYPE), jax.ShapeDtypeStruct((M, D), jnp.float32),
                   jax.ShapeDtypeStruct((M, D), jnp.float32)],
        compiler_params=_params(1),
    )(q, k, v, do)


def _proj_residual_norm(name, a, w2d, resid, g_row, tm=512, after=None):
    M, K = a.shape
    N = w2d.shape[1]
    tm = _tile(M, tm, 16)

    def epilogue(accs, ex_refs, out_refs, i, j):
        x_new = ex_refs[0][...] + accs[0]
        out_refs[0][...] = x_new
        out_refs[1][...] = (x_new * _rsqrt_mean_sq(x_new) * ex_refs[1][...]).astype(MXU_DTYPE)

    row = lambda i, j, k: (i, 0)
    return _fused_matmul(
        name, (M // tm, 1, 1),
        [(a, (tm, K), row), (w2d, (K, N), lambda i, j, k: (0, 0))],
        [(0, 1, 0, NN)], [(tm, N)],
        [(resid, (tm, N), row), (g_row, (1, N), lambda i, j, k: (0, 0))] + _order_after(after),
        [((M, N), jnp.float32, (tm, N), row), ((M, N), MXU_DTYPE, (tm, N), row)], epilogue)


def _ffn_gate_up(h, wg3, wu3, tm=512):
    M, K = h.shape
    J, _, Nc = wg3.shape
    tm = _tile(M, tm, 16)

    def epilogue(accs, ex_refs, out_refs, i, j):
        gate, up = accs
        sg = _sigmoid(gate)
        silu = gate * sg
        out_refs[0][...] = (silu * up).astype(MXU_DTYPE)
        out_refs[1][...] = (up * (sg * (1.0 + gate * (1.0 - sg)))).astype(MXU_DTYPE)
        out_refs[2][...] = silu.astype(MXU_DTYPE)

    col = lambda i, j, k: (i, j)
    wspec = lambda i, j, k: (j, 0, 0)
    return _fused_matmul(
        "ffn_gate_up", (M // tm, J, 1),
        [(h, (tm, K), lambda i, j, k: (i, 0)), (wg3, (None, K, Nc), wspec), (wu3, (None, K, Nc), wspec)],
        [(0, 1, 0, NN), (0, 2, 1, NN)], [(tm, Nc)] * 2, [],
        [((M, J * Nc), MXU_DTYPE, (tm, Nc), col)] * 3, epilogue)


def _ffn_down(act, wd2d, x2, tm=512, tn=1024):
    M, F = act.shape
    D = wd2d.shape[1]
    tm, tn = _tile(M, tm, 16), _tile(D, tn, LANES)

    def epilogue(accs, ex_refs, out_refs, i, j):
        out_refs[0][...] = ex_refs[0][...] + accs[0]

    tile = lambda i, j, k: (i, j)
    return _fused_matmul(
        "ffn_down", (M // tm, D // tn, 1),
        [(act, (tm, F), lambda i, j, k: (i, 0)), (wd2d, (F, tn), lambda i, j, k: (0, j))],
        [(0, 1, 0, NN)], [(tm, tn)], [(x2, (tm, tn), tile)],
        [((M, D), jnp.float32, (tm, tn), tile)], epilogue)[0]


def _final_norm_loss(x3, target, g_row, tm=256):
    S, D = x3.shape
    tm = _tile(S, tm, 16)

    def body(x_ref, t_ref, g_ref, dx_ref, dxl_ref, loss_ref, dg_ref):
        x3v = x_ref[...]
        g = g_ref[...]
        r = _rsqrt_mean_sq(x3v)
        xhat = x3v * r
        diff = xhat * g - t_ref[...]
        dy = diff / D
        dyg = dy * g
        dx = r * (dyg - xhat * jnp.mean(dyg * xhat, axis=-1, keepdims=True))
        dx_ref[...] = dx
        dxl_ref[...] = dx.astype(MXU_DTYPE)

        @pl.when(pl.program_id(0) == 0)
        def _():
            loss_ref[...] = jnp.zeros_like(loss_ref)
            dg_ref[...] = jnp.zeros_like(dg_ref)

        loss_ref[...] += 0.5 * _sum_all(jnp.mean(diff * diff, axis=-1, keepdims=True))
        dg_ref[...] += jnp.sum(dy * xhat, axis=0, keepdims=True)

    row = pl.BlockSpec((tm, D), lambda i: (i, 0))
    vec = pl.BlockSpec((1, D), lambda i: (0, 0))
    return pl.pallas_call(
        body, name="final_norm_loss", grid=(S // tm,), in_specs=[row, row, vec],
        out_specs=[row, row, pl.BlockSpec((1, 1), lambda i: (0, 0)), vec],
        out_shape=[jax.ShapeDtypeStruct((S, D), jnp.float32), jax.ShapeDtypeStruct((S, D), MXU_DTYPE),
                   jax.ShapeDtypeStruct((1, 1), jnp.float32), jax.ShapeDtypeStruct((1, D), jnp.float32)],
        compiler_params=_params(1),
    )(x3, target, g_row)


def _ffn_down_bwd(dx3_low, wd2d, act_by_gate, act_by_up, n_shards, tm=1024):
    M, D = dx3_low.shape
    F = wd2d.shape[0]
    tn = F // n_shards
    tm = _tile(M, tm, 16)

    def epilogue(accs, ex_refs, out_refs, i, j):
        dact = accs[0]
        out_refs[0][...] = (dact * ex_refs[0][...].astype(jnp.float32)).astype(MXU_DTYPE)
        out_refs[1][...] = (dact * ex_refs[1][...].astype(jnp.float32)).astype(MXU_DTYPE)

    col = lambda i, j, k: (i, j)
    return _fused_matmul(
        "ffn_down_bwd", (M // tm, n_shards, 1),
        [(dx3_low, (tm, D), lambda i, j, k: (i, 0)), (wd2d, (tn, D), lambda i, j, k: (j, 0))],
        [(0, 1, 0, NT)], [(tm, tn)],
        [(act_by_gate, (tm, tn), col), (act_by_up, (tm, tn), col)],
        [((M, F), MXU_DTYPE, (tm, tn), col)] * 2, epilogue)


def _matmul_nt_norm_bwd(name, a, b, xin, g_row, dres, low_dtype, tm=256, after=None):
    M = a.shape[0]
    sharded = b.ndim == 3
    N, Kc = b.shape[-2:]
    J = b.shape[0] if sharded else 1
    tm = _tile(M, tm, 16)
    has_low = low_dtype is not None

    def epilogue(accs, ex_refs, out_refs, i, j):
        dx, dg = _rmsnorm_bwd_values(accs[0], ex_refs[0][...], ex_refs[1][...])
        dx = dx + ex_refs[2][...]
        out_refs[0][...] = dx
        if has_low:
            out_refs[1][...] = dx.astype(low_dtype)
        dg_ref = out_refs[-1]

        @pl.when(i == 0)
        def _():
            dg_ref[...] = jnp.zeros_like(dg_ref)

        dg_ref[...] += dg

    row = lambda i, j, k: (i, 0)
    const = lambda i, j, k: (0, 0)
    b_operand = (b, (None, N, Kc), lambda i, j, k: (k, 0, 0)) if sharded else (b, (N, Kc), const)
    outs = [((M, N), jnp.float32, (tm, N), row)]
    if has_low:
        outs.append(((M, N), low_dtype, (tm, N), row))
    outs.append(((1, N), jnp.float32, (1, N), const))
    return _fused_matmul(
        name, (M // tm, 1, J), [(a, (tm, Kc), lambda i, j, k: (i, k)), b_operand],
        [(0, 1, 0, NT)], [(tm, N)],
        [(xin, (tm, N), row), (g_row, (1, N), const), (dres, (tm, N), row)] + _order_after(after),
        outs, epilogue)


def _row_tile(R, C, target_bytes=2 ** 21):
    return _tile(R, max(16, target_bytes // (4 * C)), 16)


def _prefetch_call(body, name, grid, pos, in_specs, out_specs, out_shape, ins, after=None):
    ordered = [] if after is None else [after]
    grid_spec = pltpu.PrefetchScalarGridSpec(
        num_scalar_prefetch=1, grid=grid, in_specs=list(in_specs) + [ANY] * len(ordered), out_specs=out_specs)
    return pl.pallas_call(body, name=name, grid_spec=grid_spec, out_shape=out_shape,
                          compiler_params=_params(len(grid)))(pos, *ins, *ordered)


def _runs(arrays):
    runs = []
    for i, a in enumerate(arrays):
        if runs and arrays[runs[-1][-1]].shape == a.shape:
            runs[-1].append(i)
        else:
            runs.append([i])
    return runs


def _per_run(fn, names, *array_lists):
    out = [None] * len(names)
    for run in _runs(array_lists[0]):
        results = fn("_".join(names[i] for i in run), *[[arrays[i] for i in run] for arrays in array_lists])
        for i, r in zip(run, results):
            out[i] = r
    return out


def _cast_into_slabs(name, xs, dtype, pos, after=None):
    n = len(xs)
    R, C = xs[0].shape
    tr = _row_tile(R, C * n, 2 ** 22)

    def body(pos_ref, *refs):
        for t in range(n):
            refs[len(refs) - n + t][...] = refs[t][...].astype(dtype)

    return _prefetch_call(
        body, "cast_" + name, (R // tr,), pos, [pl.BlockSpec((tr, C), lambda i, pos: (i, 0))] * n,
        [pl.BlockSpec((None, tr, C), lambda i, pos: (pos[0], i, 0))] * n,
        [jax.ShapeDtypeStruct((N_CHIPS, R, C), dtype)] * n, xs, after)


def _chip_sum(name, partials, gots, dtype, pos):
    n = len(partials)
    J, R, C = partials[0].shape
    Rh = R // 2
    tr = _row_tile(Rh, C * n, 2 ** 23)

    def body(pos_ref, *refs):
        for t in range(n):
            refs[2 * n + t][...] = (refs[t][...].astype(jnp.float32) + refs[n + t][...].astype(jnp.float32)).astype(dtype)

    blk = pl.BlockSpec((None, tr, C), lambda j, i, pos: (j, i, 0))
    return _prefetch_call(
        body, "chip_sum_" + name, (J, Rh // tr), pos,
        [pl.BlockSpec((None, None, tr, C), lambda j, i, pos: (j, pos[1], i, 0))] * n + [blk] * n, [blk] * n,
        [jax.ShapeDtypeStruct((J, Rh, C), dtype)] * n, [p.reshape(J, 2, Rh, C) for p in partials] + list(gots))


def _owner_sum(name, chip_sums, others, pos):
    n = len(chip_sums)
    J, Rh, C = chip_sums[0].shape
    n_other = others[0].shape[0]
    tr = _row_tile(Rh, C * (n_other + 1) * n, 2 ** 23)

    def body(pos_ref, *refs):
        for t in range(n):
            acc = refs[t][...].astype(jnp.float32)
            for k in range(n_other):
                acc = acc + refs[n + t][k].astype(jnp.float32)
            refs[2 * n + t][...] = acc

    return _prefetch_call(
        body, "owner_sum_" + name, (Rh // tr,), pos,
        [pl.BlockSpec((None, tr, C), lambda i, pos: (pos[0], i, 0))] * n
        + [pl.BlockSpec((n_other, tr, C), lambda i, pos: (0, i, 0))] * n,
        [pl.BlockSpec((None, tr, C), lambda i, pos: (pos[1], i, 0))] * n,
        [jax.ShapeDtypeStruct((2, Rh, C), jnp.float32)] * n, list(chip_sums) + list(others))


def _adamw_values(w, g, m, v):
    m = ADAM_B1 * m + (1.0 - ADAM_B1) * g
    v = ADAM_B2 * v + (1.0 - ADAM_B2) * (g * g)
    m_hat = m / (1.0 - ADAM_B1 ** ADAM_STEP)
    v_hat = v / (1.0 - ADAM_B2 ** ADAM_STEP)
    delta = -ADAM_LR * (m_hat / (jnp.sqrt(v_hat) + ADAM_EPS) + ADAM_WD * w)
    return delta, m, v


def _adamw(name, ws, gs, ms, vs):
    n = len(ws)
    R, C = ws[0].shape
    tr = _row_tile(R, C * n, 2 ** 21)

    def body(*refs):
        w_refs, g_refs, m_refs, v_refs = (refs[q * n:(q + 1) * n] for q in range(4))
        outs = refs[4 * n:]
        for t in range(n):
            g = g_refs[t][...]
            outs[4 * t][...] = g
            outs[4 * t + 1][...], outs[4 * t + 2][...], outs[4 * t + 3][...] = _adamw_values(
                w_refs[t][...], g, m_refs[t][...], v_refs[t][...])

    blk = pl.BlockSpec((tr, C), lambda i: (i, 0))
    res = pl.pallas_call(
        body, name="adamw_" + name, grid=(R // tr,), in_specs=[blk] * (4 * n), out_specs=[blk] * (4 * n),
        out_shape=[jax.ShapeDtypeStruct((R, C), jnp.float32)] * (4 * n), compiler_params=_params(1),
    )(*ws, *gs, *ms, *vs)
    return [tuple(res[4 * t:4 * t + 4]) for t in range(n)]


def _adamw_small(g_packed, ws, ms, vs, row_counts):
    n = len(ws)

    def body(*refs):
        g_ref = refs[0]
        w_refs, m_refs, v_refs = refs[1:1 + n], refs[1 + n:1 + 2 * n], refs[1 + 2 * n:1 + 3 * n]
        outs = refs[1 + 3 * n:]
        off = 0
        for t in range(n):
            g = g_ref[off:off + row_counts[t], :]
            off += row_counts[t]
            delta, m_new, v_new = _adamw_values(w_refs[t][...], g, m_refs[t][...], v_refs[t][...])
            outs[4 * t][...] = g
            outs[4 * t + 1][...] = delta
            outs[4 * t + 2][...] = m_new
            outs[4 * t + 3][...] = v_new

    out_shape = []
    for t in range(n):
        out_shape += [jax.ShapeDtypeStruct((row_counts[t], LANES), jnp.float32)] * 4
    return pl.pallas_call(
        body, name="adamw_small", in_specs=[VMEM_WHOLE] * (1 + 3 * n), out_specs=[VMEM_WHOLE] * (4 * n),
        out_shape=out_shape, compiler_params=pltpu.CompilerParams(vmem_limit_bytes=VMEM_LIMIT_BYTES),
    )(g_packed, *ws, *ms, *vs)


def _position():
    x, y, c = lax.axis_index("x"), lax.axis_index("y"), lax.axis_index("c")
    other_chips = [(1 - x, y), (x, 1 - y), (1 - x, 1 - y)]
    return x, y, c, other_chips


def _half(rows, which):
    return pl.ds(which * (rows // 2), rows // 2)


def _chunk_index(shape, dtype):
    rows = shape[-2]
    total = math.prod(shape) * jnp.dtype(dtype).itemsize
    n = max(1, min(rows // COPY_ROW_ALIGN, total // COPY_CHUNK_BYTES))
    step = -(-rows // n)
    step = -(-step // COPY_ROW_ALIGN) * COPY_ROW_ALIGN
    lead = (slice(None),) * (len(shape) - 2)
    return [lead + (pl.ds(s, min(step, rows - s)),) for s in range(0, rows, step)]


def _start_in_chunks(make, src, dst, waited_here=True):
    for idx in _chunk_index(src.shape, src.dtype):
        make(src.at[idx], dst.at[idx]).start()
    return make(src, dst) if waited_here else None


def _remote_copier(send_sem, recv_sem, to):
    return lambda src, dst: pltpu.make_async_remote_copy(
        src_ref=src, dst_ref=dst, send_sem=send_sem, recv_sem=recv_sem, device_id=to, device_id_type=MESH)


def _slab_half(ref, chip, which):
    return ref.at[chip, _half(ref.shape[1], which)]


def _slab_quarter(ref, chip, which, q):
    rows = ref.shape[1] // 4
    return ref.at[chip, pl.ds(which * 2 * rows + q * rows, rows)]


def _gather_step(name, arrived=(), ring=(), direct=(), after=None):
    groups = [g[2] for g in arrived] + list(direct) + list(ring)
    per_tensor = [2] * len(arrived) + [3] * len(direct) + [2] * len(ring)
    flat = [a for g in groups for a in g]
    T, G, A = len(flat), len(groups), len(arrived)
    ordered = [] if after is None else [after]
    sems_in = [sem for g in arrived for sem in g[:2]]

    def body(*refs):
        ins = refs[:T]
        first_hop = refs[T:T + 2 * A]
        out_sems = refs[T + 2 * A + len(ordered):T + 2 * A + len(ordered) + 2 * G]
        token = refs[-1]
        x, y, c, other_chips = _position()
        mine = 2 * x + y
        x_nbr, y_nbr = (1 - x, y, c), (x, 1 - y, c)
        from_x, from_y = 2 * (1 - x) + y, 2 * x + (1 - y)
        t0 = 0
        for gi, group in enumerate(groups):
            send, recv = out_sems[2 * gi], out_sems[2 * gi + 1]
            for t in range(len(group)):
                ref = ins[t0 + t]
                own = _slab_half(ref, mine, c)
                if gi < A:
                    send1, recv1 = first_hop[2 * gi], first_hop[2 * gi + 1]
                    for k, (peer, chip) in enumerate(((x_nbr, from_x), (y_nbr, from_y))):
                        cp = _remote_copier(send1.at[2 * t + k], recv1.at[2 * t + k], peer)(own, _slab_half(ref, chip, c))
                        cp.wait_send()
                        cp.wait_recv()
                    for k, (peer, chip) in enumerate(((y_nbr, from_x), (x_nbr, from_y))):
                        piece = _slab_quarter(ref, chip, c, k)
                        _start_in_chunks(_remote_copier(send.at[2 * t + k], recv.at[2 * t + k], peer), piece, piece,
                                         waited_here=False)
                elif gi < A + len(direct):
                    for k, (cx, cy) in enumerate(other_chips):
                        _start_in_chunks(_remote_copier(send.at[3 * t + k], recv.at[3 * t + k], (cx, cy, c)), own, own,
                                         waited_here=False)
                else:
                    for k, peer in enumerate((x_nbr, y_nbr)):
                        _start_in_chunks(_remote_copier(send.at[2 * t + k], recv.at[2 * t + k], peer), own, own,
                                         waited_here=False)
            t0 += len(group)
        token[...] = jnp.zeros_like(token)

    sem_shapes = []
    for group, n in zip(groups, per_tensor):
        sem_shapes += [pltpu.SemaphoreType.DMA((n * len(group),))] * 2
    res = pl.pallas_call(
        body, name=name, in_specs=[HBM_SPEC] * T + [SEM_SPEC] * (2 * A) + [ANY] * len(ordered),
        out_specs=[SEM_SPEC] * (2 * G) + [HBM_SPEC] * T + [VMEM_WHOLE],
        out_shape=sem_shapes + [pltpu.HBM(a.shape, a.dtype) for a in flat] + [jax.ShapeDtypeStruct((8, LANES), jnp.float32)],
        input_output_aliases={t: 2 * G + t for t in range(T)},
        compiler_params=pltpu.CompilerParams(has_side_effects=SIDE_EFFECT),
    )(*[pltpu.with_memory_space_constraint(a, pltpu.HBM) for a in flat], *sems_in, *ordered)
    flights, t0 = [], 2 * G
    for gi, group in enumerate(groups):
        flights.append((res[2 * gi], res[2 * gi + 1], res[t0:t0 + len(group)]))
        t0 += len(group)
    return flights[:A], flights[A:A + len(direct)], flights[A + len(direct):], res[-1]


def _relayed_wait(name, flight, after):
    send_sems, recv_sems, slabs = flight
    T = len(slabs)

    def body(*refs):
        ins = refs[:T]
        send, recv = refs[T:T + 2]
        x, y, c, _ = _position()
        x_nbr, y_nbr = (1 - x, y, c), (x, 1 - y, c)
        from_x, from_y, diagonal = 2 * (1 - x) + y, 2 * x + (1 - y), 2 * (1 - x) + (1 - y)
        for t in range(T):
            for k, (peer, chip) in enumerate(((y_nbr, from_x), (x_nbr, from_y))):
                cp = _remote_copier(send.at[2 * t + k], recv.at[2 * t + k], peer)(
                    _slab_quarter(ins[t], chip, c, k), _slab_quarter(ins[t], diagonal, c, k))
                cp.wait_send()
                cp.wait_recv()

    return pl.pallas_call(
        body, name=name, in_specs=[HBM_SPEC] * T + [SEM_SPEC, SEM_SPEC, ANY], out_specs=[HBM_SPEC] * T,
        out_shape=[pltpu.HBM(a.shape, a.dtype) for a in slabs], input_output_aliases={t: t for t in range(T)},
        compiler_params=pltpu.CompilerParams(has_side_effects=SIDE_EFFECT),
    )(*slabs, send_sems, recv_sems, after)


def _gather_wait(name, flight, after):
    send_sems, recv_sems, slabs = flight
    T = len(slabs)

    def body(*refs):
        ins = refs[:T]
        send, recv = refs[T:T + 2]
        x, y, c, other_chips = _position()
        mine = 2 * x + y
        for t in range(T):
            own = _slab_half(ins[t], mine, c)
            for k, (cx, cy) in enumerate(other_chips):
                landed = _slab_half(ins[t], 2 * cx + cy, c)
                cp = _remote_copier(send.at[3 * t + k], recv.at[3 * t + k], (cx, cy, c))(own, landed)
                cp.wait_send()
                cp.wait_recv()

    return pl.pallas_call(
        body, name=name, in_specs=[HBM_SPEC] * T + [SEM_SPEC, SEM_SPEC, ANY], out_specs=[HBM_SPEC] * T,
        out_shape=[pltpu.HBM(s.shape, s.dtype) for s in slabs], input_output_aliases={t: t for t in range(T)},
        compiler_params=pltpu.CompilerParams(has_side_effects=SIDE_EFFECT),
    )(*slabs, send_sems, recv_sems, after)


def _share_with_sibling(name, slabs):
    T = len(slabs)

    def body(*refs):
        outs = refs[T:2 * T]
        send_sems, recv_sems = refs[2 * T:]
        x, y, c, other_chips = _position()
        sibling = (x, y, 1 - c)
        started = []
        for t in range(T):
            for k, (cx, cy) in enumerate(other_chips):
                landed = _slab_half(outs[t], 2 * cx + cy, c)
                started.append(_start_in_chunks(_remote_copier(send_sems.at[t, k], recv_sems.at[t, k], sibling), landed, landed))
        for t in range(T):
            for k, (cx, cy) in enumerate(other_chips):
                arriving = _slab_half(outs[t], 2 * cx + cy, 1 - c)
                _remote_copier(send_sems.at[t, k], recv_sems.at[t, k], sibling)(arriving, arriving).wait_recv()
        for cp in started:
            cp.wait_send()

    return pl.pallas_call(
        body, name=name, in_specs=[ANY] * T, out_specs=[ANY] * T,
        out_shape=[jax.ShapeDtypeStruct(s.shape, s.dtype) for s in slabs],
        input_output_aliases={t: t for t in range(T)},
        scratch_shapes=[pltpu.SemaphoreType.DMA((T, 3)), pltpu.SemaphoreType.DMA((T, 3))],
    )(*slabs)


def _share_start(name, slabs):
    T = len(slabs)

    def body(*refs):
        ins = refs[:T]
        send_sems, recv_sems = refs[T:T + 2]
        token = refs[-1]
        x, y, c, other_chips = _position()
        for t in range(T):
            for k, (cx, cy) in enumerate(other_chips):
                landed = _slab_half(ins[t], 2 * cx + cy, c)
                _start_in_chunks(_remote_copier(send_sems.at[3 * t + k], recv_sems.at[3 * t + k], (x, y, 1 - c)),
                                 landed, landed, waited_here=False)
        token[...] = jnp.zeros_like(token)

    res = pl.pallas_call(
        body, name=name, in_specs=[HBM_SPEC] * T,
        out_specs=[SEM_SPEC, SEM_SPEC] + [HBM_SPEC] * T + [VMEM_WHOLE],
        out_shape=[pltpu.SemaphoreType.DMA((3 * T,)), pltpu.SemaphoreType.DMA((3 * T,))]
        + [pltpu.HBM(a.shape, a.dtype) for a in slabs] + [jax.ShapeDtypeStruct((8, LANES), jnp.float32)],
        input_output_aliases={t: 2 + t for t in range(T)},
        compiler_params=pltpu.CompilerParams(has_side_effects=SIDE_EFFECT),
    )(*[pltpu.with_memory_space_constraint(a, pltpu.HBM) for a in slabs])
    return res[0], res[1], res[2:2 + T], res[-1]


def _share_wait(name, flight, after):
    send_sems, recv_sems, slabs, _ = flight
    T = len(slabs)

    def body(*refs):
        ins = refs[:T]
        send, recv = refs[T:T + 2]
        x, y, c, other_chips = _position()
        for t in range(T):
            for k, (cx, cy) in enumerate(other_chips):
                cp = _remote_copier(send.at[3 * t + k], recv.at[3 * t + k], (x, y, 1 - c))(
                    _slab_half(ins[t], 2 * cx + cy, c), _slab_half(ins[t], 2 * cx + cy, 1 - c))
                cp.wait_send()
                cp.wait_recv()

    return pl.pallas_call(
        body, name=name, in_specs=[HBM_SPEC] * T + [SEM_SPEC, SEM_SPEC, ANY], out_specs=[HBM_SPEC] * T,
        out_shape=[pltpu.HBM(a.shape, a.dtype) for a in slabs], input_output_aliases={t: t for t in range(T)},
        compiler_params=pltpu.CompilerParams(has_side_effects=SIDE_EFFECT),
    )(*slabs, send_sems, recv_sems, after)


def _swap_halves(name, grads):
    T = len(grads)

    def body(*refs):
        ins, got = refs[:T], refs[T:2 * T]
        send_sems, recv_sems = refs[2 * T:]
        x, y, c, _ = _position()
        remote = []
        for t in range(T):
            rows = ins[t].shape[1]
            remote.append(_start_in_chunks(
                _remote_copier(send_sems.at[t], recv_sems.at[t], (x, y, 1 - c)), ins[t].at[:, _half(rows, 1 - c)], got[t]))
        for cp in remote:
            cp.wait()

    return pl.pallas_call(
        body, name=name, in_specs=[ANY] * T, out_specs=[ANY] * T,
        out_shape=[jax.ShapeDtypeStruct((g.shape[0], g.shape[1] // 2, g.shape[2]), g.dtype) for g in grads],
        scratch_shapes=[pltpu.SemaphoreType.DMA((T,)), pltpu.SemaphoreType.DMA((T,))],
    )(*grads)


def _swap_start(name, grads):
    T = len(grads)
    lands = [lax.empty((g.shape[0], g.shape[1] // 2, g.shape[2]), g.dtype) for g in grads]

    def body(*refs):
        ins, got = refs[:T], refs[T:2 * T]
        send_sems, recv_sems = refs[2 * T:2 * T + 2]
        token = refs[-1]
        x, y, c, _ = _position()
        for t in range(T):
            _start_in_chunks(_remote_copier(send_sems.at[t], recv_sems.at[t], (x, y, 1 - c)),
                             ins[t].at[:, _half(ins[t].shape[1], 1 - c)], got[t], waited_here=False)
        token[...] = jnp.zeros_like(token)

    arrays = list(grads) + lands
    res = pl.pallas_call(
        body, name=name, in_specs=[HBM_SPEC] * (2 * T),
        out_specs=[SEM_SPEC, SEM_SPEC] + [HBM_SPEC] * (2 * T) + [VMEM_WHOLE],
        out_shape=[pltpu.SemaphoreType.DMA((T,)), pltpu.SemaphoreType.DMA((T,))]
        + [pltpu.HBM(a.shape, a.dtype) for a in arrays] + [jax.ShapeDtypeStruct((8, LANES), jnp.float32)],
        input_output_aliases={i: 2 + i for i in range(2 * T)},
        compiler_params=pltpu.CompilerParams(has_side_effects=SIDE_EFFECT),
    )(*[pltpu.with_memory_space_constraint(a, pltpu.HBM) for a in arrays])
    return res[0], res[1], res[2:2 + T], res[2 + T:2 + 2 * T], res[-1]


def _swap_wait(name, flight, after):
    send_sems, recv_sems, grads, lands, _ = flight
    T = len(grads)

    def body(*refs):
        ins, got = refs[:T], refs[T:2 * T]
        send, recv = refs[2 * T:2 * T + 2]
        x, y, c, _ = _position()
        for t in range(T):
            cp = _remote_copier(send.at[t], recv.at[t], (x, y, 1 - c))(ins[t].at[:, _half(ins[t].shape[1], 1 - c)], got[t])
            cp.wait_send()
            cp.wait_recv()

    arrays = list(grads) + list(lands)
    res = pl.pallas_call(
        body, name=name, in_specs=[HBM_SPEC] * (2 * T) + [SEM_SPEC, SEM_SPEC, ANY], out_specs=[HBM_SPEC] * (2 * T),
        out_shape=[pltpu.HBM(a.shape, a.dtype) for a in arrays], input_output_aliases={i: i for i in range(2 * T)},
        compiler_params=pltpu.CompilerParams(has_side_effects=SIDE_EFFECT),
    )(*arrays, send_sems, recv_sems, after)
    return res[:T], res[T:]


def _scatter_start(name, chip_sums):
    T = len(chip_sums)
    lands = [lax.empty((3,) + s.shape[1:], s.dtype) for s in chip_sums]

    def body(*refs):
        srcs, dsts = refs[:T], refs[T:2 * T]
        send_sems, recv_sems = refs[2 * T:2 * T + 2]
        token = refs[-1]
        x, y, c, other_chips = _position()
        for t in range(T):
            for k, (cx, cy) in enumerate(other_chips):
                _start_in_chunks(_remote_copier(send_sems.at[3 * t + k], recv_sems.at[3 * t + k], (cx, cy, c)),
                                 srcs[t].at[2 * cx + cy], dsts[t].at[k], waited_here=False)
        token[...] = jnp.zeros_like(token)

    hbm = [pltpu.HBM(a.shape, a.dtype) for a in list(chip_sums) + lands]
    res = pl.pallas_call(
        body, name=name, in_specs=[HBM_SPEC] * (2 * T),
        out_specs=[SEM_SPEC, SEM_SPEC] + [HBM_SPEC] * (2 * T) + [VMEM_WHOLE],
        out_shape=[pltpu.SemaphoreType.DMA((3 * T,)), pltpu.SemaphoreType.DMA((3 * T,))] + hbm
        + [jax.ShapeDtypeStruct((8, LANES), jnp.float32)],
        input_output_aliases={i: 2 + i for i in range(2 * T)},
        compiler_params=pltpu.CompilerParams(has_side_effects=SIDE_EFFECT),
    )(*[pltpu.with_memory_space_constraint(a, pltpu.HBM) for a in list(chip_sums) + lands])
    return res[0], res[1], res[2:2 + T], res[2 + T:2 + 2 * T], res[-1]


def _scatter_wait(name, send_sems, recv_sems, chip_sums, lands, after):
    T = len(chip_sums)

    def body(*refs):
        srcs, dsts = refs[:T], refs[T:2 * T]
        send, recv = refs[2 * T:2 * T + 2]
        x, y, c, other_chips = _position()
        for t in range(T):
            for k, (cx, cy) in enumerate(other_chips):
                cp = _remote_copier(send.at[3 * t + k], recv.at[3 * t + k], (cx, cy, c))(srcs[t].at[2 * cx + cy], dsts[t].at[k])
                cp.wait_send()
                cp.wait_recv()

    arrays = list(chip_sums) + list(lands)
    res = pl.pallas_call(
        body, name=name, in_specs=[HBM_SPEC] * (2 * T) + [SEM_SPEC, SEM_SPEC, ANY],
        out_specs=[HBM_SPEC] * (2 * T), out_shape=[pltpu.HBM(a.shape, a.dtype) for a in arrays],
        input_output_aliases={i: i for i in range(2 * T)},
        compiler_params=pltpu.CompilerParams(has_side_effects=SIDE_EFFECT),
    )(*arrays, send_sems, recv_sems, after)
    return res[:T], res[T:]


def _as_shards(partial):
    return [p.reshape((N_CHIPS, -1, p.shape[-1])) for p in partial]


def _reduce_begin(tag, names, partial, pos):
    parts = _as_shards(partial)
    received = _swap_halves("swap_grad_halves_" + tag, parts)
    return _reduce_scatter_start(tag, names, parts, received, pos)


def _reduce_scatter_start(tag, names, parts, received, pos):
    chip_sums = _per_run(lambda name, p, r: _chip_sum(name, p, r, WIRE_DTYPE, pos), names, list(parts), list(received))
    return _scatter_start("scatter_start_" + tag, chip_sums)


def _reduce_end(tag, names, flight, after, pos):
    send_sems, recv_sems, chip_sums, lands, _ = flight
    chip_sums, others = _scatter_wait("scatter_wait_" + tag, send_sems, recv_sems, chip_sums, lands, after)
    reduced = _per_run(lambda name, cs, ot: _owner_sum(name, cs, ot, pos), names, list(chip_sums), list(others))
    return dict(zip(names, reduced))


def _join_start(name, halves):
    T = len(halves)

    def body(*refs):
        ins = refs[:T]
        send_sems, recv_sems = refs[T:T + 2]
        token = refs[-1]
        x, y, c, _ = _position()
        for t in range(T):
            _start_in_chunks(_remote_copier(send_sems.at[t], recv_sems.at[t], (x, y, 1 - c)),
                             ins[t].at[c], ins[t].at[c], waited_here=False)
        token[...] = jnp.zeros_like(token)

    res = pl.pallas_call(
        body, name=name, in_specs=[HBM_SPEC] * T,
        out_specs=[SEM_SPEC, SEM_SPEC] + [HBM_SPEC] * T + [VMEM_WHOLE],
        out_shape=[pltpu.SemaphoreType.DMA((T,)), pltpu.SemaphoreType.DMA((T,))]
        + [pltpu.HBM(h.shape, h.dtype) for h in halves] + [jax.ShapeDtypeStruct((8, LANES), jnp.float32)],
        input_output_aliases={t: 2 + t for t in range(T)},
        compiler_params=pltpu.CompilerParams(has_side_effects=SIDE_EFFECT),
    )(*[pltpu.with_memory_space_constraint(h, pltpu.HBM) for h in halves])
    return res[0], res[1], res[2:2 + T], res[-1]


def _join_wait(name, flight, after):
    send_sems, recv_sems, halves, _ = flight
    T = len(halves)

    def body(*refs):
        ins = refs[:T]
        send, recv = refs[T:T + 2]
        x, y, c, _ = _position()
        for t in range(T):
            cp = _remote_copier(send.at[t], recv.at[t], (x, y, 1 - c))(ins[t].at[c], ins[t].at[1 - c])
            cp.wait_send()
            cp.wait_recv()

    return pl.pallas_call(
        body, name=name, in_specs=[HBM_SPEC] * T + [SEM_SPEC, SEM_SPEC, ANY], out_specs=[HBM_SPEC] * T,
        out_shape=[pltpu.HBM(h.shape, h.dtype) for h in halves], input_output_aliases={t: t for t in range(T)},
        compiler_params=pltpu.CompilerParams(has_side_effects=SIDE_EFFECT),
    )(*halves, send_sems, recv_sems, after)


def _peers():
    x, y, c, _ = _position()
    flips = [(fx, fy, fc) for fx in (0, 1) for fy in (0, 1) for fc in (0, 1)][1:]
    return [(jnp.bitwise_xor(x, fx), jnp.bitwise_xor(y, fy), jnp.bitwise_xor(c, fc)) for fx, fy, fc in flips]


def _device_index(px, py, pc):
    return 4 * px + 2 * py + pc


def _exchange_start(name, slots):
    def body(slots_ref, send_sems, recv_sems, thru, token):
        x, y, c, _ = _position()
        own = slots_ref.at[_device_index(x, y, c)]
        for k, peer in enumerate(_peers()):
            _remote_copier(send_sems.at[k], recv_sems.at[k], peer)(own, own).start()
        token[...] = jnp.zeros_like(token)

    res = pl.pallas_call(
        body, name=name, in_specs=[HBM_SPEC], out_specs=[SEM_SPEC, SEM_SPEC, HBM_SPEC, VMEM_WHOLE],
        out_shape=[pltpu.SemaphoreType.DMA((N_DEV - 1,)), pltpu.SemaphoreType.DMA((N_DEV - 1,)),
                   pltpu.HBM(slots.shape, slots.dtype), jax.ShapeDtypeStruct((8, LANES), jnp.float32)],
        input_output_aliases={0: 2}, compiler_params=pltpu.CompilerParams(has_side_effects=SIDE_EFFECT),
    )(pltpu.with_memory_space_constraint(slots, pltpu.HBM))
    return res[0], res[1], res[2], res[3]


def _exchange_wait(name, flight, after):
    send_sems, recv_sems, slots, _ = flight

    def body(slots_ref, send, recv, after_ref, out_ref):
        x, y, c, _ = _position()
        own = slots_ref.at[_device_index(x, y, c)]
        for k, peer in enumerate(_peers()):
            cp = _remote_copier(send.at[k], recv.at[k], peer)(own, slots_ref.at[_device_index(*peer)])
            cp.wait_send()
            cp.wait_recv()

    return pl.pallas_call(
        body, name=name, in_specs=[HBM_SPEC, SEM_SPEC, SEM_SPEC, ANY], out_specs=HBM_SPEC,
        out_shape=pltpu.HBM(slots.shape, slots.dtype), input_output_aliases={0: 0},
        compiler_params=pltpu.CompilerParams(has_side_effects=SIDE_EFFECT),
    )(slots, send_sems, recv_sems, after)


def _sum_slots(name, slots):
    n, rows, lanes = slots.shape
    tr = rows if slots.size * 4 <= 2 ** 23 else _row_tile(rows, lanes * n)

    def body(s_ref, o_ref):
        acc = s_ref[0]
        for d in range(1, n):
            acc = acc + s_ref[d]
        o_ref[...] = acc

    return pl.pallas_call(
        body, name=name, grid=(rows // tr,), in_specs=[pl.BlockSpec((n, tr, lanes), lambda i: (0, i, 0))],
        out_specs=pl.BlockSpec((tr, lanes), lambda i: (i, 0)), out_shape=jax.ShapeDtypeStruct((rows, lanes), jnp.float32),
        compiler_params=_params(1))(slots)


BIG = ("w_in", "pool_w", "w_out", "w_q", "w_k", "w_v", "w_o", "w_gate", "w_up", "w_down")
SMALL = ("norm_mix_g", "pool_scale", "sgu_norm_g", "w_spatial", "b_spatial", "norm_xattn_g", "norm_mem_g",
         "norm_ffn_g", "final_norm_g")
GATHER_ORDER = (("w_in", "pool_w"), ("w_out",), ("w_q", "w_k", "w_v", "w_o"), ("w_gate", "w_up"), ("w_down",))
WEIGHTS = ("norm_mix_g", "w_in", "pool_w", "pool_scale", "sgu_norm_g", "w_spatial", "b_spatial", "w_out",
           "norm_xattn_g", "norm_mem_g", "w_q", "w_k", "w_v", "w_o", "norm_ffn_g", "w_gate", "w_up", "w_down",
           "final_norm_g")


def _as2d(a):
    return a.reshape(-1, a.shape[-1])


def _as_lanes(a):
    return a.reshape(-1, LANES)


def kernel(x, mem, norm_mix_g, w_in, pool_w, pool_scale, sgu_norm_g, w_spatial, b_spatial, w_out, norm_xattn_g, norm_mem_g, w_q, w_k, w_v, w_o, norm_ffn_g, w_gate, w_up, w_down, final_norm_g, loss_target, m_norm_mix_g, m_w_in, m_pool_w, m_pool_scale, m_sgu_norm_g, m_w_spatial, m_b_spatial, m_w_out, m_norm_xattn_g, m_norm_mem_g, m_w_q, m_w_k, m_w_v, m_w_o, m_norm_ffn_g, m_w_gate, m_w_up, m_w_down, m_final_norm_g, v_norm_mix_g, v_w_in, v_pool_w, v_pool_scale, v_sgu_norm_g, v_w_spatial, v_b_spatial, v_w_out, v_norm_xattn_g, v_norm_mem_g, v_w_q, v_w_k, v_w_v, v_w_o, v_norm_ffn_g, v_w_gate, v_w_up, v_w_down, v_final_norm_g):
    args = dict(locals())
    weights = {n: args[n] for n in WEIGHTS}
    moments_m = {n: args["m_" + n] for n in WEIGHTS}
    moments_v = {n: args["v_" + n] for n in WEIGHTS}
    low = MXU_DTYPE

    xs = x[0]
    mems = mem[0]
    target = loss_target[0]
    D = xs.shape[1]
    G, PG = pool_w.shape[1], pool_w.shape[3]

    pos = jnp.stack([2 * lax.axis_index("x") + lax.axis_index("y"), lax.axis_index("c")]).astype(jnp.int32)
    def cast(names, after=None):
        slabs = _per_run(lambda name, xs: _cast_into_slabs(name, xs, WIRE_DTYPE, pos, after), names,
                         [_as2d(weights[n]) for n in names])
        return dict(zip(names, slabs))

    def land(gi, flight, after, relayed=True):
        wait = _relayed_wait if relayed else _gather_wait
        landed = wait("gather_wait_%d" % gi, flight, after)
        return dict(zip(GATHER_ORDER[gi], _share_with_sibling("gather_share_%d" % gi, landed)))

    def rows(a):
        return a.reshape(-1, a.shape[-1])

    g_mix, g_xattn, g_mem, g_ffn = norm_mix_g, norm_xattn_g, norm_mem_g, norm_ffn_g
    g_final = final_norm_g.reshape(1, D)
    ws = w_spatial[0]
    b_full = jnp.broadcast_to(b_spatial[0][:, :, None], ws.shape[:2] + (sgu_norm_g.shape[1] // ws.shape[0],))

    first = cast(GATHER_ORDER[0])
    _, (sent_0,), _, started = _gather_step("gather_step_0", direct=[[first[n] for n in GATHER_ORDER[0]]])
    rest = cast([n for group in GATHER_ORDER[1:3] for n in group], after=started)
    _, (sent_1,), (hop_2,), started = _gather_step(
        "gather_step_1", direct=[[rest[n] for n in GATHER_ORDER[1]]], ring=[[rest[n] for n in GATHER_ORDER[2]]])
    h1 = _rmsnorm("norm_mix", xs, g_mix, low, after=started)
    mn = _rmsnorm("norm_mem", mems, g_mem, low)
    rest.update(cast(GATHER_ORDER[3], after=h1))
    rest.update(cast(GATHER_ORDER[4], after=rest[GATHER_ORDER[3][-1]]))
    slabs = [[rest[n] for n in group] for group in GATHER_ORDER[1:]]
    got = land(0, sent_0, rest["w_down"], relayed=False)
    wg_in = got["w_in"]
    pw = got["pool_w"].reshape(N_CHIPS, G, PG // N_CHIPS, PG).transpose(1, 0, 2, 3).reshape(G, PG, PG)
    proj = _matmul_nn_cols("proj_in", h1, wg_in, jnp.float32)
    z = _mixer_fwd(proj, pw, pool_scale, sgu_norm_g, ws, b_full)
    full = {"w_out": rows(land(1, sent_1, z, relayed=False)["w_out"])}
    (relay_2,), _, (hop_3,), started = _gather_step("gather_step_2", arrived=[hop_2], ring=slabs[2:3], after=full["w_out"])
    x1, h2 = _proj_residual_norm("mix_out", z, full["w_out"], xs, g_xattn, after=started)
    landed = _relayed_wait("gather_wait_2", relay_2, h2)
    full["w_q"] = rows(_share_with_sibling("gather_share_2_q", landed[:1])[0])
    sharing = _share_start("gather_share_2_start", landed[1:])
    q = _matmul_nn("xattn_q", h2, full["w_q"], low, after=sharing[-1])
    full.update({n: rows(a) for n, a in zip(GATHER_ORDER[2][1:], _share_wait("gather_share_2_wait", sharing, q))})
    k = _matmul_nn("xattn_k", mn, full["w_k"], low)
    vv = _matmul_nn("xattn_v", mn, full["w_v"], low)
    o = _attn_fwd(q, k, vv)
    (relay_3,), (sent_4,), _, started = _gather_step("gather_step_3", arrived=[hop_3], direct=slabs[3:4], after=o)
    x2, h3 = _proj_residual_norm("xattn_out", o, full["w_o"], x1, g_ffn, after=started)
    got = land(3, relay_3, h3)
    wg_gate, wg_up = got["w_gate"], got["w_up"]
    act, act_by_gate, act_by_up = _ffn_gate_up(h3, wg_gate, wg_up)
    full["w_down"] = rows(land(4, sent_4, act, relayed=False)["w_down"])
    x3 = _ffn_down(act, full["w_down"], x2)
    dx3, dx3_low, loss_part, d_final_g = _final_norm_loss(x3, target, g_final)

    dgate, dup = _ffn_down_bwd(dx3_low, full["w_down"], act_by_gate, act_by_up, N_CHIPS)
    (d_w_down,) = _matmul_tn("grad_w_down", act, [dx3_low], WIRE_DTYPE)
    (d_w_gate,) = _matmul_tn_cols("grad_w_gate", h3, [dgate], N_CHIPS, WIRE_DTYPE, tka=1024)
    (d_w_up,) = _matmul_tn_cols("grad_w_up", h3, [dup], N_CHIPS, WIRE_DTYPE, tka=1024)
    ffn_names = ("w_gate", "w_up", "w_down")
    ffn_swap = _swap_start("swap_start_ffn", _as_shards([d_w_gate, d_w_up, d_w_down]))
    dh3 = _matmul_nt_cols("ffn_dh", [dgate, dup], [wg_gate, wg_up], jnp.float32, tn=1024, after=ffn_swap[-1])
    ffn_flight = _reduce_scatter_start("ffn", ffn_names, *_swap_wait("swap_wait_ffn", ffn_swap, dh3), pos)
    dx2, dx2_low, d_ffn_g = _rmsnorm_bwd("norm_ffn_bwd", dh3, x2, g_ffn, dx3, low)

    do = _matmul_nt("xattn_do", [dx2_low], [full["w_o"]], low, after=ffn_flight[-1])
    (d_w_o,) = _matmul_tn("grad_w_o", o, [dx2_low], WIRE_DTYPE)
    dq, dk, dvv = _attn_bwd(q, k, vv, do)
    (d_w_q,) = _matmul_tn("grad_w_q", h2, [dq], WIRE_DTYPE)
    dx1, dx1_low, d_xattn_g = _matmul_nt_norm_bwd("xattn_dh_norm_bwd", dq, full["w_q"], x1, g_xattn, dx2, low)
    d_w_k, d_w_v = _matmul_tn("grad_w_kv", mn, [dk, dvv], WIRE_DTYPE)
    dmn = _matmul_nt("xattn_dmem", [dk, dvv], [full["w_k"], full["w_v"]], jnp.float32)
    (d_mem_g,) = _rmsnorm_bwd("norm_mem_bwd", dmn, mems, g_mem, None, None)

    (d_w_out,) = _matmul_tn("grad_w_out", z, [dx1_low], WIRE_DTYPE)
    attn_names = ("w_q", "w_k", "w_v", "w_o", "w_out")
    attn_swap = _swap_start("swap_start_attn", _as_shards([d_w_q, d_w_k, d_w_v, d_w_o, d_w_out]))
    dz = _matmul_nt("mix_dz", [dx1_low], [full["w_out"]], jnp.float32, after=attn_swap[-1])
    attn_flight = _reduce_scatter_start("attn", attn_names, *_swap_wait("swap_wait_attn", attn_swap, dz), pos)
    dproj, d_pw, d_ps, d_gs, d_ws, d_b = _mixer_bwd(proj, dz, pw, pool_scale, sgu_norm_g, ws, b_full)
    (d_w_in,) = _matmul_tn_cols("grad_w_in", h1, [dproj], N_CHIPS, WIRE_DTYPE, tka=1024, after=attn_flight[-1])
    d_pool_w = d_pw.reshape(G, N_CHIPS, PG // N_CHIPS, PG).transpose(1, 0, 2, 3).reshape(N_CHIPS, PG, PG).astype(WIRE_DTYPE)
    mix_names = ("w_in", "pool_w")
    mix_flight = _reduce_begin("mix", mix_names, [d_w_in, d_pool_w], pos)
    grad_x, d_mix_g = _matmul_nt_norm_bwd("mix_dh_norm_bwd", dproj, wg_in, xs, g_mix, dx1, None, tm=512, after=mix_flight[-1])

    small_parts = {
        "norm_mix_g": d_mix_g, "pool_scale": d_ps, "sgu_norm_g": d_gs, "w_spatial": d_ws, "b_spatial": d_b,
        "norm_xattn_g": d_xattn_g, "norm_mem_g": d_mem_g, "norm_ffn_g": d_ffn_g, "final_norm_g": d_final_g,
    }
    row_counts = [small_parts[n].size // LANES for n in SMALL]
    loss_rows = jnp.pad(loss_part, ((0, 7), (0, LANES - 1)))
    packed = jnp.concatenate([_as_lanes(small_parts[n]) for n in SMALL] + [loss_rows], axis=0)
    device = _device_index(lax.axis_index("x"), lax.axis_index("y"), lax.axis_index("c"))
    slots = lax.dynamic_update_slice(jnp.zeros((N_DEV,) + packed.shape, jnp.float32), packed[None], (device, 0, 0))
    small_flight = _exchange_start("small_grads_start", slots)

    groups = (("ffn", ffn_names, ffn_flight), ("attn", attn_names, attn_flight), ("mix", mix_names, mix_flight))
    joins, after = {}, small_flight[-1]
    for tag, names, flight in groups:
        reduced = _reduce_end(tag, names, flight, after, pos)
        joins[tag] = _join_start("join_start_" + tag, [reduced[n] for n in names])
        after = joins[tag][-1]
    small_sum = _sum_slots("small_grads_sum", _exchange_wait("small_grads_wait", small_flight, after))

    out_grad, out_delta, out_m, out_v = {}, {}, {}, {}
    after = small_sum
    for tag, names, _ in groups:
        grads = [g.reshape(-1, g.shape[-1]) for g in _join_wait("join_wait_" + tag, joins[tag], after)]
        updates = _per_run(_adamw, names, [_as2d(weights[n]) for n in names], grads,
                           [_as2d(moments_m[n]) for n in names], [_as2d(moments_v[n]) for n in names])
        for n, update in zip(names, updates):
            shape = weights[n].shape
            out_grad[n], out_delta[n], out_m[n], out_v[n] = (a.reshape(shape) for a in update)
        after = updates[-1][1]
    small_out = _adamw_small(
        small_sum, [_as_lanes(weights[n]) for n in SMALL], [_as_lanes(moments_m[n]) for n in SMALL],
        [_as_lanes(moments_v[n]) for n in SMALL], row_counts)
    for t, n in enumerate(SMALL):
        shape = weights[n].shape
        out_grad[n], out_delta[n], out_m[n], out_v[n] = (a.reshape(shape) for a in small_out[4 * t:4 * t + 4])

    loss = small_sum[sum(row_counts), 0]
    return (loss, grad_x[None], *[out_grad[n] for n in WEIGHTS], *[out_delta[n] for n in WEIGHTS],
            *[out_m[n] for n in WEIGHTS], *[out_v[n] for n in WEIGHTS])
```

```python
import math

import jax
import jax.numpy as jnp
from jax import lax
from jax.experimental import pallas as pl
from jax.experimental.pallas import tpu as pltpu

MXU_DTYPE = jnp.bfloat16
WIRE_DTYPE = jnp.bfloat16

EPS = 1e-6
CHUNK = 64
POOL_WINDOWS = (2, 4, 8, 16)
POOL_HALO = 16
N_XATTN_HEADS = 4
ADAM_LR = 0.001
ADAM_B1 = 0.9
ADAM_B2 = 0.999
ADAM_EPS = 1e-08
ADAM_WD = 0.01
ADAM_STEP = 10

N_CHIPS = 4
N_DEV = 8
LANES = 128
VMEM_LIMIT_BYTES = 56 * 2 ** 20
COPY_CHUNK_BYTES = 512 * 2 ** 10
COPY_ROW_ALIGN = 16

MESH = pl.DeviceIdType.MESH
ANY = pl.BlockSpec(memory_space=pl.ANY)
VMEM_WHOLE = pl.BlockSpec(memory_space=pltpu.VMEM)
HBM_SPEC = pl.BlockSpec(memory_space=pltpu.HBM)
SEM_SPEC = pl.BlockSpec(memory_space=pltpu.SEMAPHORE)
SIDE_EFFECT = pltpu.SideEffectType.DATAFLOW_SIDE_EFFECTING

NN = (((1,), (0,)), ((), ()))
NT = (((1,), (1,)), ((), ()))
TN = (((0,), (0,)), ((), ()))


def _tile(n, target, mult):
    best = None
    for t in range(mult, min(n, target) + 1, mult):
        if n % t == 0:
            best = t
    return n if best is None else best


def _params(n_grid_axes):
    return pltpu.CompilerParams(dimension_semantics=("arbitrary",) * n_grid_axes, vmem_limit_bytes=VMEM_LIMIT_BYTES)


def _rsqrt_mean_sq(x):
    return lax.rsqrt(jnp.mean(x * x, axis=-1, keepdims=True) + EPS)


def _sum_all(x):
    return jnp.sum(jnp.sum(x, axis=1, keepdims=True), axis=0, keepdims=True)


def _sigmoid(x):
    return 0.5 * jnp.tanh(0.5 * x) + 0.5


def _store_epilogue(accs, ex_refs, out_refs, i, j):
    for a, o in zip(accs, out_refs):
        o[...] = a.astype(o.dtype)


def _fused_matmul(name, grid, operands, pairs, acc_shapes, extras, outs, epilogue):
    nk = grid[2]
    n_op, n_ex, n_out, n_acc = len(operands), len(extras), len(outs), len(acc_shapes)
    in_place = nk > 1 and epilogue is _store_epilogue and all(o[1] == jnp.float32 for o in outs)

    def body(*refs):
        op_refs = refs[:n_op]
        ex_refs = refs[n_op:n_op + n_ex]
        out_refs = refs[n_op + n_ex:n_op + n_ex + n_out]
        acc_refs = out_refs if in_place else refs[n_op + n_ex + n_out:]
        i, j, k = pl.program_id(0), pl.program_id(1), pl.program_id(2)
        loaded = {}

        def operand(n):
            if n not in loaded:
                loaded[n] = op_refs[n][...].astype(MXU_DTYPE)
            return loaded[n]

        def product(ia, ib, dims):
            return lax.dot_general(operand(ia), operand(ib), dims, preferred_element_type=jnp.float32)

        def products():
            parts = [None] * n_acc
            for ia, ib, iacc, dims in pairs:
                d = product(ia, ib, dims)
                parts[iacc] = d if parts[iacc] is None else parts[iacc] + d
            return parts

        if nk == 1:
            epilogue(products(), ex_refs, out_refs, i, j)
            return

        @pl.when(k == 0)
        def _():
            for a in range(n_acc):
                acc_refs[a][...] = jnp.zeros_like(acc_refs[a])

        for a, part in enumerate(products()):
            acc_refs[a][...] += part

        if not in_place:
            @pl.when(k == nk - 1)
            def _():
                epilogue([acc_refs[a][...] for a in range(n_acc)], ex_refs, out_refs, i, j)

    arrays = [o[0] for o in operands] + [e[0] for e in extras]
    in_specs = [pl.BlockSpec(o[1], o[2]) for o in operands]
    in_specs += [ANY if e[1] is None else pl.BlockSpec(e[1], e[2]) for e in extras]
    return pl.pallas_call(
        body,
        name=name,
        grid=grid,
        in_specs=in_specs,
        out_specs=[pl.BlockSpec(o[2], o[3]) for o in outs],
        out_shape=[jax.ShapeDtypeStruct(o[0], o[1]) for o in outs],
        scratch_shapes=[pltpu.VMEM(s, jnp.float32) for s in acc_shapes] if nk > 1 and not in_place else [],
        compiler_params=_params(3),
    )(*arrays)


def _matmul_nn(name, a, b, out_dtype, tm=2048, tn=1024, after=None):
    M, K = a.shape
    N = b.shape[1]
    tm, tn = _tile(M, tm, 16), _tile(N, tn, LANES)
    return _fused_matmul(
        name, (M // tm, N // tn, 1),
        [(a, (tm, K), lambda i, j, k: (i, 0)), (b, (K, tn), lambda i, j, k: (0, j))],
        [(0, 1, 0, NN)], [(tm, tn)], _order_after(after),
        [((M, N), out_dtype, (tm, tn), lambda i, j, k: (i, j))], _store_epilogue)[0]


def _matmul_nn_cols(name, a, b3, out_dtype, tm=2048):
    M, K = a.shape
    J, _, Nc = b3.shape
    tm = _tile(M, tm, 16)
    return _fused_matmul(
        name, (M // tm, J, 1),
        [(a, (tm, K), lambda i, j, k: (i, 0)), (b3, (None, K, Nc), lambda i, j, k: (j, 0, 0))],
        [(0, 1, 0, NN)], [(tm, Nc)], [],
        [((M, J * Nc), out_dtype, (tm, Nc), lambda i, j, k: (i, j))], _store_epilogue)[0]


def _order_after(after):
    return [] if after is None else [(after, None, None)]


def _matmul_nt(name, a_list, b_list, out_dtype, tm=2048, tn=1024, after=None):
    M, K = a_list[0].shape
    N = b_list[0].shape[0]
    tm, tn = _tile(M, tm, 16), _tile(N, tn, LANES)
    n = len(a_list)
    operands = [(a, (tm, K), lambda i, j, k: (i, 0)) for a in a_list]
    operands += [(b, (tn, K), lambda i, j, k: (j, 0)) for b in b_list]
    return _fused_matmul(
        name, (M // tm, N // tn, 1), operands,
        [(p, n + p, 0, NT) for p in range(n)], [(tm, tn)], _order_after(after),
        [((M, N), out_dtype, (tm, tn), lambda i, j, k: (i, j))], _store_epilogue)[0]


def _matmul_nt_cols(name, a_list, b3_list, out_dtype, tm=1024, tn=512, after=None):
    M = a_list[0].shape[0]
    J, N, Kc = b3_list[0].shape
    tm, tn = _tile(M, tm, 16), _tile(N, tn, LANES)
    n = len(a_list)
    operands = [(a, (tm, Kc), lambda i, j, k: (i, k)) for a in a_list]
    operands += [(b, (None, tn, Kc), lambda i, j, k: (k, j, 0)) for b in b3_list]
    return _fused_matmul(
        name, (M // tm, N // tn, J), operands,
        [(p, n + p, 0, NT) for p in range(n)], [(tm, tn)], _order_after(after),
        [((M, N), out_dtype, (tm, tn), lambda i, j, k: (i, j))], _store_epilogue)[0]


def _matmul_tn(name, a, b_list, out_dtype, tka=1024, tn=1024, tkm=4096):
    M, Ka = a.shape
    N = b_list[0].shape[1]
    tka, tn, tkm = _tile(Ka, tka, LANES), _tile(N, tn, LANES), _tile(M, tkm, 16)
    n = len(b_list)
    operands = [(a, (tkm, tka), lambda i, j, k: (k, i))]
    operands += [(b, (tkm, tn), lambda i, j, k: (k, j)) for b in b_list]
    return _fused_matmul(
        name, (Ka // tka, N // tn, M // tkm), operands,
        [(0, 1 + p, p, TN) for p in range(n)], [(tka, tn)] * n, [],
        [((Ka, N), out_dtype, (tka, tn), lambda i, j, k: (i, j))] * n, _store_epilogue)


def _matmul_tn_cols(name, a, b_list, n_shards, out_dtype, tka=512, tkm=4096, after=None):
    M, Ka = a.shape
    Nc = b_list[0].shape[1] // n_shards
    tka, tkm = _tile(Ka, tka, LANES), _tile(M, tkm, 16)
    n = len(b_list)
    operands = [(a, (tkm, tka), lambda i, j, k: (k, i))]
    operands += [(b, (tkm, Nc), lambda i, j, k: (k, j)) for b in b_list]
    return _fused_matmul(
        name, (Ka // tka, n_shards, M // tkm), operands,
        [(0, 1 + p, p, TN) for p in range(n)], [(tka, Nc)] * n, _order_after(after),
        [((n_shards, Ka, Nc), out_dtype, (None, tka, Nc), lambda i, j, k: (j, i, 0))] * n, _store_epilogue)


def _rmsnorm(name, x, g_row, out_dtype, tm=512, after=None):
    S, D = x.shape
    tm = _tile(S, tm, 16)

    def body(x_ref, g_ref, *rest):
        o_ref = rest[-1]
        xv = x_ref[...]
        o_ref[...] = (xv * _rsqrt_mean_sq(xv) * g_ref[...]).astype(o_ref.dtype)

    ordered = [] if after is None else [after]
    return pl.pallas_call(
        body, name=name, grid=(S // tm,),
        in_specs=[pl.BlockSpec((tm, D), lambda i: (i, 0)), pl.BlockSpec((1, D), lambda i: (0, 0))] + [ANY] * len(ordered),
        out_specs=pl.BlockSpec((tm, D), lambda i: (i, 0)),
        out_shape=jax.ShapeDtypeStruct((S, D), out_dtype),
        compiler_params=_params(1),
    )(x, g_row, *ordered)


def _rmsnorm_bwd_values(dh, xin, g):
    r = _rsqrt_mean_sq(xin)
    xhat = xin * r
    dhg = dh * g
    dx = r * (dhg - xhat * jnp.mean(dhg * xhat, axis=-1, keepdims=True))
    return dx, jnp.sum(dh * xhat, axis=0, keepdims=True)


def _rmsnorm_bwd(name, dh, xin, g_row, dres, low_dtype, tm=256):
    S, D = xin.shape
    tm = _tile(S, tm, 16)
    has_dx = dres is not None
    has_low = has_dx and low_dtype is not None

    def body(*refs):
        dh_ref, x_ref, g_ref = refs[:3]
        rest = refs[3:]
        dx, dg = _rmsnorm_bwd_values(dh_ref[...], x_ref[...], g_ref[...])
        if has_dx:
            dres_ref, rest = rest[0], rest[1:]
            dx = dx + dres_ref[...]
            rest[0][...] = dx
            if has_low:
                rest[1][...] = dx.astype(low_dtype)
        dg_ref = rest[-1]

        @pl.when(pl.program_id(0) == 0)
        def _():
            dg_ref[...] = jnp.zeros_like(dg_ref)

        dg_ref[...] += dg

    row = pl.BlockSpec((tm, D), lambda i: (i, 0))
    vec = pl.BlockSpec((1, D), lambda i: (0, 0))
    ins, in_specs = [dh, xin, g_row], [row, row, vec]
    out_shape, out_specs = [], []
    if has_dx:
        ins.append(dres)
        in_specs.append(row)
        out_shape.append(jax.ShapeDtypeStruct((S, D), jnp.float32))
        out_specs.append(row)
        if has_low:
            out_shape.append(jax.ShapeDtypeStruct((S, D), low_dtype))
            out_specs.append(row)
    out_shape.append(jax.ShapeDtypeStruct((1, D), jnp.float32))
    out_specs.append(vec)
    return pl.pallas_call(
        body, name=name, grid=(S // tm,), in_specs=in_specs, out_specs=out_specs, out_shape=out_shape,
        compiler_params=_params(1),
    )(*ins)


def _chunk_mask(blk):
    t = lax.broadcasted_iota(jnp.int32, (blk, blk), 0)
    s = lax.broadcasted_iota(jnp.int32, (blk, blk), 1)
    return (s // CHUNK) <= (t // CHUNK)


def _pool_inputs(a, halo, row0, tm, g, pg):
    cols = slice(g * pg, (g + 1) * pg)
    w = POOL_WINDOWS[g]
    s = jnp.concatenate([halo[:, cols], a[:, cols]], axis=0)
    span = 1
    while span < w:
        s = s + pltpu.roll(s, span, axis=0)
        span *= 2
    t = lax.broadcasted_iota(jnp.int32, (tm, 1), 0) + row0
    cnt = jnp.minimum(t + 1, w).astype(jnp.float32)
    return s[POOL_HALO:] / cnt - a[:, cols]


def _mixer_specs(S, tm, DP, DS):
    per_halo = tm // POOL_HALO
    n_halo = S // POOL_HALO
    a_spec = pl.BlockSpec((tm, DP), lambda i: (i, 0))
    prev_spec = pl.BlockSpec((POOL_HALO, DP), lambda i: (jnp.maximum(i * per_halo - 1, 0), 0))
    next_spec = pl.BlockSpec((POOL_HALO, DP), lambda i: (jnp.minimum((i + 1) * per_halo, n_halo - 1), 0))
    u_spec = pl.BlockSpec((tm, DS), lambda i: (i, 1))
    v_spec = pl.BlockSpec((tm, DS), lambda i: (i, 2))
    return a_spec, prev_spec, next_spec, u_spec, v_spec


def _whole(arr):
    nd = arr.ndim
    return pl.BlockSpec(arr.shape, lambda i: (0,) * nd)


def _mixer_fwd(proj, pw, ps_row, gs_row, ws, b_full, tm=512):
    S = proj.shape[0]
    G, PG, _ = pw.shape
    DP = G * PG
    H, BLK, _ = ws.shape
    DS = gs_row.shape[1]
    HD = DS // H
    assert DP == DS and proj.shape[1] == DP + 2 * DS
    tm = _tile(S, tm, BLK)

    def body(a_ref, prev_ref, u_ref, v_ref, pw_ref, ps_ref, gs_ref, ws_ref, b_ref, z_ref):
        i = pl.program_id(0)
        a = a_ref[...]
        halo = jnp.where(i > 0, prev_ref[...], 0.0)
        for g in range(G):
            cols = slice(g * PG, (g + 1) * PG)
            p = _pool_inputs(a, halo, i * tm, tm, g, PG)
            y = jnp.dot(p.astype(MXU_DTYPE), pw_ref[g], preferred_element_type=jnp.float32) * ps_ref[:, cols]
            z_ref[:, cols] = y.astype(z_ref.dtype)
        v = v_ref[...]
        vn = (v * _rsqrt_mean_sq(v) * gs_ref[...]).astype(MXU_DTYPE)
        u = u_ref[...]
        mask = _chunk_mask(BLK)
        for h in range(H):
            wm = jnp.where(mask, ws_ref[h], 0.0).astype(MXU_DTYPE)
            bias = b_ref[h]
            cols = slice(h * HD, (h + 1) * HD)
            for n in range(tm // BLK):
                rows = slice(n * BLK, (n + 1) * BLK)
                mixed = jnp.dot(wm, vn[rows, cols], preferred_element_type=jnp.float32) + bias
                z_ref[rows, DP + h * HD:DP + (h + 1) * HD] = (u[rows, cols] * mixed).astype(z_ref.dtype)

    a_spec, prev_spec, _, u_spec, v_spec = _mixer_specs(S, tm, DP, DS)
    return pl.pallas_call(
        body, name="mixer_fwd", grid=(S // tm,),
        in_specs=[a_spec, prev_spec, u_spec, v_spec, _whole(pw), _whole(ps_row), _whole(gs_row), _whole(ws), _whole(b_full)],
        out_specs=pl.BlockSpec((tm, DP + DS), lambda i: (i, 0)),
        out_shape=jax.ShapeDtypeStruct((S, DP + DS), MXU_DTYPE),
        compiler_params=_params(1),
    )(proj, proj, proj, proj, pw, ps_row, gs_row, ws, b_full)


def _mixer_bwd(proj, dz, pw, ps_row, gs_row, ws, b_full, tm=256):
    S = proj.shape[0]
    G, PG, _ = pw.shape
    DP = G * PG
    H, BLK, _ = ws.shape
    DS = gs_row.shape[1]
    HD = DS // H
    tm = _tile(S, tm, BLK)
    nb = S // tm
    ext = tm + POOL_HALO

    def body(a_ref, prev_ref, u_ref, v_ref, dzp_ref, dzp_next_ref, dzs_ref, pw_ref, ps_ref, gs_ref, ws_ref, b_ref,
             dproj_ref, dpw_ref, dps_ref, dgs_ref, dws_ref, db_ref, dvn_ref):
        i = pl.program_id(0)

        @pl.when(i == 0)
        def _():
            for r in (dpw_ref, dps_ref, dgs_ref, dws_ref, db_ref):
                r[...] = jnp.zeros_like(r)

        a = a_ref[...]
        halo = jnp.where(i > 0, prev_ref[...], 0.0)
        dyp = dzp_ref[...]
        dyp_next = jnp.where(i < nb - 1, dzp_next_ref[...], 0.0)
        dq_ext = jnp.concatenate([dyp, dyp_next], axis=0) * ps_ref[...]
        t_ext = lax.broadcasted_iota(jnp.int32, (ext, 1), 0) + i * tm
        for g in range(G):
            cols = slice(g * PG, (g + 1) * PG)
            w = POOL_WINDOWS[g]
            pb = _pool_inputs(a, halo, i * tm, tm, g, PG).astype(MXU_DTYPE)
            y_pre = jnp.dot(pb, pw_ref[g], preferred_element_type=jnp.float32)
            dps_ref[:, cols] += jnp.sum(dyp[:, cols] * y_pre, axis=0, keepdims=True)
            dqb = dq_ext[:, cols].astype(MXU_DTYPE)
            dpw_ref[g] += lax.dot_general(pb, dqb[:tm], TN, preferred_element_type=jnp.float32)
            dp = lax.dot_general(dqb, pw_ref[g], NT, preferred_element_type=jnp.float32)
            f = dp / jnp.minimum(t_ext + 1, w).astype(jnp.float32)
            span = 1
            while span < w:
                f = f + pltpu.roll(f, ext - span, axis=0)
                span *= 2
            dproj_ref[:, cols] = (f[:tm] - dp[:tm]).astype(dproj_ref.dtype)

        v = v_ref[...]
        rv = _rsqrt_mean_sq(v)
        vhat = v * rv
        gs = gs_ref[...]
        vnb = (vhat * gs).astype(MXU_DTYPE)
        u = u_ref[...]
        dys = dzs_ref[...]
        mask = _chunk_mask(BLK)
        for h in range(H):
            wm = jnp.where(mask, ws_ref[h], 0.0).astype(MXU_DTYPE)
            bias = b_ref[h]
            cols = slice(h * HD, (h + 1) * HD)
            dm_sum = jnp.zeros((BLK, HD), jnp.float32)
            dws_sum = jnp.zeros((BLK, BLK), jnp.float32)
            for n in range(tm // BLK):
                rows = slice(n * BLK, (n + 1) * BLK)
                vblk = vnb[rows, cols]
                mixed = jnp.dot(wm, vblk, preferred_element_type=jnp.float32) + bias
                dy = dys[rows, cols]
                dproj_ref[rows, DP + h * HD:DP + (h + 1) * HD] = (dy * mixed).astype(dproj_ref.dtype)
                dm = dy * u[rows, cols]
                dm_sum = dm_sum + dm
                dmb = dm.astype(MXU_DTYPE)
                dws_sum = dws_sum + lax.dot_general(dmb, vblk, NT, preferred_element_type=jnp.float32)
                dvn_ref[rows, cols] = lax.dot_general(wm, dmb, TN, preferred_element_type=jnp.float32)
            dws_ref[h] += jnp.where(mask, dws_sum, 0.0)
            db_ref[h:h + 1, :] += jnp.sum(dm_sum.T, axis=0, keepdims=True)
        dvn = dvn_ref[...]
        dvg = dvn * gs
        dv = rv * (dvg - vhat * jnp.mean(dvg * vhat, axis=-1, keepdims=True))
        dgs_ref[...] += jnp.sum(dvn * vhat, axis=0, keepdims=True)
        dproj_ref[:, DP + DS:] = dv.astype(dproj_ref.dtype)

    a_spec, prev_spec, next_spec, u_spec, v_spec = _mixer_specs(S, tm, DP, DS)
    dzp_spec = pl.BlockSpec((tm, DP), lambda i: (i, 0))
    dzs_spec = pl.BlockSpec((tm, DS), lambda i: (i, 1))
    small = [pw, ps_row, gs_row, ws, b_full]
    out_shape = [
        jax.ShapeDtypeStruct((S, DP + 2 * DS), MXU_DTYPE),
        jax.ShapeDtypeStruct(pw.shape, jnp.float32),
        jax.ShapeDtypeStruct(ps_row.shape, jnp.float32),
        jax.ShapeDtypeStruct(gs_row.shape, jnp.float32),
        jax.ShapeDtypeStruct(ws.shape, jnp.float32),
        jax.ShapeDtypeStruct((H, BLK), jnp.float32),
    ]
    out_specs = [pl.BlockSpec((tm, DP + 2 * DS), lambda i: (i, 0))] + [_whole(s) for s in out_shape[1:]]
    return pl.pallas_call(
        body, name="mixer_bwd", grid=(nb,),
        in_specs=[a_spec, prev_spec, u_spec, v_spec, dzp_spec, next_spec, dzs_spec] + [_whole(s) for s in small],
        out_specs=out_specs, out_shape=out_shape,
        scratch_shapes=[pltpu.VMEM((tm, DS), jnp.float32)],
        compiler_params=_params(1),
    )(proj, proj, proj, proj, dz, dz, dz, *small)


def _softmax_rows(q, k, scale):
    s = lax.dot_general(q, k, NT, preferred_element_type=jnp.float32) * scale
    e = jnp.exp(s - jnp.max(s, axis=-1, keepdims=True))
    return e / jnp.sum(e, axis=-1, keepdims=True)


def _attn_fwd(q, k, v, tm=512):
    S, D = q.shape
    M = k.shape[0]
    HD = D // N_XATTN_HEADS
    scale = HD ** -0.5
    tm = _tile(S, tm, 16)

    def body(q_ref, k_ref, v_ref, o_ref):
        for h in range(N_XATTN_HEADS):
            cols = slice(h * HD, (h + 1) * HD)
            p = _softmax_rows(q_ref[:, cols], k_ref[:, cols], scale)
            o = jnp.dot(p.astype(MXU_DTYPE), v_ref[:, cols], preferred_element_type=jnp.float32)
            o_ref[:, cols] = o.astype(o_ref.dtype)

    row = pl.BlockSpec((tm, D), lambda i: (i, 0))
    mem = pl.BlockSpec((M, D), lambda i: (0, 0))
    return pl.pallas_call(
        body, name="attn_fwd", grid=(S // tm,), in_specs=[row, mem, mem], out_specs=row,
        out_shape=jax.ShapeDtypeStruct((S, D), MXU_DTYPE), compiler_params=_params(1),
    )(q, k, v)


def _attn_bwd(q, k, v, do, tm=512):
    S, D = q.shape
    M = k.shape[0]
    HD = D // N_XATTN_HEADS
    scale = HD ** -0.5
    tm = _tile(S, tm, 16)

    def body(q_ref, k_ref, v_ref, do_ref, dq_ref, dk_ref, dv_ref):
        @pl.when(pl.program_id(0) == 0)
        def _():
            dk_ref[...] = jnp.zeros_like(dk_ref)
            dv_ref[...] = jnp.zeros_like(dv_ref)

        for h in range(N_XATTN_HEADS):
            cols = slice(h * HD, (h + 1) * HD)
            qh, kh, vh, doh = q_ref[:, cols], k_ref[:, cols], v_ref[:, cols], do_ref[:, cols]
            p = _softmax_rows(qh, kh, scale)
            dp = lax.dot_general(doh, vh, NT, preferred_element_type=jnp.float32)
            dv_ref[:, cols] += lax.dot_general(p.astype(MXU_DTYPE), doh, TN, preferred_element_type=jnp.float32)
            ds = (p * (dp - jnp.sum(dp * p, axis=-1, keepdims=True)) * scale).astype(MXU_DTYPE)
            dq_ref[:, cols] = jnp.dot(ds, kh, preferred_element_type=jnp.float32).astype(dq_ref.dtype)
            dk_ref[:, cols] += lax.dot_general(ds, qh, TN, preferred_element_type=jnp.float32)

    row = pl.BlockSpec((tm, D), lambda i: (i, 0))
    mem = pl.BlockSpec((M, D), lambda i: (0, 0))
    return pl.pallas_call(
        body, name="attn_bwd", grid=(S // tm,), in_specs=[row, mem, mem, row], out_specs=[row, mem, mem],
        out_shape=[jax.ShapeDtypeStruct((S, D), MXU_DTYPE), jax.ShapeDtypeStruct((M, D), jnp.float32),
                   jax.ShapeDtypeStruct((M, D), jnp.float32)],
        compiler_params=_params(1),
    )(q, k, v, do)


def _proj_residual_norm(name, a, w2d, resid, g_row, tm=512, after=None):
    M, K = a.shape
    N = w2d.shape[1]
    tm = _tile(M, tm, 16)

    def epilogue(accs, ex_refs, out_refs, i, j):
        x_new = ex_refs[0][...] + accs[0]
        out_refs[0][...] = x_new
        out_refs[1][...] = (x_new * _rsqrt_mean_sq(x_new) * ex_refs[1][...]).astype(MXU_DTYPE)

    row = lambda i, j, k: (i, 0)
    return _fused_matmul(
        name, (M // tm, 1, 1),
        [(a, (tm, K), row), (w2d, (K, N), lambda i, j, k: (0, 0))],
        [(0, 1, 0, NN)], [(tm, N)],
        [(resid, (tm, N), row), (g_row, (1, N), lambda i, j, k: (0, 0))] + _order_after(after),
        [((M, N), jnp.float32, (tm, N), row), ((M, N), MXU_DTYPE, (tm, N), row)], epilogue)


def _ffn_gate_up(h, wg3, wu3, tm=512):
    M, K = h.shape
    J, _, Nc = wg3.shape
    tm = _tile(M, tm, 16)

    def epilogue(accs, ex_refs, out_refs, i, j):
        gate, up = accs
        sg = _sigmoid(gate)
        silu = gate * sg
        out_refs[0][...] = (silu * up).astype(MXU_DTYPE)
        out_refs[1][...] = (up * (sg * (1.0 + gate * (1.0 - sg)))).astype(MXU_DTYPE)
        out_refs[2][...] = silu.astype(MXU_DTYPE)

    col = lambda i, j, k: (i, j)
    wspec = lambda i, j, k: (j, 0, 0)
    return _fused_matmul(
        "ffn_gate_up", (M // tm, J, 1),
        [(h, (tm, K), lambda i, j, k: (i, 0)), (wg3, (None, K, Nc), wspec), (wu3, (None, K, Nc), wspec)],
        [(0, 1, 0, NN), (0, 2, 1, NN)], [(tm, Nc)] * 2, [],
        [((M, J * Nc), MXU_DTYPE, (tm, Nc), col)] * 3, epilogue)


def _ffn_down(act, wd2d, x2, tm=512, tn=1024):
    M, F = act.shape
    D = wd2d.shape[1]
    tm, tn = _tile(M, tm, 16), _tile(D, tn, LANES)

    def epilogue(accs, ex_refs, out_refs, i, j):
        out_refs[0][...] = ex_refs[0][...] + accs[0]

    tile = lambda i, j, k: (i, j)
    return _fused_matmul(
        "ffn_down", (M // tm, D // tn, 1),
        [(act, (tm, F), lambda i, j, k: (i, 0)), (wd2d, (F, tn), lambda i, j, k: (0, j))],
        [(0, 1, 0, NN)], [(tm, tn)], [(x2, (tm, tn), tile)],
        [((M, D), jnp.float32, (tm, tn), tile)], epilogue)[0]


def _final_norm_loss(x3, target, g_row, tm=256):
    S, D = x3.shape
    tm = _tile(S, tm, 16)

    def body(x_ref, t_ref, g_ref, dx_ref, dxl_ref, loss_ref, dg_ref):
        x3v = x_ref[...]
        g = g_ref[...]
        r = _rsqrt_mean_sq(x3v)
        xhat = x3v * r
        diff = xhat * g - t_ref[...]
        dy = diff / D
        dyg = dy * g
        dx = r * (dyg - xhat * jnp.mean(dyg * xhat, axis=-1, keepdims=True))
        dx_ref[...] = dx
        dxl_ref[...] = dx.astype(MXU_DTYPE)

        @pl.when(pl.program_id(0) == 0)
        def _():
            loss_ref[...] = jnp.zeros_like(loss_ref)
            dg_ref[...] = jnp.zeros_like(dg_ref)

        loss_ref[...] += 0.5 * _sum_all(jnp.mean(diff * diff, axis=-1, keepdims=True))
        dg_ref[...] += jnp.sum(dy * xhat, axis=0, keepdims=True)

    row = pl.BlockSpec((tm, D), lambda i: (i, 0))
    vec = pl.BlockSpec((1, D), lambda i: (0, 0))
    return pl.pallas_call(
        body, name="final_norm_loss", grid=(S // tm,), in_specs=[row, row, vec],
        out_specs=[row, row, pl.BlockSpec((1, 1), lambda i: (0, 0)), vec],
        out_shape=[jax.ShapeDtypeStruct((S, D), jnp.float32), jax.ShapeDtypeStruct((S, D), MXU_DTYPE),
                   jax.ShapeDtypeStruct((1, 1), jnp.float32), jax.ShapeDtypeStruct((1, D), jnp.float32)],
        compiler_params=_params(1),
    )(x3, target, g_row)


def _ffn_down_bwd(dx3_low, wd2d, act_by_gate, act_by_up, n_shards, tm=1024):
    M, D = dx3_low.shape
    F = wd2d.shape[0]
    tn = F // n_shards
    tm = _tile(M, tm, 16)

    def epilogue(accs, ex_refs, out_refs, i, j):
        dact = accs[0]
        out_refs[0][...] = (dact * ex_refs[0][...].astype(jnp.float32)).astype(MXU_DTYPE)
        out_refs[1][...] = (dact * ex_refs[1][...].astype(jnp.float32)).astype(MXU_DTYPE)

    col = lambda i, j, k: (i, j)
    return _fused_matmul(
        "ffn_down_bwd", (M // tm, n_shards, 1),
        [(dx3_low, (tm, D), lambda i, j, k: (i, 0)), (wd2d, (tn, D), lambda i, j, k: (j, 0))],
        [(0, 1, 0, NT)], [(tm, tn)],
        [(act_by_gate, (tm, tn), col), (act_by_up, (tm, tn), col)],
        [((M, F), MXU_DTYPE, (tm, tn), col)] * 2, epilogue)


def _matmul_nt_norm_bwd(name, a, b, xin, g_row, dres, low_dtype, tm=256, after=None):
    M = a.shape[0]
    sharded = b.ndim == 3
    N, Kc = b.shape[-2:]
    J = b.shape[0] if sharded else 1
    tm = _tile(M, tm, 16)
    has_low = low_dtype is not None

    def epilogue(accs, ex_refs, out_refs, i, j):
        dx, dg = _rmsnorm_bwd_values(accs[0], ex_refs[0][...], ex_refs[1][...])
        dx = dx + ex_refs[2][...]
        out_refs[0][...] = dx
        if has_low:
            out_refs[1][...] = dx.astype(low_dtype)
        dg_ref = out_refs[-1]

        @pl.when(i == 0)
        def _():
            dg_ref[...] = jnp.zeros_like(dg_ref)

        dg_ref[...] += dg

    row = lambda i, j, k: (i, 0)
    const = lambda i, j, k: (0, 0)
    b_operand = (b, (None, N, Kc), lambda i, j, k: (k, 0, 0)) if sharded else (b, (N, Kc), const)
    outs = [((M, N), jnp.float32, (tm, N), row)]
    if has_low:
        outs.append(((M, N), low_dtype, (tm, N), row))
    outs.append(((1, N), jnp.float32, (1, N), const))
    return _fused_matmul(
        name, (M // tm, 1, J), [(a, (tm, Kc), lambda i, j, k: (i, k)), b_operand],
        [(0, 1, 0, NT)], [(tm, N)],
        [(xin, (tm, N), row), (g_row, (1, N), const), (dres, (tm, N), row)] + _order_after(after),
        outs, epilogue)


def _row_tile(R, C, target_bytes=2 ** 21):
    return _tile(R, max(16, target_bytes // (4 * C)), 16)


def _prefetch_call(body, name, grid, pos, in_specs, out_specs, out_shape, ins, after=None):
    ordered = [] if after is None else [after]
    grid_spec = pltpu.PrefetchScalarGridSpec(
        num_scalar_prefetch=1, grid=grid, in_specs=list(in_specs) + [ANY] * len(ordered), out_specs=out_specs)
    return pl.pallas_call(body, name=name, grid_spec=grid_spec, out_shape=out_shape,
                          compiler_params=_params(len(grid)))(pos, *ins, *ordered)


def _runs(arrays):
    runs = []
    for i, a in enumerate(arrays):
        if runs and arrays[runs[-1][-1]].shape == a.shape:
            runs[-1].append(i)
        else:
            runs.append([i])
    return runs


def _per_run(fn, names, *array_lists):
    out = [None] * len(names)
    for run in _runs(array_lists[0]):
        results = fn("_".join(names[i] for i in run), *[[arrays[i] for i in run] for arrays in array_lists])
        for i, r in zip(run, results):
            out[i] = r
    return out


def _cast_into_slabs(name, xs, dtype, pos, after=None):
    n = len(xs)
    R, C = xs[0].shape
    tr = _row_tile(R, C * n, 2 ** 22)

    def body(pos_ref, *refs):
        for t in range(n):
            refs[len(refs) - n + t][...] = refs[t][...].astype(dtype)

    return _prefetch_call(
        body, "cast_" + name, (R // tr,), pos, [pl.BlockSpec((tr, C), lambda i, pos: (i, 0))] * n,
        [pl.BlockSpec((None, tr, C), lambda i, pos: (pos[0], i, 0))] * n,
        [jax.ShapeDtypeStruct((N_CHIPS, R, C), dtype)] * n, xs, after)


def _chip_sum(name, partials, gots, dtype, pos):
    n = len(partials)
    J, R, C = partials[0].shape
    Rh = R // 2
    tr = _row_tile(Rh, C * n, 2 ** 23)

    def body(pos_ref, *refs):
        for t in range(n):
            refs[2 * n + t][...] = (refs[t][...].astype(jnp.float32) + refs[n + t][...].astype(jnp.float32)).astype(dtype)

    blk = pl.BlockSpec((None, tr, C), lambda j, i, pos: (j, i, 0))
    return _prefetch_call(
        body, "chip_sum_" + name, (J, Rh // tr), pos,
        [pl.BlockSpec((None, None, tr, C), lambda j, i, pos: (j, pos[1], i, 0))] * n + [blk] * n, [blk] * n,
        [jax.ShapeDtypeStruct((J, Rh, C), dtype)] * n, [p.reshape(J, 2, Rh, C) for p in partials] + list(gots))


def _owner_sum(name, chip_sums, others, pos):
    n = len(chip_sums)
    J, Rh, C = chip_sums[0].shape
    n_other = others[0].shape[0]
    tr = _row_tile(Rh, C * (n_other + 1) * n, 2 ** 23)

    def body(pos_ref, *refs):
        for t in range(n):
            acc = refs[t][...].astype(jnp.float32)
            for k in range(n_other):
                acc = acc + refs[n + t][k].astype(jnp.float32)
            refs[2 * n + t][...] = acc

    return _prefetch_call(
        body, "owner_sum_" + name, (Rh // tr,), pos,
        [pl.BlockSpec((None, tr, C), lambda i, pos: (pos[0], i, 0))] * n
        + [pl.BlockSpec((n_other, tr, C), lambda i, pos: (0, i, 0))] * n,
        [pl.BlockSpec((None, tr, C), lambda i, pos: (pos[1], i, 0))] * n,
        [jax.ShapeDtypeStruct((2, Rh, C), jnp.float32)] * n, list(chip_sums) + list(others))


def _adamw_values(w, g, m, v):
    m = ADAM_B1 * m + (1.0 - ADAM_B1) * g
    v = ADAM_B2 * v + (1.0 - ADAM_B2) * (g * g)
    m_hat = m / (1.0 - ADAM_B1 ** ADAM_STEP)
    v_hat = v / (1.0 - ADAM_B2 ** ADAM_STEP)
    delta = -ADAM_LR * (m_hat / (jnp.sqrt(v_hat) + ADAM_EPS) + ADAM_WD * w)
    return delta, m, v


def _adamw(name, ws, gs, ms, vs):
    n = len(ws)
    R, C = ws[0].shape
    tr = _row_tile(R, C * n, 2 ** 21)

    def body(*refs):
        w_refs, g_refs, m_refs, v_refs = (refs[q * n:(q + 1) * n] for q in range(4))
        outs = refs[4 * n:]
        for t in range(n):
            g = g_refs[t][...]
            outs[4 * t][...] = g
            outs[4 * t + 1][...], outs[4 * t + 2][...], outs[4 * t + 3][...] = _adamw_values(
                w_refs[t][...], g, m_refs[t][...], v_refs[t][...])

    blk = pl.BlockSpec((tr, C), lambda i: (i, 0))
    res = pl.pallas_call(
        body, name="adamw_" + name, grid=(R // tr,), in_specs=[blk] * (4 * n), out_specs=[blk] * (4 * n),
        out_shape=[jax.ShapeDtypeStruct((R, C), jnp.float32)] * (4 * n), compiler_params=_params(1),
    )(*ws, *gs, *ms, *vs)
    return [tuple(res[4 * t:4 * t + 4]) for t in range(n)]


def _adamw_small(g_packed, ws, ms, vs, row_counts):
    n = len(ws)

    def body(*refs):
        g_ref = refs[0]
        w_refs, m_refs, v_refs = refs[1:1 + n], refs[1 + n:1 + 2 * n], refs[1 + 2 * n:1 + 3 * n]
        outs = refs[1 + 3 * n:]
        off = 0
        for t in range(n):
            g = g_ref[off:off + row_counts[t], :]
            off += row_counts[t]
            delta, m_new, v_new = _adamw_values(w_refs[t][...], g, m_refs[t][...], v_refs[t][...])
            outs[4 * t][...] = g
            outs[4 * t + 1][...] = delta
            outs[4 * t + 2][...] = m_new
            outs[4 * t + 3][...] = v_new

    out_shape = []
    for t in range(n):
        out_shape += [jax.ShapeDtypeStruct((row_counts[t], LANES), jnp.float32)] * 4
    return pl.pallas_call(
        body, name="adamw_small", in_specs=[VMEM_WHOLE] * (1 + 3 * n), out_specs=[VMEM_WHOLE] * (4 * n),
        out_shape=out_shape, compiler_params=pltpu.CompilerParams(vmem_limit_bytes=VMEM_LIMIT_BYTES),
    )(g_packed, *ws, *ms, *vs)


def _position():
    x, y, c = lax.axis_index("x"), lax.axis_index("y"), lax.axis_index("c")
    other_chips = [(1 - x, y), (x, 1 - y), (1 - x, 1 - y)]
    return x, y, c, other_chips


def _half(rows, which):
    return pl.ds(which * (rows // 2), rows // 2)


def _chunk_index(shape, dtype):
    rows = shape[-2]
    total = math.prod(shape) * jnp.dtype(dtype).itemsize
    n = max(1, min(rows // COPY_ROW_ALIGN, total // COPY_CHUNK_BYTES))
    step = -(-rows // n)
    step = -(-step // COPY_ROW_ALIGN) * COPY_ROW_ALIGN
    lead = (slice(None),) * (len(shape) - 2)
    return [lead + (pl.ds(s, min(step, rows - s)),) for s in range(0, rows, step)]


def _start_in_chunks(make, src, dst, waited_here=True):
    for idx in _chunk_index(src.shape, src.dtype):
        make(src.at[idx], dst.at[idx]).start()
    return make(src, dst) if waited_here else None


def _remote_copier(send_sem, recv_sem, to):
    return lambda src, dst: pltpu.make_async_remote_copy(
        src_ref=src, dst_ref=dst, send_sem=send_sem, recv_sem=recv_sem, device_id=to, device_id_type=MESH)


def _slab_half(ref, chip, which):
    return ref.at[chip, _half(ref.shape[1], which)]


def _slab_quarter(ref, chip, which, q):
    rows = ref.shape[1] // 4
    return ref.at[chip, pl.ds(which * 2 * rows + q * rows, rows)]


def _gather_step(name, arrived=(), ring=(), direct=(), after=None):
    groups = [g[2] for g in arrived] + list(direct) + list(ring)
    per_tensor = [2] * len(arrived) + [3] * len(direct) + [2] * len(ring)
    flat = [a for g in groups for a in g]
    T, G, A = len(flat), len(groups), len(arrived)
    ordered = [] if after is None else [after]
    sems_in = [sem for g in arrived for sem in g[:2]]

    def body(*refs):
        ins = refs[:T]
        first_hop = refs[T:T + 2 * A]
        out_sems = refs[T + 2 * A + len(ordered):T + 2 * A + len(ordered) + 2 * G]
        token = refs[-1]
        x, y, c, other_chips = _position()
        mine = 2 * x + y
        x_nbr, y_nbr = (1 - x, y, c), (x, 1 - y, c)
        from_x, from_y = 2 * (1 - x) + y, 2 * x + (1 - y)
        t0 = 0
        for gi, group in enumerate(groups):
            send, recv = out_sems[2 * gi], out_sems[2 * gi + 1]
            for t in range(len(group)):
                ref = ins[t0 + t]
                own = _slab_half(ref, mine, c)
                if gi < A:
                    send1, recv1 = first_hop[2 * gi], first_hop[2 * gi + 1]
                    for k, (peer, chip) in enumerate(((x_nbr, from_x), (y_nbr, from_y))):
                        cp = _remote_copier(send1.at[2 * t + k], recv1.at[2 * t + k], peer)(own, _slab_half(ref, chip, c))
                        cp.wait_send()
                        cp.wait_recv()
                    for k, (peer, chip) in enumerate(((y_nbr, from_x), (x_nbr, from_y))):
                        piece = _slab_quarter(ref, chip, c, k)
                        _start_in_chunks(_remote_copier(send.at[2 * t + k], recv.at[2 * t + k], peer), piece, piece,
                                         waited_here=False)
                elif gi < A + len(direct):
                    for k, (cx, cy) in enumerate(other_chips):
                        _start_in_chunks(_remote_copier(send.at[3 * t + k], recv.at[3 * t + k], (cx, cy, c)), own, own,
                                         waited_here=False)
                else:
                    for k, peer in enumerate((x_nbr, y_nbr)):
                        _start_in_chunks(_remote_copier(send.at[2 * t + k], recv.at[2 * t + k], peer), own, own,
                                         waited_here=False)
            t0 += len(group)
        token[...] = jnp.zeros_like(token)

    sem_shapes = []
    for group, n in zip(groups, per_tensor):
        sem_shapes += [pltpu.SemaphoreType.DMA((n * len(group),))] * 2
    res = pl.pallas_call(
        body, name=name, in_specs=[HBM_SPEC] * T + [SEM_SPEC] * (2 * A) + [ANY] * len(ordered),
        out_specs=[SEM_SPEC] * (2 * G) + [HBM_SPEC] * T + [VMEM_WHOLE],
        out_shape=sem_shapes + [pltpu.HBM(a.shape, a.dtype) for a in flat] + [jax.ShapeDtypeStruct((8, LANES), jnp.float32)],
        input_output_aliases={t: 2 * G + t for t in range(T)},
        compiler_params=pltpu.CompilerParams(has_side_effects=SIDE_EFFECT),
    )(*[pltpu.with_memory_space_constraint(a, pltpu.HBM) for a in flat], *sems_in, *ordered)
    flights, t0 = [], 2 * G
    for gi, group in enumerate(groups):
        flights.append((res[2 * gi], res[2 * gi + 1], res[t0:t0 + len(group)]))
        t0 += len(group)
    return flights[:A], flights[A:A + len(direct)], flights[A + len(direct):], res[-1]


def _relayed_wait(name, flight, after):
    send_sems, recv_sems, slabs = flight
    T = len(slabs)

    def body(*refs):
        ins = refs[:T]
        send, recv = refs[T:T + 2]
        x, y, c, _ = _position()
        x_nbr, y_nbr = (1 - x, y, c), (x, 1 - y, c)
        from_x, from_y, diagonal = 2 * (1 - x) + y, 2 * x + (1 - y), 2 * (1 - x) + (1 - y)
        for t in range(T):
            for k, (peer, chip) in enumerate(((y_nbr, from_x), (x_nbr, from_y))):
                cp = _remote_copier(send.at[2 * t + k], recv.at[2 * t + k], peer)(
                    _slab_quarter(ins[t], chip, c, k), _slab_quarter(ins[t], diagonal, c, k))
                cp.wait_send()
                cp.wait_recv()

    return pl.pallas_call(
        body, name=name, in_specs=[HBM_SPEC] * T + [SEM_SPEC, SEM_SPEC, ANY], out_specs=[HBM_SPEC] * T,
        out_shape=[pltpu.HBM(a.shape, a.dtype) for a in slabs], input_output_aliases={t: t for t in range(T)},
        compiler_params=pltpu.CompilerParams(has_side_effects=SIDE_EFFECT),
    )(*slabs, send_sems, recv_sems, after)


def _gather_wait(name, flight, after):
    send_sems, recv_sems, slabs = flight
    T = len(slabs)

    def body(*refs):
        ins = refs[:T]
        send, recv = refs[T:T + 2]
        x, y, c, other_chips = _position()
        mine = 2 * x + y
        for t in range(T):
            own = _slab_half(ins[t], mine, c)
            for k, (cx, cy) in enumerate(other_chips):
                landed = _slab_half(ins[t], 2 * cx + cy, c)
                cp = _remote_copier(send.at[3 * t + k], recv.at[3 * t + k], (cx, cy, c))(own, landed)
                cp.wait_send()
                cp.wait_recv()

    return pl.pallas_call(
        body, name=name, in_specs=[HBM_SPEC] * T + [SEM_SPEC, SEM_SPEC, ANY], out_specs=[HBM_SPEC] * T,
        out_shape=[pltpu.HBM(s.shape, s.dtype) for s in slabs], input_output_aliases={t: t for t in range(T)},
        compiler_params=pltpu.CompilerParams(has_side_effects=SIDE_EFFECT),
    )(*slabs, send_sems, recv_sems, after)


def _share_with_sibling(name, slabs):
    T = len(slabs)

    def body(*refs):
        outs = refs[T:2 * T]
        send_sems, recv_sems = refs[2 * T:]
        x, y, c, other_chips = _position()
        sibling = (x, y, 1 - c)
        started = []
        for t in range(T):
            for k, (cx, cy) in enumerate(other_chips):
                landed = _slab_half(outs[t], 2 * cx + cy, c)
                started.append(_start_in_chunks(_remote_copier(send_sems.at[t, k], recv_sems.at[t, k], sibling), landed, landed))
        for t in range(T):
            for k, (cx, cy) in enumerate(other_chips):
                arriving = _slab_half(outs[t], 2 * cx + cy, 1 - c)
                _remote_copier(send_sems.at[t, k], recv_sems.at[t, k], sibling)(arriving, arriving).wait_recv()
        for cp in started:
            cp.wait_send()

    return pl.pallas_call(
        body, name=name, in_specs=[ANY] * T, out_specs=[ANY] * T,
        out_shape=[jax.ShapeDtypeStruct(s.shape, s.dtype) for s in slabs],
        input_output_aliases={t: t for t in range(T)},
        scratch_shapes=[pltpu.SemaphoreType.DMA((T, 3)), pltpu.SemaphoreType.DMA((T, 3))],
    )(*slabs)


def _share_start(name, slabs):
    T = len(slabs)

    def body(*refs):
        ins = refs[:T]
        send_sems, recv_sems = refs[T:T + 2]
        token = refs[-1]
        x, y, c, other_chips = _position()
        for t in range(T):
            for k, (cx, cy) in enumerate(other_chips):
                landed = _slab_half(ins[t], 2 * cx + cy, c)
                _start_in_chunks(_remote_copier(send_sems.at[3 * t + k], recv_sems.at[3 * t + k], (x, y, 1 - c)),
                                 landed, landed, waited_here=False)
        token[...] = jnp.zeros_like(token)

    res = pl.pallas_call(
        body, name=name, in_specs=[HBM_SPEC] * T,
        out_specs=[SEM_SPEC, SEM_SPEC] + [HBM_SPEC] * T + [VMEM_WHOLE],
        out_shape=[pltpu.SemaphoreType.DMA((3 * T,)), pltpu.SemaphoreType.DMA((3 * T,))]
        + [pltpu.HBM(a.shape, a.dtype) for a in slabs] + [jax.ShapeDtypeStruct((8, LANES), jnp.float32)],
        input_output_aliases={t: 2 + t for t in range(T)},
        compiler_params=pltpu.CompilerParams(has_side_effects=SIDE_EFFECT),
    )(*[pltpu.with_memory_space_constraint(a, pltpu.HBM) for a in slabs])
    return res[0], res[1], res[2:2 + T], res[-1]


def _share_wait(name, flight, after):
    send_sems, recv_sems, slabs, _ = flight
    T = len(slabs)

    def body(*refs):
        ins = refs[:T]
        send, recv = refs[T:T + 2]
        x, y, c, other_chips = _position()
        for t in range(T):
            for k, (cx, cy) in enumerate(other_chips):
                cp = _remote_copier(send.at[3 * t + k], recv.at[3 * t + k], (x, y, 1 - c))(
                    _slab_half(ins[t], 2 * cx + cy, c), _slab_half(ins[t], 2 * cx + cy, 1 - c))
                cp.wait_send()
                cp.wait_recv()

    return pl.pallas_call(
        body, name=name, in_specs=[HBM_SPEC] * T + [SEM_SPEC, SEM_SPEC, ANY], out_specs=[HBM_SPEC] * T,
        out_shape=[pltpu.HBM(a.shape, a.dtype) for a in slabs], input_output_aliases={t: t for t in range(T)},
        compiler_params=pltpu.CompilerParams(has_side_effects=SIDE_EFFECT),
    )(*slabs, send_sems, recv_sems, after)


def _swap_halves(name, grads):
    T = len(grads)

    def body(*refs):
        ins, got = refs[:T], refs[T:2 * T]
        send_sems, recv_sems = refs[2 * T:]
        x, y, c, _ = _position()
        remote = []
        for t in range(T):
            rows = ins[t].shape[1]
            remote.append(_start_in_chunks(
                _remote_copier(send_sems.at[t], recv_sems.at[t], (x, y, 1 - c)), ins[t].at[:, _half(rows, 1 - c)], got[t]))
        for cp in remote:
            cp.wait()

    return pl.pallas_call(
        body, name=name, in_specs=[ANY] * T, out_specs=[ANY] * T,
        out_shape=[jax.ShapeDtypeStruct((g.shape[0], g.shape[1] // 2, g.shape[2]), g.dtype) for g in grads],
        scratch_shapes=[pltpu.SemaphoreType.DMA((T,)), pltpu.SemaphoreType.DMA((T,))],
    )(*grads)


def _swap_start(name, grads):
    T = len(grads)
    lands = [lax.empty((g.shape[0], g.shape[1] // 2, g.shape[2]), g.dtype) for g in grads]

    def body(*refs):
        ins, got = refs[:T], refs[T:2 * T]
        send_sems, recv_sems = refs[2 * T:2 * T + 2]
        token = refs[-1]
        x, y, c, _ = _position()
        for t in range(T):
            _start_in_chunks(_remote_copier(send_sems.at[t], recv_sems.at[t], (x, y, 1 - c)),
                             ins[t].at[:, _half(ins[t].shape[1], 1 - c)], got[t], waited_here=False)
        token[...] = jnp.zeros_like(token)

    arrays = list(grads) + lands
    res = pl.pallas_call(
        body, name=name, in_specs=[HBM_SPEC] * (2 * T),
        out_specs=[SEM_SPEC, SEM_SPEC] + [HBM_SPEC] * (2 * T) + [VMEM_WHOLE],
        out_shape=[pltpu.SemaphoreType.DMA((T,)), pltpu.SemaphoreType.DMA((T,))]
        + [pltpu.HBM(a.shape, a.dtype) for a in arrays] + [jax.ShapeDtypeStruct((8, LANES), jnp.float32)],
        input_output_aliases={i: 2 + i for i in range(2 * T)},
        compiler_params=pltpu.CompilerParams(has_side_effects=SIDE_EFFECT),
    )(*[pltpu.with_memory_space_constraint(a, pltpu.HBM) for a in arrays])
    return res[0], res[1], res[2:2 + T], res[2 + T:2 + 2 * T], res[-1]


def _swap_wait(name, flight, after):
    send_sems, recv_sems, grads, lands, _ = flight
    T = len(grads)

    def body(*refs):
        ins, got = refs[:T], refs[T:2 * T]
        send, recv = refs[2 * T:2 * T + 2]
        x, y, c, _ = _position()
        for t in range(T):
            cp = _remote_copier(send.at[t], recv.at[t], (x, y, 1 - c))(ins[t].at[:, _half(ins[t].shape[1], 1 - c)], got[t])
            cp.wait_send()
            cp.wait_recv()

    arrays = list(grads) + list(lands)
    res = pl.pallas_call(
        body, name=name, in_specs=[HBM_SPEC] * (2 * T) + [SEM_SPEC, SEM_SPEC, ANY], out_specs=[HBM_SPEC] * (2 * T),
        out_shape=[pltpu.HBM(a.shape, a.dtype) for a in arrays], input_output_aliases={i: i for i in range(2 * T)},
        compiler_params=pltpu.CompilerParams(has_side_effects=SIDE_EFFECT),
    )(*arrays, send_sems, recv_sems, after)
    return res[:T], res[T:]


def _scatter_start(name, chip_sums):
    T = len(chip_sums)
    lands = [lax.empty((3,) + s.shape[1:], s.dtype) for s in chip_sums]

    def body(*refs):
        srcs, dsts = refs[:T], refs[T:2 * T]
        send_sems, recv_sems = refs[2 * T:2 * T + 2]
        token = refs[-1]
        x, y, c, other_chips = _position()
        for t in range(T):
            for k, (cx, cy) in enumerate(other_chips):
                _start_in_chunks(_remote_copier(send_sems.at[3 * t + k], recv_sems.at[3 * t + k], (cx, cy, c)),
                                 srcs[t].at[2 * cx + cy], dsts[t].at[k], waited_here=False)
        token[...] = jnp.zeros_like(token)

    hbm = [pltpu.HBM(a.shape, a.dtype) for a in list(chip_sums) + lands]
    res = pl.pallas_call(
        body, name=name, in_specs=[HBM_SPEC] * (2 * T),
        out_specs=[SEM_SPEC, SEM_SPEC] + [HBM_SPEC] * (2 * T) + [VMEM_WHOLE],
        out_shape=[pltpu.SemaphoreType.DMA((3 * T,)), pltpu.SemaphoreType.DMA((3 * T,))] + hbm
        + [jax.ShapeDtypeStruct((8, LANES), jnp.float32)],
        input_output_aliases={i: 2 + i for i in range(2 * T)},
        compiler_params=pltpu.CompilerParams(has_side_effects=SIDE_EFFECT),
    )(*[pltpu.with_memory_space_constraint(a, pltpu.HBM) for a in list(chip_sums) + lands])
    return res[0], res[1], res[2:2 + T], res[2 + T:2 + 2 * T], res[-1]


def _scatter_wait(name, send_sems, recv_sems, chip_sums, lands, after):
    T = len(chip_sums)

    def body(*refs):
        srcs, dsts = refs[:T], refs[T:2 * T]
        send, recv = refs[2 * T:2 * T + 2]
        x, y, c, other_chips = _position()
        for t in range(T):
            for k, (cx, cy) in enumerate(other_chips):
                cp = _remote_copier(send.at[3 * t + k], recv.at[3 * t + k], (cx, cy, c))(srcs[t].at[2 * cx + cy], dsts[t].at[k])
                cp.wait_send()
                cp.wait_recv()

    arrays = list(chip_sums) + list(lands)
    res = pl.pallas_call(
        body, name=name, in_specs=[HBM_SPEC] * (2 * T) + [SEM_SPEC, SEM_SPEC, ANY],
        out_specs=[HBM_SPEC] * (2 * T), out_shape=[pltpu.HBM(a.shape, a.dtype) for a in arrays],
        input_output_aliases={i: i for i in range(2 * T)},
        compiler_params=pltpu.CompilerParams(has_side_effects=SIDE_EFFECT),
    )(*arrays, send_sems, recv_sems, after)
    return res[:T], res[T:]


def _as_shards(partial):
    return [p.reshape((N_CHIPS, -1, p.shape[-1])) for p in partial]


def _reduce_begin(tag, names, partial, pos):
    parts = _as_shards(partial)
    received = _swap_halves("swap_grad_halves_" + tag, parts)
    return _reduce_scatter_start(tag, names, parts, received, pos)


def _reduce_scatter_start(tag, names, parts, received, pos):
    chip_sums = _per_run(lambda name, p, r: _chip_sum(name, p, r, WIRE_DTYPE, pos), names, list(parts), list(received))
    return _scatter_start("scatter_start_" + tag, chip_sums)


def _reduce_end(tag, names, flight, after, pos):
    send_sems, recv_sems, chip_sums, lands, _ = flight
    chip_sums, others = _scatter_wait("scatter_wait_" + tag, send_sems, recv_sems, chip_sums, lands, after)
    reduced = _per_run(lambda name, cs, ot: _owner_sum(name, cs, ot, pos), names, list(chip_sums), list(others))
    return dict(zip(names, reduced))


def _join_start(name, halves):
    T = len(halves)

    def body(*refs):
        ins = refs[:T]
        send_sems, recv_sems = refs[T:T + 2]
        token = refs[-1]
        x, y, c, _ = _position()
        for t in range(T):
            _start_in_chunks(_remote_copier(send_sems.at[t], recv_sems.at[t], (x, y, 1 - c)),
                             ins[t].at[c], ins[t].at[c], waited_here=False)
        token[...] = jnp.zeros_like(token)

    res = pl.pallas_call(
        body, name=name, in_specs=[HBM_SPEC] * T,
        out_specs=[SEM_SPEC, SEM_SPEC] + [HBM_SPEC] * T + [VMEM_WHOLE],
        out_shape=[pltpu.SemaphoreType.DMA((T,)), pltpu.SemaphoreType.DMA((T,))]
        + [pltpu.HBM(h.shape, h.dtype) for h in halves] + [jax.ShapeDtypeStruct((8, LANES), jnp.float32)],
        input_output_aliases={t: 2 + t for t in range(T)},
        compiler_params=pltpu.CompilerParams(has_side_effects=SIDE_EFFECT),
    )(*[pltpu.with_memory_space_constraint(h, pltpu.HBM) for h in halves])
    return res[0], res[1], res[2:2 + T], res[-1]


def _join_wait(name, flight, after):
    send_sems, recv_sems, halves, _ = flight
    T = len(halves)

    def body(*refs):
        ins = refs[:T]
        send, recv = refs[T:T + 2]
        x, y, c, _ = _position()
        for t in range(T):
            cp = _remote_copier(send.at[t], recv.at[t], (x, y, 1 - c))(ins[t].at[c], ins[t].at[1 - c])
            cp.wait_send()
            cp.wait_recv()

    return pl.pallas_call(
        body, name=name, in_specs=[HBM_SPEC] * T + [SEM_SPEC, SEM_SPEC, ANY], out_specs=[HBM_SPEC] * T,
        out_shape=[pltpu.HBM(h.shape, h.dtype) for h in halves], input_output_aliases={t: t for t in range(T)},
        compiler_params=pltpu.CompilerParams(has_side_effects=SIDE_EFFECT),
    )(*halves, send_sems, recv_sems, after)


def _peers():
    x, y, c, _ = _position()
    flips = [(fx, fy, fc) for fx in (0, 1) for fy in (0, 1) for fc in (0, 1)][1:]
    return [(jnp.bitwise_xor(x, fx), jnp.bitwise_xor(y, fy), jnp.bitwise_xor(c, fc)) for fx, fy, fc in flips]


def _device_index(px, py, pc):
    return 4 * px + 2 * py + pc


def _exchange_start(name, slots):
    def body(slots_ref, send_sems, recv_sems, thru, token):
        x, y, c, _ = _position()
        own = slots_ref.at[_device_index(x, y, c)]
        for k, peer in enumerate(_peers()):
            _remote_copier(send_sems.at[k], recv_sems.at[k], peer)(own, own).start()
        token[...] = jnp.zeros_like(token)

    res = pl.pallas_call(
        body, name=name, in_specs=[HBM_SPEC], out_specs=[SEM_SPEC, SEM_SPEC, HBM_SPEC, VMEM_WHOLE],
        out_shape=[pltpu.SemaphoreType.DMA((N_DEV - 1,)), pltpu.SemaphoreType.DMA((N_DEV - 1,)),
                   pltpu.HBM(slots.shape, slots.dtype), jax.ShapeDtypeStruct((8, LANES), jnp.float32)],
        input_output_aliases={0: 2}, compiler_params=pltpu.CompilerParams(has_side_effects=SIDE_EFFECT),
    )(pltpu.with_memory_space_constraint(slots, pltpu.HBM))
    return res[0], res[1], res[2], res[3]


def _exchange_wait(name, flight, after):
    send_sems, recv_sems, slots, _ = flight

    def body(slots_ref, send, recv, after_ref, out_ref):
        x, y, c, _ = _position()
        own = slots_ref.at[_device_index(x, y, c)]
        for k, peer in enumerate(_peers()):
            cp = _remote_copier(send.at[k], recv.at[k], peer)(own, slots_ref.at[_device_index(*peer)])
            cp.wait_send()
            cp.wait_recv()

    return pl.pallas_call(
        body, name=name, in_specs=[HBM_SPEC, SEM_SPEC, SEM_SPEC, ANY], out_specs=HBM_SPEC,
        out_shape=pltpu.HBM(slots.shape, slots.dtype), input_output_aliases={0: 0},
        compiler_params=pltpu.CompilerParams(has_side_effects=SIDE_EFFECT),
    )(slots, send_sems, recv_sems, after)


def _sum_slots(name, slots):
    n, rows, lanes = slots.shape
    tr = rows if slots.size * 4 <= 2 ** 23 else _row_tile(rows, lanes * n)

    def body(s_ref, o_ref):
        acc = s_ref[0]
        for d in range(1, n):
            acc = acc + s_ref[d]
        o_ref[...] = acc

    return pl.pallas_call(
        body, name=name, grid=(rows // tr,), in_specs=[pl.BlockSpec((n, tr, lanes), lambda i: (0, i, 0))],
        out_specs=pl.BlockSpec((tr, lanes), lambda i: (i, 0)), out_shape=jax.ShapeDtypeStruct((rows, lanes), jnp.float32),
        compiler_params=_params(1))(slots)


BIG = ("w_in", "pool_w", "w_out", "w_q", "w_k", "w_v", "w_o", "w_gate", "w_up", "w_down")
SMALL = ("norm_mix_g", "pool_scale", "sgu_norm_g", "w_spatial", "b_spatial", "norm_xattn_g", "norm_mem_g",
         "norm_ffn_g", "final_norm_g")
GATHER_ORDER = (("w_in", "pool_w"), ("w_out",), ("w_q", "w_k", "w_v", "w_o"), ("w_gate", "w_up"), ("w_down",))
WEIGHTS = ("norm_mix_g", "w_in", "pool_w", "pool_scale", "sgu_norm_g", "w_spatial", "b_spatial", "w_out",
           "norm_xattn_g", "norm_mem_g", "w_q", "w_k", "w_v", "w_o", "norm_ffn_g", "w_gate", "w_up", "w_down",
           "final_norm_g")


def _as2d(a):
    return a.reshape(-1, a.shape[-1])


def _as_lanes(a):
    return a.reshape(-1, LANES)


def kernel(x, mem, norm_mix_g, w_in, pool_w, pool_scale, sgu_norm_g, w_spatial, b_spatial, w_out, norm_xattn_g, norm_mem_g, w_q, w_k, w_v, w_o, norm_ffn_g, w_gate, w_up, w_down, final_norm_g, loss_target, m_norm_mix_g, m_w_in, m_pool_w, m_pool_scale, m_sgu_norm_g, m_w_spatial, m_b_spatial, m_w_out, m_norm_xattn_g, m_norm_mem_g, m_w_q, m_w_k, m_w_v, m_w_o, m_norm_ffn_g, m_w_gate, m_w_up, m_w_down, m_final_norm_g, v_norm_mix_g, v_w_in, v_pool_w, v_pool_scale, v_sgu_norm_g, v_w_spatial, v_b_spatial, v_w_out, v_norm_xattn_g, v_norm_mem_g, v_w_q, v_w_k, v_w_v, v_w_o, v_norm_ffn_g, v_w_gate, v_w_up, v_w_down, v_final_norm_g):
    args = dict(locals())
    weights = {n: args[n] for n in WEIGHTS}
    moments_m = {n: args["m_" + n] for n in WEIGHTS}
    moments_v = {n: args["v_" + n] for n in WEIGHTS}
    low = MXU_DTYPE

    xs = x[0]
    mems = mem[0]
    target = loss_target[0]
    D = xs.shape[1]
    G, PG = pool_w.shape[1], pool_w.shape[3]

    pos = jnp.stack([2 * lax.axis_index("x") + lax.axis_index("y"), lax.axis_index("c")]).astype(jnp.int32)
    def cast(names, after=None):
        slabs = _per_run(lambda name, xs: _cast_into_slabs(name, xs, WIRE_DTYPE, pos, after), names,
                         [_as2d(weights[n]) for n in names])
        return dict(zip(names, slabs))

    def land(gi, flight, after, relayed=True):
        wait = _relayed_wait if relayed else _gather_wait
        landed = wait("gather_wait_%d" % gi, flight, after)
        return dict(zip(GATHER_ORDER[gi], _share_with_sibling("gather_share_%d" % gi, landed)))

    def rows(a):
        return a.reshape(-1, a.shape[-1])

    g_mix, g_xattn, g_mem, g_ffn = norm_mix_g, norm_xattn_g, norm_mem_g, norm_ffn_g
    g_final = final_norm_g.reshape(1, D)
    ws = w_spatial[0]
    b_full = jnp.broadcast_to(b_spatial[0][:, :, None], ws.shape[:2] + (sgu_norm_g.shape[1] // ws.shape[0],))

    first = cast(GATHER_ORDER[0])
    _, (sent_0,), _, started = _gather_step("gather_step_0", direct=[[first[n] for n in GATHER_ORDER[0]]])
    rest = cast([n for group in GATHER_ORDER[1:3] for n in group], after=started)
    _, (sent_1,), (hop_2,), started = _gather_step(
        "gather_step_1", direct=[[rest[n] for n in GATHER_ORDER[1]]], ring=[[rest[n] for n in GATHER_ORDER[2]]])
    h1 = _rmsnorm("norm_mix", xs, g_mix, low, after=started)
    mn = _rmsnorm("norm_mem", mems, g_mem, low)
    rest.update(cast(GATHER_ORDER[3], after=h1))
    rest.update(cast(GATHER_ORDER[4], after=rest[GATHER_ORDER[3][-1]]))
    slabs = [[rest[n] for n in group] for group in GATHER_ORDER[1:]]
    got = land(0, sent_0, rest["w_down"], relayed=False)
    wg_in = got["w_in"]
    pw = got["pool_w"].reshape(N_CHIPS, G, PG // N_CHIPS, PG).transpose(1, 0, 2, 3).reshape(G, PG, PG)
    proj = _matmul_nn_cols("proj_in", h1, wg_in, jnp.float32)
    z = _mixer_fwd(proj, pw, pool_scale, sgu_norm_g, ws, b_full)
    full = {"w_out": rows(land(1, sent_1, z, relayed=False)["w_out"])}
    (relay_2,), _, (hop_3,), started = _gather_step("gather_step_2", arrived=[hop_2], ring=slabs[2:3], after=full["w_out"])
    x1, h2 = _proj_residual_norm("mix_out", z, full["w_out"], xs, g_xattn, after=started)
    landed = _relayed_wait("gather_wait_2", relay_2, h2)
    full["w_q"] = rows(_share_with_sibling("gather_share_2_q", landed[:1])[0])
    sharing = _share_start("gather_share_2_start", landed[1:])
    q = _matmul_nn("xattn_q", h2, full["w_q"], low, after=sharing[-1])
    full.update({n: rows(a) for n, a in zip(GATHER_ORDER[2][1:], _share_wait("gather_share_2_wait", sharing, q))})
    k = _matmul_nn("xattn_k", mn, full["w_k"], low)
    vv = _matmul_nn("xattn_v", mn, full["w_v"], low)
    o = _attn_fwd(q, k, vv)
    (relay_3,), (sent_4,), _, started = _gather_step("gather_step_3", arrived=[hop_3], direct=slabs[3:4], after=o)
    x2, h3 = _proj_residual_norm("xattn_out", o, full["w_o"], x1, g_ffn, after=started)
    got = land(3, relay_3, h3)
    wg_gate, wg_up = got["w_gate"], got["w_up"]
    act, act_by_gate, act_by_up = _ffn_gate_up(h3, wg_gate, wg_up)
    full["w_down"] = rows(land(4, sent_4, act, relayed=False)["w_down"])
    x3 = _ffn_down(act, full["w_down"], x2)
    dx3, dx3_low, loss_part, d_final_g = _final_norm_loss(x3, target, g_final)

    dgate, dup = _ffn_down_bwd(dx3_low, full["w_down"], act_by_gate, act_by_up, N_CHIPS)
    (d_w_down,) = _matmul_tn("grad_w_down", act, [dx3_low], WIRE_DTYPE)
    (d_w_gate,) = _matmul_tn_cols("grad_w_gate", h3, [dgate], N_CHIPS, WIRE_DTYPE, tka=1024)
    (d_w_up,) = _matmul_tn_cols("grad_w_up", h3, [dup], N_CHIPS, WIRE_DTYPE, tka=1024)
    ffn_names = ("w_gate", "w_up", "w_down")
    ffn_swap = _swap_start("swap_start_ffn", _as_shards([d_w_gate, d_w_up, d_w_down]))
    dh3 = _matmul_nt_cols("ffn_dh", [dgate, dup], [wg_gate, wg_up], jnp.float32, tn=1024, after=ffn_swap[-1])
    ffn_flight = _reduce_scatter_start("ffn", ffn_names, *_swap_wait("swap_wait_ffn", ffn_swap, dh3), pos)
    dx2, dx2_low, d_ffn_g = _rmsnorm_bwd("norm_ffn_bwd", dh3, x2, g_ffn, dx3, low)

    do = _matmul_nt("xattn_do", [dx2_low], [full["w_o"]], low, after=ffn_flight[-1])
    (d_w_o,) = _matmul_tn("grad_w_o", o, [dx2_low], WIRE_DTYPE)
    dq, dk, dvv = _attn_bwd(q, k, vv, do)
    (d_w_q,) = _matmul_tn("grad_w_q", h2, [dq], WIRE_DTYPE)
    dx1, dx1_low, d_xattn_g = _matmul_nt_norm_bwd("xattn_dh_norm_bwd", dq, full["w_q"], x1, g_xattn, dx2, low)
    d_w_k, d_w_v = _matmul_tn("grad_w_kv", mn, [dk, dvv], WIRE_DTYPE)
    dmn = _matmul_nt("xattn_dmem", [dk, dvv], [full["w_k"], full["w_v"]], jnp.float32)
    (d_mem_g,) = _rmsnorm_bwd("norm_mem_bwd", dmn, mems, g_mem, None, None)

    (d_w_out,) = _matmul_tn("grad_w_out", z, [dx1_low], WIRE_DTYPE)
    attn_names = ("w_q", "w_k", "w_v", "w_o", "w_out")
    attn_swap = _swap_start("swap_start_attn", _as_shards([d_w_q, d_w_k, d_w_v, d_w_o, d_w_out]))
    dz = _matmul_nt("mix_dz", [dx1_low], [full["w_out"]], jnp.float32, after=attn_swap[-1])
    attn_flight = _reduce_scatter_start("attn", attn_names, *_swap_wait("swap_wait_attn", attn_swap, dz), pos)
    dproj, d_pw, d_ps, d_gs, d_ws, d_b = _mixer_bwd(proj, dz, pw, pool_scale, sgu_norm_g, ws, b_full)
    (d_w_in,) = _matmul_tn_cols("grad_w_in", h1, [dproj], N_CHIPS, WIRE_DTYPE, tka=1024, after=attn_flight[-1])
    d_pool_w = d_pw.reshape(G, N_CHIPS, PG // N_CHIPS, PG).transpose(1, 0, 2, 3).reshape(N_CHIPS, PG, PG).astype(WIRE_DTYPE)
    mix_names = ("w_in", "pool_w")
    mix_flight = _reduce_begin("mix", mix_names, [d_w_in, d_pool_w], pos)
    grad_x, d_mix_g = _matmul_nt_norm_bwd("mix_dh_norm_bwd", dproj, wg_in, xs, g_mix, dx1, None, tm=512, after=mix_flight[-1])

    small_parts = {
        "norm_mix_g": d_mix_g, "pool_scale": d_ps, "sgu_norm_g": d_gs, "w_spatial": d_ws, "b_spatial": d_b,
        "norm_xattn_g": d_xattn_g, "norm_mem_g": d_mem_g, "norm_ffn_g": d_ffn_g, "final_norm_g": d_final_g,
    }
    row_counts = [small_parts[n].size // LANES for n in SMALL]
    loss_rows = jnp.pad(loss_part, ((0, 7), (0, LANES - 1)))
    packed = jnp.concatenate([_as_lanes(small_parts[n]) for n in SMALL] + [loss_rows], axis=0)
    device = _device_index(lax.axis_index("x"), lax.axis_index("y"), lax.axis_index("c"))
    slots = lax.dynamic_update_slice(jnp.zeros((N_DEV,) + packed.shape, jnp.float32), packed[None], (device, 0, 0))
    small_flight = _exchange_start("small_grads_start", slots)

    groups = (("ffn", ffn_names, ffn_flight), ("attn", attn_names, attn_flight), ("mix", mix_names, mix_flight))
    joins, after = {}, small_flight[-1]
    for tag, names, flight in groups:
        reduced = _reduce_end(tag, names, flight, after, pos)
        joins[tag] = _join_start("join_start_" + tag, [reduced[n] for n in names])
        after = joins[tag][-1]

    out_grad, out_delta, out_m, out_v = {}, {}, {}, {}
    for tag, names, _ in groups:
        grads = [g.reshape(-1, g.shape[-1]) for g in _join_wait("join_wait_" + tag, joins[tag], after)]
        updates = _per_run(_adamw, names, [_as2d(weights[n]) for n in names], grads,
                           [_as2d(moments_m[n]) for n in names], [_as2d(moments_v[n]) for n in names])
        for n, update in zip(names, updates):
            shape = weights[n].shape
            out_grad[n], out_delta[n], out_m[n], out_v[n] = (a.reshape(shape) for a in update)
        after = updates[-1][1]
    small_sum = _sum_slots("small_grads_sum", _exchange_wait("small_grads_wait", small_flight, after))
    small_out = _adamw_small(
        small_sum, [_as_lanes(weights[n]) for n in SMALL], [_as_lanes(moments_m[n]) for n in SMALL],
        [_as_lanes(moments_v[n]) for n in SMALL], row_counts)
    for t, n in enumerate(SMALL):
        shape = weights[n].shape
        out_grad[n], out_delta[n], out_m[n], out_v[n] = (a.reshape(shape) for a in small_out[4 * t:4 * t + 4])

    loss = small_sum[sum(row_counts), 0]
    return (loss, grad_x[None], *[out_grad[n] for n in WEIGHTS], *[out_delta[n] for n in WEIGHTS],
            *[out_m[n] for n in WEIGHTS], *[out_v[n] for n in WEIGHTS])
```

```python
import math

import jax
import jax.numpy as jnp
from jax import lax
from jax.experimental import pallas as pl
from jax.experimental.pallas import tpu as pltpu

MXU_DTYPE = jnp.bfloat16
WIRE_DTYPE = jnp.bfloat16

EPS = 1e-6
CHUNK = 64
POOL_WINDOWS = (2, 4, 8, 16)
POOL_HALO = 16
N_XATTN_HEADS = 4
ADAM_LR = 0.001
ADAM_B1 = 0.9
ADAM_B2 = 0.999
ADAM_EPS = 1e-08
ADAM_WD = 0.01
ADAM_STEP = 10

N_CHIPS = 4
N_DEV = 8
LANES = 128
VMEM_LIMIT_BYTES = 56 * 2 ** 20
COPY_CHUNK_BYTES = 512 * 2 ** 10
COPY_ROW_ALIGN = 16

MESH = pl.DeviceIdType.MESH
ANY = pl.BlockSpec(memory_space=pl.ANY)
VMEM_WHOLE = pl.BlockSpec(memory_space=pltpu.VMEM)
HBM_SPEC = pl.BlockSpec(memory_space=pltpu.HBM)
SEM_SPEC = pl.BlockSpec(memory_space=pltpu.SEMAPHORE)
SIDE_EFFECT = pltpu.SideEffectType.DATAFLOW_SIDE_EFFECTING

NN = (((1,), (0,)), ((), ()))
NT = (((1,), (1,)), ((), ()))
TN = (((0,), (0,)), ((), ()))


def _tile(n, target, mult):
    best = None
    for t in range(mult, min(n, target) + 1, mult):
        if n % t == 0:
            best = t
    return n if best is None else best


def _params(n_grid_axes):
    return pltpu.CompilerParams(dimension_semantics=("arbitrary",) * n_grid_axes, vmem_limit_bytes=VMEM_LIMIT_BYTES)


def _rsqrt_mean_sq(x):
    return lax.rsqrt(jnp.mean(x * x, axis=-1, keepdims=True) + EPS)


def _sum_all(x):
    return jnp.sum(jnp.sum(x, axis=1, keepdims=True), axis=0, keepdims=True)


def _sigmoid(x):
    return 0.5 * jnp.tanh(0.5 * x) + 0.5


def _store_epilogue(accs, ex_refs, out_refs, i, j):
    for a, o in zip(accs, out_refs):
        o[...] = a.astype(o.dtype)


def _fused_matmul(name, grid, operands, pairs, acc_shapes, extras, outs, epilogue):
    nk = grid[2]
    n_op, n_ex, n_out, n_acc = len(operands), len(extras), len(outs), len(acc_shapes)
    in_place = nk > 1 and epilogue is _store_epilogue and all(o[1] == jnp.float32 for o in outs)

    def body(*refs):
        op_refs = refs[:n_op]
        ex_refs = refs[n_op:n_op + n_ex]
        out_refs = refs[n_op + n_ex:n_op + n_ex + n_out]
        acc_refs = out_refs if in_place else refs[n_op + n_ex + n_out:]
        i, j, k = pl.program_id(0), pl.program_id(1), pl.program_id(2)
        loaded = {}

        def operand(n):
            if n not in loaded:
                loaded[n] = op_refs[n][...].astype(MXU_DTYPE)
            return loaded[n]

        def product(ia, ib, dims):
            return lax.dot_general(operand(ia), operand(ib), dims, preferred_element_type=jnp.float32)

        def products():
            parts = [None] * n_acc
            for ia, ib, iacc, dims in pairs:
                d = product(ia, ib, dims)
                parts[iacc] = d if parts[iacc] is None else parts[iacc] + d
            return parts

        if nk == 1:
            epilogue(products(), ex_refs, out_refs, i, j)
            return

        @pl.when(k == 0)
        def _():
            for a in range(n_acc):
                acc_refs[a][...] = jnp.zeros_like(acc_refs[a])

        for a, part in enumerate(products()):
            acc_refs[a][...] += part

        if not in_place:
            @pl.when(k == nk - 1)
            def _():
                epilogue([acc_refs[a][...] for a in range(n_acc)], ex_refs, out_refs, i, j)

    arrays = [o[0] for o in operands] + [e[0] for e in extras]
    in_specs = [pl.BlockSpec(o[1], o[2]) for o in operands]
    in_specs += [ANY if e[1] is None else pl.BlockSpec(e[1], e[2]) for e in extras]
    return pl.pallas_call(
        body,
        name=name,
        grid=grid,
        in_specs=in_specs,
        out_specs=[pl.BlockSpec(o[2], o[3]) for o in outs],
        out_shape=[jax.ShapeDtypeStruct(o[0], o[1]) for o in outs],
        scratch_shapes=[pltpu.VMEM(s, jnp.float32) for s in acc_shapes] if nk > 1 and not in_place else [],
        compiler_params=_params(3),
    )(*arrays)


def _matmul_nn(name, a, b, out_dtype, tm=2048, tn=1024, after=None):
    M, K = a.shape
    N = b.shape[1]
    tm, tn = _tile(M, tm, 16), _tile(N, tn, LANES)
    return _fused_matmul(
        name, (M // tm, N // tn, 1),
        [(a, (tm, K), lambda i, j, k: (i, 0)), (b, (K, tn), lambda i, j, k: (0, j))],
        [(0, 1, 0, NN)], [(tm, tn)], _order_after(after),
        [((M, N), out_dtype, (tm, tn), lambda i, j, k: (i, j))], _store_epilogue)[0]


def _matmul_nn_cols(name, a, b3, out_dtype, tm=2048):
    M, K = a.shape
    J, _, Nc = b3.shape
    tm = _tile(M, tm, 16)
    return _fused_matmul(
        name, (M // tm, J, 1),
        [(a, (tm, K), lambda i, j, k: (i, 0)), (b3, (None, K, Nc), lambda i, j, k: (j, 0, 0))],
        [(0, 1, 0, NN)], [(tm, Nc)], [],
        [((M, J * Nc), out_dtype, (tm, Nc), lambda i, j, k: (i, j))], _store_epilogue)[0]


def _order_after(after):
    return [] if after is None else [(after, None, None)]


def _matmul_nt(name, a_list, b_list, out_dtype, tm=2048, tn=1024, after=None):
    M, K = a_list[0].shape
    N = b_list[0].shape[0]
    tm, tn = _tile(M, tm, 16), _tile(N, tn, LANES)
    n = len(a_list)
    operands = [(a, (tm, K), lambda i, j, k: (i, 0)) for a in a_list]
    operands += [(b, (tn, K), lambda i, j, k: (j, 0)) for b in b_list]
    return _fused_matmul(
        name, (M // tm, N // tn, 1), operands,
        [(p, n + p, 0, NT) for p in range(n)], [(tm, tn)], _order_after(after),
        [((M, N), out_dtype, (tm, tn), lambda i, j, k: (i, j))], _store_epilogue)[0]


def _matmul_nt_cols(name, a_list, b3_list, out_dtype, tm=1024, tn=512, after=None):
    M = a_list[0].shape[0]
    J, N, Kc = b3_list[0].shape
    tm, tn = _tile(M, tm, 16), _tile(N, tn, LANES)
    n = len(a_list)
    operands = [(a, (tm, Kc), lambda i, j, k: (i, k)) for a in a_list]
    operands += [(b, (None, tn, Kc), lambda i, j, k: (k, j, 0)) for b in b3_list]
    return _fused_matmul(
        name, (M // tm, N // tn, J), operands,
        [(p, n + p, 0, NT) for p in range(n)], [(tm, tn)], _order_after(after),
        [((M, N), out_dtype, (tm, tn), lambda i, j, k: (i, j))], _store_epilogue)[0]


def _matmul_tn(name, a, b_list, out_dtype, tka=1024, tn=1024, tkm=4096):
    M, Ka = a.shape
    N = b_list[0].shape[1]
    tka, tn, tkm = _tile(Ka, tka, LANES), _tile(N, tn, LANES), _tile(M, tkm, 16)
    n = len(b_list)
    operands = [(a, (tkm, tka), lambda i, j, k: (k, i))]
    operands += [(b, (tkm, tn), lambda i, j, k: (k, j)) for b in b_list]
    return _fused_matmul(
        name, (Ka // tka, N // tn, M // tkm), operands,
        [(0, 1 + p, p, TN) for p in range(n)], [(tka, tn)] * n, [],
        [((Ka, N), out_dtype, (tka, tn), lambda i, j, k: (i, j))] * n, _store_epilogue)


def _matmul_tn_cols(name, a, b_list, n_shards, out_dtype, tka=512, tkm=4096, after=None):
    M, Ka = a.shape
    Nc = b_list[0].shape[1] // n_shards
    tka, tkm = _tile(Ka, tka, LANES), _tile(M, tkm, 16)
    n = len(b_list)
    operands = [(a, (tkm, tka), lambda i, j, k: (k, i))]
    operands += [(b, (tkm, Nc), lambda i, j, k: (k, j)) for b in b_list]
    return _fused_matmul(
        name, (Ka // tka, n_shards, M // tkm), operands,
        [(0, 1 + p, p, TN) for p in range(n)], [(tka, Nc)] * n, _order_after(after),
        [((n_shards, Ka, Nc), out_dtype, (None, tka, Nc), lambda i, j, k: (j, i, 0))] * n, _store_epilogue)


def _rmsnorm(name, x, g_row, out_dtype, tm=512, after=None):
    S, D = x.shape
    tm = _tile(S, tm, 16)

    def body(x_ref, g_ref, *rest):
        o_ref = rest[-1]
        xv = x_ref[...]
        o_ref[...] = (xv * _rsqrt_mean_sq(xv) * g_ref[...]).astype(o_ref.dtype)

    ordered = [] if after is None else [after]
    return pl.pallas_call(
        body, name=name, grid=(S // tm,),
        in_specs=[pl.BlockSpec((tm, D), lambda i: (i, 0)), pl.BlockSpec((1, D), lambda i: (0, 0))] + [ANY] * len(ordered),
        out_specs=pl.BlockSpec((tm, D), lambda i: (i, 0)),
        out_shape=jax.ShapeDtypeStruct((S, D), out_dtype),
        compiler_params=_params(1),
    )(x, g_row, *ordered)


def _rmsnorm_bwd_values(dh, xin, g):
    r = _rsqrt_mean_sq(xin)
    xhat = xin * r
    dhg = dh * g
    dx = r * (dhg - xhat * jnp.mean(dhg * xhat, axis=-1, keepdims=True))
    return dx, jnp.sum(dh * xhat, axis=0, keepdims=True)


def _rmsnorm_bwd(name, dh, xin, g_row, dres, low_dtype, tm=256):
    S, D = xin.shape
    tm = _tile(S, tm, 16)
    has_dx = dres is not None
    has_low = has_dx and low_dtype is not None

    def body(*refs):
        dh_ref, x_ref, g_ref = refs[:3]
        rest = refs[3:]
        dx, dg = _rmsnorm_bwd_values(dh_ref[...], x_ref[...], g_ref[...])
        if has_dx:
            dres_ref, rest = rest[0], rest[1:]
            dx = dx + dres_ref[...]
            rest[0][...] = dx
            if has_low:
                rest[1][...] = dx.astype(low_dtype)
        dg_ref = rest[-1]

        @pl.when(pl.program_id(0) == 0)
        def _():
            dg_ref[...] = jnp.zeros_like(dg_ref)

        dg_ref[...] += dg

    row = pl.BlockSpec((tm, D), lambda i: (i, 0))
    vec = pl.BlockSpec((1, D), lambda i: (0, 0))
    ins, in_specs = [dh, xin, g_row], [row, row, vec]
    out_shape, out_specs = [], []
    if has_dx:
        ins.append(dres)
        in_specs.append(row)
        out_shape.append(jax.ShapeDtypeStruct((S, D), jnp.float32))
        out_specs.append(row)
        if has_low:
            out_shape.append(jax.ShapeDtypeStruct((S, D), low_dtype))
            out_specs.append(row)
    out_shape.append(jax.ShapeDtypeStruct((1, D), jnp.float32))
    out_specs.append(vec)
    return pl.pallas_call(
        body, name=name, grid=(S // tm,), in_specs=in_specs, out_specs=out_specs, out_shape=out_shape,
        compiler_params=_params(1),
    )(*ins)


def _chunk_mask(blk):
    t = lax.broadcasted_iota(jnp.int32, (blk, blk), 0)
    s = lax.broadcasted_iota(jnp.int32, (blk, blk), 1)
    return (s // CHUNK) <= (t // CHUNK)


def _pool_inputs(a, halo, row0, tm, g, pg):
    cols = slice(g * pg, (g + 1) * pg)
    w = POOL_WINDOWS[g]
    s = jnp.concatenate([halo[:, cols], a[:, cols]], axis=0)
    span = 1
    while span < w:
        s = s + pltpu.roll(s, span, axis=0)
        span *= 2
    t = lax.broadcasted_iota(jnp.int32, (tm, 1), 0) + row0
    cnt = jnp.minimum(t + 1, w).astype(jnp.float32)
    return s[POOL_HALO:] / cnt - a[:, cols]


def _mixer_specs(S, tm, DP, DS):
    per_halo = tm // POOL_HALO
    n_halo = S // POOL_HALO
    a_spec = pl.BlockSpec((tm, DP), lambda i: (i, 0))
    prev_spec = pl.BlockSpec((POOL_HALO, DP), lambda i: (jnp.maximum(i * per_halo - 1, 0), 0))
    next_spec = pl.BlockSpec((POOL_HALO, DP), lambda i: (jnp.minimum((i + 1) * per_halo, n_halo - 1), 0))
    u_spec = pl.BlockSpec((tm, DS), lambda i: (i, 1))
    v_spec = pl.BlockSpec((tm, DS), lambda i: (i, 2))
    return a_spec, prev_spec, next_spec, u_spec, v_spec


def _whole(arr):
    nd = arr.ndim
    return pl.BlockSpec(arr.shape, lambda i: (0,) * nd)


def _mixer_fwd(proj, pw, ps_row, gs_row, ws, b_full, tm=512):
    S = proj.shape[0]
    G, PG, _ = pw.shape
    DP = G * PG
    H, BLK, _ = ws.shape
    DS = gs_row.shape[1]
    HD = DS // H
    assert DP == DS and proj.shape[1] == DP + 2 * DS
    tm = _tile(S, tm, BLK)

    def body(a_ref, prev_ref, u_ref, v_ref, pw_ref, ps_ref, gs_ref, ws_ref, b_ref, z_ref):
        i = pl.program_id(0)
        a = a_ref[...]
        halo = jnp.where(i > 0, prev_ref[...], 0.0)
        for g in range(G):
            cols = slice(g * PG, (g + 1) * PG)
            p = _pool_inputs(a, halo, i * tm, tm, g, PG)
            y = jnp.dot(p.astype(MXU_DTYPE), pw_ref[g], preferred_element_type=jnp.float32) * ps_ref[:, cols]
            z_ref[:, cols] = y.astype(z_ref.dtype)
        v = v_ref[...]
        vn = (v * _rsqrt_mean_sq(v) * gs_ref[...]).astype(MXU_DTYPE)
        u = u_ref[...]
        mask = _chunk_mask(BLK)
        for h in range(H):
            wm = jnp.where(mask, ws_ref[h], 0.0).astype(MXU_DTYPE)
            bias = b_ref[h]
            cols = slice(h * HD, (h + 1) * HD)
            for n in range(tm // BLK):
                rows = slice(n * BLK, (n + 1) * BLK)
                mixed = jnp.dot(wm, vn[rows, cols], preferred_element_type=jnp.float32) + bias
                z_ref[rows, DP + h * HD:DP + (h + 1) * HD] = (u[rows, cols] * mixed).astype(z_ref.dtype)

    a_spec, prev_spec, _, u_spec, v_spec = _mixer_specs(S, tm, DP, DS)
    return pl.pallas_call(
        body, name="mixer_fwd", grid=(S // tm,),
        in_specs=[a_spec, prev_spec, u_spec, v_spec, _whole(pw), _whole(ps_row), _whole(gs_row), _whole(ws), _whole(b_full)],
        out_specs=pl.BlockSpec((tm, DP + DS), lambda i: (i, 0)),
        out_shape=jax.ShapeDtypeStruct((S, DP + DS), MXU_DTYPE),
        compiler_params=_params(1),
    )(proj, proj, proj, proj, pw, ps_row, gs_row, ws, b_full)


def _mixer_bwd(proj, dz, pw, ps_row, gs_row, ws, b_full, tm=256):
    S = proj.shape[0]
    G, PG, _ = pw.shape
    DP = G * PG
    H, BLK, _ = ws.shape
    DS = gs_row.shape[1]
    HD = DS // H
    tm = _tile(S, tm, BLK)
    nb = S // tm
    ext = tm + POOL_HALO

    def body(a_ref, prev_ref, u_ref, v_ref, dzp_ref, dzp_next_ref, dzs_ref, pw_ref, ps_ref, gs_ref, ws_ref, b_ref,
             dproj_ref, dpw_ref, dps_ref, dgs_ref, dws_ref, db_ref, dvn_ref):
        i = pl.program_id(0)

        @pl.when(i == 0)
        def _():
            for r in (dpw_ref, dps_ref, dgs_ref, dws_ref, db_ref):
                r[...] = jnp.zeros_like(r)

        a = a_ref[...]
        halo = jnp.where(i > 0, prev_ref[...], 0.0)
        dyp = dzp_ref[...]
        dyp_next = jnp.where(i < nb - 1, dzp_next_ref[...], 0.0)
        dq_ext = jnp.concatenate([dyp, dyp_next], axis=0) * ps_ref[...]
        t_ext = lax.broadcasted_iota(jnp.int32, (ext, 1), 0) + i * tm
        for g in range(G):
            cols = slice(g * PG, (g + 1) * PG)
            w = POOL_WINDOWS[g]
            pb = _pool_inputs(a, halo, i * tm, tm, g, PG).astype(MXU_DTYPE)
            y_pre = jnp.dot(pb, pw_ref[g], preferred_element_type=jnp.float32)
            dps_ref[:, cols] += jnp.sum(dyp[:, cols] * y_pre, axis=0, keepdims=True)
            dqb = dq_ext[:, cols].astype(MXU_DTYPE)
            dpw_ref[g] += lax.dot_general(pb, dqb[:tm], TN, preferred_element_type=jnp.float32)
            dp = lax.dot_general(dqb, pw_ref[g], NT, preferred_element_type=jnp.float32)
            f = dp / jnp.minimum(t_ext + 1, w).astype(jnp.float32)
            span = 1
            while span < w:
                f = f + pltpu.roll(f, ext - span, axis=0)
                span *= 2
            dproj_ref[:, cols] = (f[:tm] - dp[:tm]).astype(dproj_ref.dtype)

        v = v_ref[...]
        rv = _rsqrt_mean_sq(v)
        vhat = v * rv
        gs = gs_ref[...]
        vnb = (vhat * gs).astype(MXU_DTYPE)
        u = u_ref[...]
        dys = dzs_ref[...]
        mask = _chunk_mask(BLK)
        for h in range(H):
            wm = jnp.where(mask, ws_ref[h], 0.0).astype(MXU_DTYPE)
            bias = b_ref[h]
            cols = slice(h * HD, (h + 1) * HD)
            dm_sum = jnp.zeros((BLK, HD), jnp.float32)
            dws_sum = jnp.zeros((BLK, BLK), jnp.float32)
            for n in range(tm // BLK):
                rows = slice(n * BLK, (n + 1) * BLK)
                vblk = vnb[rows, cols]
                mixed = jnp.dot(wm, vblk, preferred_element_type=jnp.float32) + bias
                dy = dys[rows, cols]
                dproj_ref[rows, DP + h * HD:DP + (h + 1) * HD] = (dy * mixed).astype(dproj_ref.dtype)
                dm = dy * u[rows, cols]
                dm_sum = dm_sum + dm
                dmb = dm.astype(MXU_DTYPE)
                dws_sum = dws_sum + lax.dot_general(dmb, vblk, NT, preferred_element_type=jnp.float32)
                dvn_ref[rows, cols] = lax.dot_general(wm, dmb, TN, preferred_element_type=jnp.float32)
            dws_ref[h] += jnp.where(mask, dws_sum, 0.0)
            db_ref[h:h + 1, :] += jnp.sum(dm_sum.T, axis=0, keepdims=True)
        dvn = dvn_ref[...]
        dvg = dvn * gs
        dv = rv * (dvg - vhat * jnp.mean(dvg * vhat, axis=-1, keepdims=True))
        dgs_ref[...] += jnp.sum(dvn * vhat, axis=0, keepdims=True)
        dproj_ref[:, DP + DS:] = dv.astype(dproj_ref.dtype)

    a_spec, prev_spec, next_spec, u_spec, v_spec = _mixer_specs(S, tm, DP, DS)
    dzp_spec = pl.BlockSpec((tm, DP), lambda i: (i, 0))
    dzs_spec = pl.BlockSpec((tm, DS), lambda i: (i, 1))
    small = [pw, ps_row, gs_row, ws, b_full]
    out_shape = [
        jax.ShapeDtypeStruct((S, DP + 2 * DS), MXU_DTYPE),
        jax.ShapeDtypeStruct(pw.shape, jnp.float32),
        jax.ShapeDtypeStruct(ps_row.shape, jnp.float32),
        jax.ShapeDtypeStruct(gs_row.shape, jnp.float32),
        jax.ShapeDtypeStruct(ws.shape, jnp.float32),
        jax.ShapeDtypeStruct((H, BLK), jnp.float32),
    ]
    out_specs = [pl.BlockSpec((tm, DP + 2 * DS), lambda i: (i, 0))] + [_whole(s) for s in out_shape[1:]]
    return pl.pallas_call(
        body, name="mixer_bwd", grid=(nb,),
        in_specs=[a_spec, prev_spec, u_spec, v_spec, dzp_spec, next_spec, dzs_spec] + [_whole(s) for s in small],
        out_specs=out_specs, out_shape=out_shape,
        scratch_shapes=[pltpu.VMEM((tm, DS), jnp.float32)],
        compiler_params=_params(1),
    )(proj, proj, proj, proj, dz, dz, dz, *small)


def _softmax_rows(q, k, scale):
    s = lax.dot_general(q, k, NT, preferred_element_type=jnp.float32) * scale
    e = jnp.exp(s - jnp.max(s, axis=-1, keepdims=True))
    return e / jnp.sum(e, axis=-1, keepdims=True)


def _attn_fwd(q, k, v, tm=1024):
    S, D = q.shape
    M = k.shape[0]
    HD = D // N_XATTN_HEADS
    scale = HD ** -0.5
    tm = _tile(S, tm, 16)

    def body(q_ref, k_ref, v_ref, o_ref):
        for h in range(N_XATTN_HEADS):
            cols = slice(h * HD, (h + 1) * HD)
            p = _softmax_rows(q_ref[:, cols], k_ref[:, cols], scale)
            o = jnp.dot(p.astype(MXU_DTYPE), v_ref[:, cols], preferred_element_type=jnp.float32)
            o_ref[:, cols] = o.astype(o_ref.dtype)

    row = pl.BlockSpec((tm, D), lambda i: (i, 0))
    mem = pl.BlockSpec((M, D), lambda i: (0, 0))
    return pl.pallas_call(
        body, name="attn_fwd", grid=(S // tm,), in_specs=[row, mem, mem], out_specs=row,
        out_shape=jax.ShapeDtypeStruct((S, D), MXU_DTYPE), compiler_params=_params(1),
    )(q, k, v)


def _attn_bwd(q, k, v, do, tm=1024):
    S, D = q.shape
    M = k.shape[0]
    HD = D // N_XATTN_HEADS
    scale = HD ** -0.5
    tm = _tile(S, tm, 16)

    def body(q_ref, k_ref, v_ref, do_ref, dq_ref, dk_ref, dv_ref):
        @pl.when(pl.program_id(0) == 0)
        def _():
            dk_ref[...] = jnp.zeros_like(dk_ref)
            dv_ref[...] = jnp.zeros_like(dv_ref)

        for h in range(N_XATTN_HEADS):
            cols = slice(h * HD, (h + 1) * HD)
            qh, kh, vh, doh = q_ref[:, cols], k_ref[:, cols], v_ref[:, cols], do_ref[:, cols]
            p = _softmax_rows(qh, kh, scale)
            dp = lax.dot_general(doh, vh, NT, preferred_element_type=jnp.float32)
            dv_ref[:, cols] += lax.dot_general(p.astype(MXU_DTYPE), doh, TN, preferred_element_type=jnp.float32)
            ds = (p * (dp - jnp.sum(dp * p, axis=-1, keepdims=True)) * scale).astype(MXU_DTYPE)
            dq_ref[:, cols] = jnp.dot(ds, kh, preferred_element_type=jnp.float32).astype(dq_ref.dtype)
            dk_ref[:, cols] += lax.dot_general(ds, qh, TN, preferred_element_type=jnp.float32)

    row = pl.BlockSpec((tm, D), lambda i: (i, 0))
    mem = pl.BlockSpec((M, D), lambda i: (0, 0))
    return pl.pallas_call(
        body, name="attn_bwd", grid=(S // tm,), in_specs=[row, mem, mem, row], out_specs=[row, mem, mem],
        out_shape=[jax.ShapeDtypeStruct((S, D), MXU_DTYPE), jax.ShapeDtypeStruct((M, D), jnp.float32),
                   jax.ShapeDtypeStruct((M, D), jnp.float32)],
        compiler_params=_params(1),
    )(q, k, v, do)


def _proj_residual_norm(name, a, w2d, resid, g_row, tm=512, after=None):
    M, K = a.shape
    N = w2d.shape[1]
    tm = _tile(M, tm, 16)

    def epilogue(accs, ex_refs, out_refs, i, j):
        x_new = ex_refs[0][...] + accs[0]
        out_refs[0][...] = x_new
        out_refs[1][...] = (x_new * _rsqrt_mean_sq(x_new) * ex_refs[1][...]).astype(MXU_DTYPE)

    row = lambda i, j, k: (i, 0)
    return _fused_matmul(
        name, (M // tm, 1, 1),
        [(a, (tm, K), row), (w2d, (K, N), lambda i, j, k: (0, 0))],
        [(0, 1, 0, NN)], [(tm, N)],
        [(resid, (tm, N), row), (g_row, (1, N), lambda i, j, k: (0, 0))] + _order_after(after),
        [((M, N), jnp.float32, (tm, N), row), ((M, N), MXU_DTYPE, (tm, N), row)], epilogue)


def _ffn_gate_up(h, wg3, wu3, tm=512):
    M, K = h.shape
    J, _, Nc = wg3.shape
    tm = _tile(M, tm, 16)

    def epilogue(accs, ex_refs, out_refs, i, j):
        gate, up = accs
        sg = _sigmoid(gate)
        silu = gate * sg
        out_refs[0][...] = (silu * up).astype(MXU_DTYPE)
        out_refs[1][...] = (up * (sg * (1.0 + gate * (1.0 - sg)))).astype(MXU_DTYPE)
        out_refs[2][...] = silu.astype(MXU_DTYPE)

    col = lambda i, j, k: (i, j)
    wspec = lambda i, j, k: (j, 0, 0)
    return _fused_matmul(
        "ffn_gate_up", (M // tm, J, 1),
        [(h, (tm, K), lambda i, j, k: (i, 0)), (wg3, (None, K, Nc), wspec), (wu3, (None, K, Nc), wspec)],
        [(0, 1, 0, NN), (0, 2, 1, NN)], [(tm, Nc)] * 2, [],
        [((M, J * Nc), MXU_DTYPE, (tm, Nc), col)] * 3, epilogue)


def _ffn_down(act, wd2d, x2, tm=512, tn=1024):
    M, F = act.shape
    D = wd2d.shape[1]
    tm, tn = _tile(M, tm, 16), _tile(D, tn, LANES)

    def epilogue(accs, ex_refs, out_refs, i, j):
        out_refs[0][...] = ex_refs[0][...] + accs[0]

    tile = lambda i, j, k: (i, j)
    return _fused_matmul(
        "ffn_down", (M // tm, D // tn, 1),
        [(act, (tm, F), lambda i, j, k: (i, 0)), (wd2d, (F, tn), lambda i, j, k: (0, j))],
        [(0, 1, 0, NN)], [(tm, tn)], [(x2, (tm, tn), tile)],
        [((M, D), jnp.float32, (tm, tn), tile)], epilogue)[0]


def _final_norm_loss(x3, target, g_row, tm=256):
    S, D = x3.shape
    tm = _tile(S, tm, 16)

    def body(x_ref, t_ref, g_ref, dx_ref, dxl_ref, loss_ref, dg_ref):
        x3v = x_ref[...]
        g = g_ref[...]
        r = _rsqrt_mean_sq(x3v)
        xhat = x3v * r
        diff = xhat * g - t_ref[...]
        dy = diff / D
        dyg = dy * g
        dx = r * (dyg - xhat * jnp.mean(dyg * xhat, axis=-1, keepdims=True))
        dx_ref[...] = dx
        dxl_ref[...] = dx.astype(MXU_DTYPE)

        @pl.when(pl.program_id(0) == 0)
        def _():
            loss_ref[...] = jnp.zeros_like(loss_ref)
            dg_ref[...] = jnp.zeros_like(dg_ref)

        loss_ref[...] += 0.5 * _sum_all(jnp.mean(diff * diff, axis=-1, keepdims=True))
        dg_ref[...] += jnp.sum(dy * xhat, axis=0, keepdims=True)

    row = pl.BlockSpec((tm, D), lambda i: (i, 0))
    vec = pl.BlockSpec((1, D), lambda i: (0, 0))
    return pl.pallas_call(
        body, name="final_norm_loss", grid=(S // tm,), in_specs=[row, row, vec],
        out_specs=[row, row, pl.BlockSpec((1, 1), lambda i: (0, 0)), vec],
        out_shape=[jax.ShapeDtypeStruct((S, D), jnp.float32), jax.ShapeDtypeStruct((S, D), MXU_DTYPE),
                   jax.ShapeDtypeStruct((1, 1), jnp.float32), jax.ShapeDtypeStruct((1, D), jnp.float32)],
        compiler_params=_params(1),
    )(x3, target, g_row)


def _ffn_down_bwd(dx3_low, wd2d, act_by_gate, act_by_up, n_shards, tm=1024):
    M, D = dx3_low.shape
    F = wd2d.shape[0]
    tn = F // n_shards
    tm = _tile(M, tm, 16)

    def epilogue(accs, ex_refs, out_refs, i, j):
        dact = accs[0]
        out_refs[0][...] = (dact * ex_refs[0][...].astype(jnp.float32)).astype(MXU_DTYPE)
        out_refs[1][...] = (dact * ex_refs[1][...].astype(jnp.float32)).astype(MXU_DTYPE)

    col = lambda i, j, k: (i, j)
    return _fused_matmul(
        "ffn_down_bwd", (M // tm, n_shards, 1),
        [(dx3_low, (tm, D), lambda i, j, k: (i, 0)), (wd2d, (tn, D), lambda i, j, k: (j, 0))],
        [(0, 1, 0, NT)], [(tm, tn)],
        [(act_by_gate, (tm, tn), col), (act_by_up, (tm, tn), col)],
        [((M, F), MXU_DTYPE, (tm, tn), col)] * 2, epilogue)


def _matmul_nt_norm_bwd(name, a, b, xin, g_row, dres, low_dtype, tm=256, after=None):
    M = a.shape[0]
    sharded = b.ndim == 3
    N, Kc = b.shape[-2:]
    J = b.shape[0] if sharded else 1
    tm = _tile(M, tm, 16)
    has_low = low_dtype is not None

    def epilogue(accs, ex_refs, out_refs, i, j):
        dx, dg = _rmsnorm_bwd_values(accs[0], ex_refs[0][...], ex_refs[1][...])
        dx = dx + ex_refs[2][...]
        out_refs[0][...] = dx
        if has_low:
            out_refs[1][...] = dx.astype(low_dtype)
        dg_ref = out_refs[-1]

        @pl.when(i == 0)
        def _():
            dg_ref[...] = jnp.zeros_like(dg_ref)

        dg_ref[...] += dg

    row = lambda i, j, k: (i, 0)
    const = lambda i, j, k: (0, 0)
    b_operand = (b, (None, N, Kc), lambda i, j, k: (k, 0, 0)) if sharded else (b, (N, Kc), const)
    outs = [((M, N), jnp.float32, (tm, N), row)]
    if has_low:
        outs.append(((M, N), low_dtype, (tm, N), row))
    outs.append(((1, N), jnp.float32, (1, N), const))
    return _fused_matmul(
        name, (M // tm, 1, J), [(a, (tm, Kc), lambda i, j, k: (i, k)), b_operand],
        [(0, 1, 0, NT)], [(tm, N)],
        [(xin, (tm, N), row), (g_row, (1, N), const), (dres, (tm, N), row)] + _order_after(after),
        outs, epilogue)


def _row_tile(R, C, target_bytes=2 ** 21):
    return _tile(R, max(16, target_bytes // (4 * C)), 16)


def _prefetch_call(body, name, grid, pos, in_specs, out_specs, out_shape, ins, after=None):
    ordered = [] if after is None else [after]
    grid_spec = pltpu.PrefetchScalarGridSpec(
        num_scalar_prefetch=1, grid=grid, in_specs=list(in_specs) + [ANY] * len(ordered), out_specs=out_specs)
    return pl.pallas_call(body, name=name, grid_spec=grid_spec, out_shape=out_shape,
                          compiler_params=_params(len(grid)))(pos, *ins, *ordered)


def _runs(arrays):
    runs = []
    for i, a in enumerate(arrays):
        if runs and arrays[runs[-1][-1]].shape == a.shape:
            runs[-1].append(i)
        else:
            runs.append([i])
    return runs


def _per_run(fn, names, *array_lists):
    out = [None] * len(names)
    for run in _runs(array_lists[0]):
        results = fn("_".join(names[i] for i in run), *[[arrays[i] for i in run] for arrays in array_lists])
        for i, r in zip(run, results):
            out[i] = r
    return out


def _cast_into_slabs(name, xs, dtype, pos, after=None):
    n = len(xs)
    R, C = xs[0].shape
    tr = _row_tile(R, C * n, 2 ** 22)

    def body(pos_ref, *refs):
        for t in range(n):
            refs[len(refs) - n + t][...] = refs[t][...].astype(dtype)

    return _prefetch_call(
        body, "cast_" + name, (R // tr,), pos, [pl.BlockSpec((tr, C), lambda i, pos: (i, 0))] * n,
        [pl.BlockSpec((None, tr, C), lambda i, pos: (pos[0], i, 0))] * n,
        [jax.ShapeDtypeStruct((N_CHIPS, R, C), dtype)] * n, xs, after)


def _chip_sum(name, partials, gots, dtype, pos):
    n = len(partials)
    J, R, C = partials[0].shape
    Rh = R // 2
    tr = _row_tile(Rh, C * n, 2 ** 23)

    def body(pos_ref, *refs):
        for t in range(n):
            refs[2 * n + t][...] = (refs[t][...].astype(jnp.float32) + refs[n + t][...].astype(jnp.float32)).astype(dtype)

    blk = pl.BlockSpec((None, tr, C), lambda j, i, pos: (j, i, 0))
    return _prefetch_call(
        body, "chip_sum_" + name, (J, Rh // tr), pos,
        [pl.BlockSpec((None, None, tr, C), lambda j, i, pos: (j, pos[1], i, 0))] * n + [blk] * n, [blk] * n,
        [jax.ShapeDtypeStruct((J, Rh, C), dtype)] * n, [p.reshape(J, 2, Rh, C) for p in partials] + list(gots))


def _owner_sum(name, chip_sums, others, pos):
    n = len(chip_sums)
    J, Rh, C = chip_sums[0].shape
    n_other = others[0].shape[0]
    tr = _row_tile(Rh, C * (n_other + 1) * n, 2 ** 23)

    def body(pos_ref, *refs):
        for t in range(n):
            acc = refs[t][...].astype(jnp.float32)
            for k in range(n_other):
                acc = acc + refs[n + t][k].astype(jnp.float32)
            refs[2 * n + t][...] = acc

    return _prefetch_call(
        body, "owner_sum_" + name, (Rh // tr,), pos,
        [pl.BlockSpec((None, tr, C), lambda i, pos: (pos[0], i, 0))] * n
        + [pl.BlockSpec((n_other, tr, C), lambda i, pos: (0, i, 0))] * n,
        [pl.BlockSpec((None, tr, C), lambda i, pos: (pos[1], i, 0))] * n,
        [jax.ShapeDtypeStruct((2, Rh, C), jnp.float32)] * n, list(chip_sums) + list(others))


def _adamw_values(w, g, m, v):
    m = ADAM_B1 * m + (1.0 - ADAM_B1) * g
    v = ADAM_B2 * v + (1.0 - ADAM_B2) * (g * g)
    m_hat = m / (1.0 - ADAM_B1 ** ADAM_STEP)
    v_hat = v / (1.0 - ADAM_B2 ** ADAM_STEP)
    delta = -ADAM_LR * (m_hat / (jnp.sqrt(v_hat) + ADAM_EPS) + ADAM_WD * w)
    return delta, m, v


def _adamw(name, ws, gs, ms, vs):
    n = len(ws)
    R, C = ws[0].shape
    tr = _row_tile(R, C * n, 2 ** 21)

    def body(*refs):
        w_refs, g_refs, m_refs, v_refs = (refs[q * n:(q + 1) * n] for q in range(4))
        outs = refs[4 * n:]
        for t in range(n):
            g = g_refs[t][...]
            outs[4 * t][...] = g
            outs[4 * t + 1][...], outs[4 * t + 2][...], outs[4 * t + 3][...] = _adamw_values(
                w_refs[t][...], g, m_refs[t][...], v_refs[t][...])

    blk = pl.BlockSpec((tr, C), lambda i: (i, 0))
    res = pl.pallas_call(
        body, name="adamw_" + name, grid=(R // tr,), in_specs=[blk] * (4 * n), out_specs=[blk] * (4 * n),
        out_shape=[jax.ShapeDtypeStruct((R, C), jnp.float32)] * (4 * n), compiler_params=_params(1),
    )(*ws, *gs, *ms, *vs)
    return [tuple(res[4 * t:4 * t + 4]) for t in range(n)]


def _adamw_small(g_packed, ws, ms, vs, row_counts):
    n = len(ws)

    def body(*refs):
        g_ref = refs[0]
        w_refs, m_refs, v_refs = refs[1:1 + n], refs[1 + n:1 + 2 * n], refs[1 + 2 * n:1 + 3 * n]
        outs = refs[1 + 3 * n:]
        off = 0
        for t in range(n):
            g = g_ref[off:off + row_counts[t], :]
            off += row_counts[t]
            delta, m_new, v_new = _adamw_values(w_refs[t][...], g, m_refs[t][...], v_refs[t][...])
            outs[4 * t][...] = g
            outs[4 * t + 1][...] = delta
            outs[4 * t + 2][...] = m_new
            outs[4 * t + 3][...] = v_new

    out_shape = []
    for t in range(n):
        out_shape += [jax.ShapeDtypeStruct((row_counts[t], LANES), jnp.float32)] * 4
    return pl.pallas_call(
        body, name="adamw_small", in_specs=[VMEM_WHOLE] * (1 + 3 * n), out_specs=[VMEM_WHOLE] * (4 * n),
        out_shape=out_shape, compiler_params=pltpu.CompilerParams(vmem_limit_bytes=VMEM_LIMIT_BYTES),
    )(g_packed, *ws, *ms, *vs)


def _position():
    x, y, c = lax.axis_index("x"), lax.axis_index("y"), lax.axis_index("c")
    other_chips = [(1 - x, y), (x, 1 - y), (1 - x, 1 - y)]
    return x, y, c, other_chips


def _half(rows, which):
    return pl.ds(which * (rows // 2), rows // 2)


def _chunk_index(shape, dtype):
    rows = shape[-2]
    total = math.prod(shape) * jnp.dtype(dtype).itemsize
    n = max(1, min(rows // COPY_ROW_ALIGN, total // COPY_CHUNK_BYTES))
    step = -(-rows // n)
    step = -(-step // COPY_ROW_ALIGN) * COPY_ROW_ALIGN
    lead = (slice(None),) * (len(shape) - 2)
    return [lead + (pl.ds(s, min(step, rows - s)),) for s in range(0, rows, step)]


def _start_in_chunks(make, src, dst, waited_here=True):
    for idx in _chunk_index(src.shape, src.dtype):
        make(src.at[idx], dst.at[idx]).start()
    return make(src, dst) if waited_here else None


def _remote_copier(send_sem, recv_sem, to):
    return lambda src, dst: pltpu.make_async_remote_copy(
        src_ref=src, dst_ref=dst, send_sem=send_sem, recv_sem=recv_sem, device_id=to, device_id_type=MESH)


def _slab_half(ref, chip, which):
    return ref.at[chip, _half(ref.shape[1], which)]


def _slab_quarter(ref, chip, which, q):
    rows = ref.shape[1] // 4
    return ref.at[chip, pl.ds(which * 2 * rows + q * rows, rows)]


def _gather_step(name, arrived=(), ring=(), direct=(), after=None):
    groups = [g[2] for g in arrived] + list(direct) + list(ring)
    per_tensor = [2] * len(arrived) + [3] * len(direct) + [2] * len(ring)
    flat = [a for g in groups for a in g]
    T, G, A = len(flat), len(groups), len(arrived)
    ordered = [] if after is None else [after]
    sems_in = [sem for g in arrived for sem in g[:2]]

    def body(*refs):
        ins = refs[:T]
        first_hop = refs[T:T + 2 * A]
        out_sems = refs[T + 2 * A + len(ordered):T + 2 * A + len(ordered) + 2 * G]
        token = refs[-1]
        x, y, c, other_chips = _position()
        mine = 2 * x + y
        x_nbr, y_nbr = (1 - x, y, c), (x, 1 - y, c)
        from_x, from_y = 2 * (1 - x) + y, 2 * x + (1 - y)
        t0 = 0
        for gi, group in enumerate(groups):
            send, recv = out_sems[2 * gi], out_sems[2 * gi + 1]
            for t in range(len(group)):
                ref = ins[t0 + t]
                own = _slab_half(ref, mine, c)
                if gi < A:
                    send1, recv1 = first_hop[2 * gi], first_hop[2 * gi + 1]
                    for k, (peer, chip) in enumerate(((x_nbr, from_x), (y_nbr, from_y))):
                        cp = _remote_copier(send1.at[2 * t + k], recv1.at[2 * t + k], peer)(own, _slab_half(ref, chip, c))
                        cp.wait_send()
                        cp.wait_recv()
                    for k, (peer, chip) in enumerate(((y_nbr, from_x), (x_nbr, from_y))):
                        piece = _slab_quarter(ref, chip, c, k)
                        _start_in_chunks(_remote_copier(send.at[2 * t + k], recv.at[2 * t + k], peer), piece, piece,
                                         waited_here=False)
                elif gi < A + len(direct):
                    for k, (cx, cy) in enumerate(other_chips):
                        _start_in_chunks(_remote_copier(send.at[3 * t + k], recv.at[3 * t + k], (cx, cy, c)), own, own,
                                         waited_here=False)
                else:
                    for k, peer in enumerate((x_nbr, y_nbr)):
                        _start_in_chunks(_remote_copier(send.at[2 * t + k], recv.at[2 * t + k], peer), own, own,
                                         waited_here=False)
            t0 += len(group)
        token[...] = jnp.zeros_like(token)

    sem_shapes = []
    for group, n in zip(groups, per_tensor):
        sem_shapes += [pltpu.SemaphoreType.DMA((n * len(group),))] * 2
    res = pl.pallas_call(
        body, name=name, in_specs=[HBM_SPEC] * T + [SEM_SPEC] * (2 * A) + [ANY] * len(ordered),
        out_specs=[SEM_SPEC] * (2 * G) + [HBM_SPEC] * T + [VMEM_WHOLE],
        out_shape=sem_shapes + [pltpu.HBM(a.shape, a.dtype) for a in flat] + [jax.ShapeDtypeStruct((8, LANES), jnp.float32)],
        input_output_aliases={t: 2 * G + t for t in range(T)},
        compiler_params=pltpu.CompilerParams(has_side_effects=SIDE_EFFECT),
    )(*[pltpu.with_memory_space_constraint(a, pltpu.HBM) for a in flat], *sems_in, *ordered)
    flights, t0 = [], 2 * G
    for gi, group in enumerate(groups):
        flights.append((res[2 * gi], res[2 * gi + 1], res[t0:t0 + len(group)]))
        t0 += len(group)
    return flights[:A], flights[A:A + len(direct)], flights[A + len(direct):], res[-1]


def _relayed_wait(name, flight, after):
    send_sems, recv_sems, slabs = flight
    T = len(slabs)

    def body(*refs):
        ins = refs[:T]
        send, recv = refs[T:T + 2]
        x, y, c, _ = _position()
        x_nbr, y_nbr = (1 - x, y, c), (x, 1 - y, c)
        from_x, from_y, diagonal = 2 * (1 - x) + y, 2 * x + (1 - y), 2 * (1 - x) + (1 - y)
        for t in range(T):
            for k, (peer, chip) in enumerate(((y_nbr, from_x), (x_nbr, from_y))):
                cp = _remote_copier(send.at[2 * t + k], recv.at[2 * t + k], peer)(
                    _slab_quarter(ins[t], chip, c, k), _slab_quarter(ins[t], diagonal, c, k))
                cp.wait_send()
                cp.wait_recv()

    return pl.pallas_call(
        body, name=name, in_specs=[HBM_SPEC] * T + [SEM_SPEC, SEM_SPEC, ANY], out_specs=[HBM_SPEC] * T,
        out_shape=[pltpu.HBM(a.shape, a.dtype) for a in slabs], input_output_aliases={t: t for t in range(T)},
        compiler_params=pltpu.CompilerParams(has_side_effects=SIDE_EFFECT),
    )(*slabs, send_sems, recv_sems, after)


def _gather_wait(name, flight, after):
    send_sems, recv_sems, slabs = flight
    T = len(slabs)

    def body(*refs):
        ins = refs[:T]
        send, recv = refs[T:T + 2]
        x, y, c, other_chips = _position()
        mine = 2 * x + y
        for t in range(T):
            own = _slab_half(ins[t], mine, c)
            for k, (cx, cy) in enumerate(other_chips):
                landed = _slab_half(ins[t], 2 * cx + cy, c)
                cp = _remote_copier(send.at[3 * t + k], recv.at[3 * t + k], (cx, cy, c))(own, landed)
                cp.wait_send()
                cp.wait_recv()

    return pl.pallas_call(
        body, name=name, in_specs=[HBM_SPEC] * T + [SEM_SPEC, SEM_SPEC, ANY], out_specs=[HBM_SPEC] * T,
        out_shape=[pltpu.HBM(s.shape, s.dtype) for s in slabs], input_output_aliases={t: t for t in range(T)},
        compiler_params=pltpu.CompilerParams(has_side_effects=SIDE_EFFECT),
    )(*slabs, send_sems, recv_sems, after)


def _share_with_sibling(name, slabs):
    T = len(slabs)

    def body(*refs):
        outs = refs[T:2 * T]
        send_sems, recv_sems = refs[2 * T:]
        x, y, c, other_chips = _position()
        sibling = (x, y, 1 - c)
        started = []
        for t in range(T):
            for k, (cx, cy) in enumerate(other_chips):
                landed = _slab_half(outs[t], 2 * cx + cy, c)
                started.append(_start_in_chunks(_remote_copier(send_sems.at[t, k], recv_sems.at[t, k], sibling), landed, landed))
        for t in range(T):
            for k, (cx, cy) in enumerate(other_chips):
                arriving = _slab_half(outs[t], 2 * cx + cy, 1 - c)
                _remote_copier(send_sems.at[t, k], recv_sems.at[t, k], sibling)(arriving, arriving).wait_recv()
        for cp in started:
            cp.wait_send()

    return pl.pallas_call(
        body, name=name, in_specs=[ANY] * T, out_specs=[ANY] * T,
        out_shape=[jax.ShapeDtypeStruct(s.shape, s.dtype) for s in slabs],
        input_output_aliases={t: t for t in range(T)},
        scratch_shapes=[pltpu.SemaphoreType.DMA((T, 3)), pltpu.SemaphoreType.DMA((T, 3))],
    )(*slabs)


def _share_start(name, slabs):
    T = len(slabs)

    def body(*refs):
        ins = refs[:T]
        send_sems, recv_sems = refs[T:T + 2]
        token = refs[-1]
        x, y, c, other_chips = _position()
        for t in range(T):
            for k, (cx, cy) in enumerate(other_chips):
                landed = _slab_half(ins[t], 2 * cx + cy, c)
                _start_in_chunks(_remote_copier(send_sems.at[3 * t + k], recv_sems.at[3 * t + k], (x, y, 1 - c)),
                                 landed, landed, waited_here=False)
        token[...] = jnp.zeros_like(token)

    res = pl.pallas_call(
        body, name=name, in_specs=[HBM_SPEC] * T,
        out_specs=[SEM_SPEC, SEM_SPEC] + [HBM_SPEC] * T + [VMEM_WHOLE],
        out_shape=[pltpu.SemaphoreType.DMA((3 * T,)), pltpu.SemaphoreType.DMA((3 * T,))]
        + [pltpu.HBM(a.shape, a.dtype) for a in slabs] + [jax.ShapeDtypeStruct((8, LANES), jnp.float32)],
        input_output_aliases={t: 2 + t for t in range(T)},
        compiler_params=pltpu.CompilerParams(has_side_effects=SIDE_EFFECT),
    )(*[pltpu.with_memory_space_constraint(a, pltpu.HBM) for a in slabs])
    return res[0], res[1], res[2:2 + T], res[-1]


def _share_wait(name, flight, after):
    send_sems, recv_sems, slabs, _ = flight
    T = len(slabs)

    def body(*refs):
        ins = refs[:T]
        send, recv = refs[T:T + 2]
        x, y, c, other_chips = _position()
        for t in range(T):
            for k, (cx, cy) in enumerate(other_chips):
                cp = _remote_copier(send.at[3 * t + k], recv.at[3 * t + k], (x, y, 1 - c))(
                    _slab_half(ins[t], 2 * cx + cy, c), _slab_half(ins[t], 2 * cx + cy, 1 - c))
                cp.wait_send()
                cp.wait_recv()

    return pl.pallas_call(
        body, name=name, in_specs=[HBM_SPEC] * T + [SEM_SPEC, SEM_SPEC, ANY], out_specs=[HBM_SPEC] * T,
        out_shape=[pltpu.HBM(a.shape, a.dtype) for a in slabs], input_output_aliases={t: t for t in range(T)},
        compiler_params=pltpu.CompilerParams(has_side_effects=SIDE_EFFECT),
    )(*slabs, send_sems, recv_sems, after)


def _swap_halves(name, grads):
    T = len(grads)

    def body(*refs):
        ins, got = refs[:T], refs[T:2 * T]
        send_sems, recv_sems = refs[2 * T:]
        x, y, c, _ = _position()
        remote = []
        for t in range(T):
            rows = ins[t].shape[1]
            remote.append(_start_in_chunks(
                _remote_copier(send_sems.at[t], recv_sems.at[t], (x, y, 1 - c)), ins[t].at[:, _half(rows, 1 - c)], got[t]))
        for cp in remote:
            cp.wait()

    return pl.pallas_call(
        body, name=name, in_specs=[ANY] * T, out_specs=[ANY] * T,
        out_shape=[jax.ShapeDtypeStruct((g.shape[0], g.shape[1] // 2, g.shape[2]), g.dtype) for g in grads],
        scratch_shapes=[pltpu.SemaphoreType.DMA((T,)), pltpu.SemaphoreType.DMA((T,))],
    )(*grads)


def _swap_start(name, grads):
    T = len(grads)
    lands = [lax.empty((g.shape[0], g.shape[1] // 2, g.shape[2]), g.dtype) for g in grads]

    def body(*refs):
        ins, got = refs[:T], refs[T:2 * T]
        send_sems, recv_sems = refs[2 * T:2 * T + 2]
        token = refs[-1]
        x, y, c, _ = _position()
        for t in range(T):
            _start_in_chunks(_remote_copier(send_sems.at[t], recv_sems.at[t], (x, y, 1 - c)),
                             ins[t].at[:, _half(ins[t].shape[1], 1 - c)], got[t], waited_here=False)
        token[...] = jnp.zeros_like(token)

    arrays = list(grads) + lands
    res = pl.pallas_call(
        body, name=name, in_specs=[HBM_SPEC] * (2 * T),
        out_specs=[SEM_SPEC, SEM_SPEC] + [HBM_SPEC] * (2 * T) + [VMEM_WHOLE],
        out_shape=[pltpu.SemaphoreType.DMA((T,)), pltpu.SemaphoreType.DMA((T,))]
        + [pltpu.HBM(a.shape, a.dtype) for a in arrays] + [jax.ShapeDtypeStruct((8, LANES), jnp.float32)],
        input_output_aliases={i: 2 + i for i in range(2 * T)},
        compiler_params=pltpu.CompilerParams(has_side_effects=SIDE_EFFECT),
    )(*[pltpu.with_memory_space_constraint(a, pltpu.HBM) for a in arrays])
    return res[0], res[1], res[2:2 + T], res[2 + T:2 + 2 * T], res[-1]


def _swap_wait(name, flight, after):
    send_sems, recv_sems, grads, lands, _ = flight
    T = len(grads)

    def body(*refs):
        ins, got = refs[:T], refs[T:2 * T]
        send, recv = refs[2 * T:2 * T + 2]
        x, y, c, _ = _position()
        for t in range(T):
            cp = _remote_copier(send.at[t], recv.at[t], (x, y, 1 - c))(ins[t].at[:, _half(ins[t].shape[1], 1 - c)], got[t])
            cp.wait_send()
            cp.wait_recv()

    arrays = list(grads) + list(lands)
    res = pl.pallas_call(
        body, name=name, in_specs=[HBM_SPEC] * (2 * T) + [SEM_SPEC, SEM_SPEC, ANY], out_specs=[HBM_SPEC] * (2 * T),
        out_shape=[pltpu.HBM(a.shape, a.dtype) for a in arrays], input_output_aliases={i: i for i in range(2 * T)},
        compiler_params=pltpu.CompilerParams(has_side_effects=SIDE_EFFECT),
    )(*arrays, send_sems, recv_sems, after)
    return res[:T], res[T:]


def _scatter_start(name, chip_sums):
    T = len(chip_sums)
    lands = [lax.empty((3,) + s.shape[1:], s.dtype) for s in chip_sums]

    def body(*refs):
        srcs, dsts = refs[:T], refs[T:2 * T]
        send_sems, recv_sems = refs[2 * T:2 * T + 2]
        token = refs[-1]
        x, y, c, other_chips = _position()
        for t in range(T):
            for k, (cx, cy) in enumerate(other_chips):
                _start_in_chunks(_remote_copier(send_sems.at[3 * t + k], recv_sems.at[3 * t + k], (cx, cy, c)),
                                 srcs[t].at[2 * cx + cy], dsts[t].at[k], waited_here=False)
        token[...] = jnp.zeros_like(token)

    hbm = [pltpu.HBM(a.shape, a.dtype) for a in list(chip_sums) + lands]
    res = pl.pallas_call(
        body, name=name, in_specs=[HBM_SPEC] * (2 * T),
        out_specs=[SEM_SPEC, SEM_SPEC] + [HBM_SPEC] * (2 * T) + [VMEM_WHOLE],
        out_shape=[pltpu.SemaphoreType.DMA((3 * T,)), pltpu.SemaphoreType.DMA((3 * T,))] + hbm
        + [jax.ShapeDtypeStruct((8, LANES), jnp.float32)],
        input_output_aliases={i: 2 + i for i in range(2 * T)},
        compiler_params=pltpu.CompilerParams(has_side_effects=SIDE_EFFECT),
    )(*[pltpu.with_memory_space_constraint(a, pltpu.HBM) for a in list(chip_sums) + lands])
    return res[0], res[1], res[2:2 + T], res[2 + T:2 + 2 * T], res[-1]


def _scatter_wait(name, send_sems, recv_sems, chip_sums, lands, after):
    T = len(chip_sums)

    def body(*refs):
        srcs, dsts = refs[:T], refs[T:2 * T]
        send, recv = refs[2 * T:2 * T + 2]
        x, y, c, other_chips = _position()
        for t in range(T):
            for k, (cx, cy) in enumerate(other_chips):
                cp = _remote_copier(send.at[3 * t + k], recv.at[3 * t + k], (cx, cy, c))(srcs[t].at[2 * cx + cy], dsts[t].at[k])
                cp.wait_send()
                cp.wait_recv()

    arrays = list(chip_sums) + list(lands)
    res = pl.pallas_call(
        body, name=name, in_specs=[HBM_SPEC] * (2 * T) + [SEM_SPEC, SEM_SPEC, ANY],
        out_specs=[HBM_SPEC] * (2 * T), out_shape=[pltpu.HBM(a.shape, a.dtype) for a in arrays],
        input_output_aliases={i: i for i in range(2 * T)},
        compiler_params=pltpu.CompilerParams(has_side_effects=SIDE_EFFECT),
    )(*arrays, send_sems, recv_sems, after)
    return res[:T], res[T:]


def _as_shards(partial):
    return [p.reshape((N_CHIPS, -1, p.shape[-1])) for p in partial]


def _reduce_begin(tag, names, partial, pos):
    parts = _as_shards(partial)
    received = _swap_halves("swap_grad_halves_" + tag, parts)
    return _reduce_scatter_start(tag, names, parts, received, pos)


def _reduce_scatter_start(tag, names, parts, received, pos):
    chip_sums = _per_run(lambda name, p, r: _chip_sum(name, p, r, WIRE_DTYPE, pos), names, list(parts), list(received))
    return _scatter_start("scatter_start_" + tag, chip_sums)


def _reduce_end(tag, names, flight, after, pos):
    send_sems, recv_sems, chip_sums, lands, _ = flight
    chip_sums, others = _scatter_wait("scatter_wait_" + tag, send_sems, recv_sems, chip_sums, lands, after)
    reduced = _per_run(lambda name, cs, ot: _owner_sum(name, cs, ot, pos), names, list(chip_sums), list(others))
    return dict(zip(names, reduced))


def _join_start(name, halves):
    T = len(halves)

    def body(*refs):
        ins = refs[:T]
        send_sems, recv_sems = refs[T:T + 2]
        token = refs[-1]
        x, y, c, _ = _position()
        for t in range(T):
            _start_in_chunks(_remote_copier(send_sems.at[t], recv_sems.at[t], (x, y, 1 - c)),
                             ins[t].at[c], ins[t].at[c], waited_here=False)
        token[...] = jnp.zeros_like(token)

    res = pl.pallas_call(
        body, name=name, in_specs=[HBM_SPEC] * T,
        out_specs=[SEM_SPEC, SEM_SPEC] + [HBM_SPEC] * T + [VMEM_WHOLE],
        out_shape=[pltpu.SemaphoreType.DMA((T,)), pltpu.SemaphoreType.DMA((T,))]
        + [pltpu.HBM(h.shape, h.dtype) for h in halves] + [jax.ShapeDtypeStruct((8, LANES), jnp.float32)],
        input_output_aliases={t: 2 + t for t in range(T)},
        compiler_params=pltpu.CompilerParams(has_side_effects=SIDE_EFFECT),
    )(*[pltpu.with_memory_space_constraint(h, pltpu.HBM) for h in halves])
    return res[0], res[1], res[2:2 + T], res[-1]


def _join_wait(name, flight, after):
    send_sems, recv_sems, halves, _ = flight
    T = len(halves)

    def body(*refs):
        ins = refs[:T]
        send, recv = refs[T:T + 2]
        x, y, c, _ = _position()
        for t in range(T):
            cp = _remote_copier(send.at[t], recv.at[t], (x, y, 1 - c))(ins[t].at[c], ins[t].at[1 - c])
            cp.wait_send()
            cp.wait_recv()

    return pl.pallas_call(
        body, name=name, in_specs=[HBM_SPEC] * T + [SEM_SPEC, SEM_SPEC, ANY], out_specs=[HBM_SPEC] * T,
        out_shape=[pltpu.HBM(h.shape, h.dtype) for h in halves], input_output_aliases={t: t for t in range(T)},
        compiler_params=pltpu.CompilerParams(has_side_effects=SIDE_EFFECT),
    )(*halves, send_sems, recv_sems, after)


def _peers():
    x, y, c, _ = _position()
    flips = [(fx, fy, fc) for fx in (0, 1) for fy in (0, 1) for fc in (0, 1)][1:]
    return [(jnp.bitwise_xor(x, fx), jnp.bitwise_xor(y, fy), jnp.bitwise_xor(c, fc)) for fx, fy, fc in flips]


def _device_index(px, py, pc):
    return 4 * px + 2 * py + pc


def _exchange_start(name, slots):
    def body(slots_ref, send_sems, recv_sems, thru, token):
        x, y, c, _ = _position()
        own = slots_ref.at[_device_index(x, y, c)]
        for k, peer in enumerate(_peers()):
            _remote_copier(send_sems.at[k], recv_sems.at[k], peer)(own, own).start()
        token[...] = jnp.zeros_like(token)

    res = pl.pallas_call(
        body, name=name, in_specs=[HBM_SPEC], out_specs=[SEM_SPEC, SEM_SPEC, HBM_SPEC, VMEM_WHOLE],
        out_shape=[pltpu.SemaphoreType.DMA((N_DEV - 1,)), pltpu.SemaphoreType.DMA((N_DEV - 1,)),
                   pltpu.HBM(slots.shape, slots.dtype), jax.ShapeDtypeStruct((8, LANES), jnp.float32)],
        input_output_aliases={0: 2}, compiler_params=pltpu.CompilerParams(has_side_effects=SIDE_EFFECT),
    )(pltpu.with_memory_space_constraint(slots, pltpu.HBM))
    return res[0], res[1], res[2], res[3]


def _exchange_wait(name, flight, after):
    send_sems, recv_sems, slots, _ = flight

    def body(slots_ref, send, recv, after_ref, out_ref):
        x, y, c, _ = _position()
        own = slots_ref.at[_device_index(x, y, c)]
        for k, peer in enumerate(_peers()):
            cp = _remote_copier(send.at[k], recv.at[k], peer)(own, slots_ref.at[_device_index(*peer)])
            cp.wait_send()
            cp.wait_recv()

    return pl.pallas_call(
        body, name=name, in_specs=[HBM_SPEC, SEM_SPEC, SEM_SPEC, ANY], out_specs=HBM_SPEC,
        out_shape=pltpu.HBM(slots.shape, slots.dtype), input_output_aliases={0: 0},
        compiler_params=pltpu.CompilerParams(has_side_effects=SIDE_EFFECT),
    )(slots, send_sems, recv_sems, after)


def _sum_slots(name, slots):
    n, rows, lanes = slots.shape
    tr = rows if slots.size * 4 <= 2 ** 23 else _row_tile(rows, lanes * n)

    def body(s_ref, o_ref):
        acc = s_ref[0]
        for d in range(1, n):
            acc = acc + s_ref[d]
        o_ref[...] = acc

    return pl.pallas_call(
        body, name=name, grid=(rows // tr,), in_specs=[pl.BlockSpec((n, tr, lanes), lambda i: (0, i, 0))],
        out_specs=pl.BlockSpec((tr, lanes), lambda i: (i, 0)), out_shape=jax.ShapeDtypeStruct((rows, lanes), jnp.float32),
        compiler_params=_params(1))(slots)


BIG = ("w_in", "pool_w", "w_out", "w_q", "w_k", "w_v", "w_o", "w_gate", "w_up", "w_down")
SMALL = ("norm_mix_g", "pool_scale", "sgu_norm_g", "w_spatial", "b_spatial", "norm_xattn_g", "norm_mem_g",
         "norm_ffn_g", "final_norm_g")
GATHER_ORDER = (("w_in", "pool_w"), ("w_out",), ("w_q", "w_k", "w_v", "w_o"), ("w_gate", "w_up"), ("w_down",))
WEIGHTS = ("norm_mix_g", "w_in", "pool_w", "pool_scale", "sgu_norm_g", "w_spatial", "b_spatial", "w_out",
           "norm_xattn_g", "norm_mem_g", "w_q", "w_k", "w_v", "w_o", "norm_ffn_g", "w_gate", "w_up", "w_down",
           "final_norm_g")


def _as2d(a):
    return a.reshape(-1, a.shape[-1])


def _as_lanes(a):
    return a.reshape(-1, LANES)


def kernel(x, mem, norm_mix_g, w_in, pool_w, pool_scale, sgu_norm_g, w_spatial, b_spatial, w_out, norm_xattn_g, norm_mem_g, w_q, w_k, w_v, w_o, norm_ffn_g, w_gate, w_up, w_down, final_norm_g, loss_target, m_norm_mix_g, m_w_in, m_pool_w, m_pool_scale, m_sgu_norm_g, m_w_spatial, m_b_spatial, m_w_out, m_norm_xattn_g, m_norm_mem_g, m_w_q, m_w_k, m_w_v, m_w_o, m_norm_ffn_g, m_w_gate, m_w_up, m_w_down, m_final_norm_g, v_norm_mix_g, v_w_in, v_pool_w, v_pool_scale, v_sgu_norm_g, v_w_spatial, v_b_spatial, v_w_out, v_norm_xattn_g, v_norm_mem_g, v_w_q, v_w_k, v_w_v, v_w_o, v_norm_ffn_g, v_w_gate, v_w_up, v_w_down, v_final_norm_g):
    args = dict(locals())
    weights = {n: args[n] for n in WEIGHTS}
    moments_m = {n: args["m_" + n] for n in WEIGHTS}
    moments_v = {n: args["v_" + n] for n in WEIGHTS}
    low = MXU_DTYPE

    xs = x[0]
    mems = mem[0]
    target = loss_target[0]
    D = xs.shape[1]
    G, PG = pool_w.shape[1], pool_w.shape[3]

    pos = jnp.stack([2 * lax.axis_index("x") + lax.axis_index("y"), lax.axis_index("c")]).astype(jnp.int32)
    def cast(names, after=None):
        slabs = _per_run(lambda name, xs: _cast_into_slabs(name, xs, WIRE_DTYPE, pos, after), names,
                         [_as2d(weights[n]) for n in names])
        return dict(zip(names, slabs))

    def land(gi, flight, after, relayed=True):
        wait = _relayed_wait if relayed else _gather_wait
        landed = wait("gather_wait_%d" % gi, flight, after)
        return dict(zip(GATHER_ORDER[gi], _share_with_sibling("gather_share_%d" % gi, landed)))

    def rows(a):
        return a.reshape(-1, a.shape[-1])

    g_mix, g_xattn, g_mem, g_ffn = norm_mix_g, norm_xattn_g, norm_mem_g, norm_ffn_g
    g_final = final_norm_g.reshape(1, D)
    ws = w_spatial[0]
    b_full = jnp.broadcast_to(b_spatial[0][:, :, None], ws.shape[:2] + (sgu_norm_g.shape[1] // ws.shape[0],))

    first = cast(GATHER_ORDER[0])
    _, (sent_0,), _, started = _gather_step("gather_step_0", direct=[[first[n] for n in GATHER_ORDER[0]]])
    rest = cast([n for group in GATHER_ORDER[1:3] for n in group], after=started)
    _, (sent_1,), (hop_2,), started = _gather_step(
        "gather_step_1", direct=[[rest[n] for n in GATHER_ORDER[1]]], ring=[[rest[n] for n in GATHER_ORDER[2]]])
    h1 = _rmsnorm("norm_mix", xs, g_mix, low, after=started)
    mn = _rmsnorm("norm_mem", mems, g_mem, low)
    rest.update(cast(GATHER_ORDER[3], after=h1))
    rest.update(cast(GATHER_ORDER[4], after=rest[GATHER_ORDER[3][-1]]))
    slabs = [[rest[n] for n in group] for group in GATHER_ORDER[1:]]
    got = land(0, sent_0, rest["w_down"], relayed=False)
    wg_in = got["w_in"]
    pw = got["pool_w"].reshape(N_CHIPS, G, PG // N_CHIPS, PG).transpose(1, 0, 2, 3).reshape(G, PG, PG)
    proj = _matmul_nn_cols("proj_in", h1, wg_in, jnp.float32)
    z = _mixer_fwd(proj, pw, pool_scale, sgu_norm_g, ws, b_full)
    full = {"w_out": rows(land(1, sent_1, z, relayed=False)["w_out"])}
    (relay_2,), _, (hop_3,), started = _gather_step("gather_step_2", arrived=[hop_2], ring=slabs[2:3], after=full["w_out"])
    x1, h2 = _proj_residual_norm("mix_out", z, full["w_out"], xs, g_xattn, after=started)
    landed = _relayed_wait("gather_wait_2", relay_2, h2)
    full["w_q"] = rows(_share_with_sibling("gather_share_2_q", landed[:1])[0])
    sharing = _share_start("gather_share_2_start", landed[1:])
    q = _matmul_nn("xattn_q", h2, full["w_q"], low, after=sharing[-1])
    full.update({n: rows(a) for n, a in zip(GATHER_ORDER[2][1:], _share_wait("gather_share_2_wait", sharing, q))})
    k = _matmul_nn("xattn_k", mn, full["w_k"], low)
    vv = _matmul_nn("xattn_v", mn, full["w_v"], low)
    o = _attn_fwd(q, k, vv)
    (relay_3,), (sent_4,), _, started = _gather_step("gather_step_3", arrived=[hop_3], direct=slabs[3:4], after=o)
    x2, h3 = _proj_residual_norm("xattn_out", o, full["w_o"], x1, g_ffn, after=started)
    got = land(3, relay_3, h3)
    wg_gate, wg_up = got["w_gate"], got["w_up"]
    act, act_by_gate, act_by_up = _ffn_gate_up(h3, wg_gate, wg_up)
    full["w_down"] = rows(land(4, sent_4, act, relayed=False)["w_down"])
    x3 = _ffn_down(act, full["w_down"], x2)
    dx3, dx3_low, loss_part, d_final_g = _final_norm_loss(x3, target, g_final)

    dgate, dup = _ffn_down_bwd(dx3_low, full["w_down"], act_by_gate, act_by_up, N_CHIPS)
    (d_w_down,) = _matmul_tn("grad_w_down", act, [dx3_low], WIRE_DTYPE)
    (d_w_gate,) = _matmul_tn_cols("grad_w_gate", h3, [dgate], N_CHIPS, WIRE_DTYPE, tka=1024)
    (d_w_up,) = _matmul_tn_cols("grad_w_up", h3, [dup], N_CHIPS, WIRE_DTYPE, tka=1024)
    ffn_names = ("w_gate", "w_up", "w_down")
    ffn_swap = _swap_start("swap_start_ffn", _as_shards([d_w_gate, d_w_up, d_w_down]))
    dh3 = _matmul_nt_cols("ffn_dh", [dgate, dup], [wg_gate, wg_up], jnp.float32, tn=1024, after=ffn_swap[-1])
    ffn_flight = _reduce_scatter_start("ffn", ffn_names, *_swap_wait("swap_wait_ffn", ffn_swap, dh3), pos)
    dx2, dx2_low, d_ffn_g = _rmsnorm_bwd("norm_ffn_bwd", dh3, x2, g_ffn, dx3, low)

    do = _matmul_nt("xattn_do", [dx2_low], [full["w_o"]], low, after=ffn_flight[-1])
    (d_w_o,) = _matmul_tn("grad_w_o", o, [dx2_low], WIRE_DTYPE)
    dq, dk, dvv = _attn_bwd(q, k, vv, do)
    (d_w_q,) = _matmul_tn("grad_w_q", h2, [dq], WIRE_DTYPE)
    dx1, dx1_low, d_xattn_g = _matmul_nt_norm_bwd("xattn_dh_norm_bwd", dq, full["w_q"], x1, g_xattn, dx2, low)
    d_w_k, d_w_v = _matmul_tn("grad_w_kv", mn, [dk, dvv], WIRE_DTYPE)
    dmn = _matmul_nt("xattn_dmem", [dk, dvv], [full["w_k"], full["w_v"]], jnp.float32)
    (d_mem_g,) = _rmsnorm_bwd("norm_mem_bwd", dmn, mems, g_mem, None, None)

    (d_w_out,) = _matmul_tn("grad_w_out", z, [dx1_low], WIRE_DTYPE)
    attn_names = ("w_q", "w_k", "w_v", "w_o", "w_out")
    attn_swap = _swap_start("swap_start_attn", _as_shards([d_w_q, d_w_k, d_w_v, d_w_o, d_w_out]))
    dz = _matmul_nt("mix_dz", [dx1_low], [full["w_out"]], jnp.float32, after=attn_swap[-1])
    attn_flight = _reduce_scatter_start("attn", attn_names, *_swap_wait("swap_wait_attn", attn_swap, dz), pos)
    dproj, d_pw, d_ps, d_gs, d_ws, d_b = _mixer_bwd(proj, dz, pw, pool_scale, sgu_norm_g, ws, b_full)
    (d_w_in,) = _matmul_tn_cols("grad_w_in", h1, [dproj], N_CHIPS, WIRE_DTYPE, tka=1024, after=attn_flight[-1])
    d_pool_w = d_pw.reshape(G, N_CHIPS, PG // N_CHIPS, PG).transpose(1, 0, 2, 3).reshape(N_CHIPS, PG, PG).astype(WIRE_DTYPE)
    mix_names = ("w_in", "pool_w")
    mix_flight = _reduce_begin("mix", mix_names, [d_w_in, d_pool_w], pos)
    grad_x, d_mix_g = _matmul_nt_norm_bwd("mix_dh_norm_bwd", dproj, wg_in, xs, g_mix, dx1, None, tm=512, after=mix_flight[-1])

    small_parts = {
        "norm_mix_g": d_mix_g, "pool_scale": d_ps, "sgu_norm_g": d_gs, "w_spatial": d_ws, "b_spatial": d_b,
        "norm_xattn_g": d_xattn_g, "norm_mem_g": d_mem_g, "norm_ffn_g": d_ffn_g, "final_norm_g": d_final_g,
    }
    row_counts = [small_parts[n].size // LANES for n in SMALL]
    loss_rows = jnp.pad(loss_part, ((0, 7), (0, LANES - 1)))
    packed = jnp.concatenate([_as_lanes(small_parts[n]) for n in SMALL] + [loss_rows], axis=0)
    device = _device_index(lax.axis_index("x"), lax.axis_index("y"), lax.axis_index("c"))
    slots = lax.dynamic_update_slice(jnp.zeros((N_DEV,) + packed.shape, jnp.float32), packed[None], (device, 0, 0))
    small_flight = _exchange_start("small_grads_start", slots)

    groups = (("ffn", ffn_names, ffn_flight), ("attn", attn_names, attn_flight), ("mix", mix_names, mix_flight))
    joins, after = {}, small_flight[-1]
    for tag, names, flight in groups:
        reduced = _reduce_end(tag, names, flight, after, pos)
        joins[tag] = _join_start("join_start_" + tag, [reduced[n] for n in names])
        after = joins[tag][-1]

    out_grad, out_delta, out_m, out_v = {}, {}, {}, {}
    for tag, names, _ in groups:
        grads = [g.reshape(-1, g.shape[-1]) for g in _join_wait("join_wait_" + tag, joins[tag], after)]
        updates = _per_run(_adamw, names, [_as2d(weights[n]) for n in names], grads,
                           [_as2d(moments_m[n]) for n in names], [_as2d(moments_v[n]) for n in names])
        for n, update in zip(names, updates):
            shape = weights[n].shape
            out_grad[n], out_delta[n], out_m[n], out_v[n] = (a.reshape(shape) for a in update)
        after = updates[-1][1]
    small_sum = _sum_slots("small_grads_sum", _exchange_wait("small_grads_wait", small_flight, after))
    small_out = _adamw_small(
        small_sum, [_as_lanes(weights[n]) for n in SMALL], [_as_lanes(moments_m[n]) for n in SMALL],
        [_as_lanes(moments_v[n]) for n in SMALL], row_counts)
    for t, n in enumerate(SMALL):
        shape = weights[n].shape
        out_grad[n], out_delta[n], out_m[n], out_v[n] = (a.reshape(shape) for a in small_out[4 * t:4 * t + 4])

    loss = small_sum[sum(row_counts), 0]
    return (loss, grad_x[None], *[out_grad[n] for n in WEIGHTS], *[out_delta[n] for n in WEIGHTS],
            *[out_m[n] for n in WEIGHTS], *[out_v[n] for n in WEIGHTS])
```

```python
import math

import jax
import jax.numpy as jnp
from jax import lax
from jax.experimental import pallas as pl
from jax.experimental.pallas import tpu as pltpu

MXU_DTYPE = jnp.bfloat16
WIRE_DTYPE = jnp.bfloat16

EPS = 1e-6
CHUNK = 64
POOL_WINDOWS = (2, 4, 8, 16)
POOL_HALO = 16
N_XATTN_HEADS = 4
ADAM_LR = 0.001
ADAM_B1 = 0.9
ADAM_B2 = 0.999
ADAM_EPS = 1e-08
ADAM_WD = 0.01
ADAM_STEP = 10

N_CHIPS = 4
N_DEV = 8
LANES = 128
VMEM_LIMIT_BYTES = 56 * 2 ** 20
COPY_CHUNK_BYTES = 512 * 2 ** 10
COPY_ROW_ALIGN = 16

MESH = pl.DeviceIdType.MESH
ANY = pl.BlockSpec(memory_space=pl.ANY)
VMEM_WHOLE = pl.BlockSpec(memory_space=pltpu.VMEM)
HBM_SPEC = pl.BlockSpec(memory_space=pltpu.HBM)
SEM_SPEC = pl.BlockSpec(memory_space=pltpu.SEMAPHORE)
SIDE_EFFECT = pltpu.SideEffectType.DATAFLOW_SIDE_EFFECTING

NN = (((1,), (0,)), ((), ()))
NT = (((1,), (1,)), ((), ()))
TN = (((0,), (0,)), ((), ()))


def _tile(n, target, mult):
    best = None
    for t in range(mult, min(n, target) + 1, mult):
        if n % t == 0:
            best = t
    return n if best is None else best


def _params(n_grid_axes):
    return pltpu.CompilerParams(dimension_semantics=("arbitrary",) * n_grid_axes, vmem_limit_bytes=VMEM_LIMIT_BYTES)


def _rsqrt_mean_sq(x):
    return lax.rsqrt(jnp.mean(x * x, axis=-1, keepdims=True) + EPS)


def _sum_all(x):
    return jnp.sum(jnp.sum(x, axis=1, keepdims=True), axis=0, keepdims=True)


def _sigmoid(x):
    return 0.5 * jnp.tanh(0.5 * x) + 0.5


def _store_epilogue(accs, ex_refs, out_refs, i, j):
    for a, o in zip(accs, out_refs):
        o[...] = a.astype(o.dtype)


def _fused_matmul(name, grid, operands, pairs, acc_shapes, extras, outs, epilogue):
    nk = grid[2]
    n_op, n_ex, n_out, n_acc = len(operands), len(extras), len(outs), len(acc_shapes)
    in_place = nk > 1 and epilogue is _store_epilogue and all(o[1] == jnp.float32 for o in outs)

    def body(*refs):
        op_refs = refs[:n_op]
        ex_refs = refs[n_op:n_op + n_ex]
        out_refs = refs[n_op + n_ex:n_op + n_ex + n_out]
        acc_refs = out_refs if in_place else refs[n_op + n_ex + n_out:]
        i, j, k = pl.program_id(0), pl.program_id(1), pl.program_id(2)
        loaded = {}

        def operand(n):
            if n not in loaded:
                loaded[n] = op_refs[n][...].astype(MXU_DTYPE)
            return loaded[n]

        def product(ia, ib, dims):
            return lax.dot_general(operand(ia), operand(ib), dims, preferred_element_type=jnp.float32)

        def products():
            parts = [None] * n_acc
            for ia, ib, iacc, dims in pairs:
                d = product(ia, ib, dims)
                parts[iacc] = d if parts[iacc] is None else parts[iacc] + d
            return parts

        if nk == 1:
            epilogue(products(), ex_refs, out_refs, i, j)
            return

        @pl.when(k == 0)
        def _():
            for a in range(n_acc):
                acc_refs[a][...] = jnp.zeros_like(acc_refs[a])

        for a, part in enumerate(products()):
            acc_refs[a][...] += part

        if not in_place:
            @pl.when(k == nk - 1)
            def _():
                epilogue([acc_refs[a][...] for a in range(n_acc)], ex_refs, out_refs, i, j)

    arrays = [o[0] for o in operands] + [e[0] for e in extras]
    in_specs = [pl.BlockSpec(o[1], o[2], pipeline_mode=o[3] if len(o) > 3 else None) for o in operands]
    in_specs += [ANY if e[1] is None else pl.BlockSpec(e[1], e[2]) for e in extras]
    return pl.pallas_call(
        body,
        name=name,
        grid=grid,
        in_specs=in_specs,
        out_specs=[pl.BlockSpec(o[2], o[3]) for o in outs],
        out_shape=[jax.ShapeDtypeStruct(o[0], o[1]) for o in outs],
        scratch_shapes=[pltpu.VMEM(s, jnp.float32) for s in acc_shapes] if nk > 1 and not in_place else [],
        compiler_params=_params(3),
    )(*arrays)


def _matmul_nn(name, a, b, out_dtype, tm=2048, tn=1024, after=None):
    M, K = a.shape
    N = b.shape[1]
    tm, tn = _tile(M, tm, 16), _tile(N, tn, LANES)
    return _fused_matmul(
        name, (M // tm, N // tn, 1),
        [(a, (tm, K), lambda i, j, k: (i, 0)), (b, (K, tn), lambda i, j, k: (0, j))],
        [(0, 1, 0, NN)], [(tm, tn)], _order_after(after),
        [((M, N), out_dtype, (tm, tn), lambda i, j, k: (i, j))], _store_epilogue)[0]


def _matmul_nn_cols(name, a, b3, out_dtype, tm=2048):
    M, K = a.shape
    J, _, Nc = b3.shape
    tm = _tile(M, tm, 16)
    return _fused_matmul(
        name, (M // tm, J, 1),
        [(a, (tm, K), lambda i, j, k: (i, 0)), (b3, (None, K, Nc), lambda i, j, k: (j, 0, 0))],
        [(0, 1, 0, NN)], [(tm, Nc)], [],
        [((M, J * Nc), out_dtype, (tm, Nc), lambda i, j, k: (i, j))], _store_epilogue)[0]


def _order_after(after):
    return [] if after is None else [(after, None, None)]


def _matmul_nt(name, a_list, b_list, out_dtype, tm=2048, tn=1024, after=None):
    M, K = a_list[0].shape
    N = b_list[0].shape[0]
    tm, tn = _tile(M, tm, 16), _tile(N, tn, LANES)
    n = len(a_list)
    operands = [(a, (tm, K), lambda i, j, k: (i, 0)) for a in a_list]
    operands += [(b, (tn, K), lambda i, j, k: (j, 0)) for b in b_list]
    return _fused_matmul(
        name, (M // tm, N // tn, 1), operands,
        [(p, n + p, 0, NT) for p in range(n)], [(tm, tn)], _order_after(after),
        [((M, N), out_dtype, (tm, tn), lambda i, j, k: (i, j))], _store_epilogue)[0]


def _matmul_nt_cols(name, a_list, b3_list, out_dtype, tm=1024, tn=512, after=None):
    M = a_list[0].shape[0]
    J, N, Kc = b3_list[0].shape
    tm, tn = _tile(M, tm, 16), _tile(N, tn, LANES)
    n = len(a_list)
    operands = [(a, (tm, Kc), lambda i, j, k: (i, k)) for a in a_list]
    operands += [(b, (None, tn, Kc), lambda i, j, k: (k, j, 0)) for b in b3_list]
    return _fused_matmul(
        name, (M // tm, N // tn, J), operands,
        [(p, n + p, 0, NT) for p in range(n)], [(tm, tn)], _order_after(after),
        [((M, N), out_dtype, (tm, tn), lambda i, j, k: (i, j))], _store_epilogue)[0]


def _matmul_tn(name, a, b_list, out_dtype, tka=1024, tn=1024, tkm=4096):
    M, Ka = a.shape
    N = b_list[0].shape[1]
    tka, tn, tkm = _tile(Ka, tka, LANES), _tile(N, tn, LANES), _tile(M, tkm, 16)
    n = len(b_list)
    operands = [(a, (tkm, tka), lambda i, j, k: (k, i))]
    operands += [(b, (tkm, tn), lambda i, j, k: (k, j)) for b in b_list]
    return _fused_matmul(
        name, (Ka // tka, N // tn, M // tkm), operands,
        [(0, 1 + p, p, TN) for p in range(n)], [(tka, tn)] * n, [],
        [((Ka, N), out_dtype, (tka, tn), lambda i, j, k: (i, j))] * n, _store_epilogue)


def _matmul_tn_cols(name, a, b_list, n_shards, out_dtype, tka=512, tkm=4096, after=None):
    M, Ka = a.shape
    Nc = b_list[0].shape[1] // n_shards
    tka, tkm = _tile(Ka, tka, LANES), _tile(M, tkm, 16)
    n = len(b_list)
    operands = [(a, (tkm, tka), lambda i, j, k: (k, i))]
    operands += [(b, (tkm, Nc), lambda i, j, k: (k, j)) for b in b_list]
    return _fused_matmul(
        name, (Ka // tka, n_shards, M // tkm), operands,
        [(0, 1 + p, p, TN) for p in range(n)], [(tka, Nc)] * n, _order_after(after),
        [((n_shards, Ka, Nc), out_dtype, (None, tka, Nc), lambda i, j, k: (j, i, 0))] * n, _store_epilogue)


def _rmsnorm(name, x, g_row, out_dtype, tm=512, after=None):
    S, D = x.shape
    tm = _tile(S, tm, 16)

    def body(x_ref, g_ref, *rest):
        o_ref = rest[-1]
        xv = x_ref[...]
        o_ref[...] = (xv * _rsqrt_mean_sq(xv) * g_ref[...]).astype(o_ref.dtype)

    ordered = [] if after is None else [after]
    return pl.pallas_call(
        body, name=name, grid=(S // tm,),
        in_specs=[pl.BlockSpec((tm, D), lambda i: (i, 0)), pl.BlockSpec((1, D), lambda i: (0, 0))] + [ANY] * len(ordered),
        out_specs=pl.BlockSpec((tm, D), lambda i: (i, 0)),
        out_shape=jax.ShapeDtypeStruct((S, D), out_dtype),
        compiler_params=_params(1),
    )(x, g_row, *ordered)


def _rmsnorm_bwd_values(dh, xin, g):
    r = _rsqrt_mean_sq(xin)
    xhat = xin * r
    dhg = dh * g
    dx = r * (dhg - xhat * jnp.mean(dhg * xhat, axis=-1, keepdims=True))
    return dx, jnp.sum(dh * xhat, axis=0, keepdims=True)


def _rmsnorm_bwd(name, dh, xin, g_row, dres, low_dtype, tm=256):
    S, D = xin.shape
    tm = _tile(S, tm, 16)
    has_dx = dres is not None
    has_low = has_dx and low_dtype is not None

    def body(*refs):
        dh_ref, x_ref, g_ref = refs[:3]
        rest = refs[3:]
        dx, dg = _rmsnorm_bwd_values(dh_ref[...], x_ref[...], g_ref[...])
        if has_dx:
            dres_ref, rest = rest[0], rest[1:]
            dx = dx + dres_ref[...]
            rest[0][...] = dx
            if has_low:
                rest[1][...] = dx.astype(low_dtype)
        dg_ref = rest[-1]

        @pl.when(pl.program_id(0) == 0)
        def _():
            dg_ref[...] = jnp.zeros_like(dg_ref)

        dg_ref[...] += dg

    row = pl.BlockSpec((tm, D), lambda i: (i, 0))
    vec = pl.BlockSpec((1, D), lambda i: (0, 0))
    ins, in_specs = [dh, xin, g_row], [row, row, vec]
    out_shape, out_specs = [], []
    if has_dx:
        ins.append(dres)
        in_specs.append(row)
        out_shape.append(jax.ShapeDtypeStruct((S, D), jnp.float32))
        out_specs.append(row)
        if has_low:
            out_shape.append(jax.ShapeDtypeStruct((S, D), low_dtype))
            out_specs.append(row)
    out_shape.append(jax.ShapeDtypeStruct((1, D), jnp.float32))
    out_specs.append(vec)
    return pl.pallas_call(
        body, name=name, grid=(S // tm,), in_specs=in_specs, out_specs=out_specs, out_shape=out_shape,
        compiler_params=_params(1),
    )(*ins)


def _chunk_mask(blk):
    t = lax.broadcasted_iota(jnp.int32, (blk, blk), 0)
    s = lax.broadcasted_iota(jnp.int32, (blk, blk), 1)
    return (s // CHUNK) <= (t // CHUNK)


def _pool_inputs(a, halo, row0, tm, g, pg):
    cols = slice(g * pg, (g + 1) * pg)
    w = POOL_WINDOWS[g]
    s = jnp.concatenate([halo[:, cols], a[:, cols]], axis=0)
    span = 1
    while span < w:
        s = s + pltpu.roll(s, span, axis=0)
        span *= 2
    t = lax.broadcasted_iota(jnp.int32, (tm, 1), 0) + row0
    cnt = jnp.minimum(t + 1, w).astype(jnp.float32)
    return s[POOL_HALO:] / cnt - a[:, cols]


def _mixer_specs(S, tm, DP, DS):
    per_halo = tm // POOL_HALO
    n_halo = S // POOL_HALO
    a_spec = pl.BlockSpec((tm, DP), lambda i: (i, 0))
    prev_spec = pl.BlockSpec((POOL_HALO, DP), lambda i: (jnp.maximum(i * per_halo - 1, 0), 0))
    next_spec = pl.BlockSpec((POOL_HALO, DP), lambda i: (jnp.minimum((i + 1) * per_halo, n_halo - 1), 0))
    u_spec = pl.BlockSpec((tm, DS), lambda i: (i, 1))
    v_spec = pl.BlockSpec((tm, DS), lambda i: (i, 2))
    return a_spec, prev_spec, next_spec, u_spec, v_spec


def _whole(arr):
    nd = arr.ndim
    return pl.BlockSpec(arr.shape, lambda i: (0,) * nd)


def _mixer_fwd(proj, pw, ps_row, gs_row, ws, b_full, tm=512):
    S = proj.shape[0]
    G, PG, _ = pw.shape
    DP = G * PG
    H, BLK, _ = ws.shape
    DS = gs_row.shape[1]
    HD = DS // H
    assert DP == DS and proj.shape[1] == DP + 2 * DS
    tm = _tile(S, tm, BLK)

    def body(a_ref, prev_ref, u_ref, v_ref, pw_ref, ps_ref, gs_ref, ws_ref, b_ref, z_ref):
        i = pl.program_id(0)
        a = a_ref[...]
        halo = jnp.where(i > 0, prev_ref[...], 0.0)
        for g in range(G):
            cols = slice(g * PG, (g + 1) * PG)
            p = _pool_inputs(a, halo, i * tm, tm, g, PG)
            y = jnp.dot(p.astype(MXU_DTYPE), pw_ref[g], preferred_element_type=jnp.float32) * ps_ref[:, cols]
            z_ref[:, cols] = y.astype(z_ref.dtype)
        v = v_ref[...]
        vn = (v * _rsqrt_mean_sq(v) * gs_ref[...]).astype(MXU_DTYPE)
        u = u_ref[...]
        mask = _chunk_mask(BLK)
        for h in range(H):
            wm = jnp.where(mask, ws_ref[h], 0.0).astype(MXU_DTYPE)
            bias = b_ref[h]
            cols = slice(h * HD, (h + 1) * HD)
            for n in range(tm // BLK):
                rows = slice(n * BLK, (n + 1) * BLK)
                mixed = jnp.dot(wm, vn[rows, cols], preferred_element_type=jnp.float32) + bias
                z_ref[rows, DP + h * HD:DP + (h + 1) * HD] = (u[rows, cols] * mixed).astype(z_ref.dtype)

    a_spec, prev_spec, _, u_spec, v_spec = _mixer_specs(S, tm, DP, DS)
    return pl.pallas_call(
        body, name="mixer_fwd", grid=(S // tm,),
        in_specs=[a_spec, prev_spec, u_spec, v_spec, _whole(pw), _whole(ps_row), _whole(gs_row), _whole(ws), _whole(b_full)],
        out_specs=pl.BlockSpec((tm, DP + DS), lambda i: (i, 0)),
        out_shape=jax.ShapeDtypeStruct((S, DP + DS), MXU_DTYPE),
        compiler_params=_params(1),
    )(proj, proj, proj, proj, pw, ps_row, gs_row, ws, b_full)


def _mixer_bwd(proj, dz, pw, ps_row, gs_row, ws, b_full, tm=256):
    S = proj.shape[0]
    G, PG, _ = pw.shape
    DP = G * PG
    H, BLK, _ = ws.shape
    DS = gs_row.shape[1]
    HD = DS // H
    tm = _tile(S, tm, BLK)
    nb = S // tm
    ext = tm + POOL_HALO

    def body(a_ref, prev_ref, u_ref, v_ref, dzp_ref, dzp_next_ref, dzs_ref, pw_ref, ps_ref, gs_ref, ws_ref, b_ref,
             dproj_ref, dpw_ref, dps_ref, dgs_ref, dws_ref, db_ref, dvn_ref):
        i = pl.program_id(0)

        @pl.when(i == 0)
        def _():
            for r in (dpw_ref, dps_ref, dgs_ref, dws_ref, db_ref):
                r[...] = jnp.zeros_like(r)

        a = a_ref[...]
        halo = jnp.where(i > 0, prev_ref[...], 0.0)
        dyp = dzp_ref[...]
        dyp_next = jnp.where(i < nb - 1, dzp_next_ref[...], 0.0)
        dq_ext = jnp.concatenate([dyp, dyp_next], axis=0) * ps_ref[...]
        t_ext = lax.broadcasted_iota(jnp.int32, (ext, 1), 0) + i * tm
        for g in range(G):
            cols = slice(g * PG, (g + 1) * PG)
            w = POOL_WINDOWS[g]
            pb = _pool_inputs(a, halo, i * tm, tm, g, PG).astype(MXU_DTYPE)
            y_pre = jnp.dot(pb, pw_ref[g], preferred_element_type=jnp.float32)
            dps_ref[:, cols] += jnp.sum(dyp[:, cols] * y_pre, axis=0, keepdims=True)
            dqb = dq_ext[:, cols].astype(MXU_DTYPE)
            dpw_ref[g] += lax.dot_general(pb, dqb[:tm], TN, preferred_element_type=jnp.float32)
            dp = lax.dot_general(dqb, pw_ref[g], NT, preferred_element_type=jnp.float32)
            f = dp / jnp.minimum(t_ext + 1, w).astype(jnp.float32)
            span = 1
            while span < w:
                f = f + pltpu.roll(f, ext - span, axis=0)
                span *= 2
            dproj_ref[:, cols] = (f[:tm] - dp[:tm]).astype(dproj_ref.dtype)

        v = v_ref[...]
        rv = _rsqrt_mean_sq(v)
        vhat = v * rv
        gs = gs_ref[...]
        vnb = (vhat * gs).astype(MXU_DTYPE)
        u = u_ref[...]
        dys = dzs_ref[...]
        mask = _chunk_mask(BLK)
        for h in range(H):
            wm = jnp.where(mask, ws_ref[h], 0.0).astype(MXU_DTYPE)
            bias = b_ref[h]
            cols = slice(h * HD, (h + 1) * HD)
            dm_sum = jnp.zeros((BLK, HD), jnp.float32)
            dws_sum = jnp.zeros((BLK, BLK), jnp.float32)
            for n in range(tm // BLK):
                rows = slice(n * BLK, (n + 1) * BLK)
                vblk = vnb[rows, cols]
                mixed = jnp.dot(wm, vblk, preferred_element_type=jnp.float32) + bias
                dy = dys[rows, cols]
                dproj_ref[rows, DP + h * HD:DP + (h + 1) * HD] = (dy * mixed).astype(dproj_ref.dtype)
                dm = dy * u[rows, cols]
                dm_sum = dm_sum + dm
                dmb = dm.astype(MXU_DTYPE)
                dws_sum = dws_sum + lax.dot_general(dmb, vblk, NT, preferred_element_type=jnp.float32)
                dvn_ref[rows, cols] = lax.dot_general(wm, dmb, TN, preferred_element_type=jnp.float32)
            dws_ref[h] += jnp.where(mask, dws_sum, 0.0)
            db_ref[h:h + 1, :] += jnp.sum(dm_sum.T, axis=0, keepdims=True)
        dvn = dvn_ref[...]
        dvg = dvn * gs
        dv = rv * (dvg - vhat * jnp.mean(dvg * vhat, axis=-1, keepdims=True))
        dgs_ref[...] += jnp.sum(dvn * vhat, axis=0, keepdims=True)
        dproj_ref[:, DP + DS:] = dv.astype(dproj_ref.dtype)

    a_spec, prev_spec, next_spec, u_spec, v_spec = _mixer_specs(S, tm, DP, DS)
    dzp_spec = pl.BlockSpec((tm, DP), lambda i: (i, 0))
    dzs_spec = pl.BlockSpec((tm, DS), lambda i: (i, 1))
    small = [pw, ps_row, gs_row, ws, b_full]
    out_shape = [
        jax.ShapeDtypeStruct((S, DP + 2 * DS), MXU_DTYPE),
        jax.ShapeDtypeStruct(pw.shape, jnp.float32),
        jax.ShapeDtypeStruct(ps_row.shape, jnp.float32),
        jax.ShapeDtypeStruct(gs_row.shape, jnp.float32),
        jax.ShapeDtypeStruct(ws.shape, jnp.float32),
        jax.ShapeDtypeStruct((H, BLK), jnp.float32),
    ]
    out_specs = [pl.BlockSpec((tm, DP + 2 * DS), lambda i: (i, 0))] + [_whole(s) for s in out_shape[1:]]
    return pl.pallas_call(
        body, name="mixer_bwd", grid=(nb,),
        in_specs=[a_spec, prev_spec, u_spec, v_spec, dzp_spec, next_spec, dzs_spec] + [_whole(s) for s in small],
        out_specs=out_specs, out_shape=out_shape,
        scratch_shapes=[pltpu.VMEM((tm, DS), jnp.float32)],
        compiler_params=_params(1),
    )(proj, proj, proj, proj, dz, dz, dz, *small)


def _softmax_rows(q, k, scale):
    s = lax.dot_general(q, k, NT, preferred_element_type=jnp.float32) * scale
    e = jnp.exp(s - jnp.max(s, axis=-1, keepdims=True))
    return e / jnp.sum(e, axis=-1, keepdims=True)


def _attn_fwd(q, k, v, tm=1024):
    S, D = q.shape
    M = k.shape[0]
    HD = D // N_XATTN_HEADS
    scale = HD ** -0.5
    tm = _tile(S, tm, 16)

    def body(q_ref, k_ref, v_ref, o_ref):
        for h in range(N_XATTN_HEADS):
            cols = slice(h * HD, (h + 1) * HD)
            p = _softmax_rows(q_ref[:, cols], k_ref[:, cols], scale)
            o = jnp.dot(p.astype(MXU_DTYPE), v_ref[:, cols], preferred_element_type=jnp.float32)
            o_ref[:, cols] = o.astype(o_ref.dtype)

    row = pl.BlockSpec((tm, D), lambda i: (i, 0))
    mem = pl.BlockSpec((M, D), lambda i: (0, 0))
    return pl.pallas_call(
        body, name="attn_fwd", grid=(S // tm,), in_specs=[row, mem, mem], out_specs=row,
        out_shape=jax.ShapeDtypeStruct((S, D), MXU_DTYPE), compiler_params=_params(1),
    )(q, k, v)


def _attn_bwd(q, k, v, do, tm=1024):
    S, D = q.shape
    M = k.shape[0]
    HD = D // N_XATTN_HEADS
    scale = HD ** -0.5
    tm = _tile(S, tm, 16)

    def body(q_ref, k_ref, v_ref, do_ref, dq_ref, dk_ref, dv_ref):
        @pl.when(pl.program_id(0) == 0)
        def _():
            dk_ref[...] = jnp.zeros_like(dk_ref)
            dv_ref[...] = jnp.zeros_like(dv_ref)

        for h in range(N_XATTN_HEADS):
            cols = slice(h * HD, (h + 1) * HD)
            qh, kh, vh, doh = q_ref[:, cols], k_ref[:, cols], v_ref[:, cols], do_ref[:, cols]
            p = _softmax_rows(qh, kh, scale)
            dp = lax.dot_general(doh, vh, NT, preferred_element_type=jnp.float32)
            dv_ref[:, cols] += lax.dot_general(p.astype(MXU_DTYPE), doh, TN, preferred_element_type=jnp.float32)
            ds = (p * (dp - jnp.sum(dp * p, axis=-1, keepdims=True)) * scale).astype(MXU_DTYPE)
            dq_ref[:, cols] = jnp.dot(ds, kh, preferred_element_type=jnp.float32).astype(dq_ref.dtype)
            dk_ref[:, cols] += lax.dot_general(ds, qh, TN, preferred_element_type=jnp.float32)

    row = pl.BlockSpec((tm, D), lambda i: (i, 0))
    mem = pl.BlockSpec((M, D), lambda i: (0, 0))
    return pl.pallas_call(
        body, name="attn_bwd", grid=(S // tm,), in_specs=[row, mem, mem, row], out_specs=[row, mem, mem],
        out_shape=[jax.ShapeDtypeStruct((S, D), MXU_DTYPE), jax.ShapeDtypeStruct((M, D), jnp.float32),
                   jax.ShapeDtypeStruct((M, D), jnp.float32)],
        compiler_params=_params(1),
    )(q, k, v, do)


def _proj_residual_norm(name, a, w2d, resid, g_row, tm=512, after=None):
    M, K = a.shape
    N = w2d.shape[1]
    tm = _tile(M, tm, 16)

    def epilogue(accs, ex_refs, out_refs, i, j):
        x_new = ex_refs[0][...] + accs[0]
        out_refs[0][...] = x_new
        out_refs[1][...] = (x_new * _rsqrt_mean_sq(x_new) * ex_refs[1][...]).astype(MXU_DTYPE)

    row = lambda i, j, k: (i, 0)
    return _fused_matmul(
        name, (M // tm, 1, 1),
        [(a, (tm, K), row), (w2d, (K, N), lambda i, j, k: (0, 0))],
        [(0, 1, 0, NN)], [(tm, N)],
        [(resid, (tm, N), row), (g_row, (1, N), lambda i, j, k: (0, 0))] + _order_after(after),
        [((M, N), jnp.float32, (tm, N), row), ((M, N), MXU_DTYPE, (tm, N), row)], epilogue)


def _ffn_gate_up(h, wg3, wu3, tm=512):
    M, K = h.shape
    J, _, Nc = wg3.shape
    tm = _tile(M, tm, 16)

    def epilogue(accs, ex_refs, out_refs, i, j):
        gate, up = accs
        sg = _sigmoid(gate)
        silu = gate * sg
        out_refs[0][...] = (silu * up).astype(MXU_DTYPE)
        out_refs[1][...] = (up * (sg * (1.0 + gate * (1.0 - sg)))).astype(MXU_DTYPE)
        out_refs[2][...] = silu.astype(MXU_DTYPE)

    col = lambda i, j, k: (i, j)
    wspec = lambda i, j, k: (j, 0, 0)
    return _fused_matmul(
        "ffn_gate_up", (M // tm, J, 1),
        [(h, (tm, K), lambda i, j, k: (i, 0)), (wg3, (None, K, Nc), wspec), (wu3, (None, K, Nc), wspec)],
        [(0, 1, 0, NN), (0, 2, 1, NN)], [(tm, Nc)] * 2, [],
        [((M, J * Nc), MXU_DTYPE, (tm, Nc), col)] * 3, epilogue)


def _ffn_down(act, wd2d, x2, tm=512, tn=1024):
    M, F = act.shape
    D = wd2d.shape[1]
    tm, tn = _tile(M, tm, 16), _tile(D, tn, LANES)

    def epilogue(accs, ex_refs, out_refs, i, j):
        out_refs[0][...] = ex_refs[0][...] + accs[0]

    tile = lambda i, j, k: (i, j)
    return _fused_matmul(
        "ffn_down", (M // tm, D // tn, 1),
        [(act, (tm, F), lambda i, j, k: (i, 0)), (wd2d, (F, tn), lambda i, j, k: (0, j))],
        [(0, 1, 0, NN)], [(tm, tn)], [(x2, (tm, tn), tile)],
        [((M, D), jnp.float32, (tm, tn), tile)], epilogue)[0]


def _final_norm_loss(x3, target, g_row, tm=256):
    S, D = x3.shape
    tm = _tile(S, tm, 16)

    def body(x_ref, t_ref, g_ref, dx_ref, dxl_ref, loss_ref, dg_ref):
        x3v = x_ref[...]
        g = g_ref[...]
        r = _rsqrt_mean_sq(x3v)
        xhat = x3v * r
        diff = xhat * g - t_ref[...]
        dy = diff / D
        dyg = dy * g
        dx = r * (dyg - xhat * jnp.mean(dyg * xhat, axis=-1, keepdims=True))
        dx_ref[...] = dx
        dxl_ref[...] = dx.astype(MXU_DTYPE)

        @pl.when(pl.program_id(0) == 0)
        def _():
            loss_ref[...] = jnp.zeros_like(loss_ref)
            dg_ref[...] = jnp.zeros_like(dg_ref)

        loss_ref[...] += 0.5 * _sum_all(jnp.mean(diff * diff, axis=-1, keepdims=True))
        dg_ref[...] += jnp.sum(dy * xhat, axis=0, keepdims=True)

    row = pl.BlockSpec((tm, D), lambda i: (i, 0))
    vec = pl.BlockSpec((1, D), lambda i: (0, 0))
    return pl.pallas_call(
        body, name="final_norm_loss", grid=(S // tm,), in_specs=[row, row, vec],
        out_specs=[row, row, pl.BlockSpec((1, 1), lambda i: (0, 0)), vec],
        out_shape=[jax.ShapeDtypeStruct((S, D), jnp.float32), jax.ShapeDtypeStruct((S, D), MXU_DTYPE),
                   jax.ShapeDtypeStruct((1, 1), jnp.float32), jax.ShapeDtypeStruct((1, D), jnp.float32)],
        compiler_params=_params(1),
    )(x3, target, g_row)


def _ffn_down_bwd(dx3_low, wd2d, act_by_gate, act_by_up, n_shards, tm=1024):
    M, D = dx3_low.shape
    F = wd2d.shape[0]
    tn = F // n_shards
    tm = _tile(M, tm, 16)

    def epilogue(accs, ex_refs, out_refs, i, j):
        dact = accs[0]
        out_refs[0][...] = (dact * ex_refs[0][...].astype(jnp.float32)).astype(MXU_DTYPE)
        out_refs[1][...] = (dact * ex_refs[1][...].astype(jnp.float32)).astype(MXU_DTYPE)

    col = lambda i, j, k: (i, j)
    return _fused_matmul(
        "ffn_down_bwd", (M // tm, n_shards, 1),
        [(dx3_low, (tm, D), lambda i, j, k: (i, 0)), (wd2d, (tn, D), lambda i, j, k: (j, 0))],
        [(0, 1, 0, NT)], [(tm, tn)],
        [(act_by_gate, (tm, tn), col), (act_by_up, (tm, tn), col)],
        [((M, F), MXU_DTYPE, (tm, tn), col)] * 2, epilogue)


def _matmul_nt_norm_bwd(name, a, b, xin, g_row, dres, low_dtype, tm=256, after=None):
    M = a.shape[0]
    sharded = b.ndim == 3
    N, Kc = b.shape[-2:]
    J = b.shape[0] if sharded else 1
    tm = _tile(M, tm, 16)
    has_low = low_dtype is not None

    def epilogue(accs, ex_refs, out_refs, i, j):
        dx, dg = _rmsnorm_bwd_values(accs[0], ex_refs[0][...], ex_refs[1][...])
        dx = dx + ex_refs[2][...]
        out_refs[0][...] = dx
        if has_low:
            out_refs[1][...] = dx.astype(low_dtype)
        dg_ref = out_refs[-1]

        @pl.when(i == 0)
        def _():
            dg_ref[...] = jnp.zeros_like(dg_ref)

        dg_ref[...] += dg

    row = lambda i, j, k: (i, 0)
    const = lambda i, j, k: (0, 0)
    b_operand = (b, (None, N, Kc), lambda i, j, k: (k, 0, 0)) if sharded else (b, (N, Kc), const, pl.Buffered(1))
    outs = [((M, N), jnp.float32, (tm, N), row)]
    if has_low:
        outs.append(((M, N), low_dtype, (tm, N), row))
    outs.append(((1, N), jnp.float32, (1, N), const))
    return _fused_matmul(
        name, (M // tm, 1, J), [(a, (tm, Kc), lambda i, j, k: (i, k)), b_operand],
        [(0, 1, 0, NT)], [(tm, N)],
        [(xin, (tm, N), row), (g_row, (1, N), const), (dres, (tm, N), row)] + _order_after(after),
        outs, epilogue)


def _row_tile(R, C, target_bytes=2 ** 21):
    return _tile(R, max(16, target_bytes // (4 * C)), 16)


def _prefetch_call(body, name, grid, pos, in_specs, out_specs, out_shape, ins, after=None):
    ordered = [] if after is None else [after]
    grid_spec = pltpu.PrefetchScalarGridSpec(
        num_scalar_prefetch=1, grid=grid, in_specs=list(in_specs) + [ANY] * len(ordered), out_specs=out_specs)
    return pl.pallas_call(body, name=name, grid_spec=grid_spec, out_shape=out_shape,
                          compiler_params=_params(len(grid)))(pos, *ins, *ordered)


def _runs(arrays):
    runs = []
    for i, a in enumerate(arrays):
        if runs and arrays[runs[-1][-1]].shape == a.shape:
            runs[-1].append(i)
        else:
            runs.append([i])
    return runs


def _per_run(fn, names, *array_lists):
    out = [None] * len(names)
    for run in _runs(array_lists[0]):
        results = fn("_".join(names[i] for i in run), *[[arrays[i] for i in run] for arrays in array_lists])
        for i, r in zip(run, results):
            out[i] = r
    return out


def _cast_into_slabs(name, xs, dtype, pos, after=None):
    n = len(xs)
    R, C = xs[0].shape
    tr = _row_tile(R, C * n, 2 ** 22)

    def body(pos_ref, *refs):
        for t in range(n):
            refs[len(refs) - n + t][...] = refs[t][...].astype(dtype)

    return _prefetch_call(
        body, "cast_" + name, (R // tr,), pos, [pl.BlockSpec((tr, C), lambda i, pos: (i, 0))] * n,
        [pl.BlockSpec((None, tr, C), lambda i, pos: (pos[0], i, 0))] * n,
        [jax.ShapeDtypeStruct((N_CHIPS, R, C), dtype)] * n, xs, after)


def _chip_sum(name, partials, gots, dtype, pos):
    n = len(partials)
    J, R, C = partials[0].shape
    Rh = R // 2
    tr = _row_tile(Rh, C * n, 2 ** 23)

    def body(pos_ref, *refs):
        for t in range(n):
            refs[2 * n + t][...] = (refs[t][...].astype(jnp.float32) + refs[n + t][...].astype(jnp.float32)).astype(dtype)

    blk = pl.BlockSpec((None, tr, C), lambda j, i, pos: (j, i, 0))
    return _prefetch_call(
        body, "chip_sum_" + name, (J, Rh // tr), pos,
        [pl.BlockSpec((None, None, tr, C), lambda j, i, pos: (j, pos[1], i, 0))] * n + [blk] * n, [blk] * n,
        [jax.ShapeDtypeStruct((J, Rh, C), dtype)] * n, [p.reshape(J, 2, Rh, C) for p in partials] + list(gots))


def _owner_sum(name, chip_sums, others, pos):
    n = len(chip_sums)
    J, Rh, C = chip_sums[0].shape
    n_other = others[0].shape[0]
    tr = _row_tile(Rh, C * (n_other + 1) * n, 2 ** 23)

    def body(pos_ref, *refs):
        for t in range(n):
            acc = refs[t][...].astype(jnp.float32)
            for k in range(n_other):
                acc = acc + refs[n + t][k].astype(jnp.float32)
            refs[2 * n + t][...] = acc

    return _prefetch_call(
        body, "owner_sum_" + name, (Rh // tr,), pos,
        [pl.BlockSpec((None, tr, C), lambda i, pos: (pos[0], i, 0))] * n
        + [pl.BlockSpec((n_other, tr, C), lambda i, pos: (0, i, 0))] * n,
        [pl.BlockSpec((None, tr, C), lambda i, pos: (pos[1], i, 0))] * n,
        [jax.ShapeDtypeStruct((2, Rh, C), jnp.float32)] * n, list(chip_sums) + list(others))


def _adamw_values(w, g, m, v):
    m = ADAM_B1 * m + (1.0 - ADAM_B1) * g
    v = ADAM_B2 * v + (1.0 - ADAM_B2) * (g * g)
    m_hat = m / (1.0 - ADAM_B1 ** ADAM_STEP)
    v_hat = v / (1.0 - ADAM_B2 ** ADAM_STEP)
    delta = -ADAM_LR * (m_hat / (jnp.sqrt(v_hat) + ADAM_EPS) + ADAM_WD * w)
    return delta, m, v


def _adamw(name, ws, gs, ms, vs):
    n = len(ws)
    R, C = ws[0].shape
    tr = _row_tile(R, C * n, 2 ** 21)

    def body(*refs):
        w_refs, g_refs, m_refs, v_refs = (refs[q * n:(q + 1) * n] for q in range(4))
        outs = refs[4 * n:]
        for t in range(n):
            g = g_refs[t][...]
            outs[4 * t][...] = g
            outs[4 * t + 1][...], outs[4 * t + 2][...], outs[4 * t + 3][...] = _adamw_values(
                w_refs[t][...], g, m_refs[t][...], v_refs[t][...])

    blk = pl.BlockSpec((tr, C), lambda i: (i, 0))
    res = pl.pallas_call(
        body, name="adamw_" + name, grid=(R // tr,), in_specs=[blk] * (4 * n), out_specs=[blk] * (4 * n),
        out_shape=[jax.ShapeDtypeStruct((R, C), jnp.float32)] * (4 * n), compiler_params=_params(1),
    )(*ws, *gs, *ms, *vs)
    return [tuple(res[4 * t:4 * t + 4]) for t in range(n)]


def _adamw_small(g_packed, ws, ms, vs, row_counts):
    n = len(ws)

    def body(*refs):
        g_ref = refs[0]
        w_refs, m_refs, v_refs = refs[1:1 + n], refs[1 + n:1 + 2 * n], refs[1 + 2 * n:1 + 3 * n]
        outs = refs[1 + 3 * n:]
        off = 0
        for t in range(n):
            g = g_ref[off:off + row_counts[t], :]
            off += row_counts[t]
            delta, m_new, v_new = _adamw_values(w_refs[t][...], g, m_refs[t][...], v_refs[t][...])
            outs[4 * t][...] = g
            outs[4 * t + 1][...] = delta
            outs[4 * t + 2][...] = m_new
            outs[4 * t + 3][...] = v_new

    out_shape = []
    for t in range(n):
        out_shape += [jax.ShapeDtypeStruct((row_counts[t], LANES), jnp.float32)] * 4
    return pl.pallas_call(
        body, name="adamw_small", in_specs=[VMEM_WHOLE] * (1 + 3 * n), out_specs=[VMEM_WHOLE] * (4 * n),
        out_shape=out_shape, compiler_params=pltpu.CompilerParams(vmem_limit_bytes=VMEM_LIMIT_BYTES),
    )(g_packed, *ws, *ms, *vs)


def _position():
    x, y, c = lax.axis_index("x"), lax.axis_index("y"), lax.axis_index("c")
    other_chips = [(1 - x, y), (x, 1 - y), (1 - x, 1 - y)]
    return x, y, c, other_chips


def _half(rows, which):
    return pl.ds(which * (rows // 2), rows // 2)


def _chunk_index(shape, dtype):
    rows = shape[-2]
    total = math.prod(shape) * jnp.dtype(dtype).itemsize
    n = max(1, min(rows // COPY_ROW_ALIGN, total // COPY_CHUNK_BYTES))
    step = -(-rows // n)
    step = -(-step // COPY_ROW_ALIGN) * COPY_ROW_ALIGN
    lead = (slice(None),) * (len(shape) - 2)
    return [lead + (pl.ds(s, min(step, rows - s)),) for s in range(0, rows, step)]


def _start_in_chunks(make, src, dst, waited_here=True):
    for idx in _chunk_index(src.shape, src.dtype):
        make(src.at[idx], dst.at[idx]).start()
    return make(src, dst) if waited_here else None


def _remote_copier(send_sem, recv_sem, to):
    return lambda src, dst: pltpu.make_async_remote_copy(
        src_ref=src, dst_ref=dst, send_sem=send_sem, recv_sem=recv_sem, device_id=to, device_id_type=MESH)


def _slab_half(ref, chip, which):
    return ref.at[chip, _half(ref.shape[1], which)]


def _slab_quarter(ref, chip, which, q):
    rows = ref.shape[1] // 4
    return ref.at[chip, pl.ds(which * 2 * rows + q * rows, rows)]


def _gather_step(name, arrived=(), ring=(), direct=(), after=None):
    groups = [g[2] for g in arrived] + list(direct) + list(ring)
    per_tensor = [2] * len(arrived) + [3] * len(direct) + [2] * len(ring)
    flat = [a for g in groups for a in g]
    T, G, A = len(flat), len(groups), len(arrived)
    ordered = [] if after is None else [after]
    sems_in = [sem for g in arrived for sem in g[:2]]

    def body(*refs):
        ins = refs[:T]
        first_hop = refs[T:T + 2 * A]
        out_sems = refs[T + 2 * A + len(ordered):T + 2 * A + len(ordered) + 2 * G]
        token = refs[-1]
        x, y, c, other_chips = _position()
        mine = 2 * x + y
        x_nbr, y_nbr = (1 - x, y, c), (x, 1 - y, c)
        from_x, from_y = 2 * (1 - x) + y, 2 * x + (1 - y)
        t0 = 0
        for gi, group in enumerate(groups):
            send, recv = out_sems[2 * gi], out_sems[2 * gi + 1]
            for t in range(len(group)):
                ref = ins[t0 + t]
                own = _slab_half(ref, mine, c)
                if gi < A:
                    send1, recv1 = first_hop[2 * gi], first_hop[2 * gi + 1]
                    for k, (peer, chip) in enumerate(((x_nbr, from_x), (y_nbr, from_y))):
                        cp = _remote_copier(send1.at[2 * t + k], recv1.at[2 * t + k], peer)(own, _slab_half(ref, chip, c))
                        cp.wait_send()
                        cp.wait_recv()
                    for k, (peer, chip) in enumerate(((y_nbr, from_x), (x_nbr, from_y))):
                        piece = _slab_quarter(ref, chip, c, k)
                        _start_in_chunks(_remote_copier(send.at[2 * t + k], recv.at[2 * t + k], peer), piece, piece,
                                         waited_here=False)
                elif gi < A + len(direct):
                    for k, (cx, cy) in enumerate(other_chips):
                        _start_in_chunks(_remote_copier(send.at[3 * t + k], recv.at[3 * t + k], (cx, cy, c)), own, own,
                                         waited_here=False)
                else:
                    for k, peer in enumerate((x_nbr, y_nbr)):
                        _start_in_chunks(_remote_copier(send.at[2 * t + k], recv.at[2 * t + k], peer), own, own,
                                         waited_here=False)
            t0 += len(group)
        token[...] = jnp.zeros_like(token)

    sem_shapes = []
    for group, n in zip(groups, per_tensor):
        sem_shapes += [pltpu.SemaphoreType.DMA((n * len(group),))] * 2
    res = pl.pallas_call(
        body, name=name, in_specs=[HBM_SPEC] * T + [SEM_SPEC] * (2 * A) + [ANY] * len(ordered),
        out_specs=[SEM_SPEC] * (2 * G) + [HBM_SPEC] * T + [VMEM_WHOLE],
        out_shape=sem_shapes + [pltpu.HBM(a.shape, a.dtype) for a in flat] + [jax.ShapeDtypeStruct((8, LANES), jnp.float32)],
        input_output_aliases={t: 2 * G + t for t in range(T)},
        compiler_params=pltpu.CompilerParams(has_side_effects=SIDE_EFFECT),
    )(*[pltpu.with_memory_space_constraint(a, pltpu.HBM) for a in flat], *sems_in, *ordered)
    flights, t0 = [], 2 * G
    for gi, group in enumerate(groups):
        flights.append((res[2 * gi], res[2 * gi + 1], res[t0:t0 + len(group)]))
        t0 += len(group)
    return flights[:A], flights[A:A + len(direct)], flights[A + len(direct):], res[-1]


def _relayed_wait(name, flight, after):
    send_sems, recv_sems, slabs = flight
    T = len(slabs)

    def body(*refs):
        ins = refs[:T]
        send, recv = refs[T:T + 2]
        x, y, c, _ = _position()
        x_nbr, y_nbr = (1 - x, y, c), (x, 1 - y, c)
        from_x, from_y, diagonal = 2 * (1 - x) + y, 2 * x + (1 - y), 2 * (1 - x) + (1 - y)
        for t in range(T):
            for k, (peer, chip) in enumerate(((y_nbr, from_x), (x_nbr, from_y))):
                cp = _remote_copier(send.at[2 * t + k], recv.at[2 * t + k], peer)(
                    _slab_quarter(ins[t], chip, c, k), _slab_quarter(ins[t], diagonal, c, k))
                cp.wait_send()
                cp.wait_recv()

    return pl.pallas_call(
        body, name=name, in_specs=[HBM_SPEC] * T + [SEM_SPEC, SEM_SPEC, ANY], out_specs=[HBM_SPEC] * T,
        out_shape=[pltpu.HBM(a.shape, a.dtype) for a in slabs], input_output_aliases={t: t for t in range(T)},
        compiler_params=pltpu.CompilerParams(has_side_effects=SIDE_EFFECT),
    )(*slabs, send_sems, recv_sems, after)


def _gather_wait(name, flight, after):
    send_sems, recv_sems, slabs = flight
    T = len(slabs)

    def body(*refs):
        ins = refs[:T]
        send, recv = refs[T:T + 2]
        x, y, c, other_chips = _position()
        mine = 2 * x + y
        for t in range(T):
            own = _slab_half(ins[t], mine, c)
            for k, (cx, cy) in enumerate(other_chips):
                landed = _slab_half(ins[t], 2 * cx + cy, c)
                cp = _remote_copier(send.at[3 * t + k], recv.at[3 * t + k], (cx, cy, c))(own, landed)
                cp.wait_send()
                cp.wait_recv()

    return pl.pallas_call(
        body, name=name, in_specs=[HBM_SPEC] * T + [SEM_SPEC, SEM_SPEC, ANY], out_specs=[HBM_SPEC] * T,
        out_shape=[pltpu.HBM(s.shape, s.dtype) for s in slabs], input_output_aliases={t: t for t in range(T)},
        compiler_params=pltpu.CompilerParams(has_side_effects=SIDE_EFFECT),
    )(*slabs, send_sems, recv_sems, after)


def _share_with_sibling(name, slabs):
    T = len(slabs)

    def body(*refs):
        outs = refs[T:2 * T]
        send_sems, recv_sems = refs[2 * T:]
        x, y, c, other_chips = _position()
        sibling = (x, y, 1 - c)
        started = []
        for t in range(T):
            for k, (cx, cy) in enumerate(other_chips):
                landed = _slab_half(outs[t], 2 * cx + cy, c)
                started.append(_start_in_chunks(_remote_copier(send_sems.at[t, k], recv_sems.at[t, k], sibling), landed, landed))
        for t in range(T):
            for k, (cx, cy) in enumerate(other_chips):
                arriving = _slab_half(outs[t], 2 * cx + cy, 1 - c)
                _remote_copier(send_sems.at[t, k], recv_sems.at[t, k], sibling)(arriving, arriving).wait_recv()
        for cp in started:
            cp.wait_send()

    return pl.pallas_call(
        body, name=name, in_specs=[ANY] * T, out_specs=[ANY] * T,
        out_shape=[jax.ShapeDtypeStruct(s.shape, s.dtype) for s in slabs],
        input_output_aliases={t: t for t in range(T)},
        scratch_shapes=[pltpu.SemaphoreType.DMA((T, 3)), pltpu.SemaphoreType.DMA((T, 3))],
    )(*slabs)


def _share_start(name, slabs):
    T = len(slabs)

    def body(*refs):
        ins = refs[:T]
        send_sems, recv_sems = refs[T:T + 2]
        token = refs[-1]
        x, y, c, other_chips = _position()
        for t in range(T):
            for k, (cx, cy) in enumerate(other_chips):
                landed = _slab_half(ins[t], 2 * cx + cy, c)
                _start_in_chunks(_remote_copier(send_sems.at[3 * t + k], recv_sems.at[3 * t + k], (x, y, 1 - c)),
                                 landed, landed, waited_here=False)
        token[...] = jnp.zeros_like(token)

    res = pl.pallas_call(
        body, name=name, in_specs=[HBM_SPEC] * T,
        out_specs=[SEM_SPEC, SEM_SPEC] + [HBM_SPEC] * T + [VMEM_WHOLE],
        out_shape=[pltpu.SemaphoreType.DMA((3 * T,)), pltpu.SemaphoreType.DMA((3 * T,))]
        + [pltpu.HBM(a.shape, a.dtype) for a in slabs] + [jax.ShapeDtypeStruct((8, LANES), jnp.float32)],
        input_output_aliases={t: 2 + t for t in range(T)},
        compiler_params=pltpu.CompilerParams(has_side_effects=SIDE_EFFECT),
    )(*[pltpu.with_memory_space_constraint(a, pltpu.HBM) for a in slabs])
    return res[0], res[1], res[2:2 + T], res[-1]


def _share_wait(name, flight, after):
    send_sems, recv_sems, slabs, _ = flight
    T = len(slabs)

    def body(*refs):
        ins = refs[:T]
        send, recv = refs[T:T + 2]
        x, y, c, other_chips = _position()
        for t in range(T):
            for k, (cx, cy) in enumerate(other_chips):
                cp = _remote_copier(send.at[3 * t + k], recv.at[3 * t + k], (x, y, 1 - c))(
                    _slab_half(ins[t], 2 * cx + cy, c), _slab_half(ins[t], 2 * cx + cy, 1 - c))
                cp.wait_send()
                cp.wait_recv()

    return pl.pallas_call(
        body, name=name, in_specs=[HBM_SPEC] * T + [SEM_SPEC, SEM_SPEC, ANY], out_specs=[HBM_SPEC] * T,
        out_shape=[pltpu.HBM(a.shape, a.dtype) for a in slabs], input_output_aliases={t: t for t in range(T)},
        compiler_params=pltpu.CompilerParams(has_side_effects=SIDE_EFFECT),
    )(*slabs, send_sems, recv_sems, after)


def _swap_halves(name, grads):
    T = len(grads)

    def body(*refs):
        ins, got = refs[:T], refs[T:2 * T]
        send_sems, recv_sems = refs[2 * T:]
        x, y, c, _ = _position()
        remote = []
        for t in range(T):
            rows = ins[t].shape[1]
            remote.append(_start_in_chunks(
                _remote_copier(send_sems.at[t], recv_sems.at[t], (x, y, 1 - c)), ins[t].at[:, _half(rows, 1 - c)], got[t]))
        for cp in remote:
            cp.wait()

    return pl.pallas_call(
        body, name=name, in_specs=[ANY] * T, out_specs=[ANY] * T,
        out_shape=[jax.ShapeDtypeStruct((g.shape[0], g.shape[1] // 2, g.shape[2]), g.dtype) for g in grads],
        scratch_shapes=[pltpu.SemaphoreType.DMA((T,)), pltpu.SemaphoreType.DMA((T,))],
    )(*grads)


def _swap_start(name, grads):
    T = len(grads)
    lands = [lax.empty((g.shape[0], g.shape[1] // 2, g.shape[2]), g.dtype) for g in grads]

    def body(*refs):
        ins, got = refs[:T], refs[T:2 * T]
        send_sems, recv_sems = refs[2 * T:2 * T + 2]
        token = refs[-1]
        x, y, c, _ = _position()
        for t in range(T):
            _start_in_chunks(_remote_copier(send_sems.at[t], recv_sems.at[t], (x, y, 1 - c)),
                             ins[t].at[:, _half(ins[t].shape[1], 1 - c)], got[t], waited_here=False)
        token[...] = jnp.zeros_like(token)

    arrays = list(grads) + lands
    res = pl.pallas_call(
        body, name=name, in_specs=[HBM_SPEC] * (2 * T),
        out_specs=[SEM_SPEC, SEM_SPEC] + [HBM_SPEC] * (2 * T) + [VMEM_WHOLE],
        out_shape=[pltpu.SemaphoreType.DMA((T,)), pltpu.SemaphoreType.DMA((T,))]
        + [pltpu.HBM(a.shape, a.dtype) for a in arrays] + [jax.ShapeDtypeStruct((8, LANES), jnp.float32)],
        input_output_aliases={i: 2 + i for i in range(2 * T)},
        compiler_params=pltpu.CompilerParams(has_side_effects=SIDE_EFFECT),
    )(*[pltpu.with_memory_space_constraint(a, pltpu.HBM) for a in arrays])
    return res[0], res[1], res[2:2 + T], res[2 + T:2 + 2 * T], res[-1]


def _swap_wait(name, flight, after):
    send_sems, recv_sems, grads, lands, _ = flight
    T = len(grads)

    def body(*refs):
        ins, got = refs[:T], refs[T:2 * T]
        send, recv = refs[2 * T:2 * T + 2]
        x, y, c, _ = _position()
        for t in range(T):
            cp = _remote_copier(send.at[t], recv.at[t], (x, y, 1 - c))(ins[t].at[:, _half(ins[t].shape[1], 1 - c)], got[t])
            cp.wait_send()
            cp.wait_recv()

    arrays = list(grads) + list(lands)
    res = pl.pallas_call(
        body, name=name, in_specs=[HBM_SPEC] * (2 * T) + [SEM_SPEC, SEM_SPEC, ANY], out_specs=[HBM_SPEC] * (2 * T),
        out_shape=[pltpu.HBM(a.shape, a.dtype) for a in arrays], input_output_aliases={i: i for i in range(2 * T)},
        compiler_params=pltpu.CompilerParams(has_side_effects=SIDE_EFFECT),
    )(*arrays, send_sems, recv_sems, after)
    return res[:T], res[T:]


def _scatter_start(name, chip_sums):
    T = len(chip_sums)
    lands = [lax.empty((3,) + s.shape[1:], s.dtype) for s in chip_sums]

    def body(*refs):
        srcs, dsts = refs[:T], refs[T:2 * T]
        send_sems, recv_sems = refs[2 * T:2 * T + 2]
        token = refs[-1]
        x, y, c, other_chips = _position()
        for t in range(T):
            for k, (cx, cy) in enumerate(other_chips):
                _start_in_chunks(_remote_copier(send_sems.at[3 * t + k], recv_sems.at[3 * t + k], (cx, cy, c)),
                                 srcs[t].at[2 * cx + cy], dsts[t].at[k], waited_here=False)
        token[...] = jnp.zeros_like(token)

    hbm = [pltpu.HBM(a.shape, a.dtype) for a in list(chip_sums) + lands]
    res = pl.pallas_call(
        body, name=name, in_specs=[HBM_SPEC] * (2 * T),
        out_specs=[SEM_SPEC, SEM_SPEC] + [HBM_SPEC] * (2 * T) + [VMEM_WHOLE],
        out_shape=[pltpu.SemaphoreType.DMA((3 * T,)), pltpu.SemaphoreType.DMA((3 * T,))] + hbm
        + [jax.ShapeDtypeStruct((8, LANES), jnp.float32)],
        input_output_aliases={i: 2 + i for i in range(2 * T)},
        compiler_params=pltpu.CompilerParams(has_side_effects=SIDE_EFFECT),
    )(*[pltpu.with_memory_space_constraint(a, pltpu.HBM) for a in list(chip_sums) + lands])
    return res[0], res[1], res[2:2 + T], res[2 + T:2 + 2 * T], res[-1]


def _scatter_wait(name, send_sems, recv_sems, chip_sums, lands, after):
    T = len(chip_sums)

    def body(*refs):
        srcs, dsts = refs[:T], refs[T:2 * T]
        send, recv = refs[2 * T:2 * T + 2]
        x, y, c, other_chips = _position()
        for t in range(T):
            for k, (cx, cy) in enumerate(other_chips):
                cp = _remote_copier(send.at[3 * t + k], recv.at[3 * t + k], (cx, cy, c))(srcs[t].at[2 * cx + cy], dsts[t].at[k])
                cp.wait_send()
                cp.wait_recv()

    arrays = list(chip_sums) + list(lands)
    res = pl.pallas_call(
        body, name=name, in_specs=[HBM_SPEC] * (2 * T) + [SEM_SPEC, SEM_SPEC, ANY],
        out_specs=[HBM_SPEC] * (2 * T), out_shape=[pltpu.HBM(a.shape, a.dtype) for a in arrays],
        input_output_aliases={i: i for i in range(2 * T)},
        compiler_params=pltpu.CompilerParams(has_side_effects=SIDE_EFFECT),
    )(*arrays, send_sems, recv_sems, after)
    return res[:T], res[T:]


def _as_shards(partial):
    return [p.reshape((N_CHIPS, -1, p.shape[-1])) for p in partial]


def _reduce_begin(tag, names, partial, pos):
    parts = _as_shards(partial)
    received = _swap_halves("swap_grad_halves_" + tag, parts)
    return _reduce_scatter_start(tag, names, parts, received, pos)


def _reduce_scatter_start(tag, names, parts, received, pos):
    chip_sums = _per_run(lambda name, p, r: _chip_sum(name, p, r, WIRE_DTYPE, pos), names, list(parts), list(received))
    return _scatter_start("scatter_start_" + tag, chip_sums)


def _reduce_end(tag, names, flight, after, pos):
    send_sems, recv_sems, chip_sums, lands, _ = flight
    chip_sums, others = _scatter_wait("scatter_wait_" + tag, send_sems, recv_sems, chip_sums, lands, after)
    reduced = _per_run(lambda name, cs, ot: _owner_sum(name, cs, ot, pos), names, list(chip_sums), list(others))
    return dict(zip(names, reduced))


def _join_start(name, halves):
    T = len(halves)

    def body(*refs):
        ins = refs[:T]
        send_sems, recv_sems = refs[T:T + 2]
        token = refs[-1]
        x, y, c, _ = _position()
        for t in range(T):
            _start_in_chunks(_remote_copier(send_sems.at[t], recv_sems.at[t], (x, y, 1 - c)),
                             ins[t].at[c], ins[t].at[c], waited_here=False)
        token[...] = jnp.zeros_like(token)

    res = pl.pallas_call(
        body, name=name, in_specs=[HBM_SPEC] * T,
        out_specs=[SEM_SPEC, SEM_SPEC] + [HBM_SPEC] * T + [VMEM_WHOLE],
        out_shape=[pltpu.SemaphoreType.DMA((T,)), pltpu.SemaphoreType.DMA((T,))]
        + [pltpu.HBM(h.shape, h.dtype) for h in halves] + [jax.ShapeDtypeStruct((8, LANES), jnp.float32)],
        input_output_aliases={t: 2 + t for t in range(T)},
        compiler_params=pltpu.CompilerParams(has_side_effects=SIDE_EFFECT),
    )(*[pltpu.with_memory_space_constraint(h, pltpu.HBM) for h in halves])
    return res[0], res[1], res[2:2 + T], res[-1]


def _join_wait(name, flight, after):
    send_sems, recv_sems, halves, _ = flight
    T = len(halves)

    def body(*refs):
        ins = refs[:T]
        send, recv = refs[T:T + 2]
        x, y, c, _ = _position()
        for t in range(T):
            cp = _remote_copier(send.at[t], recv.at[t], (x, y, 1 - c))(ins[t].at[c], ins[t].at[1 - c])
            cp.wait_send()
            cp.wait_recv()

    return pl.pallas_call(
        body, name=name, in_specs=[HBM_SPEC] * T + [SEM_SPEC, SEM_SPEC, ANY], out_specs=[HBM_SPEC] * T,
        out_shape=[pltpu.HBM(h.shape, h.dtype) for h in halves], input_output_aliases={t: t for t in range(T)},
        compiler_params=pltpu.CompilerParams(has_side_effects=SIDE_EFFECT),
    )(*halves, send_sems, recv_sems, after)


def _peers():
    x, y, c, _ = _position()
    flips = [(fx, fy, fc) for fx in (0, 1) for fy in (0, 1) for fc in (0, 1)][1:]
    return [(jnp.bitwise_xor(x, fx), jnp.bitwise_xor(y, fy), jnp.bitwise_xor(c, fc)) for fx, fy, fc in flips]


def _device_index(px, py, pc):
    return 4 * px + 2 * py + pc


def _exchange_start(name, slots):
    def body(slots_ref, send_sems, recv_sems, thru, token):
        x, y, c, _ = _position()
        own = slots_ref.at[_device_index(x, y, c)]
        for k, peer in enumerate(_peers()):
            _remote_copier(send_sems.at[k], recv_sems.at[k], peer)(own, own).start()
        token[...] = jnp.zeros_like(token)

    res = pl.pallas_call(
        body, name=name, in_specs=[HBM_SPEC], out_specs=[SEM_SPEC, SEM_SPEC, HBM_SPEC, VMEM_WHOLE],
        out_shape=[pltpu.SemaphoreType.DMA((N_DEV - 1,)), pltpu.SemaphoreType.DMA((N_DEV - 1,)),
                   pltpu.HBM(slots.shape, slots.dtype), jax.ShapeDtypeStruct((8, LANES), jnp.float32)],
        input_output_aliases={0: 2}, compiler_params=pltpu.CompilerParams(has_side_effects=SIDE_EFFECT),
    )(pltpu.with_memory_space_constraint(slots, pltpu.HBM))
    return res[0], res[1], res[2], res[3]


def _exchange_wait(name, flight, after):
    send_sems, recv_sems, slots, _ = flight

    def body(slots_ref, send, recv, after_ref, out_ref):
        x, y, c, _ = _position()
        own = slots_ref.at[_device_index(x, y, c)]
        for k, peer in enumerate(_peers()):
            cp = _remote_copier(send.at[k], recv.at[k], peer)(own, slots_ref.at[_device_index(*peer)])
            cp.wait_send()
            cp.wait_recv()

    return pl.pallas_call(
        body, name=name, in_specs=[HBM_SPEC, SEM_SPEC, SEM_SPEC, ANY], out_specs=HBM_SPEC,
        out_shape=pltpu.HBM(slots.shape, slots.dtype), input_output_aliases={0: 0},
        compiler_params=pltpu.CompilerParams(has_side_effects=SIDE_EFFECT),
    )(slots, send_sems, recv_sems, after)


def _sum_slots(name, slots):
    n, rows, lanes = slots.shape
    tr = rows if slots.size * 4 <= 2 ** 23 else _row_tile(rows, lanes * n)

    def body(s_ref, o_ref):
        acc = s_ref[0]
        for d in range(1, n):
            acc = acc + s_ref[d]
        o_ref[...] = acc

    return pl.pallas_call(
        body, name=name, grid=(rows // tr,), in_specs=[pl.BlockSpec((n, tr, lanes), lambda i: (0, i, 0))],
        out_specs=pl.BlockSpec((tr, lanes), lambda i: (i, 0)), out_shape=jax.ShapeDtypeStruct((rows, lanes), jnp.float32),
        compiler_params=_params(1))(slots)


BIG = ("w_in", "pool_w", "w_out", "w_q", "w_k", "w_v", "w_o", "w_gate", "w_up", "w_down")
SMALL = ("norm_mix_g", "pool_scale", "sgu_norm_g", "w_spatial", "b_spatial", "norm_xattn_g", "norm_mem_g",
         "norm_ffn_g", "final_norm_g")
GATHER_ORDER = (("w_in", "pool_w"), ("w_out",), ("w_q", "w_k", "w_v", "w_o"), ("w_gate", "w_up"), ("w_down",))
WEIGHTS = ("norm_mix_g", "w_in", "pool_w", "pool_scale", "sgu_norm_g", "w_spatial", "b_spatial", "w_out",
           "norm_xattn_g", "norm_mem_g", "w_q", "w_k", "w_v", "w_o", "norm_ffn_g", "w_gate", "w_up", "w_down",
           "final_norm_g")


def _as2d(a):
    return a.reshape(-1, a.shape[-1])


def _as_lanes(a):
    return a.reshape(-1, LANES)


def kernel(x, mem, norm_mix_g, w_in, pool_w, pool_scale, sgu_norm_g, w_spatial, b_spatial, w_out, norm_xattn_g, norm_mem_g, w_q, w_k, w_v, w_o, norm_ffn_g, w_gate, w_up, w_down, final_norm_g, loss_target, m_norm_mix_g, m_w_in, m_pool_w, m_pool_scale, m_sgu_norm_g, m_w_spatial, m_b_spatial, m_w_out, m_norm_xattn_g, m_norm_mem_g, m_w_q, m_w_k, m_w_v, m_w_o, m_norm_ffn_g, m_w_gate, m_w_up, m_w_down, m_final_norm_g, v_norm_mix_g, v_w_in, v_pool_w, v_pool_scale, v_sgu_norm_g, v_w_spatial, v_b_spatial, v_w_out, v_norm_xattn_g, v_norm_mem_g, v_w_q, v_w_k, v_w_v, v_w_o, v_norm_ffn_g, v_w_gate, v_w_up, v_w_down, v_final_norm_g):
    args = dict(locals())
    weights = {n: args[n] for n in WEIGHTS}
    moments_m = {n: args["m_" + n] for n in WEIGHTS}
    moments_v = {n: args["v_" + n] for n in WEIGHTS}
    low = MXU_DTYPE

    xs = x[0]
    mems = mem[0]
    target = loss_target[0]
    D = xs.shape[1]
    G, PG = pool_w.shape[1], pool_w.shape[3]

    pos = jnp.stack([2 * lax.axis_index("x") + lax.axis_index("y"), lax.axis_index("c")]).astype(jnp.int32)
    def cast(names, after=None):
        slabs = _per_run(lambda name, xs: _cast_into_slabs(name, xs, WIRE_DTYPE, pos, after), names,
                         [_as2d(weights[n]) for n in names])
        return dict(zip(names, slabs))

    def land(gi, flight, after, relayed=True):
        wait = _relayed_wait if relayed else _gather_wait
        landed = wait("gather_wait_%d" % gi, flight, after)
        return dict(zip(GATHER_ORDER[gi], _share_with_sibling("gather_share_%d" % gi, landed)))

    def rows(a):
        return a.reshape(-1, a.shape[-1])

    g_mix, g_xattn, g_mem, g_ffn = norm_mix_g, norm_xattn_g, norm_mem_g, norm_ffn_g
    g_final = final_norm_g.reshape(1, D)
    ws = w_spatial[0]
    b_full = jnp.broadcast_to(b_spatial[0][:, :, None], ws.shape[:2] + (sgu_norm_g.shape[1] // ws.shape[0],))

    first = cast(GATHER_ORDER[0])
    _, (sent_0,), _, started = _gather_step("gather_step_0", direct=[[first[n] for n in GATHER_ORDER[0]]])
    rest = cast([n for group in GATHER_ORDER[1:3] for n in group], after=started)
    _, (sent_1,), (hop_2,), started = _gather_step(
        "gather_step_1", direct=[[rest[n] for n in GATHER_ORDER[1]]], ring=[[rest[n] for n in GATHER_ORDER[2]]])
    h1 = _rmsnorm("norm_mix", xs, g_mix, low, after=started)
    mn = _rmsnorm("norm_mem", mems, g_mem, low)
    rest.update(cast(GATHER_ORDER[3], after=h1))
    rest.update(cast(GATHER_ORDER[4], after=rest[GATHER_ORDER[3][-1]]))
    slabs = [[rest[n] for n in group] for group in GATHER_ORDER[1:]]
    got = land(0, sent_0, rest["w_down"], relayed=False)
    wg_in = got["w_in"]
    pw = got["pool_w"].reshape(N_CHIPS, G, PG // N_CHIPS, PG).transpose(1, 0, 2, 3).reshape(G, PG, PG)
    proj = _matmul_nn_cols("proj_in", h1, wg_in, jnp.float32)
    z = _mixer_fwd(proj, pw, pool_scale, sgu_norm_g, ws, b_full)
    full = {"w_out": rows(land(1, sent_1, z, relayed=False)["w_out"])}
    (relay_2,), _, (hop_3,), started = _gather_step("gather_step_2", arrived=[hop_2], ring=slabs[2:3], after=full["w_out"])
    x1, h2 = _proj_residual_norm("mix_out", z, full["w_out"], xs, g_xattn, after=started)
    landed = _relayed_wait("gather_wait_2", relay_2, h2)
    full["w_q"] = rows(_share_with_sibling("gather_share_2_q", landed[:1])[0])
    sharing = _share_start("gather_share_2_start", landed[1:])
    q = _matmul_nn("xattn_q", h2, full["w_q"], low, after=sharing[-1])
    full.update({n: rows(a) for n, a in zip(GATHER_ORDER[2][1:], _share_wait("gather_share_2_wait", sharing, q))})
    k = _matmul_nn("xattn_k", mn, full["w_k"], low)
    vv = _matmul_nn("xattn_v", mn, full["w_v"], low)
    o = _attn_fwd(q, k, vv)
    (relay_3,), (sent_4,), _, started = _gather_step("gather_step_3", arrived=[hop_3], direct=slabs[3:4], after=o)
    x2, h3 = _proj_residual_norm("xattn_out", o, full["w_o"], x1, g_ffn, after=started)
    got = land(3, relay_3, h3)
    wg_gate, wg_up = got["w_gate"], got["w_up"]
    act, act_by_gate, act_by_up = _ffn_gate_up(h3, wg_gate, wg_up)
    full["w_down"] = rows(land(4, sent_4, act, relayed=False)["w_down"])
    x3 = _ffn_down(act, full["w_down"], x2)
    dx3, dx3_low, loss_part, d_final_g = _final_norm_loss(x3, target, g_final)

    dgate, dup = _ffn_down_bwd(dx3_low, full["w_down"], act_by_gate, act_by_up, N_CHIPS)
    (d_w_down,) = _matmul_tn("grad_w_down", act, [dx3_low], WIRE_DTYPE)
    (d_w_gate,) = _matmul_tn_cols("grad_w_gate", h3, [dgate], N_CHIPS, WIRE_DTYPE, tka=1024)
    (d_w_up,) = _matmul_tn_cols("grad_w_up", h3, [dup], N_CHIPS, WIRE_DTYPE, tka=1024)
    ffn_names = ("w_gate", "w_up", "w_down")
    ffn_swap = _swap_start("swap_start_ffn", _as_shards([d_w_gate, d_w_up, d_w_down]))
    dh3 = _matmul_nt_cols("ffn_dh", [dgate, dup], [wg_gate, wg_up], jnp.float32, tn=1024, after=ffn_swap[-1])
    ffn_flight = _reduce_scatter_start("ffn", ffn_names, *_swap_wait("swap_wait_ffn", ffn_swap, dh3), pos)
    dx2, dx2_low, d_ffn_g = _rmsnorm_bwd("norm_ffn_bwd", dh3, x2, g_ffn, dx3, low)

    do = _matmul_nt("xattn_do", [dx2_low], [full["w_o"]], low, after=ffn_flight[-1])
    (d_w_o,) = _matmul_tn("grad_w_o", o, [dx2_low], WIRE_DTYPE)
    dq, dk, dvv = _attn_bwd(q, k, vv, do)
    (d_w_q,) = _matmul_tn("grad_w_q", h2, [dq], WIRE_DTYPE)
    dx1, dx1_low, d_xattn_g = _matmul_nt_norm_bwd("xattn_dh_norm_bwd", dq, full["w_q"], x1, g_xattn, dx2, low, tm=512)
    d_w_k, d_w_v = _matmul_tn("grad_w_kv", mn, [dk, dvv], WIRE_DTYPE)
    dmn = _matmul_nt("xattn_dmem", [dk, dvv], [full["w_k"], full["w_v"]], jnp.float32)
    (d_mem_g,) = _rmsnorm_bwd("norm_mem_bwd", dmn, mems, g_mem, None, None)

    (d_w_out,) = _matmul_tn("grad_w_out", z, [dx1_low], WIRE_DTYPE)
    attn_names = ("w_q", "w_k", "w_v", "w_o", "w_out")
    attn_swap = _swap_start("swap_start_attn", _as_shards([d_w_q, d_w_k, d_w_v, d_w_o, d_w_out]))
    dz = _matmul_nt("mix_dz", [dx1_low], [full["w_out"]], jnp.float32, after=attn_swap[-1])
    attn_flight = _reduce_scatter_start("attn", attn_names, *_swap_wait("swap_wait_attn", attn_swap, dz), pos)
    dproj, d_pw, d_ps, d_gs, d_ws, d_b = _mixer_bwd(proj, dz, pw, pool_scale, sgu_norm_g, ws, b_full)
    (d_w_in,) = _matmul_tn_cols("grad_w_in", h1, [dproj], N_CHIPS, WIRE_DTYPE, tka=1024, after=attn_flight[-1])
    d_pool_w = d_pw.reshape(G, N_CHIPS, PG // N_CHIPS, PG).transpose(1, 0, 2, 3).reshape(N_CHIPS, PG, PG).astype(WIRE_DTYPE)
    mix_names = ("w_in", "pool_w")
    mix_flight = _reduce_begin("mix", mix_names, [d_w_in, d_pool_w], pos)
    grad_x, d_mix_g = _matmul_nt_norm_bwd("mix_dh_norm_bwd", dproj, wg_in, xs, g_mix, dx1, None, tm=512, after=mix_flight[-1])

    small_parts = {
        "norm_mix_g": d_mix_g, "pool_scale": d_ps, "sgu_norm_g": d_gs, "w_spatial": d_ws, "b_spatial": d_b,
        "norm_xattn_g": d_xattn_g, "norm_mem_g": d_mem_g, "norm_ffn_g": d_ffn_g, "final_norm_g": d_final_g,
    }
    row_counts = [small_parts[n].size // LANES for n in SMALL]
    loss_rows = jnp.pad(loss_part, ((0, 7), (0, LANES - 1)))
    packed = jnp.concatenate([_as_lanes(small_parts[n]) for n in SMALL] + [loss_rows], axis=0)
    device = _device_index(lax.axis_index("x"), lax.axis_index("y"), lax.axis_index("c"))
    slots = lax.dynamic_update_slice(jnp.zeros((N_DEV,) + packed.shape, jnp.float32), packed[None], (device, 0, 0))
    small_flight = _exchange_start("small_grads_start", slots)

    groups = (("ffn", ffn_names, ffn_flight), ("attn", attn_names, attn_flight), ("mix", mix_names, mix_flight))
    joins, after = {}, small_flight[-1]
    for tag, names, flight in groups:
        reduced = _reduce_end(tag, names, flight, after, pos)
        joins[tag] = _join_start("join_start_" + tag, [reduced[n] for n in names])
        after = joins[tag][-1]

    out_grad, out_delta, out_m, out_v = {}, {}, {}, {}
    for tag, names, _ in groups:
        grads = [g.reshape(-1, g.shape[-1]) for g in _join_wait("join_wait_" + tag, joins[tag], after)]
        updates = _per_run(_adamw, names, [_as2d(weights[n]) for n in names], grads,
                           [_as2d(moments_m[n]) for n in names], [_as2d(moments_v[n]) for n in names])
        for n, update in zip(names, updates):
            shape = weights[n].shape
            out_grad[n], out_delta[n], out_m[n], out_v[n] = (a.reshape(shape) for a in update)
        after = updates[-1][1]
    small_sum = _sum_slots("small_grads_sum", _exchange_wait("small_grads_wait", small_flight, after))
    small_out = _adamw_small(
        small_sum, [_as_lanes(weights[n]) for n in SMALL], [_as_lanes(moments_m[n]) for n in SMALL],
        [_as_lanes(moments_v[n]) for n in SMALL], row_counts)
    for t, n in enumerate(SMALL):
        shape = weights[n].shape
        out_grad[n], out_delta[n], out_m[n], out_v[n] = (a.reshape(shape) for a in small_out[4 * t:4 * t + 4])

    loss = small_sum[sum(row_counts), 0]
    return (loss, grad_x[None], *[out_grad[n] for n in WEIGHTS], *[out_delta[n] for n in WEIGHTS],
            *[out_m[n] for n in WEIGHTS], *[out_v[n] for n in WEIGHTS])
```
